```python
import math
import jax, jax.numpy as jnp
from jax import lax
import numpy as np


D_MODEL = 1024
BATCH = 8
SEQ = 4096
DEPTH = 2

PLE_DIM = 256
MLA_HEADS = 4
MLA_NOPE = 64
MLA_ROPE = 32
MLA_V = 64
MLA_Q_RANK = 192
MLA_KV_RANK = 128
MLA_OUT = MLA_HEADS * MLA_V
FOX_HEADS = 4
FOX_HEAD_DIM = 64
FOX_OUT = FOX_HEADS * FOX_HEAD_DIM
LRU_WIDTH = 512
LRU_BLOCKS = 8
LRU_BLOCK = LRU_WIDTH // LRU_BLOCKS
LRU_CONV = 4
LRU_C = 8.0
D_MIX = MLA_OUT + FOX_OUT + LRU_WIDTH
IN_SIZES = (MLA_Q_RANK, MLA_KV_RANK, MLA_ROPE, FOX_OUT, FOX_OUT, FOX_OUT, FOX_HEADS, LRU_WIDTH, LRU_WIDTH)
D_IN = MLA_Q_RANK + MLA_KV_RANK + MLA_ROPE + 3 * FOX_OUT + FOX_HEADS + 2 * LRU_WIDTH
D_FF = 2816
FFN_CONV = 3
ROPE_THETA = 10000.0
EPS = 1e-6
Q_BLOCK = 128

kernel_name = 'hybrid_mla_fox_rglru_convffn_ple'


def _offsets(sizes):
    out, acc = [], 0
    for s in sizes[:-1]:
        acc += s
        out.append(acc)
    return out


def rms_norm(x, g):
    xf = x.astype(jnp.float32)
    y = xf * lax.rsqrt(jnp.mean(xf * xf, axis=-1, keepdims=True) + EPS)
    return (y * g.astype(jnp.float32)).astype(x.dtype)


def rope(x, positions):
    half = x.shape[-1] // 2
    freqs = ROPE_THETA ** (-jnp.arange(half, dtype=jnp.float32) / half)
    ang = positions.astype(jnp.float32)[..., None] * freqs
    ang = ang.reshape(ang.shape[:2] + (1,) * (x.ndim - 3) + (half,))
    cos, sin = jnp.cos(ang), jnp.sin(ang)
    xf = x.astype(jnp.float32)
    x1, x2 = xf[..., :half], xf[..., half:]
    return jnp.concatenate([x1 * cos - x2 * sin, x2 * cos + x1 * sin], axis=-1).astype(x.dtype)


def causal_dwconv(x, w, b):
    K = w.shape[0]
    S = x.shape[1]
    xp = jnp.pad(x, ((0, 0), (K - 1, 0), (0, 0)))
    out = b + xp[:, 0:S] * w[0]
    for k in range(1, K):
        out = out + xp[:, k:k + S] * w[k]
    return out


def causal_block_attention(q, k, v, scale, decay=None):
    B, S, H, dk = q.shape
    dv = v.shape[-1]
    nb = S // Q_BLOCK
    qb = q.reshape(B, nb, Q_BLOCK, H, dk).transpose(1, 0, 3, 2, 4)
    kh = k.transpose(0, 2, 1, 3)
    vh = v.transpose(0, 2, 1, 3)
    k_pos = jnp.arange(S)
    blk_idx = jnp.arange(nb)
    ck = None if decay is None else decay.transpose(0, 2, 1)

    def block(q_blk, c_blk, idx):
        s = jnp.einsum('bhqd,bhkd->bhqk', q_blk, kh, preferred_element_type=jnp.float32) * scale
        if c_blk is not None:
            s = s + c_blk[..., :, None] - ck[..., None, :]
        q_pos = idx * Q_BLOCK + jnp.arange(Q_BLOCK)
        s = jnp.where(k_pos[None, :] <= q_pos[:, None], s, -jnp.inf)
        pr = jax.nn.softmax(s, axis=-1).astype(vh.dtype)
        return jnp.einsum('bhqk,bhkd->bhqd', pr, vh)

    if decay is None:
        out = lax.map(lambda a: block(a[0], None, a[1]), (qb, blk_idx))
    else:
        cb = ck.reshape(B, H, nb, Q_BLOCK).transpose(2, 0, 1, 3)
        out = lax.map(lambda a: block(a[0], a[1], a[2]), (qb, cb, blk_idx))
    return out.transpose(1, 0, 3, 2, 4).reshape(B, S, H, dv)


def _linear_combine(left, right):
    a_l, b_l = left
    a_r, b_r = right
    return a_l * a_r, a_r * b_l + b_r


def hybrid_mixer(xn, positions, w_in, g_qc, w_uq, g_kvc, w_ukv, b_f, lru_conv_w, lru_conv_b,
                 w_r, b_r, w_i, b_i, lru_lambda, g_out, w_o):
    B, S, _ = xn.shape
    z = xn @ w_in
    q_c, kv_c, k_r, fq, fk, fv, f_logit, lx, lg = jnp.split(z, _offsets(IN_SIZES), axis=-1)

    q = (rms_norm(q_c, g_qc) @ w_uq).reshape(B, S, MLA_HEADS, MLA_NOPE + MLA_ROPE)
    q = jnp.concatenate([q[..., :MLA_NOPE], rope(q[..., MLA_NOPE:], positions)], axis=-1)
    kv = (rms_norm(kv_c, g_kvc) @ w_ukv).reshape(B, S, MLA_HEADS, MLA_NOPE + MLA_V)
    k_nope, v_mla = kv[..., :MLA_NOPE], kv[..., MLA_NOPE:]
    k_rope = rope(k_r, positions)
    k = jnp.concatenate([k_nope, jnp.broadcast_to(k_rope[:, :, None, :], (B, S, MLA_HEADS, MLA_ROPE))], axis=-1)
    o_mla = causal_block_attention(q, k, v_mla, (MLA_NOPE + MLA_ROPE) ** -0.5).reshape(B, S, MLA_OUT)

    log_f = jax.nn.log_sigmoid(f_logit.astype(jnp.float32) + b_f.astype(jnp.float32))
    c = jnp.cumsum(log_f, axis=1)
    o_fox = causal_block_attention(fq.reshape(B, S, FOX_HEADS, FOX_HEAD_DIM),
                                   fk.reshape(B, S, FOX_HEADS, FOX_HEAD_DIM),
                                   fv.reshape(B, S, FOX_HEADS, FOX_HEAD_DIM),
                                   FOX_HEAD_DIM ** -0.5, decay=c).reshape(B, S, FOX_OUT)

    xc = causal_dwconv(lx, lru_conv_w, lru_conv_b)
    xblk = xc.reshape(B, S, LRU_BLOCKS, LRU_BLOCK)
    r = jax.nn.sigmoid(jnp.einsum('bsnc,ncd->bsnd', xblk, w_r).reshape(B, S, LRU_WIDTH) + b_r)
    i = jax.nn.sigmoid(jnp.einsum('bsnc,ncd->bsnd', xblk, w_i).reshape(B, S, LRU_WIDTH) + b_i)
    log_a = -LRU_C * r.astype(jnp.float32) * jax.nn.softplus(-lru_lambda.astype(jnp.float32))
    a_t = jnp.exp(log_a)
    bx = jnp.sqrt(-jnp.expm1(2.0 * log_a)) * (i * xc).astype(jnp.float32)
    _, h = lax.associative_scan(_linear_combine, (a_t, bx), axis=1)
    o_lru = h.astype(xn.dtype) * jax.nn.gelu(lg)

    o = jnp.concatenate([
        rms_norm(o_mla, g_out[:MLA_OUT]),
        rms_norm(o_fox, g_out[MLA_OUT:MLA_OUT + FOX_OUT]),
        rms_norm(o_lru, g_out[MLA_OUT + FOX_OUT:]),
    ], axis=-1)
    return o @ w_o


def conv_ffn(xn, w_up, ffn_conv_w, ffn_conv_b, w_down):
    u = causal_dwconv(xn @ w_up, ffn_conv_w, ffn_conv_b)
    g, v = jnp.split(u, 2, axis=-1)
    return (jax.nn.silu(g) * v) @ w_down


def per_layer_embedding(h, p_i, g_ple, w_ple_gate, w_ple_proj):
    return jax.nn.sigmoid(rms_norm(h, g_ple) @ w_ple_gate) * (p_i @ w_ple_proj)


def _fwd_setup_inputs(seed: int = 0) -> dict:
    key = jax.random.key(seed)
    ks = iter(jax.random.split(key, 40))

    def nrm(shape, scale):
        return jax.random.normal(next(ks), shape, jnp.float32) * scale

    def gain(shape):
        return 1.0 + nrm(shape, 0.02)

    x = nrm((BATCH, SEQ, D_MODEL), 1.0)
    p = nrm((DEPTH, BATCH, SEQ, PLE_DIM), 1.0)
    offset = jax.random.randint(next(ks), (BATCH, 1), 0, 1024, dtype=jnp.int32)
    positions = (offset + jnp.arange(SEQ, dtype=jnp.int32)[None, :]).astype(jnp.int32)

    u = jax.random.uniform(next(ks), (DEPTH, LRU_WIDTH), jnp.float32, 0.9, 0.999)
    s = u ** (1.0 / LRU_C)
    lru_lambda = jnp.log(s) - jnp.log1p(-s)

    return {
        'x': x,
        'p': p,
        'positions': positions,
        'g_mix': gain((DEPTH, D_MODEL)),
        'w_in': nrm((DEPTH, D_MODEL, D_IN), D_MODEL ** -0.5),
        'g_qc': gain((DEPTH, MLA_Q_RANK)),
        'w_uq': nrm((DEPTH, MLA_Q_RANK, MLA_HEADS * (MLA_NOPE + MLA_ROPE)), MLA_Q_RANK ** -0.5),
        'g_kvc': gain((DEPTH, MLA_KV_RANK)),
        'w_ukv': nrm((DEPTH, MLA_KV_RANK, MLA_HEADS * (MLA_NOPE + MLA_V)), MLA_KV_RANK ** -0.5),
        'b_f': jax.random.uniform(next(ks), (DEPTH, FOX_HEADS), jnp.float32, 1.0, 4.0),
        'lru_conv_w': nrm((DEPTH, LRU_CONV, LRU_WIDTH), LRU_CONV ** -0.5),
        'lru_conv_b': nrm((DEPTH, LRU_WIDTH), 0.02),
        'w_r': nrm((DEPTH, LRU_BLOCKS, LRU_BLOCK, LRU_BLOCK), LRU_BLOCK ** -0.5),
        'b_r': nrm((DEPTH, LRU_WIDTH), 0.02),
        'w_i': nrm((DEPTH, LRU_BLOCKS, LRU_BLOCK, LRU_BLOCK), LRU_BLOCK ** -0.5),
        'b_i': nrm((DEPTH, LRU_WIDTH), 0.02),
        'lru_lambda': lru_lambda,
        'g_out': gain((DEPTH, D_MIX)),
        'w_o': nrm((DEPTH, D_MIX, D_MODEL), D_MIX ** -0.5),
        'g_ffn': gain((DEPTH, D_MODEL)),
        'w_up': nrm((DEPTH, D_MODEL, 2 * D_FF), D_MODEL ** -0.5),
        'ffn_conv_w': nrm((DEPTH, FFN_CONV, 2 * D_FF), FFN_CONV ** -0.5),
        'ffn_conv_b': nrm((DEPTH, 2 * D_FF), 0.02),
        'w_down': nrm((DEPTH, D_FF, D_MODEL), D_FF ** -0.5),
        'g_ple': gain((DEPTH, D_MODEL)),
        'w_ple_gate': nrm((DEPTH, D_MODEL, D_MODEL), D_MODEL ** -0.5),
        'w_ple_proj': nrm((DEPTH, PLE_DIM, D_MODEL), PLE_DIM ** -0.5),
        'g_final': gain((D_MODEL,)),
    }


def _fwd_reference(x, p, positions, g_mix, w_in, g_qc, w_uq, g_kvc, w_ukv, b_f, lru_conv_w, lru_conv_b,
              w_r, b_r, w_i, b_i, lru_lambda, g_out, w_o, g_ffn, w_up, ffn_conv_w, ffn_conv_b,
              w_down, g_ple, w_ple_gate, w_ple_proj, g_final):
    h = x
    for l in range(DEPTH):
        h = h + hybrid_mixer(rms_norm(h, g_mix[l]), positions, w_in[l], g_qc[l], w_uq[l], g_kvc[l],
                             w_ukv[l], b_f[l], lru_conv_w[l], lru_conv_b[l], w_r[l], b_r[l], w_i[l],
                             b_i[l], lru_lambda[l], g_out[l], w_o[l])
        h = h + conv_ffn(rms_norm(h, g_ffn[l]), w_up[l], ffn_conv_w[l], ffn_conv_b[l], w_down[l])
        h = h + per_layer_embedding(h, p[l], g_ple[l], w_ple_gate[l], w_ple_proj[l])
    return rms_norm(h, g_final)


import jax as _jax
import jax.numpy as _jnp

TWIN_FORMAT = 'train_step'
FWD_PARAMS = ['x', 'p', 'positions', 'g_mix', 'w_in', 'g_qc', 'w_uq', 'g_kvc', 'w_ukv', 'b_f', 'lru_conv_w', 'lru_conv_b', 'w_r', 'b_r', 'w_i', 'b_i', 'lru_lambda', 'g_out', 'w_o', 'g_ffn', 'w_up', 'ffn_conv_w', 'ffn_conv_b', 'w_down', 'g_ple', 'w_ple_gate', 'w_ple_proj', 'g_final']
TWIN_WEIGHTS = ['g_mix', 'w_in', 'g_qc', 'w_uq', 'g_kvc', 'w_ukv', 'b_f', 'lru_conv_w', 'lru_conv_b', 'w_r', 'b_r', 'w_i', 'b_i', 'lru_lambda', 'g_out', 'w_o', 'g_ffn', 'w_up', 'ffn_conv_w', 'ffn_conv_b', 'w_down', 'g_ple', 'w_ple_gate', 'w_ple_proj', 'g_final']
TWIN_DIFF_INPUT = 'x'
TWIN_INPUTS = ['x', 'p', 'positions', 'g_mix', 'w_in', 'g_qc', 'w_uq', 'g_kvc', 'w_ukv', 'b_f', 'lru_conv_w', 'lru_conv_b', 'w_r', 'b_r', 'w_i', 'b_i', 'lru_lambda', 'g_out', 'w_o', 'g_ffn', 'w_up', 'ffn_conv_w', 'ffn_conv_b', 'w_down', 'g_ple', 'w_ple_gate', 'w_ple_proj', 'g_final', 'loss_target', 'm_g_mix', 'm_w_in', 'm_g_qc', 'm_w_uq', 'm_g_kvc', 'm_w_ukv', 'm_b_f', 'm_lru_conv_w', 'm_lru_conv_b', 'm_w_r', 'm_b_r', 'm_w_i', 'm_b_i', 'm_lru_lambda', 'm_g_out', 'm_w_o', 'm_g_ffn', 'm_w_up', 'm_ffn_conv_w', 'm_ffn_conv_b', 'm_w_down', 'm_g_ple', 'm_w_ple_gate', 'm_w_ple_proj', 'm_g_final', 'v_g_mix', 'v_w_in', 'v_g_qc', 'v_w_uq', 'v_g_kvc', 'v_w_ukv', 'v_b_f', 'v_lru_conv_w', 'v_lru_conv_b', 'v_w_r', 'v_b_r', 'v_w_i', 'v_b_i', 'v_lru_lambda', 'v_g_out', 'v_w_o', 'v_g_ffn', 'v_w_up', 'v_ffn_conv_w', 'v_ffn_conv_b', 'v_w_down', 'v_g_ple', 'v_w_ple_gate', 'v_w_ple_proj', 'v_g_final']
TWIN_OUTPUTS = ['loss', 'grad_x', 'grad_g_mix', 'grad_w_in', 'grad_g_qc', 'grad_w_uq', 'grad_g_kvc', 'grad_w_ukv', 'grad_b_f', 'grad_lru_conv_w', 'grad_lru_conv_b', 'grad_w_r', 'grad_b_r', 'grad_w_i', 'grad_b_i', 'grad_lru_lambda', 'grad_g_out', 'grad_w_o', 'grad_g_ffn', 'grad_w_up', 'grad_ffn_conv_w', 'grad_ffn_conv_b', 'grad_w_down', 'grad_g_ple', 'grad_w_ple_gate', 'grad_w_ple_proj', 'grad_g_final', 'delta_g_mix', 'delta_w_in', 'delta_g_qc', 'delta_w_uq', 'delta_g_kvc', 'delta_w_ukv', 'delta_b_f', 'delta_lru_conv_w', 'delta_lru_conv_b', 'delta_w_r', 'delta_b_r', 'delta_w_i', 'delta_b_i', 'delta_lru_lambda', 'delta_g_out', 'delta_w_o', 'delta_g_ffn', 'delta_w_up', 'delta_ffn_conv_w', 'delta_ffn_conv_b', 'delta_w_down', 'delta_g_ple', 'delta_w_ple_gate', 'delta_w_ple_proj', 'delta_g_final', 'new_m_g_mix', 'new_m_w_in', 'new_m_g_qc', 'new_m_w_uq', 'new_m_g_kvc', 'new_m_w_ukv', 'new_m_b_f', 'new_m_lru_conv_w', 'new_m_lru_conv_b', 'new_m_w_r', 'new_m_b_r', 'new_m_w_i', 'new_m_b_i', 'new_m_lru_lambda', 'new_m_g_out', 'new_m_w_o', 'new_m_g_ffn', 'new_m_w_up', 'new_m_ffn_conv_w', 'new_m_ffn_conv_b', 'new_m_w_down', 'new_m_g_ple', 'new_m_w_ple_gate', 'new_m_w_ple_proj', 'new_m_g_final', 'new_v_g_mix', 'new_v_w_in', 'new_v_g_qc', 'new_v_w_uq', 'new_v_g_kvc', 'new_v_w_ukv', 'new_v_b_f', 'new_v_lru_conv_w', 'new_v_lru_conv_b', 'new_v_w_r', 'new_v_b_r', 'new_v_w_i', 'new_v_b_i', 'new_v_lru_lambda', 'new_v_g_out', 'new_v_w_o', 'new_v_g_ffn', 'new_v_w_up', 'new_v_ffn_conv_w', 'new_v_ffn_conv_b', 'new_v_w_down', 'new_v_g_ple', 'new_v_w_ple_gate', 'new_v_w_ple_proj', 'new_v_g_final']
TWIN_LEAF_KINDS = {'loss': 'loss', 'grad_x': 'grad_x', 'grad_g_mix': 'grad_w', 'grad_w_in': 'grad_w', 'grad_g_qc': 'grad_w', 'grad_w_uq': 'grad_w', 'grad_g_kvc': 'grad_w', 'grad_w_ukv': 'grad_w', 'grad_b_f': 'grad_w', 'grad_lru_conv_w': 'grad_w', 'grad_lru_conv_b': 'grad_w', 'grad_w_r': 'grad_w', 'grad_b_r': 'grad_w', 'grad_w_i': 'grad_w', 'grad_b_i': 'grad_w', 'grad_lru_lambda': 'grad_w', 'grad_g_out': 'grad_w', 'grad_w_o': 'grad_w', 'grad_g_ffn': 'grad_w', 'grad_w_up': 'grad_w', 'grad_ffn_conv_w': 'grad_w', 'grad_ffn_conv_b': 'grad_w', 'grad_w_down': 'grad_w', 'grad_g_ple': 'grad_w', 'grad_w_ple_gate': 'grad_w', 'grad_w_ple_proj': 'grad_w', 'grad_g_final': 'grad_w', 'delta_g_mix': 'delta_w', 'delta_w_in': 'delta_w', 'delta_g_qc': 'delta_w', 'delta_w_uq': 'delta_w', 'delta_g_kvc': 'delta_w', 'delta_w_ukv': 'delta_w', 'delta_b_f': 'delta_w', 'delta_lru_conv_w': 'delta_w', 'delta_lru_conv_b': 'delta_w', 'delta_w_r': 'delta_w', 'delta_b_r': 'delta_w', 'delta_w_i': 'delta_w', 'delta_b_i': 'delta_w', 'delta_lru_lambda': 'delta_w', 'delta_g_out': 'delta_w', 'delta_w_o': 'delta_w', 'delta_g_ffn': 'delta_w', 'delta_w_up': 'delta_w', 'delta_ffn_conv_w': 'delta_w', 'delta_ffn_conv_b': 'delta_w', 'delta_w_down': 'delta_w', 'delta_g_ple': 'delta_w', 'delta_w_ple_gate': 'delta_w', 'delta_w_ple_proj': 'delta_w', 'delta_g_final': 'delta_w', 'new_m_g_mix': 'new_m', 'new_m_w_in': 'new_m', 'new_m_g_qc': 'new_m', 'new_m_w_uq': 'new_m', 'new_m_g_kvc': 'new_m', 'new_m_w_ukv': 'new_m', 'new_m_b_f': 'new_m', 'new_m_lru_conv_w': 'new_m', 'new_m_lru_conv_b': 'new_m', 'new_m_w_r': 'new_m', 'new_m_b_r': 'new_m', 'new_m_w_i': 'new_m', 'new_m_b_i': 'new_m', 'new_m_lru_lambda': 'new_m', 'new_m_g_out': 'new_m', 'new_m_w_o': 'new_m', 'new_m_g_ffn': 'new_m', 'new_m_w_up': 'new_m', 'new_m_ffn_conv_w': 'new_m', 'new_m_ffn_conv_b': 'new_m', 'new_m_w_down': 'new_m', 'new_m_g_ple': 'new_m', 'new_m_w_ple_gate': 'new_m', 'new_m_w_ple_proj': 'new_m', 'new_m_g_final': 'new_m', 'new_v_g_mix': 'new_v', 'new_v_w_in': 'new_v', 'new_v_g_qc': 'new_v', 'new_v_w_uq': 'new_v', 'new_v_g_kvc': 'new_v', 'new_v_w_ukv': 'new_v', 'new_v_b_f': 'new_v', 'new_v_lru_conv_w': 'new_v', 'new_v_lru_conv_b': 'new_v', 'new_v_w_r': 'new_v', 'new_v_b_r': 'new_v', 'new_v_w_i': 'new_v', 'new_v_b_i': 'new_v', 'new_v_lru_lambda': 'new_v', 'new_v_g_out': 'new_v', 'new_v_w_o': 'new_v', 'new_v_g_ffn': 'new_v', 'new_v_w_up': 'new_v', 'new_v_ffn_conv_w': 'new_v', 'new_v_ffn_conv_b': 'new_v', 'new_v_w_down': 'new_v', 'new_v_g_ple': 'new_v', 'new_v_w_ple_gate': 'new_v', 'new_v_w_ple_proj': 'new_v', 'new_v_g_final': 'new_v'}


def _forward(args):
    return _fwd_reference(*[args[k] for k in FWD_PARAMS])


def _output_shape():
    out = _jax.eval_shape(lambda: _forward(_fwd_setup_inputs(0)))
    return out.shape, out.dtype

N_MICROBATCH = 1
ADAM_LR = 0.001
ADAM_B1 = 0.9
ADAM_B2 = 0.999
ADAM_EPS = 1e-08
ADAM_WD = 0.01
ADAM_STEP = 10
PER_EXAMPLE_BATCH_AXIS = {'x': 0, 'p': 1, 'positions': 0, 'loss_target': 0}
SHARED_INPUTS = []
_WEIGHT_DTYPES = {'g_mix': _jnp.float32, 'w_in': _jnp.float32, 'g_qc': _jnp.float32, 'w_uq': _jnp.float32, 'g_kvc': _jnp.float32, 'w_ukv': _jnp.float32, 'b_f': _jnp.float32, 'lru_conv_w': _jnp.float32, 'lru_conv_b': _jnp.float32, 'w_r': _jnp.float32, 'b_r': _jnp.float32, 'w_i': _jnp.float32, 'b_i': _jnp.float32, 'lru_lambda': _jnp.float32, 'g_out': _jnp.float32, 'w_o': _jnp.float32, 'g_ffn': _jnp.float32, 'w_up': _jnp.float32, 'ffn_conv_w': _jnp.float32, 'ffn_conv_b': _jnp.float32, 'w_down': _jnp.float32, 'g_ple': _jnp.float32, 'w_ple_gate': _jnp.float32, 'w_ple_proj': _jnp.float32, 'g_final': _jnp.float32}
MOMENT_SCALE = {'g_mix': 1.808155e-01, 'w_in': 1.222876e-01, 'g_qc': 1.164981e-01, 'w_uq': 8.142340e-02, 'g_kvc': 2.612836e-01, 'w_ukv': 1.154092e-01, 'b_f': 4.992367e-01, 'lru_conv_w': 1.308288e-01, 'lru_conv_b': 1.701544e+00, 'w_r': 4.523587e-02, 'b_r': 3.331438e-02, 'w_i': 8.406585e-02, 'b_i': 4.762368e-02, 'lru_lambda': 5.956030e-02, 'g_out': 1.287593e-01, 'w_o': 1.305647e-01, 'g_ffn': 9.224497e-02, 'w_up': 3.887415e-02, 'ffn_conv_w': 3.903763e-02, 'ffn_conv_b': 4.497640e-02, 'w_down': 6.376144e-02, 'g_ple': 2.111222e-02, 'w_ple_gate': 2.141341e-02, 'w_ple_proj': 5.471100e-02, 'g_final': 3.199418e+01}


def _to_microbatches(a, axis):
    t = _jnp.moveaxis(a, axis, 0)
    t = t.reshape((N_MICROBATCH, t.shape[0] // N_MICROBATCH) + t.shape[1:])
    return _jnp.moveaxis(t, 1, axis + 1)


def setup_inputs(seed: int = 0) -> dict:
    inp = _fwd_setup_inputs(seed)
    key = _jax.random.fold_in(_jax.random.key(seed), 7919)
    shape, _ = _output_shape()
    out = dict(inp)
    out["loss_target"] = _jax.random.normal(_jax.random.fold_in(key, 0), shape, _jnp.float32)
    for i, name in enumerate(TWIN_WEIGHTS):
        w = inp[name].astype(_jnp.float32)
        if MOMENT_SCALE is None:
            s = _jnp.sqrt(_jnp.mean(_jnp.square(w)) + 1e-30)
        else:
            s = MOMENT_SCALE[name]
        km, kv = _jax.random.split(_jax.random.fold_in(key, i + 1))
        out[name] = w
        out["m_" + name] = s * _jax.random.normal(km, w.shape, _jnp.float32)
        out["v_" + name] = (s * s) * _jax.random.uniform(kv, w.shape, _jnp.float32, 0.5, 1.5)
    if N_MICROBATCH > 1:
        for name, axis in PER_EXAMPLE_BATCH_AXIS.items():
            out[name] = _to_microbatches(out[name], axis)
    return {'x': out['x'], 'p': out['p'], 'positions': out['positions'], 'g_mix': out['g_mix'], 'w_in': out['w_in'], 'g_qc': out['g_qc'], 'w_uq': out['w_uq'], 'g_kvc': out['g_kvc'], 'w_ukv': out['w_ukv'], 'b_f': out['b_f'], 'lru_conv_w': out['lru_conv_w'], 'lru_conv_b': out['lru_conv_b'], 'w_r': out['w_r'], 'b_r': out['b_r'], 'w_i': out['w_i'], 'b_i': out['b_i'], 'lru_lambda': out['lru_lambda'], 'g_out': out['g_out'], 'w_o': out['w_o'], 'g_ffn': out['g_ffn'], 'w_up': out['w_up'], 'ffn_conv_w': out['ffn_conv_w'], 'ffn_conv_b': out['ffn_conv_b'], 'w_down': out['w_down'], 'g_ple': out['g_ple'], 'w_ple_gate': out['w_ple_gate'], 'w_ple_proj': out['w_ple_proj'], 'g_final': out['g_final'], 'loss_target': out['loss_target'], 'm_g_mix': out['m_g_mix'], 'm_w_in': out['m_w_in'], 'm_g_qc': out['m_g_qc'], 'm_w_uq': out['m_w_uq'], 'm_g_kvc': out['m_g_kvc'], 'm_w_ukv': out['m_w_ukv'], 'm_b_f': out['m_b_f'], 'm_lru_conv_w': out['m_lru_conv_w'], 'm_lru_conv_b': out['m_lru_conv_b'], 'm_w_r': out['m_w_r'], 'm_b_r': out['m_b_r'], 'm_w_i': out['m_w_i'], 'm_b_i': out['m_b_i'], 'm_lru_lambda': out['m_lru_lambda'], 'm_g_out': out['m_g_out'], 'm_w_o': out['m_w_o'], 'm_g_ffn': out['m_g_ffn'], 'm_w_up': out['m_w_up'], 'm_ffn_conv_w': out['m_ffn_conv_w'], 'm_ffn_conv_b': out['m_ffn_conv_b'], 'm_w_down': out['m_w_down'], 'm_g_ple': out['m_g_ple'], 'm_w_ple_gate': out['m_w_ple_gate'], 'm_w_ple_proj': out['m_w_ple_proj'], 'm_g_final': out['m_g_final'], 'v_g_mix': out['v_g_mix'], 'v_w_in': out['v_w_in'], 'v_g_qc': out['v_g_qc'], 'v_w_uq': out['v_w_uq'], 'v_g_kvc': out['v_g_kvc'], 'v_w_ukv': out['v_w_ukv'], 'v_b_f': out['v_b_f'], 'v_lru_conv_w': out['v_lru_conv_w'], 'v_lru_conv_b': out['v_lru_conv_b'], 'v_w_r': out['v_w_r'], 'v_b_r': out['v_b_r'], 'v_w_i': out['v_w_i'], 'v_b_i': out['v_b_i'], 'v_lru_lambda': out['v_lru_lambda'], 'v_g_out': out['v_g_out'], 'v_w_o': out['v_w_o'], 'v_g_ffn': out['v_g_ffn'], 'v_w_up': out['v_w_up'], 'v_ffn_conv_w': out['v_ffn_conv_w'], 'v_ffn_conv_b': out['v_ffn_conv_b'], 'v_w_down': out['v_w_down'], 'v_g_ple': out['v_g_ple'], 'v_w_ple_gate': out['v_w_ple_gate'], 'v_w_ple_proj': out['v_w_ple_proj'], 'v_g_final': out['v_g_final']}


def _loss(weights, diff, rest, loss_target):
    with _jax.named_scope("forward"):
        args = {**rest, TWIN_DIFF_INPUT: diff, **{k: w.astype(_WEIGHT_DTYPES[k]) for k, w in weights.items()}}
        y = _forward(args)
    with _jax.named_scope("loss_head"):
        err = _jnp.square(y.astype(_jnp.float32) - loss_target)
        return 0.5 * _jnp.sum(_jnp.mean(err, axis=-1)) if err.ndim else 0.5 * err


def _adamw(w, g, m, v):
    m = ADAM_B1 * m + (1.0 - ADAM_B1) * g
    v = ADAM_B2 * v + (1.0 - ADAM_B2) * _jnp.square(g)
    m_hat = m / (1.0 - ADAM_B1 ** ADAM_STEP)
    v_hat = v / (1.0 - ADAM_B2 ** ADAM_STEP)
    delta = -ADAM_LR * (m_hat / (_jnp.sqrt(v_hat) + ADAM_EPS) + ADAM_WD * w)
    return delta, m, v


def reference(x, p, positions, g_mix, w_in, g_qc, w_uq, g_kvc, w_ukv, b_f, lru_conv_w, lru_conv_b, w_r, b_r, w_i, b_i, lru_lambda, g_out, w_o, g_ffn, w_up, ffn_conv_w, ffn_conv_b, w_down, g_ple, w_ple_gate, w_ple_proj, g_final, loss_target, m_g_mix, m_w_in, m_g_qc, m_w_uq, m_g_kvc, m_w_ukv, m_b_f, m_lru_conv_w, m_lru_conv_b, m_w_r, m_b_r, m_w_i, m_b_i, m_lru_lambda, m_g_out, m_w_o, m_g_ffn, m_w_up, m_ffn_conv_w, m_ffn_conv_b, m_w_down, m_g_ple, m_w_ple_gate, m_w_ple_proj, m_g_final, v_g_mix, v_w_in, v_g_qc, v_w_uq, v_g_kvc, v_w_ukv, v_b_f, v_lru_conv_w, v_lru_conv_b, v_w_r, v_b_r, v_w_i, v_b_i, v_lru_lambda, v_g_out, v_w_o, v_g_ffn, v_w_up, v_ffn_conv_w, v_ffn_conv_b, v_w_down, v_g_ple, v_w_ple_gate, v_w_ple_proj, v_g_final):
    given = dict(x=x, p=p, positions=positions, g_mix=g_mix, w_in=w_in, g_qc=g_qc, w_uq=w_uq, g_kvc=g_kvc, w_ukv=w_ukv, b_f=b_f, lru_conv_w=lru_conv_w, lru_conv_b=lru_conv_b, w_r=w_r, b_r=b_r, w_i=w_i, b_i=b_i, lru_lambda=lru_lambda, g_out=g_out, w_o=w_o, g_ffn=g_ffn, w_up=w_up, ffn_conv_w=ffn_conv_w, ffn_conv_b=ffn_conv_b, w_down=w_down, g_ple=g_ple, w_ple_gate=w_ple_gate, w_ple_proj=w_ple_proj, g_final=g_final, loss_target=loss_target, m_g_mix=m_g_mix, m_w_in=m_w_in, m_g_qc=m_g_qc, m_w_uq=m_w_uq, m_g_kvc=m_g_kvc, m_w_ukv=m_w_ukv, m_b_f=m_b_f, m_lru_conv_w=m_lru_conv_w, m_lru_conv_b=m_lru_conv_b, m_w_r=m_w_r, m_b_r=m_b_r, m_w_i=m_w_i, m_b_i=m_b_i, m_lru_lambda=m_lru_lambda, m_g_out=m_g_out, m_w_o=m_w_o, m_g_ffn=m_g_ffn, m_w_up=m_w_up, m_ffn_conv_w=m_ffn_conv_w, m_ffn_conv_b=m_ffn_conv_b, m_w_down=m_w_down, m_g_ple=m_g_ple, m_w_ple_gate=m_w_ple_gate, m_w_ple_proj=m_w_ple_proj, m_g_final=m_g_final, v_g_mix=v_g_mix, v_w_in=v_w_in, v_g_qc=v_g_qc, v_w_uq=v_w_uq, v_g_kvc=v_g_kvc, v_w_ukv=v_w_ukv, v_b_f=v_b_f, v_lru_conv_w=v_lru_conv_w, v_lru_conv_b=v_lru_conv_b, v_w_r=v_w_r, v_b_r=v_b_r, v_w_i=v_w_i, v_b_i=v_b_i, v_lru_lambda=v_lru_lambda, v_g_out=v_g_out, v_w_o=v_w_o, v_g_ffn=v_g_ffn, v_w_up=v_w_up, v_ffn_conv_w=v_ffn_conv_w, v_ffn_conv_b=v_ffn_conv_b, v_w_down=v_w_down, v_g_ple=v_g_ple, v_w_ple_gate=v_w_ple_gate, v_w_ple_proj=v_w_ple_proj, v_g_final=v_g_final)
    weights = {n: given[n] for n in TWIN_WEIGHTS}
    shared = {n: given[n] for n in SHARED_INPUTS}
    per_example = {n: given[n] for n in ['x', 'p', 'positions']}
    grad_fn = _jax.value_and_grad(_loss, argnums=(0, 1))

    def one_microbatch(ex, loss_target):
        ex = dict(ex)
        diff = ex.pop(TWIN_DIFF_INPUT)
        return grad_fn(weights, diff, {**shared, **ex}, loss_target)

    if N_MICROBATCH == 1:
        loss, (grad_w, grad_x) = one_microbatch(per_example, given["loss_target"])
    else:
        def body(carry, xs):
            loss_sum, grad_sum = carry
            l_k, (gw_k, gx_k) = one_microbatch(xs[0], xs[1])
            with _jax.named_scope("update"):
                return (loss_sum + l_k, _jax.tree.map(_jnp.add, grad_sum, gw_k)), gx_k

        init = (_jnp.zeros((), _jnp.float32), _jax.tree.map(_jnp.zeros_like, weights))
        (loss, grad_w), grad_x = _jax.lax.scan(body, init, (per_example, given["loss_target"]))
    with _jax.named_scope("update"):
        delta_w, new_m, new_v = {}, {}, {}
        for n in TWIN_WEIGHTS:
            delta_w[n], new_m[n], new_v[n] = _adamw(weights[n], grad_w[n], given["m_" + n], given["v_" + n])
    return (loss, grad_x, *[grad_w[n] for n in TWIN_WEIGHTS], *[delta_w[n] for n in TWIN_WEIGHTS],
            *[new_m[n] for n in TWIN_WEIGHTS], *[new_v[n] for n in TWIN_WEIGHTS])
```

```python
import functools

import numpy as np
import jax
import jax.numpy as jnp
from jax import lax
from jax.experimental import pallas as pl
from jax.experimental.pallas import tpu as pltpu

f32, bf16 = jnp.float32, jnp.bfloat16

D_MODEL = 1024
PLE_DIM = 256
MLA_HEADS, MLA_NOPE, MLA_ROPE, MLA_V = 4, 64, 32, 64
MLA_Q_RANK, MLA_KV_RANK = 192, 128
FOX_HEADS, FOX_HEAD_DIM = 4, 64
LRU_WIDTH, LRU_BLOCKS, LRU_BLOCK, LRU_CONV, LRU_C = 512, 8, 64, 4, 8.0
D_FF, FFN_CONV = 2816, 3
ROPE_THETA = 10000.0
EPS = 1e-6
DEPTH = 2
ADAM_LR, ADAM_B1, ADAM_B2, ADAM_EPS, ADAM_WD, ADAM_STEP = 0.001, 0.9, 0.999, 1e-08, 0.01, 10

LANE = 128
SUBLANE = 8
HEADS = 4

Z_FQ, Z_FK, Z_FV, Z_LX, Z_LG, Z_QC, Z_KVC, Z_KR, Z_FL, Z_W = 0, 512, 1024, 1536, 2048, 2560, 2816, 2944, 3072, 3200
QC_W = 256
ROPE_AT = 64


def _head_pad_map(n_heads, width):
    m = -np.ones(n_heads * LANE, np.int64)
    for h in range(n_heads):
        m[h * LANE:h * LANE + width] = h * width + np.arange(width)
    return m


def _z_map():
    m = -np.ones(Z_W, np.int64)
    o_qc, o_kvc, o_kr = 0, MLA_Q_RANK, MLA_Q_RANK + MLA_KV_RANK
    o_fq = o_kr + MLA_ROPE
    o_fk, o_fv = o_fq + 256, o_fq + 512
    o_fl = o_fv + 256
    o_lx = o_fl + FOX_HEADS
    o_lg = o_lx + LRU_WIDTH
    m[Z_FQ:Z_FQ + 512] = np.where(_head_pad_map(4, 64) >= 0, _head_pad_map(4, 64) + o_fq, -1)
    m[Z_FK:Z_FK + 512] = np.where(_head_pad_map(4, 64) >= 0, _head_pad_map(4, 64) + o_fk, -1)
    m[Z_FV:Z_FV + 512] = np.where(_head_pad_map(4, 64) >= 0, _head_pad_map(4, 64) + o_fv, -1)
    m[Z_LX:Z_LX + 512] = o_lx + np.arange(512)
    m[Z_LG:Z_LG + 512] = o_lg + np.arange(512)
    m[Z_QC:Z_QC + MLA_Q_RANK] = o_qc + np.arange(MLA_Q_RANK)
    m[Z_KVC:Z_KVC + MLA_KV_RANK] = o_kvc + np.arange(MLA_KV_RANK)
    m[Z_KR + ROPE_AT:Z_KR + ROPE_AT + MLA_ROPE] = o_kr + np.arange(MLA_ROPE)
    m[Z_FL:Z_FL + FOX_HEADS] = o_fl + np.arange(FOX_HEADS)
    return m


def _ukv_map():
    m = -np.ones(2 * HEADS * LANE, np.int64)
    for h in range(HEADS):
        m[h * LANE:h * LANE + MLA_NOPE] = h * (MLA_NOPE + MLA_V) + np.arange(MLA_NOPE)
        m[HEADS * LANE + h * LANE:HEADS * LANE + h * LANE + MLA_V] = h * (MLA_NOPE + MLA_V) + MLA_NOPE + np.arange(MLA_V)
    return m


def _omix_map():
    return np.concatenate([_head_pad_map(4, 64), np.where(_head_pad_map(4, 64) >= 0, _head_pad_map(4, 64) + 256, -1),
                           512 + np.arange(512)])


def _pad_to(m, n):
    return np.concatenate([m, -np.ones(n - m.shape[0], np.int64)])


def _take_pad(a, m, axis):
    out = jnp.take(a, jnp.asarray(np.maximum(m, 0), jnp.int32), axis=axis)
    shape = [1] * a.ndim
    shape[axis] = m.shape[0]
    return out * jnp.asarray((m >= 0).reshape(shape), a.dtype)


def _take_inv(a, m, axis):
    n = int(m.max()) + 1
    inv = np.zeros(n, np.int64)
    inv[m[m >= 0]] = np.nonzero(m >= 0)[0]
    return jnp.take(a, jnp.asarray(inv, jnp.int32), axis=axis)


Z_MAP = _z_map()
UQ_COL_MAP = _head_pad_map(HEADS, MLA_NOPE + MLA_ROPE)
UQ_ROW_MAP = _pad_to(np.arange(MLA_Q_RANK), QC_W)
UKV_MAP = _ukv_map()
OMIX_MAP = _omix_map()
OMIX_W = 1536


def _rope_tables(width, at):
    half = MLA_ROPE // 2
    inv = ROPE_THETA ** (-np.arange(half, dtype=np.float32) / half)
    freq = np.zeros((1, width), np.float32)
    m1 = np.zeros((1, width), np.float32)
    m2 = np.zeros((1, width), np.float32)
    for h in range(width // LANE):
        b = h * LANE + at
        freq[0, b:b + half] = inv
        freq[0, b + half:b + 2 * half] = inv
        m1[0, b:b + half] = 1.0
        m2[0, b + half:b + 2 * half] = 1.0
    return freq, m1, m2


def _view(r):
    return r if isinstance(r, tuple) else (r, r.shape[1], 0)


def _blk(dim, cap):
    if dim <= cap:
        return dim
    for b in range(cap, LANE - 1, -LANE):
        if dim % b == 0:
            return b
    return dim


@functools.partial(jax.custom_vjp, nondiff_argnums=(1, 2))
def _roll(x, shift, axis):
    return pltpu.roll(x, shift, axis)


def _roll_fwd(x, shift, axis):
    return pltpu.roll(x, shift, axis), None


def _roll_bwd(shift, axis, _, g):
    return (pltpu.roll(g, g.shape[axis] - shift, axis),)


_roll.defvjp(_roll_fwd, _roll_bwd)


def _rowwise(name, fn, rows, pars, outs, tb=256):
    rows = [_view(r) for r in rows]
    n = rows[0][0].shape[0]
    tb = min(tb, n)
    nr, npar = len(rows), len(pars)

    def kern(*refs):
        r = [refs[k][...].astype(f32) for k in range(nr)]
        p = [refs[nr + k][...] for k in range(npar)]
        res = fn(*r, *p)
        for o_ref, o in zip(refs[nr + npar:], res):
            o_ref[...] = o.astype(o_ref.dtype)

    in_specs = [pl.BlockSpec((tb, w), lambda i, j=idx: (i, j)) for (_, w, idx) in rows]
    in_specs += [pl.BlockSpec(p.shape, lambda i: (0, 0)) for p in pars]
    out_specs = [pl.BlockSpec((tb, w), lambda i: (i, 0)) for (w, _) in outs]
    out_shape = [jax.ShapeDtypeStruct((n, w), dt) for (w, dt) in outs]
    return pl.pallas_call(kern, name=name, grid=(n // tb,), in_specs=in_specs, out_specs=out_specs, out_shape=out_shape,
                          compiler_params=pltpu.CompilerParams(dimension_semantics=("parallel",)))(*[r[0] for r in rows], *pars)


def _rowwise_bwd(name, fn, rows, pars, cts, ndiff, adds=None, tb=256):
    rows = [_view(r) for r in rows]
    adds = adds or {}
    add_keys = sorted(adds)
    n = rows[0][0].shape[0]
    tb = min(tb, n)
    nr, npar, nct, nadd = len(rows), len(pars), len(cts), len(add_keys)

    def kern(*refs):
        i = pl.program_id(0)
        r = [refs[k][...].astype(f32) for k in range(nr)]
        p = [refs[nr + k][...] for k in range(npar)]
        ct = [refs[nr + npar + k][...].astype(f32) for k in range(nct)]
        ad = {key: refs[nr + npar + nct + k][...] for k, key in enumerate(add_keys)}
        o_refs = refs[nr + npar + nct + nadd:]

        def g(*d):
            return tuple(fn(*d[:ndiff], *r[ndiff:], *d[ndiff:]))

        _, vjp = jax.vjp(g, *r[:ndiff], *p)
        grads = vjp(tuple(ct))
        for k in range(ndiff):
            gk = grads[k]
            if k in ad:
                gk = gk + ad[k]
            o_refs[k][...] = gk.astype(o_refs[k].dtype)

        @pl.when(i == 0)
        def _():
            for k in range(npar):
                o_refs[ndiff + k][...] = jnp.zeros_like(o_refs[ndiff + k])

        for k in range(npar):
            o_refs[ndiff + k][...] += grads[ndiff + k]

    in_specs = [pl.BlockSpec((tb, w), lambda i, j=idx: (i, j)) for (_, w, idx) in rows]
    in_specs += [pl.BlockSpec(p.shape, lambda i: (0, 0)) for p in pars]
    in_specs += [pl.BlockSpec((tb, c.shape[1]), lambda i: (i, 0)) for c in cts]
    in_specs += [pl.BlockSpec((tb, adds[k].shape[1]), lambda i: (i, 0)) for k in add_keys]
    out_specs = [pl.BlockSpec((tb, rows[k][1]), lambda i: (i, 0)) for k in range(ndiff)]
    out_specs += [pl.BlockSpec(p.shape, lambda i: (0, 0)) for p in pars]
    out_shape = [jax.ShapeDtypeStruct((n, rows[k][1]), f32) for k in range(ndiff)]
    out_shape += [jax.ShapeDtypeStruct(p.shape, f32) for p in pars]
    res = pl.pallas_call(kern, name=name, grid=(n // tb,), in_specs=in_specs, out_specs=out_specs, out_shape=out_shape,
                         compiler_params=pltpu.CompilerParams(dimension_semantics=("arbitrary",)))(
        *[r[0] for r in rows], *pars, *cts, *[adds[k] for k in add_keys])
    return res[:ndiff], res[ndiff:]


_DOT_DIMS = {"nn": ((1,), (0,)), "nt": ((1,), (1,)), "tn": ((0,), (0,))}


def _mm(name, a, b, mode="nn", out_dtype=f32, res=None):
    if mode == "nn":
        (m, k), (_, n) = a.shape, b.shape
    elif mode == "nt":
        (m, k), (n, _) = a.shape, b.shape
    else:
        (k, m), (_, n) = a.shape, b.shape
    tm, tn = _blk(m, 512), _blk(n, 512)
    tk = _blk(k, 512 if mode == "tn" else 1024)
    nk = k // tk
    dims = (_DOT_DIMS[mode], ((), ()))
    has_res = res is not None

    def kern(*refs):
        a_ref, b_ref = refs[0], refs[1]
        o_ref, acc_ref = refs[-2], refs[-1]
        kk = pl.program_id(2)

        @pl.when(kk == 0)
        def _():
            acc_ref[...] = jnp.zeros_like(acc_ref)

        acc_ref[...] += lax.dot_general(a_ref[...].astype(bf16), b_ref[...].astype(bf16), dims, preferred_element_type=f32)

        @pl.when(kk == nk - 1)
        def _():
            out = acc_ref[...]
            if has_res:
                out = out + refs[2][...]
            o_ref[...] = out.astype(o_ref.dtype)

    if mode == "tn":
        a_spec = pl.BlockSpec((tk, tm), lambda i, j, kk: (kk, i))
    else:
        a_spec = pl.BlockSpec((tm, tk), lambda i, j, kk: (i, kk))
    if mode == "nt":
        b_spec = pl.BlockSpec((tn, tk), lambda i, j, kk: (j, kk))
    else:
        b_spec = pl.BlockSpec((tk, tn), lambda i, j, kk: (kk, j))
    in_specs = [a_spec, b_spec]
    args = [a, b]
    if has_res:
        in_specs.append(pl.BlockSpec((tm, tn), lambda i, j, kk: (i, j)))
        args.append(res)
    return pl.pallas_call(
        kern, name=name, grid=(m // tm, n // tn, nk), in_specs=in_specs,
        out_specs=pl.BlockSpec((tm, tn), lambda i, j, kk: (i, j)),
        out_shape=jax.ShapeDtypeStruct((m, n), out_dtype),
        scratch_shapes=[pltpu.VMEM((tm, tn), f32)],
        compiler_params=pltpu.CompilerParams(dimension_semantics=("parallel", "parallel", "arbitrary")))(*args)


ATT_T = 512


def _att_tile(s):
    return min(ATT_T, s)


def _scores(qb, kb, scale, cq, ck, qi, kj, t):
    s = lax.dot_general(qb, kb, (_DOT_DIMS["nt"], ((), ())), preferred_element_type=f32) * scale
    if cq is not None:
        s = s + cq - ck
    row = lax.broadcasted_iota(jnp.int32, (t, t), 0) + qi * t
    col = lax.broadcasted_iota(jnp.int32, (t, t), 1) + kj * t
    return jnp.where(col <= row, s, -jnp.inf)


def _attn_fwd(name, q, k, v, scale, c_col=None, c_row=None):
    (qa, qo), (ka, ko), (va, vo) = q, k, v
    s_len = qa.shape[0]
    t = _att_tile(s_len)
    nt = s_len // t
    decay = c_col is not None

    def kern(*refs):
        q_ref, k_ref, v_ref = refs[:3]
        o_ref, lse_ref = refs[-2:]
        i = pl.program_id(1)
        qb = q_ref[...].astype(bf16)
        cq = refs[3][...] if decay else None

        def step(j, carry):
            m, l, acc = carry
            rows = pl.ds(pl.multiple_of(j * t, t), t)
            kb = k_ref[rows, :].astype(bf16)
            vb = v_ref[rows, :].astype(bf16)
            s = _scores(qb, kb, scale, cq, refs[4][j] if decay else None, i, j, t)
            m_new = jnp.maximum(m, jnp.max(s, axis=1, keepdims=True))
            alpha = jnp.exp(m - m_new)
            p = jnp.exp(s - m_new)
            l = alpha * l + jnp.sum(p, axis=1, keepdims=True)
            acc = alpha * acc + jnp.dot(p.astype(bf16), vb, preferred_element_type=f32)
            return m_new, l, acc

        init = (jnp.full((t, 1), -jnp.inf, f32), jnp.zeros((t, 1), f32), jnp.zeros((t, LANE), f32))
        m, l, acc = lax.fori_loop(0, i + 1, step, init)
        o_ref[...] = acc / l
        lse_ref[...] = m + jnp.log(l)

    in_specs = [pl.BlockSpec((t, LANE), lambda h, i: (i, qo + h)),
                pl.BlockSpec((s_len, LANE), lambda h, i: (0, ko + h)),
                pl.BlockSpec((s_len, LANE), lambda h, i: (0, vo + h))]
    args = [qa, ka, va]
    if decay:
        in_specs += [pl.BlockSpec((None, t, 1), lambda h, i: (h, i, 0)),
                     pl.BlockSpec((None, nt, 1, t), lambda h, i: (h, 0, 0, 0))]
        args += [c_col, c_row]
    return pl.pallas_call(
        kern, name=name, grid=(HEADS, nt), in_specs=in_specs,
        out_specs=[pl.BlockSpec((t, LANE), lambda h, i: (i, h)), pl.BlockSpec((None, t, 1), lambda h, i: (h, i, 0))],
        out_shape=[jax.ShapeDtypeStruct((s_len, HEADS * LANE), f32), jax.ShapeDtypeStruct((HEADS, s_len, 1), f32)],
        compiler_params=pltpu.CompilerParams(dimension_semantics=("parallel", "arbitrary")))(*args)


def _attn_dq(name, q, k, v, o, do, lse, scale, c_col=None, c_row=None):
    (qa, qo), (ka, ko), (va, vo) = q, k, v
    s_len = qa.shape[0]
    t = _att_tile(s_len)
    nt = s_len // t
    decay = c_col is not None

    def kern(*refs):
        q_ref, k_ref, v_ref, o_ref, do_ref, lse_ref = refs[:6]
        dq_ref, delta_ref, drow_ref = refs[-3:]
        i = pl.program_id(1)
        qb = q_ref[...].astype(bf16)
        dob = do_ref[...]
        delta = jnp.sum(dob * o_ref[...], axis=1, keepdims=True)
        dob = dob.astype(bf16)
        lse = lse_ref[...]
        cq = refs[6][...] if decay else None

        def step(j, carry):
            dq, drow = carry
            rows = pl.ds(pl.multiple_of(j * t, t), t)
            kb = k_ref[rows, :].astype(bf16)
            vb = v_ref[rows, :].astype(bf16)
            s = _scores(qb, kb, scale, cq, refs[7][j] if decay else None, i, j, t)
            p = jnp.exp(s - lse)
            dp = lax.dot_general(dob, vb, (_DOT_DIMS["nt"], ((), ())), preferred_element_type=f32)
            ds = p * (dp - delta)
            return dq + jnp.dot(ds.astype(bf16), kb, preferred_element_type=f32), drow + jnp.sum(ds, axis=1, keepdims=True)

        dq, drow = lax.fori_loop(0, i + 1, step, (jnp.zeros((t, LANE), f32), jnp.zeros((t, 1), f32)))
        dq_ref[...] = dq * scale
        delta_ref[...] = delta
        drow_ref[...] = drow

    in_specs = [pl.BlockSpec((t, LANE), lambda h, i: (i, qo + h)),
                pl.BlockSpec((s_len, LANE), lambda h, i: (0, ko + h)),
                pl.BlockSpec((s_len, LANE), lambda h, i: (0, vo + h)),
                pl.BlockSpec((t, LANE), lambda h, i: (i, h)),
                pl.BlockSpec((t, LANE), lambda h, i: (i, h)),
                pl.BlockSpec((None, t, 1), lambda h, i: (h, i, 0))]
    args = [qa, ka, va, o, do, lse]
    if decay:
        in_specs += [pl.BlockSpec((None, t, 1), lambda h, i: (h, i, 0)),
                     pl.BlockSpec((None, nt, 1, t), lambda h, i: (h, 0, 0, 0))]
        args += [c_col, c_row]
    col = pl.BlockSpec((None, t, 1), lambda h, i: (h, i, 0))
    return pl.pallas_call(
        kern, name=name, grid=(HEADS, nt), in_specs=in_specs,
        out_specs=[pl.BlockSpec((t, LANE), lambda h, i: (i, h)), col, col],
        out_shape=[jax.ShapeDtypeStruct((s_len, HEADS * LANE), f32), jax.ShapeDtypeStruct((HEADS, s_len, 1), f32),
                   jax.ShapeDtypeStruct((HEADS, s_len, 1), f32)],
        compiler_params=pltpu.CompilerParams(dimension_semantics=("parallel", "arbitrary")))(*args)


def _attn_dkv(name, q, k, v, do, lse, delta, scale, c_col=None, c_row=None):
    (qa, qo), (ka, ko), (va, vo) = q, k, v
    s_len = qa.shape[0]
    t = _att_tile(s_len)
    nt = s_len // t
    decay = c_col is not None

    def kern(*refs):
        q_ref, k_ref, v_ref, do_ref, lse_ref, delta_ref = refs[:6]
        j = pl.program_id(1)
        kb = k_ref[...].astype(bf16)
        vb = v_ref[...].astype(bf16)
        ck = refs[7][...] if decay else None

        def step(i, carry):
            dk, dv, dc = carry
            rows = pl.ds(pl.multiple_of(i * t, t), t)
            qb = q_ref[rows, :].astype(bf16)
            dob = do_ref[rows, :].astype(bf16)
            s = _scores(qb, kb, scale, refs[6][rows, :] if decay else None, ck, i, j, t)
            p = jnp.exp(s - lse_ref[rows, :])
            dv = dv + lax.dot_general(p.astype(bf16), dob, (_DOT_DIMS["tn"], ((), ())), preferred_element_type=f32)
            dp = lax.dot_general(dob, vb, (_DOT_DIMS["nt"], ((), ())), preferred_element_type=f32)
            ds = p * (dp - delta_ref[rows, :])
            dk = dk + lax.dot_general(ds.astype(bf16), qb, (_DOT_DIMS["tn"], ((), ())), preferred_element_type=f32)
            if decay:
                dc = dc - jnp.sum(ds, axis=0, keepdims=True)
            return dk, dv, dc

        init = (jnp.zeros((t, LANE), f32), jnp.zeros((t, LANE), f32), jnp.zeros((1, t), f32))
        dk, dv, dc = lax.fori_loop(j, nt, step, init)
        if decay:
            dk_ref, dv_ref, dc_ref = refs[-3:]
            dc_ref[...] = dc
        else:
            dk_ref, dv_ref = refs[-2:]
        dk_ref[...] = dk * scale
        dv_ref[...] = dv

    in_specs = [pl.BlockSpec((s_len, LANE), lambda h, j: (0, qo + h)),
                pl.BlockSpec((t, LANE), lambda h, j: (j, ko + h)),
                pl.BlockSpec((t, LANE), lambda h, j: (j, vo + h)),
                pl.BlockSpec((s_len, LANE), lambda h, j: (0, h)),
                pl.BlockSpec((None, s_len, 1), lambda h, j: (h, 0, 0)),
                pl.BlockSpec((None, s_len, 1), lambda h, j: (h, 0, 0))]
    args = [qa, ka, va, do, lse, delta]
    out_specs = [pl.BlockSpec((t, LANE), lambda h, j: (j, h)), pl.BlockSpec((t, LANE), lambda h, j: (j, h))]
    out_shape = [jax.ShapeDtypeStruct((s_len, HEADS * LANE), f32), jax.ShapeDtypeStruct((s_len, HEADS * LANE), f32)]
    if decay:
        in_specs += [pl.BlockSpec((None, s_len, 1), lambda h, j: (h, 0, 0)),
                     pl.BlockSpec((None, None, 1, t), lambda h, j: (h, j, 0, 0))]
        args += [c_col, c_row]
        out_specs.append(pl.BlockSpec((None, None, 1, t), lambda h, j: (h, j, 0, 0)))
        out_shape.append(jax.ShapeDtypeStruct((HEADS, nt, 1, t), f32))
    return pl.pallas_call(
        kern, name=name, grid=(HEADS, nt), in_specs=in_specs, out_specs=out_specs, out_shape=out_shape,
        compiler_params=pltpu.CompilerParams(dimension_semantics=("parallel", "arbitrary")))(*args)


CONV_TS, CONV_CB = 1024, 256


def _conv_fwd(name, x, w, b, taps):
    xa, width, xidx = _view(x)
    s_len = xa.shape[0]
    ts, cb = min(CONV_TS, s_len), CONV_CB
    xo = xidx * width // cb

    def kern(x_ref, halo_ref, w_ref, b_ref, o_ref):
        i = pl.program_id(1)
        xb = x_ref[...]
        halo = jnp.where(i == 0, 0.0, halo_ref[...])
        xx = jnp.concatenate([halo, xb], axis=0)
        out = b_ref[...] + w_ref[taps - 1:taps, :] * xb
        for k in range(taps - 1):
            out = out + w_ref[k:k + 1, :] * pltpu.roll(xx, taps - 1 - k, 0)[SUBLANE:]
        o_ref[...] = out

    return pl.pallas_call(
        kern, name=name, grid=(width // cb, s_len // ts),
        in_specs=[pl.BlockSpec((ts, cb), lambda j, i: (i, xo + j)),
                  pl.BlockSpec((SUBLANE, cb), lambda j, i: (jnp.maximum(i * (ts // SUBLANE) - 1, 0), xo + j)),
                  pl.BlockSpec((taps, cb), lambda j, i: (0, j)),
                  pl.BlockSpec((1, cb), lambda j, i: (0, j))],
        out_specs=pl.BlockSpec((ts, cb), lambda j, i: (i, j)),
        out_shape=jax.ShapeDtypeStruct((s_len, width), f32),
        compiler_params=pltpu.CompilerParams(dimension_semantics=("parallel", "parallel")))(xa, xa, w, b)


def _conv_bwd(name, x, dout, w, taps, dout2=None):
    xa, width, xidx = _view(x)
    s_len = xa.shape[0]
    ts, cb = min(CONV_TS, s_len), CONV_CB
    xo = xidx * width // cb
    n_i = s_len // ts
    two = dout2 is not None

    def kern(*refs):
        x_ref, halo_ref, w_ref = refs[:3]
        dx_ref, dw_ref, db_ref = refs[-3:]
        i = pl.program_id(1)
        if two:
            d = refs[3][...] + refs[5][...]
            dn = refs[4][...] + refs[6][...]
        else:
            d, dn = refs[3][...], refs[4][...]
        dn = jnp.where(i == n_i - 1, 0.0, dn)
        xb = x_ref[...]
        halo = jnp.where(i == 0, 0.0, halo_ref[...])
        xx = jnp.concatenate([halo, xb], axis=0)
        dd = jnp.concatenate([d, dn], axis=0)

        @pl.when(i == 0)
        def _():
            dw_ref[...] = jnp.zeros_like(dw_ref)
            db_ref[...] = jnp.zeros_like(db_ref)

        dx = w_ref[taps - 1:taps, :] * d
        dw_ref[taps - 1:taps, :] += jnp.sum(d * xb, axis=0, keepdims=True)
        for k in range(taps - 1):
            sh = taps - 1 - k
            dx = dx + w_ref[k:k + 1, :] * pltpu.roll(dd, ts + SUBLANE - sh, 0)[:ts]
            dw_ref[k:k + 1, :] += jnp.sum(d * pltpu.roll(xx, sh, 0)[SUBLANE:], axis=0, keepdims=True)
        dx_ref[...] = dx
        db_ref[...] += jnp.sum(d, axis=0, keepdims=True)

    d_spec = pl.BlockSpec((ts, cb), lambda j, i: (i, j))
    dn_spec = pl.BlockSpec((SUBLANE, cb), lambda j, i: (jnp.minimum((i + 1) * (ts // SUBLANE), s_len // SUBLANE - 1), j))
    in_specs = [pl.BlockSpec((ts, cb), lambda j, i: (i, xo + j)),
                pl.BlockSpec((SUBLANE, cb), lambda j, i: (jnp.maximum(i * (ts // SUBLANE) - 1, 0), xo + j)),
                pl.BlockSpec((taps, cb), lambda j, i: (0, j)), d_spec, dn_spec]
    args = [xa, xa, w, dout, dout]
    if two:
        in_specs += [d_spec, dn_spec]
        args += [dout2, dout2]
    return pl.pallas_call(
        kern, name=name, grid=(width // cb, n_i), in_specs=in_specs,
        out_specs=[pl.BlockSpec((ts, cb), lambda j, i: (i, j)), pl.BlockSpec((taps, cb), lambda j, i: (0, j)),
                   pl.BlockSpec((1, cb), lambda j, i: (0, j))],
        out_shape=[jax.ShapeDtypeStruct((s_len, width), f32), jax.ShapeDtypeStruct((taps, width), f32),
                   jax.ShapeDtypeStruct((1, width), f32)],
        compiler_params=pltpu.CompilerParams(dimension_semantics=("parallel", "arbitrary")))(*args)


def _segment_carries(a_last, h_last, reverse):
    ridx = lax.broadcasted_iota(jnp.int32, (SUBLANE, LANE), 0)

    def pick(m, s):
        return jnp.sum(jnp.where(ridx == s, m, 0.0), axis=0, keepdims=True)

    carry = jnp.zeros((SUBLANE, LANE), f32)
    prev = jnp.zeros((1, LANE), f32)
    order = range(SUBLANE - 2, -1, -1) if reverse else range(1, SUBLANE)
    for s in order:
        src = s + 1 if reverse else s - 1
        prev = pick(a_last, src) * prev + pick(h_last, src)
        carry = jnp.where(ridx == s, prev, carry)
    return carry


def _scan_fwd(name, a, b):
    s_len, width = a.shape
    seg = s_len // SUBLANE

    def kern(a_ref, b_ref, h_ref, ap_ref):
        def p1(t, c):
            h, acc = c
            idx = pl.ds(t, SUBLANE, stride=seg)
            av = a_ref[idx, :]
            h = av * h + b_ref[idx, :]
            acc = av * acc
            h_ref[idx, :] = h
            ap_ref[idx, :] = acc
            return h, acc

        h_last, a_last = lax.fori_loop(0, seg, p1, (jnp.zeros((SUBLANE, LANE), f32), jnp.ones((SUBLANE, LANE), f32)))
        carry = _segment_carries(a_last, h_last, False)

        def p3(t, c):
            idx = pl.ds(t, SUBLANE, stride=seg)
            h_ref[idx, :] = h_ref[idx, :] + ap_ref[idx, :] * carry
            return c

        lax.fori_loop(0, seg, p3, 0)

    spec = pl.BlockSpec((s_len, LANE), lambda j: (0, j))
    return pl.pallas_call(
        kern, name=name, grid=(width // LANE,), in_specs=[spec, spec], out_specs=spec,
        out_shape=jax.ShapeDtypeStruct((s_len, width), f32), scratch_shapes=[pltpu.VMEM((s_len, LANE), f32)],
        compiler_params=pltpu.CompilerParams(dimension_semantics=("parallel",)))(a, b)


def _scan_bwd(name, a_next, h_prev, dh):
    s_len, width = dh.shape
    seg = s_len // SUBLANE

    def kern(an_ref, hp_ref, dh_ref, da_ref, db_ref, ap_ref):
        def p1(tt, c):
            g, acc = c
            idx = pl.ds(seg - 1 - tt, SUBLANE, stride=seg)
            av = an_ref[idx, :]
            g = av * g + dh_ref[idx, :]
            acc = av * acc
            db_ref[idx, :] = g
            ap_ref[idx, :] = acc
            return g, acc

        g_last, a_last = lax.fori_loop(0, seg, p1, (jnp.zeros((SUBLANE, LANE), f32), jnp.ones((SUBLANE, LANE), f32)))
        carry = _segment_carries(a_last, g_last, True)

        def p3(t, c):
            idx = pl.ds(t, SUBLANE, stride=seg)
            g = db_ref[idx, :] + ap_ref[idx, :] * carry
            db_ref[idx, :] = g
            da_ref[idx, :] = g * hp_ref[idx, :]
            return c

        lax.fori_loop(0, seg, p3, 0)

    spec = pl.BlockSpec((s_len, LANE), lambda j: (0, j))
    return pl.pallas_call(
        kern, name=name, grid=(width // LANE,), in_specs=[spec, spec, spec], out_specs=[spec, spec],
        out_shape=[jax.ShapeDtypeStruct((s_len, width), f32)] * 2, scratch_shapes=[pltpu.VMEM((s_len, LANE), f32)],
        compiler_params=pltpu.CompilerParams(dimension_semantics=("parallel",)))(a_next, h_prev, dh)


def _lane_cumsum(x, reverse):
    n = x.shape[1]
    lane = lax.broadcasted_iota(jnp.int32, x.shape, 1)
    sh = 1
    while sh < n:
        if reverse:
            x = x + jnp.where(lane < n - sh, pltpu.roll(x, n - sh, 1), 0.0)
        else:
            x = x + jnp.where(lane >= sh, pltpu.roll(x, sh, 1), 0.0)
        sh *= 2
    return x


def _decay_fwd(name, fl_t, b8):
    def kern(f_ref, b_ref, c_ref):
        c_ref[...] = _lane_cumsum(jax.nn.log_sigmoid(f_ref[...] + b_ref[...]), False)

    return pl.pallas_call(kern, name=name, out_shape=jax.ShapeDtypeStruct(fl_t.shape, f32))(fl_t, b8)


def _decay_bwd(name, fl_t, b8, dc_key, dc_query):
    def kern(f_ref, b_ref, dck_ref, dcq_ref, df_ref, db_ref):
        dlogf = _lane_cumsum(dck_ref[...] + dcq_ref[...], True)
        df = dlogf * jax.nn.sigmoid(-(f_ref[...] + b_ref[...]))
        df_ref[...] = df
        db_ref[...] = jnp.sum(df, axis=1, keepdims=True)

    return pl.pallas_call(kern, name=name, out_shape=[jax.ShapeDtypeStruct(fl_t.shape, f32),
                                                      jax.ShapeDtypeStruct((SUBLANE, 1), f32)])(fl_t, b8, dc_key, dc_query)


def _rms(x, g, n):
    return x * lax.rsqrt(jnp.sum(x * x, axis=-1, keepdims=True) * (1.0 / n) + EPS) * g


def _loss_head(name, h, target, g, tb=256):
    n, d = h.shape
    tb = min(tb, n)

    def kern(h_ref, t_ref, g_ref, loss_ref, dh_ref, dg_ref):
        i = pl.program_id(0)
        tgt = t_ref[...]

        def f(hv, gv):
            err = _rms(hv, gv, d) - tgt
            return 0.5 * jnp.sum(jnp.sum(err * err, axis=-1, keepdims=True) * (1.0 / d), axis=0, keepdims=True)

        val, vjp = jax.vjp(f, h_ref[...], g_ref[...])
        dh, dg = vjp(jnp.ones((1, 1), f32))
        dh_ref[...] = dh

        @pl.when(i == 0)
        def _():
            loss_ref[...] = jnp.zeros_like(loss_ref)
            dg_ref[...] = jnp.zeros_like(dg_ref)

        loss_ref[...] += val
        dg_ref[...] += dg

    return pl.pallas_call(
        kern, name=name, grid=(n // tb,),
        in_specs=[pl.BlockSpec((tb, d), lambda i: (i, 0)), pl.BlockSpec((tb, d), lambda i: (i, 0)),
                  pl.BlockSpec((1, d), lambda i: (0, 0))],
        out_specs=[pl.BlockSpec((1, 1), lambda i: (0, 0)), pl.BlockSpec((tb, d), lambda i: (i, 0)),
                   pl.BlockSpec((1, d), lambda i: (0, 0))],
        out_shape=[jax.ShapeDtypeStruct((1, 1), f32), jax.ShapeDtypeStruct((n, d), f32), jax.ShapeDtypeStruct((1, d), f32)],
        compiler_params=pltpu.CompilerParams(dimension_semantics=("arbitrary",)))(h, target, g)


def _f_norm(x, g):
    return (_rms(x, g, D_MODEL),)


def _f_latent(qc, kvc, gq, gkv):
    return _rms(qc, gq, MLA_Q_RANK), _rms(kvc, gkv, MLA_KV_RANK)


def _rope(x, pos, freq, m1, m2):
    ang = pos * freq
    sin = jnp.sin(ang)
    w = x.shape[1]
    return x * jnp.cos(ang) - _roll(x, w - MLA_ROPE // 2, 1) * (sin * m1) + _roll(x, MLA_ROPE // 2, 1) * (sin * m2)


def _f_mla_prep(q, kpart, kr, pos, fq, m1q, m2q, fk, m1k, m2k):
    kr = _rope(kr, pos, fk, m1k, m2k)
    return _rope(q, pos, fq, m1q, m2q), kpart + jnp.concatenate([kr] * HEADS, axis=1)


def _f_lru_gate(gates, xc, b_r, b_i, lam):
    r = jax.nn.sigmoid(gates[:, :LRU_WIDTH] + b_r)
    i = jax.nn.sigmoid(gates[:, LRU_WIDTH:] + b_i)
    log_a = -LRU_C * r * jax.nn.softplus(-lam)
    mult = jnp.sqrt(-jnp.tanh(log_a) * (1.0 + jnp.exp(2.0 * log_a)))
    return jnp.exp(log_a), mult * (i * xc)


def _f_merge(o_mla, o_fox, hs, lg, g):
    o_lru = hs * jax.nn.gelu(lg)
    return (jnp.concatenate([_rms(o_mla, g[:, :512], HEADS * MLA_V), _rms(o_fox, g[:, 512:1024], HEADS * FOX_HEAD_DIM),
                             _rms(o_lru, g[:, 1024:], LRU_WIDTH)], axis=1),)


def _f_ffn_gate(ug, uv):
    return (jax.nn.silu(ug) * uv,)


def _f_ple(h, gpre, pp):
    return (h + jax.nn.sigmoid(gpre) * pp,)


def _prep_layer_weights(w):
    eye = jnp.eye(LRU_BLOCKS, dtype=f32)

    def block_diag(m):
        return (eye[:, None, :, None] * m[:, :, None, :]).reshape(LRU_WIDTH, LRU_WIDTH)

    return dict(
        w_in=_take_pad(w["w_in"], Z_MAP, 1),
        w_uq=_take_pad(_take_pad(w["w_uq"], UQ_COL_MAP, 1), UQ_ROW_MAP, 0),
        w_ukv=_take_pad(w["w_ukv"], UKV_MAP, 1),
        w_ri=jnp.concatenate([block_diag(w["w_r"]), block_diag(w["w_i"])], axis=1).astype(bf16),
        w_o=_take_pad(w["w_o"], OMIX_MAP, 0),
        w_up=w["w_up"], w_down=w["w_down"], w_ple_gate=w["w_ple_gate"], w_ple_proj=w["w_ple_proj"],
        g_mix=w["g_mix"].reshape(1, -1), g_ffn=w["g_ffn"].reshape(1, -1), g_ple=w["g_ple"].reshape(1, -1),
        g_qc=_take_pad(w["g_qc"], UQ_ROW_MAP, 0).reshape(1, -1), g_kvc=w["g_kvc"].reshape(1, -1),
        g_out=_take_pad(w["g_out"], OMIX_MAP, 0).reshape(1, -1),
        b_f8=_take_pad(w["b_f"], _pad_to(np.arange(FOX_HEADS), SUBLANE), 0).reshape(SUBLANE, 1),
        lru_conv_w=w["lru_conv_w"], lru_conv_b=w["lru_conv_b"].reshape(1, -1),
        b_r=w["b_r"].reshape(1, -1), b_i=w["b_i"].reshape(1, -1), lam=w["lru_lambda"].reshape(1, -1),
        ffn_conv_w=w["ffn_conv_w"], ffn_conv_b=w["ffn_conv_b"].reshape(1, -1),
    )


def _rope_consts():
    fq, m1q, m2q = _rope_tables(HEADS * LANE, ROPE_AT)
    fk, m1k, m2k = _rope_tables(LANE, ROPE_AT)
    return [jnp.asarray(t) for t in (fq, m1q, m2q, fk, m1k, m2k)]


def _c_layouts(c_t, s_len):
    t = _att_tile(s_len)
    c4 = c_t[:HEADS]
    return c4.reshape(HEADS, s_len, 1), c4.reshape(HEADS, s_len // t, 1, t)


def _layer_fwd(l, h0, p_l, pos, w):
    s_len = h0.shape[0]
    n = f"l{l}_"
    xn, = _rowwise(n + "norm_mix", _f_norm, [h0], [w["g_mix"]], [(D_MODEL, bf16)])
    z = _mm(n + "in_proj", xn, w["w_in"])
    zq = (z, QC_W, Z_QC // QC_W)
    zkv = (z, LANE, Z_KVC // LANE)
    zkr = (z, LANE, Z_KR // LANE)
    zlx = (z, LRU_WIDTH, Z_LX // LRU_WIDTH)
    zlg = (z, LRU_WIDTH, Z_LG // LRU_WIDTH)
    qcn, kvn = _rowwise(n + "latent_norm", _f_latent, [zq, zkv], [w["g_qc"], w["g_kvc"]], [(QC_W, bf16), (LANE, bf16)])
    q = _mm(n + "uq", qcn, w["w_uq"])
    kv = _mm(n + "ukv", kvn, w["w_ukv"])
    kpart = (kv, HEADS * LANE, 0)
    qr, kk = _rowwise(n + "mla_prep", _f_mla_prep, [q, kpart, zkr, pos], _rope_consts(),
                      [(HEADS * LANE, bf16), (HEADS * LANE, bf16)])
    mla_scale = (MLA_NOPE + MLA_ROPE) ** -0.5
    o_mla, lse_m = _attn_fwd(n + "mla_fwd", (qr, 0), (kk, 0), (kv, HEADS), mla_scale)
    fl_t = z[:, Z_FL:Z_FL + SUBLANE].T
    c_t = _decay_fwd(n + "decay", fl_t, w["b_f8"])
    c_col, c_row = _c_layouts(c_t, s_len)
    fox_scale = FOX_HEAD_DIM ** -0.5
    o_fox, lse_f = _attn_fwd(n + "fox_fwd", (z, Z_FQ // LANE), (z, Z_FK // LANE), (z, Z_FV // LANE), fox_scale, c_col, c_row)
    xc = _conv_fwd(n + "lru_conv", zlx, w["lru_conv_w"], w["lru_conv_b"], LRU_CONV)
    gates = _mm(n + "lru_gates", xc, w["w_ri"])
    a, bx = _rowwise(n + "lru_gate", _f_lru_gate, [gates, xc], [w["b_r"], w["b_i"], w["lam"]],
                     [(LRU_WIDTH, f32), (LRU_WIDTH, f32)])
    hs = _scan_fwd(n + "lru_scan", a, bx)
    ocat, = _rowwise(n + "merge", _f_merge, [o_mla, o_fox, hs, zlg], [w["g_out"]], [(OMIX_W, bf16)])
    h1 = _mm(n + "out_proj", ocat, w["w_o"], res=h0)
    xn2, = _rowwise(n + "norm_ffn", _f_norm, [h1], [w["g_ffn"]], [(D_MODEL, bf16)])
    up = _mm(n + "up_proj", xn2, w["w_up"])
    u = _conv_fwd(n + "ffn_conv", up, w["ffn_conv_w"], w["ffn_conv_b"], FFN_CONV)
    act, = _rowwise(n + "ffn_gate", _f_ffn_gate, [(u, D_FF, 0), (u, D_FF, 1)], [], [(D_FF, bf16)])
    h2 = _mm(n + "down_proj", act, w["w_down"], res=h1)
    hn, = _rowwise(n + "norm_ple", _f_norm, [h2], [w["g_ple"]], [(D_MODEL, bf16)])
    gpre = _mm(n + "ple_gate", hn, w["w_ple_gate"])
    pp = _mm(n + "ple_proj", p_l, w["w_ple_proj"])
    h3, = _rowwise(n + "ple_mix", _f_ple, [h2, gpre, pp], [], [(D_MODEL, f32)])
    res = dict(h0=h0, xn=xn, z=z, qcn=qcn, kvn=kvn, q=q, kv=kv, qr=qr, kk=kk, o_mla=o_mla, lse_m=lse_m, fl_t=fl_t,
               c_col=c_col, c_row=c_row, o_fox=o_fox, lse_f=lse_f, xc=xc, gates=gates, a=a, hs=hs, ocat=ocat, h1=h1,
               xn2=xn2, up=up, u=u, act=act, h2=h2, hn=hn, gpre=gpre, pp=pp, p_l=p_l)
    return h3, res


def _layer_bwd(l, dh3, r, pos, w):
    s_len = dh3.shape[0]
    n = f"l{l}_"
    g = {}
    z = r["z"]
    zq = (z, QC_W, Z_QC // QC_W)
    zkv = (z, LANE, Z_KVC // LANE)
    zkr = (z, LANE, Z_KR // LANE)
    zlx = (z, LRU_WIDTH, Z_LX // LRU_WIDTH)
    zlg = (z, LRU_WIDTH, Z_LG // LRU_WIDTH)
    (dh2a, dgpre, dpp), _ = _rowwise_bwd(n + "ple_mix_b", _f_ple, [r["h2"], r["gpre"], r["pp"]], [], [dh3], 3)
    g["w_ple_proj"] = _mm(n + "ple_proj_dw", r["p_l"], dpp, "tn")
    dhn = _mm(n + "ple_gate_dx", dgpre, w["w_ple_gate"], "nt")
    g["w_ple_gate"] = _mm(n + "ple_gate_dw", r["hn"], dgpre, "tn")
    (dh2,), (g["g_ple"],) = _rowwise_bwd(n + "norm_ple_b", _f_norm, [r["h2"]], [w["g_ple"]], [dhn], 1, adds={0: dh2a})
    dact = _mm(n + "down_dx", dh2, w["w_down"], "nt")
    g["w_down"] = _mm(n + "down_dw", r["act"], dh2, "tn")
    u = r["u"]
    (dug, duv), _ = _rowwise_bwd(n + "ffn_gate_b", _f_ffn_gate, [(u, D_FF, 0), (u, D_FF, 1)], [], [dact], 2)
    du = jnp.concatenate([dug, duv], axis=1)
    dup, g["ffn_conv_w"], g["ffn_conv_b"] = _conv_bwd(n + "ffn_conv_b", r["up"], du, w["ffn_conv_w"], FFN_CONV)
    dxn2 = _mm(n + "up_dx", dup, w["w_up"], "nt")
    g["w_up"] = _mm(n + "up_dw", r["xn2"], dup, "tn")
    (dh1,), (g["g_ffn"],) = _rowwise_bwd(n + "norm_ffn_b", _f_norm, [r["h1"]], [w["g_ffn"]], [dxn2], 1, adds={0: dh2})
    docat = _mm(n + "out_dx", dh1, w["w_o"], "nt")
    g["w_o"] = _mm(n + "out_dw", r["ocat"], dh1, "tn")
    (do_mla, do_fox, dhs, dlg), (g["g_out"],) = _rowwise_bwd(
        n + "merge_b", _f_merge, [r["o_mla"], r["o_fox"], r["hs"], zlg], [w["g_out"]], [docat], 4)
    a, hs = r["a"], r["hs"]
    a_next = jnp.concatenate([a[1:], jnp.zeros((1, LRU_WIDTH), f32)], axis=0)
    h_prev = jnp.concatenate([jnp.zeros((1, LRU_WIDTH), f32), hs[:-1]], axis=0)
    da, dbx = _scan_bwd(n + "lru_scan_b", a_next, h_prev, dhs)
    (dgates, dxc_a), (g["b_r"], g["b_i"], g["lam"]) = _rowwise_bwd(
        n + "lru_gate_b", _f_lru_gate, [r["gates"], r["xc"]], [w["b_r"], w["b_i"], w["lam"]], [da, dbx], 2)
    dxc_b = _mm(n + "lru_gates_dx", dgates, w["w_ri"], "nt")
    g["w_ri"] = _mm(n + "lru_gates_dw", r["xc"], dgates, "tn")
    dlx, g["lru_conv_w"], g["lru_conv_b"] = _conv_bwd(n + "lru_conv_b", zlx, dxc_a, w["lru_conv_w"], LRU_CONV, dout2=dxc_b)
    fox_scale = FOX_HEAD_DIM ** -0.5
    fq, fk, fv = (z, Z_FQ // LANE), (z, Z_FK // LANE), (z, Z_FV // LANE)
    dfq, delta_f, dc_q = _attn_dq(n + "fox_dq", fq, fk, fv, r["o_fox"], do_fox, r["lse_f"], fox_scale, r["c_col"], r["c_row"])
    dfk, dfv, dc_k = _attn_dkv(n + "fox_dkv", fq, fk, fv, do_fox, r["lse_f"], delta_f, fox_scale, r["c_col"], r["c_row"])
    pad_rows = jnp.zeros((SUBLANE - HEADS, s_len), f32)
    dfl_t, g["b_f8"] = _decay_bwd(n + "decay_b", r["fl_t"], w["b_f8"],
                                  jnp.concatenate([dc_k.reshape(HEADS, s_len), pad_rows], axis=0),
                                  jnp.concatenate([dc_q.reshape(HEADS, s_len), pad_rows], axis=0))
    dfl = jnp.pad(dfl_t.T, ((0, 0), (0, LANE - SUBLANE)))
    mla_scale = (MLA_NOPE + MLA_ROPE) ** -0.5
    qr, kk, kv = (r["qr"], 0), (r["kk"], 0), (r["kv"], HEADS)
    dqr, delta_m, _ = _attn_dq(n + "mla_dq", qr, kk, kv, r["o_mla"], do_mla, r["lse_m"], mla_scale)
    dkk, dv_m = _attn_dkv(n + "mla_dkv", qr, kk, kv, do_mla, r["lse_m"], delta_m, mla_scale)
    (dq, dkpart, dkr), _ = _rowwise_bwd(n + "mla_prep_b", _f_mla_prep, [r["q"], (r["kv"], HEADS * LANE, 0), zkr, pos],
                                        _rope_consts(), [dqr, dkk], 3)
    dkv = jnp.concatenate([dkpart, dv_m], axis=1)
    dkvn = _mm(n + "ukv_dx", dkv, w["w_ukv"], "nt")
    g["w_ukv"] = _mm(n + "ukv_dw", r["kvn"], dkv, "tn")
    dqcn = _mm(n + "uq_dx", dq, w["w_uq"], "nt")
    g["w_uq"] = _mm(n + "uq_dw", r["qcn"], dq, "tn")
    (dqc, dkvc), (g["g_qc"], g["g_kvc"]) = _rowwise_bwd(n + "latent_norm_b", _f_latent, [zq, zkv],
                                                        [w["g_qc"], w["g_kvc"]], [dqcn, dkvn], 2)
    dz = jnp.concatenate([dfq, dfk, dfv, dlx, dlg, dqc, dkvc, dkr, dfl], axis=1)
    dxn = _mm(n + "in_dx", dz, w["w_in"], "nt")
    g["w_in"] = _mm(n + "in_dw", r["xn"], dz, "tn")
    (dh0,), (g["g_mix"],) = _rowwise_bwd(n + "norm_mix_b", _f_norm, [r["h0"]], [w["g_mix"]], [dxn], 1, adds={0: dh1})
    return dh0, g


def _unpad_layer_grads(g):
    d_ri = g["w_ri"]
    idx = jnp.arange(LRU_BLOCKS)

    def diag_blocks(m):
        return m.reshape(LRU_BLOCKS, LRU_BLOCK, LRU_BLOCKS, LRU_BLOCK)[idx, :, idx, :]

    return dict(
        g_mix=g["g_mix"][0], w_in=_take_inv(g["w_in"], Z_MAP, 1), g_qc=g["g_qc"][0, :MLA_Q_RANK],
        w_uq=_take_inv(g["w_uq"][:MLA_Q_RANK], UQ_COL_MAP, 1), g_kvc=g["g_kvc"][0],
        w_ukv=_take_inv(g["w_ukv"], UKV_MAP, 1), b_f=g["b_f8"][:FOX_HEADS, 0],
        lru_conv_w=g["lru_conv_w"], lru_conv_b=g["lru_conv_b"][0],
        w_r=diag_blocks(d_ri[:, :LRU_WIDTH]), b_r=g["b_r"][0], w_i=diag_blocks(d_ri[:, LRU_WIDTH:]), b_i=g["b_i"][0],
        lru_lambda=g["lam"][0], g_out=_take_inv(g["g_out"][0], OMIX_MAP, 0), w_o=_take_inv(g["w_o"], OMIX_MAP, 0),
        g_ffn=g["g_ffn"][0], w_up=g["w_up"], ffn_conv_w=g["ffn_conv_w"], ffn_conv_b=g["ffn_conv_b"][0],
        w_down=g["w_down"], g_ple=g["g_ple"][0], w_ple_gate=g["w_ple_gate"], w_ple_proj=g["w_ple_proj"],
    )


LAYER_WEIGHTS = ["g_mix", "w_in", "g_qc", "w_uq", "g_kvc", "w_ukv", "b_f", "lru_conv_w", "lru_conv_b", "w_r", "b_r", "w_i",
                 "b_i", "lru_lambda", "g_out", "w_o", "g_ffn", "w_up", "ffn_conv_w", "ffn_conv_b", "w_down", "g_ple",
                 "w_ple_gate", "w_ple_proj"]
WEIGHTS = LAYER_WEIGHTS + ["g_final"]


def _local_step(x, p, positions, target, weights):
    pos = positions.astype(f32).reshape(-1, 1)
    h = x
    ws, saved = [], []
    for l in range(DEPTH):
        w = _prep_layer_weights({k: weights[k][l] for k in LAYER_WEIGHTS})
        h, r = _layer_fwd(l, h, p[l], pos, w)
        ws.append(w)
        saved.append(r)
    loss, dh, dg_final = _loss_head("loss_head", h, target, weights["g_final"].reshape(1, -1).astype(f32))
    layer_grads = [None] * DEPTH
    for l in reversed(range(DEPTH)):
        dh, g = _layer_bwd(l, dh, saved[l], pos, ws[l])
        layer_grads[l] = _unpad_layer_grads(g)
    grads = {k: jnp.stack([layer_grads[l][k] for l in range(DEPTH)]) for k in LAYER_WEIGHTS}
    grads["g_final"] = dg_final[0]
    return loss[0, 0], dh, grads


MESH_AXES = ("x", "y", "c")


def _exchange(name, src, axes, scatter):
    n = 2 ** len(axes)
    flips = [tuple((f >> (len(axes) - 1 - b)) & 1 for b in range(len(axes))) for f in range(1, n)]
    rows = src.shape[-2]

    def body(src_ref, out_ref, send_sems, recv_sems, local_sem):
        coords = {a: lax.axis_index(a) for a in MESH_AXES}

        def index_of(cd):
            idx = 0
            for a in axes:
                idx = idx * 2 + cd[a]
            return idx

        me = index_of(coords)
        local = pltpu.make_async_copy(src_ref.at[me] if scatter else src_ref, out_ref.at[me], local_sem)
        local.start()
        copies = []
        for k, f in enumerate(flips):
            peer = dict(coords)
            for a, bit in zip(axes, f):
                if bit:
                    peer[a] = 1 - coords[a]
            cp = pltpu.make_async_remote_copy(
                src_ref=src_ref.at[index_of(peer)] if scatter else src_ref, dst_ref=out_ref.at[me],
                send_sem=send_sems.at[k], recv_sem=recv_sems.at[k],
                device_id=tuple(peer[a] for a in MESH_AXES), device_id_type=pl.DeviceIdType.MESH)
            cp.start()
            copies.append(cp)
        for cp in copies:
            cp.wait()
        local.wait()

    return pl.pallas_call(
        body, name=name, out_shape=jax.ShapeDtypeStruct((n, rows, LANE), src.dtype),
        in_specs=[pl.BlockSpec(memory_space=pl.ANY)], out_specs=pl.BlockSpec(memory_space=pl.ANY),
        scratch_shapes=[pltpu.SemaphoreType.DMA((n - 1,)), pltpu.SemaphoreType.DMA((n - 1,)), pltpu.SemaphoreType.DMA])(src)


def _row_tile(rows, cap):
    if rows <= cap:
        return rows
    for t in range(cap, SUBLANE - 1, -SUBLANE):
        if rows % t == 0:
            return t
    return rows


def _sum_slabs(name, a):
    n, rows, _ = a.shape
    tr = _row_tile(rows, 512)

    def kern(a_ref, o_ref):
        acc = a_ref[0]
        for k in range(1, n):
            acc = acc + a_ref[k]
        o_ref[...] = acc

    return pl.pallas_call(
        kern, name=name, grid=(rows // tr,), in_specs=[pl.BlockSpec((n, tr, LANE), lambda i: (0, i, 0))],
        out_specs=pl.BlockSpec((tr, LANE), lambda i: (i, 0)), out_shape=jax.ShapeDtypeStruct((rows, LANE), f32),
        compiler_params=pltpu.CompilerParams(dimension_semantics=("parallel",)))(a)


def _adamw(name, w, g, m, v):
    rows = w.shape[0]
    tr = _row_tile(rows, 1024)

    def kern(w_ref, g_ref, m_ref, v_ref, d_ref, nm_ref, nv_ref):
        gv = g_ref[...]
        nm = ADAM_B1 * m_ref[...] + (1.0 - ADAM_B1) * gv
        nv = ADAM_B2 * v_ref[...] + (1.0 - ADAM_B2) * (gv * gv)
        m_hat = nm / (1.0 - ADAM_B1 ** ADAM_STEP)
        v_hat = nv / (1.0 - ADAM_B2 ** ADAM_STEP)
        d_ref[...] = -ADAM_LR * (m_hat / (jnp.sqrt(v_hat) + ADAM_EPS) + ADAM_WD * w_ref[...])
        nm_ref[...] = nm
        nv_ref[...] = nv

    spec = pl.BlockSpec((tr, LANE), lambda i: (i, 0))
    return pl.pallas_call(
        kern, name=name, grid=(rows // tr,), in_specs=[spec] * 4, out_specs=[spec] * 3,
        out_shape=[jax.ShapeDtypeStruct((rows, LANE), f32)] * 3,
        compiler_params=pltpu.CompilerParams(dimension_semantics=("parallel",)))(w, g, m, v)


def _pack(arrays, row_multiple):
    flat = jnp.concatenate([a.reshape(-1) for a in arrays])
    per = LANE * row_multiple
    total = -(-flat.shape[0] // per) * per
    return jnp.pad(flat, (0, total - flat.shape[0])).reshape(-1, LANE)


def _unpack(buf, shapes):
    flat = buf.reshape(-1)
    out, at = [], 0
    for s in shapes:
        size = int(np.prod(s))
        out.append(flat[at:at + size].reshape(s))
        at += size
    return out


SHARD_AXIS = {"w_in": 2, "w_uq": 2, "w_ukv": 2, "lru_conv_w": 2, "w_o": 1, "w_up": 2, "ffn_conv_w": 2, "w_down": 1,
              "w_ple_gate": 1, "w_ple_proj": 2}
SHARDED = [k for k in WEIGHTS if k in SHARD_AXIS]
REPLICATED = [k for k in WEIGHTS if k not in SHARD_AXIS]
ELEMENTWISE_F32 = ("lru_conv_w", "ffn_conv_w")
N_SHARDS = 4
BF16_TILE_ROWS = 16


def _gather_weights(shards):
    parts, shapes = [], []
    for k in SHARDED:
        s = shards[k]
        if k in ELEMENTWISE_F32:
            parts.append(lax.bitcast_convert_type(s, bf16))
            shapes.append(s.shape + (2,))
        else:
            parts.append(s.astype(bf16))
            shapes.append(s.shape)
    got = _exchange("gather_weights", _pack(parts, BF16_TILE_ROWS), ("x", "y"), scatter=False)
    per_shard = [_unpack(got[j], shapes) for j in range(N_SHARDS)]
    full = {}
    for i, k in enumerate(SHARDED):
        pieces = [per_shard[j][i] for j in range(N_SHARDS)]
        if k in ELEMENTWISE_F32:
            pieces = [lax.bitcast_convert_type(pc, f32) for pc in pieces]
        full[k] = jnp.concatenate(pieces, axis=SHARD_AXIS[k])
    return full


def _reduce_sharded_grads(grads, shard_shapes):
    per_shard = [[] for _ in range(N_SHARDS)]
    for k in SHARDED:
        for j, piece in enumerate(jnp.split(grads[k], N_SHARDS, axis=SHARD_AXIS[k])):
            per_shard[j].append(piece)
    halves = jnp.stack([_pack(ps, 2 * SUBLANE) for ps in per_shard])
    rows = halves.shape[1] // 2
    got = _exchange("scatter_grads", halves.reshape(2 * N_SHARDS, rows, LANE), MESH_AXES, scatter=True)
    mine = _sum_slabs("sum_grads", got)
    both = _exchange("swap_halves", mine, ("c",), scatter=False)
    return both.reshape(2 * rows, LANE), shard_shapes


def kernel(x, p, positions, g_mix, w_in, g_qc, w_uq, g_kvc, w_ukv, b_f, lru_conv_w, lru_conv_b, w_r, b_r, w_i, b_i, lru_lambda, g_out, w_o, g_ffn, w_up, ffn_conv_w, ffn_conv_b, w_down, g_ple, w_ple_gate, w_ple_proj, g_final, loss_target, m_g_mix, m_w_in, m_g_qc, m_w_uq, m_g_kvc, m_w_ukv, m_b_f, m_lru_conv_w, m_lru_conv_b, m_w_r, m_b_r, m_w_i, m_b_i, m_lru_lambda, m_g_out, m_w_o, m_g_ffn, m_w_up, m_ffn_conv_w, m_ffn_conv_b, m_w_down, m_g_ple, m_w_ple_gate, m_w_ple_proj, m_g_final, v_g_mix, v_w_in, v_g_qc, v_w_uq, v_g_kvc, v_w_ukv, v_b_f, v_lru_conv_w, v_lru_conv_b, v_w_r, v_b_r, v_w_i, v_b_i, v_lru_lambda, v_g_out, v_w_o, v_g_ffn, v_w_up, v_ffn_conv_w, v_ffn_conv_b, v_w_down, v_g_ple, v_w_ple_gate, v_w_ple_proj, v_g_final):
    given = locals()
    w = {k: given[k] for k in WEIGHTS}
    m = {k: given["m_" + k] for k in WEIGHTS}
    v = {k: given["v_" + k] for k in WEIGHTS}

    full = _gather_weights(w)
    full.update({k: w[k] for k in REPLICATED})
    loss, dx, grads = _local_step(x[0], p[:, 0], positions[0], loss_target[0], full)

    shard_shapes = [w[k].shape for k in SHARDED]
    g_big, _ = _reduce_sharded_grads(grads, shard_shapes)
    w_big, m_big, v_big = (_pack([t[k] for k in SHARDED], 2 * SUBLANE) for t in (w, m, v))
    big = [_unpack(b, shard_shapes) for b in (g_big,) + tuple(_adamw("adamw_sharded", w_big, g_big, m_big, v_big))]

    rep_shapes = [w[k].shape for k in REPLICATED] + [(1,)]
    contrib = _pack([grads[k] for k in REPLICATED] + [loss.reshape(1)], SUBLANE)
    g_rep = _sum_slabs("sum_replicated", _exchange("gather_replicated", contrib, MESH_AXES, scatter=False))
    zero = jnp.zeros((1,), f32)
    w_rep, m_rep, v_rep = (_pack([t[k] for k in REPLICATED] + [zero], SUBLANE) for t in (w, m, v))
    rep = [_unpack(b, rep_shapes) for b in (g_rep,) + tuple(_adamw("adamw_replicated", w_rep, g_rep, m_rep, v_rep))]

    outs = []
    for kind in range(4):
        by_name = dict(zip(SHARDED, big[kind]))
        by_name.update(zip(REPLICATED, rep[kind][:-1]))
        outs.append([by_name[k] for k in WEIGHTS])
    total_loss = rep[0][-1][0]
    return (total_loss, dx.reshape(x.shape), *outs[0], *outs[1], *outs[2], *outs[3])
```

```python
import functools

import numpy as np
import jax
import jax.numpy as jnp
from jax import lax
from jax.experimental import pallas as pl
from jax.experimental.pallas import tpu as pltpu

f32, bf16 = jnp.float32, jnp.bfloat16

D_MODEL = 1024
PLE_DIM = 256
MLA_HEADS, MLA_NOPE, MLA_ROPE, MLA_V = 4, 64, 32, 64
MLA_Q_RANK, MLA_KV_RANK = 192, 128
FOX_HEADS, FOX_HEAD_DIM = 4, 64
LRU_WIDTH, LRU_BLOCKS, LRU_BLOCK, LRU_CONV, LRU_C = 512, 8, 64, 4, 8.0
D_FF, FFN_CONV = 2816, 3
ROPE_THETA = 10000.0
EPS = 1e-6
DEPTH = 2
ADAM_LR, ADAM_B1, ADAM_B2, ADAM_EPS, ADAM_WD, ADAM_STEP = 0.001, 0.9, 0.999, 1e-08, 0.01, 10

LANE = 128
SUBLANE = 8
HEADS = 4

Z_FQ, Z_FK, Z_FV, Z_LX, Z_LG, Z_QC, Z_KVC, Z_KR, Z_FL, Z_W = 0, 512, 1024, 1536, 2048, 2560, 2816, 2944, 3072, 3200
QC_W = 256
ROPE_AT = 64


def _head_pad_map(n_heads, width):
    m = -np.ones(n_heads * LANE, np.int64)
    for h in range(n_heads):
        m[h * LANE:h * LANE + width] = h * width + np.arange(width)
    return m


def _z_map():
    m = -np.ones(Z_W, np.int64)
    o_qc, o_kvc, o_kr = 0, MLA_Q_RANK, MLA_Q_RANK + MLA_KV_RANK
    o_fq = o_kr + MLA_ROPE
    o_fk, o_fv = o_fq + 256, o_fq + 512
    o_fl = o_fv + 256
    o_lx = o_fl + FOX_HEADS
    o_lg = o_lx + LRU_WIDTH
    m[Z_FQ:Z_FQ + 512] = np.where(_head_pad_map(4, 64) >= 0, _head_pad_map(4, 64) + o_fq, -1)
    m[Z_FK:Z_FK + 512] = np.where(_head_pad_map(4, 64) >= 0, _head_pad_map(4, 64) + o_fk, -1)
    m[Z_FV:Z_FV + 512] = np.where(_head_pad_map(4, 64) >= 0, _head_pad_map(4, 64) + o_fv, -1)
    m[Z_LX:Z_LX + 512] = o_lx + np.arange(512)
    m[Z_LG:Z_LG + 512] = o_lg + np.arange(512)
    m[Z_QC:Z_QC + MLA_Q_RANK] = o_qc + np.arange(MLA_Q_RANK)
    m[Z_KVC:Z_KVC + MLA_KV_RANK] = o_kvc + np.arange(MLA_KV_RANK)
    m[Z_KR + ROPE_AT:Z_KR + ROPE_AT + MLA_ROPE] = o_kr + np.arange(MLA_ROPE)
    m[Z_FL:Z_FL + FOX_HEADS] = o_fl + np.arange(FOX_HEADS)
    return m


def _ukv_map():
    m = -np.ones(2 * HEADS * LANE, np.int64)
    for h in range(HEADS):
        m[h * LANE:h * LANE + MLA_NOPE] = h * (MLA_NOPE + MLA_V) + np.arange(MLA_NOPE)
        m[HEADS * LANE + h * LANE:HEADS * LANE + h * LANE + MLA_V] = h * (MLA_NOPE + MLA_V) + MLA_NOPE + np.arange(MLA_V)
    return m


def _omix_map():
    return np.concatenate([_head_pad_map(4, 64), np.where(_head_pad_map(4, 64) >= 0, _head_pad_map(4, 64) + 256, -1),
                           512 + np.arange(512)])


def _pad_to(m, n):
    return np.concatenate([m, -np.ones(n - m.shape[0], np.int64)])


def _take_pad(a, m, axis):
    out = jnp.take(a, jnp.asarray(np.maximum(m, 0), jnp.int32), axis=axis)
    shape = [1] * a.ndim
    shape[axis] = m.shape[0]
    return out * jnp.asarray((m >= 0).reshape(shape), a.dtype)


def _take_inv(a, m, axis):
    n = int(m.max()) + 1
    inv = np.zeros(n, np.int64)
    inv[m[m >= 0]] = np.nonzero(m >= 0)[0]
    return jnp.take(a, jnp.asarray(inv, jnp.int32), axis=axis)


Z_MAP = _z_map()
UQ_COL_MAP = _head_pad_map(HEADS, MLA_NOPE + MLA_ROPE)
UQ_ROW_MAP = _pad_to(np.arange(MLA_Q_RANK), QC_W)
UKV_MAP = _ukv_map()
OMIX_MAP = _omix_map()
OMIX_W = 1536


def _rope_tables(width, at):
    half = MLA_ROPE // 2
    inv = ROPE_THETA ** (-np.arange(half, dtype=np.float32) / half)
    freq = np.zeros((1, width), np.float32)
    m1 = np.zeros((1, width), np.float32)
    m2 = np.zeros((1, width), np.float32)
    for h in range(width // LANE):
        b = h * LANE + at
        freq[0, b:b + half] = inv
        freq[0, b + half:b + 2 * half] = inv
        m1[0, b:b + half] = 1.0
        m2[0, b + half:b + 2 * half] = 1.0
    return freq, m1, m2


def _view(r):
    return r if isinstance(r, tuple) else (r, r.shape[1], 0)


def _blk(dim, cap):
    if dim <= cap:
        return dim
    for b in range(cap, LANE - 1, -LANE):
        if dim % b == 0:
            return b
    return dim


@functools.partial(jax.custom_vjp, nondiff_argnums=(1, 2))
def _roll(x, shift, axis):
    return pltpu.roll(x, shift, axis)


def _roll_fwd(x, shift, axis):
    return pltpu.roll(x, shift, axis), None


def _roll_bwd(shift, axis, _, g):
    return (pltpu.roll(g, g.shape[axis] - shift, axis),)


_roll.defvjp(_roll_fwd, _roll_bwd)


def _rowwise(name, fn, rows, pars, outs, tb=256):
    rows = [_view(r) for r in rows]
    n = rows[0][0].shape[0]
    tb = min(tb, n)
    nr, npar = len(rows), len(pars)

    def kern(*refs):
        r = [refs[k][...].astype(f32) for k in range(nr)]
        p = [refs[nr + k][...] for k in range(npar)]
        res = fn(*r, *p)
        for o_ref, o in zip(refs[nr + npar:], res):
            o_ref[...] = o.astype(o_ref.dtype)

    in_specs = [pl.BlockSpec((tb, w), lambda i, j=idx: (i, j)) for (_, w, idx) in rows]
    in_specs += [pl.BlockSpec(p.shape, lambda i: (0, 0)) for p in pars]
    out_specs = [pl.BlockSpec((tb, w), lambda i: (i, 0)) for (w, _) in outs]
    out_shape = [jax.ShapeDtypeStruct((n, w), dt) for (w, dt) in outs]
    return pl.pallas_call(kern, name=name, grid=(n // tb,), in_specs=in_specs, out_specs=out_specs, out_shape=out_shape,
                          compiler_params=pltpu.CompilerParams(dimension_semantics=("parallel",)))(*[r[0] for r in rows], *pars)


def _rowwise_bwd(name, fn, rows, pars, cts, ndiff, adds=None, tb=256, dts=None):
    rows = [_view(r) for r in rows]
    dts = dts or [f32] * ndiff
    adds = adds or {}
    add_keys = sorted(adds)
    n = rows[0][0].shape[0]
    tb = min(tb, n)
    nr, npar, nct, nadd = len(rows), len(pars), len(cts), len(add_keys)

    def kern(*refs):
        i = pl.program_id(0)
        r = [refs[k][...].astype(f32) for k in range(nr)]
        p = [refs[nr + k][...] for k in range(npar)]
        ct = [refs[nr + npar + k][...].astype(f32) for k in range(nct)]
        ad = {key: refs[nr + npar + nct + k][...] for k, key in enumerate(add_keys)}
        o_refs = refs[nr + npar + nct + nadd:]

        def g(*d):
            return tuple(fn(*d[:ndiff], *r[ndiff:], *d[ndiff:]))

        _, vjp = jax.vjp(g, *r[:ndiff], *p)
        grads = vjp(tuple(ct))
        for k in range(ndiff):
            gk = grads[k]
            if k in ad:
                gk = gk + ad[k]
            o_refs[k][...] = gk.astype(o_refs[k].dtype)

        @pl.when(i == 0)
        def _():
            for k in range(npar):
                o_refs[ndiff + k][...] = jnp.zeros_like(o_refs[ndiff + k])

        for k in range(npar):
            o_refs[ndiff + k][...] += grads[ndiff + k]

    in_specs = [pl.BlockSpec((tb, w), lambda i, j=idx: (i, j)) for (_, w, idx) in rows]
    in_specs += [pl.BlockSpec(p.shape, lambda i: (0, 0)) for p in pars]
    in_specs += [pl.BlockSpec((tb, c.shape[1]), lambda i: (i, 0)) for c in cts]
    in_specs += [pl.BlockSpec((tb, adds[k].shape[1]), lambda i: (i, 0)) for k in add_keys]
    out_specs = [pl.BlockSpec((tb, rows[k][1]), lambda i: (i, 0)) for k in range(ndiff)]
    out_specs += [pl.BlockSpec(p.shape, lambda i: (0, 0)) for p in pars]
    out_shape = [jax.ShapeDtypeStruct((n, rows[k][1]), dts[k]) for k in range(ndiff)]
    out_shape += [jax.ShapeDtypeStruct(p.shape, f32) for p in pars]
    res = pl.pallas_call(kern, name=name, grid=(n // tb,), in_specs=in_specs, out_specs=out_specs, out_shape=out_shape,
                         compiler_params=pltpu.CompilerParams(dimension_semantics=("arbitrary",)))(
        *[r[0] for r in rows], *pars, *cts, *[adds[k] for k in add_keys])
    return res[:ndiff], res[ndiff:]


_DOT_DIMS = {"nn": ((1,), (0,)), "nt": ((1,), (1,)), "tn": ((0,), (0,))}

MM_VMEM_BUDGET = 36 * 2 ** 20
MM_MAX_TM = 1024
MM_STEP, MM_RESULT, MM_XPOSE, MM_CAST = 700.0, 7.5e-4, 9e-4, 1e-3


def _tile_candidates(dim):
    c = [d for d in range(LANE, dim + 1, LANE) if dim % d == 0]
    return c or [dim]


@functools.lru_cache(maxsize=None)
def _mm_tiles(mode, m, n, k, a_bytes, b_bytes, o_bytes):
    best, best_cost = None, None
    for tm in _tile_candidates(m):
        if tm > MM_MAX_TM:
            continue
        for tn in _tile_candidates(n):
            for tk in _tile_candidates(k):
                vmem = 2 * (tm * tk * a_bytes + tk * tn * b_bytes + tm * tn * o_bytes) + 4 * tm * tn * (2 if tk < k else 1)
                vmem += (2 * tm * tk if a_bytes > 2 else 0) + (2 * tk * tn if b_bytes > 2 else 0)
                if vmem > MM_VMEM_BUDGET:
                    continue
                steps = (m // tm) * (n // tn) * (k // tk)
                cost = steps * MM_STEP + m * n * (k // tk) * MM_RESULT
                if mode == "tn":
                    cost += m * k * (n // tn) * MM_XPOSE
                cost += (m * k * (n // tn) * MM_CAST if a_bytes > 2 else 0) + (k * n * (m // tm) * MM_CAST if b_bytes > 2 else 0)
                if best is None or cost < best_cost:
                    best, best_cost = (tm, tn, tk), cost
    return best


def _mm(name, a, b, mode="nn", out_dtype=f32, res=None):
    if mode == "nn":
        (m, k), (_, n) = a.shape, b.shape
    elif mode == "nt":
        (m, k), (n, _) = a.shape, b.shape
    else:
        (k, m), (_, n) = a.shape, b.shape
    has_res = res is not None
    tm, tn, tk = _mm_tiles(mode, m, n, k, a.dtype.itemsize, b.dtype.itemsize,
                           jnp.dtype(out_dtype).itemsize + (res.dtype.itemsize if has_res else 0))
    nk = k // tk
    dims = (_DOT_DIMS[mode], ((), ()))

    def kern(*refs):
        a_ref, b_ref = refs[0], refs[1]
        o_ref, acc_ref = refs[-2], refs[-1]
        kk = pl.program_id(2)
        part = lax.dot_general(a_ref[...].astype(bf16), b_ref[...].astype(bf16), dims, preferred_element_type=f32)

        def finish(out):
            if has_res:
                out = out + refs[2][...]
            o_ref[...] = out.astype(o_ref.dtype)

        if nk == 1:
            finish(part)
            return

        @pl.when(kk == 0)
        def _():
            acc_ref[...] = part

        @pl.when(jnp.logical_and(kk > 0, kk < nk - 1))
        def _():
            acc_ref[...] += part

        @pl.when(kk == nk - 1)
        def _():
            finish(acc_ref[...] + part)

    if mode == "tn":
        a_spec = pl.BlockSpec((tk, tm), lambda i, j, kk: (kk, i))
    else:
        a_spec = pl.BlockSpec((tm, tk), lambda i, j, kk: (i, kk))
    if mode == "nt":
        b_spec = pl.BlockSpec((tn, tk), lambda i, j, kk: (j, kk))
    else:
        b_spec = pl.BlockSpec((tk, tn), lambda i, j, kk: (kk, j))
    in_specs = [a_spec, b_spec]
    args = [a, b]
    if has_res:
        in_specs.append(pl.BlockSpec((tm, tn), lambda i, j, kk: (i, j)))
        args.append(res)
    return pl.pallas_call(
        kern, name=name, grid=(m // tm, n // tn, nk), in_specs=in_specs,
        out_specs=pl.BlockSpec((tm, tn), lambda i, j, kk: (i, j)),
        out_shape=jax.ShapeDtypeStruct((m, n), out_dtype),
        scratch_shapes=[pltpu.VMEM((tm, tn) if nk > 1 else (SUBLANE, LANE), f32)],
        compiler_params=pltpu.CompilerParams(dimension_semantics=("parallel", "parallel", "arbitrary")))(*args)


ATT_T = 512


def _att_tile(s):
    return min(ATT_T, s)


def _scores(qb, kb, scale, cq, ck, diagonal, t):
    s = lax.dot_general(qb, kb, (_DOT_DIMS["nt"], ((), ())), preferred_element_type=f32) * scale
    if cq is not None:
        s = s + cq - ck
    if not diagonal:
        return s
    row = lax.broadcasted_iota(jnp.int32, (t, t), 0)
    col = lax.broadcasted_iota(jnp.int32, (t, t), 1)
    return jnp.where(col <= row, s, -jnp.inf)


def _attn_fwd(name, q, k, v, scale, c_col=None, c_row=None):
    (qa, qo), (ka, ko), (va, vo) = q, k, v
    s_len = qa.shape[0]
    t = _att_tile(s_len)
    nt = s_len // t
    decay = c_col is not None

    def kern(*refs):
        q_ref, k_ref, v_ref = refs[:3]
        o_ref, lse_ref = refs[-2:]
        i = pl.program_id(1)
        qb = q_ref[...].astype(bf16)
        cq = refs[3][...] if decay else None

        def step(j, carry, diagonal):
            m, l, acc = carry
            rows = pl.ds(pl.multiple_of(j * t, t), t)
            kb = k_ref[rows, :].astype(bf16)
            vb = v_ref[rows, :].astype(bf16)
            s = _scores(qb, kb, scale, cq, refs[4][j] if decay else None, diagonal, t)
            m_new = jnp.maximum(m, jnp.max(s, axis=1, keepdims=True))
            alpha = jnp.exp(m - m_new)
            p = jnp.exp(s - m_new)
            l = alpha * l + jnp.sum(p, axis=1, keepdims=True)
            acc = alpha * acc + jnp.dot(p.astype(bf16), vb, preferred_element_type=f32)
            return m_new, l, acc

        init = (jnp.full((t, 1), -jnp.inf, f32), jnp.zeros((t, 1), f32), jnp.zeros((t, LANE), f32))
        m, l, acc = step(i, lax.fori_loop(0, i, lambda j, c: step(j, c, False), init), True)
        o_ref[...] = acc / l
        lse_ref[...] = m + jnp.log(l)

    in_specs = [pl.BlockSpec((t, LANE), lambda h, i: (i, qo + h)),
                pl.BlockSpec((s_len, LANE), lambda h, i: (0, ko + h)),
                pl.BlockSpec((s_len, LANE), lambda h, i: (0, vo + h))]
    args = [qa, ka, va]
    if decay:
        in_specs += [pl.BlockSpec((None, t, 1), lambda h, i: (h, i, 0)),
                     pl.BlockSpec((None, nt, 1, t), lambda h, i: (h, 0, 0, 0))]
        args += [c_col, c_row]
    return pl.pallas_call(
        kern, name=name, grid=(HEADS, nt), in_specs=in_specs,
        out_specs=[pl.BlockSpec((t, LANE), lambda h, i: (i, h)), pl.BlockSpec((None, t, 1), lambda h, i: (h, i, 0))],
        out_shape=[jax.ShapeDtypeStruct((s_len, HEADS * LANE), f32), jax.ShapeDtypeStruct((HEADS, s_len, 1), f32)],
        compiler_params=pltpu.CompilerParams(dimension_semantics=("parallel", "arbitrary")))(*args)


def _attn_dq(name, q, k, v, o, do, lse, scale, c_col=None, c_row=None):
    (qa, qo), (ka, ko), (va, vo) = q, k, v
    s_len = qa.shape[0]
    t = _att_tile(s_len)
    nt = s_len // t
    decay = c_col is not None

    def kern(*refs):
        q_ref, k_ref, v_ref, o_ref, do_ref, lse_ref = refs[:6]
        dq_ref, delta_ref, drow_ref = refs[-3:]
        i = pl.program_id(1)
        qb = q_ref[...].astype(bf16)
        dob = do_ref[...]
        delta = jnp.sum(dob * o_ref[...], axis=1, keepdims=True)
        dob = dob.astype(bf16)
        lse = lse_ref[...]
        cq = refs[6][...] if decay else None

        def step(j, carry, diagonal):
            dq, drow = carry
            rows = pl.ds(pl.multiple_of(j * t, t), t)
            kb = k_ref[rows, :].astype(bf16)
            vb = v_ref[rows, :].astype(bf16)
            s = _scores(qb, kb, scale, cq, refs[7][j] if decay else None, diagonal, t)
            p = jnp.exp(s - lse)
            dp = lax.dot_general(dob, vb, (_DOT_DIMS["nt"], ((), ())), preferred_element_type=f32)
            ds = p * (dp - delta)
            return dq + jnp.dot(ds.astype(bf16), kb, preferred_element_type=f32), drow + jnp.sum(ds, axis=1, keepdims=True)

        init = (jnp.zeros((t, LANE), f32), jnp.zeros((t, 1), f32))
        dq, drow = step(i, lax.fori_loop(0, i, lambda j, c: step(j, c, False), init), True)
        dq_ref[...] = dq * scale
        delta_ref[...] = delta
        drow_ref[...] = drow

    in_specs = [pl.BlockSpec((t, LANE), lambda h, i: (i, qo + h)),
                pl.BlockSpec((s_len, LANE), lambda h, i: (0, ko + h)),
                pl.BlockSpec((s_len, LANE), lambda h, i: (0, vo + h)),
                pl.BlockSpec((t, LANE), lambda h, i: (i, h)),
                pl.BlockSpec((t, LANE), lambda h, i: (i, h)),
                pl.BlockSpec((None, t, 1), lambda h, i: (h, i, 0))]
    args = [qa, ka, va, o, do, lse]
    if decay:
        in_specs += [pl.BlockSpec((None, t, 1), lambda h, i: (h, i, 0)),
                     pl.BlockSpec((None, nt, 1, t), lambda h, i: (h, 0, 0, 0))]
        args += [c_col, c_row]
    col = pl.BlockSpec((None, t, 1), lambda h, i: (h, i, 0))
    return pl.pallas_call(
        kern, name=name, grid=(HEADS, nt), in_specs=in_specs,
        out_specs=[pl.BlockSpec((t, LANE), lambda h, i: (i, h)), col, col],
        out_shape=[jax.ShapeDtypeStruct((s_len, HEADS * LANE), f32), jax.ShapeDtypeStruct((HEADS, s_len, 1), f32),
                   jax.ShapeDtypeStruct((HEADS, s_len, 1), f32)],
        compiler_params=pltpu.CompilerParams(dimension_semantics=("parallel", "arbitrary")))(*args)


def _attn_dkv(name, q, k, v, do, lse, delta, scale, c_col=None, c_row=None):
    (qa, qo), (ka, ko), (va, vo) = q, k, v
    s_len = qa.shape[0]
    t = _att_tile(s_len)
    nt = s_len // t
    decay = c_col is not None

    def kern(*refs):
        q_ref, k_ref, v_ref, do_ref, lse_ref, delta_ref = refs[:6]
        j = pl.program_id(1)
        kb = k_ref[...].astype(bf16)
        vb = v_ref[...].astype(bf16)
        ck = refs[7][...] if decay else None

        def step(i, carry, diagonal):
            dk, dv, dc = carry
            rows = pl.ds(pl.multiple_of(i * t, t), t)
            qb = q_ref[rows, :].astype(bf16)
            dob = do_ref[rows, :].astype(bf16)
            s = _scores(qb, kb, scale, refs[6][rows, :] if decay else None, ck, diagonal, t)
            p = jnp.exp(s - lse_ref[rows, :])
            dv = dv + lax.dot_general(p.astype(bf16), dob, (_DOT_DIMS["tn"], ((), ())), preferred_element_type=f32)
            dp = lax.dot_general(dob, vb, (_DOT_DIMS["nt"], ((), ())), preferred_element_type=f32)
            ds = p * (dp - delta_ref[rows, :])
            dk = dk + lax.dot_general(ds.astype(bf16), qb, (_DOT_DIMS["tn"], ((), ())), preferred_element_type=f32)
            if decay:
                dc = dc - jnp.sum(ds, axis=0, keepdims=True)
            return dk, dv, dc

        init = (jnp.zeros((t, LANE), f32), jnp.zeros((t, LANE), f32), jnp.zeros((1, t), f32))
        dk, dv, dc = lax.fori_loop(j + 1, nt, lambda i, c: step(i, c, False), step(j, init, True))
        if decay:
            dk_ref, dv_ref, dc_ref = refs[-3:]
            dc_ref[...] = dc
        else:
            dk_ref, dv_ref = refs[-2:]
        dk_ref[...] = dk * scale
        dv_ref[...] = dv

    in_specs = [pl.BlockSpec((s_len, LANE), lambda h, j: (0, qo + h)),
                pl.BlockSpec((t, LANE), lambda h, j: (j, ko + h)),
                pl.BlockSpec((t, LANE), lambda h, j: (j, vo + h)),
                pl.BlockSpec((s_len, LANE), lambda h, j: (0, h)),
                pl.BlockSpec((None, s_len, 1), lambda h, j: (h, 0, 0)),
                pl.BlockSpec((None, s_len, 1), lambda h, j: (h, 0, 0))]
    args = [qa, ka, va, do, lse, delta]
    out_specs = [pl.BlockSpec((t, LANE), lambda h, j: (j, h)), pl.BlockSpec((t, LANE), lambda h, j: (j, h))]
    out_shape = [jax.ShapeDtypeStruct((s_len, HEADS * LANE), f32), jax.ShapeDtypeStruct((s_len, HEADS * LANE), f32)]
    if decay:
        in_specs += [pl.BlockSpec((None, s_len, 1), lambda h, j: (h, 0, 0)),
                     pl.BlockSpec((None, None, 1, t), lambda h, j: (h, j, 0, 0))]
        args += [c_col, c_row]
        out_specs.append(pl.BlockSpec((None, None, 1, t), lambda h, j: (h, j, 0, 0)))
        out_shape.append(jax.ShapeDtypeStruct((HEADS, nt, 1, t), f32))
    return pl.pallas_call(
        kern, name=name, grid=(HEADS, nt), in_specs=in_specs, out_specs=out_specs, out_shape=out_shape,
        compiler_params=pltpu.CompilerParams(dimension_semantics=("parallel", "arbitrary")))(*args)


CONV_TS, CONV_CB = 1024, 256


def _conv_fwd(name, x, w, b, taps):
    xa, width, xidx = _view(x)
    s_len = xa.shape[0]
    ts, cb = min(CONV_TS, s_len), CONV_CB
    xo = xidx * width // cb

    def kern(x_ref, halo_ref, w_ref, b_ref, o_ref):
        i = pl.program_id(1)
        xb = x_ref[...]
        halo = jnp.where(i == 0, 0.0, halo_ref[...])
        xx = jnp.concatenate([halo, xb], axis=0)
        out = b_ref[...] + w_ref[taps - 1:taps, :] * xb
        for k in range(taps - 1):
            out = out + w_ref[k:k + 1, :] * pltpu.roll(xx, taps - 1 - k, 0)[SUBLANE:]
        o_ref[...] = out

    return pl.pallas_call(
        kern, name=name, grid=(width // cb, s_len // ts),
        in_specs=[pl.BlockSpec((ts, cb), lambda j, i: (i, xo + j)),
                  pl.BlockSpec((SUBLANE, cb), lambda j, i: (jnp.maximum(i * (ts // SUBLANE) - 1, 0), xo + j)),
                  pl.BlockSpec((taps, cb), lambda j, i: (0, j)),
                  pl.BlockSpec((1, cb), lambda j, i: (0, j))],
        out_specs=pl.BlockSpec((ts, cb), lambda j, i: (i, j)),
        out_shape=jax.ShapeDtypeStruct((s_len, width), f32),
        compiler_params=pltpu.CompilerParams(dimension_semantics=("parallel", "parallel")))(xa, xa, w, b)


def _conv_bwd(name, x, dout, w, taps, dout2=None, dx_dtype=f32):
    xa, width, xidx = _view(x)
    s_len = xa.shape[0]
    ts, cb = min(CONV_TS, s_len), CONV_CB
    xo = xidx * width // cb
    n_i = s_len // ts
    two = dout2 is not None

    def kern(*refs):
        x_ref, halo_ref, w_ref = refs[:3]
        dx_ref, dw_ref, db_ref = refs[-3:]
        i = pl.program_id(1)
        if two:
            d = refs[3][...] + refs[5][...]
            dn = refs[4][...] + refs[6][...]
        else:
            d, dn = refs[3][...], refs[4][...]
        dn = jnp.where(i == n_i - 1, 0.0, dn)
        xb = x_ref[...]
        halo = jnp.where(i == 0, 0.0, halo_ref[...])
        xx = jnp.concatenate([halo, xb], axis=0)
        dd = jnp.concatenate([d, dn], axis=0)

        @pl.when(i == 0)
        def _():
            dw_ref[...] = jnp.zeros_like(dw_ref)
            db_ref[...] = jnp.zeros_like(db_ref)

        dx = w_ref[taps - 1:taps, :] * d
        dw_ref[taps - 1:taps, :] += jnp.sum(d * xb, axis=0, keepdims=True)
        for k in range(taps - 1):
            sh = taps - 1 - k
            dx = dx + w_ref[k:k + 1, :] * pltpu.roll(dd, ts + SUBLANE - sh, 0)[:ts]
            dw_ref[k:k + 1, :] += jnp.sum(d * pltpu.roll(xx, sh, 0)[SUBLANE:], axis=0, keepdims=True)
        dx_ref[...] = dx.astype(dx_ref.dtype)
        db_ref[...] += jnp.sum(d, axis=0, keepdims=True)

    d_spec = pl.BlockSpec((ts, cb), lambda j, i: (i, j))
    dn_spec = pl.BlockSpec((SUBLANE, cb), lambda j, i: (jnp.minimum((i + 1) * (ts // SUBLANE), s_len // SUBLANE - 1), j))
    in_specs = [pl.BlockSpec((ts, cb), lambda j, i: (i, xo + j)),
                pl.BlockSpec((SUBLANE, cb), lambda j, i: (jnp.maximum(i * (ts // SUBLANE) - 1, 0), xo + j)),
                pl.BlockSpec((taps, cb), lambda j, i: (0, j)), d_spec, dn_spec]
    args = [xa, xa, w, dout, dout]
    if two:
        in_specs += [d_spec, dn_spec]
        args += [dout2, dout2]
    return pl.pallas_call(
        kern, name=name, grid=(width // cb, n_i), in_specs=in_specs,
        out_specs=[pl.BlockSpec((ts, cb), lambda j, i: (i, j)), pl.BlockSpec((taps, cb), lambda j, i: (0, j)),
                   pl.BlockSpec((1, cb), lambda j, i: (0, j))],
        out_shape=[jax.ShapeDtypeStruct((s_len, width), dx_dtype), jax.ShapeDtypeStruct((taps, width), f32),
                   jax.ShapeDtypeStruct((1, width), f32)],
        compiler_params=pltpu.CompilerParams(dimension_semantics=("parallel", "arbitrary")))(*args)


def _segment_carries(a_last, h_last, reverse):
    ridx = lax.broadcasted_iota(jnp.int32, (SUBLANE, LANE), 0)

    def pick(m, s):
        return jnp.sum(jnp.where(ridx == s, m, 0.0), axis=0, keepdims=True)

    carry = jnp.zeros((SUBLANE, LANE), f32)
    prev = jnp.zeros((1, LANE), f32)
    order = range(SUBLANE - 2, -1, -1) if reverse else range(1, SUBLANE)
    for s in order:
        src = s + 1 if reverse else s - 1
        prev = pick(a_last, src) * prev + pick(h_last, src)
        carry = jnp.where(ridx == s, prev, carry)
    return carry


def _scan_fwd(name, a, b):
    s_len, width = a.shape
    seg = s_len // SUBLANE

    def kern(a_ref, b_ref, h_ref, ap_ref):
        def p1(t, c):
            h, acc = c
            idx = pl.ds(t, SUBLANE, stride=seg)
            av = a_ref[idx, :]
            h = av * h + b_ref[idx, :]
            acc = av * acc
            h_ref[idx, :] = h
            ap_ref[idx, :] = acc
            return h, acc

        h_last, a_last = lax.fori_loop(0, seg, p1, (jnp.zeros((SUBLANE, LANE), f32), jnp.ones((SUBLANE, LANE), f32)))
        carry = _segment_carries(a_last, h_last, False)

        def p3(t, c):
            idx = pl.ds(t, SUBLANE, stride=seg)
            h_ref[idx, :] = h_ref[idx, :] + ap_ref[idx, :] * carry
            return c

        lax.fori_loop(0, seg, p3, 0)

    spec = pl.BlockSpec((s_len, LANE), lambda j: (0, j))
    return pl.pallas_call(
        kern, name=name, grid=(width // LANE,), in_specs=[spec, spec], out_specs=spec,
        out_shape=jax.ShapeDtypeStruct((s_len, width), f32), scratch_shapes=[pltpu.VMEM((s_len, LANE), f32)],
        compiler_params=pltpu.CompilerParams(dimension_semantics=("parallel",)))(a, b)


def _scan_bwd(name, a_next, h_prev, dh):
    s_len, width = dh.shape
    seg = s_len // SUBLANE

    def kern(an_ref, hp_ref, dh_ref, da_ref, db_ref, ap_ref):
        def p1(tt, c):
            g, acc = c
            idx = pl.ds(seg - 1 - tt, SUBLANE, stride=seg)
            av = an_ref[idx, :]
            g = av * g + dh_ref[idx, :]
            acc = av * acc
            db_ref[idx, :] = g
            ap_ref[idx, :] = acc
            return g, acc

        g_last, a_last = lax.fori_loop(0, seg, p1, (jnp.zeros((SUBLANE, LANE), f32), jnp.ones((SUBLANE, LANE), f32)))
        carry = _segment_carries(a_last, g_last, True)

        def p3(t, c):
            idx = pl.ds(t, SUBLANE, stride=seg)
            g = db_ref[idx, :] + ap_ref[idx, :] * carry
            db_ref[idx, :] = g
            da_ref[idx, :] = g * hp_ref[idx, :]
            return c

        lax.fori_loop(0, seg, p3, 0)

    spec = pl.BlockSpec((s_len, LANE), lambda j: (0, j))
    return pl.pallas_call(
        kern, name=name, grid=(width // LANE,), in_specs=[spec, spec, spec], out_specs=[spec, spec],
        out_shape=[jax.ShapeDtypeStruct((s_len, width), f32)] * 2, scratch_shapes=[pltpu.VMEM((s_len, LANE), f32)],
        compiler_params=pltpu.CompilerParams(dimension_semantics=("parallel",)))(a_next, h_prev, dh)


def _lane_cumsum(x, reverse):
    n = x.shape[1]
    lane = lax.broadcasted_iota(jnp.int32, x.shape, 1)
    sh = 1
    while sh < n:
        if reverse:
            x = x + jnp.where(lane < n - sh, pltpu.roll(x, n - sh, 1), 0.0)
        else:
            x = x + jnp.where(lane >= sh, pltpu.roll(x, sh, 1), 0.0)
        sh *= 2
    return x


def _decay_fwd(name, fl_t, b8):
    def kern(f_ref, b_ref, c_ref):
        c_ref[...] = _lane_cumsum(jax.nn.log_sigmoid(f_ref[...] + b_ref[...]), False)

    return pl.pallas_call(kern, name=name, out_shape=jax.ShapeDtypeStruct(fl_t.shape, f32))(fl_t, b8)


def _decay_bwd(name, fl_t, b8, dc_key, dc_query):
    def kern(f_ref, b_ref, dck_ref, dcq_ref, df_ref, db_ref):
        dlogf = _lane_cumsum(dck_ref[...] + dcq_ref[...], True)
        df = dlogf * jax.nn.sigmoid(-(f_ref[...] + b_ref[...]))
        df_ref[...] = df
        db_ref[...] = jnp.sum(df, axis=1, keepdims=True)

    return pl.pallas_call(kern, name=name, out_shape=[jax.ShapeDtypeStruct(fl_t.shape, f32),
                                                      jax.ShapeDtypeStruct((SUBLANE, 1), f32)])(fl_t, b8, dc_key, dc_query)


def _rms(x, g, n):
    return x * lax.rsqrt(jnp.sum(x * x, axis=-1, keepdims=True) * (1.0 / n) + EPS) * g


def _loss_head(name, h, target, g, tb=256):
    n, d = h.shape
    tb = min(tb, n)

    def kern(h_ref, t_ref, g_ref, loss_ref, dh_ref, dg_ref):
        i = pl.program_id(0)
        tgt = t_ref[...]

        def f(hv, gv):
            err = _rms(hv, gv, d) - tgt
            return 0.5 * jnp.sum(jnp.sum(err * err, axis=-1, keepdims=True) * (1.0 / d), axis=0, keepdims=True)

        val, vjp = jax.vjp(f, h_ref[...], g_ref[...])
        dh, dg = vjp(jnp.ones((1, 1), f32))
        dh_ref[...] = dh

        @pl.when(i == 0)
        def _():
            loss_ref[...] = jnp.zeros_like(loss_ref)
            dg_ref[...] = jnp.zeros_like(dg_ref)

        loss_ref[...] += val
        dg_ref[...] += dg

    return pl.pallas_call(
        kern, name=name, grid=(n // tb,),
        in_specs=[pl.BlockSpec((tb, d), lambda i: (i, 0)), pl.BlockSpec((tb, d), lambda i: (i, 0)),
                  pl.BlockSpec((1, d), lambda i: (0, 0))],
        out_specs=[pl.BlockSpec((1, 1), lambda i: (0, 0)), pl.BlockSpec((tb, d), lambda i: (i, 0)),
                   pl.BlockSpec((1, d), lambda i: (0, 0))],
        out_shape=[jax.ShapeDtypeStruct((1, 1), f32), jax.ShapeDtypeStruct((n, d), f32), jax.ShapeDtypeStruct((1, d), f32)],
        compiler_params=pltpu.CompilerParams(dimension_semantics=("arbitrary",)))(h, target, g)


def _f_norm(x, g):
    return (_rms(x, g, D_MODEL),)


def _f_latent(qc, kvc, gq, gkv):
    return _rms(qc, gq, MLA_Q_RANK), _rms(kvc, gkv, MLA_KV_RANK)


def _rope(x, pos, freq, m1, m2):
    ang = pos * freq
    sin = jnp.sin(ang)
    w = x.shape[1]
    return x * jnp.cos(ang) - _roll(x, w - MLA_ROPE // 2, 1) * (sin * m1) + _roll(x, MLA_ROPE // 2, 1) * (sin * m2)


def _f_mla_prep(q, kpart, kr, pos, fq, m1q, m2q, fk, m1k, m2k):
    kr = _rope(kr, pos, fk, m1k, m2k)
    return _rope(q, pos, fq, m1q, m2q), kpart + jnp.concatenate([kr] * HEADS, axis=1)


def _f_lru_gate(gates, xc, b_r, b_i, lam):
    r = jax.nn.sigmoid(gates[:, :LRU_WIDTH] + b_r)
    i = jax.nn.sigmoid(gates[:, LRU_WIDTH:] + b_i)
    log_a = -LRU_C * r * jax.nn.softplus(-lam)
    mult = jnp.sqrt(-jnp.tanh(log_a) * (1.0 + jnp.exp(2.0 * log_a)))
    return jnp.exp(log_a), mult * (i * xc)


def _f_merge(o_mla, o_fox, hs, lg, g):
    o_lru = hs * jax.nn.gelu(lg)
    return (jnp.concatenate([_rms(o_mla, g[:, :512], HEADS * MLA_V), _rms(o_fox, g[:, 512:1024], HEADS * FOX_HEAD_DIM),
                             _rms(o_lru, g[:, 1024:], LRU_WIDTH)], axis=1),)


def _f_ffn_gate(ug, uv):
    return (jax.nn.silu(ug) * uv,)


def _f_ple(h, gpre, pp):
    return (h + jax.nn.sigmoid(gpre) * pp,)


def _prep_layer_weights(w):
    eye = jnp.eye(LRU_BLOCKS, dtype=f32)

    def block_diag(m):
        return (eye[:, None, :, None] * m[:, :, None, :]).reshape(LRU_WIDTH, LRU_WIDTH)

    return dict(
        w_in=_take_pad(w["w_in"], Z_MAP, 1),
        w_uq=_take_pad(_take_pad(w["w_uq"], UQ_COL_MAP, 1), UQ_ROW_MAP, 0),
        w_ukv=_take_pad(w["w_ukv"], UKV_MAP, 1),
        w_ri=jnp.concatenate([block_diag(w["w_r"]), block_diag(w["w_i"])], axis=1).astype(bf16),
        w_o=_take_pad(w["w_o"], OMIX_MAP, 0),
        w_up=w["w_up"], w_down=w["w_down"], w_ple_gate=w["w_ple_gate"], w_ple_proj=w["w_ple_proj"],
        g_mix=w["g_mix"].reshape(1, -1), g_ffn=w["g_ffn"].reshape(1, -1), g_ple=w["g_ple"].reshape(1, -1),
        g_qc=_take_pad(w["g_qc"], UQ_ROW_MAP, 0).reshape(1, -1), g_kvc=w["g_kvc"].reshape(1, -1),
        g_out=_take_pad(w["g_out"], OMIX_MAP, 0).reshape(1, -1),
        b_f8=_take_pad(w["b_f"], _pad_to(np.arange(FOX_HEADS), SUBLANE), 0).reshape(SUBLANE, 1),
        lru_conv_w=w["lru_conv_w"], lru_conv_b=w["lru_conv_b"].reshape(1, -1),
        b_r=w["b_r"].reshape(1, -1), b_i=w["b_i"].reshape(1, -1), lam=w["lru_lambda"].reshape(1, -1),
        ffn_conv_w=w["ffn_conv_w"], ffn_conv_b=w["ffn_conv_b"].reshape(1, -1),
    )


def _rope_consts():
    fq, m1q, m2q = _rope_tables(HEADS * LANE, ROPE_AT)
    fk, m1k, m2k = _rope_tables(LANE, ROPE_AT)
    return [jnp.asarray(t) for t in (fq, m1q, m2q, fk, m1k, m2k)]


def _c_layouts(c_t, s_len):
    t = _att_tile(s_len)
    c4 = c_t[:HEADS]
    return c4.reshape(HEADS, s_len, 1), c4.reshape(HEADS, s_len // t, 1, t)


def _layer_fwd(l, h0, p_l, pos, w):
    s_len = h0.shape[0]
    n = f"l{l}_"
    xn, = _rowwise(n + "norm_mix", _f_norm, [h0], [w["g_mix"]], [(D_MODEL, bf16)])
    z = _mm(n + "in_proj", xn, w["w_in"])
    zq = (z, QC_W, Z_QC // QC_W)
    zkv = (z, LANE, Z_KVC // LANE)
    zkr = (z, LANE, Z_KR // LANE)
    zlx = (z, LRU_WIDTH, Z_LX // LRU_WIDTH)
    zlg = (z, LRU_WIDTH, Z_LG // LRU_WIDTH)
    qcn, kvn = _rowwise(n + "latent_norm", _f_latent, [zq, zkv], [w["g_qc"], w["g_kvc"]], [(QC_W, bf16), (LANE, bf16)])
    q = _mm(n + "uq", qcn, w["w_uq"])
    kv = _mm(n + "ukv", kvn, w["w_ukv"])
    kpart = (kv, HEADS * LANE, 0)
    qr, kk = _rowwise(n + "mla_prep", _f_mla_prep, [q, kpart, zkr, pos], _rope_consts(),
                      [(HEADS * LANE, bf16), (HEADS * LANE, bf16)])
    mla_scale = (MLA_NOPE + MLA_ROPE) ** -0.5
    o_mla, lse_m = _attn_fwd(n + "mla_fwd", (qr, 0), (kk, 0), (kv, HEADS), mla_scale)
    fl_t = z[:, Z_FL:Z_FL + SUBLANE].T
    c_t = _decay_fwd(n + "decay", fl_t, w["b_f8"])
    c_col, c_row = _c_layouts(c_t, s_len)
    fox_scale = FOX_HEAD_DIM ** -0.5
    o_fox, lse_f = _attn_fwd(n + "fox_fwd", (z, Z_FQ // LANE), (z, Z_FK // LANE), (z, Z_FV // LANE), fox_scale, c_col, c_row)
    xc = _conv_fwd(n + "lru_conv", zlx, w["lru_conv_w"], w["lru_conv_b"], LRU_CONV)
    gates = _mm(n + "lru_gates", xc, w["w_ri"])
    a, bx = _rowwise(n + "lru_gate", _f_lru_gate, [gates, xc], [w["b_r"], w["b_i"], w["lam"]],
                     [(LRU_WIDTH, f32), (LRU_WIDTH, f32)])
    hs = _scan_fwd(n + "lru_scan", a, bx)
    ocat, = _rowwise(n + "merge", _f_merge, [o_mla, o_fox, hs, zlg], [w["g_out"]], [(OMIX_W, bf16)])
    h1 = _mm(n + "out_proj", ocat, w["w_o"], res=h0)
    xn2, = _rowwise(n + "norm_ffn", _f_norm, [h1], [w["g_ffn"]], [(D_MODEL, bf16)])
    up = _mm(n + "up_proj", xn2, w["w_up"])
    u = _conv_fwd(n + "ffn_conv", up, w["ffn_conv_w"], w["ffn_conv_b"], FFN_CONV)
    act, = _rowwise(n + "ffn_gate", _f_ffn_gate, [(u, D_FF, 0), (u, D_FF, 1)], [], [(D_FF, bf16)])
    h2 = _mm(n + "down_proj", act, w["w_down"], res=h1)
    hn, = _rowwise(n + "norm_ple", _f_norm, [h2], [w["g_ple"]], [(D_MODEL, bf16)])
    gpre = _mm(n + "ple_gate", hn, w["w_ple_gate"])
    pp = _mm(n + "ple_proj", p_l, w["w_ple_proj"])
    h3, = _rowwise(n + "ple_mix", _f_ple, [h2, gpre, pp], [], [(D_MODEL, f32)])
    res = dict(h0=h0, xn=xn, z=z, qcn=qcn, kvn=kvn, q=q, kv=kv, qr=qr, kk=kk, o_mla=o_mla, lse_m=lse_m, fl_t=fl_t,
               c_col=c_col, c_row=c_row, o_fox=o_fox, lse_f=lse_f, xc=xc, gates=gates, a=a, hs=hs, ocat=ocat, h1=h1,
               xn2=xn2, up=up, u=u, act=act, h2=h2, hn=hn, gpre=gpre, pp=pp, p_l=p_l)
    return h3, res


def _layer_bwd(l, dh3, r, pos, w):
    s_len = dh3.shape[0]
    n = f"l{l}_"
    g = {}
    z = r["z"]
    zq = (z, QC_W, Z_QC // QC_W)
    zkv = (z, LANE, Z_KVC // LANE)
    zkr = (z, LANE, Z_KR // LANE)
    zlx = (z, LRU_WIDTH, Z_LX // LRU_WIDTH)
    zlg = (z, LRU_WIDTH, Z_LG // LRU_WIDTH)
    (dh2a, dgpre, dpp), _ = _rowwise_bwd(n + "ple_mix_b", _f_ple, [r["h2"], r["gpre"], r["pp"]], [], [dh3], 3,
                                         dts=[f32, bf16, bf16])
    g["w_ple_proj"] = _mm(n + "ple_proj_dw", r["p_l"], dpp, "tn")
    dhn = _mm(n + "ple_gate_dx", dgpre, w["w_ple_gate"], "nt")
    g["w_ple_gate"] = _mm(n + "ple_gate_dw", r["hn"], dgpre, "tn")
    (dh2,), (g["g_ple"],) = _rowwise_bwd(n + "norm_ple_b", _f_norm, [r["h2"]], [w["g_ple"]], [dhn], 1, adds={0: dh2a})
    dact = _mm(n + "down_dx", dh2, w["w_down"], "nt")
    g["w_down"] = _mm(n + "down_dw", r["act"], dh2, "tn")
    u = r["u"]
    (dug, duv), _ = _rowwise_bwd(n + "ffn_gate_b", _f_ffn_gate, [(u, D_FF, 0), (u, D_FF, 1)], [], [dact], 2)
    du = jnp.concatenate([dug, duv], axis=1)
    dup, g["ffn_conv_w"], g["ffn_conv_b"] = _conv_bwd(n + "ffn_conv_b", r["up"], du, w["ffn_conv_w"], FFN_CONV,
                                                             dx_dtype=bf16)
    dxn2 = _mm(n + "up_dx", dup, w["w_up"], "nt")
    g["w_up"] = _mm(n + "up_dw", r["xn2"], dup, "tn")
    (dh1,), (g["g_ffn"],) = _rowwise_bwd(n + "norm_ffn_b", _f_norm, [r["h1"]], [w["g_ffn"]], [dxn2], 1, adds={0: dh2})
    docat = _mm(n + "out_dx", dh1, w["w_o"], "nt")
    g["w_o"] = _mm(n + "out_dw", r["ocat"], dh1, "tn")
    (do_mla, do_fox, dhs, dlg), (g["g_out"],) = _rowwise_bwd(
        n + "merge_b", _f_merge, [r["o_mla"], r["o_fox"], r["hs"], zlg], [w["g_out"]], [docat], 4)
    a, hs = r["a"], r["hs"]
    a_next = jnp.concatenate([a[1:], jnp.zeros((1, LRU_WIDTH), f32)], axis=0)
    h_prev = jnp.concatenate([jnp.zeros((1, LRU_WIDTH), f32), hs[:-1]], axis=0)
    da, dbx = _scan_bwd(n + "lru_scan_b", a_next, h_prev, dhs)
    (dgates, dxc_a), (g["b_r"], g["b_i"], g["lam"]) = _rowwise_bwd(
        n + "lru_gate_b", _f_lru_gate, [r["gates"], r["xc"]], [w["b_r"], w["b_i"], w["lam"]], [da, dbx], 2,
        dts=[bf16, f32])
    dxc_b = _mm(n + "lru_gates_dx", dgates, w["w_ri"], "nt")
    g["w_ri"] = _mm(n + "lru_gates_dw", r["xc"], dgates, "tn")
    dlx, g["lru_conv_w"], g["lru_conv_b"] = _conv_bwd(n + "lru_conv_b", zlx, dxc_a, w["lru_conv_w"], LRU_CONV, dout2=dxc_b)
    fox_scale = FOX_HEAD_DIM ** -0.5
    fq, fk, fv = (z, Z_FQ // LANE), (z, Z_FK // LANE), (z, Z_FV // LANE)
    dfq, delta_f, dc_q = _attn_dq(n + "fox_dq", fq, fk, fv, r["o_fox"], do_fox, r["lse_f"], fox_scale, r["c_col"], r["c_row"])
    dfk, dfv, dc_k = _attn_dkv(n + "fox_dkv", fq, fk, fv, do_fox, r["lse_f"], delta_f, fox_scale, r["c_col"], r["c_row"])
    pad_rows = jnp.zeros((SUBLANE - HEADS, s_len), f32)
    dfl_t, g["b_f8"] = _decay_bwd(n + "decay_b", r["fl_t"], w["b_f8"],
                                  jnp.concatenate([dc_k.reshape(HEADS, s_len), pad_rows], axis=0),
                                  jnp.concatenate([dc_q.reshape(HEADS, s_len), pad_rows], axis=0))
    dfl = jnp.pad(dfl_t.T, ((0, 0), (0, LANE - SUBLANE)))
    mla_scale = (MLA_NOPE + MLA_ROPE) ** -0.5
    qr, kk, kv = (r["qr"], 0), (r["kk"], 0), (r["kv"], HEADS)
    dqr, delta_m, _ = _attn_dq(n + "mla_dq", qr, kk, kv, r["o_mla"], do_mla, r["lse_m"], mla_scale)
    dkk, dv_m = _attn_dkv(n + "mla_dkv", qr, kk, kv, do_mla, r["lse_m"], delta_m, mla_scale)
    (dq, dkpart, dkr), _ = _rowwise_bwd(n + "mla_prep_b", _f_mla_prep, [r["q"], (r["kv"], HEADS * LANE, 0), zkr, pos],
                                        _rope_consts(), [dqr, dkk], 3, dts=[bf16, bf16, f32])
    dkv = jnp.concatenate([dkpart, dv_m.astype(bf16)], axis=1)
    dkvn = _mm(n + "ukv_dx", dkv, w["w_ukv"], "nt")
    g["w_ukv"] = _mm(n + "ukv_dw", r["kvn"], dkv, "tn")
    dqcn = _mm(n + "uq_dx", dq, w["w_uq"], "nt")
    g["w_uq"] = _mm(n + "uq_dw", r["qcn"], dq, "tn")
    (dqc, dkvc), (g["g_qc"], g["g_kvc"]) = _rowwise_bwd(n + "latent_norm_b", _f_latent, [zq, zkv],
                                                        [w["g_qc"], w["g_kvc"]], [dqcn, dkvn], 2)
    dz = jnp.concatenate([t.astype(bf16) for t in (dfq, dfk, dfv, dlx, dlg, dqc, dkvc, dkr, dfl)], axis=1)
    dxn = _mm(n + "in_dx", dz, w["w_in"], "nt")
    g["w_in"] = _mm(n + "in_dw", r["xn"], dz, "tn")
    (dh0,), (g["g_mix"],) = _rowwise_bwd(n + "norm_mix_b", _f_norm, [r["h0"]], [w["g_mix"]], [dxn], 1, adds={0: dh1})
    return dh0, g


def _unpad_layer_grads(g):
    d_ri = g["w_ri"]
    idx = jnp.arange(LRU_BLOCKS)

    def diag_blocks(m):
        return m.reshape(LRU_BLOCKS, LRU_BLOCK, LRU_BLOCKS, LRU_BLOCK)[idx, :, idx, :]

    return dict(
        g_mix=g["g_mix"][0], w_in=_take_inv(g["w_in"], Z_MAP, 1), g_qc=g["g_qc"][0, :MLA_Q_RANK],
        w_uq=_take_inv(g["w_uq"][:MLA_Q_RANK], UQ_COL_MAP, 1), g_kvc=g["g_kvc"][0],
        w_ukv=_take_inv(g["w_ukv"], UKV_MAP, 1), b_f=g["b_f8"][:FOX_HEADS, 0],
        lru_conv_w=g["lru_conv_w"], lru_conv_b=g["lru_conv_b"][0],
        w_r=diag_blocks(d_ri[:, :LRU_WIDTH]), b_r=g["b_r"][0], w_i=diag_blocks(d_ri[:, LRU_WIDTH:]), b_i=g["b_i"][0],
        lru_lambda=g["lam"][0], g_out=_take_inv(g["g_out"][0], OMIX_MAP, 0), w_o=_take_inv(g["w_o"], OMIX_MAP, 0),
        g_ffn=g["g_ffn"][0], w_up=g["w_up"], ffn_conv_w=g["ffn_conv_w"], ffn_conv_b=g["ffn_conv_b"][0],
        w_down=g["w_down"], g_ple=g["g_ple"][0], w_ple_gate=g["w_ple_gate"], w_ple_proj=g["w_ple_proj"],
    )


LAYER_WEIGHTS = ["g_mix", "w_in", "g_qc", "w_uq", "g_kvc", "w_ukv", "b_f", "lru_conv_w", "lru_conv_b", "w_r", "b_r", "w_i",
                 "b_i", "lru_lambda", "g_out", "w_o", "g_ffn", "w_up", "ffn_conv_w", "ffn_conv_b", "w_down", "g_ple",
                 "w_ple_gate", "w_ple_proj"]
WEIGHTS = LAYER_WEIGHTS + ["g_final"]


def _local_step(x, p, positions, target, weights):
    pos = positions.astype(f32).reshape(-1, 1)
    h = x
    ws, saved = [], []
    for l in range(DEPTH):
        w = _prep_layer_weights({k: weights[k][l] for k in LAYER_WEIGHTS})
        h, r = _layer_fwd(l, h, p[l], pos, w)
        ws.append(w)
        saved.append(r)
    loss, dh, dg_final = _loss_head("loss_head", h, target, weights["g_final"].reshape(1, -1).astype(f32))
    layer_grads = [None] * DEPTH
    for l in reversed(range(DEPTH)):
        dh, g = _layer_bwd(l, dh, saved[l], pos, ws[l])
        layer_grads[l] = _unpad_layer_grads(g)
    grads = {k: jnp.stack([layer_grads[l][k] for l in range(DEPTH)]) for k in LAYER_WEIGHTS}
    grads["g_final"] = dg_final[0]
    return loss[0, 0], dh, grads


MESH_AXES = ("x", "y", "c")


def _exchange(name, src, axes, scatter, pieces=1):
    n = 2 ** len(axes)
    flips = [tuple((f >> (len(axes) - 1 - b)) & 1 for b in range(len(axes))) for f in range(1, n)]
    rows = src.shape[-2]
    piece_rows = rows // pieces
    assert piece_rows * pieces == rows

    def body(src_ref, out_ref, send_sems, recv_sems, local_sem):
        coords = {a: lax.axis_index(a) for a in MESH_AXES}

        def index_of(cd):
            idx = 0
            for a in axes:
                idx = idx * 2 + cd[a]
            return idx

        me = index_of(coords)
        local = pltpu.make_async_copy(src_ref.at[me] if scatter else src_ref, out_ref.at[me], local_sem)
        local.start()
        copies = []
        for k, f in enumerate(flips):
            peer = dict(coords)
            for a, bit in zip(axes, f):
                if bit:
                    peer[a] = 1 - coords[a]
            slab = src_ref.at[index_of(peer)] if scatter else src_ref
            for pc in range(pieces):
                span = pl.ds(pc * piece_rows, piece_rows)
                cp = pltpu.make_async_remote_copy(
                    src_ref=slab.at[span], dst_ref=out_ref.at[me, span],
                    send_sem=send_sems.at[k * pieces + pc], recv_sem=recv_sems.at[k * pieces + pc],
                    device_id=tuple(peer[a] for a in MESH_AXES), device_id_type=pl.DeviceIdType.MESH)
                cp.start()
                copies.append(cp)
        for cp in copies:
            cp.wait()
        local.wait()

    n_sems = (n - 1) * pieces
    return pl.pallas_call(
        body, name=name, out_shape=jax.ShapeDtypeStruct((n, rows, LANE), src.dtype),
        in_specs=[pl.BlockSpec(memory_space=pl.ANY)], out_specs=pl.BlockSpec(memory_space=pl.ANY),
        scratch_shapes=[pltpu.SemaphoreType.DMA((n_sems,)), pltpu.SemaphoreType.DMA((n_sems,)), pltpu.SemaphoreType.DMA])(src)


def _row_tile(rows, cap):
    if rows <= cap:
        return rows
    for t in range(cap, SUBLANE - 1, -SUBLANE):
        if rows % t == 0:
            return t
    return rows


def _sum_slabs(name, a):
    n, rows, _ = a.shape
    tr = _row_tile(rows, 512)

    def kern(a_ref, o_ref):
        acc = a_ref[0].astype(f32)
        for k in range(1, n):
            acc = acc + a_ref[k].astype(f32)
        o_ref[...] = acc

    return pl.pallas_call(
        kern, name=name, grid=(rows // tr,), in_specs=[pl.BlockSpec((n, tr, LANE), lambda i: (0, i, 0))],
        out_specs=pl.BlockSpec((tr, LANE), lambda i: (i, 0)), out_shape=jax.ShapeDtypeStruct((rows, LANE), f32),
        compiler_params=pltpu.CompilerParams(dimension_semantics=("parallel",)))(a)


def _adamw(name, w, g, m, v):
    rows = w.shape[0]
    tr = _row_tile(rows, 1024)

    def kern(w_ref, g_ref, m_ref, v_ref, d_ref, nm_ref, nv_ref):
        gv = g_ref[...]
        nm = ADAM_B1 * m_ref[...] + (1.0 - ADAM_B1) * gv
        nv = ADAM_B2 * v_ref[...] + (1.0 - ADAM_B2) * (gv * gv)
        m_hat = nm / (1.0 - ADAM_B1 ** ADAM_STEP)
        v_hat = nv / (1.0 - ADAM_B2 ** ADAM_STEP)
        d_ref[...] = -ADAM_LR * (m_hat / (jnp.sqrt(v_hat) + ADAM_EPS) + ADAM_WD * w_ref[...])
        nm_ref[...] = nm
        nv_ref[...] = nv

    spec = pl.BlockSpec((tr, LANE), lambda i: (i, 0))
    return pl.pallas_call(
        kern, name=name, grid=(rows // tr,), in_specs=[spec] * 4, out_specs=[spec] * 3,
        out_shape=[jax.ShapeDtypeStruct((rows, LANE), f32)] * 3,
        compiler_params=pltpu.CompilerParams(dimension_semantics=("parallel",)))(w, g, m, v)


def _pack(arrays, row_multiple):
    flat = jnp.concatenate([a.reshape(-1) for a in arrays])
    per = LANE * row_multiple
    total = -(-flat.shape[0] // per) * per
    return jnp.pad(flat, (0, total - flat.shape[0])).reshape(-1, LANE)


def _unpack(buf, shapes):
    flat = buf.reshape(-1)
    out, at = [], 0
    for s in shapes:
        size = int(np.prod(s))
        out.append(flat[at:at + size].reshape(s))
        at += size
    return out


SHARD_AXIS = {"w_in": 2, "w_uq": 2, "w_ukv": 2, "lru_conv_w": 2, "w_o": 1, "w_up": 2, "ffn_conv_w": 2, "w_down": 1,
              "w_ple_gate": 1, "w_ple_proj": 2}
SHARDED = [k for k in WEIGHTS if k in SHARD_AXIS]
REPLICATED = [k for k in WEIGHTS if k not in SHARD_AXIS]
ELEMENTWISE_F32 = ("lru_conv_w", "ffn_conv_w")
N_SHARDS = 4
BF16_TILE_ROWS = 16


def _gather_weights(shards):
    parts, shapes = [], []
    for k in SHARDED:
        s = shards[k]
        if k in ELEMENTWISE_F32:
            parts.append(lax.bitcast_convert_type(s, bf16))
            shapes.append(s.shape + (2,))
        else:
            parts.append(s.astype(bf16))
            shapes.append(s.shape)
    got = _exchange("gather_weights", _pack(parts, BF16_TILE_ROWS), ("x", "y"), scatter=False)
    per_shard = [_unpack(got[j], shapes) for j in range(N_SHARDS)]
    full = {}
    for i, k in enumerate(SHARDED):
        pieces = [per_shard[j][i] for j in range(N_SHARDS)]
        if k in ELEMENTWISE_F32:
            pieces = [lax.bitcast_convert_type(pc, f32) for pc in pieces]
        full[k] = jnp.concatenate(pieces, axis=SHARD_AXIS[k])
    return full


def _reduce_sharded_grads(grads, shard_shapes):
    per_shard = [[] for _ in range(N_SHARDS)]
    for k in SHARDED:
        for j, piece in enumerate(jnp.split(grads[k], N_SHARDS, axis=SHARD_AXIS[k])):
            per_shard[j].append(piece)
    halves = jnp.stack([_pack([t.astype(bf16) for t in ps], 2 * BF16_TILE_ROWS) for ps in per_shard])
    rows = halves.shape[1] // 2
    got = _exchange("scatter_grads", halves.reshape(2 * N_SHARDS, rows, LANE), MESH_AXES, scatter=True)
    mine = _sum_slabs("sum_grads", got)
    both = _exchange("swap_halves", mine, ("c",), scatter=False, pieces=8)
    return both.reshape(2 * rows, LANE), shard_shapes


def kernel(x, p, positions, g_mix, w_in, g_qc, w_uq, g_kvc, w_ukv, b_f, lru_conv_w, lru_conv_b, w_r, b_r, w_i, b_i, lru_lambda, g_out, w_o, g_ffn, w_up, ffn_conv_w, ffn_conv_b, w_down, g_ple, w_ple_gate, w_ple_proj, g_final, loss_target, m_g_mix, m_w_in, m_g_qc, m_w_uq, m_g_kvc, m_w_ukv, m_b_f, m_lru_conv_w, m_lru_conv_b, m_w_r, m_b_r, m_w_i, m_b_i, m_lru_lambda, m_g_out, m_w_o, m_g_ffn, m_w_up, m_ffn_conv_w, m_ffn_conv_b, m_w_down, m_g_ple, m_w_ple_gate, m_w_ple_proj, m_g_final, v_g_mix, v_w_in, v_g_qc, v_w_uq, v_g_kvc, v_w_ukv, v_b_f, v_lru_conv_w, v_lru_conv_b, v_w_r, v_b_r, v_w_i, v_b_i, v_lru_lambda, v_g_out, v_w_o, v_g_ffn, v_w_up, v_ffn_conv_w, v_ffn_conv_b, v_w_down, v_g_ple, v_w_ple_gate, v_w_ple_proj, v_g_final):
    given = locals()
    w = {k: given[k] for k in WEIGHTS}
    m = {k: given["m_" + k] for k in WEIGHTS}
    v = {k: given["v_" + k] for k in WEIGHTS}

    full = _gather_weights(w)
    full.update({k: w[k] for k in REPLICATED})
    loss, dx, grads = _local_step(x[0], p[:, 0], positions[0], loss_target[0], full)

    shard_shapes = [w[k].shape for k in SHARDED]
    g_big, _ = _reduce_sharded_grads(grads, shard_shapes)
    w_big, m_big, v_big = (_pack([t[k] for k in SHARDED], 2 * BF16_TILE_ROWS) for t in (w, m, v))
    big = [_unpack(b, shard_shapes) for b in (g_big,) + tuple(_adamw("adamw_sharded", w_big, g_big, m_big, v_big))]

    rep_shapes = [w[k].shape for k in REPLICATED] + [(1,)]
    contrib = _pack([grads[k] for k in REPLICATED] + [loss.reshape(1)], SUBLANE)
    g_rep = _sum_slabs("sum_replicated", _exchange("gather_replicated", contrib, MESH_AXES, scatter=False))
    zero = jnp.zeros((1,), f32)
    w_rep, m_rep, v_rep = (_pack([t[k] for k in REPLICATED] + [zero], SUBLANE) for t in (w, m, v))
    rep = [_unpack(b, rep_shapes) for b in (g_rep,) + tuple(_adamw("adamw_replicated", w_rep, g_rep, m_rep, v_rep))]

    outs = []
    for kind in range(4):
        by_name = dict(zip(SHARDED, big[kind]))
        by_name.update(zip(REPLICATED, rep[kind][:-1]))
        outs.append([by_name[k] for k in WEIGHTS])
    total_loss = rep[0][-1][0]
    return (total_loss, dx.reshape(x.shape), *outs[0], *outs[1], *outs[2], *outs[3])
```

```python
import functools

import numpy as np
import jax
import jax.numpy as jnp
from jax import lax
from jax.experimental import pallas as pl
from jax.experimental.pallas import tpu as pltpu

f32, bf16 = jnp.float32, jnp.bfloat16

D_MODEL = 1024
PLE_DIM = 256
MLA_HEADS, MLA_NOPE, MLA_ROPE, MLA_V = 4, 64, 32, 64
MLA_Q_RANK, MLA_KV_RANK = 192, 128
FOX_HEADS, FOX_HEAD_DIM = 4, 64
LRU_WIDTH, LRU_BLOCKS, LRU_BLOCK, LRU_CONV, LRU_C = 512, 8, 64, 4, 8.0
D_FF, FFN_CONV = 2816, 3
ROPE_THETA = 10000.0
EPS = 1e-6
DEPTH = 2
ADAM_LR, ADAM_B1, ADAM_B2, ADAM_EPS, ADAM_WD, ADAM_STEP = 0.001, 0.9, 0.999, 1e-08, 0.01, 10

LANE = 128
SUBLANE = 8
HEADS = 4

Z_FQ, Z_FK, Z_FV, Z_LX, Z_LG, Z_QC, Z_KVC, Z_KR, Z_FL, Z_W = 0, 512, 1024, 1536, 2048, 2560, 2816, 2944, 3072, 3200
QC_W = 256
ROPE_AT = 64


def _head_pad_map(n_heads, width):
    m = -np.ones(n_heads * LANE, np.int64)
    for h in range(n_heads):
        m[h * LANE:h * LANE + width] = h * width + np.arange(width)
    return m


def _z_map():
    m = -np.ones(Z_W, np.int64)
    o_qc, o_kvc, o_kr = 0, MLA_Q_RANK, MLA_Q_RANK + MLA_KV_RANK
    o_fq = o_kr + MLA_ROPE
    o_fk, o_fv = o_fq + 256, o_fq + 512
    o_fl = o_fv + 256
    o_lx = o_fl + FOX_HEADS
    o_lg = o_lx + LRU_WIDTH
    m[Z_FQ:Z_FQ + 512] = np.where(_head_pad_map(4, 64) >= 0, _head_pad_map(4, 64) + o_fq, -1)
    m[Z_FK:Z_FK + 512] = np.where(_head_pad_map(4, 64) >= 0, _head_pad_map(4, 64) + o_fk, -1)
    m[Z_FV:Z_FV + 512] = np.where(_head_pad_map(4, 64) >= 0, _head_pad_map(4, 64) + o_fv, -1)
    m[Z_LX:Z_LX + 512] = o_lx + np.arange(512)
    m[Z_LG:Z_LG + 512] = o_lg + np.arange(512)
    m[Z_QC:Z_QC + MLA_Q_RANK] = o_qc + np.arange(MLA_Q_RANK)
    m[Z_KVC:Z_KVC + MLA_KV_RANK] = o_kvc + np.arange(MLA_KV_RANK)
    m[Z_KR + ROPE_AT:Z_KR + ROPE_AT + MLA_ROPE] = o_kr + np.arange(MLA_ROPE)
    m[Z_FL:Z_FL + FOX_HEADS] = o_fl + np.arange(FOX_HEADS)
    return m


def _ukv_map():
    m = -np.ones(2 * HEADS * LANE, np.int64)
    for h in range(HEADS):
        m[h * LANE:h * LANE + MLA_NOPE] = h * (MLA_NOPE + MLA_V) + np.arange(MLA_NOPE)
        m[HEADS * LANE + h * LANE:HEADS * LANE + h * LANE + MLA_V] = h * (MLA_NOPE + MLA_V) + MLA_NOPE + np.arange(MLA_V)
    return m


def _omix_map():
    return np.concatenate([_head_pad_map(4, 64), np.where(_head_pad_map(4, 64) >= 0, _head_pad_map(4, 64) + 256, -1),
                           512 + np.arange(512)])


def _pad_to(m, n):
    return np.concatenate([m, -np.ones(n - m.shape[0], np.int64)])


def _take_pad(a, m, axis):
    out = jnp.take(a, jnp.asarray(np.maximum(m, 0), jnp.int32), axis=axis)
    shape = [1] * a.ndim
    shape[axis] = m.shape[0]
    return out * jnp.asarray((m >= 0).reshape(shape), a.dtype)


def _take_inv(a, m, axis):
    n = int(m.max()) + 1
    inv = np.zeros(n, np.int64)
    inv[m[m >= 0]] = np.nonzero(m >= 0)[0]
    return jnp.take(a, jnp.asarray(inv, jnp.int32), axis=axis)


Z_MAP = _z_map()
UQ_COL_MAP = _head_pad_map(HEADS, MLA_NOPE + MLA_ROPE)
UQ_ROW_MAP = _pad_to(np.arange(MLA_Q_RANK), QC_W)
UKV_MAP = _ukv_map()
OMIX_MAP = _omix_map()
OMIX_W = 1536


def _rope_tables(width, at):
    half = MLA_ROPE // 2
    inv = ROPE_THETA ** (-np.arange(half, dtype=np.float32) / half)
    freq = np.zeros((1, width), np.float32)
    m1 = np.zeros((1, width), np.float32)
    m2 = np.zeros((1, width), np.float32)
    for h in range(width // LANE):
        b = h * LANE + at
        freq[0, b:b + half] = inv
        freq[0, b + half:b + 2 * half] = inv
        m1[0, b:b + half] = 1.0
        m2[0, b + half:b + 2 * half] = 1.0
    return freq, m1, m2


def _view(r):
    return r if isinstance(r, tuple) else (r, r.shape[1], 0)


def _blk(dim, cap):
    if dim <= cap:
        return dim
    for b in range(cap, LANE - 1, -LANE):
        if dim % b == 0:
            return b
    return dim


@functools.partial(jax.custom_vjp, nondiff_argnums=(1, 2))
def _roll(x, shift, axis):
    return pltpu.roll(x, shift, axis)


def _roll_fwd(x, shift, axis):
    return pltpu.roll(x, shift, axis), None


def _roll_bwd(shift, axis, _, g):
    return (pltpu.roll(g, g.shape[axis] - shift, axis),)


_roll.defvjp(_roll_fwd, _roll_bwd)


def _rowwise(name, fn, rows, pars, outs, tb=256):
    rows = [_view(r) for r in rows]
    n = rows[0][0].shape[0]
    tb = min(tb, n)
    nr, npar = len(rows), len(pars)

    def kern(*refs):
        r = [refs[k][...].astype(f32) for k in range(nr)]
        p = [refs[nr + k][...] for k in range(npar)]
        res = fn(*r, *p)
        for o_ref, o in zip(refs[nr + npar:], res):
            o_ref[...] = o.astype(o_ref.dtype)

    in_specs = [pl.BlockSpec((tb, w), lambda i, j=idx: (i, j)) for (_, w, idx) in rows]
    in_specs += [pl.BlockSpec(p.shape, lambda i: (0, 0)) for p in pars]
    out_specs = [pl.BlockSpec((tb, w), lambda i: (i, 0)) for (w, _) in outs]
    out_shape = [jax.ShapeDtypeStruct((n, w), dt) for (w, dt) in outs]
    return pl.pallas_call(kern, name=name, grid=(n // tb,), in_specs=in_specs, out_specs=out_specs, out_shape=out_shape,
                          compiler_params=pltpu.CompilerParams(dimension_semantics=("parallel",)))(*[r[0] for r in rows], *pars)


def _rowwise_bwd(name, fn, rows, pars, cts, ndiff, adds=None, tb=256, dts=None):
    rows = [_view(r) for r in rows]
    dts = dts or [f32] * ndiff
    adds = adds or {}
    add_keys = sorted(adds)
    n = rows[0][0].shape[0]
    tb = min(tb, n)
    nr, npar, nct, nadd = len(rows), len(pars), len(cts), len(add_keys)

    def kern(*refs):
        i = pl.program_id(0)
        r = [refs[k][...].astype(f32) for k in range(nr)]
        p = [refs[nr + k][...] for k in range(npar)]
        ct = [refs[nr + npar + k][...].astype(f32) for k in range(nct)]
        ad = {key: refs[nr + npar + nct + k][...] for k, key in enumerate(add_keys)}
        o_refs = refs[nr + npar + nct + nadd:]

        def g(*d):
            return tuple(fn(*d[:ndiff], *r[ndiff:], *d[ndiff:]))

        _, vjp = jax.vjp(g, *r[:ndiff], *p)
        grads = vjp(tuple(ct))
        for k in range(ndiff):
            gk = grads[k]
            if k in ad:
                gk = gk + ad[k]
            o_refs[k][...] = gk.astype(o_refs[k].dtype)

        @pl.when(i == 0)
        def _():
            for k in range(npar):
                o_refs[ndiff + k][...] = jnp.zeros_like(o_refs[ndiff + k])

        for k in range(npar):
            o_refs[ndiff + k][...] += grads[ndiff + k]

    in_specs = [pl.BlockSpec((tb, w), lambda i, j=idx: (i, j)) for (_, w, idx) in rows]
    in_specs += [pl.BlockSpec(p.shape, lambda i: (0, 0)) for p in pars]
    in_specs += [pl.BlockSpec((tb, c.shape[1]), lambda i: (i, 0)) for c in cts]
    in_specs += [pl.BlockSpec((tb, adds[k].shape[1]), lambda i: (i, 0)) for k in add_keys]
    out_specs = [pl.BlockSpec((tb, rows[k][1]), lambda i: (i, 0)) for k in range(ndiff)]
    out_specs += [pl.BlockSpec(p.shape, lambda i: (0, 0)) for p in pars]
    out_shape = [jax.ShapeDtypeStruct((n, rows[k][1]), dts[k]) for k in range(ndiff)]
    out_shape += [jax.ShapeDtypeStruct(p.shape, f32) for p in pars]
    res = pl.pallas_call(kern, name=name, grid=(n // tb,), in_specs=in_specs, out_specs=out_specs, out_shape=out_shape,
                         compiler_params=pltpu.CompilerParams(dimension_semantics=("arbitrary",)))(
        *[r[0] for r in rows], *pars, *cts, *[adds[k] for k in add_keys])
    return res[:ndiff], res[ndiff:]


_DOT_DIMS = {"nn": ((1,), (0,)), "nt": ((1,), (1,)), "tn": ((0,), (0,))}

MM_VMEM_BUDGET = 36 * 2 ** 20
MM_MAX_TM = 1024
MM_STEP, MM_RESULT, MM_XPOSE, MM_CAST = 700.0, 7.5e-4, 9e-4, 1e-3


def _tile_candidates(dim):
    c = [d for d in range(LANE, dim + 1, LANE) if dim % d == 0]
    return c or [dim]


@functools.lru_cache(maxsize=None)
def _mm_tiles(mode, m, n, k, a_bytes, b_bytes, o_bytes):
    best, best_cost = None, None
    for tm in _tile_candidates(m):
        if tm > MM_MAX_TM:
            continue
        for tn in _tile_candidates(n):
            for tk in _tile_candidates(k):
                vmem = 2 * (tm * tk * a_bytes + tk * tn * b_bytes + tm * tn * o_bytes) + 4 * tm * tn * (2 if tk < k else 1)
                vmem += (2 * tm * tk if a_bytes > 2 else 0) + (2 * tk * tn if b_bytes > 2 else 0)
                if vmem > MM_VMEM_BUDGET:
                    continue
                steps = (m // tm) * (n // tn) * (k // tk)
                cost = steps * MM_STEP + m * n * (k // tk) * MM_RESULT
                if mode == "tn":
                    cost += m * k * (n // tn) * MM_XPOSE
                cost += (m * k * (n // tn) * MM_CAST if a_bytes > 2 else 0) + (k * n * (m // tm) * MM_CAST if b_bytes > 2 else 0)
                if best is None or cost < best_cost:
                    best, best_cost = (tm, tn, tk), cost
    return best


def _mm(name, a, b, mode="nn", out_dtype=f32, res=None):
    if mode == "nn":
        (m, k), (_, n) = a.shape, b.shape
    elif mode == "nt":
        (m, k), (n, _) = a.shape, b.shape
    else:
        (k, m), (_, n) = a.shape, b.shape
    has_res = res is not None
    tm, tn, tk = _mm_tiles(mode, m, n, k, a.dtype.itemsize, b.dtype.itemsize,
                           jnp.dtype(out_dtype).itemsize + (res.dtype.itemsize if has_res else 0))
    nk = k // tk
    dims = (_DOT_DIMS[mode], ((), ()))

    def kern(*refs):
        a_ref, b_ref = refs[0], refs[1]
        o_ref, acc_ref = refs[-2], refs[-1]
        kk = pl.program_id(2)
        part = lax.dot_general(a_ref[...].astype(bf16), b_ref[...].astype(bf16), dims, preferred_element_type=f32)

        def finish(out):
            if has_res:
                out = out + refs[2][...]
            o_ref[...] = out.astype(o_ref.dtype)

        if nk == 1:
            finish(part)
            return

        @pl.when(kk == 0)
        def _():
            acc_ref[...] = part

        @pl.when(jnp.logical_and(kk > 0, kk < nk - 1))
        def _():
            acc_ref[...] += part

        @pl.when(kk == nk - 1)
        def _():
            finish(acc_ref[...] + part)

    if mode == "tn":
        a_spec = pl.BlockSpec((tk, tm), lambda i, j, kk: (kk, i))
    else:
        a_spec = pl.BlockSpec((tm, tk), lambda i, j, kk: (i, kk))
    if mode == "nt":
        b_spec = pl.BlockSpec((tn, tk), lambda i, j, kk: (j, kk))
    else:
        b_spec = pl.BlockSpec((tk, tn), lambda i, j, kk: (kk, j))
    in_specs = [a_spec, b_spec]
    args = [a, b]
    if has_res:
        in_specs.append(pl.BlockSpec((tm, tn), lambda i, j, kk: (i, j)))
        args.append(res)
    return pl.pallas_call(
        kern, name=name, grid=(m // tm, n // tn, nk), in_specs=in_specs,
        out_specs=pl.BlockSpec((tm, tn), lambda i, j, kk: (i, j)),
        out_shape=jax.ShapeDtypeStruct((m, n), out_dtype),
        scratch_shapes=[pltpu.VMEM((tm, tn) if nk > 1 else (SUBLANE, LANE), f32)],
        compiler_params=pltpu.CompilerParams(dimension_semantics=("parallel", "parallel", "arbitrary")))(*args)


ATT_T = 512


def _att_tile(s):
    return min(ATT_T, s)


def _scores(qb, kb, scale, cq, ck, diagonal, t):
    s = lax.dot_general(qb, kb, (_DOT_DIMS["nt"], ((), ())), preferred_element_type=f32) * scale
    if cq is not None:
        s = s + cq - ck
    if not diagonal:
        return s
    row = lax.broadcasted_iota(jnp.int32, (t, t), 0)
    col = lax.broadcasted_iota(jnp.int32, (t, t), 1)
    return jnp.where(col <= row, s, -jnp.inf)


def _attn_fwd(name, q, k, v, scale, c_col=None, c_row=None):
    (qa, qo), (ka, ko), (va, vo) = q, k, v
    s_len = qa.shape[0]
    t = _att_tile(s_len)
    nt = s_len // t
    decay = c_col is not None

    def kern(*refs):
        q_ref, k_ref, v_ref = refs[:3]
        o_ref, lse_ref = refs[-2:]
        i = pl.program_id(1)
        qb = q_ref[...].astype(bf16)
        cq = refs[3][...] if decay else None

        def step(j, carry, diagonal):
            m, l, acc = carry
            rows = pl.ds(pl.multiple_of(j * t, t), t)
            kb = k_ref[rows, :].astype(bf16)
            vb = v_ref[rows, :].astype(bf16)
            s = _scores(qb, kb, scale, cq, refs[4][j] if decay else None, diagonal, t)
            m_new = jnp.maximum(m, jnp.max(s, axis=1, keepdims=True))
            alpha = jnp.exp(m - m_new)
            p = jnp.exp(s - m_new)
            l = alpha * l + jnp.sum(p, axis=1, keepdims=True)
            acc = alpha * acc + jnp.dot(p.astype(bf16), vb, preferred_element_type=f32)
            return m_new, l, acc

        init = (jnp.full((t, 1), -jnp.inf, f32), jnp.zeros((t, 1), f32), jnp.zeros((t, LANE), f32))
        m, l, acc = step(i, lax.fori_loop(0, i, lambda j, c: step(j, c, False), init), True)
        o_ref[...] = acc / l
        lse_ref[...] = m + jnp.log(l)

    in_specs = [pl.BlockSpec((t, LANE), lambda h, i: (i, qo + h)),
                pl.BlockSpec((s_len, LANE), lambda h, i: (0, ko + h)),
                pl.BlockSpec((s_len, LANE), lambda h, i: (0, vo + h))]
    args = [qa, ka, va]
    if decay:
        in_specs += [pl.BlockSpec((None, t, 1), lambda h, i: (h, i, 0)),
                     pl.BlockSpec((None, nt, 1, t), lambda h, i: (h, 0, 0, 0))]
        args += [c_col, c_row]
    return pl.pallas_call(
        kern, name=name, grid=(HEADS, nt), in_specs=in_specs,
        out_specs=[pl.BlockSpec((t, LANE), lambda h, i: (i, h)), pl.BlockSpec((None, t, 1), lambda h, i: (h, i, 0))],
        out_shape=[jax.ShapeDtypeStruct((s_len, HEADS * LANE), f32), jax.ShapeDtypeStruct((HEADS, s_len, 1), f32)],
        compiler_params=pltpu.CompilerParams(dimension_semantics=("parallel", "arbitrary")))(*args)


def _attn_dq(name, q, k, v, o, do, lse, scale, c_col=None, c_row=None):
    (qa, qo), (ka, ko), (va, vo) = q, k, v
    s_len = qa.shape[0]
    t = _att_tile(s_len)
    nt = s_len // t
    decay = c_col is not None

    def kern(*refs):
        q_ref, k_ref, v_ref, o_ref, do_ref, lse_ref = refs[:6]
        dq_ref, delta_ref, drow_ref = refs[-3:]
        i = pl.program_id(1)
        qb = q_ref[...].astype(bf16)
        dob = do_ref[...]
        delta = jnp.sum(dob * o_ref[...], axis=1, keepdims=True)
        dob = dob.astype(bf16)
        lse = lse_ref[...]
        cq = refs[6][...] if decay else None

        def step(j, carry, diagonal):
            dq, drow = carry
            rows = pl.ds(pl.multiple_of(j * t, t), t)
            kb = k_ref[rows, :].astype(bf16)
            vb = v_ref[rows, :].astype(bf16)
            s = _scores(qb, kb, scale, cq, refs[7][j] if decay else None, diagonal, t)
            p = jnp.exp(s - lse)
            dp = lax.dot_general(dob, vb, (_DOT_DIMS["nt"], ((), ())), preferred_element_type=f32)
            ds = p * (dp - delta)
            return dq + jnp.dot(ds.astype(bf16), kb, preferred_element_type=f32), drow + jnp.sum(ds, axis=1, keepdims=True)

        init = (jnp.zeros((t, LANE), f32), jnp.zeros((t, 1), f32))
        dq, drow = step(i, lax.fori_loop(0, i, lambda j, c: step(j, c, False), init), True)
        dq_ref[...] = dq * scale
        delta_ref[...] = delta
        drow_ref[...] = drow

    in_specs = [pl.BlockSpec((t, LANE), lambda h, i: (i, qo + h)),
                pl.BlockSpec((s_len, LANE), lambda h, i: (0, ko + h)),
                pl.BlockSpec((s_len, LANE), lambda h, i: (0, vo + h)),
                pl.BlockSpec((t, LANE), lambda h, i: (i, h)),
                pl.BlockSpec((t, LANE), lambda h, i: (i, h)),
                pl.BlockSpec((None, t, 1), lambda h, i: (h, i, 0))]
    args = [qa, ka, va, o, do, lse]
    if decay:
        in_specs += [pl.BlockSpec((None, t, 1), lambda h, i: (h, i, 0)),
                     pl.BlockSpec((None, nt, 1, t), lambda h, i: (h, 0, 0, 0))]
        args += [c_col, c_row]
    col = pl.BlockSpec((None, t, 1), lambda h, i: (h, i, 0))
    return pl.pallas_call(
        kern, name=name, grid=(HEADS, nt), in_specs=in_specs,
        out_specs=[pl.BlockSpec((t, LANE), lambda h, i: (i, h)), col, col],
        out_shape=[jax.ShapeDtypeStruct((s_len, HEADS * LANE), f32), jax.ShapeDtypeStruct((HEADS, s_len, 1), f32),
                   jax.ShapeDtypeStruct((HEADS, s_len, 1), f32)],
        compiler_params=pltpu.CompilerParams(dimension_semantics=("parallel", "arbitrary")))(*args)


def _attn_dkv(name, q, k, v, do, lse, delta, scale, c_col=None, c_row=None):
    (qa, qo), (ka, ko), (va, vo) = q, k, v
    s_len = qa.shape[0]
    t = _att_tile(s_len)
    nt = s_len // t
    decay = c_col is not None

    def kern(*refs):
        q_ref, k_ref, v_ref, do_ref, lse_ref, delta_ref = refs[:6]
        j = pl.program_id(1)
        kb = k_ref[...].astype(bf16)
        vb = v_ref[...].astype(bf16)
        ck = refs[7][...] if decay else None

        def step(i, carry, diagonal):
            dk, dv, dc = carry
            rows = pl.ds(pl.multiple_of(i * t, t), t)
            qb = q_ref[rows, :].astype(bf16)
            dob = do_ref[rows, :].astype(bf16)
            s = _scores(qb, kb, scale, refs[6][rows, :] if decay else None, ck, diagonal, t)
            p = jnp.exp(s - lse_ref[rows, :])
            dv = dv + lax.dot_general(p.astype(bf16), dob, (_DOT_DIMS["tn"], ((), ())), preferred_element_type=f32)
            dp = lax.dot_general(dob, vb, (_DOT_DIMS["nt"], ((), ())), preferred_element_type=f32)
            ds = p * (dp - delta_ref[rows, :])
            dk = dk + lax.dot_general(ds.astype(bf16), qb, (_DOT_DIMS["tn"], ((), ())), preferred_element_type=f32)
            if decay:
                dc = dc - jnp.sum(ds, axis=0, keepdims=True)
            return dk, dv, dc

        init = (jnp.zeros((t, LANE), f32), jnp.zeros((t, LANE), f32), jnp.zeros((1, t), f32))
        dk, dv, dc = lax.fori_loop(j + 1, nt, lambda i, c: step(i, c, False), step(j, init, True))
        if decay:
            dk_ref, dv_ref, dc_ref = refs[-3:]
            dc_ref[...] = dc
        else:
            dk_ref, dv_ref = refs[-2:]
        dk_ref[...] = dk * scale
        dv_ref[...] = dv

    in_specs = [pl.BlockSpec((s_len, LANE), lambda h, j: (0, qo + h)),
                pl.BlockSpec((t, LANE), lambda h, j: (j, ko + h)),
                pl.BlockSpec((t, LANE), lambda h, j: (j, vo + h)),
                pl.BlockSpec((s_len, LANE), lambda h, j: (0, h)),
                pl.BlockSpec((None, s_len, 1), lambda h, j: (h, 0, 0)),
                pl.BlockSpec((None, s_len, 1), lambda h, j: (h, 0, 0))]
    args = [qa, ka, va, do, lse, delta]
    out_specs = [pl.BlockSpec((t, LANE), lambda h, j: (j, h)), pl.BlockSpec((t, LANE), lambda h, j: (j, h))]
    out_shape = [jax.ShapeDtypeStruct((s_len, HEADS * LANE), f32), jax.ShapeDtypeStruct((s_len, HEADS * LANE), f32)]
    if decay:
        in_specs += [pl.BlockSpec((None, s_len, 1), lambda h, j: (h, 0, 0)),
                     pl.BlockSpec((None, None, 1, t), lambda h, j: (h, j, 0, 0))]
        args += [c_col, c_row]
        out_specs.append(pl.BlockSpec((None, None, 1, t), lambda h, j: (h, j, 0, 0)))
        out_shape.append(jax.ShapeDtypeStruct((HEADS, nt, 1, t), f32))
    return pl.pallas_call(
        kern, name=name, grid=(HEADS, nt), in_specs=in_specs, out_specs=out_specs, out_shape=out_shape,
        compiler_params=pltpu.CompilerParams(dimension_semantics=("parallel", "arbitrary")))(*args)


CONV_TS, CONV_CB = 1024, 256


def _conv_fwd(name, x, w, b, taps):
    xa, width, xidx = _view(x)
    s_len = xa.shape[0]
    ts, cb = min(CONV_TS, s_len), CONV_CB
    xo = xidx * width // cb

    def kern(x_ref, halo_ref, w_ref, b_ref, o_ref):
        i = pl.program_id(1)
        xb = x_ref[...]
        halo = jnp.where(i == 0, 0.0, halo_ref[...])
        xx = jnp.concatenate([halo, xb], axis=0)
        out = b_ref[...] + w_ref[taps - 1:taps, :] * xb
        for k in range(taps - 1):
            out = out + w_ref[k:k + 1, :] * pltpu.roll(xx, taps - 1 - k, 0)[SUBLANE:]
        o_ref[...] = out

    return pl.pallas_call(
        kern, name=name, grid=(width // cb, s_len // ts),
        in_specs=[pl.BlockSpec((ts, cb), lambda j, i: (i, xo + j)),
                  pl.BlockSpec((SUBLANE, cb), lambda j, i: (jnp.maximum(i * (ts // SUBLANE) - 1, 0), xo + j)),
                  pl.BlockSpec((taps, cb), lambda j, i: (0, j)),
                  pl.BlockSpec((1, cb), lambda j, i: (0, j))],
        out_specs=pl.BlockSpec((ts, cb), lambda j, i: (i, j)),
        out_shape=jax.ShapeDtypeStruct((s_len, width), f32),
        compiler_params=pltpu.CompilerParams(dimension_semantics=("parallel", "parallel")))(xa, xa, w, b)


def _conv_bwd(name, x, dout, w, taps, dout2=None, dx_dtype=f32):
    xa, width, xidx = _view(x)
    s_len = xa.shape[0]
    ts, cb = min(CONV_TS, s_len), CONV_CB
    xo = xidx * width // cb
    n_i = s_len // ts
    two = dout2 is not None

    def kern(*refs):
        x_ref, halo_ref, w_ref = refs[:3]
        dx_ref, dw_ref, db_ref = refs[-3:]
        i = pl.program_id(1)
        if two:
            d = refs[3][...] + refs[5][...]
            dn = refs[4][...] + refs[6][...]
        else:
            d, dn = refs[3][...], refs[4][...]
        dn = jnp.where(i == n_i - 1, 0.0, dn)
        xb = x_ref[...]
        halo = jnp.where(i == 0, 0.0, halo_ref[...])
        xx = jnp.concatenate([halo, xb], axis=0)
        dd = jnp.concatenate([d, dn], axis=0)

        @pl.when(i == 0)
        def _():
            dw_ref[...] = jnp.zeros_like(dw_ref)
            db_ref[...] = jnp.zeros_like(db_ref)

        dx = w_ref[taps - 1:taps, :] * d
        dw_ref[taps - 1:taps, :] += jnp.sum(d * xb, axis=0, keepdims=True)
        for k in range(taps - 1):
            sh = taps - 1 - k
            dx = dx + w_ref[k:k + 1, :] * pltpu.roll(dd, ts + SUBLANE - sh, 0)[:ts]
            dw_ref[k:k + 1, :] += jnp.sum(d * pltpu.roll(xx, sh, 0)[SUBLANE:], axis=0, keepdims=True)
        dx_ref[...] = dx.astype(dx_ref.dtype)
        db_ref[...] += jnp.sum(d, axis=0, keepdims=True)

    d_spec = pl.BlockSpec((ts, cb), lambda j, i: (i, j))
    dn_spec = pl.BlockSpec((SUBLANE, cb), lambda j, i: (jnp.minimum((i + 1) * (ts // SUBLANE), s_len // SUBLANE - 1), j))
    in_specs = [pl.BlockSpec((ts, cb), lambda j, i: (i, xo + j)),
                pl.BlockSpec((SUBLANE, cb), lambda j, i: (jnp.maximum(i * (ts // SUBLANE) - 1, 0), xo + j)),
                pl.BlockSpec((taps, cb), lambda j, i: (0, j)), d_spec, dn_spec]
    args = [xa, xa, w, dout, dout]
    if two:
        in_specs += [d_spec, dn_spec]
        args += [dout2, dout2]
    return pl.pallas_call(
        kern, name=name, grid=(width // cb, n_i), in_specs=in_specs,
        out_specs=[pl.BlockSpec((ts, cb), lambda j, i: (i, j)), pl.BlockSpec((taps, cb), lambda j, i: (0, j)),
                   pl.BlockSpec((1, cb), lambda j, i: (0, j))],
        out_shape=[jax.ShapeDtypeStruct((s_len, width), dx_dtype), jax.ShapeDtypeStruct((taps, width), f32),
                   jax.ShapeDtypeStruct((1, width), f32)],
        compiler_params=pltpu.CompilerParams(dimension_semantics=("parallel", "arbitrary")))(*args)


def _segment_carries(a_last, h_last, reverse):
    ridx = lax.broadcasted_iota(jnp.int32, (SUBLANE, LANE), 0)

    def pick(m, s):
        return jnp.sum(jnp.where(ridx == s, m, 0.0), axis=0, keepdims=True)

    carry = jnp.zeros((SUBLANE, LANE), f32)
    prev = jnp.zeros((1, LANE), f32)
    order = range(SUBLANE - 2, -1, -1) if reverse else range(1, SUBLANE)
    for s in order:
        src = s + 1 if reverse else s - 1
        prev = pick(a_last, src) * prev + pick(h_last, src)
        carry = jnp.where(ridx == s, prev, carry)
    return carry


def _scan_fwd(name, a, b):
    s_len, width = a.shape
    seg = s_len // SUBLANE

    def kern(a_ref, b_ref, h_ref, ap_ref):
        def p1(t, c):
            h, acc = c
            idx = pl.ds(t, SUBLANE, stride=seg)
            av = a_ref[idx, :]
            h = av * h + b_ref[idx, :]
            acc = av * acc
            h_ref[idx, :] = h
            ap_ref[idx, :] = acc
            return h, acc

        h_last, a_last = lax.fori_loop(0, seg, p1, (jnp.zeros((SUBLANE, LANE), f32), jnp.ones((SUBLANE, LANE), f32)))
        carry = _segment_carries(a_last, h_last, False)

        def p3(t, c):
            idx = pl.ds(t, SUBLANE, stride=seg)
            h_ref[idx, :] = h_ref[idx, :] + ap_ref[idx, :] * carry
            return c

        lax.fori_loop(0, seg, p3, 0)

    spec = pl.BlockSpec((s_len, LANE), lambda j: (0, j))
    return pl.pallas_call(
        kern, name=name, grid=(width // LANE,), in_specs=[spec, spec], out_specs=spec,
        out_shape=jax.ShapeDtypeStruct((s_len, width), f32), scratch_shapes=[pltpu.VMEM((s_len, LANE), f32)],
        compiler_params=pltpu.CompilerParams(dimension_semantics=("parallel",)))(a, b)


def _scan_bwd(name, a_next, h_prev, dh):
    s_len, width = dh.shape
    seg = s_len // SUBLANE

    def kern(an_ref, hp_ref, dh_ref, da_ref, db_ref, ap_ref):
        def p1(tt, c):
            g, acc = c
            idx = pl.ds(seg - 1 - tt, SUBLANE, stride=seg)
            av = an_ref[idx, :]
            g = av * g + dh_ref[idx, :]
            acc = av * acc
            db_ref[idx, :] = g
            ap_ref[idx, :] = acc
            return g, acc

        g_last, a_last = lax.fori_loop(0, seg, p1, (jnp.zeros((SUBLANE, LANE), f32), jnp.ones((SUBLANE, LANE), f32)))
        carry = _segment_carries(a_last, g_last, True)

        def p3(t, c):
            idx = pl.ds(t, SUBLANE, stride=seg)
            g = db_ref[idx, :] + ap_ref[idx, :] * carry
            db_ref[idx, :] = g
            da_ref[idx, :] = g * hp_ref[idx, :]
            return c

        lax.fori_loop(0, seg, p3, 0)

    spec = pl.BlockSpec((s_len, LANE), lambda j: (0, j))
    return pl.pallas_call(
        kern, name=name, grid=(width // LANE,), in_specs=[spec, spec, spec], out_specs=[spec, spec],
        out_shape=[jax.ShapeDtypeStruct((s_len, width), f32)] * 2, scratch_shapes=[pltpu.VMEM((s_len, LANE), f32)],
        compiler_params=pltpu.CompilerParams(dimension_semantics=("parallel",)))(a_next, h_prev, dh)


def _lane_cumsum(x, reverse):
    n = x.shape[1]
    lane = lax.broadcasted_iota(jnp.int32, x.shape, 1)
    sh = 1
    while sh < n:
        if reverse:
            x = x + jnp.where(lane < n - sh, pltpu.roll(x, n - sh, 1), 0.0)
        else:
            x = x + jnp.where(lane >= sh, pltpu.roll(x, sh, 1), 0.0)
        sh *= 2
    return x


def _decay_fwd(name, fl_t, b8):
    def kern(f_ref, b_ref, c_ref):
        c_ref[...] = _lane_cumsum(jax.nn.log_sigmoid(f_ref[...] + b_ref[...]), False)

    return pl.pallas_call(kern, name=name, out_shape=jax.ShapeDtypeStruct(fl_t.shape, f32))(fl_t, b8)


def _decay_bwd(name, fl_t, b8, dc_key, dc_query):
    def kern(f_ref, b_ref, dck_ref, dcq_ref, df_ref, db_ref):
        dlogf = _lane_cumsum(dck_ref[...] + dcq_ref[...], True)
        df = dlogf * jax.nn.sigmoid(-(f_ref[...] + b_ref[...]))
        df_ref[...] = df
        db_ref[...] = jnp.sum(df, axis=1, keepdims=True)

    return pl.pallas_call(kern, name=name, out_shape=[jax.ShapeDtypeStruct(fl_t.shape, f32),
                                                      jax.ShapeDtypeStruct((SUBLANE, 1), f32)])(fl_t, b8, dc_key, dc_query)


def _rms(x, g, n):
    return x * lax.rsqrt(jnp.sum(x * x, axis=-1, keepdims=True) * (1.0 / n) + EPS) * g


def _loss_head(name, h, target, g, tb=256):
    n, d = h.shape
    tb = min(tb, n)

    def kern(h_ref, t_ref, g_ref, loss_ref, dh_ref, dg_ref):
        i = pl.program_id(0)
        tgt = t_ref[...]

        def f(hv, gv):
            err = _rms(hv, gv, d) - tgt
            return 0.5 * jnp.sum(jnp.sum(err * err, axis=-1, keepdims=True) * (1.0 / d), axis=0, keepdims=True)

        val, vjp = jax.vjp(f, h_ref[...], g_ref[...])
        dh, dg = vjp(jnp.ones((1, 1), f32))
        dh_ref[...] = dh

        @pl.when(i == 0)
        def _():
            loss_ref[...] = jnp.zeros_like(loss_ref)
            dg_ref[...] = jnp.zeros_like(dg_ref)

        loss_ref[...] += val
        dg_ref[...] += dg

    return pl.pallas_call(
        kern, name=name, grid=(n // tb,),
        in_specs=[pl.BlockSpec((tb, d), lambda i: (i, 0)), pl.BlockSpec((tb, d), lambda i: (i, 0)),
                  pl.BlockSpec((1, d), lambda i: (0, 0))],
        out_specs=[pl.BlockSpec((1, 1), lambda i: (0, 0)), pl.BlockSpec((tb, d), lambda i: (i, 0)),
                   pl.BlockSpec((1, d), lambda i: (0, 0))],
        out_shape=[jax.ShapeDtypeStruct((1, 1), f32), jax.ShapeDtypeStruct((n, d), f32), jax.ShapeDtypeStruct((1, d), f32)],
        compiler_params=pltpu.CompilerParams(dimension_semantics=("arbitrary",)))(h, target, g)


def _f_norm(x, g):
    return (_rms(x, g, D_MODEL),)


def _f_latent(qc, kvc, gq, gkv):
    return _rms(qc, gq, MLA_Q_RANK), _rms(kvc, gkv, MLA_KV_RANK)


def _rope(x, pos, freq, m1, m2):
    ang = pos * freq
    sin = jnp.sin(ang)
    w = x.shape[1]
    return x * jnp.cos(ang) - _roll(x, w - MLA_ROPE // 2, 1) * (sin * m1) + _roll(x, MLA_ROPE // 2, 1) * (sin * m2)


def _f_mla_prep(q, kpart, kr, pos, fq, m1q, m2q, fk, m1k, m2k):
    kr = _rope(kr, pos, fk, m1k, m2k)
    return _rope(q, pos, fq, m1q, m2q), kpart + jnp.concatenate([kr] * HEADS, axis=1)


def _f_lru_gate(gates, xc, b_r, b_i, lam):
    r = jax.nn.sigmoid(gates[:, :LRU_WIDTH] + b_r)
    i = jax.nn.sigmoid(gates[:, LRU_WIDTH:] + b_i)
    log_a = -LRU_C * r * jax.nn.softplus(-lam)
    mult = jnp.sqrt(-jnp.tanh(log_a) * (1.0 + jnp.exp(2.0 * log_a)))
    return jnp.exp(log_a), mult * (i * xc)


def _f_merge(o_mla, o_fox, hs, lg, g):
    o_lru = hs * jax.nn.gelu(lg)
    return (jnp.concatenate([_rms(o_mla, g[:, :512], HEADS * MLA_V), _rms(o_fox, g[:, 512:1024], HEADS * FOX_HEAD_DIM),
                             _rms(o_lru, g[:, 1024:], LRU_WIDTH)], axis=1),)


def _f_ffn_gate(ug, uv):
    return (jax.nn.silu(ug) * uv,)


def _f_ple(h, gpre, pp):
    return (h + jax.nn.sigmoid(gpre) * pp,)


def _prep_layer_weights(w):
    eye = jnp.eye(LRU_BLOCKS, dtype=f32)

    def block_diag(m):
        return (eye[:, None, :, None] * m[:, :, None, :]).reshape(LRU_WIDTH, LRU_WIDTH)

    return dict(
        w_in=_take_pad(w["w_in"], Z_MAP, 1),
        w_uq=_take_pad(_take_pad(w["w_uq"], UQ_COL_MAP, 1), UQ_ROW_MAP, 0),
        w_ukv=_take_pad(w["w_ukv"], UKV_MAP, 1),
        w_ri=jnp.concatenate([block_diag(w["w_r"]), block_diag(w["w_i"])], axis=1).astype(bf16),
        w_o=_take_pad(w["w_o"], OMIX_MAP, 0),
        w_up=w["w_up"], w_down=w["w_down"], w_ple_gate=w["w_ple_gate"], w_ple_proj=w["w_ple_proj"],
        g_mix=w["g_mix"].reshape(1, -1), g_ffn=w["g_ffn"].reshape(1, -1), g_ple=w["g_ple"].reshape(1, -1),
        g_qc=_take_pad(w["g_qc"], UQ_ROW_MAP, 0).reshape(1, -1), g_kvc=w["g_kvc"].reshape(1, -1),
        g_out=_take_pad(w["g_out"], OMIX_MAP, 0).reshape(1, -1),
        b_f8=_take_pad(w["b_f"], _pad_to(np.arange(FOX_HEADS), SUBLANE), 0).reshape(SUBLANE, 1),
        lru_conv_w=w["lru_conv_w"], lru_conv_b=w["lru_conv_b"].reshape(1, -1),
        b_r=w["b_r"].reshape(1, -1), b_i=w["b_i"].reshape(1, -1), lam=w["lru_lambda"].reshape(1, -1),
        ffn_conv_w=w["ffn_conv_w"], ffn_conv_b=w["ffn_conv_b"].reshape(1, -1),
    )


def _rope_consts():
    fq, m1q, m2q = _rope_tables(HEADS * LANE, ROPE_AT)
    fk, m1k, m2k = _rope_tables(LANE, ROPE_AT)
    return [jnp.asarray(t) for t in (fq, m1q, m2q, fk, m1k, m2k)]


def _c_layouts(c_t, s_len):
    t = _att_tile(s_len)
    c4 = c_t[:HEADS]
    return c4.reshape(HEADS, s_len, 1), c4.reshape(HEADS, s_len // t, 1, t)


def _layer_fwd(l, h0, p_l, pos, w):
    s_len = h0.shape[0]
    n = f"l{l}_"
    xn, = _rowwise(n + "norm_mix", _f_norm, [h0], [w["g_mix"]], [(D_MODEL, bf16)])
    z = _mm(n + "in_proj", xn, w["w_in"])
    zq = (z, QC_W, Z_QC // QC_W)
    zkv = (z, LANE, Z_KVC // LANE)
    zkr = (z, LANE, Z_KR // LANE)
    zlx = (z, LRU_WIDTH, Z_LX // LRU_WIDTH)
    zlg = (z, LRU_WIDTH, Z_LG // LRU_WIDTH)
    qcn, kvn = _rowwise(n + "latent_norm", _f_latent, [zq, zkv], [w["g_qc"], w["g_kvc"]], [(QC_W, bf16), (LANE, bf16)])
    q = _mm(n + "uq", qcn, w["w_uq"])
    kv = _mm(n + "ukv", kvn, w["w_ukv"])
    kpart = (kv, HEADS * LANE, 0)
    qr, kk = _rowwise(n + "mla_prep", _f_mla_prep, [q, kpart, zkr, pos], _rope_consts(),
                      [(HEADS * LANE, bf16), (HEADS * LANE, bf16)])
    mla_scale = (MLA_NOPE + MLA_ROPE) ** -0.5
    o_mla, lse_m = _attn_fwd(n + "mla_fwd", (qr, 0), (kk, 0), (kv, HEADS), mla_scale)
    fl_t = z[:, Z_FL:Z_FL + SUBLANE].T
    c_t = _decay_fwd(n + "decay", fl_t, w["b_f8"])
    c_col, c_row = _c_layouts(c_t, s_len)
    fox_scale = FOX_HEAD_DIM ** -0.5
    o_fox, lse_f = _attn_fwd(n + "fox_fwd", (z, Z_FQ // LANE), (z, Z_FK // LANE), (z, Z_FV // LANE), fox_scale, c_col, c_row)
    xc = _conv_fwd(n + "lru_conv", zlx, w["lru_conv_w"], w["lru_conv_b"], LRU_CONV)
    gates = _mm(n + "lru_gates", xc, w["w_ri"])
    a, bx = _rowwise(n + "lru_gate", _f_lru_gate, [gates, xc], [w["b_r"], w["b_i"], w["lam"]],
                     [(LRU_WIDTH, f32), (LRU_WIDTH, f32)])
    hs = _scan_fwd(n + "lru_scan", a, bx)
    ocat, = _rowwise(n + "merge", _f_merge, [o_mla, o_fox, hs, zlg], [w["g_out"]], [(OMIX_W, bf16)])
    h1 = _mm(n + "out_proj", ocat, w["w_o"], res=h0)
    xn2, = _rowwise(n + "norm_ffn", _f_norm, [h1], [w["g_ffn"]], [(D_MODEL, bf16)])
    up = _mm(n + "up_proj", xn2, w["w_up"])
    u = _conv_fwd(n + "ffn_conv", up, w["ffn_conv_w"], w["ffn_conv_b"], FFN_CONV)
    act, = _rowwise(n + "ffn_gate", _f_ffn_gate, [(u, D_FF, 0), (u, D_FF, 1)], [], [(D_FF, bf16)])
    h2 = _mm(n + "down_proj", act, w["w_down"], res=h1)
    hn, = _rowwise(n + "norm_ple", _f_norm, [h2], [w["g_ple"]], [(D_MODEL, bf16)])
    gpre = _mm(n + "ple_gate", hn, w["w_ple_gate"])
    pp = _mm(n + "ple_proj", p_l, w["w_ple_proj"])
    h3, = _rowwise(n + "ple_mix", _f_ple, [h2, gpre, pp], [], [(D_MODEL, f32)])
    res = dict(h0=h0, xn=xn, z=z, qcn=qcn, kvn=kvn, q=q, kv=kv, qr=qr, kk=kk, o_mla=o_mla, lse_m=lse_m, fl_t=fl_t,
               c_col=c_col, c_row=c_row, o_fox=o_fox, lse_f=lse_f, xc=xc, gates=gates, a=a, hs=hs, ocat=ocat, h1=h1,
               xn2=xn2, up=up, u=u, act=act, h2=h2, hn=hn, gpre=gpre, pp=pp, p_l=p_l)
    return h3, res


def _layer_bwd(l, dh3, r, pos, w):
    s_len = dh3.shape[0]
    n = f"l{l}_"
    g = {}
    z = r["z"]
    zq = (z, QC_W, Z_QC // QC_W)
    zkv = (z, LANE, Z_KVC // LANE)
    zkr = (z, LANE, Z_KR // LANE)
    zlx = (z, LRU_WIDTH, Z_LX // LRU_WIDTH)
    zlg = (z, LRU_WIDTH, Z_LG // LRU_WIDTH)
    (dh2a, dgpre, dpp), _ = _rowwise_bwd(n + "ple_mix_b", _f_ple, [r["h2"], r["gpre"], r["pp"]], [], [dh3], 3,
                                         dts=[f32, bf16, bf16])
    g["w_ple_proj"] = _mm(n + "ple_proj_dw", r["p_l"], dpp, "tn", bf16)
    dhn = _mm(n + "ple_gate_dx", dgpre, w["w_ple_gate"], "nt")
    g["w_ple_gate"] = _mm(n + "ple_gate_dw", r["hn"], dgpre, "tn", bf16)
    (dh2,), (g["g_ple"],) = _rowwise_bwd(n + "norm_ple_b", _f_norm, [r["h2"]], [w["g_ple"]], [dhn], 1, adds={0: dh2a})
    dact = _mm(n + "down_dx", dh2, w["w_down"], "nt")
    g["w_down"] = _mm(n + "down_dw", r["act"], dh2, "tn", bf16)
    u = r["u"]
    (dug, duv), _ = _rowwise_bwd(n + "ffn_gate_b", _f_ffn_gate, [(u, D_FF, 0), (u, D_FF, 1)], [], [dact], 2)
    du = jnp.concatenate([dug, duv], axis=1)
    dup, g["ffn_conv_w"], g["ffn_conv_b"] = _conv_bwd(n + "ffn_conv_b", r["up"], du, w["ffn_conv_w"], FFN_CONV,
                                                             dx_dtype=bf16)
    dxn2 = _mm(n + "up_dx", dup, w["w_up"], "nt")
    g["w_up"] = _mm(n + "up_dw", r["xn2"], dup, "tn", bf16)
    (dh1,), (g["g_ffn"],) = _rowwise_bwd(n + "norm_ffn_b", _f_norm, [r["h1"]], [w["g_ffn"]], [dxn2], 1, adds={0: dh2})
    docat = _mm(n + "out_dx", dh1, w["w_o"], "nt")
    g["w_o"] = _mm(n + "out_dw", r["ocat"], dh1, "tn", bf16)
    (do_mla, do_fox, dhs, dlg), (g["g_out"],) = _rowwise_bwd(
        n + "merge_b", _f_merge, [r["o_mla"], r["o_fox"], r["hs"], zlg], [w["g_out"]], [docat], 4)
    a, hs = r["a"], r["hs"]
    a_next = jnp.concatenate([a[1:], jnp.zeros((1, LRU_WIDTH), f32)], axis=0)
    h_prev = jnp.concatenate([jnp.zeros((1, LRU_WIDTH), f32), hs[:-1]], axis=0)
    da, dbx = _scan_bwd(n + "lru_scan_b", a_next, h_prev, dhs)
    (dgates, dxc_a), (g["b_r"], g["b_i"], g["lam"]) = _rowwise_bwd(
        n + "lru_gate_b", _f_lru_gate, [r["gates"], r["xc"]], [w["b_r"], w["b_i"], w["lam"]], [da, dbx], 2,
        dts=[bf16, f32])
    dxc_b = _mm(n + "lru_gates_dx", dgates, w["w_ri"], "nt")
    g["w_ri"] = _mm(n + "lru_gates_dw", r["xc"], dgates, "tn")
    dlx, g["lru_conv_w"], g["lru_conv_b"] = _conv_bwd(n + "lru_conv_b", zlx, dxc_a, w["lru_conv_w"], LRU_CONV, dout2=dxc_b)
    fox_scale = FOX_HEAD_DIM ** -0.5
    fq, fk, fv = (z, Z_FQ // LANE), (z, Z_FK // LANE), (z, Z_FV // LANE)
    dfq, delta_f, dc_q = _attn_dq(n + "fox_dq", fq, fk, fv, r["o_fox"], do_fox, r["lse_f"], fox_scale, r["c_col"], r["c_row"])
    dfk, dfv, dc_k = _attn_dkv(n + "fox_dkv", fq, fk, fv, do_fox, r["lse_f"], delta_f, fox_scale, r["c_col"], r["c_row"])
    pad_rows = jnp.zeros((SUBLANE - HEADS, s_len), f32)
    dfl_t, g["b_f8"] = _decay_bwd(n + "decay_b", r["fl_t"], w["b_f8"],
                                  jnp.concatenate([dc_k.reshape(HEADS, s_len), pad_rows], axis=0),
                                  jnp.concatenate([dc_q.reshape(HEADS, s_len), pad_rows], axis=0))
    dfl = jnp.pad(dfl_t.T, ((0, 0), (0, LANE - SUBLANE)))
    mla_scale = (MLA_NOPE + MLA_ROPE) ** -0.5
    qr, kk, kv = (r["qr"], 0), (r["kk"], 0), (r["kv"], HEADS)
    dqr, delta_m, _ = _attn_dq(n + "mla_dq", qr, kk, kv, r["o_mla"], do_mla, r["lse_m"], mla_scale)
    dkk, dv_m = _attn_dkv(n + "mla_dkv", qr, kk, kv, do_mla, r["lse_m"], delta_m, mla_scale)
    (dq, dkpart, dkr), _ = _rowwise_bwd(n + "mla_prep_b", _f_mla_prep, [r["q"], (r["kv"], HEADS * LANE, 0), zkr, pos],
                                        _rope_consts(), [dqr, dkk], 3, dts=[bf16, bf16, f32])
    dkv = jnp.concatenate([dkpart, dv_m.astype(bf16)], axis=1)
    dkvn = _mm(n + "ukv_dx", dkv, w["w_ukv"], "nt")
    g["w_ukv"] = _mm(n + "ukv_dw", r["kvn"], dkv, "tn", bf16)
    dqcn = _mm(n + "uq_dx", dq, w["w_uq"], "nt")
    g["w_uq"] = _mm(n + "uq_dw", r["qcn"], dq, "tn", bf16)
    (dqc, dkvc), (g["g_qc"], g["g_kvc"]) = _rowwise_bwd(n + "latent_norm_b", _f_latent, [zq, zkv],
                                                        [w["g_qc"], w["g_kvc"]], [dqcn, dkvn], 2)
    dz = jnp.concatenate([t.astype(bf16) for t in (dfq, dfk, dfv, dlx, dlg, dqc, dkvc, dkr, dfl)], axis=1)
    dxn = _mm(n + "in_dx", dz, w["w_in"], "nt")
    g["w_in"] = _mm(n + "in_dw", r["xn"], dz, "tn", bf16)
    (dh0,), (g["g_mix"],) = _rowwise_bwd(n + "norm_mix_b", _f_norm, [r["h0"]], [w["g_mix"]], [dxn], 1, adds={0: dh1})
    return dh0, g


def _unpad_layer_grads(g):
    d_ri = g["w_ri"]
    idx = jnp.arange(LRU_BLOCKS)

    def diag_blocks(m):
        return m.reshape(LRU_BLOCKS, LRU_BLOCK, LRU_BLOCKS, LRU_BLOCK)[idx, :, idx, :]

    return dict(
        g_mix=g["g_mix"][0], w_in=_take_inv(g["w_in"], Z_MAP, 1), g_qc=g["g_qc"][0, :MLA_Q_RANK],
        w_uq=_take_inv(g["w_uq"][:MLA_Q_RANK], UQ_COL_MAP, 1), g_kvc=g["g_kvc"][0],
        w_ukv=_take_inv(g["w_ukv"], UKV_MAP, 1), b_f=g["b_f8"][:FOX_HEADS, 0],
        lru_conv_w=g["lru_conv_w"], lru_conv_b=g["lru_conv_b"][0],
        w_r=diag_blocks(d_ri[:, :LRU_WIDTH]), b_r=g["b_r"][0], w_i=diag_blocks(d_ri[:, LRU_WIDTH:]), b_i=g["b_i"][0],
        lru_lambda=g["lam"][0], g_out=_take_inv(g["g_out"][0], OMIX_MAP, 0), w_o=_take_inv(g["w_o"], OMIX_MAP, 0),
        g_ffn=g["g_ffn"][0], w_up=g["w_up"], ffn_conv_w=g["ffn_conv_w"], ffn_conv_b=g["ffn_conv_b"][0],
        w_down=g["w_down"], g_ple=g["g_ple"][0], w_ple_gate=g["w_ple_gate"], w_ple_proj=g["w_ple_proj"],
    )


LAYER_WEIGHTS = ["g_mix", "w_in", "g_qc", "w_uq", "g_kvc", "w_ukv", "b_f", "lru_conv_w", "lru_conv_b", "w_r", "b_r", "w_i",
                 "b_i", "lru_lambda", "g_out", "w_o", "g_ffn", "w_up", "ffn_conv_w", "ffn_conv_b", "w_down", "g_ple",
                 "w_ple_gate", "w_ple_proj"]
WEIGHTS = LAYER_WEIGHTS + ["g_final"]


def _local_step(x, p, positions, target, layers, g_final):
    pos = positions.astype(f32).reshape(-1, 1)
    h = x
    ws, saved = [], []
    for l in range(DEPTH):
        w = _prep_layer_weights(layers[l])
        h, r = _layer_fwd(l, h, p[l], pos, w)
        ws.append(w)
        saved.append(r)
    loss, dh, dg_final = _loss_head("loss_head", h, target, g_final.reshape(1, -1))
    layer_grads = [None] * DEPTH
    for l in reversed(range(DEPTH)):
        dh, g = _layer_bwd(l, dh, saved[l], pos, ws[l])
        layer_grads[l] = _unpad_layer_grads(g)
    return loss[0, 0], dh, layer_grads, dg_final[0]


MESH_AXES = ("x", "y", "c")


def _exchange(name, src, axes, scatter, pieces=1):
    n = 2 ** len(axes)
    flips = [tuple((f >> (len(axes) - 1 - b)) & 1 for b in range(len(axes))) for f in range(1, n)]
    rows = src.shape[-2]
    piece_rows = rows // pieces
    assert piece_rows * pieces == rows

    def body(src_ref, out_ref, send_sems, recv_sems, local_sem):
        coords = {a: lax.axis_index(a) for a in MESH_AXES}

        def index_of(cd):
            idx = 0
            for a in axes:
                idx = idx * 2 + cd[a]
            return idx

        me = index_of(coords)
        local = pltpu.make_async_copy(src_ref.at[me] if scatter else src_ref, out_ref.at[me], local_sem)
        local.start()
        copies = []
        for k, f in enumerate(flips):
            peer = dict(coords)
            for a, bit in zip(axes, f):
                if bit:
                    peer[a] = 1 - coords[a]
            slab = src_ref.at[index_of(peer)] if scatter else src_ref
            for pc in range(pieces):
                span = pl.ds(pc * piece_rows, piece_rows)
                cp = pltpu.make_async_remote_copy(
                    src_ref=slab.at[span], dst_ref=out_ref.at[me, span],
                    send_sem=send_sems.at[k * pieces + pc], recv_sem=recv_sems.at[k * pieces + pc],
                    device_id=tuple(peer[a] for a in MESH_AXES), device_id_type=pl.DeviceIdType.MESH)
                cp.start()
                copies.append(cp)
        for cp in copies:
            cp.wait()
        local.wait()

    n_sems = (n - 1) * pieces
    return pl.pallas_call(
        body, name=name, out_shape=jax.ShapeDtypeStruct((n, rows, LANE), src.dtype),
        in_specs=[pl.BlockSpec(memory_space=pl.ANY)], out_specs=pl.BlockSpec(memory_space=pl.ANY),
        scratch_shapes=[pltpu.SemaphoreType.DMA((n_sems,)), pltpu.SemaphoreType.DMA((n_sems,)), pltpu.SemaphoreType.DMA])(src)


def _row_tile(rows, cap):
    if rows <= cap:
        return rows
    for t in range(cap, SUBLANE - 1, -SUBLANE):
        if rows % t == 0:
            return t
    return rows


def _sum_slabs(name, a):
    n, rows, _ = a.shape
    tr = _row_tile(rows, 512)

    def kern(a_ref, o_ref):
        acc = a_ref[0].astype(f32)
        for k in range(1, n):
            acc = acc + a_ref[k].astype(f32)
        o_ref[...] = acc

    return pl.pallas_call(
        kern, name=name, grid=(rows // tr,), in_specs=[pl.BlockSpec((n, tr, LANE), lambda i: (0, i, 0))],
        out_specs=pl.BlockSpec((tr, LANE), lambda i: (i, 0)), out_shape=jax.ShapeDtypeStruct((rows, LANE), f32),
        compiler_params=pltpu.CompilerParams(dimension_semantics=("parallel",)))(a)


ADAM_BLOCK_BYTES = 2 ** 20


def _adamw(name, w, g, m, v):
    rows, cols = w.shape
    tr = _row_tile(rows, max(SUBLANE, ADAM_BLOCK_BYTES // (4 * cols) // SUBLANE * SUBLANE))

    def kern(w_ref, g_ref, m_ref, v_ref, d_ref, nm_ref, nv_ref):
        gv = g_ref[...]
        nm = ADAM_B1 * m_ref[...] + (1.0 - ADAM_B1) * gv
        nv = ADAM_B2 * v_ref[...] + (1.0 - ADAM_B2) * (gv * gv)
        m_hat = nm / (1.0 - ADAM_B1 ** ADAM_STEP)
        v_hat = nv / (1.0 - ADAM_B2 ** ADAM_STEP)
        d_ref[...] = -ADAM_LR * (m_hat / (jnp.sqrt(v_hat) + ADAM_EPS) + ADAM_WD * w_ref[...])
        nm_ref[...] = nm
        nv_ref[...] = nv

    spec = pl.BlockSpec((tr, cols), lambda i: (i, 0))
    return pl.pallas_call(
        kern, name=name, grid=(rows // tr,), in_specs=[spec] * 4, out_specs=[spec] * 3,
        out_shape=[jax.ShapeDtypeStruct((rows, cols), f32)] * 3,
        compiler_params=pltpu.CompilerParams(dimension_semantics=("parallel",)))(w, g, m, v)


def _pack(arrays, row_multiple):
    flat = jnp.concatenate([a.reshape(-1) for a in arrays])
    per = LANE * row_multiple
    total = -(-flat.shape[0] // per) * per
    return jnp.pad(flat, (0, total - flat.shape[0])).reshape(-1, LANE)


def _unpack(buf, shapes):
    flat = buf.reshape(-1)
    out, at = [], 0
    for s in shapes:
        size = int(np.prod(s))
        out.append(flat[at:at + size].reshape(s))
        at += size
    return out


SHARD_AXIS = {"w_in": 2, "w_uq": 2, "w_ukv": 2, "lru_conv_w": 2, "w_o": 1, "w_up": 2, "ffn_conv_w": 2, "w_down": 1,
              "w_ple_gate": 1, "w_ple_proj": 2}
SHARDED = [k for k in WEIGHTS if k in SHARD_AXIS]
REPLICATED = [k for k in WEIGHTS if k not in SHARD_AXIS]
ELEMENTWISE_F32 = ("lru_conv_w", "ffn_conv_w")
N_SHARDS = 4
BF16_TILE_ROWS = 16


HBM_SPEC = pl.BlockSpec(memory_space=pl.ANY)
CHIP_FLIPS = ((1, 0), (0, 1), (1, 1))
N_DEVICES = 8
SUM_BLOCK_BYTES = 4 * 2 ** 20


def _device_index():
    return 4 * lax.axis_index("x") + 2 * lax.axis_index("y") + lax.axis_index("c")


def _gather_shards(name, shards):
    n = len(shards)

    def body(*refs):
        ins, outs, (send_sems, recv_sems) = refs[:n], refs[n:2 * n], refs[2 * n:]
        x, y, c = (lax.axis_index(a) for a in MESH_AXES)
        copies = []
        for i in range(n):
            for k, (fx, fy) in enumerate(CHIP_FLIPS):
                peer = (1 - x if fx else x, 1 - y if fy else y, c)
                cp = pltpu.make_async_remote_copy(
                    src_ref=ins[i], dst_ref=outs[i].at[2 * x + y], send_sem=send_sems.at[3 * i + k],
                    recv_sem=recv_sems.at[3 * i + k], device_id=peer, device_id_type=pl.DeviceIdType.MESH)
                cp.start()
                copies.append(cp)
        for cp in copies:
            cp.wait()

    return pl.pallas_call(
        body, name=name, out_shape=[jax.ShapeDtypeStruct((N_SHARDS,) + s.shape, s.dtype) for s in shards],
        in_specs=[HBM_SPEC] * n, out_specs=[HBM_SPEC] * n,
        scratch_shapes=[pltpu.SemaphoreType.DMA((3 * n,)), pltpu.SemaphoreType.DMA((3 * n,))])(*shards)


def _scatter_layer(name, layer, chunks):
    n = len(chunks)

    def body(*refs):
        ins, outs, (send_sems, recv_sems) = refs[:n], refs[n:2 * n], refs[2 * n:]
        x, y, c = (lax.axis_index(a) for a in MESH_AXES)
        me = _device_index()
        sends = []
        for j in range(N_SHARDS):
            target = (j // 2, j % 2, layer)
            remote = jnp.logical_not((x == target[0]) & (y == target[1]) & (c == layer))
            for i in range(n):
                cp = pltpu.make_async_remote_copy(
                    src_ref=ins[i].at[j], dst_ref=outs[i].at[me], send_sem=send_sems.at[N_SHARDS * i + j],
                    recv_sem=recv_sems.at[N_DEVICES * i + me], device_id=target, device_id_type=pl.DeviceIdType.MESH)
                pl.when(remote)(cp.start)
                sends.append((remote, cp))

        @pl.when(c == layer)
        def _():
            for s in range(N_DEVICES):
                for i in range(n):
                    arrival = pltpu.make_async_remote_copy(
                        src_ref=ins[i].at[0], dst_ref=outs[i].at[s], send_sem=send_sems.at[0],
                        recv_sem=recv_sems.at[N_DEVICES * i + s], device_id=(x, y, c), device_id_type=pl.DeviceIdType.MESH)
                    pl.when(me != s)(arrival.wait_recv)

        for remote, cp in sends:
            pl.when(remote)(cp.wait_send)

    return pl.pallas_call(
        body, name=name, out_shape=[jax.ShapeDtypeStruct((N_DEVICES,) + ch.shape[1:], ch.dtype) for ch in chunks],
        in_specs=[HBM_SPEC] * n, out_specs=[HBM_SPEC] * n,
        scratch_shapes=[pltpu.SemaphoreType.DMA((N_SHARDS * n,)), pltpu.SemaphoreType.DMA((N_DEVICES * n,))])(*chunks)


def _sum_contributions(name, got, mine):
    _, a, b = got.shape
    ta = _row_tile(a, max(SUBLANE, SUM_BLOCK_BYTES // (N_DEVICES * b * got.dtype.itemsize) // SUBLANE * SUBLANE))

    def kern(got_ref, mine_ref, o_ref):
        me = _device_index()
        acc = jnp.zeros(o_ref.shape, f32)
        for s in range(N_DEVICES):
            acc = acc + jnp.where(me == s, mine_ref[...].astype(f32), got_ref[s].astype(f32))
        o_ref[...] = acc

    return pl.pallas_call(
        kern, name=name, grid=(a // ta,),
        in_specs=[pl.BlockSpec((N_DEVICES, ta, b), lambda i: (0, i, 0)), pl.BlockSpec((ta, b), lambda i: (i, 0))],
        out_specs=pl.BlockSpec((ta, b), lambda i: (i, 0)), out_shape=jax.ShapeDtypeStruct((a, b), f32),
        compiler_params=pltpu.CompilerParams(dimension_semantics=("parallel",)))(got, mine)


def _swap_layers(name, sums):
    n = len(sums[0])

    def body(*refs):
        srcs = (refs[:n], refs[n:2 * n])
        outs, (send_sems, recv_sems) = refs[2 * n:3 * n], refs[3 * n:]
        x, y, c = (lax.axis_index(a) for a in MESH_AXES)
        for i in range(n):
            for layer in range(DEPTH):
                cp = pltpu.make_async_remote_copy(
                    src_ref=srcs[layer][i], dst_ref=outs[i], send_sem=send_sems.at[i], recv_sem=recv_sems.at[i],
                    device_id=(x, y, 1 - c), device_id_type=pl.DeviceIdType.MESH)
                pl.when(c == layer)(cp.start)
        for i in range(n):
            pltpu.make_async_remote_copy(
                src_ref=srcs[0][i], dst_ref=outs[i], send_sem=send_sems.at[i], recv_sem=recv_sems.at[i],
                device_id=(x, y, 1 - c), device_id_type=pl.DeviceIdType.MESH).wait()

    return pl.pallas_call(
        body, name=name, out_shape=[jax.ShapeDtypeStruct(s.shape, s.dtype) for s in sums[0]],
        in_specs=[HBM_SPEC] * (2 * n), out_specs=[HBM_SPEC] * n,
        scratch_shapes=[pltpu.SemaphoreType.DMA((n,)), pltpu.SemaphoreType.DMA((n,))])(*sums[0], *sums[1])


def _stack_shards(g, axis):
    if axis == 1:
        return g.reshape(N_SHARDS, g.shape[0] // N_SHARDS, g.shape[1])
    return g.reshape(g.shape[0], N_SHARDS, g.shape[1] // N_SHARDS).transpose(1, 0, 2)


def _join_shards(s, axis):
    if axis == 1:
        return s.reshape(-1, s.shape[2])
    return s.transpose(1, 0, 2).reshape(s.shape[1], -1)


def _gather_weights(shards):
    sent = [shards[k] if k in ELEMENTWISE_F32 else shards[k].astype(bf16) for k in SHARDED]
    got = _gather_shards("gather_weights", sent)
    j = 2 * lax.axis_index("x") + lax.axis_index("y")
    full = [{} for _ in range(DEPTH)]
    for k, own, g in zip(SHARDED, sent, got):
        g = lax.dynamic_update_slice(g, own[None], (j, 0, 0, 0))
        for l in range(DEPTH):
            full[l][k] = _join_shards(g[:, l], SHARD_AXIS[k])
    return full


def _reduce_sharded_grads(layer_grads):
    j = 2 * lax.axis_index("x") + lax.axis_index("y")
    c = lax.axis_index("c")
    sums = []
    for l in range(DEPTH):
        chunks = [_stack_shards(layer_grads[l][k], SHARD_AXIS[k]).astype(bf16) for k in SHARDED]
        got = _scatter_layer(f"scatter_grads_l{l}", l, chunks)
        sums.append([_sum_contributions(f"sum_l{l}_{k}", g, lax.dynamic_index_in_dim(ch, j, 0, keepdims=False))
                     for k, g, ch in zip(SHARDED, got, chunks)])
    other = _swap_layers("swap_layers", sums)
    out = {}
    for i, k in enumerate(SHARDED):
        out[k] = jnp.stack([jnp.where(c == 0, sums[0][i], other[i]), jnp.where(c == 0, other[i], sums[1][i])])
    return out


def kernel(x, p, positions, g_mix, w_in, g_qc, w_uq, g_kvc, w_ukv, b_f, lru_conv_w, lru_conv_b, w_r, b_r, w_i, b_i, lru_lambda, g_out, w_o, g_ffn, w_up, ffn_conv_w, ffn_conv_b, w_down, g_ple, w_ple_gate, w_ple_proj, g_final, loss_target, m_g_mix, m_w_in, m_g_qc, m_w_uq, m_g_kvc, m_w_ukv, m_b_f, m_lru_conv_w, m_lru_conv_b, m_w_r, m_b_r, m_w_i, m_b_i, m_lru_lambda, m_g_out, m_w_o, m_g_ffn, m_w_up, m_ffn_conv_w, m_ffn_conv_b, m_w_down, m_g_ple, m_w_ple_gate, m_w_ple_proj, m_g_final, v_g_mix, v_w_in, v_g_qc, v_w_uq, v_g_kvc, v_w_ukv, v_b_f, v_lru_conv_w, v_lru_conv_b, v_w_r, v_b_r, v_w_i, v_b_i, v_lru_lambda, v_g_out, v_w_o, v_g_ffn, v_w_up, v_ffn_conv_w, v_ffn_conv_b, v_w_down, v_g_ple, v_w_ple_gate, v_w_ple_proj, v_g_final):
    given = locals()
    w = {k: given[k] for k in WEIGHTS}
    m = {k: given["m_" + k] for k in WEIGHTS}
    v = {k: given["v_" + k] for k in WEIGHTS}

    layers = _gather_weights(w)
    for l in range(DEPTH):
        layers[l].update({k: w[k][l] for k in LAYER_WEIGHTS if k in REPLICATED})
    loss, dx, layer_grads, dg_final = _local_step(x[0], p[:, 0], positions[0], loss_target[0], layers, w["g_final"])

    g_sharded = _reduce_sharded_grads(layer_grads)
    big = [[], [], [], []]
    for k in SHARDED:
        shape = w[k].shape
        flat = [t.reshape(-1, shape[-1]) for t in (w[k], g_sharded[k], m[k], v[k])]
        for kind, res in enumerate((flat[1],) + tuple(_adamw("adamw_" + k, *flat))):
            big[kind].append(res.reshape(shape))

    grads = {k: jnp.stack([layer_grads[l][k] for l in range(DEPTH)]) for k in LAYER_WEIGHTS if k in REPLICATED}
    grads["g_final"] = dg_final
    rep_shapes = [w[k].shape for k in REPLICATED] + [(1,)]
    contrib = _pack([grads[k] for k in REPLICATED] + [loss.reshape(1)], SUBLANE)
    g_rep = _sum_slabs("sum_replicated", _exchange("gather_replicated", contrib, MESH_AXES, scatter=False))
    zero = jnp.zeros((1,), f32)
    w_rep, m_rep, v_rep = (_pack([t[k] for k in REPLICATED] + [zero], SUBLANE) for t in (w, m, v))
    rep = [_unpack(b, rep_shapes) for b in (g_rep,) + tuple(_adamw("adamw_replicated", w_rep, g_rep, m_rep, v_rep))]

    outs = []
    for kind in range(4):
        by_name = dict(zip(SHARDED, big[kind]))
        by_name.update(zip(REPLICATED, rep[kind][:-1]))
        outs.append([by_name[k] for k in WEIGHTS])
    total_loss = rep[0][-1][0]
    return (total_loss, dx.reshape(x.shape), *outs[0], *outs[1], *outs[2], *outs[3])
```

```python
import functools
import math

import numpy as np
import jax
import jax.numpy as jnp
from jax import lax
from jax.experimental import pallas as pl
from jax.experimental.pallas import tpu as pltpu

f32, bf16 = jnp.float32, jnp.bfloat16

D_MODEL = 1024
PLE_DIM = 256
MLA_HEADS, MLA_NOPE, MLA_ROPE, MLA_V = 4, 64, 32, 64
MLA_Q_RANK, MLA_KV_RANK = 192, 128
FOX_HEADS, FOX_HEAD_DIM = 4, 64
LRU_WIDTH, LRU_BLOCKS, LRU_BLOCK, LRU_CONV, LRU_C = 512, 8, 64, 4, 8.0
D_FF, FFN_CONV = 2816, 3
ROPE_THETA = 10000.0
EPS = 1e-6
DEPTH = 2
ADAM_LR, ADAM_B1, ADAM_B2, ADAM_EPS, ADAM_WD, ADAM_STEP = 0.001, 0.9, 0.999, 1e-08, 0.01, 10

LANE = 128
SUBLANE = 8
HEADS = 4

Z_FQ, Z_FK, Z_FV, Z_LX, Z_LG, Z_QC, Z_KVC, Z_KR, Z_FL, Z_W = 0, 512, 1024, 1536, 2048, 2560, 2816, 2944, 3072, 3200
QC_W = 256
ROPE_AT = 64


def _head_pad_map(n_heads, width):
    m = -np.ones(n_heads * LANE, np.int64)
    for h in range(n_heads):
        m[h * LANE:h * LANE + width] = h * width + np.arange(width)
    return m


def _z_map():
    m = -np.ones(Z_W, np.int64)
    o_qc, o_kvc, o_kr = 0, MLA_Q_RANK, MLA_Q_RANK + MLA_KV_RANK
    o_fq = o_kr + MLA_ROPE
    o_fk, o_fv = o_fq + 256, o_fq + 512
    o_fl = o_fv + 256
    o_lx = o_fl + FOX_HEADS
    o_lg = o_lx + LRU_WIDTH
    m[Z_FQ:Z_FQ + 512] = np.where(_head_pad_map(4, 64) >= 0, _head_pad_map(4, 64) + o_fq, -1)
    m[Z_FK:Z_FK + 512] = np.where(_head_pad_map(4, 64) >= 0, _head_pad_map(4, 64) + o_fk, -1)
    m[Z_FV:Z_FV + 512] = np.where(_head_pad_map(4, 64) >= 0, _head_pad_map(4, 64) + o_fv, -1)
    m[Z_LX:Z_LX + 512] = o_lx + np.arange(512)
    m[Z_LG:Z_LG + 512] = o_lg + np.arange(512)
    m[Z_QC:Z_QC + MLA_Q_RANK] = o_qc + np.arange(MLA_Q_RANK)
    m[Z_KVC:Z_KVC + MLA_KV_RANK] = o_kvc + np.arange(MLA_KV_RANK)
    m[Z_KR + ROPE_AT:Z_KR + ROPE_AT + MLA_ROPE] = o_kr + np.arange(MLA_ROPE)
    m[Z_FL:Z_FL + FOX_HEADS] = o_fl + np.arange(FOX_HEADS)
    return m


def _ukv_map():
    m = -np.ones(2 * HEADS * LANE, np.int64)
    for h in range(HEADS):
        m[h * LANE:h * LANE + MLA_NOPE] = h * (MLA_NOPE + MLA_V) + np.arange(MLA_NOPE)
        m[HEADS * LANE + h * LANE:HEADS * LANE + h * LANE + MLA_V] = h * (MLA_NOPE + MLA_V) + MLA_NOPE + np.arange(MLA_V)
    return m


def _omix_map():
    return np.concatenate([_head_pad_map(4, 64), np.where(_head_pad_map(4, 64) >= 0, _head_pad_map(4, 64) + 256, -1),
                           512 + np.arange(512)])


def _pad_to(m, n):
    return np.concatenate([m, -np.ones(n - m.shape[0], np.int64)])


def _take_pad(a, m, axis):
    out = jnp.take(a, jnp.asarray(np.maximum(m, 0), jnp.int32), axis=axis)
    shape = [1] * a.ndim
    shape[axis] = m.shape[0]
    return out * jnp.asarray((m >= 0).reshape(shape), a.dtype)


def _take_inv(a, m, axis):
    n = int(m.max()) + 1
    inv = np.zeros(n, np.int64)
    inv[m[m >= 0]] = np.nonzero(m >= 0)[0]
    return jnp.take(a, jnp.asarray(inv, jnp.int32), axis=axis)


Z_MAP = _z_map()
UQ_COL_MAP = _head_pad_map(HEADS, MLA_NOPE + MLA_ROPE)
UQ_ROW_MAP = _pad_to(np.arange(MLA_Q_RANK), QC_W)
UKV_MAP = _ukv_map()
OMIX_MAP = _omix_map()
OMIX_W = 1536


def _rope_tables(width, at):
    half = MLA_ROPE // 2
    inv = ROPE_THETA ** (-np.arange(half, dtype=np.float32) / half)
    freq = np.zeros((1, width), np.float32)
    m1 = np.zeros((1, width), np.float32)
    m2 = np.zeros((1, width), np.float32)
    for h in range(width // LANE):
        b = h * LANE + at
        freq[0, b:b + half] = inv
        freq[0, b + half:b + 2 * half] = inv
        m1[0, b:b + half] = 1.0
        m2[0, b + half:b + 2 * half] = 1.0
    return freq, m1, m2


def _view(r):
    return r if isinstance(r, tuple) else (r, r.shape[1], 0)


def _blk(dim, cap):
    if dim <= cap:
        return dim
    for b in range(cap, LANE - 1, -LANE):
        if dim % b == 0:
            return b
    return dim


@functools.partial(jax.custom_vjp, nondiff_argnums=(1, 2))
def _roll(x, shift, axis):
    return pltpu.roll(x, shift, axis)


def _roll_fwd(x, shift, axis):
    return pltpu.roll(x, shift, axis), None


def _roll_bwd(shift, axis, _, g):
    return (pltpu.roll(g, g.shape[axis] - shift, axis),)


_roll.defvjp(_roll_fwd, _roll_bwd)


def _rowwise(name, fn, rows, pars, outs, tb=256):
    rows = [_view(r) for r in rows]
    n = rows[0][0].shape[0]
    tb = min(tb, n)
    nr, npar = len(rows), len(pars)

    def kern(*refs):
        r = [refs[k][...].astype(f32) for k in range(nr)]
        p = [refs[nr + k][...] for k in range(npar)]
        res = fn(*r, *p)
        for o_ref, o in zip(refs[nr + npar:], res):
            o_ref[...] = o.astype(o_ref.dtype)

    in_specs = [pl.BlockSpec((tb, w), lambda i, j=idx: (i, j)) for (_, w, idx) in rows]
    in_specs += [pl.BlockSpec(p.shape, lambda i: (0, 0)) for p in pars]
    out_specs = [pl.BlockSpec((tb, w), lambda i: (i, 0)) for (w, _) in outs]
    out_shape = [jax.ShapeDtypeStruct((n, w), dt) for (w, dt) in outs]
    return pl.pallas_call(kern, name=name, grid=(n // tb,), in_specs=in_specs, out_specs=out_specs, out_shape=out_shape,
                          compiler_params=pltpu.CompilerParams(dimension_semantics=("parallel",)))(*[r[0] for r in rows], *pars)


def _rowwise_bwd(name, fn, rows, pars, cts, ndiff, adds=None, tb=256, dts=None):
    rows = [_view(r) for r in rows]
    dts = dts or [f32] * ndiff
    adds = adds or {}
    add_keys = sorted(adds)
    n = rows[0][0].shape[0]
    tb = min(tb, n)
    nr, npar, nct, nadd = len(rows), len(pars), len(cts), len(add_keys)

    def kern(*refs):
        i = pl.program_id(0)
        r = [refs[k][...].astype(f32) for k in range(nr)]
        p = [refs[nr + k][...] for k in range(npar)]
        ct = [refs[nr + npar + k][...].astype(f32) for k in range(nct)]
        ad = {key: refs[nr + npar + nct + k][...] for k, key in enumerate(add_keys)}
        o_refs = refs[nr + npar + nct + nadd:]

        def g(*d):
            return tuple(fn(*d[:ndiff], *r[ndiff:], *d[ndiff:]))

        _, vjp = jax.vjp(g, *r[:ndiff], *p)
        grads = vjp(tuple(ct))
        for k in range(ndiff):
            gk = grads[k]
            if k in ad:
                gk = gk + ad[k]
            o_refs[k][...] = gk.astype(o_refs[k].dtype)

        @pl.when(i == 0)
        def _():
            for k in range(npar):
                o_refs[ndiff + k][...] = jnp.zeros_like(o_refs[ndiff + k])

        for k in range(npar):
            o_refs[ndiff + k][...] += grads[ndiff + k]

    in_specs = [pl.BlockSpec((tb, w), lambda i, j=idx: (i, j)) for (_, w, idx) in rows]
    in_specs += [pl.BlockSpec(p.shape, lambda i: (0, 0)) for p in pars]
    in_specs += [pl.BlockSpec((tb, c.shape[1]), lambda i: (i, 0)) for c in cts]
    in_specs += [pl.BlockSpec((tb, adds[k].shape[1]), lambda i: (i, 0)) for k in add_keys]
    out_specs = [pl.BlockSpec((tb, rows[k][1]), lambda i: (i, 0)) for k in range(ndiff)]
    out_specs += [pl.BlockSpec(p.shape, lambda i: (0, 0)) for p in pars]
    out_shape = [jax.ShapeDtypeStruct((n, rows[k][1]), dts[k]) for k in range(ndiff)]
    out_shape += [jax.ShapeDtypeStruct(p.shape, f32) for p in pars]
    res = pl.pallas_call(kern, name=name, grid=(n // tb,), in_specs=in_specs, out_specs=out_specs, out_shape=out_shape,
                         compiler_params=pltpu.CompilerParams(dimension_semantics=("arbitrary",)))(
        *[r[0] for r in rows], *pars, *cts, *[adds[k] for k in add_keys])
    return res[:ndiff], res[ndiff:]


_DOT_DIMS = {"nn": ((1,), (0,)), "nt": ((1,), (1,)), "tn": ((0,), (0,))}

MM_VMEM_BUDGET = 36 * 2 ** 20
MM_MAX_TM = 1024
MM_STEP, MM_RESULT, MM_XPOSE, MM_CAST = 700.0, 7.5e-4, 9e-4, 1e-3


def _tile_candidates(dim):
    c = [d for d in range(LANE, dim + 1, LANE) if dim % d == 0]
    return c or [dim]


@functools.lru_cache(maxsize=None)
def _mm_tiles(mode, m, n, k, a_bytes, b_bytes, o_bytes):
    best, best_cost = None, None
    for tm in _tile_candidates(m):
        if tm > MM_MAX_TM:
            continue
        for tn in _tile_candidates(n):
            for tk in _tile_candidates(k):
                vmem = 2 * (tm * tk * a_bytes + tk * tn * b_bytes + tm * tn * o_bytes) + 4 * tm * tn * (2 if tk < k else 1)
                vmem += (2 * tm * tk if a_bytes > 2 else 0) + (2 * tk * tn if b_bytes > 2 else 0)
                if vmem > MM_VMEM_BUDGET:
                    continue
                steps = (m // tm) * (n // tn) * (k // tk)
                cost = steps * MM_STEP + m * n * (k // tk) * MM_RESULT
                if mode == "tn":
                    cost += m * k * (n // tn) * MM_XPOSE
                cost += (m * k * (n // tn) * MM_CAST if a_bytes > 2 else 0) + (k * n * (m // tm) * MM_CAST if b_bytes > 2 else 0)
                if best is None or cost < best_cost:
                    best, best_cost = (tm, tn, tk), cost
    return best


def _mm(name, a, b, mode="nn", out_dtype=f32, res=None):
    if mode == "nn":
        (m, k), (_, n) = a.shape, b.shape
    elif mode == "nt":
        (m, k), (n, _) = a.shape, b.shape
    else:
        (k, m), (_, n) = a.shape, b.shape
    has_res = res is not None
    tm, tn, tk = _mm_tiles(mode, m, n, k, a.dtype.itemsize, b.dtype.itemsize,
                           jnp.dtype(out_dtype).itemsize + (res.dtype.itemsize if has_res else 0))
    nk = k // tk
    dims = (_DOT_DIMS[mode], ((), ()))

    def kern(*refs):
        a_ref, b_ref = refs[0], refs[1]
        o_ref, acc_ref = refs[-2], refs[-1]
        kk = pl.program_id(2)
        part = lax.dot_general(a_ref[...].astype(bf16), b_ref[...].astype(bf16), dims, preferred_element_type=f32)

        def finish(out):
            if has_res:
                out = out + refs[2][...]
            o_ref[...] = out.astype(o_ref.dtype)

        if nk == 1:
            finish(part)
            return

        @pl.when(kk == 0)
        def _():
            acc_ref[...] = part

        @pl.when(jnp.logical_and(kk > 0, kk < nk - 1))
        def _():
            acc_ref[...] += part

        @pl.when(kk == nk - 1)
        def _():
            finish(acc_ref[...] + part)

    if mode == "tn":
        a_spec = pl.BlockSpec((tk, tm), lambda i, j, kk: (kk, i))
    else:
        a_spec = pl.BlockSpec((tm, tk), lambda i, j, kk: (i, kk))
    if mode == "nt":
        b_spec = pl.BlockSpec((tn, tk), lambda i, j, kk: (j, kk))
    else:
        b_spec = pl.BlockSpec((tk, tn), lambda i, j, kk: (kk, j))
    in_specs = [a_spec, b_spec]
    args = [a, b]
    if has_res:
        in_specs.append(pl.BlockSpec((tm, tn), lambda i, j, kk: (i, j)))
        args.append(res)
    return pl.pallas_call(
        kern, name=name, grid=(m // tm, n // tn, nk), in_specs=in_specs,
        out_specs=pl.BlockSpec((tm, tn), lambda i, j, kk: (i, j)),
        out_shape=jax.ShapeDtypeStruct((m, n), out_dtype),
        scratch_shapes=[pltpu.VMEM((tm, tn) if nk > 1 else (SUBLANE, LANE), f32)],
        compiler_params=pltpu.CompilerParams(dimension_semantics=("parallel", "parallel", "arbitrary")))(*args)


ATT_T = 512


def _att_tile(s):
    return min(ATT_T, s)


def _fold_scale(scale):
    return (scale, 1.0) if math.frexp(scale)[0] == 0.5 else (1.0, scale)


def _scores(qb, kb, s_mul, ck, diagonal, t):
    s = lax.dot_general(qb, kb, (_DOT_DIMS["nt"], ((), ())), preferred_element_type=f32)
    if s_mul != 1.0:
        s = s * s_mul
    if ck is not None:
        s = s - ck
    if not diagonal:
        return s
    row = lax.broadcasted_iota(jnp.int32, (t, t), 0)
    col = lax.broadcasted_iota(jnp.int32, (t, t), 1)
    return jnp.where(col <= row, s, -jnp.inf)


def _attn_fwd(name, q, k, v, scale, c_row=None):
    (qa, qo), (ka, ko), (va, vo) = q, k, v
    s_len = qa.shape[0]
    t = _att_tile(s_len)
    nt = s_len // t
    decay = c_row is not None
    q_mul, s_mul = _fold_scale(scale)

    def kern(*refs):
        q_ref, k_ref, v_ref = refs[:3]
        o_ref, lse_ref = refs[-2:]
        i = pl.program_id(1)
        qb = (q_ref[...] * q_mul).astype(bf16)

        def step(j, carry, diagonal):
            m, l, acc = carry
            rows = pl.ds(pl.multiple_of(j * t, t), t)
            kb = k_ref[rows, :].astype(bf16)
            vb = v_ref[rows, :].astype(bf16)
            s = _scores(qb, kb, s_mul, refs[3][j] if decay else None, diagonal, t)
            m_new = jnp.maximum(m, jnp.max(s, axis=1, keepdims=True))
            alpha = jnp.exp(m - m_new)
            p = jnp.exp(s - m_new)
            l = alpha * l + jnp.sum(p, axis=1, keepdims=True)
            acc = alpha * acc + jnp.dot(p.astype(bf16), vb, preferred_element_type=f32)
            return m_new, l, acc

        init = (jnp.full((t, 1), -jnp.inf, f32), jnp.zeros((t, 1), f32), jnp.zeros((t, LANE), f32))
        m, l, acc = step(i, lax.fori_loop(0, i, lambda j, c: step(j, c, False), init), True)
        o_ref[...] = acc / l
        lse_ref[...] = m + jnp.log(l)

    in_specs = [pl.BlockSpec((t, LANE), lambda h, i: (i, qo + h)),
                pl.BlockSpec((s_len, LANE), lambda h, i: (0, ko + h)),
                pl.BlockSpec((s_len, LANE), lambda h, i: (0, vo + h))]
    args = [qa, ka, va]
    if decay:
        in_specs.append(pl.BlockSpec((None, nt, 1, t), lambda h, i: (h, 0, 0, 0)))
        args.append(c_row)
    return pl.pallas_call(
        kern, name=name, grid=(HEADS, nt), in_specs=in_specs,
        out_specs=[pl.BlockSpec((t, LANE), lambda h, i: (i, h)), pl.BlockSpec((None, t, 1), lambda h, i: (h, i, 0))],
        out_shape=[jax.ShapeDtypeStruct((s_len, HEADS * LANE), f32), jax.ShapeDtypeStruct((HEADS, s_len, 1), f32)],
        compiler_params=pltpu.CompilerParams(dimension_semantics=("parallel", "arbitrary")))(*args)


def _attn_dq(name, q, k, v, o, do, lse, scale, c_row=None):
    (qa, qo), (ka, ko), (va, vo) = q, k, v
    s_len = qa.shape[0]
    t = _att_tile(s_len)
    nt = s_len // t
    decay = c_row is not None
    q_mul, s_mul = _fold_scale(scale)

    def kern(*refs):
        q_ref, k_ref, v_ref, o_ref, do_ref, lse_ref = refs[:6]
        dq_ref, delta_ref, drow_ref = refs[-3:]
        i = pl.program_id(1)
        qb = (q_ref[...] * q_mul).astype(bf16)
        dob = do_ref[...]
        delta = jnp.sum(dob * o_ref[...], axis=1, keepdims=True)
        dob = dob.astype(bf16)
        lse = lse_ref[...]

        def step(j, carry, diagonal):
            dq, drow = carry
            rows = pl.ds(pl.multiple_of(j * t, t), t)
            kb = k_ref[rows, :].astype(bf16)
            vb = v_ref[rows, :].astype(bf16)
            s = _scores(qb, kb, s_mul, refs[6][j] if decay else None, diagonal, t)
            p = jnp.exp(s - lse)
            dp = lax.dot_general(dob, vb, (_DOT_DIMS["nt"], ((), ())), preferred_element_type=f32)
            ds = p * (dp - delta)
            return dq + jnp.dot(ds.astype(bf16), kb, preferred_element_type=f32), drow + jnp.sum(ds, axis=1, keepdims=True)

        init = (jnp.zeros((t, LANE), f32), jnp.zeros((t, 1), f32))
        dq, drow = step(i, lax.fori_loop(0, i, lambda j, c: step(j, c, False), init), True)
        dq_ref[...] = dq * scale
        delta_ref[...] = delta
        drow_ref[...] = drow

    in_specs = [pl.BlockSpec((t, LANE), lambda h, i: (i, qo + h)),
                pl.BlockSpec((s_len, LANE), lambda h, i: (0, ko + h)),
                pl.BlockSpec((s_len, LANE), lambda h, i: (0, vo + h)),
                pl.BlockSpec((t, LANE), lambda h, i: (i, h)),
                pl.BlockSpec((t, LANE), lambda h, i: (i, h)),
                pl.BlockSpec((None, t, 1), lambda h, i: (h, i, 0))]
    args = [qa, ka, va, o, do, lse]
    if decay:
        in_specs.append(pl.BlockSpec((None, nt, 1, t), lambda h, i: (h, 0, 0, 0)))
        args.append(c_row)
    col = pl.BlockSpec((None, t, 1), lambda h, i: (h, i, 0))
    return pl.pallas_call(
        kern, name=name, grid=(HEADS, nt), in_specs=in_specs,
        out_specs=[pl.BlockSpec((t, LANE), lambda h, i: (i, h)), col, col],
        out_shape=[jax.ShapeDtypeStruct((s_len, HEADS * LANE), f32), jax.ShapeDtypeStruct((HEADS, s_len, 1), f32),
                   jax.ShapeDtypeStruct((HEADS, s_len, 1), f32)],
        compiler_params=pltpu.CompilerParams(dimension_semantics=("parallel", "arbitrary")))(*args)


def _attn_dkv(name, q, k, v, do, lse, delta, scale, c_row=None):
    (qa, qo), (ka, ko), (va, vo) = q, k, v
    s_len = qa.shape[0]
    t = _att_tile(s_len)
    nt = s_len // t
    decay = c_row is not None
    q_mul, s_mul = _fold_scale(scale)

    def kern(*refs):
        q_ref, k_ref, v_ref, do_ref, lse_ref, delta_ref = refs[:6]
        j = pl.program_id(1)
        kb = k_ref[...].astype(bf16)
        vb = v_ref[...].astype(bf16)
        ck = refs[6][...] if decay else None

        def step(i, carry, diagonal):
            dk, dv, dc = carry
            rows = pl.ds(pl.multiple_of(i * t, t), t)
            qb = (q_ref[rows, :] * q_mul).astype(bf16)
            dob = do_ref[rows, :].astype(bf16)
            s = _scores(qb, kb, s_mul, ck, diagonal, t)
            p = jnp.exp(s - lse_ref[rows, :])
            dv = dv + lax.dot_general(p.astype(bf16), dob, (_DOT_DIMS["tn"], ((), ())), preferred_element_type=f32)
            dp = lax.dot_general(dob, vb, (_DOT_DIMS["nt"], ((), ())), preferred_element_type=f32)
            ds = p * (dp - delta_ref[rows, :])
            dk = dk + lax.dot_general(ds.astype(bf16), qb, (_DOT_DIMS["tn"], ((), ())), preferred_element_type=f32)
            if decay:
                dc = dc - jnp.sum(ds, axis=0, keepdims=True)
            return dk, dv, dc

        init = (jnp.zeros((t, LANE), f32), jnp.zeros((t, LANE), f32), jnp.zeros((1, t), f32))
        dk, dv, dc = lax.fori_loop(j + 1, nt, lambda i, c: step(i, c, False), step(j, init, True))
        if decay:
            dk_ref, dv_ref, dc_ref = refs[-3:]
            dc_ref[...] = dc
        else:
            dk_ref, dv_ref = refs[-2:]
        dk_ref[...] = dk * s_mul
        dv_ref[...] = dv

    in_specs = [pl.BlockSpec((s_len, LANE), lambda h, j: (0, qo + h)),
                pl.BlockSpec((t, LANE), lambda h, j: (j, ko + h)),
                pl.BlockSpec((t, LANE), lambda h, j: (j, vo + h)),
                pl.BlockSpec((s_len, LANE), lambda h, j: (0, h)),
                pl.BlockSpec((None, s_len, 1), lambda h, j: (h, 0, 0)),
                pl.BlockSpec((None, s_len, 1), lambda h, j: (h, 0, 0))]
    args = [qa, ka, va, do, lse, delta]
    out_specs = [pl.BlockSpec((t, LANE), lambda h, j: (j, h)), pl.BlockSpec((t, LANE), lambda h, j: (j, h))]
    out_shape = [jax.ShapeDtypeStruct((s_len, HEADS * LANE), f32), jax.ShapeDtypeStruct((s_len, HEADS * LANE), f32)]
    if decay:
        in_specs.append(pl.BlockSpec((None, None, 1, t), lambda h, j: (h, j, 0, 0)))
        args.append(c_row)
        out_specs.append(pl.BlockSpec((None, None, 1, t), lambda h, j: (h, j, 0, 0)))
        out_shape.append(jax.ShapeDtypeStruct((HEADS, nt, 1, t), f32))
    return pl.pallas_call(
        kern, name=name, grid=(HEADS, nt), in_specs=in_specs, out_specs=out_specs, out_shape=out_shape,
        compiler_params=pltpu.CompilerParams(dimension_semantics=("parallel", "arbitrary")))(*args)


CONV_TS, CONV_CB = 1024, 256


def _conv_fwd(name, x, w, b, taps):
    xa, width, xidx = _view(x)
    s_len = xa.shape[0]
    ts, cb = min(CONV_TS, s_len), CONV_CB
    xo = xidx * width // cb

    def kern(x_ref, halo_ref, w_ref, b_ref, o_ref):
        i = pl.program_id(1)
        xb = x_ref[...]
        halo = jnp.where(i == 0, 0.0, halo_ref[...])
        xx = jnp.concatenate([halo, xb], axis=0)
        out = b_ref[...] + w_ref[taps - 1:taps, :] * xb
        for k in range(taps - 1):
            out = out + w_ref[k:k + 1, :] * pltpu.roll(xx, taps - 1 - k, 0)[SUBLANE:]
        o_ref[...] = out

    return pl.pallas_call(
        kern, name=name, grid=(width // cb, s_len // ts),
        in_specs=[pl.BlockSpec((ts, cb), lambda j, i: (i, xo + j)),
                  pl.BlockSpec((SUBLANE, cb), lambda j, i: (jnp.maximum(i * (ts // SUBLANE) - 1, 0), xo + j)),
                  pl.BlockSpec((taps, cb), lambda j, i: (0, j)),
                  pl.BlockSpec((1, cb), lambda j, i: (0, j))],
        out_specs=pl.BlockSpec((ts, cb), lambda j, i: (i, j)),
        out_shape=jax.ShapeDtypeStruct((s_len, width), f32),
        compiler_params=pltpu.CompilerParams(dimension_semantics=("parallel", "parallel")))(xa, xa, w, b)


def _conv_bwd(name, x, dout, w, taps, dout2=None, dx_dtype=f32):
    xa, width, xidx = _view(x)
    s_len = xa.shape[0]
    ts, cb = min(CONV_TS, s_len), CONV_CB
    xo = xidx * width // cb
    n_i = s_len // ts
    two = dout2 is not None

    def kern(*refs):
        x_ref, halo_ref, w_ref = refs[:3]
        dx_ref, dw_ref, db_ref = refs[-3:]
        i = pl.program_id(1)
        if two:
            d = refs[3][...] + refs[5][...]
            dn = refs[4][...] + refs[6][...]
        else:
            d, dn = refs[3][...], refs[4][...]
        dn = jnp.where(i == n_i - 1, 0.0, dn)
        xb = x_ref[...]
        halo = jnp.where(i == 0, 0.0, halo_ref[...])
        xx = jnp.concatenate([halo, xb], axis=0)
        dd = jnp.concatenate([d, dn], axis=0)

        @pl.when(i == 0)
        def _():
            dw_ref[...] = jnp.zeros_like(dw_ref)
            db_ref[...] = jnp.zeros_like(db_ref)

        dx = w_ref[taps - 1:taps, :] * d
        dw_ref[taps - 1:taps, :] += jnp.sum(d * xb, axis=0, keepdims=True)
        for k in range(taps - 1):
            sh = taps - 1 - k
            dx = dx + w_ref[k:k + 1, :] * pltpu.roll(dd, ts + SUBLANE - sh, 0)[:ts]
            dw_ref[k:k + 1, :] += jnp.sum(d * pltpu.roll(xx, sh, 0)[SUBLANE:], axis=0, keepdims=True)
        dx_ref[...] = dx.astype(dx_ref.dtype)
        db_ref[...] += jnp.sum(d, axis=0, keepdims=True)

    d_spec = pl.BlockSpec((ts, cb), lambda j, i: (i, j))
    dn_spec = pl.BlockSpec((SUBLANE, cb), lambda j, i: (jnp.minimum((i + 1) * (ts // SUBLANE), s_len // SUBLANE - 1), j))
    in_specs = [pl.BlockSpec((ts, cb), lambda j, i: (i, xo + j)),
                pl.BlockSpec((SUBLANE, cb), lambda j, i: (jnp.maximum(i * (ts // SUBLANE) - 1, 0), xo + j)),
                pl.BlockSpec((taps, cb), lambda j, i: (0, j)), d_spec, dn_spec]
    args = [xa, xa, w, dout, dout]
    if two:
        in_specs += [d_spec, dn_spec]
        args += [dout2, dout2]
    return pl.pallas_call(
        kern, name=name, grid=(width // cb, n_i), in_specs=in_specs,
        out_specs=[pl.BlockSpec((ts, cb), lambda j, i: (i, j)), pl.BlockSpec((taps, cb), lambda j, i: (0, j)),
                   pl.BlockSpec((1, cb), lambda j, i: (0, j))],
        out_shape=[jax.ShapeDtypeStruct((s_len, width), dx_dtype), jax.ShapeDtypeStruct((taps, width), f32),
                   jax.ShapeDtypeStruct((1, width), f32)],
        compiler_params=pltpu.CompilerParams(dimension_semantics=("parallel", "arbitrary")))(*args)


def _segment_carries(a_last, h_last, reverse):
    ridx = lax.broadcasted_iota(jnp.int32, (SUBLANE, LANE), 0)

    def pick(m, s):
        return jnp.sum(jnp.where(ridx == s, m, 0.0), axis=0, keepdims=True)

    carry = jnp.zeros((SUBLANE, LANE), f32)
    prev = jnp.zeros((1, LANE), f32)
    order = range(SUBLANE - 2, -1, -1) if reverse else range(1, SUBLANE)
    for s in order:
        src = s + 1 if reverse else s - 1
        prev = pick(a_last, src) * prev + pick(h_last, src)
        carry = jnp.where(ridx == s, prev, carry)
    return carry


def _scan_fwd(name, a, b):
    s_len, width = a.shape
    seg = s_len // SUBLANE

    def kern(a_ref, b_ref, h_ref, ap_ref):
        def p1(t, c):
            h, acc = c
            idx = pl.ds(t, SUBLANE, stride=seg)
            av = a_ref[idx, :]
            h = av * h + b_ref[idx, :]
            acc = av * acc
            h_ref[idx, :] = h
            ap_ref[idx, :] = acc
            return h, acc

        h_last, a_last = lax.fori_loop(0, seg, p1, (jnp.zeros((SUBLANE, LANE), f32), jnp.ones((SUBLANE, LANE), f32)))
        carry = _segment_carries(a_last, h_last, False)

        def p3(t, c):
            idx = pl.ds(t, SUBLANE, stride=seg)
            h_ref[idx, :] = h_ref[idx, :] + ap_ref[idx, :] * carry
            return c

        lax.fori_loop(0, seg, p3, 0)

    spec = pl.BlockSpec((s_len, LANE), lambda j: (0, j))
    return pl.pallas_call(
        kern, name=name, grid=(width // LANE,), in_specs=[spec, spec], out_specs=spec,
        out_shape=jax.ShapeDtypeStruct((s_len, width), f32), scratch_shapes=[pltpu.VMEM((s_len, LANE), f32)],
        compiler_params=pltpu.CompilerParams(dimension_semantics=("parallel",)))(a, b)


def _scan_bwd(name, a_next, h_prev, dh):
    s_len, width = dh.shape
    seg = s_len // SUBLANE

    def kern(an_ref, hp_ref, dh_ref, da_ref, db_ref, ap_ref):
        def p1(tt, c):
            g, acc = c
            idx = pl.ds(seg - 1 - tt, SUBLANE, stride=seg)
            av = an_ref[idx, :]
            g = av * g + dh_ref[idx, :]
            acc = av * acc
            db_ref[idx, :] = g
            ap_ref[idx, :] = acc
            return g, acc

        g_last, a_last = lax.fori_loop(0, seg, p1, (jnp.zeros((SUBLANE, LANE), f32), jnp.ones((SUBLANE, LANE), f32)))
        carry = _segment_carries(a_last, g_last, True)

        def p3(t, c):
            idx = pl.ds(t, SUBLANE, stride=seg)
            g = db_ref[idx, :] + ap_ref[idx, :] * carry
            db_ref[idx, :] = g
            da_ref[idx, :] = g * hp_ref[idx, :]
            return c

        lax.fori_loop(0, seg, p3, 0)

    spec = pl.BlockSpec((s_len, LANE), lambda j: (0, j))
    return pl.pallas_call(
        kern, name=name, grid=(width // LANE,), in_specs=[spec, spec, spec], out_specs=[spec, spec],
        out_shape=[jax.ShapeDtypeStruct((s_len, width), f32)] * 2, scratch_shapes=[pltpu.VMEM((s_len, LANE), f32)],
        compiler_params=pltpu.CompilerParams(dimension_semantics=("parallel",)))(a_next, h_prev, dh)


def _lane_cumsum(x, reverse):
    n = x.shape[1]
    lane = lax.broadcasted_iota(jnp.int32, x.shape, 1)
    sh = 1
    while sh < n:
        if reverse:
            x = x + jnp.where(lane < n - sh, pltpu.roll(x, n - sh, 1), 0.0)
        else:
            x = x + jnp.where(lane >= sh, pltpu.roll(x, sh, 1), 0.0)
        sh *= 2
    return x


def _decay_fwd(name, fl_t, b8):
    def kern(f_ref, b_ref, c_ref):
        c_ref[...] = _lane_cumsum(jax.nn.log_sigmoid(f_ref[...] + b_ref[...]), False)

    return pl.pallas_call(kern, name=name, out_shape=jax.ShapeDtypeStruct(fl_t.shape, f32))(fl_t, b8)


def _decay_bwd(name, fl_t, b8, dc_key, dc_query):
    def kern(f_ref, b_ref, dck_ref, dcq_ref, df_ref, db_ref):
        dlogf = _lane_cumsum(dck_ref[...] + dcq_ref[...], True)
        df = dlogf * jax.nn.sigmoid(-(f_ref[...] + b_ref[...]))
        df_ref[...] = df
        db_ref[...] = jnp.sum(df, axis=1, keepdims=True)

    return pl.pallas_call(kern, name=name, out_shape=[jax.ShapeDtypeStruct(fl_t.shape, f32),
                                                      jax.ShapeDtypeStruct((SUBLANE, 1), f32)])(fl_t, b8, dc_key, dc_query)


def _rms(x, g, n):
    return x * lax.rsqrt(jnp.sum(x * x, axis=-1, keepdims=True) * (1.0 / n) + EPS) * g


def _loss_head(name, h, target, g, tb=256):
    n, d = h.shape
    tb = min(tb, n)

    def kern(h_ref, t_ref, g_ref, loss_ref, dh_ref, dg_ref):
        i = pl.program_id(0)
        tgt = t_ref[...]

        def f(hv, gv):
            err = _rms(hv, gv, d) - tgt
            return 0.5 * jnp.sum(jnp.sum(err * err, axis=-1, keepdims=True) * (1.0 / d), axis=0, keepdims=True)

        val, vjp = jax.vjp(f, h_ref[...], g_ref[...])
        dh, dg = vjp(jnp.ones((1, 1), f32))
        dh_ref[...] = dh

        @pl.when(i == 0)
        def _():
            loss_ref[...] = jnp.zeros_like(loss_ref)
            dg_ref[...] = jnp.zeros_like(dg_ref)

        loss_ref[...] += val
        dg_ref[...] += dg

    return pl.pallas_call(
        kern, name=name, grid=(n // tb,),
        in_specs=[pl.BlockSpec((tb, d), lambda i: (i, 0)), pl.BlockSpec((tb, d), lambda i: (i, 0)),
                  pl.BlockSpec((1, d), lambda i: (0, 0))],
        out_specs=[pl.BlockSpec((1, 1), lambda i: (0, 0)), pl.BlockSpec((tb, d), lambda i: (i, 0)),
                   pl.BlockSpec((1, d), lambda i: (0, 0))],
        out_shape=[jax.ShapeDtypeStruct((1, 1), f32), jax.ShapeDtypeStruct((n, d), f32), jax.ShapeDtypeStruct((1, d), f32)],
        compiler_params=pltpu.CompilerParams(dimension_semantics=("arbitrary",)))(h, target, g)


def _f_norm(x, g):
    return (_rms(x, g, D_MODEL),)


def _f_latent(qc, kvc, gq, gkv):
    return _rms(qc, gq, MLA_Q_RANK), _rms(kvc, gkv, MLA_KV_RANK)


def _rope(x, pos, freq, m1, m2):
    ang = pos * freq
    sin = jnp.sin(ang)
    w = x.shape[1]
    return x * jnp.cos(ang) - _roll(x, w - MLA_ROPE // 2, 1) * (sin * m1) + _roll(x, MLA_ROPE // 2, 1) * (sin * m2)


def _f_mla_prep(q, kpart, kr, pos, fq, m1q, m2q, fk, m1k, m2k):
    kr = _rope(kr, pos, fk, m1k, m2k)
    return _rope(q, pos, fq, m1q, m2q), kpart + jnp.concatenate([kr] * HEADS, axis=1)


def _f_lru_gate(gates, xc, b_r, b_i, lam):
    r = jax.nn.sigmoid(gates[:, :LRU_WIDTH] + b_r)
    i = jax.nn.sigmoid(gates[:, LRU_WIDTH:] + b_i)
    log_a = -LRU_C * r * jax.nn.softplus(-lam)
    mult = jnp.sqrt(-jnp.tanh(log_a) * (1.0 + jnp.exp(2.0 * log_a)))
    return jnp.exp(log_a), mult * (i * xc)


def _f_merge(o_mla, o_fox, hs, lg, g):
    o_lru = hs * jax.nn.gelu(lg)
    return (jnp.concatenate([_rms(o_mla, g[:, :512], HEADS * MLA_V), _rms(o_fox, g[:, 512:1024], HEADS * FOX_HEAD_DIM),
                             _rms(o_lru, g[:, 1024:], LRU_WIDTH)], axis=1),)


def _f_ffn_gate(u):
    return (jax.nn.silu(u[:, :D_FF]) * u[:, D_FF:],)


def _f_ple(h, gpre, pp):
    return (h + jax.nn.sigmoid(gpre) * pp,)


def _prep_layer_weights(w):
    eye = jnp.eye(LRU_BLOCKS, dtype=f32)

    def block_diag(m):
        return (eye[:, None, :, None] * m[:, :, None, :]).reshape(LRU_WIDTH, LRU_WIDTH)

    return dict(
        w_in=_take_pad(w["w_in"], Z_MAP, 1),
        w_uq=_take_pad(_take_pad(w["w_uq"], UQ_COL_MAP, 1), UQ_ROW_MAP, 0),
        w_ukv=_take_pad(w["w_ukv"], UKV_MAP, 1),
        w_ri=jnp.concatenate([block_diag(w["w_r"]), block_diag(w["w_i"])], axis=1).astype(bf16),
        w_o=_take_pad(w["w_o"], OMIX_MAP, 0),
        w_up=w["w_up"], w_down=w["w_down"], w_ple_gate=w["w_ple_gate"], w_ple_proj=w["w_ple_proj"],
        g_mix=w["g_mix"].reshape(1, -1), g_ffn=w["g_ffn"].reshape(1, -1), g_ple=w["g_ple"].reshape(1, -1),
        g_qc=_take_pad(w["g_qc"], UQ_ROW_MAP, 0).reshape(1, -1), g_kvc=w["g_kvc"].reshape(1, -1),
        g_out=_take_pad(w["g_out"], OMIX_MAP, 0).reshape(1, -1),
        b_f8=_take_pad(w["b_f"], _pad_to(np.arange(FOX_HEADS), SUBLANE), 0).reshape(SUBLANE, 1),
        lru_conv_w=w["lru_conv_w"], lru_conv_b=w["lru_conv_b"].reshape(1, -1),
        b_r=w["b_r"].reshape(1, -1), b_i=w["b_i"].reshape(1, -1), lam=w["lru_lambda"].reshape(1, -1),
        ffn_conv_w=w["ffn_conv_w"], ffn_conv_b=w["ffn_conv_b"].reshape(1, -1),
    )


def _rope_consts():
    fq, m1q, m2q = _rope_tables(HEADS * LANE, ROPE_AT)
    fk, m1k, m2k = _rope_tables(LANE, ROPE_AT)
    return [jnp.asarray(t) for t in (fq, m1q, m2q, fk, m1k, m2k)]


def _key_decay(c_t, s_len):
    t = _att_tile(s_len)
    return c_t[:HEADS].reshape(HEADS, s_len // t, 1, t)


def _layer_fwd(l, h0, p_l, pos, w):
    s_len = h0.shape[0]
    n = f"l{l}_"
    xn, = _rowwise(n + "norm_mix", _f_norm, [h0], [w["g_mix"]], [(D_MODEL, bf16)])
    z = _mm(n + "in_proj", xn, w["w_in"])
    zq = (z, QC_W, Z_QC // QC_W)
    zkv = (z, LANE, Z_KVC // LANE)
    zkr = (z, LANE, Z_KR // LANE)
    zlx = (z, LRU_WIDTH, Z_LX // LRU_WIDTH)
    zlg = (z, LRU_WIDTH, Z_LG // LRU_WIDTH)
    qcn, kvn = _rowwise(n + "latent_norm", _f_latent, [zq, zkv], [w["g_qc"], w["g_kvc"]], [(QC_W, bf16), (LANE, bf16)])
    q = _mm(n + "uq", qcn, w["w_uq"])
    kv = _mm(n + "ukv", kvn, w["w_ukv"])
    kpart = (kv, HEADS * LANE, 0)
    qr, kk = _rowwise(n + "mla_prep", _f_mla_prep, [q, kpart, zkr, pos], _rope_consts(),
                      [(HEADS * LANE, bf16), (HEADS * LANE, bf16)])
    mla_scale = (MLA_NOPE + MLA_ROPE) ** -0.5
    o_mla, lse_m = _attn_fwd(n + "mla_fwd", (qr, 0), (kk, 0), (kv, HEADS), mla_scale)
    fl_t = z[:, Z_FL:Z_FL + SUBLANE].T
    c_t = _decay_fwd(n + "decay", fl_t, w["b_f8"])
    c_row = _key_decay(c_t, s_len)
    fox_scale = FOX_HEAD_DIM ** -0.5
    o_fox, lse_f = _attn_fwd(n + "fox_fwd", (z, Z_FQ // LANE), (z, Z_FK // LANE), (z, Z_FV // LANE), fox_scale, c_row)
    xc = _conv_fwd(n + "lru_conv", zlx, w["lru_conv_w"], w["lru_conv_b"], LRU_CONV)
    gates = _mm(n + "lru_gates", xc, w["w_ri"])
    a, bx = _rowwise(n + "lru_gate", _f_lru_gate, [gates, xc], [w["b_r"], w["b_i"], w["lam"]],
                     [(LRU_WIDTH, f32), (LRU_WIDTH, f32)])
    hs = _scan_fwd(n + "lru_scan", a, bx)
    ocat, = _rowwise(n + "merge", _f_merge, [o_mla, o_fox, hs, zlg], [w["g_out"]], [(OMIX_W, bf16)])
    h1 = _mm(n + "out_proj", ocat, w["w_o"], res=h0)
    xn2, = _rowwise(n + "norm_ffn", _f_norm, [h1], [w["g_ffn"]], [(D_MODEL, bf16)])
    up = _mm(n + "up_proj", xn2, w["w_up"])
    u = _conv_fwd(n + "ffn_conv", up, w["ffn_conv_w"], w["ffn_conv_b"], FFN_CONV)
    act, = _rowwise(n + "ffn_gate", _f_ffn_gate, [u], [], [(D_FF, bf16)])
    h2 = _mm(n + "down_proj", act, w["w_down"], res=h1)
    hn, = _rowwise(n + "norm_ple", _f_norm, [h2], [w["g_ple"]], [(D_MODEL, bf16)])
    gpre = _mm(n + "ple_gate", hn, w["w_ple_gate"])
    pp = _mm(n + "ple_proj", p_l, w["w_ple_proj"])
    h3, = _rowwise(n + "ple_mix", _f_ple, [h2, gpre, pp], [], [(D_MODEL, f32)])
    res = dict(h0=h0, xn=xn, z=z, qcn=qcn, kvn=kvn, q=q, kv=kv, qr=qr, kk=kk, o_mla=o_mla, lse_m=lse_m, fl_t=fl_t,
               c_row=c_row, o_fox=o_fox, lse_f=lse_f, xc=xc, gates=gates, a=a, hs=hs, ocat=ocat, h1=h1,
               xn2=xn2, up=up, u=u, act=act, h2=h2, hn=hn, gpre=gpre, pp=pp, p_l=p_l)
    return h3, res


def _layer_bwd(l, dh3, r, pos, w):
    s_len = dh3.shape[0]
    n = f"l{l}_"
    g = {}
    z = r["z"]
    zq = (z, QC_W, Z_QC // QC_W)
    zkv = (z, LANE, Z_KVC // LANE)
    zkr = (z, LANE, Z_KR // LANE)
    zlx = (z, LRU_WIDTH, Z_LX // LRU_WIDTH)
    zlg = (z, LRU_WIDTH, Z_LG // LRU_WIDTH)
    (dh2a, dgpre, dpp), _ = _rowwise_bwd(n + "ple_mix_b", _f_ple, [r["h2"], r["gpre"], r["pp"]], [], [dh3], 3,
                                         dts=[f32, bf16, bf16])
    g["w_ple_proj"] = _mm(n + "ple_proj_dw", r["p_l"], dpp, "tn", bf16)
    dhn = _mm(n + "ple_gate_dx", dgpre, w["w_ple_gate"], "nt")
    g["w_ple_gate"] = _mm(n + "ple_gate_dw", r["hn"], dgpre, "tn", bf16)
    (dh2,), (g["g_ple"],) = _rowwise_bwd(n + "norm_ple_b", _f_norm, [r["h2"]], [w["g_ple"]], [dhn], 1, adds={0: dh2a})
    dact = _mm(n + "down_dx", dh2, w["w_down"], "nt")
    g["w_down"] = _mm(n + "down_dw", r["act"], dh2, "tn", bf16)
    u = r["u"]
    (du,), _ = _rowwise_bwd(n + "ffn_gate_b", _f_ffn_gate, [u], [], [dact], 1)
    dup, g["ffn_conv_w"], g["ffn_conv_b"] = _conv_bwd(n + "ffn_conv_b", r["up"], du, w["ffn_conv_w"], FFN_CONV,
                                                             dx_dtype=bf16)
    dxn2 = _mm(n + "up_dx", dup, w["w_up"], "nt")
    g["w_up"] = _mm(n + "up_dw", r["xn2"], dup, "tn", bf16)
    (dh1,), (g["g_ffn"],) = _rowwise_bwd(n + "norm_ffn_b", _f_norm, [r["h1"]], [w["g_ffn"]], [dxn2], 1, adds={0: dh2})
    docat = _mm(n + "out_dx", dh1, w["w_o"], "nt")
    g["w_o"] = _mm(n + "out_dw", r["ocat"], dh1, "tn", bf16)
    (do_mla, do_fox, dhs, dlg), (g["g_out"],) = _rowwise_bwd(
        n + "merge_b", _f_merge, [r["o_mla"], r["o_fox"], r["hs"], zlg], [w["g_out"]], [docat], 4)
    a, hs = r["a"], r["hs"]
    a_next = jnp.concatenate([a[1:], jnp.zeros((1, LRU_WIDTH), f32)], axis=0)
    h_prev = jnp.concatenate([jnp.zeros((1, LRU_WIDTH), f32), hs[:-1]], axis=0)
    da, dbx = _scan_bwd(n + "lru_scan_b", a_next, h_prev, dhs)
    (dgates, dxc_a), (g["b_r"], g["b_i"], g["lam"]) = _rowwise_bwd(
        n + "lru_gate_b", _f_lru_gate, [r["gates"], r["xc"]], [w["b_r"], w["b_i"], w["lam"]], [da, dbx], 2,
        dts=[bf16, f32])
    dxc_b = _mm(n + "lru_gates_dx", dgates, w["w_ri"], "nt")
    g["w_ri"] = _mm(n + "lru_gates_dw", r["xc"], dgates, "tn")
    dlx, g["lru_conv_w"], g["lru_conv_b"] = _conv_bwd(n + "lru_conv_b", zlx, dxc_a, w["lru_conv_w"], LRU_CONV, dout2=dxc_b)
    fox_scale = FOX_HEAD_DIM ** -0.5
    fq, fk, fv = (z, Z_FQ // LANE), (z, Z_FK // LANE), (z, Z_FV // LANE)
    dfq, delta_f, dc_q = _attn_dq(n + "fox_dq", fq, fk, fv, r["o_fox"], do_fox, r["lse_f"], fox_scale, r["c_row"])
    dfk, dfv, dc_k = _attn_dkv(n + "fox_dkv", fq, fk, fv, do_fox, r["lse_f"], delta_f, fox_scale, r["c_row"])
    pad_rows = jnp.zeros((SUBLANE - HEADS, s_len), f32)
    dfl_t, g["b_f8"] = _decay_bwd(n + "decay_b", r["fl_t"], w["b_f8"],
                                  jnp.concatenate([dc_k.reshape(HEADS, s_len), pad_rows], axis=0),
                                  jnp.concatenate([dc_q.reshape(HEADS, s_len), pad_rows], axis=0))
    dfl = jnp.pad(dfl_t.T, ((0, 0), (0, LANE - SUBLANE)))
    mla_scale = (MLA_NOPE + MLA_ROPE) ** -0.5
    qr, kk, kv = (r["qr"], 0), (r["kk"], 0), (r["kv"], HEADS)
    dqr, delta_m, _ = _attn_dq(n + "mla_dq", qr, kk, kv, r["o_mla"], do_mla, r["lse_m"], mla_scale)
    dkk, dv_m = _attn_dkv(n + "mla_dkv", qr, kk, kv, do_mla, r["lse_m"], delta_m, mla_scale)
    (dq, dkpart, dkr), _ = _rowwise_bwd(n + "mla_prep_b", _f_mla_prep, [r["q"], (r["kv"], HEADS * LANE, 0), zkr, pos],
                                        _rope_consts(), [dqr, dkk], 3, dts=[bf16, bf16, f32])
    dkv = jnp.concatenate([dkpart, dv_m.astype(bf16)], axis=1)
    dkvn = _mm(n + "ukv_dx", dkv, w["w_ukv"], "nt")
    g["w_ukv"] = _mm(n + "ukv_dw", r["kvn"], dkv, "tn", bf16)
    dqcn = _mm(n + "uq_dx", dq, w["w_uq"], "nt")
    g["w_uq"] = _mm(n + "uq_dw", r["qcn"], dq, "tn", bf16)
    (dqc, dkvc), (g["g_qc"], g["g_kvc"]) = _rowwise_bwd(n + "latent_norm_b", _f_latent, [zq, zkv],
                                                        [w["g_qc"], w["g_kvc"]], [dqcn, dkvn], 2)
    dz = jnp.concatenate([t.astype(bf16) for t in (dfq, dfk, dfv, dlx, dlg, dqc, dkvc, dkr, dfl)], axis=1)
    dxn = _mm(n + "in_dx", dz, w["w_in"], "nt")
    g["w_in"] = _mm(n + "in_dw", r["xn"], dz, "tn", bf16)
    (dh0,), (g["g_mix"],) = _rowwise_bwd(n + "norm_mix_b", _f_norm, [r["h0"]], [w["g_mix"]], [dxn], 1, adds={0: dh1})
    return dh0, g


def _unpad_layer_grads(g):
    d_ri = g["w_ri"]
    idx = jnp.arange(LRU_BLOCKS)

    def diag_blocks(m):
        return m.reshape(LRU_BLOCKS, LRU_BLOCK, LRU_BLOCKS, LRU_BLOCK)[idx, :, idx, :]

    return dict(
        g_mix=g["g_mix"][0], w_in=_take_inv(g["w_in"], Z_MAP, 1), g_qc=g["g_qc"][0, :MLA_Q_RANK],
        w_uq=_take_inv(g["w_uq"][:MLA_Q_RANK], UQ_COL_MAP, 1), g_kvc=g["g_kvc"][0],
        w_ukv=_take_inv(g["w_ukv"], UKV_MAP, 1), b_f=g["b_f8"][:FOX_HEADS, 0],
        lru_conv_w=g["lru_conv_w"], lru_conv_b=g["lru_conv_b"][0],
        w_r=diag_blocks(d_ri[:, :LRU_WIDTH]), b_r=g["b_r"][0], w_i=diag_blocks(d_ri[:, LRU_WIDTH:]), b_i=g["b_i"][0],
        lru_lambda=g["lam"][0], g_out=_take_inv(g["g_out"][0], OMIX_MAP, 0), w_o=_take_inv(g["w_o"], OMIX_MAP, 0),
        g_ffn=g["g_ffn"][0], w_up=g["w_up"], ffn_conv_w=g["ffn_conv_w"], ffn_conv_b=g["ffn_conv_b"][0],
        w_down=g["w_down"], g_ple=g["g_ple"][0], w_ple_gate=g["w_ple_gate"], w_ple_proj=g["w_ple_proj"],
    )


LAYER_WEIGHTS = ["g_mix", "w_in", "g_qc", "w_uq", "g_kvc", "w_ukv", "b_f", "lru_conv_w", "lru_conv_b", "w_r", "b_r", "w_i",
                 "b_i", "lru_lambda", "g_out", "w_o", "g_ffn", "w_up", "ffn_conv_w", "ffn_conv_b", "w_down", "g_ple",
                 "w_ple_gate", "w_ple_proj"]
WEIGHTS = LAYER_WEIGHTS + ["g_final"]


def _local_step(x, p, pos, target, g_final, weights_of, grads_to):
    h = x
    ws, saved = [], []
    for l in range(DEPTH):
        w = _prep_layer_weights(weights_of(l, h))
        h, r = _layer_fwd(l, h, p[l], pos, w)
        ws.append(w)
        saved.append(r)
    loss, dh, dg_final = _loss_head("loss_head", h, target, g_final.reshape(1, -1))
    for l in reversed(range(DEPTH)):
        dh, g = _layer_bwd(l, dh, saved[l], pos, ws[l])
        dh = grads_to(l, _unpad_layer_grads(g), dh)
    return loss[0, 0], dh, dg_final[0]


MESH_AXES = ("x", "y", "c")


def _exchange(name, src, axes, scatter, pieces=1):
    n = 2 ** len(axes)
    flips = [tuple((f >> (len(axes) - 1 - b)) & 1 for b in range(len(axes))) for f in range(1, n)]
    rows = src.shape[-2]
    piece_rows = rows // pieces
    assert piece_rows * pieces == rows

    def body(src_ref, out_ref, send_sems, recv_sems, local_sem):
        coords = {a: lax.axis_index(a) for a in MESH_AXES}

        def index_of(cd):
            idx = 0
            for a in axes:
                idx = idx * 2 + cd[a]
            return idx

        me = index_of(coords)
        local = pltpu.make_async_copy(src_ref.at[me] if scatter else src_ref, out_ref.at[me], local_sem)
        local.start()
        copies = []
        for k, f in enumerate(flips):
            peer = dict(coords)
            for a, bit in zip(axes, f):
                if bit:
                    peer[a] = 1 - coords[a]
            slab = src_ref.at[index_of(peer)] if scatter else src_ref
            for pc in range(pieces):
                span = pl.ds(pc * piece_rows, piece_rows)
                cp = pltpu.make_async_remote_copy(
                    src_ref=slab.at[span], dst_ref=out_ref.at[me, span],
                    send_sem=send_sems.at[k * pieces + pc], recv_sem=recv_sems.at[k * pieces + pc],
                    device_id=tuple(peer[a] for a in MESH_AXES), device_id_type=pl.DeviceIdType.MESH)
                cp.start()
                copies.append(cp)
        for cp in copies:
            cp.wait()
        local.wait()

    n_sems = (n - 1) * pieces
    return pl.pallas_call(
        body, name=name, out_shape=jax.ShapeDtypeStruct((n, rows, LANE), src.dtype),
        in_specs=[pl.BlockSpec(memory_space=pl.ANY)], out_specs=pl.BlockSpec(memory_space=pl.ANY),
        scratch_shapes=[pltpu.SemaphoreType.DMA((n_sems,)), pltpu.SemaphoreType.DMA((n_sems,)), pltpu.SemaphoreType.DMA])(src)


def _row_tile(rows, cap):
    if rows <= cap:
        return rows
    for t in range(cap, SUBLANE - 1, -SUBLANE):
        if rows % t == 0:
            return t
    return rows


def _sum_slabs(name, a):
    n, rows, _ = a.shape
    tr = _row_tile(rows, 512)

    def kern(a_ref, o_ref):
        acc = a_ref[0].astype(f32)
        for k in range(1, n):
            acc = acc + a_ref[k].astype(f32)
        o_ref[...] = acc

    return pl.pallas_call(
        kern, name=name, grid=(rows // tr,), in_specs=[pl.BlockSpec((n, tr, LANE), lambda i: (0, i, 0))],
        out_specs=pl.BlockSpec((tr, LANE), lambda i: (i, 0)), out_shape=jax.ShapeDtypeStruct((rows, LANE), f32),
        compiler_params=pltpu.CompilerParams(dimension_semantics=("parallel",)))(a)


ADAM_BLOCK_BYTES = 2 ** 20


def _adamw(name, w, g, m, v):
    rows, cols = w.shape
    tr = _row_tile(rows, max(SUBLANE, ADAM_BLOCK_BYTES // (4 * cols) // SUBLANE * SUBLANE))

    def kern(w_ref, g_ref, m_ref, v_ref, d_ref, nm_ref, nv_ref):
        gv = g_ref[...]
        nm = ADAM_B1 * m_ref[...] + (1.0 - ADAM_B1) * gv
        nv = ADAM_B2 * v_ref[...] + (1.0 - ADAM_B2) * (gv * gv)
        m_hat = nm / (1.0 - ADAM_B1 ** ADAM_STEP)
        v_hat = nv / (1.0 - ADAM_B2 ** ADAM_STEP)
        d_ref[...] = -ADAM_LR * (m_hat / (jnp.sqrt(v_hat) + ADAM_EPS) + ADAM_WD * w_ref[...])
        nm_ref[...] = nm
        nv_ref[...] = nv

    spec = pl.BlockSpec((tr, cols), lambda i: (i, 0))
    return pl.pallas_call(
        kern, name=name, grid=(rows // tr,), in_specs=[spec] * 4, out_specs=[spec] * 3,
        out_shape=[jax.ShapeDtypeStruct((rows, cols), f32)] * 3,
        compiler_params=pltpu.CompilerParams(dimension_semantics=("parallel",)))(w, g, m, v)


def _pack(arrays, row_multiple):
    flat = jnp.concatenate([a.reshape(-1) for a in arrays])
    per = LANE * row_multiple
    total = -(-flat.shape[0] // per) * per
    return jnp.pad(flat, (0, total - flat.shape[0])).reshape(-1, LANE)


def _unpack(buf, shapes):
    flat = buf.reshape(-1)
    out, at = [], 0
    for s in shapes:
        size = int(np.prod(s))
        out.append(flat[at:at + size].reshape(s))
        at += size
    return out


SHARD_AXIS = {"w_in": 2, "w_uq": 2, "w_ukv": 2, "lru_conv_w": 2, "w_o": 1, "w_up": 2, "ffn_conv_w": 2, "w_down": 1,
              "w_ple_gate": 1, "w_ple_proj": 2}
SHARDED = [k for k in WEIGHTS if k in SHARD_AXIS]
REPLICATED = [k for k in WEIGHTS if k not in SHARD_AXIS]
ELEMENTWISE_F32 = ("lru_conv_w", "ffn_conv_w")
N_SHARDS = 4
BF16_TILE_ROWS = 16


HBM_SPEC = pl.BlockSpec(memory_space=pl.ANY)
SEM_SPEC = pl.BlockSpec(memory_space=pltpu.SEMAPHORE)
SPLIT_EFFECT = pltpu.SideEffectType.DATAFLOW_SIDE_EFFECTING
CHIP_FLIPS = ((1, 0), (0, 1), (1, 1))
N_DEVICES = 8
SUM_BLOCK_BYTES = 4 * 2 ** 20


def _device_index():
    return 4 * lax.axis_index("x") + 2 * lax.axis_index("y") + lax.axis_index("c")


def _when(cond, fn):
    if cond is None:
        fn()
    else:
        pl.when(cond)(fn)


class _Exchange:
    def __init__(self, name, plan, srcs, land_shapes, n_send, n_recv):
        self.name, self.plan, self.srcs, self.n = name, plan, list(srcs), len(srcs)
        self.land_shapes, self.n_send, self.n_recv = land_shapes, n_send, n_recv

    def run(self):
        n = self.n

        def body(*refs):
            sends, arrivals = self.plan(refs[:n], refs[n:2 * n], refs[2 * n], refs[2 * n + 1])
            for cond, cp in sends:
                _when(cond, cp.start)
            for cond, cp in arrivals:
                _when(cond, cp.wait_recv)
            for cond, cp in sends:
                _when(cond, cp.wait_send)

        return pl.pallas_call(
            body, name=self.name, out_shape=self.land_shapes, in_specs=[HBM_SPEC] * n, out_specs=[HBM_SPEC] * n,
            scratch_shapes=[pltpu.SemaphoreType.DMA((self.n_send,)), pltpu.SemaphoreType.DMA((self.n_recv,))])(*self.srcs)

    def start(self, after):
        n = self.n
        lands = [lax.empty(s.shape, s.dtype) for s in self.land_shapes]

        def body(*refs):
            ins, lands_in = refs[:n], refs[n:2 * n]
            send_sems, recv_sems, token = refs[2 * n + 1], refs[2 * n + 2], refs[-1]
            sends, _ = self.plan(ins, lands_in, send_sems, recv_sems)
            for cond, cp in sends:
                _when(cond, cp.start)
            token[...] = jnp.zeros_like(token)

        hbm = [pltpu.with_memory_space_constraint(a, pltpu.HBM) for a in self.srcs + lands]
        res = pl.pallas_call(
            body, name=self.name + "_start",
            out_shape=(pltpu.SemaphoreType.DMA((self.n_send,)), pltpu.SemaphoreType.DMA((self.n_recv,)),
                       *[pltpu.HBM(a.shape, a.dtype) for a in hbm], jax.ShapeDtypeStruct((SUBLANE, LANE), f32)),
            in_specs=[HBM_SPEC] * (2 * n + 1),
            out_specs=(SEM_SPEC, SEM_SPEC, *[HBM_SPEC] * (2 * n), pl.BlockSpec(memory_space=pltpu.VMEM)),
            input_output_aliases={i: 2 + i for i in range(2 * n)},
            compiler_params=pltpu.CompilerParams(has_side_effects=SPLIT_EFFECT))(*hbm, after)
        self.sems, self.thru, token = res[:2], res[2:2 + 2 * n], res[-1]
        return token[0, 0]

    def finish(self, after):
        n = self.n

        def body(*refs):
            ins, lands_in, send_sems, recv_sems = refs[:n], refs[n:2 * n], refs[2 * n], refs[2 * n + 1]
            sends, arrivals = self.plan(ins, lands_in, send_sems, recv_sems)
            for cond, cp in arrivals:
                _when(cond, cp.wait_recv)
            for cond, cp in sends:
                _when(cond, cp.wait_send)

        res = pl.pallas_call(
            body, name=self.name + "_finish", out_shape=tuple(pltpu.HBM(a.shape, a.dtype) for a in self.thru),
            in_specs=[HBM_SPEC] * (2 * n) + [SEM_SPEC, SEM_SPEC, HBM_SPEC], out_specs=tuple([HBM_SPEC] * (2 * n)),
            input_output_aliases={i: i for i in range(2 * n)},
            compiler_params=pltpu.CompilerParams(has_side_effects=SPLIT_EFFECT))(*self.thru, *self.sems, after)
        return list(res[n:])


def _gather_exchange(name, shards):
    def plan(ins, lands, send_sems, recv_sems):
        x, y, c = (lax.axis_index(a) for a in MESH_AXES)
        copies = []
        for i in range(len(ins)):
            for k, (fx, fy) in enumerate(CHIP_FLIPS):
                peer = (1 - x if fx else x, 1 - y if fy else y, c)
                copies.append((None, pltpu.make_async_remote_copy(
                    src_ref=ins[i], dst_ref=lands[i].at[2 * x + y], send_sem=send_sems.at[3 * i + k],
                    recv_sem=recv_sems.at[3 * i + k], device_id=peer, device_id_type=pl.DeviceIdType.MESH)))
        return copies, copies

    n = len(shards)
    return _Exchange(name, plan, shards, [jax.ShapeDtypeStruct((N_SHARDS,) + s.shape, s.dtype) for s in shards], 3 * n, 3 * n)


def _scatter_exchange(name, layer, chunks):
    def plan(ins, lands, send_sems, recv_sems):
        x, y, c = (lax.axis_index(a) for a in MESH_AXES)
        me = _device_index()
        sends, arrivals = [], []
        for i in range(len(ins)):
            for j in range(N_SHARDS):
                target = (j // 2, j % 2, layer)
                remote = jnp.logical_not((x == target[0]) & (y == target[1]) & (c == layer))
                sends.append((remote, pltpu.make_async_remote_copy(
                    src_ref=ins[i].at[j], dst_ref=lands[i].at[me], send_sem=send_sems.at[N_SHARDS * i + j],
                    recv_sem=recv_sems.at[N_DEVICES * i + me], device_id=target, device_id_type=pl.DeviceIdType.MESH)))
            for s in range(N_DEVICES):
                arrivals.append(((c == layer) & (me != s), pltpu.make_async_remote_copy(
                    src_ref=ins[i].at[0], dst_ref=lands[i].at[s], send_sem=send_sems.at[0],
                    recv_sem=recv_sems.at[N_DEVICES * i + s], device_id=(x, y, c), device_id_type=pl.DeviceIdType.MESH)))
        return sends, arrivals

    n = len(chunks)
    lands = [jax.ShapeDtypeStruct((N_DEVICES,) + ch.shape[1:], ch.dtype) for ch in chunks]
    return _Exchange(name, plan, chunks, lands, N_SHARDS * n, N_DEVICES * n)


def _sum_contributions(name, got, mine):
    _, a, b = got.shape
    ta = _row_tile(a, max(SUBLANE, SUM_BLOCK_BYTES // (N_DEVICES * b * got.dtype.itemsize) // SUBLANE * SUBLANE))

    def kern(got_ref, mine_ref, o_ref):
        me = _device_index()
        acc = jnp.zeros(o_ref.shape, f32)
        for s in range(N_DEVICES):
            acc = acc + jnp.where(me == s, mine_ref[...].astype(f32), got_ref[s].astype(f32))
        o_ref[...] = acc

    return pl.pallas_call(
        kern, name=name, grid=(a // ta,),
        in_specs=[pl.BlockSpec((N_DEVICES, ta, b), lambda i: (0, i, 0)), pl.BlockSpec((ta, b), lambda i: (i, 0))],
        out_specs=pl.BlockSpec((ta, b), lambda i: (i, 0)), out_shape=jax.ShapeDtypeStruct((a, b), f32),
        compiler_params=pltpu.CompilerParams(dimension_semantics=("parallel",)))(got, mine)


def _swap_layers(name, sums):
    n = len(sums[0])

    def body(*refs):
        srcs = (refs[:n], refs[n:2 * n])
        outs, (send_sems, recv_sems) = refs[2 * n:3 * n], refs[3 * n:]
        x, y, c = (lax.axis_index(a) for a in MESH_AXES)
        for i in range(n):
            for layer in range(DEPTH):
                cp = pltpu.make_async_remote_copy(
                    src_ref=srcs[layer][i], dst_ref=outs[i], send_sem=send_sems.at[i], recv_sem=recv_sems.at[i],
                    device_id=(x, y, 1 - c), device_id_type=pl.DeviceIdType.MESH)
                pl.when(c == layer)(cp.start)
        for i in range(n):
            pltpu.make_async_remote_copy(
                src_ref=srcs[0][i], dst_ref=outs[i], send_sem=send_sems.at[i], recv_sem=recv_sems.at[i],
                device_id=(x, y, 1 - c), device_id_type=pl.DeviceIdType.MESH).wait()

    return pl.pallas_call(
        body, name=name, out_shape=[jax.ShapeDtypeStruct(s.shape, s.dtype) for s in sums[0]],
        in_specs=[HBM_SPEC] * (2 * n), out_specs=[HBM_SPEC] * n,
        scratch_shapes=[pltpu.SemaphoreType.DMA((n,)), pltpu.SemaphoreType.DMA((n,))])(*sums[0], *sums[1])


def _stack_shards(g, axis):
    if axis == 1:
        return g.reshape(N_SHARDS, g.shape[0] // N_SHARDS, g.shape[1])
    return g.reshape(g.shape[0], N_SHARDS, g.shape[1] // N_SHARDS).transpose(1, 0, 2)


def _join_shards(s, axis):
    if axis == 1:
        return s.reshape(-1, s.shape[2])
    return s.transpose(1, 0, 2).reshape(s.shape[1], -1)


def _layer_shards(w, l):
    return [w[k][l] if k in ELEMENTWISE_F32 else w[k][l].astype(bf16) for k in SHARDED]


def _full_weights(sent, got):
    j = 2 * lax.axis_index("x") + lax.axis_index("y")
    return {k: _join_shards(lax.dynamic_update_slice(g, own[None], (j, 0, 0)), SHARD_AXIS[k])
            for k, own, g in zip(SHARDED, sent, got)}


def _grad_chunks(grads):
    return [_stack_shards(grads[k], SHARD_AXIS[k]).astype(bf16) for k in SHARDED]


def _sum_layer(l, got, chunks):
    j = 2 * lax.axis_index("x") + lax.axis_index("y")
    return [_sum_contributions(f"sum_l{l}_{k}", g, lax.dynamic_index_in_dim(ch, j, 0, keepdims=False))
            for k, g, ch in zip(SHARDED, got, chunks)]


def _both_layers(sums):
    c = lax.axis_index("c")
    other = _swap_layers("swap_layers", sums)
    return {k: jnp.stack([jnp.where(c == 0, sums[0][i], other[i]), jnp.where(c == 0, other[i], sums[1][i])])
            for i, k in enumerate(SHARDED)}


def kernel(x, p, positions, g_mix, w_in, g_qc, w_uq, g_kvc, w_ukv, b_f, lru_conv_w, lru_conv_b, w_r, b_r, w_i, b_i, lru_lambda, g_out, w_o, g_ffn, w_up, ffn_conv_w, ffn_conv_b, w_down, g_ple, w_ple_gate, w_ple_proj, g_final, loss_target, m_g_mix, m_w_in, m_g_qc, m_w_uq, m_g_kvc, m_w_ukv, m_b_f, m_lru_conv_w, m_lru_conv_b, m_w_r, m_b_r, m_w_i, m_b_i, m_lru_lambda, m_g_out, m_w_o, m_g_ffn, m_w_up, m_ffn_conv_w, m_ffn_conv_b, m_w_down, m_g_ple, m_w_ple_gate, m_w_ple_proj, m_g_final, v_g_mix, v_w_in, v_g_qc, v_w_uq, v_g_kvc, v_w_ukv, v_b_f, v_lru_conv_w, v_lru_conv_b, v_w_r, v_b_r, v_w_i, v_b_i, v_lru_lambda, v_g_out, v_w_o, v_g_ffn, v_w_up, v_ffn_conv_w, v_ffn_conv_b, v_w_down, v_g_ple, v_w_ple_gate, v_w_ple_proj, v_g_final):
    given = locals()
    w = {k: given[k] for k in WEIGHTS}
    m = {k: given["m_" + k] for k in WEIGHTS}
    v = {k: given["v_" + k] for k in WEIGHTS}

    sent = [_layer_shards(w, l) for l in range(DEPTH)]
    first = _gather_exchange("gather_weights_l0", sent[0]).run()
    ahead = _gather_exchange("gather_weights_l1", sent[1])
    pos = positions[0].astype(f32).reshape(-1, 1) + ahead.start(after=first[0])
    behind, layer_grads, chunks = [], [None] * DEPTH, [None] * DEPTH

    def weights_of(l, h):
        got = first if l == 0 else ahead.finish(after=h)
        full = _full_weights(sent[l], got)
        full.update({k: w[k][l] for k in LAYER_WEIGHTS if k in REPLICATED})
        return full

    def grads_to(l, grads, dh):
        layer_grads[l], chunks[l] = grads, _grad_chunks(grads)
        if l == 0:
            return dh
        behind.append(_scatter_exchange("scatter_grads_l1", 1, chunks[1]))
        return dh + behind[0].start(after=dh)

    loss, dx, dg_final = _local_step(x[0], p[:, 0], pos, loss_target[0], w["g_final"], weights_of, grads_to)

    got = [_scatter_exchange("scatter_grads_l0", 0, chunks[0]).run(), behind[0].finish(after=dx)]
    g_sharded = _both_layers([_sum_layer(l, got[l], chunks[l]) for l in range(DEPTH)])
    big = [[], [], [], []]
    for k in SHARDED:
        shape = w[k].shape
        flat = [t.reshape(-1, shape[-1]) for t in (w[k], g_sharded[k], m[k], v[k])]
        for kind, res in enumerate((flat[1],) + tuple(_adamw("adamw_" + k, *flat))):
            big[kind].append(res.reshape(shape))

    grads = {k: jnp.stack([layer_grads[l][k] for l in range(DEPTH)]) for k in LAYER_WEIGHTS if k in REPLICATED}
    grads["g_final"] = dg_final
    rep_shapes = [w[k].shape for k in REPLICATED] + [(1,)]
    contrib = _pack([grads[k] for k in REPLICATED] + [loss.reshape(1)], SUBLANE)
    g_rep = _sum_slabs("sum_replicated", _exchange("gather_replicated", contrib, MESH_AXES, scatter=False))
    zero = jnp.zeros((1,), f32)
    w_rep, m_rep, v_rep = (_pack([t[k] for k in REPLICATED] + [zero], SUBLANE) for t in (w, m, v))
    rep = [_unpack(b, rep_shapes) for b in (g_rep,) + tuple(_adamw("adamw_replicated", w_rep, g_rep, m_rep, v_rep))]

    outs = []
    for kind in range(4):
        by_name = dict(zip(SHARDED, big[kind]))
        by_name.update(zip(REPLICATED, rep[kind][:-1]))
        outs.append([by_name[k] for k in WEIGHTS])
    total_loss = rep[0][-1][0]
    return (total_loss, dx.reshape(x.shape), *outs[0], *outs[1], *outs[2], *outs[3])
```

```python
import functools
import math

import numpy as np
import jax
import jax.numpy as jnp
from jax import lax
from jax.experimental import pallas as pl
from jax.experimental.pallas import tpu as pltpu

f32, bf16 = jnp.float32, jnp.bfloat16

D_MODEL = 1024
PLE_DIM = 256
MLA_HEADS, MLA_NOPE, MLA_ROPE, MLA_V = 4, 64, 32, 64
MLA_Q_RANK, MLA_KV_RANK = 192, 128
FOX_HEADS, FOX_HEAD_DIM = 4, 64
LRU_WIDTH, LRU_BLOCKS, LRU_BLOCK, LRU_CONV, LRU_C = 512, 8, 64, 4, 8.0
D_FF, FFN_CONV = 2816, 3
ROPE_THETA = 10000.0
EPS = 1e-6
DEPTH = 2
ADAM_LR, ADAM_B1, ADAM_B2, ADAM_EPS, ADAM_WD, ADAM_STEP = 0.001, 0.9, 0.999, 1e-08, 0.01, 10

LANE = 128
SUBLANE = 8
HEADS = 4

Z_FQ, Z_FK, Z_FV, Z_LX, Z_LG, Z_QC, Z_KVC, Z_KR, Z_FL, Z_W = 0, 512, 1024, 1536, 2048, 2560, 2816, 2944, 3072, 3200
QC_W = 256
ROPE_AT = 64


def _head_pad_map(n_heads, width):
    m = -np.ones(n_heads * LANE, np.int64)
    for h in range(n_heads):
        m[h * LANE:h * LANE + width] = h * width + np.arange(width)
    return m


def _z_map():
    m = -np.ones(Z_W, np.int64)
    o_qc, o_kvc, o_kr = 0, MLA_Q_RANK, MLA_Q_RANK + MLA_KV_RANK
    o_fq = o_kr + MLA_ROPE
    o_fk, o_fv = o_fq + 256, o_fq + 512
    o_fl = o_fv + 256
    o_lx = o_fl + FOX_HEADS
    o_lg = o_lx + LRU_WIDTH
    m[Z_FQ:Z_FQ + 512] = np.where(_head_pad_map(4, 64) >= 0, _head_pad_map(4, 64) + o_fq, -1)
    m[Z_FK:Z_FK + 512] = np.where(_head_pad_map(4, 64) >= 0, _head_pad_map(4, 64) + o_fk, -1)
    m[Z_FV:Z_FV + 512] = np.where(_head_pad_map(4, 64) >= 0, _head_pad_map(4, 64) + o_fv, -1)
    m[Z_LX:Z_LX + 512] = o_lx + np.arange(512)
    m[Z_LG:Z_LG + 512] = o_lg + np.arange(512)
    m[Z_QC:Z_QC + MLA_Q_RANK] = o_qc + np.arange(MLA_Q_RANK)
    m[Z_KVC:Z_KVC + MLA_KV_RANK] = o_kvc + np.arange(MLA_KV_RANK)
    m[Z_KR + ROPE_AT:Z_KR + ROPE_AT + MLA_ROPE] = o_kr + np.arange(MLA_ROPE)
    m[Z_FL:Z_FL + FOX_HEADS] = o_fl + np.arange(FOX_HEADS)
    return m


def _ukv_map():
    m = -np.ones(2 * HEADS * LANE, np.int64)
    for h in range(HEADS):
        m[h * LANE:h * LANE + MLA_NOPE] = h * (MLA_NOPE + MLA_V) + np.arange(MLA_NOPE)
        m[HEADS * LANE + h * LANE:HEADS * LANE + h * LANE + MLA_V] = h * (MLA_NOPE + MLA_V) + MLA_NOPE + np.arange(MLA_V)
    return m


def _omix_map():
    return np.concatenate([_head_pad_map(4, 64), np.where(_head_pad_map(4, 64) >= 0, _head_pad_map(4, 64) + 256, -1),
                           512 + np.arange(512)])


def _pad_to(m, n):
    return np.concatenate([m, -np.ones(n - m.shape[0], np.int64)])


def _take_pad(a, m, axis):
    out = jnp.take(a, jnp.asarray(np.maximum(m, 0), jnp.int32), axis=axis)
    shape = [1] * a.ndim
    shape[axis] = m.shape[0]
    return out * jnp.asarray((m >= 0).reshape(shape), a.dtype)


def _take_inv(a, m, axis):
    n = int(m.max()) + 1
    inv = np.zeros(n, np.int64)
    inv[m[m >= 0]] = np.nonzero(m >= 0)[0]
    return jnp.take(a, jnp.asarray(inv, jnp.int32), axis=axis)


Z_MAP = _z_map()
UQ_COL_MAP = _head_pad_map(HEADS, MLA_NOPE + MLA_ROPE)
UQ_ROW_MAP = _pad_to(np.arange(MLA_Q_RANK), QC_W)
UKV_MAP = _ukv_map()
OMIX_MAP = _omix_map()
OMIX_W = 1536


def _rope_tables(width, at):
    half = MLA_ROPE // 2
    inv = ROPE_THETA ** (-np.arange(half, dtype=np.float32) / half)
    freq = np.zeros((1, width), np.float32)
    m1 = np.zeros((1, width), np.float32)
    m2 = np.zeros((1, width), np.float32)
    for h in range(width // LANE):
        b = h * LANE + at
        freq[0, b:b + half] = inv
        freq[0, b + half:b + 2 * half] = inv
        m1[0, b:b + half] = 1.0
        m2[0, b + half:b + 2 * half] = 1.0
    return freq, m1, m2


def _view(r):
    return r if isinstance(r, tuple) else (r, r.shape[1], 0)


def _blk(dim, cap):
    if dim <= cap:
        return dim
    for b in range(cap, LANE - 1, -LANE):
        if dim % b == 0:
            return b
    return dim


@functools.partial(jax.custom_vjp, nondiff_argnums=(1, 2))
def _roll(x, shift, axis):
    return pltpu.roll(x, shift, axis)


def _roll_fwd(x, shift, axis):
    return pltpu.roll(x, shift, axis), None


def _roll_bwd(shift, axis, _, g):
    return (pltpu.roll(g, g.shape[axis] - shift, axis),)


_roll.defvjp(_roll_fwd, _roll_bwd)


def _rowwise(name, fn, rows, pars, outs, tb=256):
    rows = [_view(r) for r in rows]
    n = rows[0][0].shape[0]
    tb = min(tb, n)
    nr, npar = len(rows), len(pars)

    def kern(*refs):
        r = [refs[k][...].astype(f32) for k in range(nr)]
        p = [refs[nr + k][...] for k in range(npar)]
        res = fn(*r, *p)
        for o_ref, o in zip(refs[nr + npar:], res):
            o_ref[...] = o.astype(o_ref.dtype)

    in_specs = [pl.BlockSpec((tb, w), lambda i, j=idx: (i, j)) for (_, w, idx) in rows]
    in_specs += [pl.BlockSpec(p.shape, lambda i: (0, 0)) for p in pars]
    out_specs = [pl.BlockSpec((tb, w), lambda i: (i, 0)) for (w, _) in outs]
    out_shape = [jax.ShapeDtypeStruct((n, w), dt) for (w, dt) in outs]
    return pl.pallas_call(kern, name=name, grid=(n // tb,), in_specs=in_specs, out_specs=out_specs, out_shape=out_shape,
                          compiler_params=pltpu.CompilerParams(dimension_semantics=("parallel",)))(*[r[0] for r in rows], *pars)


def _rowwise_bwd(name, fn, rows, pars, cts, ndiff, adds=None, tb=256, dts=None):
    rows = [_view(r) for r in rows]
    dts = dts or [f32] * ndiff
    adds = adds or {}
    add_keys = sorted(adds)
    n = rows[0][0].shape[0]
    tb = min(tb, n)
    nr, npar, nct, nadd = len(rows), len(pars), len(cts), len(add_keys)

    def kern(*refs):
        i = pl.program_id(0)
        r = [refs[k][...].astype(f32) for k in range(nr)]
        p = [refs[nr + k][...] for k in range(npar)]
        ct = [refs[nr + npar + k][...].astype(f32) for k in range(nct)]
        ad = {key: refs[nr + npar + nct + k][...] for k, key in enumerate(add_keys)}
        o_refs = refs[nr + npar + nct + nadd:]

        def g(*d):
            return tuple(fn(*d[:ndiff], *r[ndiff:], *d[ndiff:]))

        _, vjp = jax.vjp(g, *r[:ndiff], *p)
        grads = vjp(tuple(ct))
        for k in range(ndiff):
            gk = grads[k]
            if k in ad:
                gk = gk + ad[k]
            o_refs[k][...] = gk.astype(o_refs[k].dtype)

        @pl.when(i == 0)
        def _():
            for k in range(npar):
                o_refs[ndiff + k][...] = jnp.zeros_like(o_refs[ndiff + k])

        for k in range(npar):
            o_refs[ndiff + k][...] += grads[ndiff + k]

    in_specs = [pl.BlockSpec((tb, w), lambda i, j=idx: (i, j)) for (_, w, idx) in rows]
    in_specs += [pl.BlockSpec(p.shape, lambda i: (0, 0)) for p in pars]
    in_specs += [pl.BlockSpec((tb, c.shape[1]), lambda i: (i, 0)) for c in cts]
    in_specs += [pl.BlockSpec((tb, adds[k].shape[1]), lambda i: (i, 0)) for k in add_keys]
    out_specs = [pl.BlockSpec((tb, rows[k][1]), lambda i: (i, 0)) for k in range(ndiff)]
    out_specs += [pl.BlockSpec(p.shape, lambda i: (0, 0)) for p in pars]
    out_shape = [jax.ShapeDtypeStruct((n, rows[k][1]), dts[k]) for k in range(ndiff)]
    out_shape += [jax.ShapeDtypeStruct(p.shape, f32) for p in pars]
    res = pl.pallas_call(kern, name=name, grid=(n // tb,), in_specs=in_specs, out_specs=out_specs, out_shape=out_shape,
                         compiler_params=pltpu.CompilerParams(dimension_semantics=("arbitrary",)))(
        *[r[0] for r in rows], *pars, *cts, *[adds[k] for k in add_keys])
    return res[:ndiff], res[ndiff:]


_DOT_DIMS = {"nn": ((1,), (0,)), "nt": ((1,), (1,)), "tn": ((0,), (0,))}

MM_VMEM_BUDGET = 36 * 2 ** 20
MM_MAX_TM = 1024
MM_STEP, MM_RESULT, MM_XPOSE, MM_CAST = 700.0, 7.5e-4, 9e-4, 1e-3


def _tile_candidates(dim):
    c = [d for d in range(LANE, dim + 1, LANE) if dim % d == 0]
    return c or [dim]


@functools.lru_cache(maxsize=None)
def _mm_tiles(mode, m, n, k, a_bytes, b_bytes, o_bytes):
    best, best_cost = None, None
    for tm in _tile_candidates(m):
        if tm > MM_MAX_TM:
            continue
        for tn in _tile_candidates(n):
            for tk in _tile_candidates(k):
                vmem = 2 * (tm * tk * a_bytes + tk * tn * b_bytes + tm * tn * o_bytes) + 4 * tm * tn * (2 if tk < k else 1)
                vmem += (2 * tm * tk if a_bytes > 2 else 0) + (2 * tk * tn if b_bytes > 2 else 0)
                if vmem > MM_VMEM_BUDGET:
                    continue
                steps = (m // tm) * (n // tn) * (k // tk)
                cost = steps * MM_STEP + m * n * (k // tk) * MM_RESULT
                if mode == "tn":
                    cost += m * k * (n // tn) * MM_XPOSE
                cost += (m * k * (n // tn) * MM_CAST if a_bytes > 2 else 0) + (k * n * (m // tm) * MM_CAST if b_bytes > 2 else 0)
                if best is None or cost < best_cost:
                    best, best_cost = (tm, tn, tk), cost
    return best


def _mm(name, a, b, mode="nn", out_dtype=f32, res=None):
    if mode == "nn":
        (m, k), (_, n) = a.shape, b.shape
    elif mode == "nt":
        (m, k), (n, _) = a.shape, b.shape
    else:
        (k, m), (_, n) = a.shape, b.shape
    has_res = res is not None
    tm, tn, tk = _mm_tiles(mode, m, n, k, a.dtype.itemsize, b.dtype.itemsize,
                           jnp.dtype(out_dtype).itemsize + (res.dtype.itemsize if has_res else 0))
    nk = k // tk
    dims = (_DOT_DIMS[mode], ((), ()))

    def kern(*refs):
        a_ref, b_ref = refs[0], refs[1]
        o_ref, acc_ref = refs[-2], refs[-1]
        kk = pl.program_id(2)
        part = lax.dot_general(a_ref[...].astype(bf16), b_ref[...].astype(bf16), dims, preferred_element_type=f32)

        def finish(out):
            if has_res:
                out = out + refs[2][...]
            o_ref[...] = out.astype(o_ref.dtype)

        if nk == 1:
            finish(part)
            return

        @pl.when(kk == 0)
        def _():
            acc_ref[...] = part

        @pl.when(jnp.logical_and(kk > 0, kk < nk - 1))
        def _():
            acc_ref[...] += part

        @pl.when(kk == nk - 1)
        def _():
            finish(acc_ref[...] + part)

    if mode == "tn":
        a_spec = pl.BlockSpec((tk, tm), lambda i, j, kk: (kk, i))
    else:
        a_spec = pl.BlockSpec((tm, tk), lambda i, j, kk: (i, kk))
    if mode == "nt":
        b_spec = pl.BlockSpec((tn, tk), lambda i, j, kk: (j, kk))
    else:
        b_spec = pl.BlockSpec((tk, tn), lambda i, j, kk: (kk, j))
    in_specs = [a_spec, b_spec]
    args = [a, b]
    if has_res:
        in_specs.append(pl.BlockSpec((tm, tn), lambda i, j, kk: (i, j)))
        args.append(res)
    return pl.pallas_call(
        kern, name=name, grid=(m // tm, n // tn, nk), in_specs=in_specs,
        out_specs=pl.BlockSpec((tm, tn), lambda i, j, kk: (i, j)),
        out_shape=jax.ShapeDtypeStruct((m, n), out_dtype),
        scratch_shapes=[pltpu.VMEM((tm, tn) if nk > 1 else (SUBLANE, LANE), f32)],
        compiler_params=pltpu.CompilerParams(dimension_semantics=("parallel", "parallel", "arbitrary")))(*args)


ATT_T = 512


def _att_tile(s):
    return min(ATT_T, s)


def _fold_scale(scale):
    return (scale, 1.0) if math.frexp(scale)[0] == 0.5 else (1.0, scale)


def _scores(qb, kb, s_mul, ck, diagonal, t):
    s = lax.dot_general(qb, kb, (_DOT_DIMS["nt"], ((), ())), preferred_element_type=f32)
    if s_mul != 1.0:
        s = s * s_mul
    if ck is not None:
        s = s - ck
    if not diagonal:
        return s
    row = lax.broadcasted_iota(jnp.int32, (t, t), 0)
    col = lax.broadcasted_iota(jnp.int32, (t, t), 1)
    return jnp.where(col <= row, s, -jnp.inf)


def _attn_fwd(name, q, k, v, scale, c_row=None):
    (qa, qo), (ka, ko), (va, vo) = q, k, v
    s_len = qa.shape[0]
    t = _att_tile(s_len)
    nt = s_len // t
    decay = c_row is not None
    q_mul, s_mul = _fold_scale(scale)

    def kern(*refs):
        q_ref, k_ref, v_ref = refs[:3]
        o_ref, lse_ref = refs[-2:]
        i = pl.program_id(1)
        qb = (q_ref[...] * q_mul).astype(bf16)

        def step(j, carry, diagonal):
            m, l, acc = carry
            rows = pl.ds(pl.multiple_of(j * t, t), t)
            kb = k_ref[rows, :].astype(bf16)
            vb = v_ref[rows, :].astype(bf16)
            s = _scores(qb, kb, s_mul, refs[3][j] if decay else None, diagonal, t)
            m_new = jnp.maximum(m, jnp.max(s, axis=1, keepdims=True))
            alpha = jnp.exp(m - m_new)
            p = jnp.exp(s - m_new)
            l = alpha * l + jnp.sum(p, axis=1, keepdims=True)
            acc = alpha * acc + jnp.dot(p.astype(bf16), vb, preferred_element_type=f32)
            return m_new, l, acc

        init = (jnp.full((t, 1), -jnp.inf, f32), jnp.zeros((t, 1), f32), jnp.zeros((t, LANE), f32))
        m, l, acc = step(i, lax.fori_loop(0, i, lambda j, c: step(j, c, False), init), True)
        o_ref[...] = acc / l
        lse_ref[...] = m + jnp.log(l)

    in_specs = [pl.BlockSpec((t, LANE), lambda h, i: (i, qo + h)),
                pl.BlockSpec((s_len, LANE), lambda h, i: (0, ko + h)),
                pl.BlockSpec((s_len, LANE), lambda h, i: (0, vo + h))]
    args = [qa, ka, va]
    if decay:
        in_specs.append(pl.BlockSpec((None, nt, 1, t), lambda h, i: (h, 0, 0, 0)))
        args.append(c_row)
    return pl.pallas_call(
        kern, name=name, grid=(HEADS, nt), in_specs=in_specs,
        out_specs=[pl.BlockSpec((t, LANE), lambda h, i: (i, h)), pl.BlockSpec((None, t, 1), lambda h, i: (h, i, 0))],
        out_shape=[jax.ShapeDtypeStruct((s_len, HEADS * LANE), f32), jax.ShapeDtypeStruct((HEADS, s_len, 1), f32)],
        compiler_params=pltpu.CompilerParams(dimension_semantics=("parallel", "arbitrary")))(*args)


def _attn_dq(name, q, k, v, o, do, lse, scale, c_row=None):
    (qa, qo), (ka, ko), (va, vo) = q, k, v
    s_len = qa.shape[0]
    t = _att_tile(s_len)
    nt = s_len // t
    decay = c_row is not None
    q_mul, s_mul = _fold_scale(scale)

    def kern(*refs):
        q_ref, k_ref, v_ref, o_ref, do_ref, lse_ref = refs[:6]
        dq_ref, delta_ref, drow_ref = refs[-3:]
        i = pl.program_id(1)
        qb = (q_ref[...] * q_mul).astype(bf16)
        dob = do_ref[...]
        delta = jnp.sum(dob * o_ref[...], axis=1, keepdims=True)
        dob = dob.astype(bf16)
        lse = lse_ref[...]

        def step(j, carry, diagonal):
            dq, drow = carry
            rows = pl.ds(pl.multiple_of(j * t, t), t)
            kb = k_ref[rows, :].astype(bf16)
            vb = v_ref[rows, :].astype(bf16)
            s = _scores(qb, kb, s_mul, refs[6][j] if decay else None, diagonal, t)
            p = jnp.exp(s - lse)
            dp = lax.dot_general(dob, vb, (_DOT_DIMS["nt"], ((), ())), preferred_element_type=f32)
            ds = p * (dp - delta)
            return dq + jnp.dot(ds.astype(bf16), kb, preferred_element_type=f32), drow + jnp.sum(ds, axis=1, keepdims=True)

        init = (jnp.zeros((t, LANE), f32), jnp.zeros((t, 1), f32))
        dq, drow = step(i, lax.fori_loop(0, i, lambda j, c: step(j, c, False), init), True)
        dq_ref[...] = dq * scale
        delta_ref[...] = delta
        drow_ref[...] = drow

    in_specs = [pl.BlockSpec((t, LANE), lambda h, i: (i, qo + h)),
                pl.BlockSpec((s_len, LANE), lambda h, i: (0, ko + h)),
                pl.BlockSpec((s_len, LANE), lambda h, i: (0, vo + h)),
                pl.BlockSpec((t, LANE), lambda h, i: (i, h)),
                pl.BlockSpec((t, LANE), lambda h, i: (i, h)),
                pl.BlockSpec((None, t, 1), lambda h, i: (h, i, 0))]
    args = [qa, ka, va, o, do, lse]
    if decay:
        in_specs.append(pl.BlockSpec((None, nt, 1, t), lambda h, i: (h, 0, 0, 0)))
        args.append(c_row)
    col = pl.BlockSpec((None, t, 1), lambda h, i: (h, i, 0))
    return pl.pallas_call(
        kern, name=name, grid=(HEADS, nt), in_specs=in_specs,
        out_specs=[pl.BlockSpec((t, LANE), lambda h, i: (i, h)), col, col],
        out_shape=[jax.ShapeDtypeStruct((s_len, HEADS * LANE), f32), jax.ShapeDtypeStruct((HEADS, s_len, 1), f32),
                   jax.ShapeDtypeStruct((HEADS, s_len, 1), f32)],
        compiler_params=pltpu.CompilerParams(dimension_semantics=("parallel", "arbitrary")))(*args)


def _attn_dkv(name, q, k, v, do, lse, delta, scale, c_row=None):
    (qa, qo), (ka, ko), (va, vo) = q, k, v
    s_len = qa.shape[0]
    t = _att_tile(s_len)
    nt = s_len // t
    decay = c_row is not None
    q_mul, s_mul = _fold_scale(scale)

    def kern(*refs):
        q_ref, k_ref, v_ref, do_ref, lse_ref, delta_ref = refs[:6]
        j = pl.program_id(1)
        kb = k_ref[...].astype(bf16)
        vb = v_ref[...].astype(bf16)
        ck = refs[6][...] if decay else None

        def step(i, carry, diagonal):
            dk, dv, dc = carry
            rows = pl.ds(pl.multiple_of(i * t, t), t)
            qb = (q_ref[rows, :] * q_mul).astype(bf16)
            dob = do_ref[rows, :].astype(bf16)
            s = _scores(qb, kb, s_mul, ck, diagonal, t)
            p = jnp.exp(s - lse_ref[rows, :])
            dv = dv + lax.dot_general(p.astype(bf16), dob, (_DOT_DIMS["tn"], ((), ())), preferred_element_type=f32)
            dp = lax.dot_general(dob, vb, (_DOT_DIMS["nt"], ((), ())), preferred_element_type=f32)
            ds = p * (dp - delta_ref[rows, :])
            dk = dk + lax.dot_general(ds.astype(bf16), qb, (_DOT_DIMS["tn"], ((), ())), preferred_element_type=f32)
            if decay:
                dc = dc - jnp.sum(ds, axis=0, keepdims=True)
            return dk, dv, dc

        init = (jnp.zeros((t, LANE), f32), jnp.zeros((t, LANE), f32), jnp.zeros((1, t), f32))
        dk, dv, dc = lax.fori_loop(j + 1, nt, lambda i, c: step(i, c, False), step(j, init, True))
        if decay:
            dk_ref, dv_ref, dc_ref = refs[-3:]
            dc_ref[...] = dc
        else:
            dk_ref, dv_ref = refs[-2:]
        dk_ref[...] = dk * s_mul
        dv_ref[...] = dv

    in_specs = [pl.BlockSpec((s_len, LANE), lambda h, j: (0, qo + h)),
                pl.BlockSpec((t, LANE), lambda h, j: (j, ko + h)),
                pl.BlockSpec((t, LANE), lambda h, j: (j, vo + h)),
                pl.BlockSpec((s_len, LANE), lambda h, j: (0, h)),
                pl.BlockSpec((None, s_len, 1), lambda h, j: (h, 0, 0)),
                pl.BlockSpec((None, s_len, 1), lambda h, j: (h, 0, 0))]
    args = [qa, ka, va, do, lse, delta]
    out_specs = [pl.BlockSpec((t, LANE), lambda h, j: (j, h)), pl.BlockSpec((t, LANE), lambda h, j: (j, h))]
    out_shape = [jax.ShapeDtypeStruct((s_len, HEADS * LANE), f32), jax.ShapeDtypeStruct((s_len, HEADS * LANE), f32)]
    if decay:
        in_specs.append(pl.BlockSpec((None, None, 1, t), lambda h, j: (h, j, 0, 0)))
        args.append(c_row)
        out_specs.append(pl.BlockSpec((None, None, 1, t), lambda h, j: (h, j, 0, 0)))
        out_shape.append(jax.ShapeDtypeStruct((HEADS, nt, 1, t), f32))
    return pl.pallas_call(
        kern, name=name, grid=(HEADS, nt), in_specs=in_specs, out_specs=out_specs, out_shape=out_shape,
        compiler_params=pltpu.CompilerParams(dimension_semantics=("parallel", "arbitrary")))(*args)


CONV_TS, CONV_CB = 1024, 256


def _conv_fwd(name, x, w, b, taps):
    xa, width, xidx = _view(x)
    s_len = xa.shape[0]
    ts, cb = min(CONV_TS, s_len), CONV_CB
    xo = xidx * width // cb

    def kern(x_ref, halo_ref, w_ref, b_ref, o_ref):
        i = pl.program_id(1)
        xb = x_ref[...]
        halo = jnp.where(i == 0, 0.0, halo_ref[...])
        xx = jnp.concatenate([halo, xb], axis=0)
        out = b_ref[...] + w_ref[taps - 1:taps, :] * xb
        for k in range(taps - 1):
            out = out + w_ref[k:k + 1, :] * pltpu.roll(xx, taps - 1 - k, 0)[SUBLANE:]
        o_ref[...] = out

    return pl.pallas_call(
        kern, name=name, grid=(width // cb, s_len // ts),
        in_specs=[pl.BlockSpec((ts, cb), lambda j, i: (i, xo + j)),
                  pl.BlockSpec((SUBLANE, cb), lambda j, i: (jnp.maximum(i * (ts // SUBLANE) - 1, 0), xo + j)),
                  pl.BlockSpec((taps, cb), lambda j, i: (0, j)),
                  pl.BlockSpec((1, cb), lambda j, i: (0, j))],
        out_specs=pl.BlockSpec((ts, cb), lambda j, i: (i, j)),
        out_shape=jax.ShapeDtypeStruct((s_len, width), f32),
        compiler_params=pltpu.CompilerParams(dimension_semantics=("parallel", "parallel")))(xa, xa, w, b)


def _conv_bwd(name, x, dout, w, taps, dout2=None, dx_dtype=f32):
    xa, width, xidx = _view(x)
    s_len = xa.shape[0]
    ts, cb = min(CONV_TS, s_len), CONV_CB
    xo = xidx * width // cb
    n_i = s_len // ts
    two = dout2 is not None

    def kern(*refs):
        x_ref, halo_ref, w_ref = refs[:3]
        dx_ref, dw_ref, db_ref = refs[-3:]
        i = pl.program_id(1)
        if two:
            d = refs[3][...] + refs[5][...]
            dn = refs[4][...] + refs[6][...]
        else:
            d, dn = refs[3][...], refs[4][...]
        dn = jnp.where(i == n_i - 1, 0.0, dn)
        xb = x_ref[...]
        halo = jnp.where(i == 0, 0.0, halo_ref[...])
        xx = jnp.concatenate([halo, xb], axis=0)
        dd = jnp.concatenate([d, dn], axis=0)

        @pl.when(i == 0)
        def _():
            dw_ref[...] = jnp.zeros_like(dw_ref)
            db_ref[...] = jnp.zeros_like(db_ref)

        dx = w_ref[taps - 1:taps, :] * d
        dw_ref[taps - 1:taps, :] += jnp.sum(d * xb, axis=0, keepdims=True)
        for k in range(taps - 1):
            sh = taps - 1 - k
            dx = dx + w_ref[k:k + 1, :] * pltpu.roll(dd, ts + SUBLANE - sh, 0)[:ts]
            dw_ref[k:k + 1, :] += jnp.sum(d * pltpu.roll(xx, sh, 0)[SUBLANE:], axis=0, keepdims=True)
        dx_ref[...] = dx.astype(dx_ref.dtype)
        db_ref[...] += jnp.sum(d, axis=0, keepdims=True)

    d_spec = pl.BlockSpec((ts, cb), lambda j, i: (i, j))
    dn_spec = pl.BlockSpec((SUBLANE, cb), lambda j, i: (jnp.minimum((i + 1) * (ts // SUBLANE), s_len // SUBLANE - 1), j))
    in_specs = [pl.BlockSpec((ts, cb), lambda j, i: (i, xo + j)),
                pl.BlockSpec((SUBLANE, cb), lambda j, i: (jnp.maximum(i * (ts // SUBLANE) - 1, 0), xo + j)),
                pl.BlockSpec((taps, cb), lambda j, i: (0, j)), d_spec, dn_spec]
    args = [xa, xa, w, dout, dout]
    if two:
        in_specs += [d_spec, dn_spec]
        args += [dout2, dout2]
    return pl.pallas_call(
        kern, name=name, grid=(width // cb, n_i), in_specs=in_specs,
        out_specs=[pl.BlockSpec((ts, cb), lambda j, i: (i, j)), pl.BlockSpec((taps, cb), lambda j, i: (0, j)),
                   pl.BlockSpec((1, cb), lambda j, i: (0, j))],
        out_shape=[jax.ShapeDtypeStruct((s_len, width), dx_dtype), jax.ShapeDtypeStruct((taps, width), f32),
                   jax.ShapeDtypeStruct((1, width), f32)],
        compiler_params=pltpu.CompilerParams(dimension_semantics=("parallel", "arbitrary")))(*args)


def _conv_rows(xx, w_ref, b_ref, taps):
    out = b_ref[...] + w_ref[taps - 1:taps, :] * xx[SUBLANE:]
    for k in range(taps - 1):
        out = out + w_ref[k:k + 1, :] * pltpu.roll(xx, taps - 1 - k, 0)[SUBLANE:]
    return out


def _ffn_act_fwd(name, up, w, b):
    s_len = up.shape[0]
    ts, cb = min(CONV_TS, s_len), CONV_CB
    nf = D_FF // cb

    def kern(g_ref, gp_ref, v_ref, vp_ref, wg_ref, wv_ref, bg_ref, bv_ref, o_ref):
        first = pl.program_id(1) == 0
        ug = _conv_rows(jnp.concatenate([jnp.where(first, 0.0, gp_ref[...]), g_ref[...]], axis=0), wg_ref, bg_ref, FFN_CONV)
        uv = _conv_rows(jnp.concatenate([jnp.where(first, 0.0, vp_ref[...]), v_ref[...]], axis=0), wv_ref, bv_ref, FFN_CONV)
        o_ref[...] = (jax.nn.silu(ug) * uv).astype(o_ref.dtype)

    def half(off):
        return [pl.BlockSpec((ts, cb), lambda j, i: (i, off + j)),
                pl.BlockSpec((SUBLANE, cb), lambda j, i: (jnp.maximum(i * (ts // SUBLANE) - 1, 0), off + j))]

    def par(rows, off):
        return pl.BlockSpec((rows, cb), lambda j, i: (0, off + j))

    return pl.pallas_call(
        kern, name=name, grid=(nf, s_len // ts),
        in_specs=half(0) + half(nf) + [par(FFN_CONV, 0), par(FFN_CONV, nf), par(1, 0), par(1, nf)],
        out_specs=pl.BlockSpec((ts, cb), lambda j, i: (i, j)),
        out_shape=jax.ShapeDtypeStruct((s_len, D_FF), bf16),
        compiler_params=pltpu.CompilerParams(dimension_semantics=("parallel", "parallel")))(up, up, up, up, w, w, b, b)


def _ffn_act_bwd(name, up, dact, w, b):
    s_len = up.shape[0]
    ts, cb = min(CONV_TS, s_len), CONV_CB
    nf = D_FF // cb
    n_i = s_len // ts
    taps = FFN_CONV

    def kern(g_ref, gp_ref, gn_ref, v_ref, vp_ref, vn_ref, d_ref, dn_ref, wg_ref, wv_ref, bg_ref, bv_ref,
             dg_ref, dv_ref, dwg_ref, dwv_ref, dbg_ref, dbv_ref):
        i = pl.program_id(1)
        first, last = i == 0, i == n_i - 1

        def extended(x_ref, p_ref, n_ref):
            return jnp.concatenate([jnp.where(first, 0.0, p_ref[...]), x_ref[...], jnp.where(last, 0.0, n_ref[...])], axis=0)

        gx, vx = extended(g_ref, gp_ref, gn_ref), extended(v_ref, vp_ref, vn_ref)
        ug, uv = _conv_rows(gx, wg_ref, bg_ref, taps), _conv_rows(vx, wv_ref, bv_ref, taps)
        dd = jnp.concatenate([d_ref[...], jnp.where(last, 0.0, dn_ref[...])], axis=0)
        sg = jax.nn.sigmoid(ug)
        dug = dd * uv * (sg * (1.0 + ug * (1.0 - sg)))
        duv = dd * (ug * sg)

        @pl.when(first)
        def _():
            for ref in (dwg_ref, dwv_ref, dbg_ref, dbv_ref):
                ref[...] = jnp.zeros_like(ref)

        def transposed(du, xx, w_ref, dx_ref, dw_ref, db_ref):
            d = du[:ts]
            dx = w_ref[taps - 1:taps, :] * d
            dw_ref[taps - 1:taps, :] += jnp.sum(d * xx[SUBLANE:SUBLANE + ts], axis=0, keepdims=True)
            for k in range(taps - 1):
                sh = taps - 1 - k
                dx = dx + w_ref[k:k + 1, :] * pltpu.roll(du, ts + SUBLANE - sh, 0)[:ts]
                dw_ref[k:k + 1, :] += jnp.sum(d * pltpu.roll(xx, sh, 0)[SUBLANE:SUBLANE + ts], axis=0, keepdims=True)
            dx_ref[...] = dx.astype(dx_ref.dtype)
            db_ref[...] += jnp.sum(d, axis=0, keepdims=True)

        transposed(dug, gx, wg_ref, dg_ref, dwg_ref, dbg_ref)
        transposed(duv, vx, wv_ref, dv_ref, dwv_ref, dbv_ref)

    blocks = s_len // SUBLANE

    def half(off):
        return [pl.BlockSpec((ts, cb), lambda j, i: (i, off + j)),
                pl.BlockSpec((SUBLANE, cb), lambda j, i: (jnp.maximum(i * (ts // SUBLANE) - 1, 0), off + j)),
                pl.BlockSpec((SUBLANE, cb), lambda j, i: (jnp.minimum((i + 1) * (ts // SUBLANE), blocks - 1), off + j))]

    def par(rows, off):
        return pl.BlockSpec((rows, cb), lambda j, i: (0, off + j))

    d_specs = [pl.BlockSpec((ts, cb), lambda j, i: (i, j)),
               pl.BlockSpec((SUBLANE, cb), lambda j, i: (jnp.minimum((i + 1) * (ts // SUBLANE), blocks - 1), j))]
    out_par = [pl.BlockSpec((r, cb), lambda j, i: (0, j)) for r in (taps, taps, 1, 1)]
    return pl.pallas_call(
        kern, name=name, grid=(nf, n_i),
        in_specs=half(0) + half(nf) + d_specs + [par(taps, 0), par(taps, nf), par(1, 0), par(1, nf)],
        out_specs=[pl.BlockSpec((ts, cb), lambda j, i: (i, j))] * 2 + out_par,
        out_shape=[jax.ShapeDtypeStruct((s_len, D_FF), bf16)] * 2 + [jax.ShapeDtypeStruct((taps, D_FF), f32)] * 2
        + [jax.ShapeDtypeStruct((1, D_FF), f32)] * 2,
        compiler_params=pltpu.CompilerParams(dimension_semantics=("parallel", "arbitrary")))(
        up, up, up, up, up, up, dact, dact, w, w, b, b)


def _segment_carries(a_last, h_last, reverse):
    ridx = lax.broadcasted_iota(jnp.int32, (SUBLANE, LANE), 0)

    def pick(m, s):
        return jnp.sum(jnp.where(ridx == s, m, 0.0), axis=0, keepdims=True)

    carry = jnp.zeros((SUBLANE, LANE), f32)
    prev = jnp.zeros((1, LANE), f32)
    order = range(SUBLANE - 2, -1, -1) if reverse else range(1, SUBLANE)
    for s in order:
        src = s + 1 if reverse else s - 1
        prev = pick(a_last, src) * prev + pick(h_last, src)
        carry = jnp.where(ridx == s, prev, carry)
    return carry


def _scan_fwd(name, a, b):
    s_len, width = a.shape
    seg = s_len // SUBLANE

    def kern(a_ref, b_ref, h_ref, ap_ref):
        def p1(t, c):
            h, acc = c
            idx = pl.ds(t, SUBLANE, stride=seg)
            av = a_ref[idx, :]
            h = av * h + b_ref[idx, :]
            acc = av * acc
            h_ref[idx, :] = h
            ap_ref[idx, :] = acc
            return h, acc

        h_last, a_last = lax.fori_loop(0, seg, p1, (jnp.zeros((SUBLANE, LANE), f32), jnp.ones((SUBLANE, LANE), f32)))
        carry = _segment_carries(a_last, h_last, False)

        def p3(t, c):
            idx = pl.ds(t, SUBLANE, stride=seg)
            h_ref[idx, :] = h_ref[idx, :] + ap_ref[idx, :] * carry
            return c

        lax.fori_loop(0, seg, p3, 0)

    spec = pl.BlockSpec((s_len, LANE), lambda j: (0, j))
    return pl.pallas_call(
        kern, name=name, grid=(width // LANE,), in_specs=[spec, spec], out_specs=spec,
        out_shape=jax.ShapeDtypeStruct((s_len, width), f32), scratch_shapes=[pltpu.VMEM((s_len, LANE), f32)],
        compiler_params=pltpu.CompilerParams(dimension_semantics=("parallel",)))(a, b)


def _scan_bwd(name, a_next, h_prev, dh):
    s_len, width = dh.shape
    seg = s_len // SUBLANE

    def kern(an_ref, hp_ref, dh_ref, da_ref, db_ref, ap_ref):
        def p1(tt, c):
            g, acc = c
            idx = pl.ds(seg - 1 - tt, SUBLANE, stride=seg)
            av = an_ref[idx, :]
            g = av * g + dh_ref[idx, :]
            acc = av * acc
            db_ref[idx, :] = g
            ap_ref[idx, :] = acc
            return g, acc

        g_last, a_last = lax.fori_loop(0, seg, p1, (jnp.zeros((SUBLANE, LANE), f32), jnp.ones((SUBLANE, LANE), f32)))
        carry = _segment_carries(a_last, g_last, True)

        def p3(t, c):
            idx = pl.ds(t, SUBLANE, stride=seg)
            g = db_ref[idx, :] + ap_ref[idx, :] * carry
            db_ref[idx, :] = g
            da_ref[idx, :] = g * hp_ref[idx, :]
            return c

        lax.fori_loop(0, seg, p3, 0)

    spec = pl.BlockSpec((s_len, LANE), lambda j: (0, j))
    return pl.pallas_call(
        kern, name=name, grid=(width // LANE,), in_specs=[spec, spec, spec], out_specs=[spec, spec],
        out_shape=[jax.ShapeDtypeStruct((s_len, width), f32)] * 2, scratch_shapes=[pltpu.VMEM((s_len, LANE), f32)],
        compiler_params=pltpu.CompilerParams(dimension_semantics=("parallel",)))(a_next, h_prev, dh)


def _lane_cumsum(x, reverse):
    n = x.shape[1]
    lane = lax.broadcasted_iota(jnp.int32, x.shape, 1)
    sh = 1
    while sh < n:
        if reverse:
            x = x + jnp.where(lane < n - sh, pltpu.roll(x, n - sh, 1), 0.0)
        else:
            x = x + jnp.where(lane >= sh, pltpu.roll(x, sh, 1), 0.0)
        sh *= 2
    return x


def _decay_fwd(name, fl_t, b8):
    def kern(f_ref, b_ref, c_ref):
        c_ref[...] = _lane_cumsum(jax.nn.log_sigmoid(f_ref[...] + b_ref[...]), False)

    return pl.pallas_call(kern, name=name, out_shape=jax.ShapeDtypeStruct(fl_t.shape, f32))(fl_t, b8)


def _decay_bwd(name, fl_t, b8, dc_key, dc_query):
    def kern(f_ref, b_ref, dck_ref, dcq_ref, df_ref, db_ref):
        dlogf = _lane_cumsum(dck_ref[...] + dcq_ref[...], True)
        df = dlogf * jax.nn.sigmoid(-(f_ref[...] + b_ref[...]))
        df_ref[...] = df
        db_ref[...] = jnp.sum(df, axis=1, keepdims=True)

    return pl.pallas_call(kern, name=name, out_shape=[jax.ShapeDtypeStruct(fl_t.shape, f32),
                                                      jax.ShapeDtypeStruct((SUBLANE, 1), f32)])(fl_t, b8, dc_key, dc_query)


def _rms(x, g, n):
    return x * lax.rsqrt(jnp.sum(x * x, axis=-1, keepdims=True) * (1.0 / n) + EPS) * g


def _loss_head(name, h, target, g, tb=256):
    n, d = h.shape
    tb = min(tb, n)

    def kern(h_ref, t_ref, g_ref, loss_ref, dh_ref, dg_ref):
        i = pl.program_id(0)
        tgt = t_ref[...]

        def f(hv, gv):
            err = _rms(hv, gv, d) - tgt
            return 0.5 * jnp.sum(jnp.sum(err * err, axis=-1, keepdims=True) * (1.0 / d), axis=0, keepdims=True)

        val, vjp = jax.vjp(f, h_ref[...], g_ref[...])
        dh, dg = vjp(jnp.ones((1, 1), f32))
        dh_ref[...] = dh

        @pl.when(i == 0)
        def _():
            loss_ref[...] = jnp.zeros_like(loss_ref)
            dg_ref[...] = jnp.zeros_like(dg_ref)

        loss_ref[...] += val
        dg_ref[...] += dg

    return pl.pallas_call(
        kern, name=name, grid=(n // tb,),
        in_specs=[pl.BlockSpec((tb, d), lambda i: (i, 0)), pl.BlockSpec((tb, d), lambda i: (i, 0)),
                  pl.BlockSpec((1, d), lambda i: (0, 0))],
        out_specs=[pl.BlockSpec((1, 1), lambda i: (0, 0)), pl.BlockSpec((tb, d), lambda i: (i, 0)),
                   pl.BlockSpec((1, d), lambda i: (0, 0))],
        out_shape=[jax.ShapeDtypeStruct((1, 1), f32), jax.ShapeDtypeStruct((n, d), f32), jax.ShapeDtypeStruct((1, d), f32)],
        compiler_params=pltpu.CompilerParams(dimension_semantics=("arbitrary",)))(h, target, g)


def _f_norm(x, g):
    return (_rms(x, g, D_MODEL),)


def _f_latent(qc, kvc, gq, gkv):
    return _rms(qc, gq, MLA_Q_RANK), _rms(kvc, gkv, MLA_KV_RANK)


def _rope(x, pos, freq, m1, m2):
    ang = pos * freq
    sin = jnp.sin(ang)
    w = x.shape[1]
    return x * jnp.cos(ang) - _roll(x, w - MLA_ROPE // 2, 1) * (sin * m1) + _roll(x, MLA_ROPE // 2, 1) * (sin * m2)


def _f_mla_prep(q, kpart, kr, pos, fq, m1q, m2q, fk, m1k, m2k):
    kr = _rope(kr, pos, fk, m1k, m2k)
    return _rope(q, pos, fq, m1q, m2q), kpart + jnp.concatenate([kr] * HEADS, axis=1)


def _f_lru_gate(gates, xc, b_r, b_i, lam):
    r = jax.nn.sigmoid(gates[:, :LRU_WIDTH] + b_r)
    i = jax.nn.sigmoid(gates[:, LRU_WIDTH:] + b_i)
    log_a = -LRU_C * r * jax.nn.softplus(-lam)
    mult = jnp.sqrt(-jnp.tanh(log_a) * (1.0 + jnp.exp(2.0 * log_a)))
    return jnp.exp(log_a), mult * (i * xc)


def _f_merge(o_mla, o_fox, hs, lg, g):
    o_lru = hs * jax.nn.gelu(lg)
    return (jnp.concatenate([_rms(o_mla, g[:, :512], HEADS * MLA_V), _rms(o_fox, g[:, 512:1024], HEADS * FOX_HEAD_DIM),
                             _rms(o_lru, g[:, 1024:], LRU_WIDTH)], axis=1),)


def _f_ffn_gate(u):
    return (jax.nn.silu(u[:, :D_FF]) * u[:, D_FF:],)


def _f_ple(h, gpre, pp):
    return (h + jax.nn.sigmoid(gpre) * pp,)


MIX_PART = ["w_in", "w_uq", "w_ukv", "lru_conv_w", "w_o"]
FFN_PART = ["w_up", "ffn_conv_w", "w_down", "w_ple_gate", "w_ple_proj"]


def _prep_mix_weights(w):
    eye = jnp.eye(LRU_BLOCKS, dtype=f32)

    def block_diag(m):
        return (eye[:, None, :, None] * m[:, :, None, :]).reshape(LRU_WIDTH, LRU_WIDTH)

    return dict(
        w_in=_take_pad(w["w_in"], Z_MAP, 1),
        w_uq=_take_pad(_take_pad(w["w_uq"], UQ_COL_MAP, 1), UQ_ROW_MAP, 0),
        w_ukv=_take_pad(w["w_ukv"], UKV_MAP, 1),
        w_ri=jnp.concatenate([block_diag(w["w_r"]), block_diag(w["w_i"])], axis=1).astype(bf16),
        w_o=_take_pad(w["w_o"], OMIX_MAP, 0),
        g_mix=w["g_mix"].reshape(1, -1), g_ffn=w["g_ffn"].reshape(1, -1), g_ple=w["g_ple"].reshape(1, -1),
        g_qc=_take_pad(w["g_qc"], UQ_ROW_MAP, 0).reshape(1, -1), g_kvc=w["g_kvc"].reshape(1, -1),
        g_out=_take_pad(w["g_out"], OMIX_MAP, 0).reshape(1, -1),
        b_f8=_take_pad(w["b_f"], _pad_to(np.arange(FOX_HEADS), SUBLANE), 0).reshape(SUBLANE, 1),
        lru_conv_w=w["lru_conv_w"], lru_conv_b=w["lru_conv_b"].reshape(1, -1),
        b_r=w["b_r"].reshape(1, -1), b_i=w["b_i"].reshape(1, -1), lam=w["lru_lambda"].reshape(1, -1),
        ffn_conv_b=w["ffn_conv_b"].reshape(1, -1),
    )


def _prep_ffn_weights(w):
    return dict(w_up=w["w_up"], w_up_g=w["w_up"][:, :D_FF], w_up_v=w["w_up"][:, D_FF:], ffn_conv_w=w["ffn_conv_w"],
                w_down=w["w_down"], w_ple_gate=w["w_ple_gate"], w_ple_proj=w["w_ple_proj"])


def _rope_consts():
    fq, m1q, m2q = _rope_tables(HEADS * LANE, ROPE_AT)
    fk, m1k, m2k = _rope_tables(LANE, ROPE_AT)
    return [jnp.asarray(t) for t in (fq, m1q, m2q, fk, m1k, m2k)]


def _key_decay(c_t, s_len):
    t = _att_tile(s_len)
    return c_t[:HEADS].reshape(HEADS, s_len // t, 1, t)


def _layer_fwd(l, h0, p_l, pos, weights_of):
    s_len = h0.shape[0]
    n = f"l{l}_"
    w = _prep_mix_weights(weights_of("mix", h0))
    xn, = _rowwise(n + "norm_mix", _f_norm, [h0], [w["g_mix"]], [(D_MODEL, bf16)])
    z = _mm(n + "in_proj", xn, w["w_in"])
    zq = (z, QC_W, Z_QC // QC_W)
    zkv = (z, LANE, Z_KVC // LANE)
    zkr = (z, LANE, Z_KR // LANE)
    zlx = (z, LRU_WIDTH, Z_LX // LRU_WIDTH)
    zlg = (z, LRU_WIDTH, Z_LG // LRU_WIDTH)
    qcn, kvn = _rowwise(n + "latent_norm", _f_latent, [zq, zkv], [w["g_qc"], w["g_kvc"]], [(QC_W, bf16), (LANE, bf16)])
    q = _mm(n + "uq", qcn, w["w_uq"])
    kv = _mm(n + "ukv", kvn, w["w_ukv"])
    kpart = (kv, HEADS * LANE, 0)
    qr, kk = _rowwise(n + "mla_prep", _f_mla_prep, [q, kpart, zkr, pos], _rope_consts(),
                      [(HEADS * LANE, bf16), (HEADS * LANE, bf16)])
    mla_scale = (MLA_NOPE + MLA_ROPE) ** -0.5
    o_mla, lse_m = _attn_fwd(n + "mla_fwd", (qr, 0), (kk, 0), (kv, HEADS), mla_scale)
    fl_t = z[:, Z_FL:Z_FL + SUBLANE].T
    c_t = _decay_fwd(n + "decay", fl_t, w["b_f8"])
    c_row = _key_decay(c_t, s_len)
    fox_scale = FOX_HEAD_DIM ** -0.5
    o_fox, lse_f = _attn_fwd(n + "fox_fwd", (z, Z_FQ // LANE), (z, Z_FK // LANE), (z, Z_FV // LANE), fox_scale, c_row)
    xc = _conv_fwd(n + "lru_conv", zlx, w["lru_conv_w"], w["lru_conv_b"], LRU_CONV)
    gates = _mm(n + "lru_gates", xc, w["w_ri"])
    a, bx = _rowwise(n + "lru_gate", _f_lru_gate, [gates, xc], [w["b_r"], w["b_i"], w["lam"]],
                     [(LRU_WIDTH, f32), (LRU_WIDTH, f32)])
    hs = _scan_fwd(n + "lru_scan", a, bx)
    ocat, = _rowwise(n + "merge", _f_merge, [o_mla, o_fox, hs, zlg], [w["g_out"]], [(OMIX_W, bf16)])
    h1 = _mm(n + "out_proj", ocat, w["w_o"], res=h0)
    w.update(_prep_ffn_weights(weights_of("ffn", h1)))
    xn2, = _rowwise(n + "norm_ffn", _f_norm, [h1], [w["g_ffn"]], [(D_MODEL, bf16)])
    up = _mm(n + "up_proj", xn2, w["w_up"])
    act = _ffn_act_fwd(n + "ffn_act", up, w["ffn_conv_w"], w["ffn_conv_b"])
    h2 = _mm(n + "down_proj", act, w["w_down"], res=h1)
    hn, = _rowwise(n + "norm_ple", _f_norm, [h2], [w["g_ple"]], [(D_MODEL, bf16)])
    gpre = _mm(n + "ple_gate", hn, w["w_ple_gate"])
    pp = _mm(n + "ple_proj", p_l, w["w_ple_proj"])
    h3, = _rowwise(n + "ple_mix", _f_ple, [h2, gpre, pp], [], [(D_MODEL, f32)])
    res = dict(h0=h0, xn=xn, z=z, qcn=qcn, kvn=kvn, q=q, kv=kv, qr=qr, kk=kk, o_mla=o_mla, lse_m=lse_m, fl_t=fl_t,
               c_row=c_row, o_fox=o_fox, lse_f=lse_f, xc=xc, gates=gates, a=a, hs=hs, ocat=ocat, h1=h1,
               xn2=xn2, up=up, act=act, h2=h2, hn=hn, gpre=gpre, pp=pp, p_l=p_l)
    return h3, res, w


def _layer_bwd(l, dh3, r, pos, w, token, grads_to):
    s_len = dh3.shape[0]
    n = f"l{l}_"
    g = {}
    w = dict(w, g_ple=w["g_ple"] + token)
    z = r["z"]
    zq = (z, QC_W, Z_QC // QC_W)
    zkv = (z, LANE, Z_KVC // LANE)
    zkr = (z, LANE, Z_KR // LANE)
    zlx = (z, LRU_WIDTH, Z_LX // LRU_WIDTH)
    zlg = (z, LRU_WIDTH, Z_LG // LRU_WIDTH)
    (dh2a, dgpre, dpp), _ = _rowwise_bwd(n + "ple_mix_b", _f_ple, [r["h2"], r["gpre"], r["pp"]], [], [dh3], 3,
                                         dts=[f32, bf16, bf16])
    g["w_ple_proj"] = _mm(n + "ple_proj_dw", r["p_l"], dpp, "tn", bf16)
    dhn = _mm(n + "ple_gate_dx", dgpre, w["w_ple_gate"], "nt")
    g["w_ple_gate"] = _mm(n + "ple_gate_dw", r["hn"], dgpre, "tn", bf16)
    (dh2,), (g["g_ple"],) = _rowwise_bwd(n + "norm_ple_b", _f_norm, [r["h2"]], [w["g_ple"]], [dhn], 1, adds={0: dh2a})
    dact = _mm(n + "down_dx", dh2, w["w_down"], "nt")
    g["w_down"] = _mm(n + "down_dw", r["act"], dh2, "tn", bf16)
    dup_g, dup_v, dcw_g, dcw_v, dcb_g, dcb_v = _ffn_act_bwd(n + "ffn_act_b", r["up"], dact, w["ffn_conv_w"], w["ffn_conv_b"])
    g["ffn_conv_w"] = jnp.concatenate([dcw_g, dcw_v], axis=1)
    g["ffn_conv_b"] = jnp.concatenate([dcb_g, dcb_v], axis=1)
    dxn2 = _mm(n + "up_dx_v", dup_v, w["w_up_v"], "nt", res=_mm(n + "up_dx_g", dup_g, w["w_up_g"], "nt"))
    g["w_up"] = jnp.concatenate([_mm(n + "up_dw_g", r["xn2"], dup_g, "tn", bf16),
                                 _mm(n + "up_dw_v", r["xn2"], dup_v, "tn", bf16)], axis=1)
    (dh1,), (g["g_ffn"],) = _rowwise_bwd(n + "norm_ffn_b", _f_norm, [r["h1"]], [w["g_ffn"]], [dxn2], 1, adds={0: dh2})
    token = grads_to("ffn", dict(w_up=g["w_up"], ffn_conv_w=g["ffn_conv_w"], w_down=g["w_down"],
                                 w_ple_gate=g["w_ple_gate"], w_ple_proj=g["w_ple_proj"]))
    w = dict(w, g_out=w["g_out"] + token)
    docat = _mm(n + "out_dx", dh1, w["w_o"], "nt")
    g["w_o"] = _mm(n + "out_dw", r["ocat"], dh1, "tn", bf16)
    (do_mla, do_fox, dhs, dlg), (g["g_out"],) = _rowwise_bwd(
        n + "merge_b", _f_merge, [r["o_mla"], r["o_fox"], r["hs"], zlg], [w["g_out"]], [docat], 4)
    a, hs = r["a"], r["hs"]
    a_next = jnp.concatenate([a[1:], jnp.zeros((1, LRU_WIDTH), f32)], axis=0)
    h_prev = jnp.concatenate([jnp.zeros((1, LRU_WIDTH), f32), hs[:-1]], axis=0)
    da, dbx = _scan_bwd(n + "lru_scan_b", a_next, h_prev, dhs)
    (dgates, dxc_a), (g["b_r"], g["b_i"], g["lam"]) = _rowwise_bwd(
        n + "lru_gate_b", _f_lru_gate, [r["gates"], r["xc"]], [w["b_r"], w["b_i"], w["lam"]], [da, dbx], 2,
        dts=[bf16, f32])
    dxc_b = _mm(n + "lru_gates_dx", dgates, w["w_ri"], "nt")
    g["w_ri"] = _mm(n + "lru_gates_dw", r["xc"], dgates, "tn")
    dlx, g["lru_conv_w"], g["lru_conv_b"] = _conv_bwd(n + "lru_conv_b", zlx, dxc_a, w["lru_conv_w"], LRU_CONV, dout2=dxc_b)
    fox_scale = FOX_HEAD_DIM ** -0.5
    fq, fk, fv = (z, Z_FQ // LANE), (z, Z_FK // LANE), (z, Z_FV // LANE)
    dfq, delta_f, dc_q = _attn_dq(n + "fox_dq", fq, fk, fv, r["o_fox"], do_fox, r["lse_f"], fox_scale, r["c_row"])
    dfk, dfv, dc_k = _attn_dkv(n + "fox_dkv", fq, fk, fv, do_fox, r["lse_f"], delta_f, fox_scale, r["c_row"])
    pad_rows = jnp.zeros((SUBLANE - HEADS, s_len), f32)
    dfl_t, g["b_f8"] = _decay_bwd(n + "decay_b", r["fl_t"], w["b_f8"],
                                  jnp.concatenate([dc_k.reshape(HEADS, s_len), pad_rows], axis=0),
                                  jnp.concatenate([dc_q.reshape(HEADS, s_len), pad_rows], axis=0))
    dfl = jnp.pad(dfl_t.T, ((0, 0), (0, LANE - SUBLANE)))
    mla_scale = (MLA_NOPE + MLA_ROPE) ** -0.5
    qr, kk, kv = (r["qr"], 0), (r["kk"], 0), (r["kv"], HEADS)
    dqr, delta_m, _ = _attn_dq(n + "mla_dq", qr, kk, kv, r["o_mla"], do_mla, r["lse_m"], mla_scale)
    dkk, dv_m = _attn_dkv(n + "mla_dkv", qr, kk, kv, do_mla, r["lse_m"], delta_m, mla_scale)
    (dq, dkpart, dkr), _ = _rowwise_bwd(n + "mla_prep_b", _f_mla_prep, [r["q"], (r["kv"], HEADS * LANE, 0), zkr, pos],
                                        _rope_consts(), [dqr, dkk], 3, dts=[bf16, bf16, f32])
    dkv = jnp.concatenate([dkpart, dv_m.astype(bf16)], axis=1)
    dkvn = _mm(n + "ukv_dx", dkv, w["w_ukv"], "nt")
    g["w_ukv"] = _mm(n + "ukv_dw", r["kvn"], dkv, "tn", bf16)
    dqcn = _mm(n + "uq_dx", dq, w["w_uq"], "nt")
    g["w_uq"] = _mm(n + "uq_dw", r["qcn"], dq, "tn", bf16)
    (dqc, dkvc), (g["g_qc"], g["g_kvc"]) = _rowwise_bwd(n + "latent_norm_b", _f_latent, [zq, zkv],
                                                        [w["g_qc"], w["g_kvc"]], [dqcn, dkvn], 2)
    dz = jnp.concatenate([t.astype(bf16) for t in (dfq, dfk, dfv, dlx, dlg, dqc, dkvc, dkr, dfl)], axis=1)
    dxn = _mm(n + "in_dx", dz, w["w_in"], "nt")
    g["w_in"] = _mm(n + "in_dw", r["xn"], dz, "tn", bf16)
    (dh0,), (g["g_mix"],) = _rowwise_bwd(n + "norm_mix_b", _f_norm, [r["h0"]], [w["g_mix"]], [dxn], 1, adds={0: dh1})
    return dh0, grads_to("mix", _unpad_mix_grads(g))


def _unpad_mix_grads(g):
    d_ri = g["w_ri"]
    idx = jnp.arange(LRU_BLOCKS)

    def diag_blocks(m):
        return m.reshape(LRU_BLOCKS, LRU_BLOCK, LRU_BLOCKS, LRU_BLOCK)[idx, :, idx, :]

    return dict(
        g_mix=g["g_mix"][0], w_in=_take_inv(g["w_in"], Z_MAP, 1), g_qc=g["g_qc"][0, :MLA_Q_RANK],
        w_uq=_take_inv(g["w_uq"][:MLA_Q_RANK], UQ_COL_MAP, 1), g_kvc=g["g_kvc"][0],
        w_ukv=_take_inv(g["w_ukv"], UKV_MAP, 1), b_f=g["b_f8"][:FOX_HEADS, 0],
        lru_conv_w=g["lru_conv_w"], lru_conv_b=g["lru_conv_b"][0],
        w_r=diag_blocks(d_ri[:, :LRU_WIDTH]), b_r=g["b_r"][0], w_i=diag_blocks(d_ri[:, LRU_WIDTH:]), b_i=g["b_i"][0],
        lru_lambda=g["lam"][0], g_out=_take_inv(g["g_out"][0], OMIX_MAP, 0), w_o=_take_inv(g["w_o"], OMIX_MAP, 0),
        g_ffn=g["g_ffn"][0], ffn_conv_b=g["ffn_conv_b"][0], g_ple=g["g_ple"][0],
    )


LAYER_WEIGHTS = ["g_mix", "w_in", "g_qc", "w_uq", "g_kvc", "w_ukv", "b_f", "lru_conv_w", "lru_conv_b", "w_r", "b_r", "w_i",
                 "b_i", "lru_lambda", "g_out", "w_o", "g_ffn", "w_up", "ffn_conv_w", "ffn_conv_b", "w_down", "g_ple",
                 "w_ple_gate", "w_ple_proj"]
WEIGHTS = LAYER_WEIGHTS + ["g_final"]


def _local_step(x, p, pos, target, g_final, weights_of, grads_to):
    h = x
    ws, saved = [], []
    for l in range(DEPTH):
        h, r, w = _layer_fwd(l, h, p[l], pos, functools.partial(weights_of, l))
        ws.append(w)
        saved.append(r)
    loss, dh, dg_final = _loss_head("loss_head", h, target, g_final.reshape(1, -1))
    token = jnp.zeros((), f32)
    for l in reversed(range(DEPTH)):
        dh, token = _layer_bwd(l, dh, saved[l], pos, ws[l], token, functools.partial(grads_to, l))
    return loss[0, 0], dh, dg_final[0]


MESH_AXES = ("x", "y", "c")


def _exchange(name, src, axes, scatter, pieces=1):
    n = 2 ** len(axes)
    flips = [tuple((f >> (len(axes) - 1 - b)) & 1 for b in range(len(axes))) for f in range(1, n)]
    rows = src.shape[-2]
    piece_rows = rows // pieces
    assert piece_rows * pieces == rows

    def body(src_ref, out_ref, send_sems, recv_sems, local_sem):
        coords = {a: lax.axis_index(a) for a in MESH_AXES}

        def index_of(cd):
            idx = 0
            for a in axes:
                idx = idx * 2 + cd[a]
            return idx

        me = index_of(coords)
        local = pltpu.make_async_copy(src_ref.at[me] if scatter else src_ref, out_ref.at[me], local_sem)
        local.start()
        copies = []
        for k, f in enumerate(flips):
            peer = dict(coords)
            for a, bit in zip(axes, f):
                if bit:
                    peer[a] = 1 - coords[a]
            slab = src_ref.at[index_of(peer)] if scatter else src_ref
            for pc in range(pieces):
                span = pl.ds(pc * piece_rows, piece_rows)
                cp = pltpu.make_async_remote_copy(
                    src_ref=slab.at[span], dst_ref=out_ref.at[me, span],
                    send_sem=send_sems.at[k * pieces + pc], recv_sem=recv_sems.at[k * pieces + pc],
                    device_id=tuple(peer[a] for a in MESH_AXES), device_id_type=pl.DeviceIdType.MESH)
                cp.start()
                copies.append(cp)
        for cp in copies:
            cp.wait()
        local.wait()

    n_sems = (n - 1) * pieces
    return pl.pallas_call(
        body, name=name, out_shape=jax.ShapeDtypeStruct((n, rows, LANE), src.dtype),
        in_specs=[pl.BlockSpec(memory_space=pl.ANY)], out_specs=pl.BlockSpec(memory_space=pl.ANY),
        scratch_shapes=[pltpu.SemaphoreType.DMA((n_sems,)), pltpu.SemaphoreType.DMA((n_sems,)), pltpu.SemaphoreType.DMA])(src)


def _row_tile(rows, cap):
    if rows <= cap:
        return rows
    for t in range(cap, SUBLANE - 1, -SUBLANE):
        if rows % t == 0:
            return t
    return rows


def _sum_slabs(name, a):
    n, rows, _ = a.shape
    tr = _row_tile(rows, 512)

    def kern(a_ref, o_ref):
        acc = a_ref[0].astype(f32)
        for k in range(1, n):
            acc = acc + a_ref[k].astype(f32)
        o_ref[...] = acc

    return pl.pallas_call(
        kern, name=name, grid=(rows // tr,), in_specs=[pl.BlockSpec((n, tr, LANE), lambda i: (0, i, 0))],
        out_specs=pl.BlockSpec((tr, LANE), lambda i: (i, 0)), out_shape=jax.ShapeDtypeStruct((rows, LANE), f32),
        compiler_params=pltpu.CompilerParams(dimension_semantics=("parallel",)))(a)


ADAM_BLOCK_BYTES = 2 ** 20


def _adamw(name, w, g, m, v):
    rows, cols = w.shape
    tr = _row_tile(rows, max(SUBLANE, ADAM_BLOCK_BYTES // (4 * cols) // SUBLANE * SUBLANE))

    def kern(w_ref, g_ref, m_ref, v_ref, d_ref, nm_ref, nv_ref):
        gv = g_ref[...]
        nm = ADAM_B1 * m_ref[...] + (1.0 - ADAM_B1) * gv
        nv = ADAM_B2 * v_ref[...] + (1.0 - ADAM_B2) * (gv * gv)
        m_hat = nm / (1.0 - ADAM_B1 ** ADAM_STEP)
        v_hat = nv / (1.0 - ADAM_B2 ** ADAM_STEP)
        d_ref[...] = -ADAM_LR * (m_hat / (jnp.sqrt(v_hat) + ADAM_EPS) + ADAM_WD * w_ref[...])
        nm_ref[...] = nm
        nv_ref[...] = nv

    spec = pl.BlockSpec((tr, cols), lambda i: (i, 0))
    return pl.pallas_call(
        kern, name=name, grid=(rows // tr,), in_specs=[spec] * 4, out_specs=[spec] * 3,
        out_shape=[jax.ShapeDtypeStruct((rows, cols), f32)] * 3,
        compiler_params=pltpu.CompilerParams(dimension_semantics=("parallel",)))(w, g, m, v)


def _pack(arrays, row_multiple):
    flat = jnp.concatenate([a.reshape(-1) for a in arrays])
    per = LANE * row_multiple
    total = -(-flat.shape[0] // per) * per
    return jnp.pad(flat, (0, total - flat.shape[0])).reshape(-1, LANE)


def _unpack(buf, shapes):
    flat = buf.reshape(-1)
    out, at = [], 0
    for s in shapes:
        size = int(np.prod(s))
        out.append(flat[at:at + size].reshape(s))
        at += size
    return out


SHARD_AXIS = {"w_in": 2, "w_uq": 2, "w_ukv": 2, "lru_conv_w": 2, "w_o": 1, "w_up": 2, "ffn_conv_w": 2, "w_down": 1,
              "w_ple_gate": 1, "w_ple_proj": 2}
SHARDED = [k for k in WEIGHTS if k in SHARD_AXIS]
REPLICATED = [k for k in WEIGHTS if k not in SHARD_AXIS]
ELEMENTWISE_F32 = ("lru_conv_w", "ffn_conv_w")
N_SHARDS = 4
BF16_TILE_ROWS = 16


HBM_SPEC = pl.BlockSpec(memory_space=pl.ANY)
SEM_SPEC = pl.BlockSpec(memory_space=pltpu.SEMAPHORE)
SPLIT_EFFECT = pltpu.SideEffectType.DATAFLOW_SIDE_EFFECTING
CHIP_FLIPS = ((1, 0), (0, 1), (1, 1))
N_DEVICES = 8
SUM_BLOCK_BYTES = 4 * 2 ** 20


def _device_index():
    return 4 * lax.axis_index("x") + 2 * lax.axis_index("y") + lax.axis_index("c")


def _when(cond, fn):
    if cond is None:
        fn()
    else:
        pl.when(cond)(fn)


class _Exchange:
    def __init__(self, name, plan, srcs, land_shapes, n_send, n_recv):
        self.name, self.plan, self.srcs, self.n = name, plan, list(srcs), len(srcs)
        self.land_shapes, self.n_send, self.n_recv = land_shapes, n_send, n_recv

    def run(self):
        n = self.n

        def body(*refs):
            sends, arrivals = self.plan(refs[:n], refs[n:2 * n], refs[2 * n], refs[2 * n + 1])
            for cond, cp in sends:
                _when(cond, cp.start)
            for cond, cp in arrivals:
                _when(cond, cp.wait_recv)
            for cond, cp in sends:
                _when(cond, cp.wait_send)

        return pl.pallas_call(
            body, name=self.name, out_shape=self.land_shapes, in_specs=[HBM_SPEC] * n, out_specs=[HBM_SPEC] * n,
            scratch_shapes=[pltpu.SemaphoreType.DMA((self.n_send,)), pltpu.SemaphoreType.DMA((self.n_recv,))])(*self.srcs)

    def start(self, after=None):
        n = self.n
        lands = [lax.empty(s.shape, s.dtype) for s in self.land_shapes]
        extra = [] if after is None else [after]

        def body(*refs):
            ins, lands_in = refs[:n], refs[n:2 * n]
            send_sems, recv_sems, token = refs[2 * n + len(extra)], refs[2 * n + len(extra) + 1], refs[-1]
            sends, _ = self.plan(ins, lands_in, send_sems, recv_sems)
            for cond, cp in sends:
                _when(cond, cp.start)
            token[...] = jnp.zeros_like(token)

        hbm = [pltpu.with_memory_space_constraint(a, pltpu.HBM) for a in self.srcs + lands]
        res = pl.pallas_call(
            body, name=self.name + "_start",
            out_shape=(pltpu.SemaphoreType.DMA((self.n_send,)), pltpu.SemaphoreType.DMA((self.n_recv,)),
                       *[pltpu.HBM(a.shape, a.dtype) for a in hbm], jax.ShapeDtypeStruct((SUBLANE, LANE), f32)),
            in_specs=[HBM_SPEC] * (2 * n + len(extra)),
            out_specs=(SEM_SPEC, SEM_SPEC, *[HBM_SPEC] * (2 * n), pl.BlockSpec(memory_space=pltpu.VMEM)),
            input_output_aliases={i: 2 + i for i in range(2 * n)},
            compiler_params=pltpu.CompilerParams(has_side_effects=SPLIT_EFFECT))(*hbm, *extra)
        self.sems, self.thru, token = res[:2], res[2:2 + 2 * n], res[-1]
        return token[0, 0]

    def finish(self, after):
        n = self.n

        def body(*refs):
            ins, lands_in, send_sems, recv_sems = refs[:n], refs[n:2 * n], refs[2 * n], refs[2 * n + 1]
            sends, arrivals = self.plan(ins, lands_in, send_sems, recv_sems)
            for cond, cp in arrivals:
                _when(cond, cp.wait_recv)
            for cond, cp in sends:
                _when(cond, cp.wait_send)

        res = pl.pallas_call(
            body, name=self.name + "_finish", out_shape=tuple(pltpu.HBM(a.shape, a.dtype) for a in self.thru),
            in_specs=[HBM_SPEC] * (2 * n) + [SEM_SPEC, SEM_SPEC, HBM_SPEC], out_specs=tuple([HBM_SPEC] * (2 * n)),
            input_output_aliases={i: i for i in range(2 * n)},
            compiler_params=pltpu.CompilerParams(has_side_effects=SPLIT_EFFECT))(*self.thru, *self.sems, after)
        return list(res[n:])


def _gather_exchange(name, shards):
    def plan(ins, lands, send_sems, recv_sems):
        x, y, c = (lax.axis_index(a) for a in MESH_AXES)
        copies = []
        for i in range(len(ins)):
            for k, (fx, fy) in enumerate(CHIP_FLIPS):
                peer = (1 - x if fx else x, 1 - y if fy else y, c)
                copies.append((None, pltpu.make_async_remote_copy(
                    src_ref=ins[i], dst_ref=lands[i].at[2 * x + y], send_sem=send_sems.at[3 * i + k],
                    recv_sem=recv_sems.at[3 * i + k], device_id=peer, device_id_type=pl.DeviceIdType.MESH)))
        return copies, copies

    n = len(shards)
    return _Exchange(name, plan, shards, [jax.ShapeDtypeStruct((N_SHARDS,) + s.shape, s.dtype) for s in shards], 3 * n, 3 * n)


def _scatter_exchange(name, layer, chunks):
    def plan(ins, lands, send_sems, recv_sems):
        x, y, c = (lax.axis_index(a) for a in MESH_AXES)
        me = _device_index()
        sends, arrivals = [], []
        for i in range(len(ins)):
            for j in range(N_SHARDS):
                target = (j // 2, j % 2, layer)
                remote = jnp.logical_not((x == target[0]) & (y == target[1]) & (c == layer))
                sends.append((remote, pltpu.make_async_remote_copy(
                    src_ref=ins[i].at[j], dst_ref=lands[i].at[me], send_sem=send_sems.at[N_SHARDS * i + j],
                    recv_sem=recv_sems.at[N_DEVICES * i + me], device_id=target, device_id_type=pl.DeviceIdType.MESH)))
            for s in range(N_DEVICES):
                arrivals.append(((c == layer) & (me != s), pltpu.make_async_remote_copy(
                    src_ref=ins[i].at[0], dst_ref=lands[i].at[s], send_sem=send_sems.at[0],
                    recv_sem=recv_sems.at[N_DEVICES * i + s], device_id=(x, y, c), device_id_type=pl.DeviceIdType.MESH)))
        return sends, arrivals

    n = len(chunks)
    lands = [jax.ShapeDtypeStruct((N_DEVICES,) + ch.shape[1:], ch.dtype) for ch in chunks]
    return _Exchange(name, plan, chunks, lands, N_SHARDS * n, N_DEVICES * n)


def _sum_contributions(name, got, mine):
    _, a, b = got.shape
    ta = _row_tile(a, max(SUBLANE, SUM_BLOCK_BYTES // (N_DEVICES * b * got.dtype.itemsize) // SUBLANE * SUBLANE))

    def kern(got_ref, mine_ref, o_ref):
        me = _device_index()
        acc = jnp.zeros(o_ref.shape, f32)
        for s in range(N_DEVICES):
            acc = acc + jnp.where(me == s, mine_ref[...].astype(f32), got_ref[s].astype(f32))
        o_ref[...] = acc

    return pl.pallas_call(
        kern, name=name, grid=(a // ta,),
        in_specs=[pl.BlockSpec((N_DEVICES, ta, b), lambda i: (0, i, 0)), pl.BlockSpec((ta, b), lambda i: (i, 0))],
        out_specs=pl.BlockSpec((ta, b), lambda i: (i, 0)), out_shape=jax.ShapeDtypeStruct((a, b), f32),
        compiler_params=pltpu.CompilerParams(dimension_semantics=("parallel",)))(got, mine)


def _swap_layers(name, sums):
    n = len(sums[0])

    def body(*refs):
        srcs = (refs[:n], refs[n:2 * n])
        outs, (send_sems, recv_sems) = refs[2 * n:3 * n], refs[3 * n:]
        x, y, c = (lax.axis_index(a) for a in MESH_AXES)
        for i in range(n):
            for layer in range(DEPTH):
                cp = pltpu.make_async_remote_copy(
                    src_ref=srcs[layer][i], dst_ref=outs[i], send_sem=send_sems.at[i], recv_sem=recv_sems.at[i],
                    device_id=(x, y, 1 - c), device_id_type=pl.DeviceIdType.MESH)
                pl.when(c == layer)(cp.start)
        for i in range(n):
            pltpu.make_async_remote_copy(
                src_ref=srcs[0][i], dst_ref=outs[i], send_sem=send_sems.at[i], recv_sem=recv_sems.at[i],
                device_id=(x, y, 1 - c), device_id_type=pl.DeviceIdType.MESH).wait()

    return pl.pallas_call(
        body, name=name, out_shape=[jax.ShapeDtypeStruct(s.shape, s.dtype) for s in sums[0]],
        in_specs=[HBM_SPEC] * (2 * n), out_specs=[HBM_SPEC] * n,
        scratch_shapes=[pltpu.SemaphoreType.DMA((n,)), pltpu.SemaphoreType.DMA((n,))])(*sums[0], *sums[1])


def _stack_shards(g, axis):
    if axis == 1:
        return g.reshape(N_SHARDS, g.shape[0] // N_SHARDS, g.shape[1])
    return g.reshape(g.shape[0], N_SHARDS, g.shape[1] // N_SHARDS).transpose(1, 0, 2)


def _join_shards(s, axis):
    if axis == 1:
        return s.reshape(-1, s.shape[2])
    return s.transpose(1, 0, 2).reshape(s.shape[1], -1)


def _layer_shards(w, l, names):
    return [w[k][l] if k in ELEMENTWISE_F32 else w[k][l].astype(bf16) for k in names]


def _full_weights(names, sent, got):
    j = 2 * lax.axis_index("x") + lax.axis_index("y")
    return {k: _join_shards(lax.dynamic_update_slice(g, own[None], (j, 0, 0)), SHARD_AXIS[k])
            for k, own, g in zip(names, sent, got)}


def _grad_chunks(grads, names):
    return [_stack_shards(grads[k], SHARD_AXIS[k]).astype(bf16) for k in names]


def _sum_group(l, names, got, chunks):
    j = 2 * lax.axis_index("x") + lax.axis_index("y")
    return {k: _sum_contributions(f"sum_l{l}_{k}", g, lax.dynamic_index_in_dim(ch, j, 0, keepdims=False))
            for k, g, ch in zip(names, got, chunks)}


def _both_layers(sums):
    c = lax.axis_index("c")
    other = _swap_layers("swap_layers", sums)
    return {k: jnp.stack([jnp.where(c == 0, sums[0][i], other[i]), jnp.where(c == 0, other[i], sums[1][i])])
            for i, k in enumerate(SHARDED)}


def kernel(x, p, positions, g_mix, w_in, g_qc, w_uq, g_kvc, w_ukv, b_f, lru_conv_w, lru_conv_b, w_r, b_r, w_i, b_i, lru_lambda, g_out, w_o, g_ffn, w_up, ffn_conv_w, ffn_conv_b, w_down, g_ple, w_ple_gate, w_ple_proj, g_final, loss_target, m_g_mix, m_w_in, m_g_qc, m_w_uq, m_g_kvc, m_w_ukv, m_b_f, m_lru_conv_w, m_lru_conv_b, m_w_r, m_b_r, m_w_i, m_b_i, m_lru_lambda, m_g_out, m_w_o, m_g_ffn, m_w_up, m_ffn_conv_w, m_ffn_conv_b, m_w_down, m_g_ple, m_w_ple_gate, m_w_ple_proj, m_g_final, v_g_mix, v_w_in, v_g_qc, v_w_uq, v_g_kvc, v_w_ukv, v_b_f, v_lru_conv_w, v_lru_conv_b, v_w_r, v_b_r, v_w_i, v_b_i, v_lru_lambda, v_g_out, v_w_o, v_g_ffn, v_w_up, v_ffn_conv_w, v_ffn_conv_b, v_w_down, v_g_ple, v_w_ple_gate, v_w_ple_proj, v_g_final):
    given = locals()
    w = {k: given[k] for k in WEIGHTS}
    m = {k: given["m_" + k] for k in WEIGHTS}
    v = {k: given["v_" + k] for k in WEIGHTS}

    parts = {"mix": MIX_PART, "ffn": FFN_PART}
    groups = [(l, part) for l in range(DEPTH) for part in ("mix", "ffn")]
    sent = {g: _layer_shards(w, g[0], parts[g[1]]) for g in groups}
    first = _gather_exchange("gather_l0_mix", sent[groups[0]]).run()
    ahead = {g: _gather_exchange(f"gather_l{g[0]}_{g[1]}", sent[g]) for g in groups[1:]}
    pos = positions[0].astype(f32).reshape(-1, 1)
    for ex in ahead.values():
        pos = pos + ex.start(after=first[0])
    behind, layer_grads, chunks = {}, [{} for _ in range(DEPTH)], {}

    def weights_of(l, part, after):
        g = (l, part)
        full = _full_weights(parts[part], sent[g], first if g == groups[0] else ahead[g].finish(after=after))
        if part == "mix":
            full.update({k: w[k][l] for k in LAYER_WEIGHTS if k in REPLICATED})
        return full

    def grads_to(l, part, grads):
        g = (l, part)
        layer_grads[l].update(grads)
        chunks[g] = _grad_chunks(grads, parts[part])
        if g == groups[0]:
            return jnp.zeros((), f32)
        behind[g] = _scatter_exchange(f"scatter_l{l}_{part}", l, chunks[g])
        return behind[g].start()

    loss, dx, dg_final = _local_step(x[0], p[:, 0], pos, loss_target[0], w["g_final"], weights_of, grads_to)

    sums = [{} for _ in range(DEPTH)]
    for g in groups:
        got = _scatter_exchange("scatter_l0_mix", 0, chunks[g]).run() if g == groups[0] else behind[g].finish(after=dx)
        sums[g[0]].update(_sum_group(g[0], parts[g[1]], got, chunks[g]))
    g_sharded = _both_layers([[sums[l][k] for k in SHARDED] for l in range(DEPTH)])
    big = [[], [], [], []]
    for k in SHARDED:
        shape = w[k].shape
        flat = [t.reshape(-1, shape[-1]) for t in (w[k], g_sharded[k], m[k], v[k])]
        for kind, res in enumerate((flat[1],) + tuple(_adamw("adamw_" + k, *flat))):
            big[kind].append(res.reshape(shape))

    grads = {k: jnp.stack([layer_grads[l][k] for l in range(DEPTH)]) for k in LAYER_WEIGHTS if k in REPLICATED}
    grads["g_final"] = dg_final
    rep_shapes = [w[k].shape for k in REPLICATED] + [(1,)]
    contrib = _pack([grads[k] for k in REPLICATED] + [loss.reshape(1)], SUBLANE)
    g_rep = _sum_slabs("sum_replicated", _exchange("gather_replicated", contrib, MESH_AXES, scatter=False))
    zero = jnp.zeros((1,), f32)
    w_rep, m_rep, v_rep = (_pack([t[k] for k in REPLICATED] + [zero], SUBLANE) for t in (w, m, v))
    rep = [_unpack(b, rep_shapes) for b in (g_rep,) + tuple(_adamw("adamw_replicated", w_rep, g_rep, m_rep, v_rep))]

    outs = []
    for kind in range(4):
        by_name = dict(zip(SHARDED, big[kind]))
        by_name.update(zip(REPLICATED, rep[kind][:-1]))
        outs.append([by_name[k] for k in WEIGHTS])
    total_loss = rep[0][-1][0]
    return (total_loss, dx.reshape(x.shape), *outs[0], *outs[1], *outs[2], *outs[3])
```

```python
import functools
import math

import numpy as np
import jax
import jax.numpy as jnp
from jax import lax
from jax.experimental import pallas as pl
from jax.experimental.pallas import tpu as pltpu

f32, bf16 = jnp.float32, jnp.bfloat16

D_MODEL = 1024
PLE_DIM = 256
MLA_HEADS, MLA_NOPE, MLA_ROPE, MLA_V = 4, 64, 32, 64
MLA_Q_RANK, MLA_KV_RANK = 192, 128
FOX_HEADS, FOX_HEAD_DIM = 4, 64
LRU_WIDTH, LRU_BLOCKS, LRU_BLOCK, LRU_CONV, LRU_C = 512, 8, 64, 4, 8.0
D_FF, FFN_CONV = 2816, 3
ROPE_THETA = 10000.0
EPS = 1e-6
DEPTH = 2
ADAM_LR, ADAM_B1, ADAM_B2, ADAM_EPS, ADAM_WD, ADAM_STEP = 0.001, 0.9, 0.999, 1e-08, 0.01, 10

LANE = 128
SUBLANE = 8
HEADS = 4

Z_FQ, Z_FK, Z_FV, Z_LX, Z_LG, Z_QC, Z_KVC, Z_KR, Z_FL, Z_W = 0, 512, 1024, 1536, 2048, 2560, 2816, 2944, 3072, 3200
QC_W = 256
ROPE_AT = 64


def _head_pad_map(n_heads, width):
    m = -np.ones(n_heads * LANE, np.int64)
    for h in range(n_heads):
        m[h * LANE:h * LANE + width] = h * width + np.arange(width)
    return m


def _z_map():
    m = -np.ones(Z_W, np.int64)
    o_qc, o_kvc, o_kr = 0, MLA_Q_RANK, MLA_Q_RANK + MLA_KV_RANK
    o_fq = o_kr + MLA_ROPE
    o_fk, o_fv = o_fq + 256, o_fq + 512
    o_fl = o_fv + 256
    o_lx = o_fl + FOX_HEADS
    o_lg = o_lx + LRU_WIDTH
    m[Z_FQ:Z_FQ + 512] = np.where(_head_pad_map(4, 64) >= 0, _head_pad_map(4, 64) + o_fq, -1)
    m[Z_FK:Z_FK + 512] = np.where(_head_pad_map(4, 64) >= 0, _head_pad_map(4, 64) + o_fk, -1)
    m[Z_FV:Z_FV + 512] = np.where(_head_pad_map(4, 64) >= 0, _head_pad_map(4, 64) + o_fv, -1)
    m[Z_LX:Z_LX + 512] = o_lx + np.arange(512)
    m[Z_LG:Z_LG + 512] = o_lg + np.arange(512)
    m[Z_QC:Z_QC + MLA_Q_RANK] = o_qc + np.arange(MLA_Q_RANK)
    m[Z_KVC:Z_KVC + MLA_KV_RANK] = o_kvc + np.arange(MLA_KV_RANK)
    m[Z_KR + ROPE_AT:Z_KR + ROPE_AT + MLA_ROPE] = o_kr + np.arange(MLA_ROPE)
    m[Z_FL:Z_FL + FOX_HEADS] = o_fl + np.arange(FOX_HEADS)
    return m


def _ukv_map():
    m = -np.ones(2 * HEADS * LANE, np.int64)
    for h in range(HEADS):
        m[h * LANE:h * LANE + MLA_NOPE] = h * (MLA_NOPE + MLA_V) + np.arange(MLA_NOPE)
        m[HEADS * LANE + h * LANE:HEADS * LANE + h * LANE + MLA_V] = h * (MLA_NOPE + MLA_V) + MLA_NOPE + np.arange(MLA_V)
    return m


def _omix_map():
    return np.concatenate([_head_pad_map(4, 64), np.where(_head_pad_map(4, 64) >= 0, _head_pad_map(4, 64) + 256, -1),
                           512 + np.arange(512)])


def _pad_to(m, n):
    return np.concatenate([m, -np.ones(n - m.shape[0], np.int64)])


def _take_pad(a, m, axis):
    out = jnp.take(a, jnp.asarray(np.maximum(m, 0), jnp.int32), axis=axis)
    shape = [1] * a.ndim
    shape[axis] = m.shape[0]
    return out * jnp.asarray((m >= 0).reshape(shape), a.dtype)


def _take_inv(a, m, axis):
    n = int(m.max()) + 1
    inv = np.zeros(n, np.int64)
    inv[m[m >= 0]] = np.nonzero(m >= 0)[0]
    return jnp.take(a, jnp.asarray(inv, jnp.int32), axis=axis)


Z_MAP = _z_map()
UQ_COL_MAP = _head_pad_map(HEADS, MLA_NOPE + MLA_ROPE)
UQ_ROW_MAP = _pad_to(np.arange(MLA_Q_RANK), QC_W)
UKV_MAP = _ukv_map()
OMIX_MAP = _omix_map()
OMIX_W = 1536


def _rope_tables(width, at):
    half = MLA_ROPE // 2
    inv = ROPE_THETA ** (-np.arange(half, dtype=np.float32) / half)
    freq = np.zeros((1, width), np.float32)
    m1 = np.zeros((1, width), np.float32)
    m2 = np.zeros((1, width), np.float32)
    for h in range(width // LANE):
        b = h * LANE + at
        freq[0, b:b + half] = inv
        freq[0, b + half:b + 2 * half] = inv
        m1[0, b:b + half] = 1.0
        m2[0, b + half:b + 2 * half] = 1.0
    return freq, m1, m2


def _view(r):
    return r if isinstance(r, tuple) else (r, r.shape[1], 0)


def _blk(dim, cap):
    if dim <= cap:
        return dim
    for b in range(cap, LANE - 1, -LANE):
        if dim % b == 0:
            return b
    return dim


@functools.partial(jax.custom_vjp, nondiff_argnums=(1, 2))
def _roll(x, shift, axis):
    return pltpu.roll(x, shift, axis)


def _roll_fwd(x, shift, axis):
    return pltpu.roll(x, shift, axis), None


def _roll_bwd(shift, axis, _, g):
    return (pltpu.roll(g, g.shape[axis] - shift, axis),)


_roll.defvjp(_roll_fwd, _roll_bwd)


def _rowwise(name, fn, rows, pars, outs, tb=256):
    rows = [_view(r) for r in rows]
    n = rows[0][0].shape[0]
    tb = min(tb, n)
    nr, npar = len(rows), len(pars)

    def kern(*refs):
        r = [refs[k][...].astype(f32) for k in range(nr)]
        p = [refs[nr + k][...] for k in range(npar)]
        res = fn(*r, *p)
        for o_ref, o in zip(refs[nr + npar:], res):
            o_ref[...] = o.astype(o_ref.dtype)

    in_specs = [pl.BlockSpec((tb, w), lambda i, j=idx: (i, j)) for (_, w, idx) in rows]
    in_specs += [pl.BlockSpec(p.shape, lambda i: (0, 0)) for p in pars]
    out_specs = [pl.BlockSpec((tb, w), lambda i: (i, 0)) for (w, _) in outs]
    out_shape = [jax.ShapeDtypeStruct((n, w), dt) for (w, dt) in outs]
    return pl.pallas_call(kern, name=name, grid=(n // tb,), in_specs=in_specs, out_specs=out_specs, out_shape=out_shape,
                          compiler_params=pltpu.CompilerParams(dimension_semantics=("parallel",)))(*[r[0] for r in rows], *pars)


def _rowwise_bwd(name, fn, rows, pars, cts, ndiff, adds=None, tb=256, dts=None):
    rows = [_view(r) for r in rows]
    dts = dts or [f32] * ndiff
    adds = adds or {}
    add_keys = sorted(adds)
    n = rows[0][0].shape[0]
    tb = min(tb, n)
    nr, npar, nct, nadd = len(rows), len(pars), len(cts), len(add_keys)

    def kern(*refs):
        i = pl.program_id(0)
        r = [refs[k][...].astype(f32) for k in range(nr)]
        p = [refs[nr + k][...] for k in range(npar)]
        ct = [refs[nr + npar + k][...].astype(f32) for k in range(nct)]
        ad = {key: refs[nr + npar + nct + k][...] for k, key in enumerate(add_keys)}
        o_refs = refs[nr + npar + nct + nadd:]

        def g(*d):
            return tuple(fn(*d[:ndiff], *r[ndiff:], *d[ndiff:]))

        _, vjp = jax.vjp(g, *r[:ndiff], *p)
        grads = vjp(tuple(ct))
        for k in range(ndiff):
            gk = grads[k]
            if k in ad:
                gk = gk + ad[k]
            o_refs[k][...] = gk.astype(o_refs[k].dtype)

        @pl.when(i == 0)
        def _():
            for k in range(npar):
                o_refs[ndiff + k][...] = jnp.zeros_like(o_refs[ndiff + k])

        for k in range(npar):
            o_refs[ndiff + k][...] += grads[ndiff + k]

    in_specs = [pl.BlockSpec((tb, w), lambda i, j=idx: (i, j)) for (_, w, idx) in rows]
    in_specs += [pl.BlockSpec(p.shape, lambda i: (0, 0)) for p in pars]
    in_specs += [pl.BlockSpec((tb, c.shape[1]), lambda i: (i, 0)) for c in cts]
    in_specs += [pl.BlockSpec((tb, adds[k].shape[1]), lambda i: (i, 0)) for k in add_keys]
    out_specs = [pl.BlockSpec((tb, rows[k][1]), lambda i: (i, 0)) for k in range(ndiff)]
    out_specs += [pl.BlockSpec(p.shape, lambda i: (0, 0)) for p in pars]
    out_shape = [jax.ShapeDtypeStruct((n, rows[k][1]), dts[k]) for k in range(ndiff)]
    out_shape += [jax.ShapeDtypeStruct(p.shape, f32) for p in pars]
    res = pl.pallas_call(kern, name=name, grid=(n // tb,), in_specs=in_specs, out_specs=out_specs, out_shape=out_shape,
                         compiler_params=pltpu.CompilerParams(dimension_semantics=("arbitrary",)))(
        *[r[0] for r in rows], *pars, *cts, *[adds[k] for k in add_keys])
    return res[:ndiff], res[ndiff:]


_DOT_DIMS = {"nn": ((1,), (0,)), "nt": ((1,), (1,)), "tn": ((0,), (0,))}

MM_VMEM_BUDGET = 36 * 2 ** 20
MM_MAX_TM = 1024
MM_STEP, MM_RESULT, MM_XPOSE, MM_CAST = 700.0, 7.5e-4, 9e-4, 1e-3


def _tile_candidates(dim):
    c = [d for d in range(LANE, dim + 1, LANE) if dim % d == 0]
    return c or [dim]


@functools.lru_cache(maxsize=None)
def _mm_tiles(mode, m, n, k, a_bytes, b_bytes, o_bytes):
    best, best_cost = None, None
    for tm in _tile_candidates(m):
        if tm > MM_MAX_TM:
            continue
        for tn in _tile_candidates(n):
            for tk in _tile_candidates(k):
                vmem = 2 * (tm * tk * a_bytes + tk * tn * b_bytes + tm * tn * o_bytes) + 4 * tm * tn * (2 if tk < k else 1)
                vmem += (2 * tm * tk if a_bytes > 2 else 0) + (2 * tk * tn if b_bytes > 2 else 0)
                if vmem > MM_VMEM_BUDGET:
                    continue
                steps = (m // tm) * (n // tn) * (k // tk)
                cost = steps * MM_STEP + m * n * (k // tk) * MM_RESULT
                if mode == "tn":
                    cost += m * k * (n // tn) * MM_XPOSE
                cost += (m * k * (n // tn) * MM_CAST if a_bytes > 2 else 0) + (k * n * (m // tm) * MM_CAST if b_bytes > 2 else 0)
                if best is None or cost < best_cost:
                    best, best_cost = (tm, tn, tk), cost
    return best


def _mm(name, a, b, mode="nn", out_dtype=f32, res=None):
    if mode == "nn":
        (m, k), (_, n) = a.shape, b.shape
    elif mode == "nt":
        (m, k), (n, _) = a.shape, b.shape
    else:
        (k, m), (_, n) = a.shape, b.shape
    has_res = res is not None
    tm, tn, tk = _mm_tiles(mode, m, n, k, a.dtype.itemsize, b.dtype.itemsize,
                           jnp.dtype(out_dtype).itemsize + (res.dtype.itemsize if has_res else 0))
    nk = k // tk
    dims = (_DOT_DIMS[mode], ((), ()))

    def kern(*refs):
        a_ref, b_ref = refs[0], refs[1]
        o_ref, acc_ref = refs[-2], refs[-1]
        kk = pl.program_id(2)
        part = lax.dot_general(a_ref[...].astype(bf16), b_ref[...].astype(bf16), dims, preferred_element_type=f32)

        def finish(out):
            if has_res:
                out = out + refs[2][...]
            o_ref[...] = out.astype(o_ref.dtype)

        if nk == 1:
            finish(part)
            return

        @pl.when(kk == 0)
        def _():
            acc_ref[...] = part

        @pl.when(jnp.logical_and(kk > 0, kk < nk - 1))
        def _():
            acc_ref[...] += part

        @pl.when(kk == nk - 1)
        def _():
            finish(acc_ref[...] + part)

    if mode == "tn":
        a_spec = pl.BlockSpec((tk, tm), lambda i, j, kk: (kk, i))
    else:
        a_spec = pl.BlockSpec((tm, tk), lambda i, j, kk: (i, kk))
    if mode == "nt":
        b_spec = pl.BlockSpec((tn, tk), lambda i, j, kk: (j, kk))
    else:
        b_spec = pl.BlockSpec((tk, tn), lambda i, j, kk: (kk, j))
    in_specs = [a_spec, b_spec]
    args = [a, b]
    if has_res:
        in_specs.append(pl.BlockSpec((tm, tn), lambda i, j, kk: (i, j)))
        args.append(res)
    return pl.pallas_call(
        kern, name=name, grid=(m // tm, n // tn, nk), in_specs=in_specs,
        out_specs=pl.BlockSpec((tm, tn), lambda i, j, kk: (i, j)),
        out_shape=jax.ShapeDtypeStruct((m, n), out_dtype),
        scratch_shapes=[pltpu.VMEM((tm, tn) if nk > 1 else (SUBLANE, LANE), f32)],
        compiler_params=pltpu.CompilerParams(dimension_semantics=("parallel", "parallel", "arbitrary")))(*args)


ATT_T = 512


def _att_tile(s):
    return min(ATT_T, s)


def _fold_scale(scale):
    return (scale, 1.0) if math.frexp(scale)[0] == 0.5 else (1.0, scale)


def _scores(qb, kb, s_mul, ck, diagonal, t):
    s = lax.dot_general(qb, kb, (_DOT_DIMS["nt"], ((), ())), preferred_element_type=f32)
    if s_mul != 1.0:
        s = s * s_mul
    if ck is not None:
        s = s - ck
    if not diagonal:
        return s
    row = lax.broadcasted_iota(jnp.int32, (t, t), 0)
    col = lax.broadcasted_iota(jnp.int32, (t, t), 1)
    return jnp.where(col <= row, s, -jnp.inf)


def _attn_fwd(name, q, k, v, scale, c_row=None):
    (qa, qo), (ka, ko), (va, vo) = q, k, v
    s_len = qa.shape[0]
    t = _att_tile(s_len)
    nt = s_len // t
    decay = c_row is not None
    q_mul, s_mul = _fold_scale(scale)

    def kern(*refs):
        q_ref, k_ref, v_ref = refs[:3]
        o_ref, lse_ref = refs[-2:]
        i = pl.program_id(1)
        qb = (q_ref[...] * q_mul).astype(bf16)

        def step(j, carry, diagonal):
            m, l, acc = carry
            rows = pl.ds(pl.multiple_of(j * t, t), t)
            kb = k_ref[rows, :].astype(bf16)
            vb = v_ref[rows, :].astype(bf16)
            s = _scores(qb, kb, s_mul, refs[3][j] if decay else None, diagonal, t)
            m_new = jnp.maximum(m, jnp.max(s, axis=1, keepdims=True))
            alpha = jnp.exp(m - m_new)
            p = jnp.exp(s - m_new)
            l = alpha * l + jnp.sum(p, axis=1, keepdims=True)
            acc = alpha * acc + jnp.dot(p.astype(bf16), vb, preferred_element_type=f32)
            return m_new, l, acc

        init = (jnp.full((t, 1), -jnp.inf, f32), jnp.zeros((t, 1), f32), jnp.zeros((t, LANE), f32))
        m, l, acc = step(i, lax.fori_loop(0, i, lambda j, c: step(j, c, False), init), True)
        o_ref[...] = acc / l
        lse_ref[...] = m + jnp.log(l)

    in_specs = [pl.BlockSpec((t, LANE), lambda h, i: (i, qo + h)),
                pl.BlockSpec((s_len, LANE), lambda h, i: (0, ko + h)),
                pl.BlockSpec((s_len, LANE), lambda h, i: (0, vo + h))]
    args = [qa, ka, va]
    if decay:
        in_specs.append(pl.BlockSpec((None, nt, 1, t), lambda h, i: (h, 0, 0, 0)))
        args.append(c_row)
    return pl.pallas_call(
        kern, name=name, grid=(HEADS, nt), in_specs=in_specs,
        out_specs=[pl.BlockSpec((t, LANE), lambda h, i: (i, h)), pl.BlockSpec((None, t, 1), lambda h, i: (h, i, 0))],
        out_shape=[jax.ShapeDtypeStruct((s_len, HEADS * LANE), f32), jax.ShapeDtypeStruct((HEADS, s_len, 1), f32)],
        compiler_params=pltpu.CompilerParams(dimension_semantics=("parallel", "arbitrary")))(*args)


def _attn_dq(name, q, k, v, o, do, lse, scale, c_row=None):
    (qa, qo), (ka, ko), (va, vo) = q, k, v
    s_len = qa.shape[0]
    t = _att_tile(s_len)
    nt = s_len // t
    decay = c_row is not None
    q_mul, s_mul = _fold_scale(scale)

    def kern(*refs):
        q_ref, k_ref, v_ref, o_ref, do_ref, lse_ref = refs[:6]
        dq_ref, delta_ref, drow_ref = refs[-3:]
        i = pl.program_id(1)
        qb = (q_ref[...] * q_mul).astype(bf16)
        dob = do_ref[...]
        delta = jnp.sum(dob * o_ref[...], axis=1, keepdims=True)
        dob = dob.astype(bf16)
        lse = lse_ref[...]

        def step(j, carry, diagonal):
            dq, drow = carry
            rows = pl.ds(pl.multiple_of(j * t, t), t)
            kb = k_ref[rows, :].astype(bf16)
            vb = v_ref[rows, :].astype(bf16)
            s = _scores(qb, kb, s_mul, refs[6][j] if decay else None, diagonal, t)
            p = jnp.exp(s - lse)
            dp = lax.dot_general(dob, vb, (_DOT_DIMS["nt"], ((), ())), preferred_element_type=f32)
            ds = p * (dp - delta)
            return dq + jnp.dot(ds.astype(bf16), kb, preferred_element_type=f32), drow + jnp.sum(ds, axis=1, keepdims=True)

        init = (jnp.zeros((t, LANE), f32), jnp.zeros((t, 1), f32))
        dq, drow = step(i, lax.fori_loop(0, i, lambda j, c: step(j, c, False), init), True)
        dq_ref[...] = dq * scale
        delta_ref[...] = delta
        drow_ref[...] = drow

    in_specs = [pl.BlockSpec((t, LANE), lambda h, i: (i, qo + h)),
                pl.BlockSpec((s_len, LANE), lambda h, i: (0, ko + h)),
                pl.BlockSpec((s_len, LANE), lambda h, i: (0, vo + h)),
                pl.BlockSpec((t, LANE), lambda h, i: (i, h)),
                pl.BlockSpec((t, LANE), lambda h, i: (i, h)),
                pl.BlockSpec((None, t, 1), lambda h, i: (h, i, 0))]
    args = [qa, ka, va, o, do, lse]
    if decay:
        in_specs.append(pl.BlockSpec((None, nt, 1, t), lambda h, i: (h, 0, 0, 0)))
        args.append(c_row)
    col = pl.BlockSpec((None, t, 1), lambda h, i: (h, i, 0))
    return pl.pallas_call(
        kern, name=name, grid=(HEADS, nt), in_specs=in_specs,
        out_specs=[pl.BlockSpec((t, LANE), lambda h, i: (i, h)), col, col],
        out_shape=[jax.ShapeDtypeStruct((s_len, HEADS * LANE), f32), jax.ShapeDtypeStruct((HEADS, s_len, 1), f32),
                   jax.ShapeDtypeStruct((HEADS, s_len, 1), f32)],
        compiler_params=pltpu.CompilerParams(dimension_semantics=("parallel", "arbitrary")))(*args)


def _attn_dkv(name, q, k, v, do, lse, delta, scale, c_row=None):
    (qa, qo), (ka, ko), (va, vo) = q, k, v
    s_len = qa.shape[0]
    t = _att_tile(s_len)
    nt = s_len // t
    decay = c_row is not None
    q_mul, s_mul = _fold_scale(scale)

    def kern(*refs):
        q_ref, k_ref, v_ref, do_ref, lse_ref, delta_ref = refs[:6]
        j = pl.program_id(1)
        kb = k_ref[...].astype(bf16)
        vb = v_ref[...].astype(bf16)
        ck = refs[6][...] if decay else None

        def step(i, carry, diagonal):
            dk, dv, dc = carry
            rows = pl.ds(pl.multiple_of(i * t, t), t)
            qb = (q_ref[rows, :] * q_mul).astype(bf16)
            dob = do_ref[rows, :].astype(bf16)
            s = _scores(qb, kb, s_mul, ck, diagonal, t)
            p = jnp.exp(s - lse_ref[rows, :])
            dv = dv + lax.dot_general(p.astype(bf16), dob, (_DOT_DIMS["tn"], ((), ())), preferred_element_type=f32)
            dp = lax.dot_general(dob, vb, (_DOT_DIMS["nt"], ((), ())), preferred_element_type=f32)
            ds = p * (dp - delta_ref[rows, :])
            dk = dk + lax.dot_general(ds.astype(bf16), qb, (_DOT_DIMS["tn"], ((), ())), preferred_element_type=f32)
            if decay:
                dc = dc - jnp.sum(ds, axis=0, keepdims=True)
            return dk, dv, dc

        init = (jnp.zeros((t, LANE), f32), jnp.zeros((t, LANE), f32), jnp.zeros((1, t), f32))
        dk, dv, dc = lax.fori_loop(j + 1, nt, lambda i, c: step(i, c, False), step(j, init, True))
        if decay:
            dk_ref, dv_ref, dc_ref = refs[-3:]
            dc_ref[...] = dc
        else:
            dk_ref, dv_ref = refs[-2:]
        dk_ref[...] = dk * s_mul
        dv_ref[...] = dv

    in_specs = [pl.BlockSpec((s_len, LANE), lambda h, j: (0, qo + h)),
                pl.BlockSpec((t, LANE), lambda h, j: (j, ko + h)),
                pl.BlockSpec((t, LANE), lambda h, j: (j, vo + h)),
                pl.BlockSpec((s_len, LANE), lambda h, j: (0, h)),
                pl.BlockSpec((None, s_len, 1), lambda h, j: (h, 0, 0)),
                pl.BlockSpec((None, s_len, 1), lambda h, j: (h, 0, 0))]
    args = [qa, ka, va, do, lse, delta]
    out_specs = [pl.BlockSpec((t, LANE), lambda h, j: (j, h)), pl.BlockSpec((t, LANE), lambda h, j: (j, h))]
    out_shape = [jax.ShapeDtypeStruct((s_len, HEADS * LANE), f32), jax.ShapeDtypeStruct((s_len, HEADS * LANE), f32)]
    if decay:
        in_specs.append(pl.BlockSpec((None, None, 1, t), lambda h, j: (h, j, 0, 0)))
        args.append(c_row)
        out_specs.append(pl.BlockSpec((None, None, 1, t), lambda h, j: (h, j, 0, 0)))
        out_shape.append(jax.ShapeDtypeStruct((HEADS, nt, 1, t), f32))
    return pl.pallas_call(
        kern, name=name, grid=(HEADS, nt), in_specs=in_specs, out_specs=out_specs, out_shape=out_shape,
        compiler_params=pltpu.CompilerParams(dimension_semantics=("parallel", "arbitrary")))(*args)


CONV_TS, CONV_CB = 1024, 256


def _conv_fwd(name, x, w, b, taps):
    xa, width, xidx = _view(x)
    s_len = xa.shape[0]
    ts, cb = min(CONV_TS, s_len), CONV_CB
    xo = xidx * width // cb

    def kern(x_ref, halo_ref, w_ref, b_ref, o_ref):
        i = pl.program_id(1)
        xb = x_ref[...]
        halo = jnp.where(i == 0, 0.0, halo_ref[...])
        xx = jnp.concatenate([halo, xb], axis=0)
        out = b_ref[...] + w_ref[taps - 1:taps, :] * xb
        for k in range(taps - 1):
            out = out + w_ref[k:k + 1, :] * pltpu.roll(xx, taps - 1 - k, 0)[SUBLANE:]
        o_ref[...] = out

    return pl.pallas_call(
        kern, name=name, grid=(width // cb, s_len // ts),
        in_specs=[pl.BlockSpec((ts, cb), lambda j, i: (i, xo + j)),
                  pl.BlockSpec((SUBLANE, cb), lambda j, i: (jnp.maximum(i * (ts // SUBLANE) - 1, 0), xo + j)),
                  pl.BlockSpec((taps, cb), lambda j, i: (0, j)),
                  pl.BlockSpec((1, cb), lambda j, i: (0, j))],
        out_specs=pl.BlockSpec((ts, cb), lambda j, i: (i, j)),
        out_shape=jax.ShapeDtypeStruct((s_len, width), f32),
        compiler_params=pltpu.CompilerParams(dimension_semantics=("parallel", "parallel")))(xa, xa, w, b)


def _conv_bwd(name, x, dout, w, taps, dout2=None, dx_dtype=f32):
    xa, width, xidx = _view(x)
    s_len = xa.shape[0]
    ts, cb = min(CONV_TS, s_len), CONV_CB
    xo = xidx * width // cb
    n_i = s_len // ts
    two = dout2 is not None

    def kern(*refs):
        x_ref, halo_ref, w_ref = refs[:3]
        dx_ref, dw_ref, db_ref = refs[-3:]
        i = pl.program_id(1)
        if two:
            d = refs[3][...] + refs[5][...]
            dn = refs[4][...] + refs[6][...]
        else:
            d, dn = refs[3][...], refs[4][...]
        dn = jnp.where(i == n_i - 1, 0.0, dn)
        xb = x_ref[...]
        halo = jnp.where(i == 0, 0.0, halo_ref[...])
        xx = jnp.concatenate([halo, xb], axis=0)
        dd = jnp.concatenate([d, dn], axis=0)

        @pl.when(i == 0)
        def _():
            dw_ref[...] = jnp.zeros_like(dw_ref)
            db_ref[...] = jnp.zeros_like(db_ref)

        dx = w_ref[taps - 1:taps, :] * d
        dw_ref[taps - 1:taps, :] += jnp.sum(d * xb, axis=0, keepdims=True)
        for k in range(taps - 1):
            sh = taps - 1 - k
            dx = dx + w_ref[k:k + 1, :] * pltpu.roll(dd, ts + SUBLANE - sh, 0)[:ts]
            dw_ref[k:k + 1, :] += jnp.sum(d * pltpu.roll(xx, sh, 0)[SUBLANE:], axis=0, keepdims=True)
        dx_ref[...] = dx.astype(dx_ref.dtype)
        db_ref[...] += jnp.sum(d, axis=0, keepdims=True)

    d_spec = pl.BlockSpec((ts, cb), lambda j, i: (i, j))
    dn_spec = pl.BlockSpec((SUBLANE, cb), lambda j, i: (jnp.minimum((i + 1) * (ts // SUBLANE), s_len // SUBLANE - 1), j))
    in_specs = [pl.BlockSpec((ts, cb), lambda j, i: (i, xo + j)),
                pl.BlockSpec((SUBLANE, cb), lambda j, i: (jnp.maximum(i * (ts // SUBLANE) - 1, 0), xo + j)),
                pl.BlockSpec((taps, cb), lambda j, i: (0, j)), d_spec, dn_spec]
    args = [xa, xa, w, dout, dout]
    if two:
        in_specs += [d_spec, dn_spec]
        args += [dout2, dout2]
    return pl.pallas_call(
        kern, name=name, grid=(width // cb, n_i), in_specs=in_specs,
        out_specs=[pl.BlockSpec((ts, cb), lambda j, i: (i, j)), pl.BlockSpec((taps, cb), lambda j, i: (0, j)),
                   pl.BlockSpec((1, cb), lambda j, i: (0, j))],
        out_shape=[jax.ShapeDtypeStruct((s_len, width), dx_dtype), jax.ShapeDtypeStruct((taps, width), f32),
                   jax.ShapeDtypeStruct((1, width), f32)],
        compiler_params=pltpu.CompilerParams(dimension_semantics=("parallel", "arbitrary")))(*args)


def _conv_rows(xx, w_ref, b_ref, taps):
    out = b_ref[...] + w_ref[taps - 1:taps, :] * xx[SUBLANE:]
    for k in range(taps - 1):
        out = out + w_ref[k:k + 1, :] * pltpu.roll(xx, taps - 1 - k, 0)[SUBLANE:]
    return out


def _ffn_act_fwd(name, up, w, b):
    s_len = up.shape[0]
    ts, cb = min(CONV_TS, s_len), CONV_CB
    nf = D_FF // cb

    def kern(g_ref, gp_ref, v_ref, vp_ref, wg_ref, wv_ref, bg_ref, bv_ref, o_ref):
        first = pl.program_id(1) == 0
        ug = _conv_rows(jnp.concatenate([jnp.where(first, 0.0, gp_ref[...]), g_ref[...]], axis=0), wg_ref, bg_ref, FFN_CONV)
        uv = _conv_rows(jnp.concatenate([jnp.where(first, 0.0, vp_ref[...]), v_ref[...]], axis=0), wv_ref, bv_ref, FFN_CONV)
        o_ref[...] = (jax.nn.silu(ug) * uv).astype(o_ref.dtype)

    def half(off):
        return [pl.BlockSpec((ts, cb), lambda j, i: (i, off + j)),
                pl.BlockSpec((SUBLANE, cb), lambda j, i: (jnp.maximum(i * (ts // SUBLANE) - 1, 0), off + j))]

    def par(rows, off):
        return pl.BlockSpec((rows, cb), lambda j, i: (0, off + j))

    return pl.pallas_call(
        kern, name=name, grid=(nf, s_len // ts),
        in_specs=half(0) + half(nf) + [par(FFN_CONV, 0), par(FFN_CONV, nf), par(1, 0), par(1, nf)],
        out_specs=pl.BlockSpec((ts, cb), lambda j, i: (i, j)),
        out_shape=jax.ShapeDtypeStruct((s_len, D_FF), bf16),
        compiler_params=pltpu.CompilerParams(dimension_semantics=("parallel", "parallel")))(up, up, up, up, w, w, b, b)


def _ffn_act_bwd(name, up, dact, w, b):
    s_len = up.shape[0]
    ts, cb = min(CONV_TS, s_len), CONV_CB
    nf = D_FF // cb
    n_i = s_len // ts
    taps = FFN_CONV

    def kern(g_ref, gp_ref, gn_ref, v_ref, vp_ref, vn_ref, d_ref, dn_ref, wg_ref, wv_ref, bg_ref, bv_ref,
             dg_ref, dv_ref, dwg_ref, dwv_ref, dbg_ref, dbv_ref):
        i = pl.program_id(1)
        first, last = i == 0, i == n_i - 1

        def extended(x_ref, p_ref, n_ref):
            return jnp.concatenate([jnp.where(first, 0.0, p_ref[...]), x_ref[...], jnp.where(last, 0.0, n_ref[...])], axis=0)

        gx, vx = extended(g_ref, gp_ref, gn_ref), extended(v_ref, vp_ref, vn_ref)
        ug, uv = _conv_rows(gx, wg_ref, bg_ref, taps), _conv_rows(vx, wv_ref, bv_ref, taps)
        dd = jnp.concatenate([d_ref[...], jnp.where(last, 0.0, dn_ref[...])], axis=0)
        sg = jax.nn.sigmoid(ug)
        dug = dd * uv * (sg * (1.0 + ug * (1.0 - sg)))
        duv = dd * (ug * sg)

        @pl.when(first)
        def _():
            for ref in (dwg_ref, dwv_ref, dbg_ref, dbv_ref):
                ref[...] = jnp.zeros_like(ref)

        def transposed(du, xx, w_ref, dx_ref, dw_ref, db_ref):
            d = du[:ts]
            dx = w_ref[taps - 1:taps, :] * d
            dw_ref[taps - 1:taps, :] += jnp.sum(d * xx[SUBLANE:SUBLANE + ts], axis=0, keepdims=True)
            for k in range(taps - 1):
                sh = taps - 1 - k
                dx = dx + w_ref[k:k + 1, :] * pltpu.roll(du, ts + SUBLANE - sh, 0)[:ts]
                dw_ref[k:k + 1, :] += jnp.sum(d * pltpu.roll(xx, sh, 0)[SUBLANE:SUBLANE + ts], axis=0, keepdims=True)
            dx_ref[...] = dx.astype(dx_ref.dtype)
            db_ref[...] += jnp.sum(d, axis=0, keepdims=True)

        transposed(dug, gx, wg_ref, dg_ref, dwg_ref, dbg_ref)
        transposed(duv, vx, wv_ref, dv_ref, dwv_ref, dbv_ref)

    blocks = s_len // SUBLANE

    def half(off):
        return [pl.BlockSpec((ts, cb), lambda j, i: (i, off + j)),
                pl.BlockSpec((SUBLANE, cb), lambda j, i: (jnp.maximum(i * (ts // SUBLANE) - 1, 0), off + j)),
                pl.BlockSpec((SUBLANE, cb), lambda j, i: (jnp.minimum((i + 1) * (ts // SUBLANE), blocks - 1), off + j))]

    def par(rows, off):
        return pl.BlockSpec((rows, cb), lambda j, i: (0, off + j))

    d_specs = [pl.BlockSpec((ts, cb), lambda j, i: (i, j)),
               pl.BlockSpec((SUBLANE, cb), lambda j, i: (jnp.minimum((i + 1) * (ts // SUBLANE), blocks - 1), j))]
    out_par = [pl.BlockSpec((r, cb), lambda j, i: (0, j)) for r in (taps, taps, 1, 1)]
    return pl.pallas_call(
        kern, name=name, grid=(nf, n_i),
        in_specs=half(0) + half(nf) + d_specs + [par(taps, 0), par(taps, nf), par(1, 0), par(1, nf)],
        out_specs=[pl.BlockSpec((ts, cb), lambda j, i: (i, j))] * 2 + out_par,
        out_shape=[jax.ShapeDtypeStruct((s_len, D_FF), bf16)] * 2 + [jax.ShapeDtypeStruct((taps, D_FF), f32)] * 2
        + [jax.ShapeDtypeStruct((1, D_FF), f32)] * 2,
        compiler_params=pltpu.CompilerParams(dimension_semantics=("parallel", "arbitrary")))(
        up, up, up, up, up, up, dact, dact, w, w, b, b)


SCAN_ROWS = 128


def _block_scan(a, b, reverse):
    t = a.shape[0]
    row = lax.broadcasted_iota(jnp.int32, a.shape, 0)
    d = 1
    while d < t:
        keep = row < t - d if reverse else row >= d
        shift = t - d if reverse else d
        a_far = jnp.where(keep, pltpu.roll(a, shift, 0), 1.0)
        b_far = jnp.where(keep, pltpu.roll(b, shift, 0), 0.0)
        b = a * b_far + b
        a = a * a_far
        d *= 2
    return a, b


def _scan_fwd(name, a, b):
    s_len, width = a.shape
    t = min(SCAN_ROWS, s_len)

    def kern(a_ref, b_ref, h_ref):
        def block(k, carry):
            rows = pl.ds(pl.multiple_of(k * t, t), t)
            acc, h = _block_scan(a_ref[rows, :], b_ref[rows, :], False)
            h_ref[rows, :] = h + acc * carry
            return h_ref[pl.ds(k * t + t - 1, 1), :]

        lax.fori_loop(0, s_len // t, block, jnp.zeros((1, LANE), f32))

    spec = pl.BlockSpec((s_len, LANE), lambda j: (0, j))
    return pl.pallas_call(
        kern, name=name, grid=(width // LANE,), in_specs=[spec, spec], out_specs=spec,
        out_shape=jax.ShapeDtypeStruct((s_len, width), f32),
        compiler_params=pltpu.CompilerParams(dimension_semantics=("parallel",)))(a, b)


def _scan_bwd(name, a_next, h_prev, dh):
    s_len, width = dh.shape
    t = min(SCAN_ROWS, s_len)
    n_blocks = s_len // t

    def kern(an_ref, hp_ref, dh_ref, da_ref, db_ref):
        def block(kk, carry):
            k = n_blocks - 1 - kk
            rows = pl.ds(pl.multiple_of(k * t, t), t)
            acc, g = _block_scan(an_ref[rows, :], dh_ref[rows, :], True)
            g = g + acc * carry
            db_ref[rows, :] = g
            da_ref[rows, :] = g * hp_ref[rows, :]
            return db_ref[pl.ds(k * t, 1), :]

        lax.fori_loop(0, n_blocks, block, jnp.zeros((1, LANE), f32))

    spec = pl.BlockSpec((s_len, LANE), lambda j: (0, j))
    return pl.pallas_call(
        kern, name=name, grid=(width // LANE,), in_specs=[spec, spec, spec], out_specs=[spec, spec],
        out_shape=[jax.ShapeDtypeStruct((s_len, width), f32)] * 2,
        compiler_params=pltpu.CompilerParams(dimension_semantics=("parallel",)))(a_next, h_prev, dh)


def _lane_cumsum(x, reverse):
    n = x.shape[1]
    lane = lax.broadcasted_iota(jnp.int32, x.shape, 1)
    sh = 1
    while sh < n:
        if reverse:
            x = x + jnp.where(lane < n - sh, pltpu.roll(x, n - sh, 1), 0.0)
        else:
            x = x + jnp.where(lane >= sh, pltpu.roll(x, sh, 1), 0.0)
        sh *= 2
    return x


def _decay_fwd(name, fl_t, b8):
    def kern(f_ref, b_ref, c_ref):
        c_ref[...] = _lane_cumsum(jax.nn.log_sigmoid(f_ref[...] + b_ref[...]), False)

    return pl.pallas_call(kern, name=name, out_shape=jax.ShapeDtypeStruct(fl_t.shape, f32))(fl_t, b8)


def _decay_bwd(name, fl_t, b8, dc_key, dc_query):
    def kern(f_ref, b_ref, dck_ref, dcq_ref, df_ref, db_ref):
        dlogf = _lane_cumsum(dck_ref[...] + dcq_ref[...], True)
        df = dlogf * jax.nn.sigmoid(-(f_ref[...] + b_ref[...]))
        df_ref[...] = df
        db_ref[...] = jnp.sum(df, axis=1, keepdims=True)

    return pl.pallas_call(kern, name=name, out_shape=[jax.ShapeDtypeStruct(fl_t.shape, f32),
                                                      jax.ShapeDtypeStruct((SUBLANE, 1), f32)])(fl_t, b8, dc_key, dc_query)


def _rms(x, g, n):
    return x * lax.rsqrt(jnp.sum(x * x, axis=-1, keepdims=True) * (1.0 / n) + EPS) * g


def _loss_head(name, h, target, g, tb=256):
    n, d = h.shape
    tb = min(tb, n)

    def kern(h_ref, t_ref, g_ref, loss_ref, dh_ref, dg_ref):
        i = pl.program_id(0)
        tgt = t_ref[...]

        def f(hv, gv):
            err = _rms(hv, gv, d) - tgt
            return 0.5 * jnp.sum(jnp.sum(err * err, axis=-1, keepdims=True) * (1.0 / d), axis=0, keepdims=True)

        val, vjp = jax.vjp(f, h_ref[...], g_ref[...])
        dh, dg = vjp(jnp.ones((1, 1), f32))
        dh_ref[...] = dh

        @pl.when(i == 0)
        def _():
            loss_ref[...] = jnp.zeros_like(loss_ref)
            dg_ref[...] = jnp.zeros_like(dg_ref)

        loss_ref[...] += val
        dg_ref[...] += dg

    return pl.pallas_call(
        kern, name=name, grid=(n // tb,),
        in_specs=[pl.BlockSpec((tb, d), lambda i: (i, 0)), pl.BlockSpec((tb, d), lambda i: (i, 0)),
                  pl.BlockSpec((1, d), lambda i: (0, 0))],
        out_specs=[pl.BlockSpec((1, 1), lambda i: (0, 0)), pl.BlockSpec((tb, d), lambda i: (i, 0)),
                   pl.BlockSpec((1, d), lambda i: (0, 0))],
        out_shape=[jax.ShapeDtypeStruct((1, 1), f32), jax.ShapeDtypeStruct((n, d), f32), jax.ShapeDtypeStruct((1, d), f32)],
        compiler_params=pltpu.CompilerParams(dimension_semantics=("arbitrary",)))(h, target, g)


def _f_norm(x, g):
    return (_rms(x, g, D_MODEL),)


def _f_latent(qc, kvc, gq, gkv):
    return _rms(qc, gq, MLA_Q_RANK), _rms(kvc, gkv, MLA_KV_RANK)


def _f_rope_table(pos, freq, m1, m2):
    ang = pos * freq
    sin = jnp.sin(ang)
    return jnp.cos(ang), -sin * m1, sin * m2


def _rope(x, cos, s_up, s_down):
    w = x.shape[1]
    return x * cos + _roll(x, w - MLA_ROPE // 2, 1) * s_up + _roll(x, MLA_ROPE // 2, 1) * s_down


def _f_mla_prep(q, kpart, kr, cos, s_up, s_down):
    def heads(t):
        return jnp.concatenate([t] * HEADS, axis=1)

    kr = _rope(kr, cos, s_up, s_down)
    return _rope(q, heads(cos), heads(s_up), heads(s_down)), kpart + heads(kr)


def _f_lru_gate(gates, xc, b_r, b_i, lam):
    r = jax.nn.sigmoid(gates[:, :LRU_WIDTH] + b_r)
    i = jax.nn.sigmoid(gates[:, LRU_WIDTH:] + b_i)
    log_a = -LRU_C * r * jax.nn.softplus(-lam)
    mult = jnp.sqrt(-jnp.tanh(log_a) * (1.0 + jnp.exp(2.0 * log_a)))
    return jnp.exp(log_a), mult * (i * xc)


def _f_merge(o_mla, o_fox, hs, lg, g):
    o_lru = hs * jax.nn.gelu(lg)
    return (jnp.concatenate([_rms(o_mla, g[:, :512], HEADS * MLA_V), _rms(o_fox, g[:, 512:1024], HEADS * FOX_HEAD_DIM),
                             _rms(o_lru, g[:, 1024:], LRU_WIDTH)], axis=1),)


def _f_ffn_gate(u):
    return (jax.nn.silu(u[:, :D_FF]) * u[:, D_FF:],)


def _f_ple(h, gpre, pp):
    return (h + jax.nn.sigmoid(gpre) * pp,)


MIX_PART = ["w_in", "w_uq", "w_ukv", "lru_conv_w", "w_o"]
FFN_PART = ["w_up", "ffn_conv_w", "w_down", "w_ple_gate", "w_ple_proj"]


def _prep_mix_weights(w):
    eye = jnp.eye(LRU_BLOCKS, dtype=f32)

    def block_diag(m):
        return (eye[:, None, :, None] * m[:, :, None, :]).reshape(LRU_WIDTH, LRU_WIDTH)

    return dict(
        w_in=_take_pad(w["w_in"], Z_MAP, 1),
        w_uq=_take_pad(_take_pad(w["w_uq"], UQ_COL_MAP, 1), UQ_ROW_MAP, 0),
        w_ukv=_take_pad(w["w_ukv"], UKV_MAP, 1),
        w_ri=jnp.concatenate([block_diag(w["w_r"]), block_diag(w["w_i"])], axis=1).astype(bf16),
        w_o=_take_pad(w["w_o"], OMIX_MAP, 0),
        g_mix=w["g_mix"].reshape(1, -1), g_ffn=w["g_ffn"].reshape(1, -1), g_ple=w["g_ple"].reshape(1, -1),
        g_qc=_take_pad(w["g_qc"], UQ_ROW_MAP, 0).reshape(1, -1), g_kvc=w["g_kvc"].reshape(1, -1),
        g_out=_take_pad(w["g_out"], OMIX_MAP, 0).reshape(1, -1),
        b_f8=_take_pad(w["b_f"], _pad_to(np.arange(FOX_HEADS), SUBLANE), 0).reshape(SUBLANE, 1),
        lru_conv_w=w["lru_conv_w"], lru_conv_b=w["lru_conv_b"].reshape(1, -1),
        b_r=w["b_r"].reshape(1, -1), b_i=w["b_i"].reshape(1, -1), lam=w["lru_lambda"].reshape(1, -1),
        ffn_conv_b=w["ffn_conv_b"].reshape(1, -1),
    )


def _prep_ffn_weights(w):
    return dict(w_up=w["w_up"], w_up_g=w["w_up"][:, :D_FF], w_up_v=w["w_up"][:, D_FF:], ffn_conv_w=w["ffn_conv_w"],
                w_down=w["w_down"], w_ple_gate=w["w_ple_gate"], w_ple_proj=w["w_ple_proj"])


def _rope_rows(pos):
    consts = [jnp.asarray(t) for t in _rope_tables(LANE, ROPE_AT)]
    return _rowwise("rope_table", _f_rope_table, [pos], consts, [(LANE, f32)] * 3)


def _key_decay(c_t, s_len):
    t = _att_tile(s_len)
    return c_t[:HEADS].reshape(HEADS, s_len // t, 1, t)


def _layer_fwd(l, h0, p_l, rope, weights_of):
    s_len = h0.shape[0]
    n = f"l{l}_"
    w = _prep_mix_weights(weights_of("mix", h0))
    xn, = _rowwise(n + "norm_mix", _f_norm, [h0], [w["g_mix"]], [(D_MODEL, bf16)])
    z = _mm(n + "in_proj", xn, w["w_in"])
    zq = (z, QC_W, Z_QC // QC_W)
    zkv = (z, LANE, Z_KVC // LANE)
    zkr = (z, LANE, Z_KR // LANE)
    zlx = (z, LRU_WIDTH, Z_LX // LRU_WIDTH)
    zlg = (z, LRU_WIDTH, Z_LG // LRU_WIDTH)
    qcn, kvn = _rowwise(n + "latent_norm", _f_latent, [zq, zkv], [w["g_qc"], w["g_kvc"]], [(QC_W, bf16), (LANE, bf16)])
    q = _mm(n + "uq", qcn, w["w_uq"])
    kv = _mm(n + "ukv", kvn, w["w_ukv"])
    kpart = (kv, HEADS * LANE, 0)
    qr, kk = _rowwise(n + "mla_prep", _f_mla_prep, [q, kpart, zkr, *rope], [],
                      [(HEADS * LANE, bf16), (HEADS * LANE, bf16)])
    mla_scale = (MLA_NOPE + MLA_ROPE) ** -0.5
    o_mla, lse_m = _attn_fwd(n + "mla_fwd", (qr, 0), (kk, 0), (kv, HEADS), mla_scale)
    fl_t = z[:, Z_FL:Z_FL + SUBLANE].T
    c_t = _decay_fwd(n + "decay", fl_t, w["b_f8"])
    c_row = _key_decay(c_t, s_len)
    fox_scale = FOX_HEAD_DIM ** -0.5
    o_fox, lse_f = _attn_fwd(n + "fox_fwd", (z, Z_FQ // LANE), (z, Z_FK // LANE), (z, Z_FV // LANE), fox_scale, c_row)
    xc = _conv_fwd(n + "lru_conv", zlx, w["lru_conv_w"], w["lru_conv_b"], LRU_CONV)
    gates = _mm(n + "lru_gates", xc, w["w_ri"])
    a, bx = _rowwise(n + "lru_gate", _f_lru_gate, [gates, xc], [w["b_r"], w["b_i"], w["lam"]],
                     [(LRU_WIDTH, f32), (LRU_WIDTH, f32)])
    hs = _scan_fwd(n + "lru_scan", a, bx)
    ocat, = _rowwise(n + "merge", _f_merge, [o_mla, o_fox, hs, zlg], [w["g_out"]], [(OMIX_W, bf16)])
    h1 = _mm(n + "out_proj", ocat, w["w_o"], res=h0)
    w.update(_prep_ffn_weights(weights_of("ffn", h1)))
    xn2, = _rowwise(n + "norm_ffn", _f_norm, [h1], [w["g_ffn"]], [(D_MODEL, bf16)])
    up = _mm(n + "up_proj", xn2, w["w_up"])
    act = _ffn_act_fwd(n + "ffn_act", up, w["ffn_conv_w"], w["ffn_conv_b"])
    h2 = _mm(n + "down_proj", act, w["w_down"], res=h1)
    hn, = _rowwise(n + "norm_ple", _f_norm, [h2], [w["g_ple"]], [(D_MODEL, bf16)])
    gpre = _mm(n + "ple_gate", hn, w["w_ple_gate"])
    pp = _mm(n + "ple_proj", p_l, w["w_ple_proj"])
    h3, = _rowwise(n + "ple_mix", _f_ple, [h2, gpre, pp], [], [(D_MODEL, f32)])
    res = dict(h0=h0, xn=xn, z=z, qcn=qcn, kvn=kvn, q=q, kv=kv, qr=qr, kk=kk, o_mla=o_mla, lse_m=lse_m, fl_t=fl_t,
               c_row=c_row, o_fox=o_fox, lse_f=lse_f, xc=xc, gates=gates, a=a, hs=hs, ocat=ocat, h1=h1,
               xn2=xn2, up=up, act=act, h2=h2, hn=hn, gpre=gpre, pp=pp, p_l=p_l)
    return h3, res, w


def _layer_bwd(l, dh3, r, rope, w, token, grads_to):
    s_len = dh3.shape[0]
    n = f"l{l}_"
    g = {}
    w = dict(w, g_ple=w["g_ple"] + token)
    z = r["z"]
    zq = (z, QC_W, Z_QC // QC_W)
    zkv = (z, LANE, Z_KVC // LANE)
    zkr = (z, LANE, Z_KR // LANE)
    zlx = (z, LRU_WIDTH, Z_LX // LRU_WIDTH)
    zlg = (z, LRU_WIDTH, Z_LG // LRU_WIDTH)
    (dh2a, dgpre, dpp), _ = _rowwise_bwd(n + "ple_mix_b", _f_ple, [r["h2"], r["gpre"], r["pp"]], [], [dh3], 3,
                                         dts=[f32, bf16, bf16])
    g["w_ple_proj"] = _mm(n + "ple_proj_dw", r["p_l"], dpp, "tn", bf16)
    dhn = _mm(n + "ple_gate_dx", dgpre, w["w_ple_gate"], "nt")
    g["w_ple_gate"] = _mm(n + "ple_gate_dw", r["hn"], dgpre, "tn", bf16)
    (dh2,), (g["g_ple"],) = _rowwise_bwd(n + "norm_ple_b", _f_norm, [r["h2"]], [w["g_ple"]], [dhn], 1, adds={0: dh2a})
    dact = _mm(n + "down_dx", dh2, w["w_down"], "nt")
    g["w_down"] = _mm(n + "down_dw", r["act"], dh2, "tn", bf16)
    dup_g, dup_v, dcw_g, dcw_v, dcb_g, dcb_v = _ffn_act_bwd(n + "ffn_act_b", r["up"], dact, w["ffn_conv_w"], w["ffn_conv_b"])
    g["ffn_conv_w"] = jnp.concatenate([dcw_g, dcw_v], axis=1)
    g["ffn_conv_b"] = jnp.concatenate([dcb_g, dcb_v], axis=1)
    dxn2 = _mm(n + "up_dx_v", dup_v, w["w_up_v"], "nt", res=_mm(n + "up_dx_g", dup_g, w["w_up_g"], "nt"))
    g["w_up"] = jnp.concatenate([_mm(n + "up_dw_g", r["xn2"], dup_g, "tn", bf16),
                                 _mm(n + "up_dw_v", r["xn2"], dup_v, "tn", bf16)], axis=1)
    (dh1,), (g["g_ffn"],) = _rowwise_bwd(n + "norm_ffn_b", _f_norm, [r["h1"]], [w["g_ffn"]], [dxn2], 1, adds={0: dh2})
    token = grads_to("ffn", dict(w_up=g["w_up"], ffn_conv_w=g["ffn_conv_w"], w_down=g["w_down"],
                                 w_ple_gate=g["w_ple_gate"], w_ple_proj=g["w_ple_proj"]))
    w = dict(w, g_out=w["g_out"] + token)
    docat = _mm(n + "out_dx", dh1, w["w_o"], "nt")
    g["w_o"] = _mm(n + "out_dw", r["ocat"], dh1, "tn", bf16)
    (do_mla, do_fox, dhs, dlg), (g["g_out"],) = _rowwise_bwd(
        n + "merge_b", _f_merge, [r["o_mla"], r["o_fox"], r["hs"], zlg], [w["g_out"]], [docat], 4)
    a, hs = r["a"], r["hs"]
    a_next = jnp.concatenate([a[1:], jnp.zeros((1, LRU_WIDTH), f32)], axis=0)
    h_prev = jnp.concatenate([jnp.zeros((1, LRU_WIDTH), f32), hs[:-1]], axis=0)
    da, dbx = _scan_bwd(n + "lru_scan_b", a_next, h_prev, dhs)
    (dgates, dxc_a), (g["b_r"], g["b_i"], g["lam"]) = _rowwise_bwd(
        n + "lru_gate_b", _f_lru_gate, [r["gates"], r["xc"]], [w["b_r"], w["b_i"], w["lam"]], [da, dbx], 2,
        dts=[bf16, f32])
    dxc_b = _mm(n + "lru_gates_dx", dgates, w["w_ri"], "nt")
    g["w_ri"] = _mm(n + "lru_gates_dw", r["xc"], dgates, "tn")
    dlx, g["lru_conv_w"], g["lru_conv_b"] = _conv_bwd(n + "lru_conv_b", zlx, dxc_a, w["lru_conv_w"], LRU_CONV, dout2=dxc_b)
    fox_scale = FOX_HEAD_DIM ** -0.5
    fq, fk, fv = (z, Z_FQ // LANE), (z, Z_FK // LANE), (z, Z_FV // LANE)
    dfq, delta_f, dc_q = _attn_dq(n + "fox_dq", fq, fk, fv, r["o_fox"], do_fox, r["lse_f"], fox_scale, r["c_row"])
    dfk, dfv, dc_k = _attn_dkv(n + "fox_dkv", fq, fk, fv, do_fox, r["lse_f"], delta_f, fox_scale, r["c_row"])
    pad_rows = jnp.zeros((SUBLANE - HEADS, s_len), f32)
    dfl_t, g["b_f8"] = _decay_bwd(n + "decay_b", r["fl_t"], w["b_f8"],
                                  jnp.concatenate([dc_k.reshape(HEADS, s_len), pad_rows], axis=0),
                                  jnp.concatenate([dc_q.reshape(HEADS, s_len), pad_rows], axis=0))
    dfl = jnp.pad(dfl_t.T, ((0, 0), (0, LANE - SUBLANE)))
    mla_scale = (MLA_NOPE + MLA_ROPE) ** -0.5
    qr, kk, kv = (r["qr"], 0), (r["kk"], 0), (r["kv"], HEADS)
    dqr, delta_m, _ = _attn_dq(n + "mla_dq", qr, kk, kv, r["o_mla"], do_mla, r["lse_m"], mla_scale)
    dkk, dv_m = _attn_dkv(n + "mla_dkv", qr, kk, kv, do_mla, r["lse_m"], delta_m, mla_scale)
    (dq, dkpart, dkr), _ = _rowwise_bwd(n + "mla_prep_b", _f_mla_prep, [r["q"], (r["kv"], HEADS * LANE, 0), zkr, *rope],
                                        [], [dqr, dkk], 3, dts=[bf16, bf16, f32])
    dkv = jnp.concatenate([dkpart, dv_m.astype(bf16)], axis=1)
    dkvn = _mm(n + "ukv_dx", dkv, w["w_ukv"], "nt")
    g["w_ukv"] = _mm(n + "ukv_dw", r["kvn"], dkv, "tn", bf16)
    dqcn = _mm(n + "uq_dx", dq, w["w_uq"], "nt")
    g["w_uq"] = _mm(n + "uq_dw", r["qcn"], dq, "tn", bf16)
    (dqc, dkvc), (g["g_qc"], g["g_kvc"]) = _rowwise_bwd(n + "latent_norm_b", _f_latent, [zq, zkv],
                                                        [w["g_qc"], w["g_kvc"]], [dqcn, dkvn], 2)
    dz = jnp.concatenate([t.astype(bf16) for t in (dfq, dfk, dfv, dlx, dlg, dqc, dkvc, dkr, dfl)], axis=1)
    dxn = _mm(n + "in_dx", dz, w["w_in"], "nt")
    g["w_in"] = _mm(n + "in_dw", r["xn"], dz, "tn", bf16)
    (dh0,), (g["g_mix"],) = _rowwise_bwd(n + "norm_mix_b", _f_norm, [r["h0"]], [w["g_mix"]], [dxn], 1, adds={0: dh1})
    return dh0, grads_to("mix", _unpad_mix_grads(g))


def _unpad_mix_grads(g):
    d_ri = g["w_ri"]
    idx = jnp.arange(LRU_BLOCKS)

    def diag_blocks(m):
        return m.reshape(LRU_BLOCKS, LRU_BLOCK, LRU_BLOCKS, LRU_BLOCK)[idx, :, idx, :]

    return dict(
        g_mix=g["g_mix"][0], w_in=_take_inv(g["w_in"], Z_MAP, 1), g_qc=g["g_qc"][0, :MLA_Q_RANK],
        w_uq=_take_inv(g["w_uq"][:MLA_Q_RANK], UQ_COL_MAP, 1), g_kvc=g["g_kvc"][0],
        w_ukv=_take_inv(g["w_ukv"], UKV_MAP, 1), b_f=g["b_f8"][:FOX_HEADS, 0],
        lru_conv_w=g["lru_conv_w"], lru_conv_b=g["lru_conv_b"][0],
        w_r=diag_blocks(d_ri[:, :LRU_WIDTH]), b_r=g["b_r"][0], w_i=diag_blocks(d_ri[:, LRU_WIDTH:]), b_i=g["b_i"][0],
        lru_lambda=g["lam"][0], g_out=_take_inv(g["g_out"][0], OMIX_MAP, 0), w_o=_take_inv(g["w_o"], OMIX_MAP, 0),
        g_ffn=g["g_ffn"][0], ffn_conv_b=g["ffn_conv_b"][0], g_ple=g["g_ple"][0],
    )


LAYER_WEIGHTS = ["g_mix", "w_in", "g_qc", "w_uq", "g_kvc", "w_ukv", "b_f", "lru_conv_w", "lru_conv_b", "w_r", "b_r", "w_i",
                 "b_i", "lru_lambda", "g_out", "w_o", "g_ffn", "w_up", "ffn_conv_w", "ffn_conv_b", "w_down", "g_ple",
                 "w_ple_gate", "w_ple_proj"]
WEIGHTS = LAYER_WEIGHTS + ["g_final"]


def _local_step(x, p, pos, target, g_final, weights_of, grads_to):
    h = x
    rope = _rope_rows(pos)
    ws, saved = [], []
    for l in range(DEPTH):
        h, r, w = _layer_fwd(l, h, p[l], rope, functools.partial(weights_of, l))
        ws.append(w)
        saved.append(r)
    loss, dh, dg_final = _loss_head("loss_head", h, target, g_final.reshape(1, -1))
    token = jnp.zeros((), f32)
    for l in reversed(range(DEPTH)):
        dh, token = _layer_bwd(l, dh, saved[l], rope, ws[l], token, functools.partial(grads_to, l))
    return loss[0, 0], dh, dg_final[0]


MESH_AXES = ("x", "y", "c")


def _exchange(name, src, axes, scatter, pieces=1):
    n = 2 ** len(axes)
    flips = [tuple((f >> (len(axes) - 1 - b)) & 1 for b in range(len(axes))) for f in range(1, n)]
    rows = src.shape[-2]
    piece_rows = rows // pieces
    assert piece_rows * pieces == rows

    def body(src_ref, out_ref, send_sems, recv_sems, local_sem):
        coords = {a: lax.axis_index(a) for a in MESH_AXES}

        def index_of(cd):
            idx = 0
            for a in axes:
                idx = idx * 2 + cd[a]
            return idx

        me = index_of(coords)
        local = pltpu.make_async_copy(src_ref.at[me] if scatter else src_ref, out_ref.at[me], local_sem)
        local.start()
        copies = []
        for k, f in enumerate(flips):
            peer = dict(coords)
            for a, bit in zip(axes, f):
                if bit:
                    peer[a] = 1 - coords[a]
            slab = src_ref.at[index_of(peer)] if scatter else src_ref
            for pc in range(pieces):
                span = pl.ds(pc * piece_rows, piece_rows)
                cp = pltpu.make_async_remote_copy(
                    src_ref=slab.at[span], dst_ref=out_ref.at[me, span],
                    send_sem=send_sems.at[k * pieces + pc], recv_sem=recv_sems.at[k * pieces + pc],
                    device_id=tuple(peer[a] for a in MESH_AXES), device_id_type=pl.DeviceIdType.MESH)
                cp.start()
                copies.append(cp)
        for cp in copies:
            cp.wait()
        local.wait()

    n_sems = (n - 1) * pieces
    return pl.pallas_call(
        body, name=name, out_shape=jax.ShapeDtypeStruct((n, rows, LANE), src.dtype),
        in_specs=[pl.BlockSpec(memory_space=pl.ANY)], out_specs=pl.BlockSpec(memory_space=pl.ANY),
        scratch_shapes=[pltpu.SemaphoreType.DMA((n_sems,)), pltpu.SemaphoreType.DMA((n_sems,)), pltpu.SemaphoreType.DMA])(src)


def _row_tile(rows, cap):
    if rows <= cap:
        return rows
    for t in range(cap, SUBLANE - 1, -SUBLANE):
        if rows % t == 0:
            return t
    return rows


def _sum_slabs(name, a):
    n, rows, _ = a.shape
    tr = _row_tile(rows, 512)

    def kern(a_ref, o_ref):
        acc = a_ref[0].astype(f32)
        for k in range(1, n):
            acc = acc + a_ref[k].astype(f32)
        o_ref[...] = acc

    return pl.pallas_call(
        kern, name=name, grid=(rows // tr,), in_specs=[pl.BlockSpec((n, tr, LANE), lambda i: (0, i, 0))],
        out_specs=pl.BlockSpec((tr, LANE), lambda i: (i, 0)), out_shape=jax.ShapeDtypeStruct((rows, LANE), f32),
        compiler_params=pltpu.CompilerParams(dimension_semantics=("parallel",)))(a)


ADAM_BLOCK_BYTES = 2 ** 20


def _adamw(name, w, g, m, v):
    rows, cols = w.shape
    tr = _row_tile(rows, max(SUBLANE, ADAM_BLOCK_BYTES // (4 * cols) // SUBLANE * SUBLANE))

    def kern(w_ref, g_ref, m_ref, v_ref, d_ref, nm_ref, nv_ref):
        gv = g_ref[...]
        nm = ADAM_B1 * m_ref[...] + (1.0 - ADAM_B1) * gv
        nv = ADAM_B2 * v_ref[...] + (1.0 - ADAM_B2) * (gv * gv)
        m_hat = nm / (1.0 - ADAM_B1 ** ADAM_STEP)
        v_hat = nv / (1.0 - ADAM_B2 ** ADAM_STEP)
        d_ref[...] = -ADAM_LR * (m_hat / (jnp.sqrt(v_hat) + ADAM_EPS) + ADAM_WD * w_ref[...])
        nm_ref[...] = nm
        nv_ref[...] = nv

    spec = pl.BlockSpec((tr, cols), lambda i: (i, 0))
    return pl.pallas_call(
        kern, name=name, grid=(rows // tr,), in_specs=[spec] * 4, out_specs=[spec] * 3,
        out_shape=[jax.ShapeDtypeStruct((rows, cols), f32)] * 3,
        compiler_params=pltpu.CompilerParams(dimension_semantics=("parallel",)))(w, g, m, v)


def _packed_rows(shape):
    return -(-int(np.prod(shape)) // (SUBLANE * LANE)) * SUBLANE


def _pack(arrays):
    rows = []
    for a in arrays:
        flat = a.reshape(-1)
        rows.append(jnp.pad(flat, (0, _packed_rows(a.shape) * LANE - flat.shape[0])).reshape(-1, LANE))
    return jnp.concatenate(rows, axis=0)


def _unpack(buf, shapes):
    out, at = [], 0
    for s in shapes:
        rows = _packed_rows(s)
        out.append(buf[at:at + rows].reshape(-1)[:int(np.prod(s))].reshape(s))
        at += rows
    return out


SHARD_AXIS = {"w_in": 2, "w_uq": 2, "w_ukv": 2, "lru_conv_w": 2, "w_o": 1, "w_up": 2, "ffn_conv_w": 2, "w_down": 1,
              "w_ple_gate": 1, "w_ple_proj": 2}
SHARDED = [k for k in WEIGHTS if k in SHARD_AXIS]
REPLICATED = [k for k in WEIGHTS if k not in SHARD_AXIS]
ELEMENTWISE_F32 = ("lru_conv_w", "ffn_conv_w")
N_SHARDS = 4
BF16_TILE_ROWS = 16


HBM_SPEC = pl.BlockSpec(memory_space=pl.ANY)
SEM_SPEC = pl.BlockSpec(memory_space=pltpu.SEMAPHORE)
SPLIT_EFFECT = pltpu.SideEffectType.DATAFLOW_SIDE_EFFECTING
CHIP_FLIPS = ((1, 0), (0, 1), (1, 1))
N_DEVICES = 8
SUM_BLOCK_BYTES = 4 * 2 ** 20


def _device_index():
    return 4 * lax.axis_index("x") + 2 * lax.axis_index("y") + lax.axis_index("c")


def _when(cond, fn):
    if cond is None:
        fn()
    else:
        pl.when(cond)(fn)


class _Exchange:
    def __init__(self, name, plan, srcs, land_shapes, n_send, n_recv):
        self.name, self.plan, self.srcs, self.n = name, plan, list(srcs), len(srcs)
        self.land_shapes, self.n_send, self.n_recv = land_shapes, n_send, n_recv

    def run(self):
        n = self.n

        def body(*refs):
            sends, arrivals = self.plan(refs[:n], refs[n:2 * n], refs[2 * n], refs[2 * n + 1])
            for cond, cp in sends:
                _when(cond, cp.start)
            for cond, cp in arrivals:
                _when(cond, cp.wait_recv)
            for cond, cp in sends:
                _when(cond, cp.wait_send)

        return pl.pallas_call(
            body, name=self.name, out_shape=self.land_shapes, in_specs=[HBM_SPEC] * n, out_specs=[HBM_SPEC] * n,
            scratch_shapes=[pltpu.SemaphoreType.DMA((self.n_send,)), pltpu.SemaphoreType.DMA((self.n_recv,))])(*self.srcs)

    def start(self, after=None):
        n = self.n
        lands = [lax.empty(s.shape, s.dtype) for s in self.land_shapes]
        extra = [] if after is None else [after]

        def body(*refs):
            ins, lands_in = refs[:n], refs[n:2 * n]
            send_sems, recv_sems, token = refs[2 * n + len(extra)], refs[2 * n + len(extra) + 1], refs[-1]
            sends, _ = self.plan(ins, lands_in, send_sems, recv_sems)
            for cond, cp in sends:
                _when(cond, cp.start)
            token[...] = jnp.zeros_like(token)

        hbm = [pltpu.with_memory_space_constraint(a, pltpu.HBM) for a in self.srcs + lands]
        res = pl.pallas_call(
            body, name=self.name + "_start",
            out_shape=(pltpu.SemaphoreType.DMA((self.n_send,)), pltpu.SemaphoreType.DMA((self.n_recv,)),
                       *[pltpu.HBM(a.shape, a.dtype) for a in hbm], jax.ShapeDtypeStruct((SUBLANE, LANE), f32)),
            in_specs=[HBM_SPEC] * (2 * n + len(extra)),
            out_specs=(SEM_SPEC, SEM_SPEC, *[HBM_SPEC] * (2 * n), pl.BlockSpec(memory_space=pltpu.VMEM)),
            input_output_aliases={i: 2 + i for i in range(2 * n)},
            compiler_params=pltpu.CompilerParams(has_side_effects=SPLIT_EFFECT))(*hbm, *extra)
        self.sems, self.thru, token = res[:2], res[2:2 + 2 * n], res[-1]
        return token[0, 0]

    def finish(self, after):
        n = self.n

        def body(*refs):
            ins, lands_in, send_sems, recv_sems = refs[:n], refs[n:2 * n], refs[2 * n], refs[2 * n + 1]
            sends, arrivals = self.plan(ins, lands_in, send_sems, recv_sems)
            for cond, cp in arrivals:
                _when(cond, cp.wait_recv)
            for cond, cp in sends:
                _when(cond, cp.wait_send)

        res = pl.pallas_call(
            body, name=self.name + "_finish", out_shape=tuple(pltpu.HBM(a.shape, a.dtype) for a in self.thru),
            in_specs=[HBM_SPEC] * (2 * n) + [SEM_SPEC, SEM_SPEC, HBM_SPEC], out_specs=tuple([HBM_SPEC] * (2 * n)),
            input_output_aliases={i: i for i in range(2 * n)},
            compiler_params=pltpu.CompilerParams(has_side_effects=SPLIT_EFFECT))(*self.thru, *self.sems, after)
        return list(res[n:])


def _gather_exchange(name, shards):
    def plan(ins, lands, send_sems, recv_sems):
        x, y, c = (lax.axis_index(a) for a in MESH_AXES)
        copies = []
        for i in range(len(ins)):
            for k, (fx, fy) in enumerate(CHIP_FLIPS):
                peer = (1 - x if fx else x, 1 - y if fy else y, c)
                copies.append((None, pltpu.make_async_remote_copy(
                    src_ref=ins[i], dst_ref=lands[i].at[2 * x + y], send_sem=send_sems.at[3 * i + k],
                    recv_sem=recv_sems.at[3 * i + k], device_id=peer, device_id_type=pl.DeviceIdType.MESH)))
        return copies, copies

    n = len(shards)
    return _Exchange(name, plan, shards, [jax.ShapeDtypeStruct((N_SHARDS,) + s.shape, s.dtype) for s in shards], 3 * n, 3 * n)


def _scatter_exchange(name, layer, chunks):
    def plan(ins, lands, send_sems, recv_sems):
        x, y, c = (lax.axis_index(a) for a in MESH_AXES)
        me = _device_index()
        sends, arrivals = [], []
        for i in range(len(ins)):
            for j in range(N_SHARDS):
                target = (j // 2, j % 2, layer)
                remote = jnp.logical_not((x == target[0]) & (y == target[1]) & (c == layer))
                sends.append((remote, pltpu.make_async_remote_copy(
                    src_ref=ins[i].at[j], dst_ref=lands[i].at[me], send_sem=send_sems.at[N_SHARDS * i + j],
                    recv_sem=recv_sems.at[N_DEVICES * i + me], device_id=target, device_id_type=pl.DeviceIdType.MESH)))
            for s in range(N_DEVICES):
                arrivals.append(((c == layer) & (me != s), pltpu.make_async_remote_copy(
                    src_ref=ins[i].at[0], dst_ref=lands[i].at[s], send_sem=send_sems.at[0],
                    recv_sem=recv_sems.at[N_DEVICES * i + s], device_id=(x, y, c), device_id_type=pl.DeviceIdType.MESH)))
        return sends, arrivals

    n = len(chunks)
    lands = [jax.ShapeDtypeStruct((N_DEVICES,) + ch.shape[1:], ch.dtype) for ch in chunks]
    return _Exchange(name, plan, chunks, lands, N_SHARDS * n, N_DEVICES * n)


def _sum_contributions(name, got, mine):
    _, a, b = got.shape
    ta = _row_tile(a, max(SUBLANE, SUM_BLOCK_BYTES // (N_DEVICES * b * got.dtype.itemsize) // SUBLANE * SUBLANE))

    def kern(got_ref, mine_ref, o_ref):
        me = _device_index()
        acc = jnp.zeros(o_ref.shape, f32)
        for s in range(N_DEVICES):
            acc = acc + jnp.where(me == s, mine_ref[...].astype(f32), got_ref[s].astype(f32))
        o_ref[...] = acc

    return pl.pallas_call(
        kern, name=name, grid=(a // ta,),
        in_specs=[pl.BlockSpec((N_DEVICES, ta, b), lambda i: (0, i, 0)), pl.BlockSpec((ta, b), lambda i: (i, 0))],
        out_specs=pl.BlockSpec((ta, b), lambda i: (i, 0)), out_shape=jax.ShapeDtypeStruct((a, b), f32),
        compiler_params=pltpu.CompilerParams(dimension_semantics=("parallel",)))(got, mine)


def _swap_layers(name, sums):
    n = len(sums[0])

    def body(*refs):
        srcs = (refs[:n], refs[n:2 * n])
        outs, (send_sems, recv_sems) = refs[2 * n:3 * n], refs[3 * n:]
        x, y, c = (lax.axis_index(a) for a in MESH_AXES)
        for i in range(n):
            for layer in range(DEPTH):
                cp = pltpu.make_async_remote_copy(
                    src_ref=srcs[layer][i], dst_ref=outs[i], send_sem=send_sems.at[i], recv_sem=recv_sems.at[i],
                    device_id=(x, y, 1 - c), device_id_type=pl.DeviceIdType.MESH)
                pl.when(c == layer)(cp.start)
        for i in range(n):
            pltpu.make_async_remote_copy(
                src_ref=srcs[0][i], dst_ref=outs[i], send_sem=send_sems.at[i], recv_sem=recv_sems.at[i],
                device_id=(x, y, 1 - c), device_id_type=pl.DeviceIdType.MESH).wait()

    return pl.pallas_call(
        body, name=name, out_shape=[jax.ShapeDtypeStruct(s.shape, s.dtype) for s in sums[0]],
        in_specs=[HBM_SPEC] * (2 * n), out_specs=[HBM_SPEC] * n,
        scratch_shapes=[pltpu.SemaphoreType.DMA((n,)), pltpu.SemaphoreType.DMA((n,))])(*sums[0], *sums[1])


def _stack_shards(g, axis):
    if axis == 1:
        return g.reshape(N_SHARDS, g.shape[0] // N_SHARDS, g.shape[1])
    return g.reshape(g.shape[0], N_SHARDS, g.shape[1] // N_SHARDS).transpose(1, 0, 2)


def _join_shards(s, axis):
    if axis == 1:
        return s.reshape(-1, s.shape[2])
    return s.transpose(1, 0, 2).reshape(s.shape[1], -1)


def _layer_shards(w, l, names):
    return [w[k][l] if k in ELEMENTWISE_F32 else w[k][l].astype(bf16) for k in names]


def _full_weights(names, sent, got):
    j = 2 * lax.axis_index("x") + lax.axis_index("y")
    return {k: _join_shards(lax.dynamic_update_slice(g, own[None], (j, 0, 0)), SHARD_AXIS[k])
            for k, own, g in zip(names, sent, got)}


def _grad_chunks(grads, names):
    return [_stack_shards(grads[k], SHARD_AXIS[k]).astype(bf16) for k in names]


def _sum_group(l, names, got, chunks):
    j = 2 * lax.axis_index("x") + lax.axis_index("y")
    return {k: _sum_contributions(f"sum_l{l}_{k}", g, lax.dynamic_index_in_dim(ch, j, 0, keepdims=False))
            for k, g, ch in zip(names, got, chunks)}


def _both_layers(sums):
    c = lax.axis_index("c")
    other = _swap_layers("swap_layers", sums)
    return {k: jnp.stack([jnp.where(c == 0, sums[0][i], other[i]), jnp.where(c == 0, other[i], sums[1][i])])
            for i, k in enumerate(SHARDED)}


def kernel(x, p, positions, g_mix, w_in, g_qc, w_uq, g_kvc, w_ukv, b_f, lru_conv_w, lru_conv_b, w_r, b_r, w_i, b_i, lru_lambda, g_out, w_o, g_ffn, w_up, ffn_conv_w, ffn_conv_b, w_down, g_ple, w_ple_gate, w_ple_proj, g_final, loss_target, m_g_mix, m_w_in, m_g_qc, m_w_uq, m_g_kvc, m_w_ukv, m_b_f, m_lru_conv_w, m_lru_conv_b, m_w_r, m_b_r, m_w_i, m_b_i, m_lru_lambda, m_g_out, m_w_o, m_g_ffn, m_w_up, m_ffn_conv_w, m_ffn_conv_b, m_w_down, m_g_ple, m_w_ple_gate, m_w_ple_proj, m_g_final, v_g_mix, v_w_in, v_g_qc, v_w_uq, v_g_kvc, v_w_ukv, v_b_f, v_lru_conv_w, v_lru_conv_b, v_w_r, v_b_r, v_w_i, v_b_i, v_lru_lambda, v_g_out, v_w_o, v_g_ffn, v_w_up, v_ffn_conv_w, v_ffn_conv_b, v_w_down, v_g_ple, v_w_ple_gate, v_w_ple_proj, v_g_final):
    given = locals()
    w = {k: given[k] for k in WEIGHTS}
    m = {k: given["m_" + k] for k in WEIGHTS}
    v = {k: given["v_" + k] for k in WEIGHTS}

    parts = {"mix": MIX_PART, "ffn": FFN_PART}
    groups = [(l, part) for l in range(DEPTH) for part in ("mix", "ffn")]
    sent = {g: _layer_shards(w, g[0], parts[g[1]]) for g in groups}
    first = _gather_exchange("gather_l0_mix", sent[groups[0]]).run()
    ahead = {g: _gather_exchange(f"gather_l{g[0]}_{g[1]}", sent[g]) for g in groups[1:]}
    pos = positions[0].astype(f32).reshape(-1, 1)
    for ex in ahead.values():
        pos = pos + ex.start(after=first[0])
    behind, layer_grads, chunks = {}, [{} for _ in range(DEPTH)], {}

    def weights_of(l, part, after):
        g = (l, part)
        full = _full_weights(parts[part], sent[g], first if g == groups[0] else ahead[g].finish(after=after))
        if part == "mix":
            full.update({k: w[k][l] for k in LAYER_WEIGHTS if k in REPLICATED})
        return full

    def grads_to(l, part, grads):
        g = (l, part)
        layer_grads[l].update(grads)
        chunks[g] = _grad_chunks(grads, parts[part])
        if g == groups[0]:
            return jnp.zeros((), f32)
        behind[g] = _scatter_exchange(f"scatter_l{l}_{part}", l, chunks[g])
        return behind[g].start()

    loss, dx, dg_final = _local_step(x[0], p[:, 0], pos, loss_target[0], w["g_final"], weights_of, grads_to)

    sums = [{} for _ in range(DEPTH)]
    for g in groups:
        got = _scatter_exchange("scatter_l0_mix", 0, chunks[g]).run() if g == groups[0] else behind[g].finish(after=dx)
        sums[g[0]].update(_sum_group(g[0], parts[g[1]], got, chunks[g]))
    g_sharded = _both_layers([[sums[l][k] for k in SHARDED] for l in range(DEPTH)])
    big = [[], [], [], []]
    for k in SHARDED:
        shape = w[k].shape
        flat = [t.reshape(-1, shape[-1]) for t in (w[k], g_sharded[k], m[k], v[k])]
        for kind, res in enumerate((flat[1],) + tuple(_adamw("adamw_" + k, *flat))):
            big[kind].append(res.reshape(shape))

    grads = {k: jnp.stack([layer_grads[l][k] for l in range(DEPTH)]) for k in LAYER_WEIGHTS if k in REPLICATED}
    grads["g_final"] = dg_final
    rep_shapes = [w[k].shape for k in REPLICATED] + [(1,)]
    contrib = _pack([grads[k] for k in REPLICATED] + [loss.reshape(1)])
    g_rep = _sum_slabs("sum_replicated", _exchange("gather_replicated", contrib, MESH_AXES, scatter=False))
    zero = jnp.zeros((1,), f32)
    w_rep, m_rep, v_rep = (_pack([t[k] for k in REPLICATED] + [zero]) for t in (w, m, v))
    rep = [_unpack(b, rep_shapes) for b in (g_rep,) + tuple(_adamw("adamw_replicated", w_rep, g_rep, m_rep, v_rep))]

    outs = []
    for kind in range(4):
        by_name = dict(zip(SHARDED, big[kind]))
        by_name.update(zip(REPLICATED, rep[kind][:-1]))
        outs.append([by_name[k] for k in WEIGHTS])
    total_loss = rep[0][-1][0]
    return (total_loss, dx.reshape(x.shape), *outs[0], *outs[1], *outs[2], *outs[3])
```

```python
import functools
import math

import numpy as np
import jax
import jax.numpy as jnp
from jax import lax
from jax.experimental import pallas as pl
from jax.experimental.pallas import tpu as pltpu

f32, bf16 = jnp.float32, jnp.bfloat16

D_MODEL = 1024
PLE_DIM = 256
MLA_HEADS, MLA_NOPE, MLA_ROPE, MLA_V = 4, 64, 32, 64
MLA_Q_RANK, MLA_KV_RANK = 192, 128
FOX_HEADS, FOX_HEAD_DIM = 4, 64
LRU_WIDTH, LRU_BLOCKS, LRU_BLOCK, LRU_CONV, LRU_C = 512, 8, 64, 4, 8.0
D_FF, FFN_CONV = 2816, 3
ROPE_THETA = 10000.0
EPS = 1e-6
DEPTH = 2
ADAM_LR, ADAM_B1, ADAM_B2, ADAM_EPS, ADAM_WD, ADAM_STEP = 0.001, 0.9, 0.999, 1e-08, 0.01, 10

LANE = 128
SUBLANE = 8
HEADS = 4

Z_FQ, Z_FK, Z_FV, Z_LX, Z_LG, Z_QC, Z_KVC, Z_KR, Z_FL, Z_W = 0, 512, 1024, 1536, 2048, 2560, 2816, 2944, 3072, 3200
QC_W = 256
ROPE_AT = 64


def _head_pad_map(n_heads, width):
    m = -np.ones(n_heads * LANE, np.int64)
    for h in range(n_heads):
        m[h * LANE:h * LANE + width] = h * width + np.arange(width)
    return m


def _z_map():
    m = -np.ones(Z_W, np.int64)
    o_qc, o_kvc, o_kr = 0, MLA_Q_RANK, MLA_Q_RANK + MLA_KV_RANK
    o_fq = o_kr + MLA_ROPE
    o_fk, o_fv = o_fq + 256, o_fq + 512
    o_fl = o_fv + 256
    o_lx = o_fl + FOX_HEADS
    o_lg = o_lx + LRU_WIDTH
    m[Z_FQ:Z_FQ + 512] = np.where(_head_pad_map(4, 64) >= 0, _head_pad_map(4, 64) + o_fq, -1)
    m[Z_FK:Z_FK + 512] = np.where(_head_pad_map(4, 64) >= 0, _head_pad_map(4, 64) + o_fk, -1)
    m[Z_FV:Z_FV + 512] = np.where(_head_pad_map(4, 64) >= 0, _head_pad_map(4, 64) + o_fv, -1)
    m[Z_LX:Z_LX + 512] = o_lx + np.arange(512)
    m[Z_LG:Z_LG + 512] = o_lg + np.arange(512)
    m[Z_QC:Z_QC + MLA_Q_RANK] = o_qc + np.arange(MLA_Q_RANK)
    m[Z_KVC:Z_KVC + MLA_KV_RANK] = o_kvc + np.arange(MLA_KV_RANK)
    m[Z_KR + ROPE_AT:Z_KR + ROPE_AT + MLA_ROPE] = o_kr + np.arange(MLA_ROPE)
    m[Z_FL:Z_FL + FOX_HEADS] = o_fl + np.arange(FOX_HEADS)
    return m


def _ukv_map():
    m = -np.ones(2 * HEADS * LANE, np.int64)
    for h in range(HEADS):
        m[h * LANE:h * LANE + MLA_NOPE] = h * (MLA_NOPE + MLA_V) + np.arange(MLA_NOPE)
        m[HEADS * LANE + h * LANE:HEADS * LANE + h * LANE + MLA_V] = h * (MLA_NOPE + MLA_V) + MLA_NOPE + np.arange(MLA_V)
    return m


def _omix_map():
    return np.concatenate([_head_pad_map(4, 64), np.where(_head_pad_map(4, 64) >= 0, _head_pad_map(4, 64) + 256, -1),
                           512 + np.arange(512)])


def _pad_to(m, n):
    return np.concatenate([m, -np.ones(n - m.shape[0], np.int64)])


def _take_pad(a, m, axis):
    out = jnp.take(a, jnp.asarray(np.maximum(m, 0), jnp.int32), axis=axis)
    shape = [1] * a.ndim
    shape[axis] = m.shape[0]
    return out * jnp.asarray((m >= 0).reshape(shape), a.dtype)


def _take_inv(a, m, axis):
    n = int(m.max()) + 1
    inv = np.zeros(n, np.int64)
    inv[m[m >= 0]] = np.nonzero(m >= 0)[0]
    return jnp.take(a, jnp.asarray(inv, jnp.int32), axis=axis)


Z_MAP = _z_map()
UQ_COL_MAP = _head_pad_map(HEADS, MLA_NOPE + MLA_ROPE)
UQ_ROW_MAP = _pad_to(np.arange(MLA_Q_RANK), QC_W)
UKV_MAP = _ukv_map()
OMIX_MAP = _omix_map()
OMIX_W = 1536


def _rope_tables(width, at):
    half = MLA_ROPE // 2
    inv = ROPE_THETA ** (-np.arange(half, dtype=np.float32) / half)
    freq = np.zeros((1, width), np.float32)
    m1 = np.zeros((1, width), np.float32)
    m2 = np.zeros((1, width), np.float32)
    for h in range(width // LANE):
        b = h * LANE + at
        freq[0, b:b + half] = inv
        freq[0, b + half:b + 2 * half] = inv
        m1[0, b:b + half] = 1.0
        m2[0, b + half:b + 2 * half] = 1.0
    return freq, m1, m2


def _view(r):
    return r if isinstance(r, tuple) else (r, r.shape[1], 0)


def _blk(dim, cap):
    if dim <= cap:
        return dim
    for b in range(cap, LANE - 1, -LANE):
        if dim % b == 0:
            return b
    return dim


@functools.partial(jax.custom_vjp, nondiff_argnums=(1, 2))
def _roll(x, shift, axis):
    return pltpu.roll(x, shift, axis)


def _roll_fwd(x, shift, axis):
    return pltpu.roll(x, shift, axis), None


def _roll_bwd(shift, axis, _, g):
    return (pltpu.roll(g, g.shape[axis] - shift, axis),)


_roll.defvjp(_roll_fwd, _roll_bwd)


def _rowwise(name, fn, rows, pars, outs, tb=256, transposed=()):
    rows = [_view(r) for r in rows]
    n = rows[0][0].shape[0]
    tb = min(tb, n)
    nr, npar, nout = len(rows), len(pars), len(outs)

    def kern(*refs):
        r = [refs[k][...].astype(f32) for k in range(nr)]
        p = [refs[nr + k][...] for k in range(npar)]
        res = fn(*r, *p)
        o_refs = refs[nr + npar:]
        for o_ref, o in zip(o_refs, res):
            o_ref[...] = o.astype(o_ref.dtype)
        for t_ref, k in zip(o_refs[nout:], transposed):
            t_ref[...] = res[k].astype(t_ref.dtype).T

    in_specs = [pl.BlockSpec((tb, w), lambda i, j=idx: (i, j)) for (_, w, idx) in rows]
    in_specs += [pl.BlockSpec(p.shape, lambda i: (0, 0)) for p in pars]
    out_specs = [pl.BlockSpec((tb, w), lambda i: (i, 0)) for (w, _) in outs]
    out_shape = [jax.ShapeDtypeStruct((n, w), dt) for (w, dt) in outs]
    out_specs += [pl.BlockSpec((outs[k][0], tb), lambda i: (0, i)) for k in transposed]
    out_shape += [jax.ShapeDtypeStruct((outs[k][0], n), outs[k][1]) for k in transposed]
    return pl.pallas_call(kern, name=name, grid=(n // tb,), in_specs=in_specs, out_specs=out_specs, out_shape=out_shape,
                          compiler_params=pltpu.CompilerParams(dimension_semantics=("parallel",)))(*[r[0] for r in rows], *pars)


def _rowwise_bwd(name, fn, rows, pars, cts, ndiff, adds=None, tb=256, dts=None):
    rows = [_view(r) for r in rows]
    dts = dts or [f32] * ndiff
    adds = adds or {}
    add_keys = sorted(adds)
    n = rows[0][0].shape[0]
    tb = min(tb, n)
    nr, npar, nct, nadd = len(rows), len(pars), len(cts), len(add_keys)

    def kern(*refs):
        i = pl.program_id(0)
        r = [refs[k][...].astype(f32) for k in range(nr)]
        p = [refs[nr + k][...] for k in range(npar)]
        ct = [refs[nr + npar + k][...].astype(f32) for k in range(nct)]
        ad = {key: refs[nr + npar + nct + k][...] for k, key in enumerate(add_keys)}
        o_refs = refs[nr + npar + nct + nadd:]

        def g(*d):
            return tuple(fn(*d[:ndiff], *r[ndiff:], *d[ndiff:]))

        _, vjp = jax.vjp(g, *r[:ndiff], *p)
        grads = vjp(tuple(ct))
        for k in range(ndiff):
            gk = grads[k]
            if k in ad:
                gk = gk + ad[k]
            o_refs[k][...] = gk.astype(o_refs[k].dtype)

        @pl.when(i == 0)
        def _():
            for k in range(npar):
                o_refs[ndiff + k][...] = jnp.zeros_like(o_refs[ndiff + k])

        for k in range(npar):
            o_refs[ndiff + k][...] += grads[ndiff + k]

    in_specs = [pl.BlockSpec((tb, w), lambda i, j=idx: (i, j)) for (_, w, idx) in rows]
    in_specs += [pl.BlockSpec(p.shape, lambda i: (0, 0)) for p in pars]
    in_specs += [pl.BlockSpec((tb, c.shape[1]), lambda i: (i, 0)) for c in cts]
    in_specs += [pl.BlockSpec((tb, adds[k].shape[1]), lambda i: (i, 0)) for k in add_keys]
    out_specs = [pl.BlockSpec((tb, rows[k][1]), lambda i: (i, 0)) for k in range(ndiff)]
    out_specs += [pl.BlockSpec(p.shape, lambda i: (0, 0)) for p in pars]
    out_shape = [jax.ShapeDtypeStruct((n, rows[k][1]), dts[k]) for k in range(ndiff)]
    out_shape += [jax.ShapeDtypeStruct(p.shape, f32) for p in pars]
    res = pl.pallas_call(kern, name=name, grid=(n // tb,), in_specs=in_specs, out_specs=out_specs, out_shape=out_shape,
                         compiler_params=pltpu.CompilerParams(dimension_semantics=("arbitrary",)))(
        *[r[0] for r in rows], *pars, *cts, *[adds[k] for k in add_keys])
    return res[:ndiff], res[ndiff:]


_DOT_DIMS = {"nn": ((1,), (0,)), "nt": ((1,), (1,)), "tn": ((0,), (0,))}

MM_VMEM_BUDGET = 36 * 2 ** 20
MM_MAX_TM = 1024
MM_STEP, MM_RESULT, MM_XPOSE, MM_CAST = 700.0, 7.5e-4, 9e-4, 1e-3


def _tile_candidates(dim):
    c = [d for d in range(LANE, dim + 1, LANE) if dim % d == 0]
    return c or [dim]


@functools.lru_cache(maxsize=None)
def _mm_tiles(mode, m, n, k, a_bytes, b_bytes, o_bytes):
    best, best_cost = None, None
    for tm in _tile_candidates(m):
        if tm > MM_MAX_TM:
            continue
        for tn in _tile_candidates(n):
            for tk in _tile_candidates(k):
                vmem = 2 * (tm * tk * a_bytes + tk * tn * b_bytes + tm * tn * o_bytes) + 4 * tm * tn * (2 if tk < k else 1)
                vmem += (2 * tm * tk if a_bytes > 2 else 0) + (2 * tk * tn if b_bytes > 2 else 0)
                if vmem > MM_VMEM_BUDGET:
                    continue
                steps = (m // tm) * (n // tn) * (k // tk)
                cost = steps * MM_STEP + m * n * (k // tk) * MM_RESULT
                if mode == "tn":
                    cost += m * k * (n // tn) * MM_XPOSE
                cost += (m * k * (n // tn) * MM_CAST if a_bytes > 2 else 0) + (k * n * (m // tm) * MM_CAST if b_bytes > 2 else 0)
                if best is None or cost < best_cost:
                    best, best_cost = (tm, tn, tk), cost
    return best


def _mm(name, a, b, mode="nn", out_dtype=f32, res=None):
    if mode == "nn":
        (m, k), (_, n) = a.shape, b.shape
    elif mode == "nt":
        (m, k), (n, _) = a.shape, b.shape
    else:
        (k, m), (_, n) = a.shape, b.shape
    has_res = res is not None
    tm, tn, tk = _mm_tiles(mode, m, n, k, a.dtype.itemsize, b.dtype.itemsize,
                           jnp.dtype(out_dtype).itemsize + (res.dtype.itemsize if has_res else 0))
    nk = k // tk
    dims = (_DOT_DIMS[mode], ((), ()))

    def kern(*refs):
        a_ref, b_ref = refs[0], refs[1]
        o_ref, acc_ref = refs[-2], refs[-1]
        kk = pl.program_id(2)
        part = lax.dot_general(a_ref[...].astype(bf16), b_ref[...].astype(bf16), dims, preferred_element_type=f32)

        def finish(out):
            if has_res:
                out = out + refs[2][...]
            o_ref[...] = out.astype(o_ref.dtype)

        if nk == 1:
            finish(part)
            return

        @pl.when(kk == 0)
        def _():
            acc_ref[...] = part

        @pl.when(jnp.logical_and(kk > 0, kk < nk - 1))
        def _():
            acc_ref[...] += part

        @pl.when(kk == nk - 1)
        def _():
            finish(acc_ref[...] + part)

    if mode == "tn":
        a_spec = pl.BlockSpec((tk, tm), lambda i, j, kk: (kk, i))
    else:
        a_spec = pl.BlockSpec((tm, tk), lambda i, j, kk: (i, kk))
    if mode == "nt":
        b_spec = pl.BlockSpec((tn, tk), lambda i, j, kk: (j, kk))
    else:
        b_spec = pl.BlockSpec((tk, tn), lambda i, j, kk: (kk, j))
    in_specs = [a_spec, b_spec]
    args = [a, b]
    if has_res:
        in_specs.append(pl.BlockSpec((tm, tn), lambda i, j, kk: (i, j)))
        args.append(res)
    return pl.pallas_call(
        kern, name=name, grid=(m // tm, n // tn, nk), in_specs=in_specs,
        out_specs=pl.BlockSpec((tm, tn), lambda i, j, kk: (i, j)),
        out_shape=jax.ShapeDtypeStruct((m, n), out_dtype),
        scratch_shapes=[pltpu.VMEM((tm, tn) if nk > 1 else (SUBLANE, LANE), f32)],
        compiler_params=pltpu.CompilerParams(dimension_semantics=("parallel", "parallel", "arbitrary")))(*args)


ATT_TQ, ATT_TK = 512, 512


def _att_tiles(s_len):
    tk = min(ATT_TK, s_len)
    return min(ATT_TQ, tk), tk


def _fold_scale(scale):
    return (scale, 1.0) if math.frexp(scale)[0] == 0.5 else (1.0, scale)


def _head_tiles_t(x, off, tk):
    s_len = x.shape[0]
    xh = x[:, off * LANE:(off + HEADS) * LANE].astype(bf16)
    return xh.reshape(s_len // tk, tk, HEADS, LANE).transpose(2, 0, 3, 1)


def _scores_t(kb, q_t, s_mul, ck, diag_offset, tq, tk):
    s = jnp.dot(kb, q_t, preferred_element_type=f32)
    if s_mul != 1.0:
        s = s * s_mul
    if ck is not None:
        s = s - ck
    if diag_offset is None:
        return s
    key = lax.broadcasted_iota(jnp.int32, (tk, tq), 0)
    query = lax.broadcasted_iota(jnp.int32, (tk, tq), 1) + diag_offset
    return jnp.where(key <= query, s, -jnp.inf)


def _attn_fwd(name, q, k, v, scale, c_col=None):
    (qa, qo), (ka, ko), (va, vo) = q, k, v
    s_len = qa.shape[0]
    tq, tk = _att_tiles(s_len)
    nq, per = s_len // tq, tk // tq
    decay = c_col is not None
    q_mul, s_mul = _fold_scale(scale)

    def kern(*refs):
        q_ref, k_ref, vt_ref = refs[:3]
        o_ref, lse_ref = refs[-2:]
        i = pl.program_id(1)
        q_ts = [(q_ref[r * tq:(r + 1) * tq, :] * q_mul).astype(bf16).T for r in range(per)]

        def step(j, carry, diagonal):
            rows = pl.ds(pl.multiple_of(j * tk, tk), tk)
            kb = k_ref[rows, :].astype(bf16)
            ck = refs[3][rows, :] if decay else None
            vt = vt_ref[j]
            new = []
            for r in range(per):
                m, l, acc = carry[r]
                n = (r + 1) * tq if diagonal else tk
                s = _scores_t(kb[:n], q_ts[r], s_mul, None if ck is None else ck[:n], r * tq if diagonal else None, tq, n)
                m_new = jnp.maximum(m, jnp.max(s, axis=0, keepdims=True))
                alpha = jnp.exp(m - m_new)
                p = jnp.exp(s - m_new)
                l = alpha * l + jnp.sum(p, axis=0, keepdims=True)
                acc = alpha * acc + jnp.dot(vt[:, :n], p.astype(bf16), preferred_element_type=f32)
                new.append((m_new, l, acc))
            return tuple(new)

        init = tuple((jnp.full((1, tq), -jnp.inf, f32), jnp.zeros((1, tq), f32), jnp.zeros((LANE, tq), f32))
                     for _ in range(per))
        carry = step(i, lax.fori_loop(0, i, lambda j, c: step(j, c, False), init), True)
        for r, (m, l, acc) in enumerate(carry):
            o_ref[r * tq:(r + 1) * tq, :] = (acc / l).T
            lse_ref[r] = m + jnp.log(l)

    in_specs = [pl.BlockSpec((tk, LANE), lambda h, i: (i, qo + h)),
                pl.BlockSpec((s_len, LANE), lambda h, i: (0, ko + h)),
                pl.BlockSpec((None, s_len // tk, LANE, tk), lambda h, i: (h, 0, 0, 0))]
    args = [qa, ka, _head_tiles_t(va, vo, tk)]
    if decay:
        in_specs.append(pl.BlockSpec((None, s_len, 1), lambda h, i: (h, 0, 0)))
        args.append(c_col)
    return pl.pallas_call(
        kern, name=name, grid=(HEADS, s_len // tk), in_specs=in_specs,
        out_specs=[pl.BlockSpec((tk, LANE), lambda h, i: (i, h)), pl.BlockSpec((None, per, 1, tq), lambda h, i: (h, i, 0, 0))],
        out_shape=[jax.ShapeDtypeStruct((s_len, HEADS * LANE), f32), jax.ShapeDtypeStruct((HEADS, nq, 1, tq), f32)],
        compiler_params=pltpu.CompilerParams(dimension_semantics=("parallel", "arbitrary")))(*args)


def _attn_dq(name, q, k, v, o, do, lse, scale, c_col=None):
    (qa, qo), (ka, ko), (va, vo) = q, k, v
    s_len = qa.shape[0]
    tq, tk = _att_tiles(s_len)
    nq, per = s_len // tq, tk // tq
    decay = c_col is not None
    q_mul, s_mul = _fold_scale(scale)

    def kern(*refs):
        q_ref, k_ref, kt_ref, v_ref, o_ref, do_ref, lse_ref = refs[:7]
        dq_ref, delta_ref, dcq_ref = refs[-3:]
        i = pl.program_id(1)
        q_ts, do_ts, deltas, lses = [], [], [], []
        for r in range(per):
            sub = slice(r * tq, (r + 1) * tq)
            q_ts.append((q_ref[sub, :] * q_mul).astype(bf16).T)
            do_t = do_ref[sub, :].T
            deltas.append(jnp.sum(do_t * o_ref[sub, :].T, axis=0, keepdims=True))
            do_ts.append(do_t.astype(bf16))
            lses.append(lse_ref[r])

        def step(j, carry, diagonal):
            rows = pl.ds(pl.multiple_of(j * tk, tk), tk)
            kb = k_ref[rows, :].astype(bf16)
            vb = v_ref[rows, :].astype(bf16)
            ck = refs[7][rows, :] if decay else None
            kt = kt_ref[j]
            new = []
            for r in range(per):
                dq_t, dcq = carry[r]
                n = (r + 1) * tq if diagonal else tk
                s = _scores_t(kb[:n], q_ts[r], s_mul, None if ck is None else ck[:n], r * tq if diagonal else None, tq, n)
                p = jnp.exp(s - lses[r])
                dp = jnp.dot(vb[:n], do_ts[r], preferred_element_type=f32)
                ds = p * (dp - deltas[r])
                new.append((dq_t + jnp.dot(kt[:, :n], ds.astype(bf16), preferred_element_type=f32),
                            dcq + jnp.sum(ds, axis=0, keepdims=True)))
            return tuple(new)

        init = tuple((jnp.zeros((LANE, tq), f32), jnp.zeros((1, tq), f32)) for _ in range(per))
        carry = step(i, lax.fori_loop(0, i, lambda j, c: step(j, c, False), init), True)
        for r, (dq_t, dcq) in enumerate(carry):
            dq_ref[r * tq:(r + 1) * tq, :] = (dq_t * scale).T
            delta_ref[r] = deltas[r]
            dcq_ref[r] = dcq

    row = pl.BlockSpec((None, per, 1, tq), lambda h, i: (h, i, 0, 0))
    in_specs = [pl.BlockSpec((tk, LANE), lambda h, i: (i, qo + h)),
                pl.BlockSpec((s_len, LANE), lambda h, i: (0, ko + h)),
                pl.BlockSpec((None, s_len // tk, LANE, tk), lambda h, i: (h, 0, 0, 0)),
                pl.BlockSpec((s_len, LANE), lambda h, i: (0, vo + h)),
                pl.BlockSpec((tk, LANE), lambda h, i: (i, h)),
                pl.BlockSpec((tk, LANE), lambda h, i: (i, h)), row]
    args = [qa, ka, _head_tiles_t(ka, ko, tk), va, o, do, lse]
    if decay:
        in_specs.append(pl.BlockSpec((None, s_len, 1), lambda h, i: (h, 0, 0)))
        args.append(c_col)
    stat = jax.ShapeDtypeStruct((HEADS, nq, 1, tq), f32)
    return pl.pallas_call(
        kern, name=name, grid=(HEADS, s_len // tk), in_specs=in_specs,
        out_specs=[pl.BlockSpec((tk, LANE), lambda h, i: (i, h)), row, row],
        out_shape=[jax.ShapeDtypeStruct((s_len, HEADS * LANE), f32), stat, stat],
        compiler_params=pltpu.CompilerParams(dimension_semantics=("parallel", "arbitrary")))(*args)


def _attn_dkv(name, q, k, v, do, lse, delta, scale, c_col=None):
    (qa, qo), (ka, ko), (va, vo) = q, k, v
    s_len = qa.shape[0]
    tq, tk = _att_tiles(s_len)
    nq, per = s_len // tq, tk // tq
    decay = c_col is not None
    q_mul, s_mul = _fold_scale(scale)

    def kern(*refs):
        q_ref, k_ref, v_ref, do_ref, lse_ref, delta_ref = refs[:6]
        j = pl.program_id(1)
        kb = k_ref[...].astype(bf16)
        vb = v_ref[...].astype(bf16)
        ck = refs[6][...] if decay else None

        def step(i, carry, diagonal):
            dk, dv, dsum = carry
            for d in range(per):
                tile = i * per + d
                rows = pl.ds(pl.multiple_of(tile * tq, tq), tq)
                qb = (q_ref[rows, :] * q_mul).astype(bf16)
                dob = do_ref[rows, :].astype(bf16)
                s = _scores_t(kb, qb.T, s_mul, ck, d * tq if diagonal else None, tq, tk)
                p = jnp.exp(s - lse_ref[tile])
                dv = dv + jnp.dot(p.astype(bf16), dob, preferred_element_type=f32)
                dp = jnp.dot(vb, dob.T, preferred_element_type=f32)
                ds = p * (dp - delta_ref[tile])
                dk = dk + jnp.dot(ds.astype(bf16), qb, preferred_element_type=f32)
                if decay:
                    dsum = dsum + ds
            return dk, dv, dsum

        init = (jnp.zeros((tk, LANE), f32), jnp.zeros((tk, LANE), f32), jnp.zeros((tk, tq), f32))
        dk, dv, dsum = lax.fori_loop(j + 1, s_len // tk, lambda i, c: step(i, c, False), step(j, init, True))
        if decay:
            dk_ref, dv_ref, dc_ref = refs[-3:]
            dc_ref[...] = -jnp.sum(dsum, axis=1, keepdims=True)
        else:
            dk_ref, dv_ref = refs[-2:]
        dk_ref[...] = dk * s_mul
        dv_ref[...] = dv

    stat = pl.BlockSpec((None, nq, 1, tq), lambda h, j: (h, 0, 0, 0))
    in_specs = [pl.BlockSpec((s_len, LANE), lambda h, j: (0, qo + h)),
                pl.BlockSpec((tk, LANE), lambda h, j: (j, ko + h)),
                pl.BlockSpec((tk, LANE), lambda h, j: (j, vo + h)),
                pl.BlockSpec((s_len, LANE), lambda h, j: (0, h)), stat, stat]
    args = [qa, ka, va, do, lse, delta]
    out_specs = [pl.BlockSpec((tk, LANE), lambda h, j: (j, h)), pl.BlockSpec((tk, LANE), lambda h, j: (j, h))]
    out_shape = [jax.ShapeDtypeStruct((s_len, HEADS * LANE), f32), jax.ShapeDtypeStruct((s_len, HEADS * LANE), f32)]
    if decay:
        in_specs.append(pl.BlockSpec((None, tk, 1), lambda h, j: (h, j, 0)))
        args.append(c_col)
        out_specs.append(pl.BlockSpec((None, tk, 1), lambda h, j: (h, j, 0)))
        out_shape.append(jax.ShapeDtypeStruct((HEADS, s_len, 1), f32))
    return pl.pallas_call(
        kern, name=name, grid=(HEADS, s_len // tk), in_specs=in_specs, out_specs=out_specs, out_shape=out_shape,
        compiler_params=pltpu.CompilerParams(dimension_semantics=("parallel", "arbitrary")))(*args)


CONV_TS, CONV_CB = 1024, 256


def _conv_fwd(name, x, w, b, taps):
    xa, width, xidx = _view(x)
    s_len = xa.shape[0]
    ts, cb = min(CONV_TS, s_len), CONV_CB
    xo = xidx * width // cb

    def kern(x_ref, halo_ref, w_ref, b_ref, o_ref):
        i = pl.program_id(1)
        xb = x_ref[...]
        halo = jnp.where(i == 0, 0.0, halo_ref[...])
        xx = jnp.concatenate([halo, xb], axis=0)
        out = b_ref[...] + w_ref[taps - 1:taps, :] * xb
        for k in range(taps - 1):
            out = out + w_ref[k:k + 1, :] * pltpu.roll(xx, taps - 1 - k, 0)[SUBLANE:]
        o_ref[...] = out

    return pl.pallas_call(
        kern, name=name, grid=(width // cb, s_len // ts),
        in_specs=[pl.BlockSpec((ts, cb), lambda j, i: (i, xo + j)),
                  pl.BlockSpec((SUBLANE, cb), lambda j, i: (jnp.maximum(i * (ts // SUBLANE) - 1, 0), xo + j)),
                  pl.BlockSpec((taps, cb), lambda j, i: (0, j)),
                  pl.BlockSpec((1, cb), lambda j, i: (0, j))],
        out_specs=pl.BlockSpec((ts, cb), lambda j, i: (i, j)),
        out_shape=jax.ShapeDtypeStruct((s_len, width), f32),
        compiler_params=pltpu.CompilerParams(dimension_semantics=("parallel", "parallel")))(xa, xa, w, b)


def _conv_bwd(name, x, dout, w, taps, dout2=None, dx_dtype=f32):
    xa, width, xidx = _view(x)
    s_len = xa.shape[0]
    ts, cb = min(CONV_TS, s_len), CONV_CB
    xo = xidx * width // cb
    n_i = s_len // ts
    two = dout2 is not None

    def kern(*refs):
        x_ref, halo_ref, w_ref = refs[:3]
        dx_ref, dw_ref, db_ref = refs[-3:]
        i = pl.program_id(1)
        if two:
            d = refs[3][...] + refs[5][...]
            dn = refs[4][...] + refs[6][...]
        else:
            d, dn = refs[3][...], refs[4][...]
        dn = jnp.where(i == n_i - 1, 0.0, dn)
        xb = x_ref[...]
        halo = jnp.where(i == 0, 0.0, halo_ref[...])
        xx = jnp.concatenate([halo, xb], axis=0)
        dd = jnp.concatenate([d, dn], axis=0)

        @pl.when(i == 0)
        def _():
            dw_ref[...] = jnp.zeros_like(dw_ref)
            db_ref[...] = jnp.zeros_like(db_ref)

        dx = w_ref[taps - 1:taps, :] * d
        dw_ref[taps - 1:taps, :] += jnp.sum(d * xb, axis=0, keepdims=True)
        for k in range(taps - 1):
            sh = taps - 1 - k
            dx = dx + w_ref[k:k + 1, :] * pltpu.roll(dd, ts + SUBLANE - sh, 0)[:ts]
            dw_ref[k:k + 1, :] += jnp.sum(d * pltpu.roll(xx, sh, 0)[SUBLANE:], axis=0, keepdims=True)
        dx_ref[...] = dx.astype(dx_ref.dtype)
        db_ref[...] += jnp.sum(d, axis=0, keepdims=True)

    d_spec = pl.BlockSpec((ts, cb), lambda j, i: (i, j))
    dn_spec = pl.BlockSpec((SUBLANE, cb), lambda j, i: (jnp.minimum((i + 1) * (ts // SUBLANE), s_len // SUBLANE - 1), j))
    in_specs = [pl.BlockSpec((ts, cb), lambda j, i: (i, xo + j)),
                pl.BlockSpec((SUBLANE, cb), lambda j, i: (jnp.maximum(i * (ts // SUBLANE) - 1, 0), xo + j)),
                pl.BlockSpec((taps, cb), lambda j, i: (0, j)), d_spec, dn_spec]
    args = [xa, xa, w, dout, dout]
    if two:
        in_specs += [d_spec, dn_spec]
        args += [dout2, dout2]
    return pl.pallas_call(
        kern, name=name, grid=(width // cb, n_i), in_specs=in_specs,
        out_specs=[pl.BlockSpec((ts, cb), lambda j, i: (i, j)), pl.BlockSpec((taps, cb), lambda j, i: (0, j)),
                   pl.BlockSpec((1, cb), lambda j, i: (0, j))],
        out_shape=[jax.ShapeDtypeStruct((s_len, width), dx_dtype), jax.ShapeDtypeStruct((taps, width), f32),
                   jax.ShapeDtypeStruct((1, width), f32)],
        compiler_params=pltpu.CompilerParams(dimension_semantics=("parallel", "arbitrary")))(*args)


def _conv_rows(xx, w_ref, b_ref, taps):
    out = b_ref[...] + w_ref[taps - 1:taps, :] * xx[SUBLANE:]
    for k in range(taps - 1):
        out = out + w_ref[k:k + 1, :] * pltpu.roll(xx, taps - 1 - k, 0)[SUBLANE:]
    return out


def _ffn_act_fwd(name, up, w, b):
    s_len = up.shape[0]
    ts, cb = min(CONV_TS, s_len), CONV_CB
    nf = D_FF // cb

    def kern(g_ref, gp_ref, v_ref, vp_ref, wg_ref, wv_ref, bg_ref, bv_ref, o_ref, ot_ref):
        first = pl.program_id(1) == 0
        ug = _conv_rows(jnp.concatenate([jnp.where(first, 0.0, gp_ref[...]), g_ref[...]], axis=0), wg_ref, bg_ref, FFN_CONV)
        uv = _conv_rows(jnp.concatenate([jnp.where(first, 0.0, vp_ref[...]), v_ref[...]], axis=0), wv_ref, bv_ref, FFN_CONV)
        act = (jax.nn.silu(ug) * uv).astype(o_ref.dtype)
        o_ref[...] = act
        ot_ref[...] = act.T

    def half(off):
        return [pl.BlockSpec((ts, cb), lambda j, i: (i, off + j)),
                pl.BlockSpec((SUBLANE, cb), lambda j, i: (jnp.maximum(i * (ts // SUBLANE) - 1, 0), off + j))]

    def par(rows, off):
        return pl.BlockSpec((rows, cb), lambda j, i: (0, off + j))

    return pl.pallas_call(
        kern, name=name, grid=(nf, s_len // ts),
        in_specs=half(0) + half(nf) + [par(FFN_CONV, 0), par(FFN_CONV, nf), par(1, 0), par(1, nf)],
        out_specs=[pl.BlockSpec((ts, cb), lambda j, i: (i, j)), pl.BlockSpec((cb, ts), lambda j, i: (j, i))],
        out_shape=[jax.ShapeDtypeStruct((s_len, D_FF), bf16), jax.ShapeDtypeStruct((D_FF, s_len), bf16)],
        compiler_params=pltpu.CompilerParams(dimension_semantics=("parallel", "parallel")))(up, up, up, up, w, w, b, b)


def _ffn_act_bwd(name, up, dact, w, b):
    s_len = up.shape[0]
    ts, cb = min(CONV_TS, s_len), CONV_CB
    nf = D_FF // cb
    n_i = s_len // ts
    taps = FFN_CONV

    def kern(g_ref, gp_ref, gn_ref, v_ref, vp_ref, vn_ref, d_ref, dn_ref, wg_ref, wv_ref, bg_ref, bv_ref,
             dg_ref, dv_ref, dwg_ref, dwv_ref, dbg_ref, dbv_ref):
        i = pl.program_id(1)
        first, last = i == 0, i == n_i - 1

        def extended(x_ref, p_ref, n_ref):
            return jnp.concatenate([jnp.where(first, 0.0, p_ref[...]), x_ref[...], jnp.where(last, 0.0, n_ref[...])], axis=0)

        gx, vx = extended(g_ref, gp_ref, gn_ref), extended(v_ref, vp_ref, vn_ref)
        ug, uv = _conv_rows(gx, wg_ref, bg_ref, taps), _conv_rows(vx, wv_ref, bv_ref, taps)
        dd = jnp.concatenate([d_ref[...], jnp.where(last, 0.0, dn_ref[...])], axis=0)
        sg = jax.nn.sigmoid(ug)
        dug = dd * uv * (sg * (1.0 + ug * (1.0 - sg)))
        duv = dd * (ug * sg)

        @pl.when(first)
        def _():
            for ref in (dwg_ref, dwv_ref, dbg_ref, dbv_ref):
                ref[...] = jnp.zeros_like(ref)

        def transposed(du, xx, w_ref, dx_ref, dw_ref, db_ref):
            d = du[:ts]
            dx = w_ref[taps - 1:taps, :] * d
            dw_ref[taps - 1:taps, :] += jnp.sum(d * xx[SUBLANE:SUBLANE + ts], axis=0, keepdims=True)
            for k in range(taps - 1):
                sh = taps - 1 - k
                dx = dx + w_ref[k:k + 1, :] * pltpu.roll(du, ts + SUBLANE - sh, 0)[:ts]
                dw_ref[k:k + 1, :] += jnp.sum(d * pltpu.roll(xx, sh, 0)[SUBLANE:SUBLANE + ts], axis=0, keepdims=True)
            dx_ref[...] = dx.astype(dx_ref.dtype)
            db_ref[...] += jnp.sum(d, axis=0, keepdims=True)

        transposed(dug, gx, wg_ref, dg_ref, dwg_ref, dbg_ref)
        transposed(duv, vx, wv_ref, dv_ref, dwv_ref, dbv_ref)

    blocks = s_len // SUBLANE

    def half(off):
        return [pl.BlockSpec((ts, cb), lambda j, i: (i, off + j)),
                pl.BlockSpec((SUBLANE, cb), lambda j, i: (jnp.maximum(i * (ts // SUBLANE) - 1, 0), off + j)),
                pl.BlockSpec((SUBLANE, cb), lambda j, i: (jnp.minimum((i + 1) * (ts // SUBLANE), blocks - 1), off + j))]

    def par(rows, off):
        return pl.BlockSpec((rows, cb), lambda j, i: (0, off + j))

    d_specs = [pl.BlockSpec((ts, cb), lambda j, i: (i, j)),
               pl.BlockSpec((SUBLANE, cb), lambda j, i: (jnp.minimum((i + 1) * (ts // SUBLANE), blocks - 1), j))]
    out_par = [pl.BlockSpec((r, cb), lambda j, i: (0, j)) for r in (taps, taps, 1, 1)]
    return pl.pallas_call(
        kern, name=name, grid=(nf, n_i),
        in_specs=half(0) + half(nf) + d_specs + [par(taps, 0), par(taps, nf), par(1, 0), par(1, nf)],
        out_specs=[pl.BlockSpec((ts, cb), lambda j, i: (i, j))] * 2 + out_par,
        out_shape=[jax.ShapeDtypeStruct((s_len, D_FF), bf16)] * 2 + [jax.ShapeDtypeStruct((taps, D_FF), f32)] * 2
        + [jax.ShapeDtypeStruct((1, D_FF), f32)] * 2,
        compiler_params=pltpu.CompilerParams(dimension_semantics=("parallel", "arbitrary")))(
        up, up, up, up, up, up, dact, dact, w, w, b, b)


SCAN_ROWS = 128


def _block_scan(a, b, reverse):
    t = a.shape[0]
    row = lax.broadcasted_iota(jnp.int32, a.shape, 0)
    d = 1
    while d < t:
        keep = row < t - d if reverse else row >= d
        shift = t - d if reverse else d
        a_far = jnp.where(keep, pltpu.roll(a, shift, 0), 1.0)
        b_far = jnp.where(keep, pltpu.roll(b, shift, 0), 0.0)
        b = a * b_far + b
        a = a * a_far
        d *= 2
    return a, b


def _scan_fwd(name, a, b):
    s_len, width = a.shape
    t = min(SCAN_ROWS, s_len)

    def kern(a_ref, b_ref, h_ref):
        def block(k, carry):
            rows = pl.ds(pl.multiple_of(k * t, t), t)
            acc, h = _block_scan(a_ref[rows, :], b_ref[rows, :], False)
            h_ref[rows, :] = h + acc * carry
            return h_ref[pl.ds(k * t + t - 1, 1), :]

        lax.fori_loop(0, s_len // t, block, jnp.zeros((1, LANE), f32))

    spec = pl.BlockSpec((s_len, LANE), lambda j: (0, j))
    return pl.pallas_call(
        kern, name=name, grid=(width // LANE,), in_specs=[spec, spec], out_specs=spec,
        out_shape=jax.ShapeDtypeStruct((s_len, width), f32),
        compiler_params=pltpu.CompilerParams(dimension_semantics=("parallel",)))(a, b)


def _scan_bwd(name, a_next, h_prev, dh):
    s_len, width = dh.shape
    t = min(SCAN_ROWS, s_len)
    n_blocks = s_len // t

    def kern(an_ref, hp_ref, dh_ref, da_ref, db_ref):
        def block(kk, carry):
            k = n_blocks - 1 - kk
            rows = pl.ds(pl.multiple_of(k * t, t), t)
            acc, g = _block_scan(an_ref[rows, :], dh_ref[rows, :], True)
            g = g + acc * carry
            db_ref[rows, :] = g
            da_ref[rows, :] = g * hp_ref[rows, :]
            return db_ref[pl.ds(k * t, 1), :]

        lax.fori_loop(0, n_blocks, block, jnp.zeros((1, LANE), f32))

    spec = pl.BlockSpec((s_len, LANE), lambda j: (0, j))
    return pl.pallas_call(
        kern, name=name, grid=(width // LANE,), in_specs=[spec, spec, spec], out_specs=[spec, spec],
        out_shape=[jax.ShapeDtypeStruct((s_len, width), f32)] * 2,
        compiler_params=pltpu.CompilerParams(dimension_semantics=("parallel",)))(a_next, h_prev, dh)


def _lane_cumsum(x, reverse):
    n = x.shape[1]
    lane = lax.broadcasted_iota(jnp.int32, x.shape, 1)
    sh = 1
    while sh < n:
        if reverse:
            x = x + jnp.where(lane < n - sh, pltpu.roll(x, n - sh, 1), 0.0)
        else:
            x = x + jnp.where(lane >= sh, pltpu.roll(x, sh, 1), 0.0)
        sh *= 2
    return x


def _decay_fwd(name, fl_t, b8):
    def kern(f_ref, b_ref, c_ref):
        c_ref[...] = _lane_cumsum(jax.nn.log_sigmoid(f_ref[...] + b_ref[...]), False)

    return pl.pallas_call(kern, name=name, out_shape=jax.ShapeDtypeStruct(fl_t.shape, f32))(fl_t, b8)


def _decay_bwd(name, fl_t, b8, dc_key, dc_query):
    def kern(f_ref, b_ref, dck_ref, dcq_ref, df_ref, db_ref):
        dlogf = _lane_cumsum(dck_ref[...] + dcq_ref[...], True)
        df = dlogf * jax.nn.sigmoid(-(f_ref[...] + b_ref[...]))
        df_ref[...] = df
        db_ref[...] = jnp.sum(df, axis=1, keepdims=True)

    return pl.pallas_call(kern, name=name, out_shape=[jax.ShapeDtypeStruct(fl_t.shape, f32),
                                                      jax.ShapeDtypeStruct((SUBLANE, 1), f32)])(fl_t, b8, dc_key, dc_query)


def _rms(x, g, n):
    return x * lax.rsqrt(jnp.sum(x * x, axis=-1, keepdims=True) * (1.0 / n) + EPS) * g


def _loss_head(name, h, target, g, tb=256):
    n, d = h.shape
    tb = min(tb, n)

    def kern(h_ref, t_ref, g_ref, loss_ref, dh_ref, dg_ref):
        i = pl.program_id(0)
        tgt = t_ref[...]

        def f(hv, gv):
            err = _rms(hv, gv, d) - tgt
            return 0.5 * jnp.sum(jnp.sum(err * err, axis=-1, keepdims=True) * (1.0 / d), axis=0, keepdims=True)

        val, vjp = jax.vjp(f, h_ref[...], g_ref[...])
        dh, dg = vjp(jnp.ones((1, 1), f32))
        dh_ref[...] = dh

        @pl.when(i == 0)
        def _():
            loss_ref[...] = jnp.zeros_like(loss_ref)
            dg_ref[...] = jnp.zeros_like(dg_ref)

        loss_ref[...] += val
        dg_ref[...] += dg

    return pl.pallas_call(
        kern, name=name, grid=(n // tb,),
        in_specs=[pl.BlockSpec((tb, d), lambda i: (i, 0)), pl.BlockSpec((tb, d), lambda i: (i, 0)),
                  pl.BlockSpec((1, d), lambda i: (0, 0))],
        out_specs=[pl.BlockSpec((1, 1), lambda i: (0, 0)), pl.BlockSpec((tb, d), lambda i: (i, 0)),
                   pl.BlockSpec((1, d), lambda i: (0, 0))],
        out_shape=[jax.ShapeDtypeStruct((1, 1), f32), jax.ShapeDtypeStruct((n, d), f32), jax.ShapeDtypeStruct((1, d), f32)],
        compiler_params=pltpu.CompilerParams(dimension_semantics=("arbitrary",)))(h, target, g)


def _f_norm(x, g):
    return (_rms(x, g, D_MODEL),)


def _f_latent(qc, kvc, gq, gkv):
    return _rms(qc, gq, MLA_Q_RANK), _rms(kvc, gkv, MLA_KV_RANK)


def _f_rope_table(pos, freq, m1, m2):
    ang = pos * freq
    sin = jnp.sin(ang)
    return jnp.cos(ang), -sin * m1, sin * m2


def _rope(x, cos, s_up, s_down):
    w = x.shape[1]
    return x * cos + _roll(x, w - MLA_ROPE // 2, 1) * s_up + _roll(x, MLA_ROPE // 2, 1) * s_down


def _f_mla_prep(q, kpart, kr, cos, s_up, s_down):
    def heads(t):
        return jnp.concatenate([t] * HEADS, axis=1)

    kr = _rope(kr, cos, s_up, s_down)
    return _rope(q, heads(cos), heads(s_up), heads(s_down)), kpart + heads(kr)


def _f_lru_gate(gates, xc, b_r, b_i, lam):
    r = jax.nn.sigmoid(gates[:, :LRU_WIDTH] + b_r)
    i = jax.nn.sigmoid(gates[:, LRU_WIDTH:] + b_i)
    log_a = -LRU_C * r * jax.nn.softplus(-lam)
    mult = jnp.sqrt(-jnp.tanh(log_a) * (1.0 + jnp.exp(2.0 * log_a)))
    return jnp.exp(log_a), mult * (i * xc)


def _f_merge(o_mla, o_fox, hs, lg, g):
    o_lru = hs * jax.nn.gelu(lg)
    return (jnp.concatenate([_rms(o_mla, g[:, :512], HEADS * MLA_V), _rms(o_fox, g[:, 512:1024], HEADS * FOX_HEAD_DIM),
                             _rms(o_lru, g[:, 1024:], LRU_WIDTH)], axis=1),)


def _f_ffn_gate(u):
    return (jax.nn.silu(u[:, :D_FF]) * u[:, D_FF:],)


def _f_ple(h, gpre, pp):
    return (h + jax.nn.sigmoid(gpre) * pp,)


MIX_PART = ["w_in", "w_uq", "w_ukv", "lru_conv_w", "w_o"]
FFN_PART = ["w_up", "ffn_conv_w", "w_down", "w_ple_gate", "w_ple_proj"]


def _prep_mix_weights(w):
    eye = jnp.eye(LRU_BLOCKS, dtype=f32)

    def block_diag(m):
        return (eye[:, None, :, None] * m[:, :, None, :]).reshape(LRU_WIDTH, LRU_WIDTH)

    return dict(
        w_in=_take_pad(w["w_in"], Z_MAP, 1),
        w_uq=_take_pad(_take_pad(w["w_uq"], UQ_COL_MAP, 1), UQ_ROW_MAP, 0),
        w_ukv=_take_pad(w["w_ukv"], UKV_MAP, 1),
        w_ri=jnp.concatenate([block_diag(w["w_r"]), block_diag(w["w_i"])], axis=1).astype(bf16),
        w_o=_take_pad(w["w_o"], OMIX_MAP, 0),
        g_mix=w["g_mix"].reshape(1, -1), g_ffn=w["g_ffn"].reshape(1, -1), g_ple=w["g_ple"].reshape(1, -1),
        g_qc=_take_pad(w["g_qc"], UQ_ROW_MAP, 0).reshape(1, -1), g_kvc=w["g_kvc"].reshape(1, -1),
        g_out=_take_pad(w["g_out"], OMIX_MAP, 0).reshape(1, -1),
        b_f8=_take_pad(w["b_f"], _pad_to(np.arange(FOX_HEADS), SUBLANE), 0).reshape(SUBLANE, 1),
        lru_conv_w=w["lru_conv_w"], lru_conv_b=w["lru_conv_b"].reshape(1, -1),
        b_r=w["b_r"].reshape(1, -1), b_i=w["b_i"].reshape(1, -1), lam=w["lru_lambda"].reshape(1, -1),
        ffn_conv_b=w["ffn_conv_b"].reshape(1, -1),
    )


def _prep_ffn_weights(w):
    return dict(w_up=w["w_up"], w_up_g=w["w_up"][:, :D_FF], w_up_v=w["w_up"][:, D_FF:], ffn_conv_w=w["ffn_conv_w"],
                w_down=w["w_down"], w_ple_gate=w["w_ple_gate"], w_ple_proj=w["w_ple_proj"])


def _rope_rows(pos):
    consts = [jnp.asarray(t) for t in _rope_tables(LANE, ROPE_AT)]
    return _rowwise("rope_table", _f_rope_table, [pos], consts, [(LANE, f32)] * 3)


def _key_decay(c_t, s_len):
    return c_t[:HEADS].reshape(HEADS, s_len, 1)


def _layer_fwd(l, h0, p_l, rope, weights_of):
    s_len = h0.shape[0]
    n = f"l{l}_"
    w = _prep_mix_weights(weights_of("mix", h0))
    xn, xn_t = _rowwise(n + "norm_mix", _f_norm, [h0], [w["g_mix"]], [(D_MODEL, bf16)], transposed=(0,))
    z = _mm(n + "in_proj", xn, w["w_in"])
    zq = (z, QC_W, Z_QC // QC_W)
    zkv = (z, LANE, Z_KVC // LANE)
    zkr = (z, LANE, Z_KR // LANE)
    zlx = (z, LRU_WIDTH, Z_LX // LRU_WIDTH)
    zlg = (z, LRU_WIDTH, Z_LG // LRU_WIDTH)
    qcn, kvn, qcn_t, kvn_t = _rowwise(n + "latent_norm", _f_latent, [zq, zkv], [w["g_qc"], w["g_kvc"]],
                                      [(QC_W, bf16), (LANE, bf16)], transposed=(0, 1))
    q = _mm(n + "uq", qcn, w["w_uq"])
    kv = _mm(n + "ukv", kvn, w["w_ukv"])
    kpart = (kv, HEADS * LANE, 0)
    qr, kk = _rowwise(n + "mla_prep", _f_mla_prep, [q, kpart, zkr, *rope], [],
                      [(HEADS * LANE, bf16), (HEADS * LANE, bf16)])
    mla_scale = (MLA_NOPE + MLA_ROPE) ** -0.5
    o_mla, lse_m = _attn_fwd(n + "mla_fwd", (qr, 0), (kk, 0), (kv, HEADS), mla_scale)
    fl_t = z[:, Z_FL:Z_FL + SUBLANE].T
    c_t = _decay_fwd(n + "decay", fl_t, w["b_f8"])
    c_col = _key_decay(c_t, s_len)
    fox_scale = FOX_HEAD_DIM ** -0.5
    o_fox, lse_f = _attn_fwd(n + "fox_fwd", (z, Z_FQ // LANE), (z, Z_FK // LANE), (z, Z_FV // LANE), fox_scale, c_col)
    xc = _conv_fwd(n + "lru_conv", zlx, w["lru_conv_w"], w["lru_conv_b"], LRU_CONV)
    gates = _mm(n + "lru_gates", xc, w["w_ri"])
    a, bx = _rowwise(n + "lru_gate", _f_lru_gate, [gates, xc], [w["b_r"], w["b_i"], w["lam"]],
                     [(LRU_WIDTH, f32), (LRU_WIDTH, f32)])
    hs = _scan_fwd(n + "lru_scan", a, bx)
    ocat, ocat_t = _rowwise(n + "merge", _f_merge, [o_mla, o_fox, hs, zlg], [w["g_out"]], [(OMIX_W, bf16)], transposed=(0,))
    h1 = _mm(n + "out_proj", ocat, w["w_o"], res=h0)
    w.update(_prep_ffn_weights(weights_of("ffn", h1)))
    xn2, xn2_t = _rowwise(n + "norm_ffn", _f_norm, [h1], [w["g_ffn"]], [(D_MODEL, bf16)], transposed=(0,))
    up = _mm(n + "up_proj", xn2, w["w_up"])
    act, act_t = _ffn_act_fwd(n + "ffn_act", up, w["ffn_conv_w"], w["ffn_conv_b"])
    h2 = _mm(n + "down_proj", act, w["w_down"], res=h1)
    hn, hn_t = _rowwise(n + "norm_ple", _f_norm, [h2], [w["g_ple"]], [(D_MODEL, bf16)], transposed=(0,))
    gpre = _mm(n + "ple_gate", hn, w["w_ple_gate"])
    pp = _mm(n + "ple_proj", p_l, w["w_ple_proj"])
    h3, = _rowwise(n + "ple_mix", _f_ple, [h2, gpre, pp], [], [(D_MODEL, f32)])
    res = dict(h0=h0, xn_t=xn_t, z=z, qcn_t=qcn_t, kvn_t=kvn_t, ocat_t=ocat_t, xn2_t=xn2_t, act_t=act_t, hn_t=hn_t, q=q, kv=kv, qr=qr, kk=kk, o_mla=o_mla, lse_m=lse_m, fl_t=fl_t,
               c_col=c_col, o_fox=o_fox, lse_f=lse_f, xc=xc, gates=gates, a=a, hs=hs, h1=h1,
               up=up, h2=h2, gpre=gpre, pp=pp, p_l=p_l)
    return h3, res, w


def _layer_bwd(l, dh3, r, rope, w, token, grads_to):
    s_len = dh3.shape[0]
    n = f"l{l}_"
    g = {}
    w = dict(w, g_ple=w["g_ple"] + token)
    z = r["z"]
    zq = (z, QC_W, Z_QC // QC_W)
    zkv = (z, LANE, Z_KVC // LANE)
    zkr = (z, LANE, Z_KR // LANE)
    zlx = (z, LRU_WIDTH, Z_LX // LRU_WIDTH)
    zlg = (z, LRU_WIDTH, Z_LG // LRU_WIDTH)
    (dh2a, dgpre, dpp), _ = _rowwise_bwd(n + "ple_mix_b", _f_ple, [r["h2"], r["gpre"], r["pp"]], [], [dh3], 3,
                                         dts=[f32, bf16, bf16])
    g["w_ple_proj"] = _mm(n + "ple_proj_dw", r["p_l"].astype(bf16).T, dpp, "nn", bf16)
    dhn = _mm(n + "ple_gate_dx", dgpre, w["w_ple_gate"], "nt")
    g["w_ple_gate"] = _mm(n + "ple_gate_dw", r["hn_t"], dgpre, "nn", bf16)
    (dh2,), (g["g_ple"],) = _rowwise_bwd(n + "norm_ple_b", _f_norm, [r["h2"]], [w["g_ple"]], [dhn], 1, adds={0: dh2a})
    dact = _mm(n + "down_dx", dh2, w["w_down"], "nt")
    g["w_down"] = _mm(n + "down_dw", r["act_t"], dh2, "nn", bf16)
    dup_g, dup_v, dcw_g, dcw_v, dcb_g, dcb_v = _ffn_act_bwd(n + "ffn_act_b", r["up"], dact, w["ffn_conv_w"], w["ffn_conv_b"])
    g["ffn_conv_w"] = jnp.concatenate([dcw_g, dcw_v], axis=1)
    g["ffn_conv_b"] = jnp.concatenate([dcb_g, dcb_v], axis=1)
    dxn2 = _mm(n + "up_dx_v", dup_v, w["w_up_v"], "nt", res=_mm(n + "up_dx_g", dup_g, w["w_up_g"], "nt"))
    g["w_up"] = jnp.concatenate([_mm(n + "up_dw_g", r["xn2_t"], dup_g, "nn", bf16),
                                 _mm(n + "up_dw_v", r["xn2_t"], dup_v, "nn", bf16)], axis=1)
    (dh1,), (g["g_ffn"],) = _rowwise_bwd(n + "norm_ffn_b", _f_norm, [r["h1"]], [w["g_ffn"]], [dxn2], 1, adds={0: dh2})
    token = grads_to("ffn", dict(w_up=g["w_up"], ffn_conv_w=g["ffn_conv_w"], w_down=g["w_down"],
                                 w_ple_gate=g["w_ple_gate"], w_ple_proj=g["w_ple_proj"]))
    w = dict(w, g_out=w["g_out"] + token)
    docat = _mm(n + "out_dx", dh1, w["w_o"], "nt")
    g["w_o"] = _mm(n + "out_dw", r["ocat_t"], dh1, "nn", bf16)
    (do_mla, do_fox, dhs, dlg), (g["g_out"],) = _rowwise_bwd(
        n + "merge_b", _f_merge, [r["o_mla"], r["o_fox"], r["hs"], zlg], [w["g_out"]], [docat], 4)
    a, hs = r["a"], r["hs"]
    a_next = jnp.concatenate([a[1:], jnp.zeros((1, LRU_WIDTH), f32)], axis=0)
    h_prev = jnp.concatenate([jnp.zeros((1, LRU_WIDTH), f32), hs[:-1]], axis=0)
    da, dbx = _scan_bwd(n + "lru_scan_b", a_next, h_prev, dhs)
    (dgates, dxc_a), (g["b_r"], g["b_i"], g["lam"]) = _rowwise_bwd(
        n + "lru_gate_b", _f_lru_gate, [r["gates"], r["xc"]], [w["b_r"], w["b_i"], w["lam"]], [da, dbx], 2,
        dts=[bf16, f32])
    dxc_b = _mm(n + "lru_gates_dx", dgates, w["w_ri"], "nt")
    g["w_ri"] = _mm(n + "lru_gates_dw", r["xc"].astype(bf16).T, dgates, "nn")
    dlx, g["lru_conv_w"], g["lru_conv_b"] = _conv_bwd(n + "lru_conv_b", zlx, dxc_a, w["lru_conv_w"], LRU_CONV, dout2=dxc_b)
    fox_scale = FOX_HEAD_DIM ** -0.5
    fq, fk, fv = (z, Z_FQ // LANE), (z, Z_FK // LANE), (z, Z_FV // LANE)
    dfq, delta_f, dc_q = _attn_dq(n + "fox_dq", fq, fk, fv, r["o_fox"], do_fox, r["lse_f"], fox_scale, r["c_col"])
    dfk, dfv, dc_k = _attn_dkv(n + "fox_dkv", fq, fk, fv, do_fox, r["lse_f"], delta_f, fox_scale, r["c_col"])
    pad_rows = jnp.zeros((SUBLANE - HEADS, s_len), f32)
    dfl_t, g["b_f8"] = _decay_bwd(n + "decay_b", r["fl_t"], w["b_f8"],
                                  jnp.concatenate([dc_k.reshape(HEADS, s_len), pad_rows], axis=0),
                                  jnp.concatenate([dc_q.reshape(HEADS, s_len), pad_rows], axis=0))
    dfl = jnp.pad(dfl_t.T, ((0, 0), (0, LANE - SUBLANE)))
    mla_scale = (MLA_NOPE + MLA_ROPE) ** -0.5
    qr, kk, kv = (r["qr"], 0), (r["kk"], 0), (r["kv"], HEADS)
    dqr, delta_m, _ = _attn_dq(n + "mla_dq", qr, kk, kv, r["o_mla"], do_mla, r["lse_m"], mla_scale)
    dkk, dv_m = _attn_dkv(n + "mla_dkv", qr, kk, kv, do_mla, r["lse_m"], delta_m, mla_scale)
    (dq, dkpart, dkr), _ = _rowwise_bwd(n + "mla_prep_b", _f_mla_prep, [r["q"], (r["kv"], HEADS * LANE, 0), zkr, *rope],
                                        [], [dqr, dkk], 3, dts=[bf16, bf16, f32])
    dkv = jnp.concatenate([dkpart, dv_m.astype(bf16)], axis=1)
    dkvn = _mm(n + "ukv_dx", dkv, w["w_ukv"], "nt")
    g["w_ukv"] = _mm(n + "ukv_dw", r["kvn_t"], dkv, "nn", bf16)
    dqcn = _mm(n + "uq_dx", dq, w["w_uq"], "nt")
    g["w_uq"] = _mm(n + "uq_dw", r["qcn_t"], dq, "nn", bf16)
    (dqc, dkvc), (g["g_qc"], g["g_kvc"]) = _rowwise_bwd(n + "latent_norm_b", _f_latent, [zq, zkv],
                                                        [w["g_qc"], w["g_kvc"]], [dqcn, dkvn], 2)
    dz = jnp.concatenate([t.astype(bf16) for t in (dfq, dfk, dfv, dlx, dlg, dqc, dkvc, dkr, dfl)], axis=1)
    dxn = _mm(n + "in_dx", dz, w["w_in"], "nt")
    g["w_in"] = _mm(n + "in_dw", r["xn_t"], dz, "nn", bf16)
    (dh0,), (g["g_mix"],) = _rowwise_bwd(n + "norm_mix_b", _f_norm, [r["h0"]], [w["g_mix"]], [dxn], 1, adds={0: dh1})
    return dh0, grads_to("mix", _unpad_mix_grads(g))


def _unpad_mix_grads(g):
    d_ri = g["w_ri"]
    idx = jnp.arange(LRU_BLOCKS)

    def diag_blocks(m):
        return m.reshape(LRU_BLOCKS, LRU_BLOCK, LRU_BLOCKS, LRU_BLOCK)[idx, :, idx, :]

    return dict(
        g_mix=g["g_mix"][0], w_in=_take_inv(g["w_in"], Z_MAP, 1), g_qc=g["g_qc"][0, :MLA_Q_RANK],
        w_uq=_take_inv(g["w_uq"][:MLA_Q_RANK], UQ_COL_MAP, 1), g_kvc=g["g_kvc"][0],
        w_ukv=_take_inv(g["w_ukv"], UKV_MAP, 1), b_f=g["b_f8"][:FOX_HEADS, 0],
        lru_conv_w=g["lru_conv_w"], lru_conv_b=g["lru_conv_b"][0],
        w_r=diag_blocks(d_ri[:, :LRU_WIDTH]), b_r=g["b_r"][0], w_i=diag_blocks(d_ri[:, LRU_WIDTH:]), b_i=g["b_i"][0],
        lru_lambda=g["lam"][0], g_out=_take_inv(g["g_out"][0], OMIX_MAP, 0), w_o=_take_inv(g["w_o"], OMIX_MAP, 0),
        g_ffn=g["g_ffn"][0], ffn_conv_b=g["ffn_conv_b"][0], g_ple=g["g_ple"][0],
    )


LAYER_WEIGHTS = ["g_mix", "w_in", "g_qc", "w_uq", "g_kvc", "w_ukv", "b_f", "lru_conv_w", "lru_conv_b", "w_r", "b_r", "w_i",
                 "b_i", "lru_lambda", "g_out", "w_o", "g_ffn", "w_up", "ffn_conv_w", "ffn_conv_b", "w_down", "g_ple",
                 "w_ple_gate", "w_ple_proj"]
WEIGHTS = LAYER_WEIGHTS + ["g_final"]


def _local_step(x, p, pos, target, g_final, weights_of, grads_to):
    h = x
    rope = _rope_rows(pos)
    ws, saved = [], []
    for l in range(DEPTH):
        h, r, w = _layer_fwd(l, h, p[l], rope, functools.partial(weights_of, l))
        ws.append(w)
        saved.append(r)
    loss, dh, dg_final = _loss_head("loss_head", h, target, g_final.reshape(1, -1))
    token = jnp.zeros((), f32)
    for l in reversed(range(DEPTH)):
        dh, token = _layer_bwd(l, dh, saved[l], rope, ws[l], token, functools.partial(grads_to, l))
    return loss[0, 0], dh, dg_final[0]


MESH_AXES = ("x", "y", "c")


def _exchange(name, src, axes, scatter, pieces=1):
    n = 2 ** len(axes)
    flips = [tuple((f >> (len(axes) - 1 - b)) & 1 for b in range(len(axes))) for f in range(1, n)]
    rows = src.shape[-2]
    piece_rows = rows // pieces
    assert piece_rows * pieces == rows

    def body(src_ref, out_ref, send_sems, recv_sems, local_sem):
        coords = {a: lax.axis_index(a) for a in MESH_AXES}

        def index_of(cd):
            idx = 0
            for a in axes:
                idx = idx * 2 + cd[a]
            return idx

        me = index_of(coords)
        local = pltpu.make_async_copy(src_ref.at[me] if scatter else src_ref, out_ref.at[me], local_sem)
        local.start()
        copies = []
        for k, f in enumerate(flips):
            peer = dict(coords)
            for a, bit in zip(axes, f):
                if bit:
                    peer[a] = 1 - coords[a]
            slab = src_ref.at[index_of(peer)] if scatter else src_ref
            for pc in range(pieces):
                span = pl.ds(pc * piece_rows, piece_rows)
                cp = pltpu.make_async_remote_copy(
                    src_ref=slab.at[span], dst_ref=out_ref.at[me, span],
                    send_sem=send_sems.at[k * pieces + pc], recv_sem=recv_sems.at[k * pieces + pc],
                    device_id=tuple(peer[a] for a in MESH_AXES), device_id_type=pl.DeviceIdType.MESH)
                cp.start()
                copies.append(cp)
        for cp in copies:
            cp.wait()
        local.wait()

    n_sems = (n - 1) * pieces
    return pl.pallas_call(
        body, name=name, out_shape=jax.ShapeDtypeStruct((n, rows, LANE), src.dtype),
        in_specs=[pl.BlockSpec(memory_space=pl.ANY)], out_specs=pl.BlockSpec(memory_space=pl.ANY),
        scratch_shapes=[pltpu.SemaphoreType.DMA((n_sems,)), pltpu.SemaphoreType.DMA((n_sems,)), pltpu.SemaphoreType.DMA])(src)


def _row_tile(rows, cap):
    if rows <= cap:
        return rows
    for t in range(cap, SUBLANE - 1, -SUBLANE):
        if rows % t == 0:
            return t
    return rows


def _sum_slabs(name, a):
    n, rows, _ = a.shape
    tr = _row_tile(rows, 512)

    def kern(a_ref, o_ref):
        acc = a_ref[0].astype(f32)
        for k in range(1, n):
            acc = acc + a_ref[k].astype(f32)
        o_ref[...] = acc

    return pl.pallas_call(
        kern, name=name, grid=(rows // tr,), in_specs=[pl.BlockSpec((n, tr, LANE), lambda i: (0, i, 0))],
        out_specs=pl.BlockSpec((tr, LANE), lambda i: (i, 0)), out_shape=jax.ShapeDtypeStruct((rows, LANE), f32),
        compiler_params=pltpu.CompilerParams(dimension_semantics=("parallel",)))(a)


ADAM_BLOCK_BYTES = 2 ** 20


def _adamw(name, w, g, m, v):
    rows, cols = w.shape
    tr = _row_tile(rows, max(SUBLANE, ADAM_BLOCK_BYTES // (4 * cols) // SUBLANE * SUBLANE))

    def kern(w_ref, g_ref, m_ref, v_ref, d_ref, nm_ref, nv_ref):
        gv = g_ref[...]
        nm = ADAM_B1 * m_ref[...] + (1.0 - ADAM_B1) * gv
        nv = ADAM_B2 * v_ref[...] + (1.0 - ADAM_B2) * (gv * gv)
        m_hat = nm / (1.0 - ADAM_B1 ** ADAM_STEP)
        v_hat = nv / (1.0 - ADAM_B2 ** ADAM_STEP)
        d_ref[...] = -ADAM_LR * (m_hat / (jnp.sqrt(v_hat) + ADAM_EPS) + ADAM_WD * w_ref[...])
        nm_ref[...] = nm
        nv_ref[...] = nv

    spec = pl.BlockSpec((tr, cols), lambda i: (i, 0))
    return pl.pallas_call(
        kern, name=name, grid=(rows // tr,), in_specs=[spec] * 4, out_specs=[spec] * 3,
        out_shape=[jax.ShapeDtypeStruct((rows, cols), f32)] * 3,
        compiler_params=pltpu.CompilerParams(dimension_semantics=("parallel",)))(w, g, m, v)


def _packed_rows(shape):
    return -(-int(np.prod(shape)) // (SUBLANE * LANE)) * SUBLANE


def _pack(arrays):
    rows = []
    for a in arrays:
        flat = a.reshape(-1)
        rows.append(jnp.pad(flat, (0, _packed_rows(a.shape) * LANE - flat.shape[0])).reshape(-1, LANE))
    return jnp.concatenate(rows, axis=0)


def _unpack(buf, shapes):
    out, at = [], 0
    for s in shapes:
        rows = _packed_rows(s)
        out.append(buf[at:at + rows].reshape(-1)[:int(np.prod(s))].reshape(s))
        at += rows
    return out


SHARD_AXIS = {"w_in": 2, "w_uq": 2, "w_ukv": 2, "lru_conv_w": 2, "w_o": 1, "w_up": 2, "ffn_conv_w": 2, "w_down": 1,
              "w_ple_gate": 1, "w_ple_proj": 2}
SHARDED = [k for k in WEIGHTS if k in SHARD_AXIS]
REPLICATED = [k for k in WEIGHTS if k not in SHARD_AXIS]
ELEMENTWISE_F32 = ("lru_conv_w", "ffn_conv_w")
N_SHARDS = 4
BF16_TILE_ROWS = 16


HBM_SPEC = pl.BlockSpec(memory_space=pl.ANY)
SEM_SPEC = pl.BlockSpec(memory_space=pltpu.SEMAPHORE)
SPLIT_EFFECT = pltpu.SideEffectType.DATAFLOW_SIDE_EFFECTING
CHIP_FLIPS = ((1, 0), (0, 1), (1, 1))
N_DEVICES = 8
SUM_BLOCK_BYTES = 4 * 2 ** 20


def _device_index():
    return 4 * lax.axis_index("x") + 2 * lax.axis_index("y") + lax.axis_index("c")


def _when(cond, fn):
    if cond is None:
        fn()
    else:
        pl.when(cond)(fn)


class _Exchange:
    def __init__(self, name, plan, srcs, land_shapes, n_send, n_recv):
        self.name, self.plan, self.srcs, self.n = name, plan, list(srcs), len(srcs)
        self.land_shapes, self.n_send, self.n_recv = land_shapes, n_send, n_recv

    def run(self):
        n = self.n

        def body(*refs):
            sends, arrivals = self.plan(refs[:n], refs[n:2 * n], refs[2 * n], refs[2 * n + 1])
            for cond, cp in sends:
                _when(cond, cp.start)
            for cond, cp in arrivals:
                _when(cond, cp.wait_recv)
            for cond, cp in sends:
                _when(cond, cp.wait_send)

        return pl.pallas_call(
            body, name=self.name, out_shape=self.land_shapes, in_specs=[HBM_SPEC] * n, out_specs=[HBM_SPEC] * n,
            scratch_shapes=[pltpu.SemaphoreType.DMA((self.n_send,)), pltpu.SemaphoreType.DMA((self.n_recv,))])(*self.srcs)

    def start(self, after=None):
        n = self.n
        lands = [lax.empty(s.shape, s.dtype) for s in self.land_shapes]
        extra = [] if after is None else [after]

        def body(*refs):
            ins, lands_in = refs[:n], refs[n:2 * n]
            send_sems, recv_sems, token = refs[2 * n + len(extra)], refs[2 * n + len(extra) + 1], refs[-1]
            sends, _ = self.plan(ins, lands_in, send_sems, recv_sems)
            for cond, cp in sends:
                _when(cond, cp.start)
            token[...] = jnp.zeros_like(token)

        hbm = [pltpu.with_memory_space_constraint(a, pltpu.HBM) for a in self.srcs + lands]
        res = pl.pallas_call(
            body, name=self.name + "_start",
            out_shape=(pltpu.SemaphoreType.DMA((self.n_send,)), pltpu.SemaphoreType.DMA((self.n_recv,)),
                       *[pltpu.HBM(a.shape, a.dtype) for a in hbm], jax.ShapeDtypeStruct((SUBLANE, LANE), f32)),
            in_specs=[HBM_SPEC] * (2 * n + len(extra)),
            out_specs=(SEM_SPEC, SEM_SPEC, *[HBM_SPEC] * (2 * n), pl.BlockSpec(memory_space=pltpu.VMEM)),
            input_output_aliases={i: 2 + i for i in range(2 * n)},
            compiler_params=pltpu.CompilerParams(has_side_effects=SPLIT_EFFECT))(*hbm, *extra)
        self.sems, self.thru, token = res[:2], res[2:2 + 2 * n], res[-1]
        return token[0, 0]

    def finish(self, after):
        n = self.n

        def body(*refs):
            ins, lands_in, send_sems, recv_sems = refs[:n], refs[n:2 * n], refs[2 * n], refs[2 * n + 1]
            sends, arrivals = self.plan(ins, lands_in, send_sems, recv_sems)
            for cond, cp in arrivals:
                _when(cond, cp.wait_recv)
            for cond, cp in sends:
                _when(cond, cp.wait_send)

        res = pl.pallas_call(
            body, name=self.name + "_finish", out_shape=tuple(pltpu.HBM(a.shape, a.dtype) for a in self.thru),
            in_specs=[HBM_SPEC] * (2 * n) + [SEM_SPEC, SEM_SPEC, HBM_SPEC], out_specs=tuple([HBM_SPEC] * (2 * n)),
            input_output_aliases={i: i for i in range(2 * n)},
            compiler_params=pltpu.CompilerParams(has_side_effects=SPLIT_EFFECT))(*self.thru, *self.sems, after)
        return list(res[n:])


def _gather_exchange(name, shards):
    def plan(ins, lands, send_sems, recv_sems):
        x, y, c = (lax.axis_index(a) for a in MESH_AXES)
        copies = []
        for i in range(len(ins)):
            for k, (fx, fy) in enumerate(CHIP_FLIPS):
                peer = (1 - x if fx else x, 1 - y if fy else y, c)
                copies.append((None, pltpu.make_async_remote_copy(
                    src_ref=ins[i], dst_ref=lands[i].at[2 * x + y], send_sem=send_sems.at[3 * i + k],
                    recv_sem=recv_sems.at[3 * i + k], device_id=peer, device_id_type=pl.DeviceIdType.MESH)))
        return copies, copies

    n = len(shards)
    return _Exchange(name, plan, shards, [jax.ShapeDtypeStruct((N_SHARDS,) + s.shape, s.dtype) for s in shards], 3 * n, 3 * n)


def _scatter_exchange(name, layer, chunks):
    def plan(ins, lands, send_sems, recv_sems):
        x, y, c = (lax.axis_index(a) for a in MESH_AXES)
        me = _device_index()
        sends, arrivals = [], []
        for i in range(len(ins)):
            for j in range(N_SHARDS):
                target = (j // 2, j % 2, layer)
                remote = jnp.logical_not((x == target[0]) & (y == target[1]) & (c == layer))
                sends.append((remote, pltpu.make_async_remote_copy(
                    src_ref=ins[i].at[j], dst_ref=lands[i].at[me], send_sem=send_sems.at[N_SHARDS * i + j],
                    recv_sem=recv_sems.at[N_DEVICES * i + me], device_id=target, device_id_type=pl.DeviceIdType.MESH)))
            for s in range(N_DEVICES):
                arrivals.append(((c == layer) & (me != s), pltpu.make_async_remote_copy(
                    src_ref=ins[i].at[0], dst_ref=lands[i].at[s], send_sem=send_sems.at[0],
                    recv_sem=recv_sems.at[N_DEVICES * i + s], device_id=(x, y, c), device_id_type=pl.DeviceIdType.MESH)))
        return sends, arrivals

    n = len(chunks)
    lands = [jax.ShapeDtypeStruct((N_DEVICES,) + ch.shape[1:], ch.dtype) for ch in chunks]
    return _Exchange(name, plan, chunks, lands, N_SHARDS * n, N_DEVICES * n)


def _sum_contributions(name, got, mine):
    _, a, b = got.shape
    ta = _row_tile(a, max(SUBLANE, SUM_BLOCK_BYTES // (N_DEVICES * b * got.dtype.itemsize) // SUBLANE * SUBLANE))

    def kern(got_ref, mine_ref, o_ref):
        me = _device_index()
        acc = jnp.zeros(o_ref.shape, f32)
        for s in range(N_DEVICES):
            acc = acc + jnp.where(me == s, mine_ref[...].astype(f32), got_ref[s].astype(f32))
        o_ref[...] = acc

    return pl.pallas_call(
        kern, name=name, grid=(a // ta,),
        in_specs=[pl.BlockSpec((N_DEVICES, ta, b), lambda i: (0, i, 0)), pl.BlockSpec((ta, b), lambda i: (i, 0))],
        out_specs=pl.BlockSpec((ta, b), lambda i: (i, 0)), out_shape=jax.ShapeDtypeStruct((a, b), f32),
        compiler_params=pltpu.CompilerParams(dimension_semantics=("parallel",)))(got, mine)


def _swap_layers(name, sums):
    n = len(sums[0])

    def body(*refs):
        srcs = (refs[:n], refs[n:2 * n])
        outs, (send_sems, recv_sems) = refs[2 * n:3 * n], refs[3 * n:]
        x, y, c = (lax.axis_index(a) for a in MESH_AXES)
        for i in range(n):
            for layer in range(DEPTH):
                cp = pltpu.make_async_remote_copy(
                    src_ref=srcs[layer][i], dst_ref=outs[i], send_sem=send_sems.at[i], recv_sem=recv_sems.at[i],
                    device_id=(x, y, 1 - c), device_id_type=pl.DeviceIdType.MESH)
                pl.when(c == layer)(cp.start)
        for i in range(n):
            pltpu.make_async_remote_copy(
                src_ref=srcs[0][i], dst_ref=outs[i], send_sem=send_sems.at[i], recv_sem=recv_sems.at[i],
                device_id=(x, y, 1 - c), device_id_type=pl.DeviceIdType.MESH).wait()

    return pl.pallas_call(
        body, name=name, out_shape=[jax.ShapeDtypeStruct(s.shape, s.dtype) for s in sums[0]],
        in_specs=[HBM_SPEC] * (2 * n), out_specs=[HBM_SPEC] * n,
        scratch_shapes=[pltpu.SemaphoreType.DMA((n,)), pltpu.SemaphoreType.DMA((n,))])(*sums[0], *sums[1])


def _stack_shards(g, axis):
    if axis == 1:
        return g.reshape(N_SHARDS, g.shape[0] // N_SHARDS, g.shape[1])
    return g.reshape(g.shape[0], N_SHARDS, g.shape[1] // N_SHARDS).transpose(1, 0, 2)


def _join_shards(s, axis):
    if axis == 1:
        return s.reshape(-1, s.shape[2])
    return s.transpose(1, 0, 2).reshape(s.shape[1], -1)


def _layer_shards(w, l, names):
    return [w[k][l] if k in ELEMENTWISE_F32 else w[k][l].astype(bf16) for k in names]


def _full_weights(names, sent, got):
    j = 2 * lax.axis_index("x") + lax.axis_index("y")
    return {k: _join_shards(lax.dynamic_update_slice(g, own[None], (j, 0, 0)), SHARD_AXIS[k])
            for k, own, g in zip(names, sent, got)}


def _grad_chunks(grads, names):
    return [_stack_shards(grads[k], SHARD_AXIS[k]).astype(bf16) for k in names]


def _sum_group(l, names, got, chunks):
    j = 2 * lax.axis_index("x") + lax.axis_index("y")
    return {k: _sum_contributions(f"sum_l{l}_{k}", g, lax.dynamic_index_in_dim(ch, j, 0, keepdims=False))
            for k, g, ch in zip(names, got, chunks)}


def _both_layers(sums):
    c = lax.axis_index("c")
    other = _swap_layers("swap_layers", sums)
    return {k: jnp.stack([jnp.where(c == 0, sums[0][i], other[i]), jnp.where(c == 0, other[i], sums[1][i])])
            for i, k in enumerate(SHARDED)}


def kernel(x, p, positions, g_mix, w_in, g_qc, w_uq, g_kvc, w_ukv, b_f, lru_conv_w, lru_conv_b, w_r, b_r, w_i, b_i, lru_lambda, g_out, w_o, g_ffn, w_up, ffn_conv_w, ffn_conv_b, w_down, g_ple, w_ple_gate, w_ple_proj, g_final, loss_target, m_g_mix, m_w_in, m_g_qc, m_w_uq, m_g_kvc, m_w_ukv, m_b_f, m_lru_conv_w, m_lru_conv_b, m_w_r, m_b_r, m_w_i, m_b_i, m_lru_lambda, m_g_out, m_w_o, m_g_ffn, m_w_up, m_ffn_conv_w, m_ffn_conv_b, m_w_down, m_g_ple, m_w_ple_gate, m_w_ple_proj, m_g_final, v_g_mix, v_w_in, v_g_qc, v_w_uq, v_g_kvc, v_w_ukv, v_b_f, v_lru_conv_w, v_lru_conv_b, v_w_r, v_b_r, v_w_i, v_b_i, v_lru_lambda, v_g_out, v_w_o, v_g_ffn, v_w_up, v_ffn_conv_w, v_ffn_conv_b, v_w_down, v_g_ple, v_w_ple_gate, v_w_ple_proj, v_g_final):
    given = locals()
    w = {k: given[k] for k in WEIGHTS}
    m = {k: given["m_" + k] for k in WEIGHTS}
    v = {k: given["v_" + k] for k in WEIGHTS}

    parts = {"mix": MIX_PART, "ffn": FFN_PART}
    groups = [(l, part) for l in range(DEPTH) for part in ("mix", "ffn")]
    sent = {g: _layer_shards(w, g[0], parts[g[1]]) for g in groups}
    first = _gather_exchange("gather_l0_mix", sent[groups[0]]).run()
    ahead = {g: _gather_exchange(f"gather_l{g[0]}_{g[1]}", sent[g]) for g in groups[1:]}
    pos = positions[0].astype(f32).reshape(-1, 1)
    for ex in ahead.values():
        pos = pos + ex.start(after=first[0])
    behind, layer_grads, chunks = {}, [{} for _ in range(DEPTH)], {}

    def weights_of(l, part, after):
        g = (l, part)
        full = _full_weights(parts[part], sent[g], first if g == groups[0] else ahead[g].finish(after=after))
        if part == "mix":
            full.update({k: w[k][l] for k in LAYER_WEIGHTS if k in REPLICATED})
        return full

    def grads_to(l, part, grads):
        g = (l, part)
        layer_grads[l].update(grads)
        chunks[g] = _grad_chunks(grads, parts[part])
        if g == groups[0]:
            return jnp.zeros((), f32)
        behind[g] = _scatter_exchange(f"scatter_l{l}_{part}", l, chunks[g])
        return behind[g].start()

    loss, dx, dg_final = _local_step(x[0], p[:, 0], pos, loss_target[0], w["g_final"], weights_of, grads_to)

    sums = [{} for _ in range(DEPTH)]
    for g in groups:
        got = _scatter_exchange("scatter_l0_mix", 0, chunks[g]).run() if g == groups[0] else behind[g].finish(after=dx)
        sums[g[0]].update(_sum_group(g[0], parts[g[1]], got, chunks[g]))
    g_sharded = _both_layers([[sums[l][k] for k in SHARDED] for l in range(DEPTH)])
    big = [[], [], [], []]
    for k in SHARDED:
        shape = w[k].shape
        flat = [t.reshape(-1, shape[-1]) for t in (w[k], g_sharded[k], m[k], v[k])]
        for kind, res in enumerate((flat[1],) + tuple(_adamw("adamw_" + k, *flat))):
            big[kind].append(res.reshape(shape))

    grads = {k: jnp.stack([layer_grads[l][k] for l in range(DEPTH)]) for k in LAYER_WEIGHTS if k in REPLICATED}
    grads["g_final"] = dg_final
    rep_shapes = [w[k].shape for k in REPLICATED] + [(1,)]
    contrib = _pack([grads[k] for k in REPLICATED] + [loss.reshape(1)])
    g_rep = _sum_slabs("sum_replicated", _exchange("gather_replicated", contrib, MESH_AXES, scatter=False))
    zero = jnp.zeros((1,), f32)
    w_rep, m_rep, v_rep = (_pack([t[k] for k in REPLICATED] + [zero]) for t in (w, m, v))
    rep = [_unpack(b, rep_shapes) for b in (g_rep,) + tuple(_adamw("adamw_replicated", w_rep, g_rep, m_rep, v_rep))]

    outs = []
    for kind in range(4):
        by_name = dict(zip(SHARDED, big[kind]))
        by_name.update(zip(REPLICATED, rep[kind][:-1]))
        outs.append([by_name[k] for k in WEIGHTS])
    total_loss = rep[0][-1][0]
    return (total_loss, dx.reshape(x.shape), *outs[0], *outs[1], *outs[2], *outs[3])
```

```python
import functools
import math

import numpy as np
import jax
import jax.numpy as jnp
from jax import lax
from jax.experimental import pallas as pl
from jax.experimental.pallas import tpu as pltpu

f32, bf16 = jnp.float32, jnp.bfloat16

D_MODEL = 1024
PLE_DIM = 256
MLA_HEADS, MLA_NOPE, MLA_ROPE, MLA_V = 4, 64, 32, 64
MLA_Q_RANK, MLA_KV_RANK = 192, 128
FOX_HEADS, FOX_HEAD_DIM = 4, 64
LRU_WIDTH, LRU_BLOCKS, LRU_BLOCK, LRU_CONV, LRU_C = 512, 8, 64, 4, 8.0
D_FF, FFN_CONV = 2816, 3
ROPE_THETA = 10000.0
EPS = 1e-6
DEPTH = 2
ADAM_LR, ADAM_B1, ADAM_B2, ADAM_EPS, ADAM_WD, ADAM_STEP = 0.001, 0.9, 0.999, 1e-08, 0.01, 10

LANE = 128
SUBLANE = 8
HEADS = 4

Z_FQ, Z_FK, Z_FV, Z_LX, Z_LG, Z_QC, Z_KVC, Z_KR, Z_FL, Z_W = 0, 512, 1024, 1536, 2048, 2560, 2816, 2944, 3072, 3200
QC_W = 256
ROPE_AT = 64


def _head_pad_map(n_heads, width):
    m = -np.ones(n_heads * LANE, np.int64)
    for h in range(n_heads):
        m[h * LANE:h * LANE + width] = h * width + np.arange(width)
    return m


def _z_map():
    m = -np.ones(Z_W, np.int64)
    o_qc, o_kvc, o_kr = 0, MLA_Q_RANK, MLA_Q_RANK + MLA_KV_RANK
    o_fq = o_kr + MLA_ROPE
    o_fk, o_fv = o_fq + 256, o_fq + 512
    o_fl = o_fv + 256
    o_lx = o_fl + FOX_HEADS
    o_lg = o_lx + LRU_WIDTH
    m[Z_FQ:Z_FQ + 512] = np.where(_head_pad_map(4, 64) >= 0, _head_pad_map(4, 64) + o_fq, -1)
    m[Z_FK:Z_FK + 512] = np.where(_head_pad_map(4, 64) >= 0, _head_pad_map(4, 64) + o_fk, -1)
    m[Z_FV:Z_FV + 512] = np.where(_head_pad_map(4, 64) >= 0, _head_pad_map(4, 64) + o_fv, -1)
    m[Z_LX:Z_LX + 512] = o_lx + np.arange(512)
    m[Z_LG:Z_LG + 512] = o_lg + np.arange(512)
    m[Z_QC:Z_QC + MLA_Q_RANK] = o_qc + np.arange(MLA_Q_RANK)
    m[Z_KVC:Z_KVC + MLA_KV_RANK] = o_kvc + np.arange(MLA_KV_RANK)
    m[Z_KR + ROPE_AT:Z_KR + ROPE_AT + MLA_ROPE] = o_kr + np.arange(MLA_ROPE)
    m[Z_FL:Z_FL + FOX_HEADS] = o_fl + np.arange(FOX_HEADS)
    return m


def _ukv_map():
    m = -np.ones(2 * HEADS * LANE, np.int64)
    for h in range(HEADS):
        m[h * LANE:h * LANE + MLA_NOPE] = h * (MLA_NOPE + MLA_V) + np.arange(MLA_NOPE)
        m[HEADS * LANE + h * LANE:HEADS * LANE + h * LANE + MLA_V] = h * (MLA_NOPE + MLA_V) + MLA_NOPE + np.arange(MLA_V)
    return m


def _omix_map():
    return np.concatenate([_head_pad_map(4, 64), np.where(_head_pad_map(4, 64) >= 0, _head_pad_map(4, 64) + 256, -1),
                           512 + np.arange(512)])


def _pad_to(m, n):
    return np.concatenate([m, -np.ones(n - m.shape[0], np.int64)])


def _take_pad(a, m, axis):
    out = jnp.take(a, jnp.asarray(np.maximum(m, 0), jnp.int32), axis=axis)
    shape = [1] * a.ndim
    shape[axis] = m.shape[0]
    return out * jnp.asarray((m >= 0).reshape(shape), a.dtype)


def _take_inv(a, m, axis):
    n = int(m.max()) + 1
    inv = np.zeros(n, np.int64)
    inv[m[m >= 0]] = np.nonzero(m >= 0)[0]
    return jnp.take(a, jnp.asarray(inv, jnp.int32), axis=axis)


Z_MAP = _z_map()
UQ_COL_MAP = _head_pad_map(HEADS, MLA_NOPE + MLA_ROPE)
UQ_ROW_MAP = _pad_to(np.arange(MLA_Q_RANK), QC_W)
UKV_MAP = _ukv_map()
OMIX_MAP = _omix_map()
OMIX_W = 1536


def _rope_tables(width, at):
    half = MLA_ROPE // 2
    inv = ROPE_THETA ** (-np.arange(half, dtype=np.float32) / half)
    freq = np.zeros((1, width), np.float32)
    m1 = np.zeros((1, width), np.float32)
    m2 = np.zeros((1, width), np.float32)
    for h in range(width // LANE):
        b = h * LANE + at
        freq[0, b:b + half] = inv
        freq[0, b + half:b + 2 * half] = inv
        m1[0, b:b + half] = 1.0
        m2[0, b + half:b + 2 * half] = 1.0
    return freq, m1, m2


def _view(r):
    return r if isinstance(r, tuple) else (r, r.shape[1], 0)


def _blk(dim, cap):
    if dim <= cap:
        return dim
    for b in range(cap, LANE - 1, -LANE):
        if dim % b == 0:
            return b
    return dim


@functools.partial(jax.custom_vjp, nondiff_argnums=(1, 2))
def _roll(x, shift, axis):
    return pltpu.roll(x, shift, axis)


def _roll_fwd(x, shift, axis):
    return pltpu.roll(x, shift, axis), None


def _roll_bwd(shift, axis, _, g):
    return (pltpu.roll(g, g.shape[axis] - shift, axis),)


_roll.defvjp(_roll_fwd, _roll_bwd)


def _rowwise(name, fn, rows, pars, outs, tb=256):
    rows = [_view(r) for r in rows]
    n = rows[0][0].shape[0]
    tb = min(tb, n)
    nr, npar = len(rows), len(pars)

    def kern(*refs):
        r = [refs[k][...].astype(f32) for k in range(nr)]
        p = [refs[nr + k][...] for k in range(npar)]
        res = fn(*r, *p)
        for o_ref, o in zip(refs[nr + npar:], res):
            o_ref[...] = o.astype(o_ref.dtype)

    in_specs = [pl.BlockSpec((tb, w), lambda i, j=idx: (i, j)) for (_, w, idx) in rows]
    in_specs += [pl.BlockSpec(p.shape, lambda i: (0, 0)) for p in pars]
    out_specs = [pl.BlockSpec((tb, w), lambda i: (i, 0)) for (w, _) in outs]
    out_shape = [jax.ShapeDtypeStruct((n, w), dt) for (w, dt) in outs]
    return pl.pallas_call(kern, name=name, grid=(n // tb,), in_specs=in_specs, out_specs=out_specs, out_shape=out_shape,
                          compiler_params=pltpu.CompilerParams(dimension_semantics=("parallel",)))(*[r[0] for r in rows], *pars)


def _rowwise_bwd(name, fn, rows, pars, cts, ndiff, adds=None, tb=256, dts=None):
    rows = [_view(r) for r in rows]
    dts = dts or [f32] * ndiff
    adds = adds or {}
    add_keys = sorted(adds)
    n = rows[0][0].shape[0]
    tb = min(tb, n)
    nr, npar, nct, nadd = len(rows), len(pars), len(cts), len(add_keys)

    def kern(*refs):
        i = pl.program_id(0)
        r = [refs[k][...].astype(f32) for k in range(nr)]
        p = [refs[nr + k][...] for k in range(npar)]
        ct = [refs[nr + npar + k][...].astype(f32) for k in range(nct)]
        ad = {key: refs[nr + npar + nct + k][...] for k, key in enumerate(add_keys)}
        o_refs = refs[nr + npar + nct + nadd:]

        def g(*d):
            return tuple(fn(*d[:ndiff], *r[ndiff:], *d[ndiff:]))

        _, vjp = jax.vjp(g, *r[:ndiff], *p)
        grads = vjp(tuple(ct))
        for k in range(ndiff):
            gk = grads[k]
            if k in ad:
                gk = gk + ad[k]
            o_refs[k][...] = gk.astype(o_refs[k].dtype)

        @pl.when(i == 0)
        def _():
            for k in range(npar):
                o_refs[ndiff + k][...] = jnp.zeros_like(o_refs[ndiff + k])

        for k in range(npar):
            o_refs[ndiff + k][...] += grads[ndiff + k]

    in_specs = [pl.BlockSpec((tb, w), lambda i, j=idx: (i, j)) for (_, w, idx) in rows]
    in_specs += [pl.BlockSpec(p.shape, lambda i: (0, 0)) for p in pars]
    in_specs += [pl.BlockSpec((tb, c.shape[1]), lambda i: (i, 0)) for c in cts]
    in_specs += [pl.BlockSpec((tb, adds[k].shape[1]), lambda i: (i, 0)) for k in add_keys]
    out_specs = [pl.BlockSpec((tb, rows[k][1]), lambda i: (i, 0)) for k in range(ndiff)]
    out_specs += [pl.BlockSpec(p.shape, lambda i: (0, 0)) for p in pars]
    out_shape = [jax.ShapeDtypeStruct((n, rows[k][1]), dts[k]) for k in range(ndiff)]
    out_shape += [jax.ShapeDtypeStruct(p.shape, f32) for p in pars]
    res = pl.pallas_call(kern, name=name, grid=(n // tb,), in_specs=in_specs, out_specs=out_specs, out_shape=out_shape,
                         compiler_params=pltpu.CompilerParams(dimension_semantics=("arbitrary",)))(
        *[r[0] for r in rows], *pars, *cts, *[adds[k] for k in add_keys])
    return res[:ndiff], res[ndiff:]


_DOT_DIMS = {"nn": ((1,), (0,)), "nt": ((1,), (1,)), "tn": ((0,), (0,))}

MM_VMEM_BUDGET = 36 * 2 ** 20
MM_MAX_TM = 1408
MM_STEP, MM_RESULT, MM_XPOSE, MM_CAST = 700.0, 7.5e-4, 9e-4, 1e-3


def _tile_candidates(dim):
    c = [d for d in range(LANE, dim + 1, LANE) if dim % d == 0]
    return c or [dim]


@functools.lru_cache(maxsize=None)
def _mm_tiles(mode, m, n, k, a_bytes, b_bytes, o_bytes):
    best, best_cost = None, None
    for tm in _tile_candidates(m):
        if tm > MM_MAX_TM:
            continue
        for tn in _tile_candidates(n):
            for tk in _tile_candidates(k):
                vmem = 2 * (tm * tk * a_bytes + tk * tn * b_bytes + tm * tn * o_bytes) + 4 * tm * tn * (2 if tk < k else 1)
                vmem += (2 * tm * tk if a_bytes > 2 else 0) + (2 * tk * tn if b_bytes > 2 else 0)
                if vmem > MM_VMEM_BUDGET:
                    continue
                steps = (m // tm) * (n // tn) * (k // tk)
                cost = steps * MM_STEP + m * n * (k // tk) * MM_RESULT
                if mode == "tn":
                    cost += m * k * (n // tn) * MM_XPOSE
                cost += (m * k * (n // tn) * MM_CAST if a_bytes > 2 else 0) + (k * n * (m // tm) * MM_CAST if b_bytes > 2 else 0)
                if best is None or cost < best_cost:
                    best, best_cost = (tm, tn, tk), cost
    return best


def _mm(name, a, b, mode="nn", out_dtype=f32, res=None):
    if mode == "nn":
        (m, k), (_, n) = a.shape, b.shape
    elif mode == "nt":
        (m, k), (n, _) = a.shape, b.shape
    else:
        (k, m), (_, n) = a.shape, b.shape
    has_res = res is not None
    tm, tn, tk = _mm_tiles(mode, m, n, k, a.dtype.itemsize, b.dtype.itemsize,
                           jnp.dtype(out_dtype).itemsize + (res.dtype.itemsize if has_res else 0))
    nk = k // tk
    dims = (_DOT_DIMS[mode], ((), ()))

    def kern(*refs):
        a_ref, b_ref = refs[0], refs[1]
        o_ref, acc_ref = refs[-2], refs[-1]
        kk = pl.program_id(2)
        part = lax.dot_general(a_ref[...].astype(bf16), b_ref[...].astype(bf16), dims, preferred_element_type=f32)

        def finish(out):
            if has_res:
                out = out + refs[2][...]
            o_ref[...] = out.astype(o_ref.dtype)

        if nk == 1:
            finish(part)
            return

        @pl.when(kk == 0)
        def _():
            acc_ref[...] = part

        @pl.when(jnp.logical_and(kk > 0, kk < nk - 1))
        def _():
            acc_ref[...] += part

        @pl.when(kk == nk - 1)
        def _():
            finish(acc_ref[...] + part)

    if mode == "tn":
        a_spec = pl.BlockSpec((tk, tm), lambda i, j, kk: (kk, i))
    else:
        a_spec = pl.BlockSpec((tm, tk), lambda i, j, kk: (i, kk))
    if mode == "nt":
        b_spec = pl.BlockSpec((tn, tk), lambda i, j, kk: (j, kk))
    else:
        b_spec = pl.BlockSpec((tk, tn), lambda i, j, kk: (kk, j))
    in_specs = [a_spec, b_spec]
    args = [a, b]
    if has_res:
        in_specs.append(pl.BlockSpec((tm, tn), lambda i, j, kk: (i, j)))
        args.append(res)
    return pl.pallas_call(
        kern, name=name, grid=(m // tm, n // tn, nk), in_specs=in_specs,
        out_specs=pl.BlockSpec((tm, tn), lambda i, j, kk: (i, j)),
        out_shape=jax.ShapeDtypeStruct((m, n), out_dtype),
        scratch_shapes=[pltpu.VMEM((tm, tn) if nk > 1 else (SUBLANE, LANE), f32)],
        compiler_params=pltpu.CompilerParams(dimension_semantics=("parallel", "parallel", "arbitrary")))(*args)


ATT_TQ, ATT_TK = 512, 512


def _att_tiles(s_len):
    tk = min(ATT_TK, s_len)
    return min(ATT_TQ, tk), tk


def _fold_scale(scale):
    return (scale, 1.0) if math.frexp(scale)[0] == 0.5 else (1.0, scale)


def _query_rows(x):
    s_len = x.shape[1]
    tq = _att_tiles(s_len)[0]
    return x.reshape(HEADS, s_len // tq, 1, tq)


def _scores_t(kb, q_t, s_mul, ck, diag_offset, tq, tk):
    s = jnp.dot(kb, q_t, preferred_element_type=f32)
    if s_mul != 1.0:
        s = s * s_mul
    if ck is not None:
        s = s - ck
    if diag_offset is None:
        return s
    key = lax.broadcasted_iota(jnp.int32, (tk, tq), 0)
    query = lax.broadcasted_iota(jnp.int32, (tk, tq), 1) + diag_offset
    return jnp.where(key <= query, s, -jnp.inf)


def _scores(qb, kb, s_mul, ck, diagonal, t):
    s = lax.dot_general(qb, kb, (_DOT_DIMS["nt"], ((), ())), preferred_element_type=f32)
    if s_mul != 1.0:
        s = s * s_mul
    if ck is not None:
        s = s - ck
    if not diagonal:
        return s
    row = lax.broadcasted_iota(jnp.int32, (t, t), 0)
    col = lax.broadcasted_iota(jnp.int32, (t, t), 1)
    return jnp.where(col <= row, s, -jnp.inf)


def _attn_fwd(name, q, k, v, scale, c_row=None):
    (qa, qo), (ka, ko), (va, vo) = q, k, v
    s_len = qa.shape[0]
    t = _att_tiles(s_len)[1]
    nt = s_len // t
    decay = c_row is not None
    q_mul, s_mul = _fold_scale(scale)

    def kern(*refs):
        q_ref, k_ref, v_ref = refs[:3]
        o_ref, lse_ref = refs[-2:]
        i = pl.program_id(1)
        qb = (q_ref[...] * q_mul).astype(bf16)

        def step(j, carry, diagonal):
            m, l, acc = carry
            rows = pl.ds(pl.multiple_of(j * t, t), t)
            kb = k_ref[rows, :].astype(bf16)
            vb = v_ref[rows, :].astype(bf16)
            s = _scores(qb, kb, s_mul, refs[3][j] if decay else None, diagonal, t)
            m_new = jnp.maximum(m, jnp.max(s, axis=1, keepdims=True))
            alpha = jnp.exp(m - m_new)
            p = jnp.exp(s - m_new)
            l = alpha * l + jnp.sum(p, axis=1, keepdims=True)
            acc = alpha * acc + jnp.dot(p.astype(bf16), vb, preferred_element_type=f32)
            return m_new, l, acc

        init = (jnp.full((t, 1), -jnp.inf, f32), jnp.zeros((t, 1), f32), jnp.zeros((t, LANE), f32))
        m, l, acc = step(i, lax.fori_loop(0, i, lambda j, c: step(j, c, False), init), True)
        o_ref[...] = acc / l
        lse_ref[...] = m + jnp.log(l)

    in_specs = [pl.BlockSpec((t, LANE), lambda h, i: (i, qo + h)),
                pl.BlockSpec((s_len, LANE), lambda h, i: (0, ko + h)),
                pl.BlockSpec((s_len, LANE), lambda h, i: (0, vo + h))]
    args = [qa, ka, va]
    if decay:
        in_specs.append(pl.BlockSpec((None, nt, 1, t), lambda h, i: (h, 0, 0, 0)))
        args.append(c_row)
    return pl.pallas_call(
        kern, name=name, grid=(HEADS, nt), in_specs=in_specs,
        out_specs=[pl.BlockSpec((t, LANE), lambda h, i: (i, h)), pl.BlockSpec((None, t, 1), lambda h, i: (h, i, 0))],
        out_shape=[jax.ShapeDtypeStruct((s_len, HEADS * LANE), f32), jax.ShapeDtypeStruct((HEADS, s_len, 1), f32)],
        compiler_params=pltpu.CompilerParams(dimension_semantics=("parallel", "arbitrary")))(*args)


def _attn_dq(name, q, k, v, o, do, lse, scale, c_row=None):
    (qa, qo), (ka, ko), (va, vo) = q, k, v
    s_len = qa.shape[0]
    t = _att_tiles(s_len)[1]
    nt = s_len // t
    decay = c_row is not None
    q_mul, s_mul = _fold_scale(scale)

    def kern(*refs):
        q_ref, k_ref, v_ref, o_ref, do_ref, lse_ref = refs[:6]
        dq_ref, delta_ref, drow_ref = refs[-3:]
        i = pl.program_id(1)
        qb = (q_ref[...] * q_mul).astype(bf16)
        dob = do_ref[...]
        delta = jnp.sum(dob * o_ref[...], axis=1, keepdims=True)
        dob = dob.astype(bf16)
        lse = lse_ref[...]

        def step(j, carry, diagonal):
            dq, drow = carry
            rows = pl.ds(pl.multiple_of(j * t, t), t)
            kb = k_ref[rows, :].astype(bf16)
            vb = v_ref[rows, :].astype(bf16)
            s = _scores(qb, kb, s_mul, refs[6][j] if decay else None, diagonal, t)
            p = jnp.exp(s - lse)
            dp = lax.dot_general(dob, vb, (_DOT_DIMS["nt"], ((), ())), preferred_element_type=f32)
            ds = p * (dp - delta)
            return dq + jnp.dot(ds.astype(bf16), kb, preferred_element_type=f32), drow + jnp.sum(ds, axis=1, keepdims=True)

        init = (jnp.zeros((t, LANE), f32), jnp.zeros((t, 1), f32))
        dq, drow = step(i, lax.fori_loop(0, i, lambda j, c: step(j, c, False), init), True)
        dq_ref[...] = dq * scale
        delta_ref[...] = delta
        drow_ref[...] = drow

    in_specs = [pl.BlockSpec((t, LANE), lambda h, i: (i, qo + h)),
                pl.BlockSpec((s_len, LANE), lambda h, i: (0, ko + h)),
                pl.BlockSpec((s_len, LANE), lambda h, i: (0, vo + h)),
                pl.BlockSpec((t, LANE), lambda h, i: (i, h)),
                pl.BlockSpec((t, LANE), lambda h, i: (i, h)),
                pl.BlockSpec((None, t, 1), lambda h, i: (h, i, 0))]
    args = [qa, ka, va, o, do, lse]
    if decay:
        in_specs.append(pl.BlockSpec((None, nt, 1, t), lambda h, i: (h, 0, 0, 0)))
        args.append(c_row)
    col = pl.BlockSpec((None, t, 1), lambda h, i: (h, i, 0))
    return pl.pallas_call(
        kern, name=name, grid=(HEADS, nt), in_specs=in_specs,
        out_specs=[pl.BlockSpec((t, LANE), lambda h, i: (i, h)), col, col],
        out_shape=[jax.ShapeDtypeStruct((s_len, HEADS * LANE), f32), jax.ShapeDtypeStruct((HEADS, s_len, 1), f32),
                   jax.ShapeDtypeStruct((HEADS, s_len, 1), f32)],
        compiler_params=pltpu.CompilerParams(dimension_semantics=("parallel", "arbitrary")))(*args)


def _attn_dkv(name, q, k, v, do, lse, delta, scale, c_col=None):
    (qa, qo), (ka, ko), (va, vo) = q, k, v
    s_len = qa.shape[0]
    tq, tk = _att_tiles(s_len)
    nq, per = s_len // tq, tk // tq
    decay = c_col is not None
    q_mul, s_mul = _fold_scale(scale)

    def kern(*refs):
        q_ref, k_ref, v_ref, do_ref, lse_ref, delta_ref = refs[:6]
        j = pl.program_id(1)
        kb = k_ref[...].astype(bf16)
        vb = v_ref[...].astype(bf16)
        ck = refs[6][...] if decay else None

        def step(i, carry, diagonal):
            dk, dv, dsum = carry
            for d in range(per):
                tile = i * per + d
                rows = pl.ds(pl.multiple_of(tile * tq, tq), tq)
                qb = (q_ref[rows, :] * q_mul).astype(bf16)
                dob = do_ref[rows, :].astype(bf16)
                s = _scores_t(kb, qb.T, s_mul, ck, d * tq if diagonal else None, tq, tk)
                p = jnp.exp(s - lse_ref[tile])
                dv = dv + jnp.dot(p.astype(bf16), dob, preferred_element_type=f32)
                dp = jnp.dot(vb, dob.T, preferred_element_type=f32)
                ds = p * (dp - delta_ref[tile])
                dk = dk + jnp.dot(ds.astype(bf16), qb, preferred_element_type=f32)
                if decay:
                    dsum = dsum + ds
            return dk, dv, dsum

        init = (jnp.zeros((tk, LANE), f32), jnp.zeros((tk, LANE), f32), jnp.zeros((tk, tq), f32))
        dk, dv, dsum = lax.fori_loop(j + 1, s_len // tk, lambda i, c: step(i, c, False), step(j, init, True))
        if decay:
            dk_ref, dv_ref, dc_ref = refs[-3:]
            dc_ref[...] = -jnp.sum(dsum, axis=1, keepdims=True)
        else:
            dk_ref, dv_ref = refs[-2:]
        dk_ref[...] = dk * s_mul
        dv_ref[...] = dv

    stat = pl.BlockSpec((None, nq, 1, tq), lambda h, j: (h, 0, 0, 0))
    in_specs = [pl.BlockSpec((s_len, LANE), lambda h, j: (0, qo + h)),
                pl.BlockSpec((tk, LANE), lambda h, j: (j, ko + h)),
                pl.BlockSpec((tk, LANE), lambda h, j: (j, vo + h)),
                pl.BlockSpec((s_len, LANE), lambda h, j: (0, h)), stat, stat]
    args = [qa, ka, va, do, lse, delta]
    out_specs = [pl.BlockSpec((tk, LANE), lambda h, j: (j, h)), pl.BlockSpec((tk, LANE), lambda h, j: (j, h))]
    out_shape = [jax.ShapeDtypeStruct((s_len, HEADS * LANE), f32), jax.ShapeDtypeStruct((s_len, HEADS * LANE), f32)]
    if decay:
        in_specs.append(pl.BlockSpec((None, tk, 1), lambda h, j: (h, j, 0)))
        args.append(c_col)
        out_specs.append(pl.BlockSpec((None, tk, 1), lambda h, j: (h, j, 0)))
        out_shape.append(jax.ShapeDtypeStruct((HEADS, s_len, 1), f32))
    return pl.pallas_call(
        kern, name=name, grid=(HEADS, s_len // tk), in_specs=in_specs, out_specs=out_specs, out_shape=out_shape,
        compiler_params=pltpu.CompilerParams(dimension_semantics=("parallel", "arbitrary")))(*args)


CONV_TS, CONV_CB = 1024, 256


def _conv_fwd(name, x, w, b, taps):
    xa, width, xidx = _view(x)
    s_len = xa.shape[0]
    ts, cb = min(CONV_TS, s_len), CONV_CB
    xo = xidx * width // cb

    def kern(x_ref, halo_ref, w_ref, b_ref, o_ref):
        i = pl.program_id(1)
        xb = x_ref[...]
        halo = jnp.where(i == 0, 0.0, halo_ref[...])
        xx = jnp.concatenate([halo, xb], axis=0)
        out = b_ref[...] + w_ref[taps - 1:taps, :] * xb
        for k in range(taps - 1):
            out = out + w_ref[k:k + 1, :] * pltpu.roll(xx, taps - 1 - k, 0)[SUBLANE:]
        o_ref[...] = out

    return pl.pallas_call(
        kern, name=name, grid=(width // cb, s_len // ts),
        in_specs=[pl.BlockSpec((ts, cb), lambda j, i: (i, xo + j)),
                  pl.BlockSpec((SUBLANE, cb), lambda j, i: (jnp.maximum(i * (ts // SUBLANE) - 1, 0), xo + j)),
                  pl.BlockSpec((taps, cb), lambda j, i: (0, j)),
                  pl.BlockSpec((1, cb), lambda j, i: (0, j))],
        out_specs=pl.BlockSpec((ts, cb), lambda j, i: (i, j)),
        out_shape=jax.ShapeDtypeStruct((s_len, width), f32),
        compiler_params=pltpu.CompilerParams(dimension_semantics=("parallel", "parallel")))(xa, xa, w, b)


def _conv_bwd(name, x, dout, w, taps, dout2=None, dx_dtype=f32):
    xa, width, xidx = _view(x)
    s_len = xa.shape[0]
    ts, cb = min(CONV_TS, s_len), CONV_CB
    xo = xidx * width // cb
    n_i = s_len // ts
    two = dout2 is not None

    def kern(*refs):
        x_ref, halo_ref, w_ref = refs[:3]
        dx_ref, dw_ref, db_ref = refs[-3:]
        i = pl.program_id(1)
        if two:
            d = refs[3][...] + refs[5][...]
            dn = refs[4][...] + refs[6][...]
        else:
            d, dn = refs[3][...], refs[4][...]
        dn = jnp.where(i == n_i - 1, 0.0, dn)
        xb = x_ref[...]
        halo = jnp.where(i == 0, 0.0, halo_ref[...])
        xx = jnp.concatenate([halo, xb], axis=0)
        dd = jnp.concatenate([d, dn], axis=0)

        @pl.when(i == 0)
        def _():
            dw_ref[...] = jnp.zeros_like(dw_ref)
            db_ref[...] = jnp.zeros_like(db_ref)

        dx = w_ref[taps - 1:taps, :] * d
        dw_ref[taps - 1:taps, :] += jnp.sum(d * xb, axis=0, keepdims=True)
        for k in range(taps - 1):
            sh = taps - 1 - k
            dx = dx + w_ref[k:k + 1, :] * pltpu.roll(dd, ts + SUBLANE - sh, 0)[:ts]
            dw_ref[k:k + 1, :] += jnp.sum(d * pltpu.roll(xx, sh, 0)[SUBLANE:], axis=0, keepdims=True)
        dx_ref[...] = dx.astype(dx_ref.dtype)
        db_ref[...] += jnp.sum(d, axis=0, keepdims=True)

    d_spec = pl.BlockSpec((ts, cb), lambda j, i: (i, j))
    dn_spec = pl.BlockSpec((SUBLANE, cb), lambda j, i: (jnp.minimum((i + 1) * (ts // SUBLANE), s_len // SUBLANE - 1), j))
    in_specs = [pl.BlockSpec((ts, cb), lambda j, i: (i, xo + j)),
                pl.BlockSpec((SUBLANE, cb), lambda j, i: (jnp.maximum(i * (ts // SUBLANE) - 1, 0), xo + j)),
                pl.BlockSpec((taps, cb), lambda j, i: (0, j)), d_spec, dn_spec]
    args = [xa, xa, w, dout, dout]
    if two:
        in_specs += [d_spec, dn_spec]
        args += [dout2, dout2]
    return pl.pallas_call(
        kern, name=name, grid=(width // cb, n_i), in_specs=in_specs,
        out_specs=[pl.BlockSpec((ts, cb), lambda j, i: (i, j)), pl.BlockSpec((taps, cb), lambda j, i: (0, j)),
                   pl.BlockSpec((1, cb), lambda j, i: (0, j))],
        out_shape=[jax.ShapeDtypeStruct((s_len, width), dx_dtype), jax.ShapeDtypeStruct((taps, width), f32),
                   jax.ShapeDtypeStruct((1, width), f32)],
        compiler_params=pltpu.CompilerParams(dimension_semantics=("parallel", "arbitrary")))(*args)


def _conv_rows(xx, w_ref, b_ref, taps):
    out = b_ref[...] + w_ref[taps - 1:taps, :] * xx[SUBLANE:]
    for k in range(taps - 1):
        out = out + w_ref[k:k + 1, :] * pltpu.roll(xx, taps - 1 - k, 0)[SUBLANE:]
    return out


def _ffn_act_fwd(name, up, w, b):
    s_len = up.shape[0]
    ts, cb = min(CONV_TS, s_len), CONV_CB
    nf = D_FF // cb

    def kern(g_ref, gp_ref, v_ref, vp_ref, wg_ref, wv_ref, bg_ref, bv_ref, o_ref):
        first = pl.program_id(1) == 0
        ug = _conv_rows(jnp.concatenate([jnp.where(first, 0.0, gp_ref[...]), g_ref[...]], axis=0), wg_ref, bg_ref, FFN_CONV)
        uv = _conv_rows(jnp.concatenate([jnp.where(first, 0.0, vp_ref[...]), v_ref[...]], axis=0), wv_ref, bv_ref, FFN_CONV)
        o_ref[...] = (jax.nn.silu(ug) * uv).astype(o_ref.dtype)

    def half(off):
        return [pl.BlockSpec((ts, cb), lambda j, i: (i, off + j)),
                pl.BlockSpec((SUBLANE, cb), lambda j, i: (jnp.maximum(i * (ts // SUBLANE) - 1, 0), off + j))]

    def par(rows, off):
        return pl.BlockSpec((rows, cb), lambda j, i: (0, off + j))

    return pl.pallas_call(
        kern, name=name, grid=(nf, s_len // ts),
        in_specs=half(0) + half(nf) + [par(FFN_CONV, 0), par(FFN_CONV, nf), par(1, 0), par(1, nf)],
        out_specs=pl.BlockSpec((ts, cb), lambda j, i: (i, j)),
        out_shape=jax.ShapeDtypeStruct((s_len, D_FF), bf16),
        compiler_params=pltpu.CompilerParams(dimension_semantics=("parallel", "parallel")))(up, up, up, up, w, w, b, b)


def _ffn_act_bwd(name, up, dact, w, b):
    s_len = up.shape[0]
    ts, cb = min(CONV_TS, s_len), CONV_CB
    nf = D_FF // cb
    n_i = s_len // ts
    taps = FFN_CONV

    def kern(g_ref, gp_ref, gn_ref, v_ref, vp_ref, vn_ref, d_ref, dn_ref, wg_ref, wv_ref, bg_ref, bv_ref,
             dg_ref, dv_ref, dwg_ref, dwv_ref, dbg_ref, dbv_ref):
        i = pl.program_id(1)
        first, last = i == 0, i == n_i - 1

        def extended(x_ref, p_ref, n_ref):
            return jnp.concatenate([jnp.where(first, 0.0, p_ref[...]), x_ref[...], jnp.where(last, 0.0, n_ref[...])], axis=0)

        gx, vx = extended(g_ref, gp_ref, gn_ref), extended(v_ref, vp_ref, vn_ref)
        ug, uv = _conv_rows(gx, wg_ref, bg_ref, taps), _conv_rows(vx, wv_ref, bv_ref, taps)
        dd = jnp.concatenate([d_ref[...], jnp.where(last, 0.0, dn_ref[...])], axis=0)
        sg = jax.nn.sigmoid(ug)
        dug = dd * uv * (sg * (1.0 + ug * (1.0 - sg)))
        duv = dd * (ug * sg)

        @pl.when(first)
        def _():
            for ref in (dwg_ref, dwv_ref, dbg_ref, dbv_ref):
                ref[...] = jnp.zeros_like(ref)

        def transposed(du, xx, w_ref, dx_ref, dw_ref, db_ref):
            d = du[:ts]
            dx = w_ref[taps - 1:taps, :] * d
            dw_ref[taps - 1:taps, :] += jnp.sum(d * xx[SUBLANE:SUBLANE + ts], axis=0, keepdims=True)
            for k in range(taps - 1):
                sh = taps - 1 - k
                dx = dx + w_ref[k:k + 1, :] * pltpu.roll(du, ts + SUBLANE - sh, 0)[:ts]
                dw_ref[k:k + 1, :] += jnp.sum(d * pltpu.roll(xx, sh, 0)[SUBLANE:SUBLANE + ts], axis=0, keepdims=True)
            dx_ref[...] = dx.astype(dx_ref.dtype)
            db_ref[...] += jnp.sum(d, axis=0, keepdims=True)

        transposed(dug, gx, wg_ref, dg_ref, dwg_ref, dbg_ref)
        transposed(duv, vx, wv_ref, dv_ref, dwv_ref, dbv_ref)

    blocks = s_len // SUBLANE

    def half(off):
        return [pl.BlockSpec((ts, cb), lambda j, i: (i, off + j)),
                pl.BlockSpec((SUBLANE, cb), lambda j, i: (jnp.maximum(i * (ts // SUBLANE) - 1, 0), off + j)),
                pl.BlockSpec((SUBLANE, cb), lambda j, i: (jnp.minimum((i + 1) * (ts // SUBLANE), blocks - 1), off + j))]

    def par(rows, off):
        return pl.BlockSpec((rows, cb), lambda j, i: (0, off + j))

    d_specs = [pl.BlockSpec((ts, cb), lambda j, i: (i, j)),
               pl.BlockSpec((SUBLANE, cb), lambda j, i: (jnp.minimum((i + 1) * (ts // SUBLANE), blocks - 1), j))]
    out_par = [pl.BlockSpec((r, cb), lambda j, i: (0, j)) for r in (taps, taps, 1, 1)]
    return pl.pallas_call(
        kern, name=name, grid=(nf, n_i),
        in_specs=half(0) + half(nf) + d_specs + [par(taps, 0), par(taps, nf), par(1, 0), par(1, nf)],
        out_specs=[pl.BlockSpec((ts, cb), lambda j, i: (i, j))] * 2 + out_par,
        out_shape=[jax.ShapeDtypeStruct((s_len, D_FF), bf16)] * 2 + [jax.ShapeDtypeStruct((taps, D_FF), f32)] * 2
        + [jax.ShapeDtypeStruct((1, D_FF), f32)] * 2,
        compiler_params=pltpu.CompilerParams(dimension_semantics=("parallel", "arbitrary")))(
        up, up, up, up, up, up, dact, dact, w, w, b, b)


SCAN_ROWS = 128


def _block_scan(a, b, reverse):
    t = a.shape[0]
    row = lax.broadcasted_iota(jnp.int32, a.shape, 0)
    d = 1
    while d < t:
        keep = row < t - d if reverse else row >= d
        shift = t - d if reverse else d
        a_far = jnp.where(keep, pltpu.roll(a, shift, 0), 1.0)
        b_far = jnp.where(keep, pltpu.roll(b, shift, 0), 0.0)
        b = a * b_far + b
        a = a * a_far
        d *= 2
    return a, b


def _scan_fwd(name, a, b):
    s_len, width = a.shape
    t = min(SCAN_ROWS, s_len)

    def kern(a_ref, b_ref, h_ref):
        def block(k, carry):
            rows = pl.ds(pl.multiple_of(k * t, t), t)
            acc, h = _block_scan(a_ref[rows, :], b_ref[rows, :], False)
            h_ref[rows, :] = h + acc * carry
            return h_ref[pl.ds(k * t + t - 1, 1), :]

        lax.fori_loop(0, s_len // t, block, jnp.zeros((1, LANE), f32))

    spec = pl.BlockSpec((s_len, LANE), lambda j: (0, j))
    return pl.pallas_call(
        kern, name=name, grid=(width // LANE,), in_specs=[spec, spec], out_specs=spec,
        out_shape=jax.ShapeDtypeStruct((s_len, width), f32),
        compiler_params=pltpu.CompilerParams(dimension_semantics=("parallel",)))(a, b)


def _scan_bwd(name, a_next, h_prev, dh):
    s_len, width = dh.shape
    t = min(SCAN_ROWS, s_len)
    n_blocks = s_len // t

    def kern(an_ref, hp_ref, dh_ref, da_ref, db_ref):
        def block(kk, carry):
            k = n_blocks - 1 - kk
            rows = pl.ds(pl.multiple_of(k * t, t), t)
            acc, g = _block_scan(an_ref[rows, :], dh_ref[rows, :], True)
            g = g + acc * carry
            db_ref[rows, :] = g
            da_ref[rows, :] = g * hp_ref[rows, :]
            return db_ref[pl.ds(k * t, 1), :]

        lax.fori_loop(0, n_blocks, block, jnp.zeros((1, LANE), f32))

    spec = pl.BlockSpec((s_len, LANE), lambda j: (0, j))
    return pl.pallas_call(
        kern, name=name, grid=(width // LANE,), in_specs=[spec, spec, spec], out_specs=[spec, spec],
        out_shape=[jax.ShapeDtypeStruct((s_len, width), f32)] * 2,
        compiler_params=pltpu.CompilerParams(dimension_semantics=("parallel",)))(a_next, h_prev, dh)


def _lane_cumsum(x, reverse):
    n = x.shape[1]
    lane = lax.broadcasted_iota(jnp.int32, x.shape, 1)
    sh = 1
    while sh < n:
        if reverse:
            x = x + jnp.where(lane < n - sh, pltpu.roll(x, n - sh, 1), 0.0)
        else:
            x = x + jnp.where(lane >= sh, pltpu.roll(x, sh, 1), 0.0)
        sh *= 2
    return x


def _decay_fwd(name, fl_t, b8):
    def kern(f_ref, b_ref, c_ref):
        c_ref[...] = _lane_cumsum(jax.nn.log_sigmoid(f_ref[...] + b_ref[...]), False)

    return pl.pallas_call(kern, name=name, out_shape=jax.ShapeDtypeStruct(fl_t.shape, f32))(fl_t, b8)


def _decay_bwd(name, fl_t, b8, dc_key, dc_query):
    def kern(f_ref, b_ref, dck_ref, dcq_ref, df_ref, db_ref):
        dlogf = _lane_cumsum(dck_ref[...] + dcq_ref[...], True)
        df = dlogf * jax.nn.sigmoid(-(f_ref[...] + b_ref[...]))
        df_ref[...] = df
        db_ref[...] = jnp.sum(df, axis=1, keepdims=True)

    return pl.pallas_call(kern, name=name, out_shape=[jax.ShapeDtypeStruct(fl_t.shape, f32),
                                                      jax.ShapeDtypeStruct((SUBLANE, 1), f32)])(fl_t, b8, dc_key, dc_query)


def _rms(x, g, n):
    return x * lax.rsqrt(jnp.sum(x * x, axis=-1, keepdims=True) * (1.0 / n) + EPS) * g


def _loss_head(name, h, target, g, tb=256):
    n, d = h.shape
    tb = min(tb, n)

    def kern(h_ref, t_ref, g_ref, loss_ref, dh_ref, dg_ref):
        i = pl.program_id(0)
        tgt = t_ref[...]

        def f(hv, gv):
            err = _rms(hv, gv, d) - tgt
            return 0.5 * jnp.sum(jnp.sum(err * err, axis=-1, keepdims=True) * (1.0 / d), axis=0, keepdims=True)

        val, vjp = jax.vjp(f, h_ref[...], g_ref[...])
        dh, dg = vjp(jnp.ones((1, 1), f32))
        dh_ref[...] = dh

        @pl.when(i == 0)
        def _():
            loss_ref[...] = jnp.zeros_like(loss_ref)
            dg_ref[...] = jnp.zeros_like(dg_ref)

        loss_ref[...] += val
        dg_ref[...] += dg

    return pl.pallas_call(
        kern, name=name, grid=(n // tb,),
        in_specs=[pl.BlockSpec((tb, d), lambda i: (i, 0)), pl.BlockSpec((tb, d), lambda i: (i, 0)),
                  pl.BlockSpec((1, d), lambda i: (0, 0))],
        out_specs=[pl.BlockSpec((1, 1), lambda i: (0, 0)), pl.BlockSpec((tb, d), lambda i: (i, 0)),
                   pl.BlockSpec((1, d), lambda i: (0, 0))],
        out_shape=[jax.ShapeDtypeStruct((1, 1), f32), jax.ShapeDtypeStruct((n, d), f32), jax.ShapeDtypeStruct((1, d), f32)],
        compiler_params=pltpu.CompilerParams(dimension_semantics=("arbitrary",)))(h, target, g)


def _f_norm(x, g):
    return (_rms(x, g, D_MODEL),)


def _f_latent(qc, kvc, gq, gkv):
    return _rms(qc, gq, MLA_Q_RANK), _rms(kvc, gkv, MLA_KV_RANK)


def _f_rope_table(pos, freq, m1, m2):
    ang = pos * freq
    sin = jnp.sin(ang)
    return jnp.cos(ang), -sin * m1, sin * m2


def _rope(x, cos, s_up, s_down):
    w = x.shape[1]
    return x * cos + _roll(x, w - MLA_ROPE // 2, 1) * s_up + _roll(x, MLA_ROPE // 2, 1) * s_down


def _f_mla_prep(q, kpart, kr, cos, s_up, s_down):
    def heads(t):
        return jnp.concatenate([t] * HEADS, axis=1)

    kr = _rope(kr, cos, s_up, s_down)
    return _rope(q, heads(cos), heads(s_up), heads(s_down)), kpart + heads(kr)


def _f_lru_gate(gates, xc, b_r, b_i, lam):
    r = jax.nn.sigmoid(gates[:, :LRU_WIDTH] + b_r)
    i = jax.nn.sigmoid(gates[:, LRU_WIDTH:] + b_i)
    log_a = -LRU_C * r * jax.nn.softplus(-lam)
    mult = jnp.sqrt(-jnp.tanh(log_a) * (1.0 + jnp.exp(2.0 * log_a)))
    return jnp.exp(log_a), mult * (i * xc)


def _f_merge(o_mla, o_fox, hs, lg, g):
    o_lru = hs * jax.nn.gelu(lg)
    return (jnp.concatenate([_rms(o_mla, g[:, :512], HEADS * MLA_V), _rms(o_fox, g[:, 512:1024], HEADS * FOX_HEAD_DIM),
                             _rms(o_lru, g[:, 1024:], LRU_WIDTH)], axis=1),)


def _f_ffn_gate(u):
    return (jax.nn.silu(u[:, :D_FF]) * u[:, D_FF:],)


def _f_ple(h, gpre, pp):
    return (h + jax.nn.sigmoid(gpre) * pp,)


MIX_PART = ["w_in", "w_uq", "w_ukv", "lru_conv_w", "w_o"]
FFN_PART = ["w_up", "ffn_conv_w", "w_down", "w_ple_gate", "w_ple_proj"]


def _prep_mix_weights(w):
    eye = jnp.eye(LRU_BLOCKS, dtype=f32)

    def block_diag(m):
        return (eye[:, None, :, None] * m[:, :, None, :]).reshape(LRU_WIDTH, LRU_WIDTH)

    return dict(
        w_in=_take_pad(w["w_in"], Z_MAP, 1),
        w_uq=_take_pad(_take_pad(w["w_uq"], UQ_COL_MAP, 1), UQ_ROW_MAP, 0),
        w_ukv=_take_pad(w["w_ukv"], UKV_MAP, 1),
        w_ri=jnp.concatenate([block_diag(w["w_r"]), block_diag(w["w_i"])], axis=1).astype(bf16),
        w_o=_take_pad(w["w_o"], OMIX_MAP, 0),
        g_mix=w["g_mix"].reshape(1, -1), g_ffn=w["g_ffn"].reshape(1, -1), g_ple=w["g_ple"].reshape(1, -1),
        g_qc=_take_pad(w["g_qc"], UQ_ROW_MAP, 0).reshape(1, -1), g_kvc=w["g_kvc"].reshape(1, -1),
        g_out=_take_pad(w["g_out"], OMIX_MAP, 0).reshape(1, -1),
        b_f8=_take_pad(w["b_f"], _pad_to(np.arange(FOX_HEADS), SUBLANE), 0).reshape(SUBLANE, 1),
        lru_conv_w=w["lru_conv_w"], lru_conv_b=w["lru_conv_b"].reshape(1, -1),
        b_r=w["b_r"].reshape(1, -1), b_i=w["b_i"].reshape(1, -1), lam=w["lru_lambda"].reshape(1, -1),
        ffn_conv_b=w["ffn_conv_b"].reshape(1, -1),
    )


def _prep_ffn_weights(w):
    return dict(w_up=w["w_up"], w_up_g=w["w_up"][:, :D_FF], w_up_v=w["w_up"][:, D_FF:], ffn_conv_w=w["ffn_conv_w"],
                w_down=w["w_down"], w_ple_gate=w["w_ple_gate"], w_ple_proj=w["w_ple_proj"])


def _rope_rows(pos):
    consts = [jnp.asarray(t) for t in _rope_tables(LANE, ROPE_AT)]
    return _rowwise("rope_table", _f_rope_table, [pos], consts, [(LANE, f32)] * 3)


def _key_decay(c_t, s_len):
    t = _att_tiles(s_len)[1]
    return c_t[:HEADS].reshape(HEADS, s_len // t, 1, t), c_t[:HEADS].reshape(HEADS, s_len, 1)


def _layer_fwd(l, h0, p_l, rope, weights_of):
    s_len = h0.shape[0]
    n = f"l{l}_"
    w = _prep_mix_weights(weights_of("mix", h0))
    xn, = _rowwise(n + "norm_mix", _f_norm, [h0], [w["g_mix"]], [(D_MODEL, bf16)])
    z = _mm(n + "in_proj", xn, w["w_in"])
    zq = (z, QC_W, Z_QC // QC_W)
    zkv = (z, LANE, Z_KVC // LANE)
    zkr = (z, LANE, Z_KR // LANE)
    zlx = (z, LRU_WIDTH, Z_LX // LRU_WIDTH)
    zlg = (z, LRU_WIDTH, Z_LG // LRU_WIDTH)
    qcn, kvn = _rowwise(n + "latent_norm", _f_latent, [zq, zkv], [w["g_qc"], w["g_kvc"]], [(QC_W, bf16), (LANE, bf16)])
    q = _mm(n + "uq", qcn, w["w_uq"])
    kv = _mm(n + "ukv", kvn, w["w_ukv"])
    kpart = (kv, HEADS * LANE, 0)
    qr, kk = _rowwise(n + "mla_prep", _f_mla_prep, [q, kpart, zkr, *rope], [],
                      [(HEADS * LANE, bf16), (HEADS * LANE, bf16)])
    mla_scale = (MLA_NOPE + MLA_ROPE) ** -0.5
    o_mla, lse_m = _attn_fwd(n + "mla_fwd", (qr, 0), (kk, 0), (kv, HEADS), mla_scale)
    fl_t = z[:, Z_FL:Z_FL + SUBLANE].T
    c_t = _decay_fwd(n + "decay", fl_t, w["b_f8"])
    c_row, c_col = _key_decay(c_t, s_len)
    fox_scale = FOX_HEAD_DIM ** -0.5
    o_fox, lse_f = _attn_fwd(n + "fox_fwd", (z, Z_FQ // LANE), (z, Z_FK // LANE), (z, Z_FV // LANE), fox_scale, c_row)
    xc = _conv_fwd(n + "lru_conv", zlx, w["lru_conv_w"], w["lru_conv_b"], LRU_CONV)
    gates = _mm(n + "lru_gates", xc, w["w_ri"])
    a, bx = _rowwise(n + "lru_gate", _f_lru_gate, [gates, xc], [w["b_r"], w["b_i"], w["lam"]],
                     [(LRU_WIDTH, f32), (LRU_WIDTH, f32)])
    hs = _scan_fwd(n + "lru_scan", a, bx)
    ocat, = _rowwise(n + "merge", _f_merge, [o_mla, o_fox, hs, zlg], [w["g_out"]], [(OMIX_W, bf16)])
    h1 = _mm(n + "out_proj", ocat, w["w_o"], res=h0)
    w.update(_prep_ffn_weights(weights_of("ffn", h1)))
    xn2, = _rowwise(n + "norm_ffn", _f_norm, [h1], [w["g_ffn"]], [(D_MODEL, bf16)])
    up = _mm(n + "up_proj", xn2, w["w_up"])
    act = _ffn_act_fwd(n + "ffn_act", up, w["ffn_conv_w"], w["ffn_conv_b"])
    h2 = _mm(n + "down_proj", act, w["w_down"], res=h1)
    hn, = _rowwise(n + "norm_ple", _f_norm, [h2], [w["g_ple"]], [(D_MODEL, bf16)])
    gpre = _mm(n + "ple_gate", hn, w["w_ple_gate"])
    pp = _mm(n + "ple_proj", p_l, w["w_ple_proj"])
    h3, = _rowwise(n + "ple_mix", _f_ple, [h2, gpre, pp], [], [(D_MODEL, f32)])
    res = dict(h0=h0, xn=xn, z=z, qcn=qcn, kvn=kvn, q=q, kv=kv, qr=qr, kk=kk, o_mla=o_mla, lse_m=lse_m, fl_t=fl_t,
               c_row=c_row, c_col=c_col, o_fox=o_fox, lse_f=lse_f, xc=xc, gates=gates, a=a, hs=hs, ocat=ocat, h1=h1,
               xn2=xn2, up=up, act=act, h2=h2, hn=hn, gpre=gpre, pp=pp, p_l=p_l)
    return h3, res, w


def _layer_bwd(l, dh3, r, rope, w, token, grads_to):
    s_len = dh3.shape[0]
    n = f"l{l}_"
    g = {}
    w = dict(w, g_ple=w["g_ple"] + token)
    z = r["z"]
    zq = (z, QC_W, Z_QC // QC_W)
    zkv = (z, LANE, Z_KVC // LANE)
    zkr = (z, LANE, Z_KR // LANE)
    zlx = (z, LRU_WIDTH, Z_LX // LRU_WIDTH)
    zlg = (z, LRU_WIDTH, Z_LG // LRU_WIDTH)
    (dh2a, dgpre, dpp), _ = _rowwise_bwd(n + "ple_mix_b", _f_ple, [r["h2"], r["gpre"], r["pp"]], [], [dh3], 3,
                                         dts=[f32, bf16, bf16])
    g["w_ple_proj"] = _mm(n + "ple_proj_dw", r["p_l"], dpp, "tn", bf16)
    dhn = _mm(n + "ple_gate_dx", dgpre, w["w_ple_gate"], "nt")
    g["w_ple_gate"] = _mm(n + "ple_gate_dw", r["hn"], dgpre, "tn", bf16)
    (dh2,), (g["g_ple"],) = _rowwise_bwd(n + "norm_ple_b", _f_norm, [r["h2"]], [w["g_ple"]], [dhn], 1, adds={0: dh2a})
    dact = _mm(n + "down_dx", dh2, w["w_down"], "nt")
    g["w_down"] = _mm(n + "down_dw", r["act"], dh2, "tn", bf16)
    dup_g, dup_v, dcw_g, dcw_v, dcb_g, dcb_v = _ffn_act_bwd(n + "ffn_act_b", r["up"], dact, w["ffn_conv_w"], w["ffn_conv_b"])
    g["ffn_conv_w"] = jnp.concatenate([dcw_g, dcw_v], axis=1)
    g["ffn_conv_b"] = jnp.concatenate([dcb_g, dcb_v], axis=1)
    dxn2 = _mm(n + "up_dx_v", dup_v, w["w_up_v"], "nt", res=_mm(n + "up_dx_g", dup_g, w["w_up_g"], "nt"))
    g["w_up"] = jnp.concatenate([_mm(n + "up_dw_g", r["xn2"], dup_g, "tn", bf16),
                                 _mm(n + "up_dw_v", r["xn2"], dup_v, "tn", bf16)], axis=1)
    (dh1,), (g["g_ffn"],) = _rowwise_bwd(n + "norm_ffn_b", _f_norm, [r["h1"]], [w["g_ffn"]], [dxn2], 1, adds={0: dh2})
    token = grads_to("ffn", dict(w_up=g["w_up"], ffn_conv_w=g["ffn_conv_w"], w_down=g["w_down"],
                                 w_ple_gate=g["w_ple_gate"], w_ple_proj=g["w_ple_proj"]))
    w = dict(w, g_out=w["g_out"] + token)
    docat = _mm(n + "out_dx", dh1, w["w_o"], "nt")
    g["w_o"] = _mm(n + "out_dw", r["ocat"], dh1, "tn", bf16)
    (do_mla, do_fox, dhs, dlg), (g["g_out"],) = _rowwise_bwd(
        n + "merge_b", _f_merge, [r["o_mla"], r["o_fox"], r["hs"], zlg], [w["g_out"]], [docat], 4)
    a, hs = r["a"], r["hs"]
    a_next = jnp.concatenate([a[1:], jnp.zeros((1, LRU_WIDTH), f32)], axis=0)
    h_prev = jnp.concatenate([jnp.zeros((1, LRU_WIDTH), f32), hs[:-1]], axis=0)
    da, dbx = _scan_bwd(n + "lru_scan_b", a_next, h_prev, dhs)
    (dgates, dxc_a), (g["b_r"], g["b_i"], g["lam"]) = _rowwise_bwd(
        n + "lru_gate_b", _f_lru_gate, [r["gates"], r["xc"]], [w["b_r"], w["b_i"], w["lam"]], [da, dbx], 2,
        dts=[bf16, f32])
    dxc_b = _mm(n + "lru_gates_dx", dgates, w["w_ri"], "nt")
    g["w_ri"] = _mm(n + "lru_gates_dw", r["xc"], dgates, "tn")
    dlx, g["lru_conv_w"], g["lru_conv_b"] = _conv_bwd(n + "lru_conv_b", zlx, dxc_a, w["lru_conv_w"], LRU_CONV, dout2=dxc_b)
    fox_scale = FOX_HEAD_DIM ** -0.5
    fq, fk, fv = (z, Z_FQ // LANE), (z, Z_FK // LANE), (z, Z_FV // LANE)
    dfq, delta_f, dc_q = _attn_dq(n + "fox_dq", fq, fk, fv, r["o_fox"], do_fox, r["lse_f"], fox_scale, r["c_row"])
    dfk, dfv, dc_k = _attn_dkv(n + "fox_dkv", fq, fk, fv, do_fox, _query_rows(r["lse_f"]), _query_rows(delta_f), fox_scale,
                               r["c_col"])
    pad_rows = jnp.zeros((SUBLANE - HEADS, s_len), f32)
    dfl_t, g["b_f8"] = _decay_bwd(n + "decay_b", r["fl_t"], w["b_f8"],
                                  jnp.concatenate([dc_k.reshape(HEADS, s_len), pad_rows], axis=0),
                                  jnp.concatenate([dc_q.reshape(HEADS, s_len), pad_rows], axis=0))
    dfl = jnp.pad(dfl_t.T, ((0, 0), (0, LANE - SUBLANE)))
    mla_scale = (MLA_NOPE + MLA_ROPE) ** -0.5
    qr, kk, kv = (r["qr"], 0), (r["kk"], 0), (r["kv"], HEADS)
    dqr, delta_m, _ = _attn_dq(n + "mla_dq", qr, kk, kv, r["o_mla"], do_mla, r["lse_m"], mla_scale)
    dkk, dv_m = _attn_dkv(n + "mla_dkv", qr, kk, kv, do_mla, _query_rows(r["lse_m"]), _query_rows(delta_m), mla_scale)
    (dq, dkpart, dkr), _ = _rowwise_bwd(n + "mla_prep_b", _f_mla_prep, [r["q"], (r["kv"], HEADS * LANE, 0), zkr, *rope],
                                        [], [dqr, dkk], 3, dts=[bf16, bf16, f32])
    dkv = jnp.concatenate([dkpart, dv_m.astype(bf16)], axis=1)
    dkvn = _mm(n + "ukv_dx", dkv, w["w_ukv"], "nt")
    g["w_ukv"] = _mm(n + "ukv_dw", r["kvn"], dkv, "tn", bf16)
    dqcn = _mm(n + "uq_dx", dq, w["w_uq"], "nt")
    g["w_uq"] = _mm(n + "uq_dw", r["qcn"], dq, "tn", bf16)
    (dqc, dkvc), (g["g_qc"], g["g_kvc"]) = _rowwise_bwd(n + "latent_norm_b", _f_latent, [zq, zkv],
                                                        [w["g_qc"], w["g_kvc"]], [dqcn, dkvn], 2)
    dz = jnp.concatenate([t.astype(bf16) for t in (dfq, dfk, dfv, dlx, dlg, dqc, dkvc, dkr, dfl)], axis=1)
    dxn = _mm(n + "in_dx", dz, w["w_in"], "nt")
    g["w_in"] = _mm(n + "in_dw", r["xn"], dz, "tn", bf16)
    (dh0,), (g["g_mix"],) = _rowwise_bwd(n + "norm_mix_b", _f_norm, [r["h0"]], [w["g_mix"]], [dxn], 1, adds={0: dh1})
    return dh0, grads_to("mix", _unpad_mix_grads(g))


def _unpad_mix_grads(g):
    d_ri = g["w_ri"]
    idx = jnp.arange(LRU_BLOCKS)

    def diag_blocks(m):
        return m.reshape(LRU_BLOCKS, LRU_BLOCK, LRU_BLOCKS, LRU_BLOCK)[idx, :, idx, :]

    return dict(
        g_mix=g["g_mix"][0], w_in=_take_inv(g["w_in"], Z_MAP, 1), g_qc=g["g_qc"][0, :MLA_Q_RANK],
        w_uq=_take_inv(g["w_uq"][:MLA_Q_RANK], UQ_COL_MAP, 1), g_kvc=g["g_kvc"][0],
        w_ukv=_take_inv(g["w_ukv"], UKV_MAP, 1), b_f=g["b_f8"][:FOX_HEADS, 0],
        lru_conv_w=g["lru_conv_w"], lru_conv_b=g["lru_conv_b"][0],
        w_r=diag_blocks(d_ri[:, :LRU_WIDTH]), b_r=g["b_r"][0], w_i=diag_blocks(d_ri[:, LRU_WIDTH:]), b_i=g["b_i"][0],
        lru_lambda=g["lam"][0], g_out=_take_inv(g["g_out"][0], OMIX_MAP, 0), w_o=_take_inv(g["w_o"], OMIX_MAP, 0),
        g_ffn=g["g_ffn"][0], ffn_conv_b=g["ffn_conv_b"][0], g_ple=g["g_ple"][0],
    )


LAYER_WEIGHTS = ["g_mix", "w_in", "g_qc", "w_uq", "g_kvc", "w_ukv", "b_f", "lru_conv_w", "lru_conv_b", "w_r", "b_r", "w_i",
                 "b_i", "lru_lambda", "g_out", "w_o", "g_ffn", "w_up", "ffn_conv_w", "ffn_conv_b", "w_down", "g_ple",
                 "w_ple_gate", "w_ple_proj"]
WEIGHTS = LAYER_WEIGHTS + ["g_final"]


def _local_step(x, p, pos, target, g_final, weights_of, grads_to):
    h = x
    rope = _rope_rows(pos)
    ws, saved = [], []
    for l in range(DEPTH):
        h, r, w = _layer_fwd(l, h, p[l], rope, functools.partial(weights_of, l))
        ws.append(w)
        saved.append(r)
    loss, dh, dg_final = _loss_head("loss_head", h, target, g_final.reshape(1, -1))
    token = jnp.zeros((), f32)
    for l in reversed(range(DEPTH)):
        dh, token = _layer_bwd(l, dh, saved[l], rope, ws[l], token, functools.partial(grads_to, l))
    return loss[0, 0], dh, dg_final[0]


MESH_AXES = ("x", "y", "c")


def _exchange(name, src, axes, scatter, pieces=1):
    n = 2 ** len(axes)
    flips = [tuple((f >> (len(axes) - 1 - b)) & 1 for b in range(len(axes))) for f in range(1, n)]
    rows = src.shape[-2]
    piece_rows = rows // pieces
    assert piece_rows * pieces == rows

    def body(src_ref, out_ref, send_sems, recv_sems, local_sem):
        coords = {a: lax.axis_index(a) for a in MESH_AXES}

        def index_of(cd):
            idx = 0
            for a in axes:
                idx = idx * 2 + cd[a]
            return idx

        me = index_of(coords)
        local = pltpu.make_async_copy(src_ref.at[me] if scatter else src_ref, out_ref.at[me], local_sem)
        local.start()
        copies = []
        for k, f in enumerate(flips):
            peer = dict(coords)
            for a, bit in zip(axes, f):
                if bit:
                    peer[a] = 1 - coords[a]
            slab = src_ref.at[index_of(peer)] if scatter else src_ref
            for pc in range(pieces):
                span = pl.ds(pc * piece_rows, piece_rows)
                cp = pltpu.make_async_remote_copy(
                    src_ref=slab.at[span], dst_ref=out_ref.at[me, span],
                    send_sem=send_sems.at[k * pieces + pc], recv_sem=recv_sems.at[k * pieces + pc],
                    device_id=tuple(peer[a] for a in MESH_AXES), device_id_type=pl.DeviceIdType.MESH)
                cp.start()
                copies.append(cp)
        for cp in copies:
            cp.wait()
        local.wait()

    n_sems = (n - 1) * pieces
    return pl.pallas_call(
        body, name=name, out_shape=jax.ShapeDtypeStruct((n, rows, LANE), src.dtype),
        in_specs=[pl.BlockSpec(memory_space=pl.ANY)], out_specs=pl.BlockSpec(memory_space=pl.ANY),
        scratch_shapes=[pltpu.SemaphoreType.DMA((n_sems,)), pltpu.SemaphoreType.DMA((n_sems,)), pltpu.SemaphoreType.DMA])(src)


def _row_tile(rows, cap):
    if rows <= cap:
        return rows
    for t in range(cap, SUBLANE - 1, -SUBLANE):
        if rows % t == 0:
            return t
    return rows


def _sum_slabs(name, a):
    n, rows, _ = a.shape
    tr = _row_tile(rows, 512)

    def kern(a_ref, o_ref):
        acc = a_ref[0].astype(f32)
        for k in range(1, n):
            acc = acc + a_ref[k].astype(f32)
        o_ref[...] = acc

    return pl.pallas_call(
        kern, name=name, grid=(rows // tr,), in_specs=[pl.BlockSpec((n, tr, LANE), lambda i: (0, i, 0))],
        out_specs=pl.BlockSpec((tr, LANE), lambda i: (i, 0)), out_shape=jax.ShapeDtypeStruct((rows, LANE), f32),
        compiler_params=pltpu.CompilerParams(dimension_semantics=("parallel",)))(a)


ADAM_BLOCK_BYTES = 2 ** 20


def _adamw(name, w, g, m, v):
    rows, cols = w.shape
    tr = _row_tile(rows, max(SUBLANE, ADAM_BLOCK_BYTES // (4 * cols) // SUBLANE * SUBLANE))

    def kern(w_ref, g_ref, m_ref, v_ref, d_ref, nm_ref, nv_ref):
        gv = g_ref[...]
        nm = ADAM_B1 * m_ref[...] + (1.0 - ADAM_B1) * gv
        nv = ADAM_B2 * v_ref[...] + (1.0 - ADAM_B2) * (gv * gv)
        m_hat = nm / (1.0 - ADAM_B1 ** ADAM_STEP)
        v_hat = nv / (1.0 - ADAM_B2 ** ADAM_STEP)
        d_ref[...] = -ADAM_LR * (m_hat / (jnp.sqrt(v_hat) + ADAM_EPS) + ADAM_WD * w_ref[...])
        nm_ref[...] = nm
        nv_ref[...] = nv

    spec = pl.BlockSpec((tr, cols), lambda i: (i, 0))
    return pl.pallas_call(
        kern, name=name, grid=(rows // tr,), in_specs=[spec] * 4, out_specs=[spec] * 3,
        out_shape=[jax.ShapeDtypeStruct((rows, cols), f32)] * 3,
        compiler_params=pltpu.CompilerParams(dimension_semantics=("parallel",)))(w, g, m, v)


def _packed_rows(shape):
    return -(-int(np.prod(shape)) // (SUBLANE * LANE)) * SUBLANE


def _pack(arrays):
    rows = []
    for a in arrays:
        flat = a.reshape(-1)
        rows.append(jnp.pad(flat, (0, _packed_rows(a.shape) * LANE - flat.shape[0])).reshape(-1, LANE))
    return jnp.concatenate(rows, axis=0)


def _unpack(buf, shapes):
    out, at = [], 0
    for s in shapes:
        rows = _packed_rows(s)
        out.append(buf[at:at + rows].reshape(-1)[:int(np.prod(s))].reshape(s))
        at += rows
    return out


SHARD_AXIS = {"w_in": 2, "w_uq": 2, "w_ukv": 2, "lru_conv_w": 2, "w_o": 1, "w_up": 2, "ffn_conv_w": 2, "w_down": 1,
              "w_ple_gate": 1, "w_ple_proj": 2}
SHARDED = [k for k in WEIGHTS if k in SHARD_AXIS]
REPLICATED = [k for k in WEIGHTS if k not in SHARD_AXIS]
ELEMENTWISE_F32 = ("lru_conv_w", "ffn_conv_w")
N_SHARDS = 4
BF16_TILE_ROWS = 16


HBM_SPEC = pl.BlockSpec(memory_space=pl.ANY)
SEM_SPEC = pl.BlockSpec(memory_space=pltpu.SEMAPHORE)
SPLIT_EFFECT = pltpu.SideEffectType.DATAFLOW_SIDE_EFFECTING
CHIP_FLIPS = ((1, 0), (0, 1), (1, 1))
N_DEVICES = 8
SUM_BLOCK_BYTES = 4 * 2 ** 20


def _device_index():
    return 4 * lax.axis_index("x") + 2 * lax.axis_index("y") + lax.axis_index("c")


def _when(cond, fn):
    if cond is None:
        fn()
    else:
        pl.when(cond)(fn)


class _Exchange:
    def __init__(self, name, plan, srcs, land_shapes, n_send, n_recv):
        self.name, self.plan, self.srcs, self.n = name, plan, list(srcs), len(srcs)
        self.land_shapes, self.n_send, self.n_recv = land_shapes, n_send, n_recv

    def run(self):
        n = self.n

        def body(*refs):
            sends, arrivals = self.plan(refs[:n], refs[n:2 * n], refs[2 * n], refs[2 * n + 1])
            for cond, cp in sends:
                _when(cond, cp.start)
            for cond, cp in arrivals:
                _when(cond, cp.wait_recv)
            for cond, cp in sends:
                _when(cond, cp.wait_send)

        return pl.pallas_call(
            body, name=self.name, out_shape=self.land_shapes, in_specs=[HBM_SPEC] * n, out_specs=[HBM_SPEC] * n,
            scratch_shapes=[pltpu.SemaphoreType.DMA((self.n_send,)), pltpu.SemaphoreType.DMA((self.n_recv,))])(*self.srcs)

    def start(self, after=None):
        n = self.n
        lands = [lax.empty(s.shape, s.dtype) for s in self.land_shapes]
        extra = [] if after is None else [after]

        def body(*refs):
            ins, lands_in = refs[:n], refs[n:2 * n]
            send_sems, recv_sems, token = refs[2 * n + len(extra)], refs[2 * n + len(extra) + 1], refs[-1]
            sends, _ = self.plan(ins, lands_in, send_sems, recv_sems)
            for cond, cp in sends:
                _when(cond, cp.start)
            token[...] = jnp.zeros_like(token)

        hbm = [pltpu.with_memory_space_constraint(a, pltpu.HBM) for a in self.srcs + lands]
        res = pl.pallas_call(
            body, name=self.name + "_start",
            out_shape=(pltpu.SemaphoreType.DMA((self.n_send,)), pltpu.SemaphoreType.DMA((self.n_recv,)),
                       *[pltpu.HBM(a.shape, a.dtype) for a in hbm], jax.ShapeDtypeStruct((SUBLANE, LANE), f32)),
            in_specs=[HBM_SPEC] * (2 * n + len(extra)),
            out_specs=(SEM_SPEC, SEM_SPEC, *[HBM_SPEC] * (2 * n), pl.BlockSpec(memory_space=pltpu.VMEM)),
            input_output_aliases={i: 2 + i for i in range(2 * n)},
            compiler_params=pltpu.CompilerParams(has_side_effects=SPLIT_EFFECT))(*hbm, *extra)
        self.sems, self.thru, token = res[:2], res[2:2 + 2 * n], res[-1]
        return token[0, 0]

    def finish(self, after):
        n = self.n

        def body(*refs):
            ins, lands_in, send_sems, recv_sems = refs[:n], refs[n:2 * n], refs[2 * n], refs[2 * n + 1]
            sends, arrivals = self.plan(ins, lands_in, send_sems, recv_sems)
            for cond, cp in arrivals:
                _when(cond, cp.wait_recv)
            for cond, cp in sends:
                _when(cond, cp.wait_send)

        res = pl.pallas_call(
            body, name=self.name + "_finish", out_shape=tuple(pltpu.HBM(a.shape, a.dtype) for a in self.thru),
            in_specs=[HBM_SPEC] * (2 * n) + [SEM_SPEC, SEM_SPEC, HBM_SPEC], out_specs=tuple([HBM_SPEC] * (2 * n)),
            input_output_aliases={i: i for i in range(2 * n)},
            compiler_params=pltpu.CompilerParams(has_side_effects=SPLIT_EFFECT))(*self.thru, *self.sems, after)
        return list(res[n:])


def _gather_exchange(name, shards):
    def plan(ins, lands, send_sems, recv_sems):
        x, y, c = (lax.axis_index(a) for a in MESH_AXES)
        copies = []
        for i in range(len(ins)):
            for k, (fx, fy) in enumerate(CHIP_FLIPS):
                peer = (1 - x if fx else x, 1 - y if fy else y, c)
                copies.append((None, pltpu.make_async_remote_copy(
                    src_ref=ins[i], dst_ref=lands[i].at[2 * x + y], send_sem=send_sems.at[3 * i + k],
                    recv_sem=recv_sems.at[3 * i + k], device_id=peer, device_id_type=pl.DeviceIdType.MESH)))
        return copies, copies

    n = len(shards)
    return _Exchange(name, plan, shards, [jax.ShapeDtypeStruct((N_SHARDS,) + s.shape, s.dtype) for s in shards], 3 * n, 3 * n)


def _scatter_exchange(name, layer, chunks):
    def plan(ins, lands, send_sems, recv_sems):
        x, y, c = (lax.axis_index(a) for a in MESH_AXES)
        me = _device_index()
        sends, arrivals = [], []
        for i in range(len(ins)):
            for j in range(N_SHARDS):
                target = (j // 2, j % 2, layer)
                remote = jnp.logical_not((x == target[0]) & (y == target[1]) & (c == layer))
                sends.append((remote, pltpu.make_async_remote_copy(
                    src_ref=ins[i].at[j], dst_ref=lands[i].at[me], send_sem=send_sems.at[N_SHARDS * i + j],
                    recv_sem=recv_sems.at[N_DEVICES * i + me], device_id=target, device_id_type=pl.DeviceIdType.MESH)))
            for s in range(N_DEVICES):
                arrivals.append(((c == layer) & (me != s), pltpu.make_async_remote_copy(
                    src_ref=ins[i].at[0], dst_ref=lands[i].at[s], send_sem=send_sems.at[0],
                    recv_sem=recv_sems.at[N_DEVICES * i + s], device_id=(x, y, c), device_id_type=pl.DeviceIdType.MESH)))
        return sends, arrivals

    n = len(chunks)
    lands = [jax.ShapeDtypeStruct((N_DEVICES,) + ch.shape[1:], ch.dtype) for ch in chunks]
    return _Exchange(name, plan, chunks, lands, N_SHARDS * n, N_DEVICES * n)


def _sum_contributions(name, got, mine):
    _, a, b = got.shape
    ta = _row_tile(a, max(SUBLANE, SUM_BLOCK_BYTES // (N_DEVICES * b * got.dtype.itemsize) // SUBLANE * SUBLANE))

    def kern(got_ref, mine_ref, o_ref):
        me = _device_index()
        acc = jnp.zeros(o_ref.shape, f32)
        for s in range(N_DEVICES):
            acc = acc + jnp.where(me == s, mine_ref[...].astype(f32), got_ref[s].astype(f32))
        o_ref[...] = acc

    return pl.pallas_call(
        kern, name=name, grid=(a // ta,),
        in_specs=[pl.BlockSpec((N_DEVICES, ta, b), lambda i: (0, i, 0)), pl.BlockSpec((ta, b), lambda i: (i, 0))],
        out_specs=pl.BlockSpec((ta, b), lambda i: (i, 0)), out_shape=jax.ShapeDtypeStruct((a, b), f32),
        compiler_params=pltpu.CompilerParams(dimension_semantics=("parallel",)))(got, mine)


def _swap_layers(name, sums):
    n = len(sums[0])

    def body(*refs):
        srcs = (refs[:n], refs[n:2 * n])
        outs, (send_sems, recv_sems) = refs[2 * n:3 * n], refs[3 * n:]
        x, y, c = (lax.axis_index(a) for a in MESH_AXES)
        for i in range(n):
            for layer in range(DEPTH):
                cp = pltpu.make_async_remote_copy(
                    src_ref=srcs[layer][i], dst_ref=outs[i], send_sem=send_sems.at[i], recv_sem=recv_sems.at[i],
                    device_id=(x, y, 1 - c), device_id_type=pl.DeviceIdType.MESH)
                pl.when(c == layer)(cp.start)
        for i in range(n):
            pltpu.make_async_remote_copy(
                src_ref=srcs[0][i], dst_ref=outs[i], send_sem=send_sems.at[i], recv_sem=recv_sems.at[i],
                device_id=(x, y, 1 - c), device_id_type=pl.DeviceIdType.MESH).wait()

    return pl.pallas_call(
        body, name=name, out_shape=[jax.ShapeDtypeStruct(s.shape, s.dtype) for s in sums[0]],
        in_specs=[HBM_SPEC] * (2 * n), out_specs=[HBM_SPEC] * n,
        scratch_shapes=[pltpu.SemaphoreType.DMA((n,)), pltpu.SemaphoreType.DMA((n,))])(*sums[0], *sums[1])


def _stack_shards(g, axis):
    if axis == 1:
        return g.reshape(N_SHARDS, g.shape[0] // N_SHARDS, g.shape[1])
    return g.reshape(g.shape[0], N_SHARDS, g.shape[1] // N_SHARDS).transpose(1, 0, 2)


def _join_shards(s, axis):
    if axis == 1:
        return s.reshape(-1, s.shape[2])
    return s.transpose(1, 0, 2).reshape(s.shape[1], -1)


def _layer_shards(w, l, names):
    return [w[k][l] if k in ELEMENTWISE_F32 else w[k][l].astype(bf16) for k in names]


def _full_weights(names, sent, got):
    j = 2 * lax.axis_index("x") + lax.axis_index("y")
    return {k: _join_shards(lax.dynamic_update_slice(g, own[None], (j, 0, 0)), SHARD_AXIS[k])
            for k, own, g in zip(names, sent, got)}


def _grad_chunks(grads, names):
    return [_stack_shards(grads[k], SHARD_AXIS[k]).astype(bf16) for k in names]


def _sum_group(l, names, got, chunks):
    j = 2 * lax.axis_index("x") + lax.axis_index("y")
    return {k: _sum_contributions(f"sum_l{l}_{k}", g, lax.dynamic_index_in_dim(ch, j, 0, keepdims=False))
            for k, g, ch in zip(names, got, chunks)}


def _both_layers(sums):
    c = lax.axis_index("c")
    other = _swap_layers("swap_layers", sums)
    return {k: jnp.stack([jnp.where(c == 0, sums[0][i], other[i]), jnp.where(c == 0, other[i], sums[1][i])])
            for i, k in enumerate(SHARDED)}


def kernel(x, p, positions, g_mix, w_in, g_qc, w_uq, g_kvc, w_ukv, b_f, lru_conv_w, lru_conv_b, w_r, b_r, w_i, b_i, lru_lambda, g_out, w_o, g_ffn, w_up, ffn_conv_w, ffn_conv_b, w_down, g_ple, w_ple_gate, w_ple_proj, g_final, loss_target, m_g_mix, m_w_in, m_g_qc, m_w_uq, m_g_kvc, m_w_ukv, m_b_f, m_lru_conv_w, m_lru_conv_b, m_w_r, m_b_r, m_w_i, m_b_i, m_lru_lambda, m_g_out, m_w_o, m_g_ffn, m_w_up, m_ffn_conv_w, m_ffn_conv_b, m_w_down, m_g_ple, m_w_ple_gate, m_w_ple_proj, m_g_final, v_g_mix, v_w_in, v_g_qc, v_w_uq, v_g_kvc, v_w_ukv, v_b_f, v_lru_conv_w, v_lru_conv_b, v_w_r, v_b_r, v_w_i, v_b_i, v_lru_lambda, v_g_out, v_w_o, v_g_ffn, v_w_up, v_ffn_conv_w, v_ffn_conv_b, v_w_down, v_g_ple, v_w_ple_gate, v_w_ple_proj, v_g_final):
    given = locals()
    w = {k: given[k] for k in WEIGHTS}
    m = {k: given["m_" + k] for k in WEIGHTS}
    v = {k: given["v_" + k] for k in WEIGHTS}

    parts = {"mix": MIX_PART, "ffn": FFN_PART}
    groups = [(l, part) for l in range(DEPTH) for part in ("mix", "ffn")]
    sent = {g: _layer_shards(w, g[0], parts[g[1]]) for g in groups}
    first = _gather_exchange("gather_l0_mix", sent[groups[0]]).run()
    ahead = {g: _gather_exchange(f"gather_l{g[0]}_{g[1]}", sent[g]) for g in groups[1:]}
    pos = positions[0].astype(f32).reshape(-1, 1)
    for ex in ahead.values():
        pos = pos + ex.start(after=first[0])
    behind, layer_grads, chunks = {}, [{} for _ in range(DEPTH)], {}

    def weights_of(l, part, after):
        g = (l, part)
        full = _full_weights(parts[part], sent[g], first if g == groups[0] else ahead[g].finish(after=after))
        if part == "mix":
            full.update({k: w[k][l] for k in LAYER_WEIGHTS if k in REPLICATED})
        return full

    def grads_to(l, part, grads):
        g = (l, part)
        layer_grads[l].update(grads)
        chunks[g] = _grad_chunks(grads, parts[part])
        if g == groups[0]:
            return jnp.zeros((), f32)
        behind[g] = _scatter_exchange(f"scatter_l{l}_{part}", l, chunks[g])
        return behind[g].start()

    loss, dx, dg_final = _local_step(x[0], p[:, 0], pos, loss_target[0], w["g_final"], weights_of, grads_to)

    sums = [{} for _ in range(DEPTH)]
    for g in groups:
        got = _scatter_exchange("scatter_l0_mix", 0, chunks[g]).run() if g == groups[0] else behind[g].finish(after=dx)
        sums[g[0]].update(_sum_group(g[0], parts[g[1]], got, chunks[g]))
    g_sharded = _both_layers([[sums[l][k] for k in SHARDED] for l in range(DEPTH)])
    big = [[], [], [], []]
    for k in SHARDED:
        shape = w[k].shape
        flat = [t.reshape(-1, shape[-1]) for t in (w[k], g_sharded[k], m[k], v[k])]
        for kind, res in enumerate((flat[1],) + tuple(_adamw("adamw_" + k, *flat))):
            big[kind].append(res.reshape(shape))

    grads = {k: jnp.stack([layer_grads[l][k] for l in range(DEPTH)]) for k in LAYER_WEIGHTS if k in REPLICATED}
    grads["g_final"] = dg_final
    rep_shapes = [w[k].shape for k in REPLICATED] + [(1,)]
    contrib = _pack([grads[k] for k in REPLICATED] + [loss.reshape(1)])
    g_rep = _sum_slabs("sum_replicated", _exchange("gather_replicated", contrib, MESH_AXES, scatter=False))
    zero = jnp.zeros((1,), f32)
    w_rep, m_rep, v_rep = (_pack([t[k] for k in REPLICATED] + [zero]) for t in (w, m, v))
    rep = [_unpack(b, rep_shapes) for b in (g_rep,) + tuple(_adamw("adamw_replicated", w_rep, g_rep, m_rep, v_rep))]

    outs = []
    for kind in range(4):
        by_name = dict(zip(SHARDED, big[kind]))
        by_name.update(zip(REPLICATED, rep[kind][:-1]))
        outs.append([by_name[k] for k in WEIGHTS])
    total_loss = rep[0][-1][0]
    return (total_loss, dx.reshape(x.shape), *outs[0], *outs[1], *outs[2], *outs[3])
```

```python
import functools
import math

import numpy as np
import jax
import jax.numpy as jnp
from jax import lax
from jax.experimental import pallas as pl
from jax.experimental.pallas import tpu as pltpu

f32, bf16 = jnp.float32, jnp.bfloat16

D_MODEL = 1024
PLE_DIM = 256
MLA_HEADS, MLA_NOPE, MLA_ROPE, MLA_V = 4, 64, 32, 64
MLA_Q_RANK, MLA_KV_RANK = 192, 128
FOX_HEADS, FOX_HEAD_DIM = 4, 64
LRU_WIDTH, LRU_BLOCKS, LRU_BLOCK, LRU_CONV, LRU_C = 512, 8, 64, 4, 8.0
D_FF, FFN_CONV = 2816, 3
ROPE_THETA = 10000.0
EPS = 1e-6
DEPTH = 2
ADAM_LR, ADAM_B1, ADAM_B2, ADAM_EPS, ADAM_WD, ADAM_STEP = 0.001, 0.9, 0.999, 1e-08, 0.01, 10

LANE = 128
SUBLANE = 8
HEADS = 4

Z_FQ, Z_FK, Z_FV, Z_LX, Z_LG, Z_QC, Z_KVC, Z_KR, Z_FL, Z_W = 0, 512, 1024, 1536, 2048, 2560, 2816, 2944, 3072, 3200
QC_W = 256
ROPE_AT = 64


def _head_pad_map(n_heads, width):
    m = -np.ones(n_heads * LANE, np.int64)
    for h in range(n_heads):
        m[h * LANE:h * LANE + width] = h * width + np.arange(width)
    return m


def _z_map():
    m = -np.ones(Z_W, np.int64)
    o_qc, o_kvc, o_kr = 0, MLA_Q_RANK, MLA_Q_RANK + MLA_KV_RANK
    o_fq = o_kr + MLA_ROPE
    o_fk, o_fv = o_fq + 256, o_fq + 512
    o_fl = o_fv + 256
    o_lx = o_fl + FOX_HEADS
    o_lg = o_lx + LRU_WIDTH
    m[Z_FQ:Z_FQ + 512] = np.where(_head_pad_map(4, 64) >= 0, _head_pad_map(4, 64) + o_fq, -1)
    m[Z_FK:Z_FK + 512] = np.where(_head_pad_map(4, 64) >= 0, _head_pad_map(4, 64) + o_fk, -1)
    m[Z_FV:Z_FV + 512] = np.where(_head_pad_map(4, 64) >= 0, _head_pad_map(4, 64) + o_fv, -1)
    m[Z_LX:Z_LX + 512] = o_lx + np.arange(512)
    m[Z_LG:Z_LG + 512] = o_lg + np.arange(512)
    m[Z_QC:Z_QC + MLA_Q_RANK] = o_qc + np.arange(MLA_Q_RANK)
    m[Z_KVC:Z_KVC + MLA_KV_RANK] = o_kvc + np.arange(MLA_KV_RANK)
    m[Z_KR + ROPE_AT:Z_KR + ROPE_AT + MLA_ROPE] = o_kr + np.arange(MLA_ROPE)
    m[Z_FL:Z_FL + FOX_HEADS] = o_fl + np.arange(FOX_HEADS)
    return m


def _ukv_map():
    m = -np.ones(2 * HEADS * LANE, np.int64)
    for h in range(HEADS):
        m[h * LANE:h * LANE + MLA_NOPE] = h * (MLA_NOPE + MLA_V) + np.arange(MLA_NOPE)
        m[HEADS * LANE + h * LANE:HEADS * LANE + h * LANE + MLA_V] = h * (MLA_NOPE + MLA_V) + MLA_NOPE + np.arange(MLA_V)
    return m


def _omix_map():
    return np.concatenate([_head_pad_map(4, 64), np.where(_head_pad_map(4, 64) >= 0, _head_pad_map(4, 64) + 256, -1),
                           512 + np.arange(512)])


def _pad_to(m, n):
    return np.concatenate([m, -np.ones(n - m.shape[0], np.int64)])


def _take_pad(a, m, axis):
    out = jnp.take(a, jnp.asarray(np.maximum(m, 0), jnp.int32), axis=axis)
    shape = [1] * a.ndim
    shape[axis] = m.shape[0]
    return out * jnp.asarray((m >= 0).reshape(shape), a.dtype)


def _take_inv(a, m, axis):
    n = int(m.max()) + 1
    inv = np.zeros(n, np.int64)
    inv[m[m >= 0]] = np.nonzero(m >= 0)[0]
    return jnp.take(a, jnp.asarray(inv, jnp.int32), axis=axis)


Z_MAP = _z_map()
UQ_COL_MAP = _head_pad_map(HEADS, MLA_NOPE + MLA_ROPE)
UQ_ROW_MAP = _pad_to(np.arange(MLA_Q_RANK), QC_W)
UKV_MAP = _ukv_map()
OMIX_MAP = _omix_map()
OMIX_W = 1536


def _rope_tables(width, at):
    half = MLA_ROPE // 2
    inv = ROPE_THETA ** (-np.arange(half, dtype=np.float32) / half)
    freq = np.zeros((1, width), np.float32)
    m1 = np.zeros((1, width), np.float32)
    m2 = np.zeros((1, width), np.float32)
    for h in range(width // LANE):
        b = h * LANE + at
        freq[0, b:b + half] = inv
        freq[0, b + half:b + 2 * half] = inv
        m1[0, b:b + half] = 1.0
        m2[0, b + half:b + 2 * half] = 1.0
    return freq, m1, m2


def _view(r):
    return r if isinstance(r, tuple) else (r, r.shape[1], 0)


def _blk(dim, cap):
    if dim <= cap:
        return dim
    for b in range(cap, LANE - 1, -LANE):
        if dim % b == 0:
            return b
    return dim


@functools.partial(jax.custom_vjp, nondiff_argnums=(1, 2))
def _roll(x, shift, axis):
    return pltpu.roll(x, shift, axis)


def _roll_fwd(x, shift, axis):
    return pltpu.roll(x, shift, axis), None


def _roll_bwd(shift, axis, _, g):
    return (pltpu.roll(g, g.shape[axis] - shift, axis),)


_roll.defvjp(_roll_fwd, _roll_bwd)


def _rowwise(name, fn, rows, pars, outs, tb=256):
    rows = [_view(r) for r in rows]
    n = rows[0][0].shape[0]
    tb = min(tb, n)
    nr, npar = len(rows), len(pars)

    def kern(*refs):
        r = [refs[k][...].astype(f32) for k in range(nr)]
        p = [refs[nr + k][...] for k in range(npar)]
        res = fn(*r, *p)
        for o_ref, o in zip(refs[nr + npar:], res):
            o_ref[...] = o.astype(o_ref.dtype)

    in_specs = [pl.BlockSpec((tb, w), lambda i, j=idx: (i, j)) for (_, w, idx) in rows]
    in_specs += [pl.BlockSpec(p.shape, lambda i: (0, 0)) for p in pars]
    out_specs = [pl.BlockSpec((tb, w), lambda i: (i, 0)) for (w, _) in outs]
    out_shape = [jax.ShapeDtypeStruct((n, w), dt) for (w, dt) in outs]
    return pl.pallas_call(kern, name=name, grid=(n // tb,), in_specs=in_specs, out_specs=out_specs, out_shape=out_shape,
                          compiler_params=pltpu.CompilerParams(dimension_semantics=("parallel",)))(*[r[0] for r in rows], *pars)


def _rowwise_bwd(name, fn, rows, pars, cts, ndiff, adds=None, tb=256, dts=None):
    rows = [_view(r) for r in rows]
    dts = dts or [f32] * ndiff
    adds = adds or {}
    add_keys = sorted(adds)
    n = rows[0][0].shape[0]
    tb = min(tb, n)
    nr, npar, nct, nadd = len(rows), len(pars), len(cts), len(add_keys)

    def kern(*refs):
        i = pl.program_id(0)
        r = [refs[k][...].astype(f32) for k in range(nr)]
        p = [refs[nr + k][...] for k in range(npar)]
        ct = [refs[nr + npar + k][...].astype(f32) for k in range(nct)]
        ad = {key: refs[nr + npar + nct + k][...] for k, key in enumerate(add_keys)}
        o_refs = refs[nr + npar + nct + nadd:]

        def g(*d):
            return tuple(fn(*d[:ndiff], *r[ndiff:], *d[ndiff:]))

        _, vjp = jax.vjp(g, *r[:ndiff], *p)
        grads = vjp(tuple(ct))
        for k in range(ndiff):
            gk = grads[k]
            if k in ad:
                gk = gk + ad[k]
            o_refs[k][...] = gk.astype(o_refs[k].dtype)

        @pl.when(i == 0)
        def _():
            for k in range(npar):
                o_refs[ndiff + k][...] = jnp.zeros_like(o_refs[ndiff + k])

        for k in range(npar):
            o_refs[ndiff + k][...] += grads[ndiff + k]

    in_specs = [pl.BlockSpec((tb, w), lambda i, j=idx: (i, j)) for (_, w, idx) in rows]
    in_specs += [pl.BlockSpec(p.shape, lambda i: (0, 0)) for p in pars]
    in_specs += [pl.BlockSpec((tb, c.shape[1]), lambda i: (i, 0)) for c in cts]
    in_specs += [pl.BlockSpec((tb, adds[k].shape[1]), lambda i: (i, 0)) for k in add_keys]
    out_specs = [pl.BlockSpec((tb, rows[k][1]), lambda i: (i, 0)) for k in range(ndiff)]
    out_specs += [pl.BlockSpec(p.shape, lambda i: (0, 0)) for p in pars]
    out_shape = [jax.ShapeDtypeStruct((n, rows[k][1]), dts[k]) for k in range(ndiff)]
    out_shape += [jax.ShapeDtypeStruct(p.shape, f32) for p in pars]
    res = pl.pallas_call(kern, name=name, grid=(n // tb,), in_specs=in_specs, out_specs=out_specs, out_shape=out_shape,
                         compiler_params=pltpu.CompilerParams(dimension_semantics=("arbitrary",)))(
        *[r[0] for r in rows], *pars, *cts, *[adds[k] for k in add_keys])
    return res[:ndiff], res[ndiff:]


_DOT_DIMS = {"nn": ((1,), (0,)), "nt": ((1,), (1,)), "tn": ((0,), (0,))}

MM_VMEM_BUDGET = 36 * 2 ** 20
MM_MAX_TM = 1408
MM_STEP, MM_RESULT, MM_XPOSE, MM_CAST = 700.0, 7.5e-4, 9e-4, 1e-3


def _tile_candidates(dim):
    c = [d for d in range(LANE, dim + 1, LANE) if dim % d == 0]
    return c or [dim]


@functools.lru_cache(maxsize=None)
def _mm_tiles(mode, m, n, k, a_bytes, b_bytes, o_bytes):
    best, best_cost = None, None
    for tm in _tile_candidates(m):
        if tm > MM_MAX_TM:
            continue
        for tn in _tile_candidates(n):
            for tk in _tile_candidates(k):
                vmem = 2 * (tm * tk * a_bytes + tk * tn * b_bytes + tm * tn * o_bytes) + 4 * tm * tn * (2 if tk < k else 1)
                vmem += (2 * tm * tk if a_bytes > 2 else 0) + (2 * tk * tn if b_bytes > 2 else 0)
                if vmem > MM_VMEM_BUDGET:
                    continue
                steps = (m // tm) * (n // tn) * (k // tk)
                cost = steps * MM_STEP + m * n * (k // tk) * MM_RESULT
                if mode == "tn":
                    cost += m * k * (n // tn) * MM_XPOSE
                cost += (m * k * (n // tn) * MM_CAST if a_bytes > 2 else 0) + (k * n * (m // tm) * MM_CAST if b_bytes > 2 else 0)
                if best is None or cost < best_cost:
                    best, best_cost = (tm, tn, tk), cost
    return best


def _mm(name, a, b, mode="nn", out_dtype=f32, res=None):
    if mode == "nn":
        (m, k), (_, n) = a.shape, b.shape
    elif mode == "nt":
        (m, k), (n, _) = a.shape, b.shape
    else:
        (k, m), (_, n) = a.shape, b.shape
    has_res = res is not None
    tm, tn, tk = _mm_tiles(mode, m, n, k, a.dtype.itemsize, b.dtype.itemsize,
                           jnp.dtype(out_dtype).itemsize + (res.dtype.itemsize if has_res else 0))
    nk = k // tk
    dims = (_DOT_DIMS[mode], ((), ()))

    def kern(*refs):
        a_ref, b_ref = refs[0], refs[1]
        o_ref, acc_ref = refs[-2], refs[-1]
        kk = pl.program_id(2)
        part = lax.dot_general(a_ref[...].astype(bf16), b_ref[...].astype(bf16), dims, preferred_element_type=f32)

        def finish(out):
            if has_res:
                out = out + refs[2][...]
            o_ref[...] = out.astype(o_ref.dtype)

        if nk == 1:
            finish(part)
            return

        @pl.when(kk == 0)
        def _():
            acc_ref[...] = part

        @pl.when(jnp.logical_and(kk > 0, kk < nk - 1))
        def _():
            acc_ref[...] += part

        @pl.when(kk == nk - 1)
        def _():
            finish(acc_ref[...] + part)

    if mode == "tn":
        a_spec = pl.BlockSpec((tk, tm), lambda i, j, kk: (kk, i))
    else:
        a_spec = pl.BlockSpec((tm, tk), lambda i, j, kk: (i, kk))
    if mode == "nt":
        b_spec = pl.BlockSpec((tn, tk), lambda i, j, kk: (j, kk))
    else:
        b_spec = pl.BlockSpec((tk, tn), lambda i, j, kk: (kk, j))
    in_specs = [a_spec, b_spec]
    args = [a, b]
    if has_res:
        in_specs.append(pl.BlockSpec((tm, tn), lambda i, j, kk: (i, j)))
        args.append(res)
    return pl.pallas_call(
        kern, name=name, grid=(m // tm, n // tn, nk), in_specs=in_specs,
        out_specs=pl.BlockSpec((tm, tn), lambda i, j, kk: (i, j)),
        out_shape=jax.ShapeDtypeStruct((m, n), out_dtype),
        scratch_shapes=[pltpu.VMEM((tm, tn) if nk > 1 else (SUBLANE, LANE), f32)],
        compiler_params=pltpu.CompilerParams(dimension_semantics=("parallel", "parallel", "arbitrary")))(*args)


ATT_TQ, ATT_TK = 512, 512


def _att_tiles(s_len):
    tk = min(ATT_TK, s_len)
    return min(ATT_TQ, tk), tk


def _fold_scale(scale):
    return (scale, 1.0) if math.frexp(scale)[0] == 0.5 else (1.0, scale)


def _query_rows(x):
    s_len = x.shape[1]
    tq = _att_tiles(s_len)[0]
    return x.reshape(HEADS, s_len // tq, 1, tq)


def _scores_t(kb, q_t, s_mul, ck, diag_offset, tq, tk):
    s = jnp.dot(kb, q_t, preferred_element_type=f32)
    if s_mul != 1.0:
        s = s * s_mul
    if ck is not None:
        s = s - ck
    if diag_offset is None:
        return s
    key = lax.broadcasted_iota(jnp.int32, (tk, tq), 0)
    query = lax.broadcasted_iota(jnp.int32, (tk, tq), 1) + diag_offset
    return jnp.where(key <= query, s, -jnp.inf)


def _scores(qb, kb, s_mul, ck, diagonal, t):
    s = lax.dot_general(qb, kb, (_DOT_DIMS["nt"], ((), ())), preferred_element_type=f32)
    if s_mul != 1.0:
        s = s * s_mul
    if ck is not None:
        s = s - ck
    if not diagonal:
        return s
    row = lax.broadcasted_iota(jnp.int32, (t, t), 0)
    col = lax.broadcasted_iota(jnp.int32, (t, t), 1)
    return jnp.where(col <= row, s, -jnp.inf)


def _attn_fwd(name, q, k, v, scale, c_row=None):
    (qa, qo), (ka, ko), (va, vo) = q, k, v
    s_len = qa.shape[0]
    t = _att_tiles(s_len)[1]
    nt = s_len // t
    decay = c_row is not None
    q_mul, s_mul = _fold_scale(scale)

    def kern(*refs):
        q_ref, k_ref, v_ref = refs[:3]
        o_ref, lse_ref = refs[-2:]
        i = pl.program_id(1)
        qb = (q_ref[...] * q_mul).astype(bf16)

        def step(j, carry, diagonal):
            m, l, acc = carry
            rows = pl.ds(pl.multiple_of(j * t, t), t)
            kb = k_ref[rows, :].astype(bf16)
            vb = v_ref[rows, :].astype(bf16)
            s = _scores(qb, kb, s_mul, refs[3][j] if decay else None, diagonal, t)
            m_new = jnp.maximum(m, jnp.max(s, axis=1, keepdims=True))
            alpha = jnp.exp(m - m_new)
            p = jnp.exp(s - m_new)
            l = alpha * l + jnp.sum(p, axis=1, keepdims=True)
            acc = alpha * acc + jnp.dot(p.astype(bf16), vb, preferred_element_type=f32)
            return m_new, l, acc

        init = (jnp.full((t, 1), -jnp.inf, f32), jnp.zeros((t, 1), f32), jnp.zeros((t, LANE), f32))
        m, l, acc = step(i, lax.fori_loop(0, i, lambda j, c: step(j, c, False), init), True)
        o_ref[...] = acc / l
        lse_ref[...] = m + jnp.log(l)

    in_specs = [pl.BlockSpec((t, LANE), lambda h, i: (i, qo + h)),
                pl.BlockSpec((s_len, LANE), lambda h, i: (0, ko + h)),
                pl.BlockSpec((s_len, LANE), lambda h, i: (0, vo + h))]
    args = [qa, ka, va]
    if decay:
        in_specs.append(pl.BlockSpec((None, nt, 1, t), lambda h, i: (h, 0, 0, 0)))
        args.append(c_row)
    return pl.pallas_call(
        kern, name=name, grid=(HEADS, nt), in_specs=in_specs,
        out_specs=[pl.BlockSpec((t, LANE), lambda h, i: (i, h)), pl.BlockSpec((None, t, 1), lambda h, i: (h, i, 0))],
        out_shape=[jax.ShapeDtypeStruct((s_len, HEADS * LANE), f32), jax.ShapeDtypeStruct((HEADS, s_len, 1), f32)],
        compiler_params=pltpu.CompilerParams(dimension_semantics=("parallel", "arbitrary")))(*args)


def _attn_dq(name, q, k, v, o, do, lse, scale, c_row=None):
    (qa, qo), (ka, ko), (va, vo) = q, k, v
    s_len = qa.shape[0]
    t = _att_tiles(s_len)[1]
    nt = s_len // t
    decay = c_row is not None
    q_mul, s_mul = _fold_scale(scale)

    def kern(*refs):
        q_ref, k_ref, v_ref, o_ref, do_ref, lse_ref = refs[:6]
        dq_ref, delta_ref, drow_ref = refs[-3:]
        i = pl.program_id(1)
        qb = (q_ref[...] * q_mul).astype(bf16)
        dob = do_ref[...]
        delta = jnp.sum(dob * o_ref[...], axis=1, keepdims=True)
        dob = dob.astype(bf16)
        lse = lse_ref[...]

        def step(j, carry, diagonal):
            dq, drow = carry
            rows = pl.ds(pl.multiple_of(j * t, t), t)
            kb = k_ref[rows, :].astype(bf16)
            vb = v_ref[rows, :].astype(bf16)
            s = _scores(qb, kb, s_mul, refs[6][j] if decay else None, diagonal, t)
            p = jnp.exp(s - lse)
            dp = lax.dot_general(dob, vb, (_DOT_DIMS["nt"], ((), ())), preferred_element_type=f32)
            ds = p * (dp - delta)
            return dq + jnp.dot(ds.astype(bf16), kb, preferred_element_type=f32), drow + jnp.sum(ds, axis=1, keepdims=True)

        init = (jnp.zeros((t, LANE), f32), jnp.zeros((t, 1), f32))
        dq, drow = step(i, lax.fori_loop(0, i, lambda j, c: step(j, c, False), init), True)
        dq_ref[...] = dq * scale
        delta_ref[...] = delta
        drow_ref[...] = drow

    in_specs = [pl.BlockSpec((t, LANE), lambda h, i: (i, qo + h)),
                pl.BlockSpec((s_len, LANE), lambda h, i: (0, ko + h)),
                pl.BlockSpec((s_len, LANE), lambda h, i: (0, vo + h)),
                pl.BlockSpec((t, LANE), lambda h, i: (i, h)),
                pl.BlockSpec((t, LANE), lambda h, i: (i, h)),
                pl.BlockSpec((None, t, 1), lambda h, i: (h, i, 0))]
    args = [qa, ka, va, o, do, lse]
    if decay:
        in_specs.append(pl.BlockSpec((None, nt, 1, t), lambda h, i: (h, 0, 0, 0)))
        args.append(c_row)
    col = pl.BlockSpec((None, t, 1), lambda h, i: (h, i, 0))
    return pl.pallas_call(
        kern, name=name, grid=(HEADS, nt), in_specs=in_specs,
        out_specs=[pl.BlockSpec((t, LANE), lambda h, i: (i, h)), col, col],
        out_shape=[jax.ShapeDtypeStruct((s_len, HEADS * LANE), f32), jax.ShapeDtypeStruct((HEADS, s_len, 1), f32),
                   jax.ShapeDtypeStruct((HEADS, s_len, 1), f32)],
        compiler_params=pltpu.CompilerParams(dimension_semantics=("parallel", "arbitrary")))(*args)


def _attn_dkv(name, q, k, v, do, lse, delta, scale, c_col=None):
    (qa, qo), (ka, ko), (va, vo) = q, k, v
    s_len = qa.shape[0]
    tq, tk = _att_tiles(s_len)
    nq, per = s_len // tq, tk // tq
    decay = c_col is not None
    q_mul, s_mul = _fold_scale(scale)

    def kern(*refs):
        q_ref, k_ref, v_ref, do_ref, lse_ref, delta_ref = refs[:6]
        j = pl.program_id(1)
        kb = k_ref[...].astype(bf16)
        vb = v_ref[...].astype(bf16)
        ck = refs[6][...] if decay else None

        def step(i, carry, diagonal):
            dk, dv, dsum = carry
            for d in range(per):
                tile = i * per + d
                rows = pl.ds(pl.multiple_of(tile * tq, tq), tq)
                qb = (q_ref[rows, :] * q_mul).astype(bf16)
                dob = do_ref[rows, :].astype(bf16)
                s = _scores_t(kb, qb.T, s_mul, ck, d * tq if diagonal else None, tq, tk)
                p = jnp.exp(s - lse_ref[tile])
                dv = dv + jnp.dot(p.astype(bf16), dob, preferred_element_type=f32)
                dp = jnp.dot(vb, dob.T, preferred_element_type=f32)
                ds = p * (dp - delta_ref[tile])
                dk = dk + jnp.dot(ds.astype(bf16), qb, preferred_element_type=f32)
                if decay:
                    dsum = dsum + ds
            return dk, dv, dsum

        init = (jnp.zeros((tk, LANE), f32), jnp.zeros((tk, LANE), f32), jnp.zeros((tk, tq), f32))
        dk, dv, dsum = lax.fori_loop(j + 1, s_len // tk, lambda i, c: step(i, c, False), step(j, init, True))
        if decay:
            dk_ref, dv_ref, dc_ref = refs[-3:]
            dc_ref[...] = -jnp.sum(dsum, axis=1, keepdims=True)
        else:
            dk_ref, dv_ref = refs[-2:]
        dk_ref[...] = dk * s_mul
        dv_ref[...] = dv

    stat = pl.BlockSpec((None, nq, 1, tq), lambda h, j: (h, 0, 0, 0))
    in_specs = [pl.BlockSpec((s_len, LANE), lambda h, j: (0, qo + h)),
                pl.BlockSpec((tk, LANE), lambda h, j: (j, ko + h)),
                pl.BlockSpec((tk, LANE), lambda h, j: (j, vo + h)),
                pl.BlockSpec((s_len, LANE), lambda h, j: (0, h)), stat, stat]
    args = [qa, ka, va, do, lse, delta]
    out_specs = [pl.BlockSpec((tk, LANE), lambda h, j: (j, h)), pl.BlockSpec((tk, LANE), lambda h, j: (j, h))]
    out_shape = [jax.ShapeDtypeStruct((s_len, HEADS * LANE), f32), jax.ShapeDtypeStruct((s_len, HEADS * LANE), f32)]
    if decay:
        in_specs.append(pl.BlockSpec((None, tk, 1), lambda h, j: (h, j, 0)))
        args.append(c_col)
        out_specs.append(pl.BlockSpec((None, tk, 1), lambda h, j: (h, j, 0)))
        out_shape.append(jax.ShapeDtypeStruct((HEADS, s_len, 1), f32))
    return pl.pallas_call(
        kern, name=name, grid=(HEADS, s_len // tk), in_specs=in_specs, out_specs=out_specs, out_shape=out_shape,
        compiler_params=pltpu.CompilerParams(dimension_semantics=("parallel", "arbitrary")))(*args)


CONV_TS, CONV_CB = 1024, 256


def _conv_fwd(name, x, w, b, taps):
    xa, width, xidx = _view(x)
    s_len = xa.shape[0]
    ts, cb = min(CONV_TS, s_len), CONV_CB
    xo = xidx * width // cb

    def kern(x_ref, halo_ref, w_ref, b_ref, o_ref):
        i = pl.program_id(1)
        xb = x_ref[...]
        halo = jnp.where(i == 0, 0.0, halo_ref[...])
        xx = jnp.concatenate([halo, xb], axis=0)
        out = b_ref[...] + w_ref[taps - 1:taps, :] * xb
        for k in range(taps - 1):
            out = out + w_ref[k:k + 1, :] * pltpu.roll(xx, taps - 1 - k, 0)[SUBLANE:]
        o_ref[...] = out

    return pl.pallas_call(
        kern, name=name, grid=(width // cb, s_len // ts),
        in_specs=[pl.BlockSpec((ts, cb), lambda j, i: (i, xo + j)),
                  pl.BlockSpec((SUBLANE, cb), lambda j, i: (jnp.maximum(i * (ts // SUBLANE) - 1, 0), xo + j)),
                  pl.BlockSpec((taps, cb), lambda j, i: (0, j)),
                  pl.BlockSpec((1, cb), lambda j, i: (0, j))],
        out_specs=pl.BlockSpec((ts, cb), lambda j, i: (i, j)),
        out_shape=jax.ShapeDtypeStruct((s_len, width), f32),
        compiler_params=pltpu.CompilerParams(dimension_semantics=("parallel", "parallel")))(xa, xa, w, b)


def _conv_bwd(name, x, dout, w, taps, dout2=None, dx_dtype=f32):
    xa, width, xidx = _view(x)
    s_len = xa.shape[0]
    ts, cb = min(CONV_TS, s_len), CONV_CB
    xo = xidx * width // cb
    n_i = s_len // ts
    two = dout2 is not None

    def kern(*refs):
        x_ref, halo_ref, w_ref = refs[:3]
        dx_ref, dw_ref, db_ref = refs[-3:]
        i = pl.program_id(1)
        if two:
            d = refs[3][...] + refs[5][...]
            dn = refs[4][...] + refs[6][...]
        else:
            d, dn = refs[3][...], refs[4][...]
        dn = jnp.where(i == n_i - 1, 0.0, dn)
        xb = x_ref[...]
        halo = jnp.where(i == 0, 0.0, halo_ref[...])
        xx = jnp.concatenate([halo, xb], axis=0)
        dd = jnp.concatenate([d, dn], axis=0)

        @pl.when(i == 0)
        def _():
            dw_ref[...] = jnp.zeros_like(dw_ref)
            db_ref[...] = jnp.zeros_like(db_ref)

        dx = w_ref[taps - 1:taps, :] * d
        dw_ref[taps - 1:taps, :] += jnp.sum(d * xb, axis=0, keepdims=True)
        for k in range(taps - 1):
            sh = taps - 1 - k
            dx = dx + w_ref[k:k + 1, :] * pltpu.roll(dd, ts + SUBLANE - sh, 0)[:ts]
            dw_ref[k:k + 1, :] += jnp.sum(d * pltpu.roll(xx, sh, 0)[SUBLANE:], axis=0, keepdims=True)
        dx_ref[...] = dx.astype(dx_ref.dtype)
        db_ref[...] += jnp.sum(d, axis=0, keepdims=True)

    d_spec = pl.BlockSpec((ts, cb), lambda j, i: (i, j))
    dn_spec = pl.BlockSpec((SUBLANE, cb), lambda j, i: (jnp.minimum((i + 1) * (ts // SUBLANE), s_len // SUBLANE - 1), j))
    in_specs = [pl.BlockSpec((ts, cb), lambda j, i: (i, xo + j)),
                pl.BlockSpec((SUBLANE, cb), lambda j, i: (jnp.maximum(i * (ts // SUBLANE) - 1, 0), xo + j)),
                pl.BlockSpec((taps, cb), lambda j, i: (0, j)), d_spec, dn_spec]
    args = [xa, xa, w, dout, dout]
    if two:
        in_specs += [d_spec, dn_spec]
        args += [dout2, dout2]
    return pl.pallas_call(
        kern, name=name, grid=(width // cb, n_i), in_specs=in_specs,
        out_specs=[pl.BlockSpec((ts, cb), lambda j, i: (i, j)), pl.BlockSpec((taps, cb), lambda j, i: (0, j)),
                   pl.BlockSpec((1, cb), lambda j, i: (0, j))],
        out_shape=[jax.ShapeDtypeStruct((s_len, width), dx_dtype), jax.ShapeDtypeStruct((taps, width), f32),
                   jax.ShapeDtypeStruct((1, width), f32)],
        compiler_params=pltpu.CompilerParams(dimension_semantics=("parallel", "arbitrary")))(*args)


def _conv_rows(xx, w_ref, b_ref, taps):
    out = b_ref[...] + w_ref[taps - 1:taps, :] * xx[SUBLANE:]
    for k in range(taps - 1):
        out = out + w_ref[k:k + 1, :] * pltpu.roll(xx, taps - 1 - k, 0)[SUBLANE:]
    return out


def _ffn_act_fwd(name, up, w, b):
    s_len = up.shape[0]
    ts, cb = min(CONV_TS, s_len), CONV_CB
    nf = D_FF // cb

    def kern(g_ref, gp_ref, v_ref, vp_ref, wg_ref, wv_ref, bg_ref, bv_ref, o_ref):
        first = pl.program_id(1) == 0
        ug = _conv_rows(jnp.concatenate([jnp.where(first, 0.0, gp_ref[...]), g_ref[...]], axis=0), wg_ref, bg_ref, FFN_CONV)
        uv = _conv_rows(jnp.concatenate([jnp.where(first, 0.0, vp_ref[...]), v_ref[...]], axis=0), wv_ref, bv_ref, FFN_CONV)
        o_ref[...] = (jax.nn.silu(ug) * uv).astype(o_ref.dtype)

    def half(off):
        return [pl.BlockSpec((ts, cb), lambda j, i: (i, off + j)),
                pl.BlockSpec((SUBLANE, cb), lambda j, i: (jnp.maximum(i * (ts // SUBLANE) - 1, 0), off + j))]

    def par(rows, off):
        return pl.BlockSpec((rows, cb), lambda j, i: (0, off + j))

    return pl.pallas_call(
        kern, name=name, grid=(nf, s_len // ts),
        in_specs=half(0) + half(nf) + [par(FFN_CONV, 0), par(FFN_CONV, nf), par(1, 0), par(1, nf)],
        out_specs=pl.BlockSpec((ts, cb), lambda j, i: (i, j)),
        out_shape=jax.ShapeDtypeStruct((s_len, D_FF), bf16),
        compiler_params=pltpu.CompilerParams(dimension_semantics=("parallel", "parallel")))(up, up, up, up, w, w, b, b)


def _ffn_act_bwd(name, up, dact, w, b):
    s_len = up.shape[0]
    ts, cb = min(CONV_TS, s_len), CONV_CB
    nf = D_FF // cb
    n_i = s_len // ts
    taps = FFN_CONV

    def kern(g_ref, gp_ref, gn_ref, v_ref, vp_ref, vn_ref, d_ref, dn_ref, wg_ref, wv_ref, bg_ref, bv_ref,
             dg_ref, dv_ref, dwg_ref, dwv_ref, dbg_ref, dbv_ref):
        i = pl.program_id(1)
        first, last = i == 0, i == n_i - 1

        def extended(x_ref, p_ref, n_ref):
            return jnp.concatenate([jnp.where(first, 0.0, p_ref[...]), x_ref[...], jnp.where(last, 0.0, n_ref[...])], axis=0)

        gx, vx = extended(g_ref, gp_ref, gn_ref), extended(v_ref, vp_ref, vn_ref)
        ug, uv = _conv_rows(gx, wg_ref, bg_ref, taps), _conv_rows(vx, wv_ref, bv_ref, taps)
        dd = jnp.concatenate([d_ref[...], jnp.where(last, 0.0, dn_ref[...])], axis=0)
        sg = jax.nn.sigmoid(ug)
        dug = dd * uv * (sg * (1.0 + ug * (1.0 - sg)))
        duv = dd * (ug * sg)

        @pl.when(first)
        def _():
            for ref in (dwg_ref, dwv_ref, dbg_ref, dbv_ref):
                ref[...] = jnp.zeros_like(ref)

        def transposed(du, xx, w_ref, dx_ref, dw_ref, db_ref):
            d = du[:ts]
            dx = w_ref[taps - 1:taps, :] * d
            dw_ref[taps - 1:taps, :] += jnp.sum(d * xx[SUBLANE:SUBLANE + ts], axis=0, keepdims=True)
            for k in range(taps - 1):
                sh = taps - 1 - k
                dx = dx + w_ref[k:k + 1, :] * pltpu.roll(du, ts + SUBLANE - sh, 0)[:ts]
                dw_ref[k:k + 1, :] += jnp.sum(d * pltpu.roll(xx, sh, 0)[SUBLANE:SUBLANE + ts], axis=0, keepdims=True)
            dx_ref[...] = dx.astype(dx_ref.dtype)
            db_ref[...] += jnp.sum(d, axis=0, keepdims=True)

        transposed(dug, gx, wg_ref, dg_ref, dwg_ref, dbg_ref)
        transposed(duv, vx, wv_ref, dv_ref, dwv_ref, dbv_ref)

    blocks = s_len // SUBLANE

    def half(off):
        return [pl.BlockSpec((ts, cb), lambda j, i: (i, off + j)),
                pl.BlockSpec((SUBLANE, cb), lambda j, i: (jnp.maximum(i * (ts // SUBLANE) - 1, 0), off + j)),
                pl.BlockSpec((SUBLANE, cb), lambda j, i: (jnp.minimum((i + 1) * (ts // SUBLANE), blocks - 1), off + j))]

    def par(rows, off):
        return pl.BlockSpec((rows, cb), lambda j, i: (0, off + j))

    d_specs = [pl.BlockSpec((ts, cb), lambda j, i: (i, j)),
               pl.BlockSpec((SUBLANE, cb), lambda j, i: (jnp.minimum((i + 1) * (ts // SUBLANE), blocks - 1), j))]
    out_par = [pl.BlockSpec((r, cb), lambda j, i: (0, j)) for r in (taps, taps, 1, 1)]
    return pl.pallas_call(
        kern, name=name, grid=(nf, n_i),
        in_specs=half(0) + half(nf) + d_specs + [par(taps, 0), par(taps, nf), par(1, 0), par(1, nf)],
        out_specs=[pl.BlockSpec((ts, cb), lambda j, i: (i, j))] * 2 + out_par,
        out_shape=[jax.ShapeDtypeStruct((s_len, D_FF), bf16)] * 2 + [jax.ShapeDtypeStruct((taps, D_FF), f32)] * 2
        + [jax.ShapeDtypeStruct((1, D_FF), f32)] * 2,
        compiler_params=pltpu.CompilerParams(dimension_semantics=("parallel", "arbitrary")))(
        up, up, up, up, up, up, dact, dact, w, w, b, b)


SCAN_ROWS = 128


def _block_scan(a, b, reverse):
    t = a.shape[0]
    row = lax.broadcasted_iota(jnp.int32, a.shape, 0)
    d = 1
    while d < t:
        keep = row < t - d if reverse else row >= d
        shift = t - d if reverse else d
        a_far = jnp.where(keep, pltpu.roll(a, shift, 0), 1.0)
        b_far = jnp.where(keep, pltpu.roll(b, shift, 0), 0.0)
        b = a * b_far + b
        a = a * a_far
        d *= 2
    return a, b


def _scan_fwd(name, a, b):
    s_len, width = a.shape
    t = min(SCAN_ROWS, s_len)

    def kern(a_ref, b_ref, h_ref):
        def block(k, carry):
            rows = pl.ds(pl.multiple_of(k * t, t), t)
            acc, h = _block_scan(a_ref[rows, :], b_ref[rows, :], False)
            h_ref[rows, :] = h + acc * carry
            return h_ref[pl.ds(k * t + t - 1, 1), :]

        lax.fori_loop(0, s_len // t, block, jnp.zeros((1, LANE), f32))

    spec = pl.BlockSpec((s_len, LANE), lambda j: (0, j))
    return pl.pallas_call(
        kern, name=name, grid=(width // LANE,), in_specs=[spec, spec], out_specs=spec,
        out_shape=jax.ShapeDtypeStruct((s_len, width), f32),
        compiler_params=pltpu.CompilerParams(dimension_semantics=("parallel",)))(a, b)


def _scan_bwd(name, a_next, h_prev, dh):
    s_len, width = dh.shape
    t = min(SCAN_ROWS, s_len)
    n_blocks = s_len // t

    def kern(an_ref, hp_ref, dh_ref, da_ref, db_ref):
        def block(kk, carry):
            k = n_blocks - 1 - kk
            rows = pl.ds(pl.multiple_of(k * t, t), t)
            acc, g = _block_scan(an_ref[rows, :], dh_ref[rows, :], True)
            g = g + acc * carry
            db_ref[rows, :] = g
            da_ref[rows, :] = g * hp_ref[rows, :]
            return db_ref[pl.ds(k * t, 1), :]

        lax.fori_loop(0, n_blocks, block, jnp.zeros((1, LANE), f32))

    spec = pl.BlockSpec((s_len, LANE), lambda j: (0, j))
    return pl.pallas_call(
        kern, name=name, grid=(width // LANE,), in_specs=[spec, spec, spec], out_specs=[spec, spec],
        out_shape=[jax.ShapeDtypeStruct((s_len, width), f32)] * 2,
        compiler_params=pltpu.CompilerParams(dimension_semantics=("parallel",)))(a_next, h_prev, dh)


def _lane_cumsum(x, reverse):
    n = x.shape[1]
    lane = lax.broadcasted_iota(jnp.int32, x.shape, 1)
    sh = 1
    while sh < n:
        if reverse:
            x = x + jnp.where(lane < n - sh, pltpu.roll(x, n - sh, 1), 0.0)
        else:
            x = x + jnp.where(lane >= sh, pltpu.roll(x, sh, 1), 0.0)
        sh *= 2
    return x


def _decay_fwd(name, fl_t, b8):
    def kern(f_ref, b_ref, c_ref):
        c_ref[...] = _lane_cumsum(jax.nn.log_sigmoid(f_ref[...] + b_ref[...]), False)

    return pl.pallas_call(kern, name=name, out_shape=jax.ShapeDtypeStruct(fl_t.shape, f32))(fl_t, b8)


def _decay_bwd(name, fl_t, b8, dc_key, dc_query):
    def kern(f_ref, b_ref, dck_ref, dcq_ref, df_ref, db_ref):
        dlogf = _lane_cumsum(dck_ref[...] + dcq_ref[...], True)
        df = dlogf * jax.nn.sigmoid(-(f_ref[...] + b_ref[...]))
        df_ref[...] = df
        db_ref[...] = jnp.sum(df, axis=1, keepdims=True)

    return pl.pallas_call(kern, name=name, out_shape=[jax.ShapeDtypeStruct(fl_t.shape, f32),
                                                      jax.ShapeDtypeStruct((SUBLANE, 1), f32)])(fl_t, b8, dc_key, dc_query)


def _rms(x, g, n):
    return x * lax.rsqrt(jnp.sum(x * x, axis=-1, keepdims=True) * (1.0 / n) + EPS) * g


def _loss_head(name, h, target, g, tb=256):
    n, d = h.shape
    tb = min(tb, n)

    def kern(h_ref, t_ref, g_ref, loss_ref, dh_ref, dg_ref):
        i = pl.program_id(0)
        tgt = t_ref[...]

        def f(hv, gv):
            err = _rms(hv, gv, d) - tgt
            return 0.5 * jnp.sum(jnp.sum(err * err, axis=-1, keepdims=True) * (1.0 / d), axis=0, keepdims=True)

        val, vjp = jax.vjp(f, h_ref[...], g_ref[...])
        dh, dg = vjp(jnp.ones((1, 1), f32))
        dh_ref[...] = dh

        @pl.when(i == 0)
        def _():
            loss_ref[...] = jnp.zeros_like(loss_ref)
            dg_ref[...] = jnp.zeros_like(dg_ref)

        loss_ref[...] += val
        dg_ref[...] += dg

    return pl.pallas_call(
        kern, name=name, grid=(n // tb,),
        in_specs=[pl.BlockSpec((tb, d), lambda i: (i, 0)), pl.BlockSpec((tb, d), lambda i: (i, 0)),
                  pl.BlockSpec((1, d), lambda i: (0, 0))],
        out_specs=[pl.BlockSpec((1, 1), lambda i: (0, 0)), pl.BlockSpec((tb, d), lambda i: (i, 0)),
                   pl.BlockSpec((1, d), lambda i: (0, 0))],
        out_shape=[jax.ShapeDtypeStruct((1, 1), f32), jax.ShapeDtypeStruct((n, d), f32), jax.ShapeDtypeStruct((1, d), f32)],
        compiler_params=pltpu.CompilerParams(dimension_semantics=("arbitrary",)))(h, target, g)


def _f_norm(x, g):
    return (_rms(x, g, D_MODEL),)


def _f_latent(qc, kvc, gq, gkv):
    return _rms(qc, gq, MLA_Q_RANK), _rms(kvc, gkv, MLA_KV_RANK)


def _f_rope_table(pos, freq, m1, m2):
    ang = pos * freq
    sin = jnp.sin(ang)
    return jnp.cos(ang), -sin * m1, sin * m2


def _rope(x, cos, s_up, s_down):
    w = x.shape[1]
    return x * cos + _roll(x, w - MLA_ROPE // 2, 1) * s_up + _roll(x, MLA_ROPE // 2, 1) * s_down


def _f_mla_prep(q, kpart, kr, cos, s_up, s_down):
    def heads(t):
        return jnp.concatenate([t] * HEADS, axis=1)

    kr = _rope(kr, cos, s_up, s_down)
    return _rope(q, heads(cos), heads(s_up), heads(s_down)), kpart + heads(kr)


def _f_lru_gate(gates, xc, b_r, b_i, lam):
    r = jax.nn.sigmoid(gates[:, :LRU_WIDTH] + b_r)
    i = jax.nn.sigmoid(gates[:, LRU_WIDTH:] + b_i)
    log_a = -LRU_C * r * jax.nn.softplus(-lam)
    mult = jnp.sqrt(-jnp.tanh(log_a) * (1.0 + jnp.exp(2.0 * log_a)))
    return jnp.exp(log_a), mult * (i * xc)


def _f_merge(o_mla, o_fox, hs, lg, g):
    o_lru = hs * jax.nn.gelu(lg)
    return (jnp.concatenate([_rms(o_mla, g[:, :512], HEADS * MLA_V), _rms(o_fox, g[:, 512:1024], HEADS * FOX_HEAD_DIM),
                             _rms(o_lru, g[:, 1024:], LRU_WIDTH)], axis=1),)


def _f_ffn_gate(u):
    return (jax.nn.silu(u[:, :D_FF]) * u[:, D_FF:],)


def _f_ple(h, gpre, pp):
    return (h + jax.nn.sigmoid(gpre) * pp,)


MIX_PART = ["w_in", "w_uq", "w_ukv", "lru_conv_w"]
FFN_PART = ["w_o", "w_up", "ffn_conv_w", "w_down", "w_ple_gate", "w_ple_proj"]


def _prep_mix_weights(w):
    eye = jnp.eye(LRU_BLOCKS, dtype=f32)

    def block_diag(m):
        return (eye[:, None, :, None] * m[:, :, None, :]).reshape(LRU_WIDTH, LRU_WIDTH)

    return dict(
        w_in=_take_pad(w["w_in"], Z_MAP, 1),
        w_uq=_take_pad(_take_pad(w["w_uq"], UQ_COL_MAP, 1), UQ_ROW_MAP, 0),
        w_ukv=_take_pad(w["w_ukv"], UKV_MAP, 1),
        w_ri=jnp.concatenate([block_diag(w["w_r"]), block_diag(w["w_i"])], axis=1).astype(bf16),
        g_mix=w["g_mix"].reshape(1, -1), g_ffn=w["g_ffn"].reshape(1, -1), g_ple=w["g_ple"].reshape(1, -1),
        g_qc=_take_pad(w["g_qc"], UQ_ROW_MAP, 0).reshape(1, -1), g_kvc=w["g_kvc"].reshape(1, -1),
        g_out=_take_pad(w["g_out"], OMIX_MAP, 0).reshape(1, -1),
        b_f8=_take_pad(w["b_f"], _pad_to(np.arange(FOX_HEADS), SUBLANE), 0).reshape(SUBLANE, 1),
        lru_conv_w=w["lru_conv_w"], lru_conv_b=w["lru_conv_b"].reshape(1, -1),
        b_r=w["b_r"].reshape(1, -1), b_i=w["b_i"].reshape(1, -1), lam=w["lru_lambda"].reshape(1, -1),
        ffn_conv_b=w["ffn_conv_b"].reshape(1, -1),
    )


def _prep_ffn_weights(w):
    return dict(w_o=_take_pad(w["w_o"], OMIX_MAP, 0),
                w_up=w["w_up"], w_up_g=w["w_up"][:, :D_FF], w_up_v=w["w_up"][:, D_FF:], ffn_conv_w=w["ffn_conv_w"],
                w_down=w["w_down"], w_ple_gate=w["w_ple_gate"], w_ple_proj=w["w_ple_proj"])


def _rope_rows(pos):
    consts = [jnp.asarray(t) for t in _rope_tables(LANE, ROPE_AT)]
    return _rowwise("rope_table", _f_rope_table, [pos], consts, [(LANE, f32)] * 3)


def _key_decay(c_t, s_len):
    t = _att_tiles(s_len)[1]
    return c_t[:HEADS].reshape(HEADS, s_len // t, 1, t), c_t[:HEADS].reshape(HEADS, s_len, 1)


def _layer_fwd(l, h0, p_l, rope, weights_of):
    s_len = h0.shape[0]
    n = f"l{l}_"
    w = _prep_mix_weights(weights_of("mix", h0))
    xn, = _rowwise(n + "norm_mix", _f_norm, [h0], [w["g_mix"]], [(D_MODEL, bf16)])
    z = _mm(n + "in_proj", xn, w["w_in"])
    zq = (z, QC_W, Z_QC // QC_W)
    zkv = (z, LANE, Z_KVC // LANE)
    zkr = (z, LANE, Z_KR // LANE)
    zlx = (z, LRU_WIDTH, Z_LX // LRU_WIDTH)
    zlg = (z, LRU_WIDTH, Z_LG // LRU_WIDTH)
    qcn, kvn = _rowwise(n + "latent_norm", _f_latent, [zq, zkv], [w["g_qc"], w["g_kvc"]], [(QC_W, bf16), (LANE, bf16)])
    q = _mm(n + "uq", qcn, w["w_uq"])
    kv = _mm(n + "ukv", kvn, w["w_ukv"])
    kpart = (kv, HEADS * LANE, 0)
    qr, kk = _rowwise(n + "mla_prep", _f_mla_prep, [q, kpart, zkr, *rope], [],
                      [(HEADS * LANE, bf16), (HEADS * LANE, bf16)])
    mla_scale = (MLA_NOPE + MLA_ROPE) ** -0.5
    o_mla, lse_m = _attn_fwd(n + "mla_fwd", (qr, 0), (kk, 0), (kv, HEADS), mla_scale)
    fl_t = z[:, Z_FL:Z_FL + SUBLANE].T
    c_t = _decay_fwd(n + "decay", fl_t, w["b_f8"])
    c_row, c_col = _key_decay(c_t, s_len)
    fox_scale = FOX_HEAD_DIM ** -0.5
    o_fox, lse_f = _attn_fwd(n + "fox_fwd", (z, Z_FQ // LANE), (z, Z_FK // LANE), (z, Z_FV // LANE), fox_scale, c_row)
    xc = _conv_fwd(n + "lru_conv", zlx, w["lru_conv_w"], w["lru_conv_b"], LRU_CONV)
    gates = _mm(n + "lru_gates", xc, w["w_ri"])
    a, bx = _rowwise(n + "lru_gate", _f_lru_gate, [gates, xc], [w["b_r"], w["b_i"], w["lam"]],
                     [(LRU_WIDTH, f32), (LRU_WIDTH, f32)])
    hs = _scan_fwd(n + "lru_scan", a, bx)
    ocat, = _rowwise(n + "merge", _f_merge, [o_mla, o_fox, hs, zlg], [w["g_out"]], [(OMIX_W, bf16)])
    w.update(_prep_ffn_weights(weights_of("ffn", ocat)))
    h1 = _mm(n + "out_proj", ocat, w["w_o"], res=h0)
    xn2, = _rowwise(n + "norm_ffn", _f_norm, [h1], [w["g_ffn"]], [(D_MODEL, bf16)])
    up = _mm(n + "up_proj", xn2, w["w_up"])
    act = _ffn_act_fwd(n + "ffn_act", up, w["ffn_conv_w"], w["ffn_conv_b"])
    h2 = _mm(n + "down_proj", act, w["w_down"], res=h1)
    hn, = _rowwise(n + "norm_ple", _f_norm, [h2], [w["g_ple"]], [(D_MODEL, bf16)])
    gpre = _mm(n + "ple_gate", hn, w["w_ple_gate"])
    pp = _mm(n + "ple_proj", p_l, w["w_ple_proj"])
    h3, = _rowwise(n + "ple_mix", _f_ple, [h2, gpre, pp], [], [(D_MODEL, f32)])
    res = dict(h0=h0, xn=xn, z=z, qcn=qcn, kvn=kvn, q=q, kv=kv, qr=qr, kk=kk, o_mla=o_mla, lse_m=lse_m, fl_t=fl_t,
               c_row=c_row, c_col=c_col, o_fox=o_fox, lse_f=lse_f, xc=xc, gates=gates, a=a, hs=hs, ocat=ocat, h1=h1,
               xn2=xn2, up=up, act=act, h2=h2, hn=hn, gpre=gpre, pp=pp, p_l=p_l)
    return h3, res, w


def _layer_bwd(l, dh3, r, rope, w, token, grads_to):
    s_len = dh3.shape[0]
    n = f"l{l}_"
    g = {}
    w = dict(w, g_ple=w["g_ple"] + token)
    z = r["z"]
    zq = (z, QC_W, Z_QC // QC_W)
    zkv = (z, LANE, Z_KVC // LANE)
    zkr = (z, LANE, Z_KR // LANE)
    zlx = (z, LRU_WIDTH, Z_LX // LRU_WIDTH)
    zlg = (z, LRU_WIDTH, Z_LG // LRU_WIDTH)
    (dh2a, dgpre, dpp), _ = _rowwise_bwd(n + "ple_mix_b", _f_ple, [r["h2"], r["gpre"], r["pp"]], [], [dh3], 3,
                                         dts=[f32, bf16, bf16])
    g["w_ple_proj"] = _mm(n + "ple_proj_dw", r["p_l"], dpp, "tn", bf16)
    dhn = _mm(n + "ple_gate_dx", dgpre, w["w_ple_gate"], "nt")
    g["w_ple_gate"] = _mm(n + "ple_gate_dw", r["hn"], dgpre, "tn", bf16)
    (dh2,), (g["g_ple"],) = _rowwise_bwd(n + "norm_ple_b", _f_norm, [r["h2"]], [w["g_ple"]], [dhn], 1, adds={0: dh2a})
    dact = _mm(n + "down_dx", dh2, w["w_down"], "nt")
    g["w_down"] = _mm(n + "down_dw", r["act"], dh2, "tn", bf16)
    dup_g, dup_v, dcw_g, dcw_v, dcb_g, dcb_v = _ffn_act_bwd(n + "ffn_act_b", r["up"], dact, w["ffn_conv_w"], w["ffn_conv_b"])
    g["ffn_conv_w"] = jnp.concatenate([dcw_g, dcw_v], axis=1)
    g["ffn_conv_b"] = jnp.concatenate([dcb_g, dcb_v], axis=1)
    dxn2 = _mm(n + "up_dx_v", dup_v, w["w_up_v"], "nt", res=_mm(n + "up_dx_g", dup_g, w["w_up_g"], "nt"))
    g["w_up"] = jnp.concatenate([_mm(n + "up_dw_g", r["xn2"], dup_g, "tn", bf16),
                                 _mm(n + "up_dw_v", r["xn2"], dup_v, "tn", bf16)], axis=1)
    (dh1,), (g["g_ffn"],) = _rowwise_bwd(n + "norm_ffn_b", _f_norm, [r["h1"]], [w["g_ffn"]], [dxn2], 1, adds={0: dh2})
    docat = _mm(n + "out_dx", dh1, w["w_o"], "nt")
    g["w_o"] = _mm(n + "out_dw", r["ocat"], dh1, "tn", bf16)
    token = grads_to("ffn", dict(w_o=_take_inv(g["w_o"], OMIX_MAP, 0), w_up=g["w_up"], ffn_conv_w=g["ffn_conv_w"],
                                 w_down=g["w_down"], w_ple_gate=g["w_ple_gate"], w_ple_proj=g["w_ple_proj"]))
    w = dict(w, g_out=w["g_out"] + token)
    (do_mla, do_fox, dhs, dlg), (g["g_out"],) = _rowwise_bwd(
        n + "merge_b", _f_merge, [r["o_mla"], r["o_fox"], r["hs"], zlg], [w["g_out"]], [docat], 4)
    a, hs = r["a"], r["hs"]
    a_next = jnp.concatenate([a[1:], jnp.zeros((1, LRU_WIDTH), f32)], axis=0)
    h_prev = jnp.concatenate([jnp.zeros((1, LRU_WIDTH), f32), hs[:-1]], axis=0)
    da, dbx = _scan_bwd(n + "lru_scan_b", a_next, h_prev, dhs)
    (dgates, dxc_a), (g["b_r"], g["b_i"], g["lam"]) = _rowwise_bwd(
        n + "lru_gate_b", _f_lru_gate, [r["gates"], r["xc"]], [w["b_r"], w["b_i"], w["lam"]], [da, dbx], 2,
        dts=[bf16, f32])
    dxc_b = _mm(n + "lru_gates_dx", dgates, w["w_ri"], "nt")
    g["w_ri"] = _mm(n + "lru_gates_dw", r["xc"], dgates, "tn")
    dlx, g["lru_conv_w"], g["lru_conv_b"] = _conv_bwd(n + "lru_conv_b", zlx, dxc_a, w["lru_conv_w"], LRU_CONV, dout2=dxc_b)
    fox_scale = FOX_HEAD_DIM ** -0.5
    fq, fk, fv = (z, Z_FQ // LANE), (z, Z_FK // LANE), (z, Z_FV // LANE)
    dfq, delta_f, dc_q = _attn_dq(n + "fox_dq", fq, fk, fv, r["o_fox"], do_fox, r["lse_f"], fox_scale, r["c_row"])
    dfk, dfv, dc_k = _attn_dkv(n + "fox_dkv", fq, fk, fv, do_fox, _query_rows(r["lse_f"]), _query_rows(delta_f), fox_scale,
                               r["c_col"])
    pad_rows = jnp.zeros((SUBLANE - HEADS, s_len), f32)
    dfl_t, g["b_f8"] = _decay_bwd(n + "decay_b", r["fl_t"], w["b_f8"],
                                  jnp.concatenate([dc_k.reshape(HEADS, s_len), pad_rows], axis=0),
                                  jnp.concatenate([dc_q.reshape(HEADS, s_len), pad_rows], axis=0))
    dfl = jnp.pad(dfl_t.T, ((0, 0), (0, LANE - SUBLANE)))
    mla_scale = (MLA_NOPE + MLA_ROPE) ** -0.5
    qr, kk, kv = (r["qr"], 0), (r["kk"], 0), (r["kv"], HEADS)
    dqr, delta_m, _ = _attn_dq(n + "mla_dq", qr, kk, kv, r["o_mla"], do_mla, r["lse_m"], mla_scale)
    dkk, dv_m = _attn_dkv(n + "mla_dkv", qr, kk, kv, do_mla, _query_rows(r["lse_m"]), _query_rows(delta_m), mla_scale)
    (dq, dkpart, dkr), _ = _rowwise_bwd(n + "mla_prep_b", _f_mla_prep, [r["q"], (r["kv"], HEADS * LANE, 0), zkr, *rope],
                                        [], [dqr, dkk], 3, dts=[bf16, bf16, f32])
    dkv = jnp.concatenate([dkpart, dv_m.astype(bf16)], axis=1)
    dkvn = _mm(n + "ukv_dx", dkv, w["w_ukv"], "nt")
    g["w_ukv"] = _mm(n + "ukv_dw", r["kvn"], dkv, "tn", bf16)
    dqcn = _mm(n + "uq_dx", dq, w["w_uq"], "nt")
    g["w_uq"] = _mm(n + "uq_dw", r["qcn"], dq, "tn", bf16)
    (dqc, dkvc), (g["g_qc"], g["g_kvc"]) = _rowwise_bwd(n + "latent_norm_b", _f_latent, [zq, zkv],
                                                        [w["g_qc"], w["g_kvc"]], [dqcn, dkvn], 2)
    dz = jnp.concatenate([t.astype(bf16) for t in (dfq, dfk, dfv, dlx, dlg, dqc, dkvc, dkr, dfl)], axis=1)
    dxn = _mm(n + "in_dx", dz, w["w_in"], "nt")
    g["w_in"] = _mm(n + "in_dw", r["xn"], dz, "tn", bf16)
    (dh0,), (g["g_mix"],) = _rowwise_bwd(n + "norm_mix_b", _f_norm, [r["h0"]], [w["g_mix"]], [dxn], 1, adds={0: dh1})
    return dh0, grads_to("mix", _unpad_mix_grads(g))


def _unpad_mix_grads(g):
    d_ri = g["w_ri"]
    idx = jnp.arange(LRU_BLOCKS)

    def diag_blocks(m):
        return m.reshape(LRU_BLOCKS, LRU_BLOCK, LRU_BLOCKS, LRU_BLOCK)[idx, :, idx, :]

    return dict(
        g_mix=g["g_mix"][0], w_in=_take_inv(g["w_in"], Z_MAP, 1), g_qc=g["g_qc"][0, :MLA_Q_RANK],
        w_uq=_take_inv(g["w_uq"][:MLA_Q_RANK], UQ_COL_MAP, 1), g_kvc=g["g_kvc"][0],
        w_ukv=_take_inv(g["w_ukv"], UKV_MAP, 1), b_f=g["b_f8"][:FOX_HEADS, 0],
        lru_conv_w=g["lru_conv_w"], lru_conv_b=g["lru_conv_b"][0],
        w_r=diag_blocks(d_ri[:, :LRU_WIDTH]), b_r=g["b_r"][0], w_i=diag_blocks(d_ri[:, LRU_WIDTH:]), b_i=g["b_i"][0],
        lru_lambda=g["lam"][0], g_out=_take_inv(g["g_out"][0], OMIX_MAP, 0),
        g_ffn=g["g_ffn"][0], ffn_conv_b=g["ffn_conv_b"][0], g_ple=g["g_ple"][0],
    )


LAYER_WEIGHTS = ["g_mix", "w_in", "g_qc", "w_uq", "g_kvc", "w_ukv", "b_f", "lru_conv_w", "lru_conv_b", "w_r", "b_r", "w_i",
                 "b_i", "lru_lambda", "g_out", "w_o", "g_ffn", "w_up", "ffn_conv_w", "ffn_conv_b", "w_down", "g_ple",
                 "w_ple_gate", "w_ple_proj"]
WEIGHTS = LAYER_WEIGHTS + ["g_final"]


def _local_step(x, p, pos, target, g_final, weights_of, grads_to):
    h = x
    rope = _rope_rows(pos)
    ws, saved = [], []
    for l in range(DEPTH):
        h, r, w = _layer_fwd(l, h, p[l], rope, functools.partial(weights_of, l))
        ws.append(w)
        saved.append(r)
    loss, dh, dg_final = _loss_head("loss_head", h, target, g_final.reshape(1, -1))
    token = jnp.zeros((), f32)
    for l in reversed(range(DEPTH)):
        dh, token = _layer_bwd(l, dh, saved[l], rope, ws[l], token, functools.partial(grads_to, l))
    return loss[0, 0], dh, dg_final[0]


MESH_AXES = ("x", "y", "c")


def _row_tile(rows, cap):
    if rows <= cap:
        return rows
    for t in range(cap, SUBLANE - 1, -SUBLANE):
        if rows % t == 0:
            return t
    return rows


ADAM_BLOCK_BYTES = 2 ** 20


def _adamw(name, w, g, m, v):
    rows, cols = w.shape
    tr = _row_tile(rows, max(SUBLANE, ADAM_BLOCK_BYTES // (4 * cols) // SUBLANE * SUBLANE))

    def kern(w_ref, g_ref, m_ref, v_ref, d_ref, nm_ref, nv_ref):
        gv = g_ref[...]
        nm = ADAM_B1 * m_ref[...] + (1.0 - ADAM_B1) * gv
        nv = ADAM_B2 * v_ref[...] + (1.0 - ADAM_B2) * (gv * gv)
        m_hat = nm / (1.0 - ADAM_B1 ** ADAM_STEP)
        v_hat = nv / (1.0 - ADAM_B2 ** ADAM_STEP)
        d_ref[...] = -ADAM_LR * (m_hat / (jnp.sqrt(v_hat) + ADAM_EPS) + ADAM_WD * w_ref[...])
        nm_ref[...] = nm
        nv_ref[...] = nv

    spec = pl.BlockSpec((tr, cols), lambda i: (i, 0))
    return pl.pallas_call(
        kern, name=name, grid=(rows // tr,), in_specs=[spec] * 4, out_specs=[spec] * 3,
        out_shape=[jax.ShapeDtypeStruct((rows, cols), f32)] * 3,
        compiler_params=pltpu.CompilerParams(dimension_semantics=("parallel",)))(w, g, m, v)


def _packed_rows(shape):
    return -(-int(np.prod(shape)) // (SUBLANE * LANE)) * SUBLANE


def _pack(arrays):
    rows = []
    for a in arrays:
        flat = a.reshape(-1)
        rows.append(jnp.pad(flat, (0, _packed_rows(a.shape) * LANE - flat.shape[0])).reshape(-1, LANE))
    return jnp.concatenate(rows, axis=0)


def _unpack(buf, shapes):
    out, at = [], 0
    for s in shapes:
        rows = _packed_rows(s)
        out.append(buf[at:at + rows].reshape(-1)[:int(np.prod(s))].reshape(s))
        at += rows
    return out


SHARD_AXIS = {"w_in": 2, "w_uq": 2, "w_ukv": 2, "lru_conv_w": 2, "w_o": 1, "w_up": 2, "ffn_conv_w": 2, "w_down": 1,
              "w_ple_gate": 1, "w_ple_proj": 2}
SHARDED = [k for k in WEIGHTS if k in SHARD_AXIS]
REPLICATED = [k for k in WEIGHTS if k not in SHARD_AXIS]
ELEMENTWISE_F32 = ("lru_conv_w", "ffn_conv_w")
N_SHARDS = 4
BF16_TILE_ROWS = 16


HBM_SPEC = pl.BlockSpec(memory_space=pl.ANY)
SEM_SPEC = pl.BlockSpec(memory_space=pltpu.SEMAPHORE)
SPLIT_EFFECT = pltpu.SideEffectType.DATAFLOW_SIDE_EFFECTING
CHIP_FLIPS = ((1, 0), (0, 1), (1, 1))
N_DEVICES = 8
SUM_BLOCK_BYTES = 4 * 2 ** 20


def _device_index():
    return 4 * lax.axis_index("x") + 2 * lax.axis_index("y") + lax.axis_index("c")


def _when(cond, fn):
    if cond is None:
        fn()
    else:
        pl.when(cond)(fn)


class _Exchange:
    def __init__(self, name, plan, srcs, land_shapes, n_send, n_recv):
        self.name, self.plan, self.srcs, self.n = name, plan, list(srcs), len(srcs)
        self.land_shapes, self.n_send, self.n_recv = land_shapes, n_send, n_recv

    def run(self):
        n = self.n

        def body(*refs):
            sends, arrivals = self.plan(refs[:n], refs[n:2 * n], refs[2 * n], refs[2 * n + 1])
            for cond, cp in sends:
                _when(cond, cp.start)
            for cond, cp in arrivals:
                _when(cond, cp.wait_recv)
            for cond, cp in sends:
                _when(cond, cp.wait_send)

        return pl.pallas_call(
            body, name=self.name, out_shape=self.land_shapes, in_specs=[HBM_SPEC] * n, out_specs=[HBM_SPEC] * n,
            scratch_shapes=[pltpu.SemaphoreType.DMA((self.n_send,)), pltpu.SemaphoreType.DMA((self.n_recv,))])(*self.srcs)

    def start(self, after=None):
        n = self.n
        lands = [lax.empty(s.shape, s.dtype) for s in self.land_shapes]
        extra = [] if after is None else [after]

        def body(*refs):
            ins, lands_in = refs[:n], refs[n:2 * n]
            send_sems, recv_sems, token = refs[2 * n + len(extra)], refs[2 * n + len(extra) + 1], refs[-1]
            sends, _ = self.plan(ins, lands_in, send_sems, recv_sems)
            for cond, cp in sends:
                _when(cond, cp.start)
            token[...] = jnp.zeros_like(token)

        hbm = [pltpu.with_memory_space_constraint(a, pltpu.HBM) for a in self.srcs + lands]
        res = pl.pallas_call(
            body, name=self.name + "_start",
            out_shape=(pltpu.SemaphoreType.DMA((self.n_send,)), pltpu.SemaphoreType.DMA((self.n_recv,)),
                       *[pltpu.HBM(a.shape, a.dtype) for a in hbm], jax.ShapeDtypeStruct((SUBLANE, LANE), f32)),
            in_specs=[HBM_SPEC] * (2 * n + len(extra)),
            out_specs=(SEM_SPEC, SEM_SPEC, *[HBM_SPEC] * (2 * n), pl.BlockSpec(memory_space=pltpu.VMEM)),
            input_output_aliases={i: 2 + i for i in range(2 * n)},
            compiler_params=pltpu.CompilerParams(has_side_effects=SPLIT_EFFECT))(*hbm, *extra)
        self.sems, self.thru, token = res[:2], res[2:2 + 2 * n], res[-1]
        return token[0, 0]

    def finish(self, after):
        n = self.n

        def body(*refs):
            ins, lands_in, send_sems, recv_sems = refs[:n], refs[n:2 * n], refs[2 * n], refs[2 * n + 1]
            sends, arrivals = self.plan(ins, lands_in, send_sems, recv_sems)
            for cond, cp in arrivals:
                _when(cond, cp.wait_recv)
            for cond, cp in sends:
                _when(cond, cp.wait_send)

        res = pl.pallas_call(
            body, name=self.name + "_finish", out_shape=tuple(pltpu.HBM(a.shape, a.dtype) for a in self.thru),
            in_specs=[HBM_SPEC] * (2 * n) + [SEM_SPEC, SEM_SPEC, HBM_SPEC], out_specs=tuple([HBM_SPEC] * (2 * n)),
            input_output_aliases={i: i for i in range(2 * n)},
            compiler_params=pltpu.CompilerParams(has_side_effects=SPLIT_EFFECT))(*self.thru, *self.sems, after)
        return list(res[n:])


def _gather_exchange(name, shards):
    def plan(ins, lands, send_sems, recv_sems):
        x, y, c = (lax.axis_index(a) for a in MESH_AXES)
        copies = []
        for i in range(len(ins)):
            for k, (fx, fy) in enumerate(CHIP_FLIPS):
                peer = (1 - x if fx else x, 1 - y if fy else y, c)
                copies.append((None, pltpu.make_async_remote_copy(
                    src_ref=ins[i], dst_ref=lands[i].at[2 * x + y], send_sem=send_sems.at[3 * i + k],
                    recv_sem=recv_sems.at[3 * i + k], device_id=peer, device_id_type=pl.DeviceIdType.MESH)))
        return copies, copies

    n = len(shards)
    return _Exchange(name, plan, shards, [jax.ShapeDtypeStruct((N_SHARDS,) + s.shape, s.dtype) for s in shards], 3 * n, 3 * n)


def _scatter_exchange(name, layer, chunks):
    def plan(ins, lands, send_sems, recv_sems):
        x, y, c = (lax.axis_index(a) for a in MESH_AXES)
        me = _device_index()
        sends, arrivals = [], []
        for i in range(len(ins)):
            for j in range(N_SHARDS):
                target = (j // 2, j % 2, layer)
                remote = jnp.logical_not((x == target[0]) & (y == target[1]) & (c == layer))
                sends.append((remote, pltpu.make_async_remote_copy(
                    src_ref=ins[i].at[j], dst_ref=lands[i].at[me], send_sem=send_sems.at[N_SHARDS * i + j],
                    recv_sem=recv_sems.at[N_DEVICES * i + me], device_id=target, device_id_type=pl.DeviceIdType.MESH)))
            for s in range(N_DEVICES):
                arrivals.append(((c == layer) & (me != s), pltpu.make_async_remote_copy(
                    src_ref=ins[i].at[0], dst_ref=lands[i].at[s], send_sem=send_sems.at[0],
                    recv_sem=recv_sems.at[N_DEVICES * i + s], device_id=(x, y, c), device_id_type=pl.DeviceIdType.MESH)))
        return sends, arrivals

    n = len(chunks)
    lands = [jax.ShapeDtypeStruct((N_DEVICES,) + ch.shape[1:], ch.dtype) for ch in chunks]
    return _Exchange(name, plan, chunks, lands, N_SHARDS * n, N_DEVICES * n)


def _sum_contributions(name, got, mine):
    _, a, b = got.shape
    ta = _row_tile(a, max(SUBLANE, SUM_BLOCK_BYTES // (N_DEVICES * b * got.dtype.itemsize) // SUBLANE * SUBLANE))

    def kern(got_ref, mine_ref, o_ref):
        me = _device_index()
        acc = jnp.zeros(o_ref.shape, f32)
        for s in range(N_DEVICES):
            acc = acc + jnp.where(me == s, mine_ref[...].astype(f32), got_ref[s].astype(f32))
        o_ref[...] = acc

    return pl.pallas_call(
        kern, name=name, grid=(a // ta,),
        in_specs=[pl.BlockSpec((N_DEVICES, ta, b), lambda i: (0, i, 0)), pl.BlockSpec((ta, b), lambda i: (i, 0))],
        out_specs=pl.BlockSpec((ta, b), lambda i: (i, 0)), out_shape=jax.ShapeDtypeStruct((a, b), f32),
        compiler_params=pltpu.CompilerParams(dimension_semantics=("parallel",)))(got, mine)


def _swap_layers(name, sums):
    n = len(sums[0])

    def body(*refs):
        srcs = (refs[:n], refs[n:2 * n])
        outs, (send_sems, recv_sems) = refs[2 * n:3 * n], refs[3 * n:]
        x, y, c = (lax.axis_index(a) for a in MESH_AXES)
        for i in range(n):
            for layer in range(DEPTH):
                cp = pltpu.make_async_remote_copy(
                    src_ref=srcs[layer][i], dst_ref=outs[i], send_sem=send_sems.at[i], recv_sem=recv_sems.at[i],
                    device_id=(x, y, 1 - c), device_id_type=pl.DeviceIdType.MESH)
                pl.when(c == layer)(cp.start)
        for i in range(n):
            pltpu.make_async_remote_copy(
                src_ref=srcs[0][i], dst_ref=outs[i], send_sem=send_sems.at[i], recv_sem=recv_sems.at[i],
                device_id=(x, y, 1 - c), device_id_type=pl.DeviceIdType.MESH).wait()

    return pl.pallas_call(
        body, name=name, out_shape=[jax.ShapeDtypeStruct(s.shape, s.dtype) for s in sums[0]],
        in_specs=[HBM_SPEC] * (2 * n), out_specs=[HBM_SPEC] * n,
        scratch_shapes=[pltpu.SemaphoreType.DMA((n,)), pltpu.SemaphoreType.DMA((n,))])(*sums[0], *sums[1])


def _stack_shards(g, axis):
    if axis == 1:
        return g.reshape(N_SHARDS, g.shape[0] // N_SHARDS, g.shape[1])
    return g.reshape(g.shape[0], N_SHARDS, g.shape[1] // N_SHARDS).transpose(1, 0, 2)


def _join_shards(s, axis):
    if axis == 1:
        return s.reshape(-1, s.shape[2])
    return s.transpose(1, 0, 2).reshape(s.shape[1], -1)


def _layer_shards(w, l, names):
    return [w[k][l] if k in ELEMENTWISE_F32 else w[k][l].astype(bf16) for k in names]


def _full_weights(names, sent, got):
    j = 2 * lax.axis_index("x") + lax.axis_index("y")
    return {k: _join_shards(lax.dynamic_update_slice(g, own[None], (j, 0, 0)), SHARD_AXIS[k])
            for k, own, g in zip(names, sent, got)}


def _grad_chunks(grads, names):
    return [_stack_shards(grads[k], SHARD_AXIS[k]).astype(bf16) for k in names]


def _sum_group(l, names, got, chunks):
    j = 2 * lax.axis_index("x") + lax.axis_index("y")
    return {k: _sum_contributions(f"sum_l{l}_{k}", g, lax.dynamic_index_in_dim(ch, j, 0, keepdims=False))
            for k, g, ch in zip(names, got, chunks)}


def _both_layers(name, names, sums):
    c = lax.axis_index("c")
    mine = [[sums[l][k] for k in names] for l in range(DEPTH)]
    other = _swap_layers(name, mine)
    return {k: jnp.stack([jnp.where(c == 0, mine[0][i], other[i]), jnp.where(c == 0, other[i], mine[1][i])])
            for i, k in enumerate(names)}


def _gather_all_exchange(name, src):
    def plan(ins, lands, send_sems, recv_sems):
        coords = [lax.axis_index(a) for a in MESH_AXES]
        me = _device_index()
        sends, arrivals = [], []
        for f in range(1, N_DEVICES):
            peer = tuple(1 - cd if (f >> (2 - b)) & 1 else cd for b, cd in enumerate(coords))
            sends.append((None, pltpu.make_async_remote_copy(
                src_ref=ins[0], dst_ref=lands[0].at[me], send_sem=send_sems.at[f - 1], recv_sem=recv_sems.at[me],
                device_id=peer, device_id_type=pl.DeviceIdType.MESH)))
        for s in range(N_DEVICES):
            arrivals.append((me != s, pltpu.make_async_remote_copy(
                src_ref=ins[0], dst_ref=lands[0].at[s], send_sem=send_sems.at[0], recv_sem=recv_sems.at[s],
                device_id=tuple(coords), device_id_type=pl.DeviceIdType.MESH)))
        return sends, arrivals

    return _Exchange(name, plan, [src], [jax.ShapeDtypeStruct((N_DEVICES,) + src.shape, src.dtype)], N_DEVICES - 1, N_DEVICES)


def kernel(x, p, positions, g_mix, w_in, g_qc, w_uq, g_kvc, w_ukv, b_f, lru_conv_w, lru_conv_b, w_r, b_r, w_i, b_i, lru_lambda, g_out, w_o, g_ffn, w_up, ffn_conv_w, ffn_conv_b, w_down, g_ple, w_ple_gate, w_ple_proj, g_final, loss_target, m_g_mix, m_w_in, m_g_qc, m_w_uq, m_g_kvc, m_w_ukv, m_b_f, m_lru_conv_w, m_lru_conv_b, m_w_r, m_b_r, m_w_i, m_b_i, m_lru_lambda, m_g_out, m_w_o, m_g_ffn, m_w_up, m_ffn_conv_w, m_ffn_conv_b, m_w_down, m_g_ple, m_w_ple_gate, m_w_ple_proj, m_g_final, v_g_mix, v_w_in, v_g_qc, v_w_uq, v_g_kvc, v_w_ukv, v_b_f, v_lru_conv_w, v_lru_conv_b, v_w_r, v_b_r, v_w_i, v_b_i, v_lru_lambda, v_g_out, v_w_o, v_g_ffn, v_w_up, v_ffn_conv_w, v_ffn_conv_b, v_w_down, v_g_ple, v_w_ple_gate, v_w_ple_proj, v_g_final):
    given = locals()
    w = {k: given[k] for k in WEIGHTS}
    m = {k: given["m_" + k] for k in WEIGHTS}
    v = {k: given["v_" + k] for k in WEIGHTS}

    parts = {"mix": MIX_PART, "ffn": FFN_PART}
    groups = [(l, part) for l in range(DEPTH) for part in ("mix", "ffn")]
    sent = {g: _layer_shards(w, g[0], parts[g[1]]) for g in groups}
    first = _gather_exchange("gather_l0_mix", sent[groups[0]]).run()
    ahead = {g: _gather_exchange(f"gather_l{g[0]}_{g[1]}", sent[g]) for g in groups[1:]}
    pos = positions[0].astype(f32).reshape(-1, 1)
    for ex in ahead.values():
        pos = pos + ex.start(after=first[0])
    behind, layer_grads, chunks = {}, [{} for _ in range(DEPTH)], {}

    def weights_of(l, part, after):
        g = (l, part)
        full = _full_weights(parts[part], sent[g], first if g == groups[0] else ahead[g].finish(after=after))
        if part == "mix":
            full.update({k: w[k][l] for k in LAYER_WEIGHTS if k in REPLICATED})
        return full

    def grads_to(l, part, grads):
        g = (l, part)
        layer_grads[l].update(grads)
        chunks[g] = _grad_chunks(grads, parts[part])
        if g == groups[0]:
            return jnp.zeros((), f32)
        behind[g] = _scatter_exchange(f"scatter_l{l}_{part}", l, chunks[g])
        return behind[g].start()

    loss, dx, dg_final = _local_step(x[0], p[:, 0], pos, loss_target[0], w["g_final"], weights_of, grads_to)

    grads = {k: jnp.stack([layer_grads[l][k] for l in range(DEPTH)]) for k in LAYER_WEIGHTS if k in REPLICATED}
    grads["g_final"] = dg_final
    rep_shapes = [w[k].shape for k in REPLICATED] + [(1,)]
    contrib = _pack([grads[k] for k in REPLICATED] + [loss.reshape(1)])
    last = _scatter_exchange("scatter_l0_mix", 0, chunks[groups[0]])
    everyone = _gather_all_exchange("gather_replicated", contrib)
    started = (last.start() + everyone.start() + dx[0, 0]).reshape(1, 1)

    def adamw_of(names, g_sharded):
        out = {}
        for k in names:
            shape = w[k].shape
            flat = [t.reshape(-1, shape[-1]) for t in (w[k], g_sharded[k], m[k], v[k])]
            out[k] = [t.reshape(shape) for t in (flat[1],) + tuple(_adamw("adamw_" + k, *flat))]
        return out

    sums = [{} for _ in range(DEPTH)]
    for g in groups[1:]:
        sums[g[0]].update(_sum_group(g[0], parts[g[1]], behind[g].finish(after=started), chunks[g]))
    big = adamw_of(FFN_PART, _both_layers("swap_ffn", FFN_PART, sums))
    sums[0].update(_sum_group(0, MIX_PART, last.finish(after=big[FFN_PART[0]][1]), chunks[groups[0]]))
    big.update(adamw_of(MIX_PART, _both_layers("swap_mix", MIX_PART, sums)))

    g_rep = _sum_contributions("sum_replicated", everyone.finish(after=big[MIX_PART[0]][1])[0], contrib)
    zero = jnp.zeros((1,), f32)
    w_rep, m_rep, v_rep = (_pack([t[k] for k in REPLICATED] + [zero]) for t in (w, m, v))
    rep = [_unpack(b, rep_shapes) for b in (g_rep,) + tuple(_adamw("adamw_replicated", w_rep, g_rep, m_rep, v_rep))]

    outs = []
    for kind in range(4):
        by_name = {k: big[k][kind] for k in SHARDED}
        by_name.update(zip(REPLICATED, rep[kind][:-1]))
        outs.append([by_name[k] for k in WEIGHTS])
    total_loss = rep[0][-1][0]
    return (total_loss, dx.reshape(x.shape), *outs[0], *outs[1], *outs[2], *outs[3])
```

```python
import functools
import math

import numpy as np
import jax
import jax.numpy as jnp
from jax import lax
from jax.experimental import pallas as pl
from jax.experimental.pallas import tpu as pltpu

f32, bf16 = jnp.float32, jnp.bfloat16

D_MODEL = 1024
PLE_DIM = 256
MLA_HEADS, MLA_NOPE, MLA_ROPE, MLA_V = 4, 64, 32, 64
MLA_Q_RANK, MLA_KV_RANK = 192, 128
FOX_HEADS, FOX_HEAD_DIM = 4, 64
LRU_WIDTH, LRU_BLOCKS, LRU_BLOCK, LRU_CONV, LRU_C = 512, 8, 64, 4, 8.0
D_FF, FFN_CONV = 2816, 3
ROPE_THETA = 10000.0
EPS = 1e-6
DEPTH = 2
ADAM_LR, ADAM_B1, ADAM_B2, ADAM_EPS, ADAM_WD, ADAM_STEP = 0.001, 0.9, 0.999, 1e-08, 0.01, 10

LANE = 128
SUBLANE = 8
HEADS = 4

Z_FQ, Z_FK, Z_FV, Z_LX, Z_LG, Z_QC, Z_KVC, Z_KR, Z_FL, Z_W = 0, 512, 1024, 1536, 2048, 2560, 2816, 2944, 3072, 3200
QC_W = 256
ROPE_AT = 64


def _head_pad_map(n_heads, width):
    m = -np.ones(n_heads * LANE, np.int64)
    for h in range(n_heads):
        m[h * LANE:h * LANE + width] = h * width + np.arange(width)
    return m


def _z_map():
    m = -np.ones(Z_W, np.int64)
    o_qc, o_kvc, o_kr = 0, MLA_Q_RANK, MLA_Q_RANK + MLA_KV_RANK
    o_fq = o_kr + MLA_ROPE
    o_fk, o_fv = o_fq + 256, o_fq + 512
    o_fl = o_fv + 256
    o_lx = o_fl + FOX_HEADS
    o_lg = o_lx + LRU_WIDTH
    m[Z_FQ:Z_FQ + 512] = np.where(_head_pad_map(4, 64) >= 0, _head_pad_map(4, 64) + o_fq, -1)
    m[Z_FK:Z_FK + 512] = np.where(_head_pad_map(4, 64) >= 0, _head_pad_map(4, 64) + o_fk, -1)
    m[Z_FV:Z_FV + 512] = np.where(_head_pad_map(4, 64) >= 0, _head_pad_map(4, 64) + o_fv, -1)
    m[Z_LX:Z_LX + 512] = o_lx + np.arange(512)
    m[Z_LG:Z_LG + 512] = o_lg + np.arange(512)
    m[Z_QC:Z_QC + MLA_Q_RANK] = o_qc + np.arange(MLA_Q_RANK)
    m[Z_KVC:Z_KVC + MLA_KV_RANK] = o_kvc + np.arange(MLA_KV_RANK)
    m[Z_KR + ROPE_AT:Z_KR + ROPE_AT + MLA_ROPE] = o_kr + np.arange(MLA_ROPE)
    m[Z_FL:Z_FL + FOX_HEADS] = o_fl + np.arange(FOX_HEADS)
    return m


def _ukv_map():
    m = -np.ones(2 * HEADS * LANE, np.int64)
    for h in range(HEADS):
        m[h * LANE:h * LANE + MLA_NOPE] = h * (MLA_NOPE + MLA_V) + np.arange(MLA_NOPE)
        m[HEADS * LANE + h * LANE:HEADS * LANE + h * LANE + MLA_V] = h * (MLA_NOPE + MLA_V) + MLA_NOPE + np.arange(MLA_V)
    return m


def _omix_map():
    return np.concatenate([_head_pad_map(4, 64), np.where(_head_pad_map(4, 64) >= 0, _head_pad_map(4, 64) + 256, -1),
                           512 + np.arange(512)])


def _pad_to(m, n):
    return np.concatenate([m, -np.ones(n - m.shape[0], np.int64)])


def _take_pad(a, m, axis):
    out = jnp.take(a, jnp.asarray(np.maximum(m, 0), jnp.int32), axis=axis)
    shape = [1] * a.ndim
    shape[axis] = m.shape[0]
    return out * jnp.asarray((m >= 0).reshape(shape), a.dtype)


def _take_inv(a, m, axis):
    n = int(m.max()) + 1
    inv = np.zeros(n, np.int64)
    inv[m[m >= 0]] = np.nonzero(m >= 0)[0]
    return jnp.take(a, jnp.asarray(inv, jnp.int32), axis=axis)


Z_MAP = _z_map()
UQ_COL_MAP = _head_pad_map(HEADS, MLA_NOPE + MLA_ROPE)
UQ_ROW_MAP = _pad_to(np.arange(MLA_Q_RANK), QC_W)
UKV_MAP = _ukv_map()
OMIX_MAP = _omix_map()
OMIX_W = 1536


def _rope_tables(width, at):
    half = MLA_ROPE // 2
    inv = ROPE_THETA ** (-np.arange(half, dtype=np.float32) / half)
    freq = np.zeros((1, width), np.float32)
    m1 = np.zeros((1, width), np.float32)
    m2 = np.zeros((1, width), np.float32)
    for h in range(width // LANE):
        b = h * LANE + at
        freq[0, b:b + half] = inv
        freq[0, b + half:b + 2 * half] = inv
        m1[0, b:b + half] = 1.0
        m2[0, b + half:b + 2 * half] = 1.0
    return freq, m1, m2


def _view(r):
    return r if isinstance(r, tuple) else (r, r.shape[1], 0)


def _blk(dim, cap):
    if dim <= cap:
        return dim
    for b in range(cap, LANE - 1, -LANE):
        if dim % b == 0:
            return b
    return dim


@functools.partial(jax.custom_vjp, nondiff_argnums=(1, 2))
def _roll(x, shift, axis):
    return pltpu.roll(x, shift, axis)


def _roll_fwd(x, shift, axis):
    return pltpu.roll(x, shift, axis), None


def _roll_bwd(shift, axis, _, g):
    return (pltpu.roll(g, g.shape[axis] - shift, axis),)


_roll.defvjp(_roll_fwd, _roll_bwd)


def _rowwise(name, fn, rows, pars, outs, tb=256):
    rows = [_view(r) for r in rows]
    n = rows[0][0].shape[0]
    tb = min(tb, n)
    nr, npar = len(rows), len(pars)

    def kern(*refs):
        r = [refs[k][...].astype(f32) for k in range(nr)]
        p = [refs[nr + k][...] for k in range(npar)]
        res = fn(*r, *p)
        for o_ref, o in zip(refs[nr + npar:], res):
            o_ref[...] = o.astype(o_ref.dtype)

    in_specs = [pl.BlockSpec((tb, w), lambda i, j=idx: (i, j)) for (_, w, idx) in rows]
    in_specs += [pl.BlockSpec(p.shape, lambda i: (0, 0)) for p in pars]
    out_specs = [pl.BlockSpec((tb, w), lambda i: (i, 0)) for (w, _) in outs]
    out_shape = [jax.ShapeDtypeStruct((n, w), dt) for (w, dt) in outs]
    return pl.pallas_call(kern, name=name, grid=(n // tb,), in_specs=in_specs, out_specs=out_specs, out_shape=out_shape,
                          compiler_params=pltpu.CompilerParams(dimension_semantics=("parallel",)))(*[r[0] for r in rows], *pars)


def _rowwise_bwd(name, fn, rows, pars, cts, ndiff, adds=None, tb=256, dts=None):
    rows = [_view(r) for r in rows]
    dts = dts or [f32] * ndiff
    adds = adds or {}
    add_keys = sorted(adds)
    n = rows[0][0].shape[0]
    tb = min(tb, n)
    nr, npar, nct, nadd = len(rows), len(pars), len(cts), len(add_keys)

    def kern(*refs):
        i = pl.program_id(0)
        r = [refs[k][...].astype(f32) for k in range(nr)]
        p = [refs[nr + k][...] for k in range(npar)]
        ct = [refs[nr + npar + k][...].astype(f32) for k in range(nct)]
        ad = {key: refs[nr + npar + nct + k][...] for k, key in enumerate(add_keys)}
        o_refs = refs[nr + npar + nct + nadd:]

        def g(*d):
            return tuple(fn(*d[:ndiff], *r[ndiff:], *d[ndiff:]))

        _, vjp = jax.vjp(g, *r[:ndiff], *p)
        grads = vjp(tuple(ct))
        for k in range(ndiff):
            gk = grads[k]
            if k in ad:
                gk = gk + ad[k]
            o_refs[k][...] = gk.astype(o_refs[k].dtype)

        @pl.when(i == 0)
        def _():
            for k in range(npar):
                o_refs[ndiff + k][...] = jnp.zeros_like(o_refs[ndiff + k])

        for k in range(npar):
            o_refs[ndiff + k][...] += grads[ndiff + k]

    in_specs = [pl.BlockSpec((tb, w), lambda i, j=idx: (i, j)) for (_, w, idx) in rows]
    in_specs += [pl.BlockSpec(p.shape, lambda i: (0, 0)) for p in pars]
    in_specs += [pl.BlockSpec((tb, c.shape[1]), lambda i: (i, 0)) for c in cts]
    in_specs += [pl.BlockSpec((tb, adds[k].shape[1]), lambda i: (i, 0)) for k in add_keys]
    out_specs = [pl.BlockSpec((tb, rows[k][1]), lambda i: (i, 0)) for k in range(ndiff)]
    out_specs += [pl.BlockSpec(p.shape, lambda i: (0, 0)) for p in pars]
    out_shape = [jax.ShapeDtypeStruct((n, rows[k][1]), dts[k]) for k in range(ndiff)]
    out_shape += [jax.ShapeDtypeStruct(p.shape, f32) for p in pars]
    res = pl.pallas_call(kern, name=name, grid=(n // tb,), in_specs=in_specs, out_specs=out_specs, out_shape=out_shape,
                         compiler_params=pltpu.CompilerParams(dimension_semantics=("arbitrary",)))(
        *[r[0] for r in rows], *pars, *cts, *[adds[k] for k in add_keys])
    return res[:ndiff], res[ndiff:]


_DOT_DIMS = {"nn": ((1,), (0,)), "nt": ((1,), (1,)), "tn": ((0,), (0,))}

MM_VMEM_BUDGET = 36 * 2 ** 20
MM_MAX_TM = 1408
MM_STEP, MM_RESULT, MM_XPOSE, MM_CAST = 700.0, 7.5e-4, 9e-4, 1e-3


def _tile_candidates(dim):
    c = [d for d in range(LANE, dim + 1, LANE) if dim % d == 0]
    return c or [dim]


@functools.lru_cache(maxsize=None)
def _mm_tiles(mode, m, n, k, a_bytes, b_bytes, o_bytes):
    best, best_cost = None, None
    for tm in _tile_candidates(m):
        if tm > MM_MAX_TM:
            continue
        for tn in _tile_candidates(n):
            for tk in _tile_candidates(k):
                vmem = 2 * (tm * tk * a_bytes + tk * tn * b_bytes + tm * tn * o_bytes) + 4 * tm * tn * (2 if tk < k else 1)
                vmem += (2 * tm * tk if a_bytes > 2 else 0) + (2 * tk * tn if b_bytes > 2 else 0)
                if vmem > MM_VMEM_BUDGET:
                    continue
                steps = (m // tm) * (n // tn) * (k // tk)
                cost = steps * MM_STEP + m * n * (k // tk) * MM_RESULT
                if mode == "tn":
                    cost += m * k * (n // tn) * MM_XPOSE
                cost += (m * k * (n // tn) * MM_CAST if a_bytes > 2 else 0) + (k * n * (m // tm) * MM_CAST if b_bytes > 2 else 0)
                if best is None or cost < best_cost:
                    best, best_cost = (tm, tn, tk), cost
    return best


def _mm(name, a, b, mode="nn", out_dtype=f32, res=None):
    if mode == "nn":
        (m, k), (_, n) = a.shape, b.shape
    elif mode == "nt":
        (m, k), (n, _) = a.shape, b.shape
    else:
        (k, m), (_, n) = a.shape, b.shape
    has_res = res is not None
    tm, tn, tk = _mm_tiles(mode, m, n, k, a.dtype.itemsize, b.dtype.itemsize,
                           jnp.dtype(out_dtype).itemsize + (res.dtype.itemsize if has_res else 0))
    nk = k // tk
    dims = (_DOT_DIMS[mode], ((), ()))

    def kern(*refs):
        a_ref, b_ref = refs[0], refs[1]
        o_ref, acc_ref = refs[-2], refs[-1]
        kk = pl.program_id(2)
        part = lax.dot_general(a_ref[...].astype(bf16), b_ref[...].astype(bf16), dims, preferred_element_type=f32)

        def finish(out):
            if has_res:
                out = out + refs[2][...]
            o_ref[...] = out.astype(o_ref.dtype)

        if nk == 1:
            finish(part)
            return

        @pl.when(kk == 0)
        def _():
            acc_ref[...] = part

        @pl.when(jnp.logical_and(kk > 0, kk < nk - 1))
        def _():
            acc_ref[...] += part

        @pl.when(kk == nk - 1)
        def _():
            finish(acc_ref[...] + part)

    if mode == "tn":
        a_spec = pl.BlockSpec((tk, tm), lambda i, j, kk: (kk, i))
    else:
        a_spec = pl.BlockSpec((tm, tk), lambda i, j, kk: (i, kk))
    if mode == "nt":
        b_spec = pl.BlockSpec((tn, tk), lambda i, j, kk: (j, kk))
    else:
        b_spec = pl.BlockSpec((tk, tn), lambda i, j, kk: (kk, j))
    in_specs = [a_spec, b_spec]
    args = [a, b]
    if has_res:
        in_specs.append(pl.BlockSpec((tm, tn), lambda i, j, kk: (i, j)))
        args.append(res)
    return pl.pallas_call(
        kern, name=name, grid=(m // tm, n // tn, nk), in_specs=in_specs,
        out_specs=pl.BlockSpec((tm, tn), lambda i, j, kk: (i, j)),
        out_shape=jax.ShapeDtypeStruct((m, n), out_dtype),
        scratch_shapes=[pltpu.VMEM((tm, tn) if nk > 1 else (SUBLANE, LANE), f32)],
        compiler_params=pltpu.CompilerParams(dimension_semantics=("parallel", "parallel", "arbitrary")))(*args)


ATT_TQ, ATT_TK = 512, 512


def _att_tiles(s_len):
    tk = min(ATT_TK, s_len)
    return min(ATT_TQ, tk), tk


def _fold_scale(scale):
    return (scale, 1.0) if math.frexp(scale)[0] == 0.5 else (1.0, scale)


def _query_rows(x):
    s_len = x.shape[1]
    tq = _att_tiles(s_len)[0]
    return x.reshape(HEADS, s_len // tq, 1, tq)


def _scores_t(kb, q_t, s_mul, ck, diag_offset, tq, tk):
    s = jnp.dot(kb, q_t, preferred_element_type=f32)
    if s_mul != 1.0:
        s = s * s_mul
    if ck is not None:
        s = s - ck
    if diag_offset is None:
        return s
    key = lax.broadcasted_iota(jnp.int32, (tk, tq), 0)
    query = lax.broadcasted_iota(jnp.int32, (tk, tq), 1) + diag_offset
    return jnp.where(key <= query, s, -jnp.inf)


def _scores(qb, kb, s_mul, ck, diagonal, t):
    s = lax.dot_general(qb, kb, (_DOT_DIMS["nt"], ((), ())), preferred_element_type=f32)
    if s_mul != 1.0:
        s = s * s_mul
    if ck is not None:
        s = s - ck
    if not diagonal:
        return s
    row = lax.broadcasted_iota(jnp.int32, (t, t), 0)
    col = lax.broadcasted_iota(jnp.int32, (t, t), 1)
    return jnp.where(col <= row, s, -jnp.inf)


def _attn_fwd(name, q, k, v, scale, c_row=None):
    (qa, qo), (ka, ko), (va, vo) = q, k, v
    s_len = qa.shape[0]
    t = _att_tiles(s_len)[1]
    nt = s_len // t
    decay = c_row is not None
    q_mul, s_mul = _fold_scale(scale)

    def kern(*refs):
        q_ref, k_ref, v_ref = refs[:3]
        o_ref, lse_ref = refs[-2:]
        i = pl.program_id(1)
        qb = (q_ref[...] * q_mul).astype(bf16)

        def step(j, carry, diagonal):
            m, l, acc = carry
            rows = pl.ds(pl.multiple_of(j * t, t), t)
            kb = k_ref[rows, :].astype(bf16)
            vb = v_ref[rows, :].astype(bf16)
            s = _scores(qb, kb, s_mul, refs[3][j] if decay else None, diagonal, t)
            m_new = jnp.maximum(m, jnp.max(s, axis=1, keepdims=True))
            alpha = jnp.exp(m - m_new)
            p = jnp.exp(s - m_new)
            l = alpha * l + jnp.sum(p, axis=1, keepdims=True)
            acc = alpha * acc + jnp.dot(p.astype(bf16), vb, preferred_element_type=f32)
            return m_new, l, acc

        init = (jnp.full((t, 1), -jnp.inf, f32), jnp.zeros((t, 1), f32), jnp.zeros((t, LANE), f32))
        m, l, acc = step(i, lax.fori_loop(0, i, lambda j, c: step(j, c, False), init), True)
        o_ref[...] = acc / l
        lse_ref[...] = m + jnp.log(l)

    in_specs = [pl.BlockSpec((t, LANE), lambda h, i: (i, qo + h)),
                pl.BlockSpec((s_len, LANE), lambda h, i: (0, ko + h)),
                pl.BlockSpec((s_len, LANE), lambda h, i: (0, vo + h))]
    args = [qa, ka, va]
    if decay:
        in_specs.append(pl.BlockSpec((None, nt, 1, t), lambda h, i: (h, 0, 0, 0)))
        args.append(c_row)
    return pl.pallas_call(
        kern, name=name, grid=(HEADS, nt), in_specs=in_specs,
        out_specs=[pl.BlockSpec((t, LANE), lambda h, i: (i, h)), pl.BlockSpec((None, t, 1), lambda h, i: (h, i, 0))],
        out_shape=[jax.ShapeDtypeStruct((s_len, HEADS * LANE), f32), jax.ShapeDtypeStruct((HEADS, s_len, 1), f32)],
        compiler_params=pltpu.CompilerParams(dimension_semantics=("parallel", "arbitrary")))(*args)


def _attn_dq(name, q, k, v, o, do, lse, scale, c_row=None):
    (qa, qo), (ka, ko), (va, vo) = q, k, v
    s_len = qa.shape[0]
    t = _att_tiles(s_len)[1]
    nt = s_len // t
    decay = c_row is not None
    q_mul, s_mul = _fold_scale(scale)

    def kern(*refs):
        q_ref, k_ref, v_ref, o_ref, do_ref, lse_ref = refs[:6]
        dq_ref, delta_ref, drow_ref = refs[-3:]
        i = pl.program_id(1)
        qb = (q_ref[...] * q_mul).astype(bf16)
        dob = do_ref[...]
        delta = jnp.sum(dob * o_ref[...], axis=1, keepdims=True)
        dob = dob.astype(bf16)
        lse = lse_ref[...]

        def step(j, carry, diagonal):
            dq, drow = carry
            rows = pl.ds(pl.multiple_of(j * t, t), t)
            kb = k_ref[rows, :].astype(bf16)
            vb = v_ref[rows, :].astype(bf16)
            s = _scores(qb, kb, s_mul, refs[6][j] if decay else None, diagonal, t)
            p = jnp.exp(s - lse)
            dp = lax.dot_general(dob, vb, (_DOT_DIMS["nt"], ((), ())), preferred_element_type=f32)
            ds = p * (dp - delta)
            return dq + jnp.dot(ds.astype(bf16), kb, preferred_element_type=f32), drow + jnp.sum(ds, axis=1, keepdims=True)

        init = (jnp.zeros((t, LANE), f32), jnp.zeros((t, 1), f32))
        dq, drow = step(i, lax.fori_loop(0, i, lambda j, c: step(j, c, False), init), True)
        dq_ref[...] = dq * scale
        delta_ref[...] = delta
        drow_ref[...] = drow

    in_specs = [pl.BlockSpec((t, LANE), lambda h, i: (i, qo + h)),
                pl.BlockSpec((s_len, LANE), lambda h, i: (0, ko + h)),
                pl.BlockSpec((s_len, LANE), lambda h, i: (0, vo + h)),
                pl.BlockSpec((t, LANE), lambda h, i: (i, h)),
                pl.BlockSpec((t, LANE), lambda h, i: (i, h)),
                pl.BlockSpec((None, t, 1), lambda h, i: (h, i, 0))]
    args = [qa, ka, va, o, do, lse]
    if decay:
        in_specs.append(pl.BlockSpec((None, nt, 1, t), lambda h, i: (h, 0, 0, 0)))
        args.append(c_row)
    col = pl.BlockSpec((None, t, 1), lambda h, i: (h, i, 0))
    return pl.pallas_call(
        kern, name=name, grid=(HEADS, nt), in_specs=in_specs,
        out_specs=[pl.BlockSpec((t, LANE), lambda h, i: (i, h)), col, col],
        out_shape=[jax.ShapeDtypeStruct((s_len, HEADS * LANE), f32), jax.ShapeDtypeStruct((HEADS, s_len, 1), f32),
                   jax.ShapeDtypeStruct((HEADS, s_len, 1), f32)],
        compiler_params=pltpu.CompilerParams(dimension_semantics=("parallel", "arbitrary")))(*args)


def _attn_dkv(name, q, k, v, do, lse, delta, scale, c_col=None):
    (qa, qo), (ka, ko), (va, vo) = q, k, v
    s_len = qa.shape[0]
    tq, tk = _att_tiles(s_len)
    nq, per = s_len // tq, tk // tq
    decay = c_col is not None
    q_mul, s_mul = _fold_scale(scale)

    def kern(*refs):
        q_ref, k_ref, v_ref, do_ref, lse_ref, delta_ref = refs[:6]
        j = pl.program_id(1)
        kb = k_ref[...].astype(bf16)
        vb = v_ref[...].astype(bf16)
        ck = refs[6][...] if decay else None

        def step(i, carry, diagonal):
            dk, dv, dsum = carry
            for d in range(per):
                tile = i * per + d
                rows = pl.ds(pl.multiple_of(tile * tq, tq), tq)
                qb = (q_ref[rows, :] * q_mul).astype(bf16)
                dob = do_ref[rows, :].astype(bf16)
                s = _scores_t(kb, qb.T, s_mul, ck, d * tq if diagonal else None, tq, tk)
                p = jnp.exp(s - lse_ref[tile])
                dv = dv + jnp.dot(p.astype(bf16), dob, preferred_element_type=f32)
                dp = jnp.dot(vb, dob.T, preferred_element_type=f32)
                ds = p * (dp - delta_ref[tile])
                dk = dk + jnp.dot(ds.astype(bf16), qb, preferred_element_type=f32)
                if decay:
                    dsum = dsum + ds
            return dk, dv, dsum

        init = (jnp.zeros((tk, LANE), f32), jnp.zeros((tk, LANE), f32), jnp.zeros((tk, tq), f32))
        dk, dv, dsum = lax.fori_loop(j + 1, s_len // tk, lambda i, c: step(i, c, False), step(j, init, True))
        if decay:
            dk_ref, dv_ref, dc_ref = refs[-3:]
            dc_ref[...] = -jnp.sum(dsum, axis=1, keepdims=True)
        else:
            dk_ref, dv_ref = refs[-2:]
        dk_ref[...] = dk * s_mul
        dv_ref[...] = dv

    stat = pl.BlockSpec((None, nq, 1, tq), lambda h, j: (h, 0, 0, 0))
    in_specs = [pl.BlockSpec((s_len, LANE), lambda h, j: (0, qo + h)),
                pl.BlockSpec((tk, LANE), lambda h, j: (j, ko + h)),
                pl.BlockSpec((tk, LANE), lambda h, j: (j, vo + h)),
                pl.BlockSpec((s_len, LANE), lambda h, j: (0, h)), stat, stat]
    args = [qa, ka, va, do, lse, delta]
    out_specs = [pl.BlockSpec((tk, LANE), lambda h, j: (j, h)), pl.BlockSpec((tk, LANE), lambda h, j: (j, h))]
    out_shape = [jax.ShapeDtypeStruct((s_len, HEADS * LANE), f32), jax.ShapeDtypeStruct((s_len, HEADS * LANE), f32)]
    if decay:
        in_specs.append(pl.BlockSpec((None, tk, 1), lambda h, j: (h, j, 0)))
        args.append(c_col)
        out_specs.append(pl.BlockSpec((None, tk, 1), lambda h, j: (h, j, 0)))
        out_shape.append(jax.ShapeDtypeStruct((HEADS, s_len, 1), f32))
    return pl.pallas_call(
        kern, name=name, grid=(HEADS, s_len // tk), in_specs=in_specs, out_specs=out_specs, out_shape=out_shape,
        compiler_params=pltpu.CompilerParams(dimension_semantics=("parallel", "arbitrary")))(*args)


CONV_TS, CONV_CB = 1024, 256
FFN_ROWS = 64


def _conv_fwd(name, x, w, b, taps):
    xa, width, xidx = _view(x)
    s_len = xa.shape[0]
    ts, cb = min(CONV_TS, s_len), CONV_CB
    xo = xidx * width // cb

    def kern(x_ref, halo_ref, w_ref, b_ref, o_ref):
        i = pl.program_id(1)
        xb = x_ref[...]
        halo = jnp.where(i == 0, 0.0, halo_ref[...])
        xx = jnp.concatenate([halo, xb], axis=0)
        out = b_ref[...] + w_ref[taps - 1:taps, :] * xb
        for k in range(taps - 1):
            out = out + w_ref[k:k + 1, :] * pltpu.roll(xx, taps - 1 - k, 0)[SUBLANE:]
        o_ref[...] = out

    return pl.pallas_call(
        kern, name=name, grid=(width // cb, s_len // ts),
        in_specs=[pl.BlockSpec((ts, cb), lambda j, i: (i, xo + j)),
                  pl.BlockSpec((SUBLANE, cb), lambda j, i: (jnp.maximum(i * (ts // SUBLANE) - 1, 0), xo + j)),
                  pl.BlockSpec((taps, cb), lambda j, i: (0, j)),
                  pl.BlockSpec((1, cb), lambda j, i: (0, j))],
        out_specs=pl.BlockSpec((ts, cb), lambda j, i: (i, j)),
        out_shape=jax.ShapeDtypeStruct((s_len, width), f32),
        compiler_params=pltpu.CompilerParams(dimension_semantics=("parallel", "parallel")))(xa, xa, w, b)


def _conv_bwd(name, x, dout, w, taps, dout2=None, dx_dtype=f32):
    xa, width, xidx = _view(x)
    s_len = xa.shape[0]
    ts, cb = min(CONV_TS, s_len), CONV_CB
    xo = xidx * width // cb
    n_i = s_len // ts
    two = dout2 is not None

    def kern(*refs):
        x_ref, halo_ref, w_ref = refs[:3]
        dx_ref, dw_ref, db_ref = refs[-3:]
        i = pl.program_id(1)
        if two:
            d = refs[3][...] + refs[5][...]
            dn = refs[4][...] + refs[6][...]
        else:
            d, dn = refs[3][...], refs[4][...]
        dn = jnp.where(i == n_i - 1, 0.0, dn)
        xb = x_ref[...]
        halo = jnp.where(i == 0, 0.0, halo_ref[...])
        xx = jnp.concatenate([halo, xb], axis=0)
        dd = jnp.concatenate([d, dn], axis=0)

        @pl.when(i == 0)
        def _():
            dw_ref[...] = jnp.zeros_like(dw_ref)
            db_ref[...] = jnp.zeros_like(db_ref)

        dx = w_ref[taps - 1:taps, :] * d
        dw_ref[taps - 1:taps, :] += jnp.sum(d * xb, axis=0, keepdims=True)
        for k in range(taps - 1):
            sh = taps - 1 - k
            dx = dx + w_ref[k:k + 1, :] * pltpu.roll(dd, ts + SUBLANE - sh, 0)[:ts]
            dw_ref[k:k + 1, :] += jnp.sum(d * pltpu.roll(xx, sh, 0)[SUBLANE:], axis=0, keepdims=True)
        dx_ref[...] = dx.astype(dx_ref.dtype)
        db_ref[...] += jnp.sum(d, axis=0, keepdims=True)

    d_spec = pl.BlockSpec((ts, cb), lambda j, i: (i, j))
    dn_spec = pl.BlockSpec((SUBLANE, cb), lambda j, i: (jnp.minimum((i + 1) * (ts // SUBLANE), s_len // SUBLANE - 1), j))
    in_specs = [pl.BlockSpec((ts, cb), lambda j, i: (i, xo + j)),
                pl.BlockSpec((SUBLANE, cb), lambda j, i: (jnp.maximum(i * (ts // SUBLANE) - 1, 0), xo + j)),
                pl.BlockSpec((taps, cb), lambda j, i: (0, j)), d_spec, dn_spec]
    args = [xa, xa, w, dout, dout]
    if two:
        in_specs += [d_spec, dn_spec]
        args += [dout2, dout2]
    return pl.pallas_call(
        kern, name=name, grid=(width // cb, n_i), in_specs=in_specs,
        out_specs=[pl.BlockSpec((ts, cb), lambda j, i: (i, j)), pl.BlockSpec((taps, cb), lambda j, i: (0, j)),
                   pl.BlockSpec((1, cb), lambda j, i: (0, j))],
        out_shape=[jax.ShapeDtypeStruct((s_len, width), dx_dtype), jax.ShapeDtypeStruct((taps, width), f32),
                   jax.ShapeDtypeStruct((1, width), f32)],
        compiler_params=pltpu.CompilerParams(dimension_semantics=("parallel", "arbitrary")))(*args)


def _conv_rows(xx, w_ref, b_ref, taps):
    out = b_ref[...] + w_ref[taps - 1:taps, :] * xx[SUBLANE:]
    for k in range(taps - 1):
        out = out + w_ref[k:k + 1, :] * pltpu.roll(xx, taps - 1 - k, 0)[SUBLANE:]
    return out


def _ffn_act_fwd(name, up, w, b):
    s_len = up.shape[0]
    ts, cb = min(CONV_TS, s_len), CONV_CB
    nf = D_FF // cb

    def kern(g_ref, gp_ref, v_ref, vp_ref, wg_ref, wv_ref, bg_ref, bv_ref, o_ref):
        first = pl.program_id(1) == 0
        ug = _conv_rows(jnp.concatenate([jnp.where(first, 0.0, gp_ref[...]), g_ref[...]], axis=0), wg_ref, bg_ref, FFN_CONV)
        uv = _conv_rows(jnp.concatenate([jnp.where(first, 0.0, vp_ref[...]), v_ref[...]], axis=0), wv_ref, bv_ref, FFN_CONV)
        o_ref[...] = (jax.nn.silu(ug) * uv).astype(o_ref.dtype)

    def half(off):
        return [pl.BlockSpec((ts, cb), lambda j, i: (i, off + j)),
                pl.BlockSpec((SUBLANE, cb), lambda j, i: (jnp.maximum(i * (ts // SUBLANE) - 1, 0), off + j))]

    def par(rows, off):
        return pl.BlockSpec((rows, cb), lambda j, i: (0, off + j))

    return pl.pallas_call(
        kern, name=name, grid=(nf, s_len // ts),
        in_specs=half(0) + half(nf) + [par(FFN_CONV, 0), par(FFN_CONV, nf), par(1, 0), par(1, nf)],
        out_specs=pl.BlockSpec((ts, cb), lambda j, i: (i, j)),
        out_shape=jax.ShapeDtypeStruct((s_len, D_FF), bf16),
        compiler_params=pltpu.CompilerParams(dimension_semantics=("parallel", "parallel")))(up, up, up, up, w, w, b, b)


def _ffn_act_bwd(name, up, dact, w, b):
    s_len = up.shape[0]
    ts, cb = min(CONV_TS, s_len), CONV_CB
    nf = D_FF // cb
    n_i = s_len // ts
    taps = FFN_CONV

    ch = min(FFN_ROWS, ts)

    def kern(g_ref, gp_ref, gn_ref, v_ref, vp_ref, vn_ref, d_ref, dn_ref, wg_ref, wv_ref, bg_ref, bv_ref,
             dg_ref, dv_ref, dwg_ref, dwv_ref, dbg_ref, dbv_ref, gx_ref, vx_ref, dd_ref):
        i = pl.program_id(1)
        first, last = i == 0, i == n_i - 1
        for x_ref, p_ref, n_ref, ext in ((g_ref, gp_ref, gn_ref, gx_ref), (v_ref, vp_ref, vn_ref, vx_ref)):
            ext[:SUBLANE, :] = jnp.where(first, 0.0, p_ref[...])
            ext[SUBLANE:SUBLANE + ts, :] = x_ref[...]
            ext[SUBLANE + ts:, :] = jnp.where(last, 0.0, n_ref[...])
        dd_ref[:ts, :] = d_ref[...]
        dd_ref[ts:, :] = jnp.where(last, 0.0, dn_ref[...])

        @pl.when(first)
        def _():
            for ref in (dwg_ref, dwv_ref, dbg_ref, dbv_ref):
                ref[...] = jnp.zeros_like(ref)

        def rows_of(c, carry):
            r0 = pl.multiple_of(c * ch, ch)
            gx, vx = gx_ref[pl.ds(r0, ch + 2 * SUBLANE), :], vx_ref[pl.ds(r0, ch + 2 * SUBLANE), :]
            ug, uv = _conv_rows(gx, wg_ref, bg_ref, taps), _conv_rows(vx, wv_ref, bv_ref, taps)
            dd = dd_ref[pl.ds(r0, ch + SUBLANE), :]
            sg = jax.nn.sigmoid(ug)
            out = []
            for du, xx, w_ref, dx_ref, sums in ((dd * uv * (sg * (1.0 + ug * (1.0 - sg))), gx, wg_ref, dg_ref, carry[0]),
                                                (dd * (ug * sg), vx, wv_ref, dv_ref, carry[1])):
                d = du[:ch]
                dx = w_ref[taps - 1:taps, :] * d
                new = [None] * (taps + 1)
                new[taps - 1] = sums[taps - 1] + jnp.sum(d * xx[SUBLANE:SUBLANE + ch], axis=0, keepdims=True)
                for k in range(taps - 1):
                    sh = taps - 1 - k
                    dx = dx + w_ref[k:k + 1, :] * pltpu.roll(du, ch + SUBLANE - sh, 0)[:ch]
                    new[k] = sums[k] + jnp.sum(d * pltpu.roll(xx, sh, 0)[SUBLANE:SUBLANE + ch], axis=0, keepdims=True)
                new[taps] = sums[taps] + jnp.sum(d, axis=0, keepdims=True)
                dx_ref[pl.ds(r0, ch), :] = dx.astype(dx_ref.dtype)
                out.append(tuple(new))
            return tuple(out)

        zero = tuple(jnp.zeros((1, cb), f32) for _ in range(taps + 1))
        sums_g, sums_v = lax.fori_loop(0, ts // ch, rows_of, (zero, zero))
        for sums, dw_ref, db_ref in ((sums_g, dwg_ref, dbg_ref), (sums_v, dwv_ref, dbv_ref)):
            for k in range(taps):
                dw_ref[k:k + 1, :] += sums[k]
            db_ref[...] += sums[taps]

    blocks = s_len // SUBLANE

    def half(off):
        return [pl.BlockSpec((ts, cb), lambda j, i: (i, off + j)),
                pl.BlockSpec((SUBLANE, cb), lambda j, i: (jnp.maximum(i * (ts // SUBLANE) - 1, 0), off + j)),
                pl.BlockSpec((SUBLANE, cb), lambda j, i: (jnp.minimum((i + 1) * (ts // SUBLANE), blocks - 1), off + j))]

    def par(rows, off):
        return pl.BlockSpec((rows, cb), lambda j, i: (0, off + j))

    d_specs = [pl.BlockSpec((ts, cb), lambda j, i: (i, j)),
               pl.BlockSpec((SUBLANE, cb), lambda j, i: (jnp.minimum((i + 1) * (ts // SUBLANE), blocks - 1), j))]
    out_par = [pl.BlockSpec((r, cb), lambda j, i: (0, j)) for r in (taps, taps, 1, 1)]
    return pl.pallas_call(
        kern, name=name, grid=(nf, n_i),
        in_specs=half(0) + half(nf) + d_specs + [par(taps, 0), par(taps, nf), par(1, 0), par(1, nf)],
        out_specs=[pl.BlockSpec((ts, cb), lambda j, i: (i, j))] * 2 + out_par,
        out_shape=[jax.ShapeDtypeStruct((s_len, D_FF), bf16)] * 2 + [jax.ShapeDtypeStruct((taps, D_FF), f32)] * 2
        + [jax.ShapeDtypeStruct((1, D_FF), f32)] * 2,
        scratch_shapes=[pltpu.VMEM((ts + 2 * SUBLANE, cb), f32)] * 2 + [pltpu.VMEM((ts + SUBLANE, cb), f32)],
        compiler_params=pltpu.CompilerParams(dimension_semantics=("parallel", "arbitrary")))(
        up, up, up, up, up, up, dact, dact, w, w, b, b)


SCAN_ROWS = 128


def _block_scan(a, b, reverse):
    t = a.shape[0]
    row = lax.broadcasted_iota(jnp.int32, a.shape, 0)
    d = 1
    while d < t:
        keep = row < t - d if reverse else row >= d
        shift = t - d if reverse else d
        a_far = jnp.where(keep, pltpu.roll(a, shift, 0), 1.0)
        b_far = jnp.where(keep, pltpu.roll(b, shift, 0), 0.0)
        b = a * b_far + b
        a = a * a_far
        d *= 2
    return a, b


def _scan_fwd(name, a, b):
    s_len, width = a.shape
    t = min(SCAN_ROWS, s_len)

    def kern(a_ref, b_ref, h_ref):
        def block(k, carry):
            rows = pl.ds(pl.multiple_of(k * t, t), t)
            acc, h = _block_scan(a_ref[rows, :], b_ref[rows, :], False)
            h_ref[rows, :] = h + acc * carry
            return h_ref[pl.ds(k * t + t - 1, 1), :]

        lax.fori_loop(0, s_len // t, block, jnp.zeros((1, LANE), f32))

    spec = pl.BlockSpec((s_len, LANE), lambda j: (0, j))
    return pl.pallas_call(
        kern, name=name, grid=(width // LANE,), in_specs=[spec, spec], out_specs=spec,
        out_shape=jax.ShapeDtypeStruct((s_len, width), f32),
        compiler_params=pltpu.CompilerParams(dimension_semantics=("parallel",)))(a, b)


def _scan_bwd(name, a_next, h_prev, dh):
    s_len, width = dh.shape
    t = min(SCAN_ROWS, s_len)
    n_blocks = s_len // t

    def kern(an_ref, hp_ref, dh_ref, da_ref, db_ref):
        def block(kk, carry):
            k = n_blocks - 1 - kk
            rows = pl.ds(pl.multiple_of(k * t, t), t)
            acc, g = _block_scan(an_ref[rows, :], dh_ref[rows, :], True)
            g = g + acc * carry
            db_ref[rows, :] = g
            da_ref[rows, :] = g * hp_ref[rows, :]
            return db_ref[pl.ds(k * t, 1), :]

        lax.fori_loop(0, n_blocks, block, jnp.zeros((1, LANE), f32))

    spec = pl.BlockSpec((s_len, LANE), lambda j: (0, j))
    return pl.pallas_call(
        kern, name=name, grid=(width // LANE,), in_specs=[spec, spec, spec], out_specs=[spec, spec],
        out_shape=[jax.ShapeDtypeStruct((s_len, width), f32)] * 2,
        compiler_params=pltpu.CompilerParams(dimension_semantics=("parallel",)))(a_next, h_prev, dh)


def _lane_cumsum(x, reverse):
    n = x.shape[1]
    lane = lax.broadcasted_iota(jnp.int32, x.shape, 1)
    sh = 1
    while sh < n:
        if reverse:
            x = x + jnp.where(lane < n - sh, pltpu.roll(x, n - sh, 1), 0.0)
        else:
            x = x + jnp.where(lane >= sh, pltpu.roll(x, sh, 1), 0.0)
        sh *= 2
    return x


def _decay_fwd(name, fl_t, b8):
    def kern(f_ref, b_ref, c_ref):
        c_ref[...] = _lane_cumsum(jax.nn.log_sigmoid(f_ref[...] + b_ref[...]), False)

    return pl.pallas_call(kern, name=name, out_shape=jax.ShapeDtypeStruct(fl_t.shape, f32))(fl_t, b8)


def _decay_bwd(name, fl_t, b8, dc_key, dc_query):
    def kern(f_ref, b_ref, dck_ref, dcq_ref, df_ref, db_ref):
        dlogf = _lane_cumsum(dck_ref[...] + dcq_ref[...], True)
        df = dlogf * jax.nn.sigmoid(-(f_ref[...] + b_ref[...]))
        df_ref[...] = df
        db_ref[...] = jnp.sum(df, axis=1, keepdims=True)

    return pl.pallas_call(kern, name=name, out_shape=[jax.ShapeDtypeStruct(fl_t.shape, f32),
                                                      jax.ShapeDtypeStruct((SUBLANE, 1), f32)])(fl_t, b8, dc_key, dc_query)


def _rms(x, g, n):
    return x * lax.rsqrt(jnp.sum(x * x, axis=-1, keepdims=True) * (1.0 / n) + EPS) * g


def _loss_head(name, h, target, g, tb=256):
    n, d = h.shape
    tb = min(tb, n)

    def kern(h_ref, t_ref, g_ref, loss_ref, dh_ref, dg_ref):
        i = pl.program_id(0)
        tgt = t_ref[...]

        def f(hv, gv):
            err = _rms(hv, gv, d) - tgt
            return 0.5 * jnp.sum(jnp.sum(err * err, axis=-1, keepdims=True) * (1.0 / d), axis=0, keepdims=True)

        val, vjp = jax.vjp(f, h_ref[...], g_ref[...])
        dh, dg = vjp(jnp.ones((1, 1), f32))
        dh_ref[...] = dh

        @pl.when(i == 0)
        def _():
            loss_ref[...] = jnp.zeros_like(loss_ref)
            dg_ref[...] = jnp.zeros_like(dg_ref)

        loss_ref[...] += val
        dg_ref[...] += dg

    return pl.pallas_call(
        kern, name=name, grid=(n // tb,),
        in_specs=[pl.BlockSpec((tb, d), lambda i: (i, 0)), pl.BlockSpec((tb, d), lambda i: (i, 0)),
                  pl.BlockSpec((1, d), lambda i: (0, 0))],
        out_specs=[pl.BlockSpec((1, 1), lambda i: (0, 0)), pl.BlockSpec((tb, d), lambda i: (i, 0)),
                   pl.BlockSpec((1, d), lambda i: (0, 0))],
        out_shape=[jax.ShapeDtypeStruct((1, 1), f32), jax.ShapeDtypeStruct((n, d), f32), jax.ShapeDtypeStruct((1, d), f32)],
        compiler_params=pltpu.CompilerParams(dimension_semantics=("arbitrary",)))(h, target, g)


def _f_norm(x, g):
    return (_rms(x, g, D_MODEL),)


def _f_latent(qc, kvc, gq, gkv):
    return _rms(qc, gq, MLA_Q_RANK), _rms(kvc, gkv, MLA_KV_RANK)


def _f_rope_table(pos, freq, m1, m2):
    ang = pos * freq
    sin = jnp.sin(ang)
    return jnp.cos(ang), -sin * m1, sin * m2


def _rope(x, cos, s_up, s_down):
    w = x.shape[1]
    return x * cos + _roll(x, w - MLA_ROPE // 2, 1) * s_up + _roll(x, MLA_ROPE // 2, 1) * s_down


def _f_mla_prep(q, kpart, kr, cos, s_up, s_down):
    def heads(t):
        return jnp.concatenate([t] * HEADS, axis=1)

    kr = _rope(kr, cos, s_up, s_down)
    return _rope(q, heads(cos), heads(s_up), heads(s_down)), kpart + heads(kr)


def _f_lru_gate(gates, xc, b_r, b_i, lam):
    r = jax.nn.sigmoid(gates[:, :LRU_WIDTH] + b_r)
    i = jax.nn.sigmoid(gates[:, LRU_WIDTH:] + b_i)
    log_a = -LRU_C * r * jax.nn.softplus(-lam)
    mult = jnp.sqrt(-jnp.tanh(log_a) * (1.0 + jnp.exp(2.0 * log_a)))
    return jnp.exp(log_a), mult * (i * xc)


def _f_merge(o_mla, o_fox, hs, lg, g):
    o_lru = hs * jax.nn.gelu(lg)
    return (jnp.concatenate([_rms(o_mla, g[:, :512], HEADS * MLA_V), _rms(o_fox, g[:, 512:1024], HEADS * FOX_HEAD_DIM),
                             _rms(o_lru, g[:, 1024:], LRU_WIDTH)], axis=1),)


def _f_ffn_gate(u):
    return (jax.nn.silu(u[:, :D_FF]) * u[:, D_FF:],)


def _f_ple(h, gpre, pp):
    return (h + jax.nn.sigmoid(gpre) * pp,)


MIX_PART = ["w_in", "w_uq", "w_ukv", "lru_conv_w"]
FFN_PART = ["w_o", "w_up", "ffn_conv_w", "w_down", "w_ple_gate", "w_ple_proj"]


def _prep_mix_weights(w):
    eye = jnp.eye(LRU_BLOCKS, dtype=f32)

    def block_diag(m):
        return (eye[:, None, :, None] * m[:, :, None, :]).reshape(LRU_WIDTH, LRU_WIDTH)

    return dict(
        w_in=_take_pad(w["w_in"], Z_MAP, 1),
        w_uq=_take_pad(_take_pad(w["w_uq"], UQ_COL_MAP, 1), UQ_ROW_MAP, 0),
        w_ukv=_take_pad(w["w_ukv"], UKV_MAP, 1),
        w_ri=jnp.concatenate([block_diag(w["w_r"]), block_diag(w["w_i"])], axis=1).astype(bf16),
        g_mix=w["g_mix"].reshape(1, -1), g_ffn=w["g_ffn"].reshape(1, -1), g_ple=w["g_ple"].reshape(1, -1),
        g_qc=_take_pad(w["g_qc"], UQ_ROW_MAP, 0).reshape(1, -1), g_kvc=w["g_kvc"].reshape(1, -1),
        g_out=_take_pad(w["g_out"], OMIX_MAP, 0).reshape(1, -1),
        b_f8=_take_pad(w["b_f"], _pad_to(np.arange(FOX_HEADS), SUBLANE), 0).reshape(SUBLANE, 1),
        lru_conv_w=w["lru_conv_w"], lru_conv_b=w["lru_conv_b"].reshape(1, -1),
        b_r=w["b_r"].reshape(1, -1), b_i=w["b_i"].reshape(1, -1), lam=w["lru_lambda"].reshape(1, -1),
        ffn_conv_b=w["ffn_conv_b"].reshape(1, -1),
    )


def _prep_ffn_weights(w):
    return dict(w_o=_take_pad(w["w_o"], OMIX_MAP, 0),
                w_up=w["w_up"], w_up_g=w["w_up"][:, :D_FF], w_up_v=w["w_up"][:, D_FF:], ffn_conv_w=w["ffn_conv_w"],
                w_down=w["w_down"], w_ple_gate=w["w_ple_gate"], w_ple_proj=w["w_ple_proj"])


def _rope_rows(pos):
    consts = [jnp.asarray(t) for t in _rope_tables(LANE, ROPE_AT)]
    return _rowwise("rope_table", _f_rope_table, [pos], consts, [(LANE, f32)] * 3)


def _key_decay(c_t, s_len):
    t = _att_tiles(s_len)[1]
    return c_t[:HEADS].reshape(HEADS, s_len // t, 1, t), c_t[:HEADS].reshape(HEADS, s_len, 1)


def _layer_fwd(l, h0, p_l, rope, weights_of):
    s_len = h0.shape[0]
    n = f"l{l}_"
    w = _prep_mix_weights(weights_of("mix", h0))
    xn, = _rowwise(n + "norm_mix", _f_norm, [h0], [w["g_mix"]], [(D_MODEL, bf16)])
    z = _mm(n + "in_proj", xn, w["w_in"])
    zq = (z, QC_W, Z_QC // QC_W)
    zkv = (z, LANE, Z_KVC // LANE)
    zkr = (z, LANE, Z_KR // LANE)
    zlx = (z, LRU_WIDTH, Z_LX // LRU_WIDTH)
    zlg = (z, LRU_WIDTH, Z_LG // LRU_WIDTH)
    qcn, kvn = _rowwise(n + "latent_norm", _f_latent, [zq, zkv], [w["g_qc"], w["g_kvc"]], [(QC_W, bf16), (LANE, bf16)])
    q = _mm(n + "uq", qcn, w["w_uq"])
    kv = _mm(n + "ukv", kvn, w["w_ukv"])
    kpart = (kv, HEADS * LANE, 0)
    qr, kk = _rowwise(n + "mla_prep", _f_mla_prep, [q, kpart, zkr, *rope], [],
                      [(HEADS * LANE, bf16), (HEADS * LANE, bf16)])
    mla_scale = (MLA_NOPE + MLA_ROPE) ** -0.5
    o_mla, lse_m = _attn_fwd(n + "mla_fwd", (qr, 0), (kk, 0), (kv, HEADS), mla_scale)
    fl_t = z[:, Z_FL:Z_FL + SUBLANE].T
    c_t = _decay_fwd(n + "decay", fl_t, w["b_f8"])
    c_row, c_col = _key_decay(c_t, s_len)
    fox_scale = FOX_HEAD_DIM ** -0.5
    o_fox, lse_f = _attn_fwd(n + "fox_fwd", (z, Z_FQ // LANE), (z, Z_FK // LANE), (z, Z_FV // LANE), fox_scale, c_row)
    xc = _conv_fwd(n + "lru_conv", zlx, w["lru_conv_w"], w["lru_conv_b"], LRU_CONV)
    gates = _mm(n + "lru_gates", xc, w["w_ri"])
    a, bx = _rowwise(n + "lru_gate", _f_lru_gate, [gates, xc], [w["b_r"], w["b_i"], w["lam"]],
                     [(LRU_WIDTH, f32), (LRU_WIDTH, f32)])
    hs = _scan_fwd(n + "lru_scan", a, bx)
    ocat, = _rowwise(n + "merge", _f_merge, [o_mla, o_fox, hs, zlg], [w["g_out"]], [(OMIX_W, bf16)])
    w.update(_prep_ffn_weights(weights_of("ffn", ocat)))
    h1 = _mm(n + "out_proj", ocat, w["w_o"], res=h0)
    xn2, = _rowwise(n + "norm_ffn", _f_norm, [h1], [w["g_ffn"]], [(D_MODEL, bf16)])
    up = _mm(n + "up_proj", xn2, w["w_up"])
    act = _ffn_act_fwd(n + "ffn_act", up, w["ffn_conv_w"], w["ffn_conv_b"])
    h2 = _mm(n + "down_proj", act, w["w_down"], res=h1)
    hn, = _rowwise(n + "norm_ple", _f_norm, [h2], [w["g_ple"]], [(D_MODEL, bf16)])
    gpre = _mm(n + "ple_gate", hn, w["w_ple_gate"])
    pp = _mm(n + "ple_proj", p_l, w["w_ple_proj"])
    h3, = _rowwise(n + "ple_mix", _f_ple, [h2, gpre, pp], [], [(D_MODEL, f32)])
    res = dict(h0=h0, xn=xn, z=z, qcn=qcn, kvn=kvn, q=q, kv=kv, qr=qr, kk=kk, o_mla=o_mla, lse_m=lse_m, fl_t=fl_t,
               c_row=c_row, c_col=c_col, o_fox=o_fox, lse_f=lse_f, xc=xc, gates=gates, a=a, hs=hs, ocat=ocat, h1=h1,
               xn2=xn2, up=up, act=act, h2=h2, hn=hn, gpre=gpre, pp=pp, p_l=p_l)
    return h3, res, w


def _layer_bwd(l, dh3, r, rope, w, token, grads_to):
    s_len = dh3.shape[0]
    n = f"l{l}_"
    g = {}
    w = dict(w, g_ple=w["g_ple"] + token)
    z = r["z"]
    zq = (z, QC_W, Z_QC // QC_W)
    zkv = (z, LANE, Z_KVC // LANE)
    zkr = (z, LANE, Z_KR // LANE)
    zlx = (z, LRU_WIDTH, Z_LX // LRU_WIDTH)
    zlg = (z, LRU_WIDTH, Z_LG // LRU_WIDTH)
    (dh2a, dgpre, dpp), _ = _rowwise_bwd(n + "ple_mix_b", _f_ple, [r["h2"], r["gpre"], r["pp"]], [], [dh3], 3,
                                         dts=[f32, bf16, bf16])
    g["w_ple_proj"] = _mm(n + "ple_proj_dw", r["p_l"], dpp, "tn", bf16)
    dhn = _mm(n + "ple_gate_dx", dgpre, w["w_ple_gate"], "nt")
    g["w_ple_gate"] = _mm(n + "ple_gate_dw", r["hn"], dgpre, "tn", bf16)
    (dh2,), (g["g_ple"],) = _rowwise_bwd(n + "norm_ple_b", _f_norm, [r["h2"]], [w["g_ple"]], [dhn], 1, adds={0: dh2a})
    dact = _mm(n + "down_dx", dh2, w["w_down"], "nt")
    g["w_down"] = _mm(n + "down_dw", r["act"], dh2, "tn", bf16)
    dup_g, dup_v, dcw_g, dcw_v, dcb_g, dcb_v = _ffn_act_bwd(n + "ffn_act_b", r["up"], dact, w["ffn_conv_w"], w["ffn_conv_b"])
    g["ffn_conv_w"] = jnp.concatenate([dcw_g, dcw_v], axis=1)
    g["ffn_conv_b"] = jnp.concatenate([dcb_g, dcb_v], axis=1)
    dxn2 = _mm(n + "up_dx_v", dup_v, w["w_up_v"], "nt", res=_mm(n + "up_dx_g", dup_g, w["w_up_g"], "nt"))
    g["w_up"] = jnp.concatenate([_mm(n + "up_dw_g", r["xn2"], dup_g, "tn", bf16),
                                 _mm(n + "up_dw_v", r["xn2"], dup_v, "tn", bf16)], axis=1)
    (dh1,), (g["g_ffn"],) = _rowwise_bwd(n + "norm_ffn_b", _f_norm, [r["h1"]], [w["g_ffn"]], [dxn2], 1, adds={0: dh2})
    docat = _mm(n + "out_dx", dh1, w["w_o"], "nt")
    g["w_o"] = _mm(n + "out_dw", r["ocat"], dh1, "tn", bf16)
    token = grads_to("ffn", dict(w_o=_take_inv(g["w_o"], OMIX_MAP, 0), w_up=g["w_up"], ffn_conv_w=g["ffn_conv_w"],
                                 w_down=g["w_down"], w_ple_gate=g["w_ple_gate"], w_ple_proj=g["w_ple_proj"]))
    w = dict(w, g_out=w["g_out"] + token)
    (do_mla, do_fox, dhs, dlg), (g["g_out"],) = _rowwise_bwd(
        n + "merge_b", _f_merge, [r["o_mla"], r["o_fox"], r["hs"], zlg], [w["g_out"]], [docat], 4)
    a, hs = r["a"], r["hs"]
    a_next = jnp.concatenate([a[1:], jnp.zeros((1, LRU_WIDTH), f32)], axis=0)
    h_prev = jnp.concatenate([jnp.zeros((1, LRU_WIDTH), f32), hs[:-1]], axis=0)
    da, dbx = _scan_bwd(n + "lru_scan_b", a_next, h_prev, dhs)
    (dgates, dxc_a), (g["b_r"], g["b_i"], g["lam"]) = _rowwise_bwd(
        n + "lru_gate_b", _f_lru_gate, [r["gates"], r["xc"]], [w["b_r"], w["b_i"], w["lam"]], [da, dbx], 2,
        dts=[bf16, f32])
    dxc_b = _mm(n + "lru_gates_dx", dgates, w["w_ri"], "nt")
    g["w_ri"] = _mm(n + "lru_gates_dw", r["xc"], dgates, "tn")
    dlx, g["lru_conv_w"], g["lru_conv_b"] = _conv_bwd(n + "lru_conv_b", zlx, dxc_a, w["lru_conv_w"], LRU_CONV, dout2=dxc_b)
    fox_scale = FOX_HEAD_DIM ** -0.5
    fq, fk, fv = (z, Z_FQ // LANE), (z, Z_FK // LANE), (z, Z_FV // LANE)
    dfq, delta_f, dc_q = _attn_dq(n + "fox_dq", fq, fk, fv, r["o_fox"], do_fox, r["lse_f"], fox_scale, r["c_row"])
    dfk, dfv, dc_k = _attn_dkv(n + "fox_dkv", fq, fk, fv, do_fox, _query_rows(r["lse_f"]), _query_rows(delta_f), fox_scale,
                               r["c_col"])
    pad_rows = jnp.zeros((SUBLANE - HEADS, s_len), f32)
    dfl_t, g["b_f8"] = _decay_bwd(n + "decay_b", r["fl_t"], w["b_f8"],
                                  jnp.concatenate([dc_k.reshape(HEADS, s_len), pad_rows], axis=0),
                                  jnp.concatenate([dc_q.reshape(HEADS, s_len), pad_rows], axis=0))
    dfl = jnp.pad(dfl_t.T, ((0, 0), (0, LANE - SUBLANE)))
    mla_scale = (MLA_NOPE + MLA_ROPE) ** -0.5
    qr, kk, kv = (r["qr"], 0), (r["kk"], 0), (r["kv"], HEADS)
    dqr, delta_m, _ = _attn_dq(n + "mla_dq", qr, kk, kv, r["o_mla"], do_mla, r["lse_m"], mla_scale)
    dkk, dv_m = _attn_dkv(n + "mla_dkv", qr, kk, kv, do_mla, _query_rows(r["lse_m"]), _query_rows(delta_m), mla_scale)
    (dq, dkpart, dkr), _ = _rowwise_bwd(n + "mla_prep_b", _f_mla_prep, [r["q"], (r["kv"], HEADS * LANE, 0), zkr, *rope],
                                        [], [dqr, dkk], 3, dts=[bf16, bf16, f32])
    dkv = jnp.concatenate([dkpart, dv_m.astype(bf16)], axis=1)
    dkvn = _mm(n + "ukv_dx", dkv, w["w_ukv"], "nt")
    g["w_ukv"] = _mm(n + "ukv_dw", r["kvn"], dkv, "tn", bf16)
    dqcn = _mm(n + "uq_dx", dq, w["w_uq"], "nt")
    g["w_uq"] = _mm(n + "uq_dw", r["qcn"], dq, "tn", bf16)
    (dqc, dkvc), (g["g_qc"], g["g_kvc"]) = _rowwise_bwd(n + "latent_norm_b", _f_latent, [zq, zkv],
                                                        [w["g_qc"], w["g_kvc"]], [dqcn, dkvn], 2)
    dz = jnp.concatenate([t.astype(bf16) for t in (dfq, dfk, dfv, dlx, dlg, dqc, dkvc, dkr, dfl)], axis=1)
    dxn = _mm(n + "in_dx", dz, w["w_in"], "nt")
    g["w_in"] = _mm(n + "in_dw", r["xn"], dz, "tn", bf16)
    (dh0,), (g["g_mix"],) = _rowwise_bwd(n + "norm_mix_b", _f_norm, [r["h0"]], [w["g_mix"]], [dxn], 1, adds={0: dh1})
    return dh0, grads_to("mix", _unpad_mix_grads(g))


def _unpad_mix_grads(g):
    d_ri = g["w_ri"]
    idx = jnp.arange(LRU_BLOCKS)

    def diag_blocks(m):
        return m.reshape(LRU_BLOCKS, LRU_BLOCK, LRU_BLOCKS, LRU_BLOCK)[idx, :, idx, :]

    return dict(
        g_mix=g["g_mix"][0], w_in=_take_inv(g["w_in"], Z_MAP, 1), g_qc=g["g_qc"][0, :MLA_Q_RANK],
        w_uq=_take_inv(g["w_uq"][:MLA_Q_RANK], UQ_COL_MAP, 1), g_kvc=g["g_kvc"][0],
        w_ukv=_take_inv(g["w_ukv"], UKV_MAP, 1), b_f=g["b_f8"][:FOX_HEADS, 0],
        lru_conv_w=g["lru_conv_w"], lru_conv_b=g["lru_conv_b"][0],
        w_r=diag_blocks(d_ri[:, :LRU_WIDTH]), b_r=g["b_r"][0], w_i=diag_blocks(d_ri[:, LRU_WIDTH:]), b_i=g["b_i"][0],
        lru_lambda=g["lam"][0], g_out=_take_inv(g["g_out"][0], OMIX_MAP, 0),
        g_ffn=g["g_ffn"][0], ffn_conv_b=g["ffn_conv_b"][0], g_ple=g["g_ple"][0],
    )


LAYER_WEIGHTS = ["g_mix", "w_in", "g_qc", "w_uq", "g_kvc", "w_ukv", "b_f", "lru_conv_w", "lru_conv_b", "w_r", "b_r", "w_i",
                 "b_i", "lru_lambda", "g_out", "w_o", "g_ffn", "w_up", "ffn_conv_w", "ffn_conv_b", "w_down", "g_ple",
                 "w_ple_gate", "w_ple_proj"]
WEIGHTS = LAYER_WEIGHTS + ["g_final"]


def _local_step(x, p, pos, target, g_final, weights_of, grads_to):
    h = x
    rope = _rope_rows(pos)
    ws, saved = [], []
    for l in range(DEPTH):
        h, r, w = _layer_fwd(l, h, p[l], rope, functools.partial(weights_of, l))
        ws.append(w)
        saved.append(r)
    loss, dh, dg_final = _loss_head("loss_head", h, target, g_final.reshape(1, -1))
    token = jnp.zeros((), f32)
    for l in reversed(range(DEPTH)):
        dh, token = _layer_bwd(l, dh, saved[l], rope, ws[l], token, functools.partial(grads_to, l))
    return loss[0, 0], dh, dg_final[0]


MESH_AXES = ("x", "y", "c")


def _row_tile(rows, cap):
    if rows <= cap:
        return rows
    for t in range(cap, SUBLANE - 1, -SUBLANE):
        if rows % t == 0:
            return t
    return rows


ADAM_BLOCK_BYTES = 2 ** 20


def _adamw(name, w, g, m, v):
    rows, cols = w.shape
    tr = _row_tile(rows, max(SUBLANE, ADAM_BLOCK_BYTES // (4 * cols) // SUBLANE * SUBLANE))

    def kern(w_ref, g_ref, m_ref, v_ref, d_ref, nm_ref, nv_ref):
        gv = g_ref[...]
        nm = ADAM_B1 * m_ref[...] + (1.0 - ADAM_B1) * gv
        nv = ADAM_B2 * v_ref[...] + (1.0 - ADAM_B2) * (gv * gv)
        m_hat = nm / (1.0 - ADAM_B1 ** ADAM_STEP)
        v_hat = nv / (1.0 - ADAM_B2 ** ADAM_STEP)
        d_ref[...] = -ADAM_LR * (m_hat / (jnp.sqrt(v_hat) + ADAM_EPS) + ADAM_WD * w_ref[...])
        nm_ref[...] = nm
        nv_ref[...] = nv

    spec = pl.BlockSpec((tr, cols), lambda i: (i, 0))
    return pl.pallas_call(
        kern, name=name, grid=(rows // tr,), in_specs=[spec] * 4, out_specs=[spec] * 3,
        out_shape=[jax.ShapeDtypeStruct((rows, cols), f32)] * 3,
        compiler_params=pltpu.CompilerParams(dimension_semantics=("parallel",)))(w, g, m, v)


def _packed_rows(shape):
    return -(-int(np.prod(shape)) // (SUBLANE * LANE)) * SUBLANE


def _pack(arrays):
    rows = []
    for a in arrays:
        flat = a.reshape(-1)
        rows.append(jnp.pad(flat, (0, _packed_rows(a.shape) * LANE - flat.shape[0])).reshape(-1, LANE))
    return jnp.concatenate(rows, axis=0)


def _unpack(buf, shapes):
    out, at = [], 0
    for s in shapes:
        rows = _packed_rows(s)
        out.append(buf[at:at + rows].reshape(-1)[:int(np.prod(s))].reshape(s))
        at += rows
    return out


SHARD_AXIS = {"w_in": 2, "w_uq": 2, "w_ukv": 2, "lru_conv_w": 2, "w_o": 1, "w_up": 2, "ffn_conv_w": 2, "w_down": 1,
              "w_ple_gate": 1, "w_ple_proj": 2}
SHARDED = [k for k in WEIGHTS if k in SHARD_AXIS]
REPLICATED = [k for k in WEIGHTS if k not in SHARD_AXIS]
ELEMENTWISE_F32 = ("lru_conv_w", "ffn_conv_w")
N_SHARDS = 4
BF16_TILE_ROWS = 16


HBM_SPEC = pl.BlockSpec(memory_space=pl.ANY)
SEM_SPEC = pl.BlockSpec(memory_space=pltpu.SEMAPHORE)
SPLIT_EFFECT = pltpu.SideEffectType.DATAFLOW_SIDE_EFFECTING
CHIP_FLIPS = ((1, 0), (0, 1), (1, 1))
N_DEVICES = 8
SUM_BLOCK_BYTES = 4 * 2 ** 20


def _device_index():
    return 4 * lax.axis_index("x") + 2 * lax.axis_index("y") + lax.axis_index("c")


def _when(cond, fn):
    if cond is None:
        fn()
    else:
        pl.when(cond)(fn)


class _Exchange:
    def __init__(self, name, plan, srcs, land_shapes, n_send, n_recv):
        self.name, self.plan, self.srcs, self.n = name, plan, list(srcs), len(srcs)
        self.land_shapes, self.n_send, self.n_recv = land_shapes, n_send, n_recv

    def run(self):
        n = self.n

        def body(*refs):
            sends, arrivals = self.plan(refs[:n], refs[n:2 * n], refs[2 * n], refs[2 * n + 1])
            for cond, cp in sends:
                _when(cond, cp.start)
            for cond, cp in arrivals:
                _when(cond, cp.wait_recv)
            for cond, cp in sends:
                _when(cond, cp.wait_send)

        return pl.pallas_call(
            body, name=self.name, out_shape=self.land_shapes, in_specs=[HBM_SPEC] * n, out_specs=[HBM_SPEC] * n,
            scratch_shapes=[pltpu.SemaphoreType.DMA((self.n_send,)), pltpu.SemaphoreType.DMA((self.n_recv,))])(*self.srcs)

    def start(self, after=None):
        n = self.n
        lands = [lax.empty(s.shape, s.dtype) for s in self.land_shapes]
        extra = [] if after is None else [after]

        def body(*refs):
            ins, lands_in = refs[:n], refs[n:2 * n]
            send_sems, recv_sems, token = refs[2 * n + len(extra)], refs[2 * n + len(extra) + 1], refs[-1]
            sends, _ = self.plan(ins, lands_in, send_sems, recv_sems)
            for cond, cp in sends:
                _when(cond, cp.start)
            token[...] = jnp.zeros_like(token)

        hbm = [pltpu.with_memory_space_constraint(a, pltpu.HBM) for a in self.srcs + lands]
        res = pl.pallas_call(
            body, name=self.name + "_start",
            out_shape=(pltpu.SemaphoreType.DMA((self.n_send,)), pltpu.SemaphoreType.DMA((self.n_recv,)),
                       *[pltpu.HBM(a.shape, a.dtype) for a in hbm], jax.ShapeDtypeStruct((SUBLANE, LANE), f32)),
            in_specs=[HBM_SPEC] * (2 * n + len(extra)),
            out_specs=(SEM_SPEC, SEM_SPEC, *[HBM_SPEC] * (2 * n), pl.BlockSpec(memory_space=pltpu.VMEM)),
            input_output_aliases={i: 2 + i for i in range(2 * n)},
            compiler_params=pltpu.CompilerParams(has_side_effects=SPLIT_EFFECT))(*hbm, *extra)
        self.sems, self.thru, token = res[:2], res[2:2 + 2 * n], res[-1]
        return token[0, 0]

    def finish(self, after):
        n = self.n

        def body(*refs):
            ins, lands_in, send_sems, recv_sems = refs[:n], refs[n:2 * n], refs[2 * n], refs[2 * n + 1]
            sends, arrivals = self.plan(ins, lands_in, send_sems, recv_sems)
            for cond, cp in arrivals:
                _when(cond, cp.wait_recv)
            for cond, cp in sends:
                _when(cond, cp.wait_send)

        res = pl.pallas_call(
            body, name=self.name + "_finish", out_shape=tuple(pltpu.HBM(a.shape, a.dtype) for a in self.thru),
            in_specs=[HBM_SPEC] * (2 * n) + [SEM_SPEC, SEM_SPEC, HBM_SPEC], out_specs=tuple([HBM_SPEC] * (2 * n)),
            input_output_aliases={i: i for i in range(2 * n)},
            compiler_params=pltpu.CompilerParams(has_side_effects=SPLIT_EFFECT))(*self.thru, *self.sems, after)
        return list(res[n:])


def _gather_exchange(name, shards):
    def plan(ins, lands, send_sems, recv_sems):
        x, y, c = (lax.axis_index(a) for a in MESH_AXES)
        copies = []
        for i in range(len(ins)):
            for k, (fx, fy) in enumerate(CHIP_FLIPS):
                peer = (1 - x if fx else x, 1 - y if fy else y, c)
                copies.append((None, pltpu.make_async_remote_copy(
                    src_ref=ins[i], dst_ref=lands[i].at[2 * x + y], send_sem=send_sems.at[3 * i + k],
                    recv_sem=recv_sems.at[3 * i + k], device_id=peer, device_id_type=pl.DeviceIdType.MESH)))
        return copies, copies

    n = len(shards)
    return _Exchange(name, plan, shards, [jax.ShapeDtypeStruct((N_SHARDS,) + s.shape, s.dtype) for s in shards], 3 * n, 3 * n)


def _scatter_exchange(name, layer, chunks):
    def plan(ins, lands, send_sems, recv_sems):
        x, y, c = (lax.axis_index(a) for a in MESH_AXES)
        me = _device_index()
        sends, arrivals = [], []
        for i in range(len(ins)):
            for j in range(N_SHARDS):
                target = (j // 2, j % 2, layer)
                remote = jnp.logical_not((x == target[0]) & (y == target[1]) & (c == layer))
                sends.append((remote, pltpu.make_async_remote_copy(
                    src_ref=ins[i].at[j], dst_ref=lands[i].at[me], send_sem=send_sems.at[N_SHARDS * i + j],
                    recv_sem=recv_sems.at[N_DEVICES * i + me], device_id=target, device_id_type=pl.DeviceIdType.MESH)))
            for s in range(N_DEVICES):
                arrivals.append(((c == layer) & (me != s), pltpu.make_async_remote_copy(
                    src_ref=ins[i].at[0], dst_ref=lands[i].at[s], send_sem=send_sems.at[0],
                    recv_sem=recv_sems.at[N_DEVICES * i + s], device_id=(x, y, c), device_id_type=pl.DeviceIdType.MESH)))
        return sends, arrivals

    n = len(chunks)
    lands = [jax.ShapeDtypeStruct((N_DEVICES,) + ch.shape[1:], ch.dtype) for ch in chunks]
    return _Exchange(name, plan, chunks, lands, N_SHARDS * n, N_DEVICES * n)


def _sum_contributions(name, got, mine):
    _, a, b = got.shape
    ta = _row_tile(a, max(SUBLANE, SUM_BLOCK_BYTES // (N_DEVICES * b * got.dtype.itemsize) // SUBLANE * SUBLANE))

    def kern(got_ref, mine_ref, o_ref):
        me = _device_index()
        acc = jnp.zeros(o_ref.shape, f32)
        for s in range(N_DEVICES):
            acc = acc + jnp.where(me == s, mine_ref[...].astype(f32), got_ref[s].astype(f32))
        o_ref[...] = acc

    return pl.pallas_call(
        kern, name=name, grid=(a // ta,),
        in_specs=[pl.BlockSpec((N_DEVICES, ta, b), lambda i: (0, i, 0)), pl.BlockSpec((ta, b), lambda i: (i, 0))],
        out_specs=pl.BlockSpec((ta, b), lambda i: (i, 0)), out_shape=jax.ShapeDtypeStruct((a, b), f32),
        compiler_params=pltpu.CompilerParams(dimension_semantics=("parallel",)))(got, mine)


def _swap_layers(name, sums):
    n = len(sums[0])

    def body(*refs):
        srcs = (refs[:n], refs[n:2 * n])
        outs, (send_sems, recv_sems) = refs[2 * n:3 * n], refs[3 * n:]
        x, y, c = (lax.axis_index(a) for a in MESH_AXES)
        for i in range(n):
            for layer in range(DEPTH):
                cp = pltpu.make_async_remote_copy(
                    src_ref=srcs[layer][i], dst_ref=outs[i], send_sem=send_sems.at[i], recv_sem=recv_sems.at[i],
                    device_id=(x, y, 1 - c), device_id_type=pl.DeviceIdType.MESH)
                pl.when(c == layer)(cp.start)
        for i in range(n):
            pltpu.make_async_remote_copy(
                src_ref=srcs[0][i], dst_ref=outs[i], send_sem=send_sems.at[i], recv_sem=recv_sems.at[i],
                device_id=(x, y, 1 - c), device_id_type=pl.DeviceIdType.MESH).wait()

    return pl.pallas_call(
        body, name=name, out_shape=[jax.ShapeDtypeStruct(s.shape, s.dtype) for s in sums[0]],
        in_specs=[HBM_SPEC] * (2 * n), out_specs=[HBM_SPEC] * n,
        scratch_shapes=[pltpu.SemaphoreType.DMA((n,)), pltpu.SemaphoreType.DMA((n,))])(*sums[0], *sums[1])


def _stack_shards(g, axis):
    if axis == 1:
        return g.reshape(N_SHARDS, g.shape[0] // N_SHARDS, g.shape[1])
    return g.reshape(g.shape[0], N_SHARDS, g.shape[1] // N_SHARDS).transpose(1, 0, 2)


def _join_shards(s, axis):
    if axis == 1:
        return s.reshape(-1, s.shape[2])
    return s.transpose(1, 0, 2).reshape(s.shape[1], -1)


def _layer_shards(w, l, names):
    return [w[k][l] if k in ELEMENTWISE_F32 else w[k][l].astype(bf16) for k in names]


def _full_weights(names, sent, got):
    j = 2 * lax.axis_index("x") + lax.axis_index("y")
    return {k: _join_shards(lax.dynamic_update_slice(g, own[None], (j, 0, 0)), SHARD_AXIS[k])
            for k, own, g in zip(names, sent, got)}


def _grad_chunks(grads, names):
    return [_stack_shards(grads[k], SHARD_AXIS[k]).astype(bf16) for k in names]


def _sum_group(l, names, got, chunks):
    j = 2 * lax.axis_index("x") + lax.axis_index("y")
    return {k: _sum_contributions(f"sum_l{l}_{k}", g, lax.dynamic_index_in_dim(ch, j, 0, keepdims=False))
            for k, g, ch in zip(names, got, chunks)}


def _both_layers(name, names, sums):
    c = lax.axis_index("c")
    mine = [[sums[l][k] for k in names] for l in range(DEPTH)]
    other = _swap_layers(name, mine)
    return {k: jnp.stack([jnp.where(c == 0, mine[0][i], other[i]), jnp.where(c == 0, other[i], mine[1][i])])
            for i, k in enumerate(names)}


def _gather_all_exchange(name, src):
    def plan(ins, lands, send_sems, recv_sems):
        coords = [lax.axis_index(a) for a in MESH_AXES]
        me = _device_index()
        sends, arrivals = [], []
        for f in range(1, N_DEVICES):
            peer = tuple(1 - cd if (f >> (2 - b)) & 1 else cd for b, cd in enumerate(coords))
            sends.append((None, pltpu.make_async_remote_copy(
                src_ref=ins[0], dst_ref=lands[0].at[me], send_sem=send_sems.at[f - 1], recv_sem=recv_sems.at[me],
                device_id=peer, device_id_type=pl.DeviceIdType.MESH)))
        for s in range(N_DEVICES):
            arrivals.append((me != s, pltpu.make_async_remote_copy(
                src_ref=ins[0], dst_ref=lands[0].at[s], send_sem=send_sems.at[0], recv_sem=recv_sems.at[s],
                device_id=tuple(coords), device_id_type=pl.DeviceIdType.MESH)))
        return sends, arrivals

    return _Exchange(name, plan, [src], [jax.ShapeDtypeStruct((N_DEVICES,) + src.shape, src.dtype)], N_DEVICES - 1, N_DEVICES)


def kernel(x, p, positions, g_mix, w_in, g_qc, w_uq, g_kvc, w_ukv, b_f, lru_conv_w, lru_conv_b, w_r, b_r, w_i, b_i, lru_lambda, g_out, w_o, g_ffn, w_up, ffn_conv_w, ffn_conv_b, w_down, g_ple, w_ple_gate, w_ple_proj, g_final, loss_target, m_g_mix, m_w_in, m_g_qc, m_w_uq, m_g_kvc, m_w_ukv, m_b_f, m_lru_conv_w, m_lru_conv_b, m_w_r, m_b_r, m_w_i, m_b_i, m_lru_lambda, m_g_out, m_w_o, m_g_ffn, m_w_up, m_ffn_conv_w, m_ffn_conv_b, m_w_down, m_g_ple, m_w_ple_gate, m_w_ple_proj, m_g_final, v_g_mix, v_w_in, v_g_qc, v_w_uq, v_g_kvc, v_w_ukv, v_b_f, v_lru_conv_w, v_lru_conv_b, v_w_r, v_b_r, v_w_i, v_b_i, v_lru_lambda, v_g_out, v_w_o, v_g_ffn, v_w_up, v_ffn_conv_w, v_ffn_conv_b, v_w_down, v_g_ple, v_w_ple_gate, v_w_ple_proj, v_g_final):
    given = locals()
    w = {k: given[k] for k in WEIGHTS}
    m = {k: given["m_" + k] for k in WEIGHTS}
    v = {k: given["v_" + k] for k in WEIGHTS}

    parts = {"mix": MIX_PART, "ffn": FFN_PART}
    groups = [(l, part) for l in range(DEPTH) for part in ("mix", "ffn")]
    sent = {g: _layer_shards(w, g[0], parts[g[1]]) for g in groups}
    first = _gather_exchange("gather_l0_mix", sent[groups[0]]).run()
    ahead = {g: _gather_exchange(f"gather_l{g[0]}_{g[1]}", sent[g]) for g in groups[1:]}
    pos = positions[0].astype(f32).reshape(-1, 1)
    for ex in ahead.values():
        pos = pos + ex.start(after=first[0])
    behind, layer_grads, chunks = {}, [{} for _ in range(DEPTH)], {}

    def weights_of(l, part, after):
        g = (l, part)
        full = _full_weights(parts[part], sent[g], first if g == groups[0] else ahead[g].finish(after=after))
        if part == "mix":
            full.update({k: w[k][l] for k in LAYER_WEIGHTS if k in REPLICATED})
        return full

    def grads_to(l, part, grads):
        g = (l, part)
        layer_grads[l].update(grads)
        chunks[g] = _grad_chunks(grads, parts[part])
        if g == groups[0]:
            return jnp.zeros((), f32)
        behind[g] = _scatter_exchange(f"scatter_l{l}_{part}", l, chunks[g])
        return behind[g].start()

    loss, dx, dg_final = _local_step(x[0], p[:, 0], pos, loss_target[0], w["g_final"], weights_of, grads_to)

    grads = {k: jnp.stack([layer_grads[l][k] for l in range(DEPTH)]) for k in LAYER_WEIGHTS if k in REPLICATED}
    grads["g_final"] = dg_final
    rep_shapes = [w[k].shape for k in REPLICATED] + [(1,)]
    contrib = _pack([grads[k] for k in REPLICATED] + [loss.reshape(1)])
    last = _scatter_exchange("scatter_l0_mix", 0, chunks[groups[0]])
    everyone = _gather_all_exchange("gather_replicated", contrib)
    started = (last.start() + everyone.start() + dx[0, 0]).reshape(1, 1)

    def adamw_of(names, g_sharded):
        out = {}
        for k in names:
            shape = w[k].shape
            flat = [t.reshape(-1, shape[-1]) for t in (w[k], g_sharded[k], m[k], v[k])]
            out[k] = [t.reshape(shape) for t in (flat[1],) + tuple(_adamw("adamw_" + k, *flat))]
        return out

    sums = [{} for _ in range(DEPTH)]
    for g in groups[1:]:
        sums[g[0]].update(_sum_group(g[0], parts[g[1]], behind[g].finish(after=started), chunks[g]))
    big = adamw_of(FFN_PART, _both_layers("swap_ffn", FFN_PART, sums))
    sums[0].update(_sum_group(0, MIX_PART, last.finish(after=big[FFN_PART[0]][1]), chunks[groups[0]]))
    big.update(adamw_of(MIX_PART, _both_layers("swap_mix", MIX_PART, sums)))

    g_rep = _sum_contributions("sum_replicated", everyone.finish(after=big[MIX_PART[0]][1])[0], contrib)
    zero = jnp.zeros((1,), f32)
    w_rep, m_rep, v_rep = (_pack([t[k] for k in REPLICATED] + [zero]) for t in (w, m, v))
    rep = [_unpack(b, rep_shapes) for b in (g_rep,) + tuple(_adamw("adamw_replicated", w_rep, g_rep, m_rep, v_rep))]

    outs = []
    for kind in range(4):
        by_name = {k: big[k][kind] for k in SHARDED}
        by_name.update(zip(REPLICATED, rep[kind][:-1]))
        outs.append([by_name[k] for k in WEIGHTS])
    total_loss = rep[0][-1][0]
    return (total_loss, dx.reshape(x.shape), *outs[0], *outs[1], *outs[2], *outs[3])
```

```python
import functools
import math

import numpy as np
import jax
import jax.numpy as jnp
from jax import lax
from jax.experimental import pallas as pl
from jax.experimental.pallas import tpu as pltpu

f32, bf16 = jnp.float32, jnp.bfloat16

D_MODEL = 1024
PLE_DIM = 256
MLA_HEADS, MLA_NOPE, MLA_ROPE, MLA_V = 4, 64, 32, 64
MLA_Q_RANK, MLA_KV_RANK = 192, 128
FOX_HEADS, FOX_HEAD_DIM = 4, 64
LRU_WIDTH, LRU_BLOCKS, LRU_BLOCK, LRU_CONV, LRU_C = 512, 8, 64, 4, 8.0
D_FF, FFN_CONV = 2816, 3
ROPE_THETA = 10000.0
EPS = 1e-6
DEPTH = 2
ADAM_LR, ADAM_B1, ADAM_B2, ADAM_EPS, ADAM_WD, ADAM_STEP = 0.001, 0.9, 0.999, 1e-08, 0.01, 10

LANE = 128
SUBLANE = 8
HEADS = 4

Z_FQ, Z_FK, Z_FV, Z_LX, Z_LG, Z_QC, Z_KVC, Z_KR, Z_FL, Z_W = 0, 512, 1024, 1536, 2048, 2560, 2816, 2944, 3072, 3200
QC_W = 256
ROPE_AT = 64


def _head_pad_map(n_heads, width):
    m = -np.ones(n_heads * LANE, np.int64)
    for h in range(n_heads):
        m[h * LANE:h * LANE + width] = h * width + np.arange(width)
    return m


def _z_map():
    m = -np.ones(Z_W, np.int64)
    o_qc, o_kvc, o_kr = 0, MLA_Q_RANK, MLA_Q_RANK + MLA_KV_RANK
    o_fq = o_kr + MLA_ROPE
    o_fk, o_fv = o_fq + 256, o_fq + 512
    o_fl = o_fv + 256
    o_lx = o_fl + FOX_HEADS
    o_lg = o_lx + LRU_WIDTH
    m[Z_FQ:Z_FQ + 512] = np.where(_head_pad_map(4, 64) >= 0, _head_pad_map(4, 64) + o_fq, -1)
    m[Z_FK:Z_FK + 512] = np.where(_head_pad_map(4, 64) >= 0, _head_pad_map(4, 64) + o_fk, -1)
    m[Z_FV:Z_FV + 512] = np.where(_head_pad_map(4, 64) >= 0, _head_pad_map(4, 64) + o_fv, -1)
    m[Z_LX:Z_LX + 512] = o_lx + np.arange(512)
    m[Z_LG:Z_LG + 512] = o_lg + np.arange(512)
    m[Z_QC:Z_QC + MLA_Q_RANK] = o_qc + np.arange(MLA_Q_RANK)
    m[Z_KVC:Z_KVC + MLA_KV_RANK] = o_kvc + np.arange(MLA_KV_RANK)
    m[Z_KR + ROPE_AT:Z_KR + ROPE_AT + MLA_ROPE] = o_kr + np.arange(MLA_ROPE)
    m[Z_FL:Z_FL + FOX_HEADS] = o_fl + np.arange(FOX_HEADS)
    return m


def _ukv_map():
    m = -np.ones(2 * HEADS * LANE, np.int64)
    for h in range(HEADS):
        m[h * LANE:h * LANE + MLA_NOPE] = h * (MLA_NOPE + MLA_V) + np.arange(MLA_NOPE)
        m[HEADS * LANE + h * LANE:HEADS * LANE + h * LANE + MLA_V] = h * (MLA_NOPE + MLA_V) + MLA_NOPE + np.arange(MLA_V)
    return m


def _omix_map():
    return np.concatenate([_head_pad_map(4, 64), np.where(_head_pad_map(4, 64) >= 0, _head_pad_map(4, 64) + 256, -1),
                           512 + np.arange(512)])


def _pad_to(m, n):
    return np.concatenate([m, -np.ones(n - m.shape[0], np.int64)])


def _take_pad(a, m, axis):
    out = jnp.take(a, jnp.asarray(np.maximum(m, 0), jnp.int32), axis=axis)
    shape = [1] * a.ndim
    shape[axis] = m.shape[0]
    return out * jnp.asarray((m >= 0).reshape(shape), a.dtype)


def _take_inv(a, m, axis):
    n = int(m.max()) + 1
    inv = np.zeros(n, np.int64)
    inv[m[m >= 0]] = np.nonzero(m >= 0)[0]
    return jnp.take(a, jnp.asarray(inv, jnp.int32), axis=axis)


Z_MAP = _z_map()
UQ_COL_MAP = _head_pad_map(HEADS, MLA_NOPE + MLA_ROPE)
UQ_ROW_MAP = _pad_to(np.arange(MLA_Q_RANK), QC_W)
UKV_MAP = _ukv_map()
OMIX_MAP = _omix_map()
OMIX_W = 1536


def _rope_tables(width, at):
    half = MLA_ROPE // 2
    inv = ROPE_THETA ** (-np.arange(half, dtype=np.float32) / half)
    freq = np.zeros((1, width), np.float32)
    m1 = np.zeros((1, width), np.float32)
    m2 = np.zeros((1, width), np.float32)
    for h in range(width // LANE):
        b = h * LANE + at
        freq[0, b:b + half] = inv
        freq[0, b + half:b + 2 * half] = inv
        m1[0, b:b + half] = 1.0
        m2[0, b + half:b + 2 * half] = 1.0
    return freq, m1, m2


def _view(r):
    return r if isinstance(r, tuple) else (r, r.shape[1], 0)


def _blk(dim, cap):
    if dim <= cap:
        return dim
    for b in range(cap, LANE - 1, -LANE):
        if dim % b == 0:
            return b
    return dim


@functools.partial(jax.custom_vjp, nondiff_argnums=(1, 2))
def _roll(x, shift, axis):
    return pltpu.roll(x, shift, axis)


def _roll_fwd(x, shift, axis):
    return pltpu.roll(x, shift, axis), None


def _roll_bwd(shift, axis, _, g):
    return (pltpu.roll(g, g.shape[axis] - shift, axis),)


_roll.defvjp(_roll_fwd, _roll_bwd)


def _rowwise(name, fn, rows, pars, outs, tb=256):
    rows = [_view(r) for r in rows]
    n = rows[0][0].shape[0]
    tb = min(tb, n)
    nr, npar = len(rows), len(pars)

    def kern(*refs):
        r = [refs[k][...].astype(f32) for k in range(nr)]
        p = [refs[nr + k][...] for k in range(npar)]
        res = fn(*r, *p)
        for o_ref, o in zip(refs[nr + npar:], res):
            o_ref[...] = o.astype(o_ref.dtype)

    in_specs = [pl.BlockSpec((tb, w), lambda i, j=idx: (i, j)) for (_, w, idx) in rows]
    in_specs += [pl.BlockSpec(p.shape, lambda i: (0, 0)) for p in pars]
    out_specs = [pl.BlockSpec((tb, w), lambda i: (i, 0)) for (w, _) in outs]
    out_shape = [jax.ShapeDtypeStruct((n, w), dt) for (w, dt) in outs]
    return pl.pallas_call(kern, name=name, grid=(n // tb,), in_specs=in_specs, out_specs=out_specs, out_shape=out_shape,
                          compiler_params=pltpu.CompilerParams(dimension_semantics=("parallel",)))(*[r[0] for r in rows], *pars)


def _rowwise_bwd(name, fn, rows, pars, cts, ndiff, adds=None, tb=256, dts=None):
    rows = [_view(r) for r in rows]
    dts = dts or [f32] * ndiff
    adds = adds or {}
    add_keys = sorted(adds)
    n = rows[0][0].shape[0]
    tb = min(tb, n)
    nr, npar, nct, nadd = len(rows), len(pars), len(cts), len(add_keys)

    def kern(*refs):
        i = pl.program_id(0)
        r = [refs[k][...].astype(f32) for k in range(nr)]
        p = [refs[nr + k][...] for k in range(npar)]
        ct = [refs[nr + npar + k][...].astype(f32) for k in range(nct)]
        ad = {key: refs[nr + npar + nct + k][...] for k, key in enumerate(add_keys)}
        o_refs = refs[nr + npar + nct + nadd:]

        def g(*d):
            return tuple(fn(*d[:ndiff], *r[ndiff:], *d[ndiff:]))

        _, vjp = jax.vjp(g, *r[:ndiff], *p)
        grads = vjp(tuple(ct))
        for k in range(ndiff):
            gk = grads[k]
            if k in ad:
                gk = gk + ad[k]
            o_refs[k][...] = gk.astype(o_refs[k].dtype)

        @pl.when(i == 0)
        def _():
            for k in range(npar):
                o_refs[ndiff + k][...] = jnp.zeros_like(o_refs[ndiff + k])

        for k in range(npar):
            o_refs[ndiff + k][...] += grads[ndiff + k]

    in_specs = [pl.BlockSpec((tb, w), lambda i, j=idx: (i, j)) for (_, w, idx) in rows]
    in_specs += [pl.BlockSpec(p.shape, lambda i: (0, 0)) for p in pars]
    in_specs += [pl.BlockSpec((tb, c.shape[1]), lambda i: (i, 0)) for c in cts]
    in_specs += [pl.BlockSpec((tb, adds[k].shape[1]), lambda i: (i, 0)) for k in add_keys]
    out_specs = [pl.BlockSpec((tb, rows[k][1]), lambda i: (i, 0)) for k in range(ndiff)]
    out_specs += [pl.BlockSpec(p.shape, lambda i: (0, 0)) for p in pars]
    out_shape = [jax.ShapeDtypeStruct((n, rows[k][1]), dts[k]) for k in range(ndiff)]
    out_shape += [jax.ShapeDtypeStruct(p.shape, f32) for p in pars]
    res = pl.pallas_call(kern, name=name, grid=(n // tb,), in_specs=in_specs, out_specs=out_specs, out_shape=out_shape,
                         compiler_params=pltpu.CompilerParams(dimension_semantics=("arbitrary",)))(
        *[r[0] for r in rows], *pars, *cts, *[adds[k] for k in add_keys])
    return res[:ndiff], res[ndiff:]


_DOT_DIMS = {"nn": ((1,), (0,)), "nt": ((1,), (1,)), "tn": ((0,), (0,))}

MM_VMEM_BUDGET = 36 * 2 ** 20
MM_MAX_TM = 1408
MM_STEP, MM_RESULT, MM_XPOSE, MM_CAST = 700.0, 7.5e-4, 9e-4, 1e-3


def _tile_candidates(dim):
    c = [d for d in range(LANE, dim + 1, LANE) if dim % d == 0]
    return c or [dim]


@functools.lru_cache(maxsize=None)
def _mm_tiles(mode, m, n, k, a_bytes, b_bytes, o_bytes):
    best, best_cost = None, None
    for tm in _tile_candidates(m):
        if tm > MM_MAX_TM:
            continue
        for tn in _tile_candidates(n):
            for tk in _tile_candidates(k):
                vmem = 2 * (tm * tk * a_bytes + tk * tn * b_bytes + tm * tn * o_bytes) + 4 * tm * tn * (2 if tk < k else 1)
                vmem += (2 * tm * tk if a_bytes > 2 else 0) + (2 * tk * tn if b_bytes > 2 else 0)
                if vmem > MM_VMEM_BUDGET:
                    continue
                steps = (m // tm) * (n // tn) * (k // tk)
                cost = steps * MM_STEP + m * n * (k // tk) * MM_RESULT
                if mode == "tn":
                    cost += m * k * (n // tn) * MM_XPOSE
                cost += (m * k * (n // tn) * MM_CAST if a_bytes > 2 else 0) + (k * n * (m // tm) * MM_CAST if b_bytes > 2 else 0)
                if best is None or cost < best_cost:
                    best, best_cost = (tm, tn, tk), cost
    return best


def _mm(name, a, b, mode="nn", out_dtype=f32, res=None):
    if mode == "nn":
        (m, k), (_, n) = a.shape, b.shape
    elif mode == "nt":
        (m, k), (n, _) = a.shape, b.shape
    else:
        (k, m), (_, n) = a.shape, b.shape
    has_res = res is not None
    tm, tn, tk = _mm_tiles(mode, m, n, k, a.dtype.itemsize, b.dtype.itemsize,
                           jnp.dtype(out_dtype).itemsize + (res.dtype.itemsize if has_res else 0))
    nk = k // tk
    dims = (_DOT_DIMS[mode], ((), ()))

    def kern(*refs):
        a_ref, b_ref = refs[0], refs[1]
        o_ref, acc_ref = refs[-2], refs[-1]
        kk = pl.program_id(2)
        part = lax.dot_general(a_ref[...].astype(bf16), b_ref[...].astype(bf16), dims, preferred_element_type=f32)

        def finish(out):
            if has_res:
                out = out + refs[2][...]
            o_ref[...] = out.astype(o_ref.dtype)

        if nk == 1:
            finish(part)
            return

        @pl.when(kk == 0)
        def _():
            acc_ref[...] = part

        @pl.when(jnp.logical_and(kk > 0, kk < nk - 1))
        def _():
            acc_ref[...] += part

        @pl.when(kk == nk - 1)
        def _():
            finish(acc_ref[...] + part)

    if mode == "tn":
        a_spec = pl.BlockSpec((tk, tm), lambda i, j, kk: (kk, i))
    else:
        a_spec = pl.BlockSpec((tm, tk), lambda i, j, kk: (i, kk))
    if mode == "nt":
        b_spec = pl.BlockSpec((tn, tk), lambda i, j, kk: (j, kk))
    else:
        b_spec = pl.BlockSpec((tk, tn), lambda i, j, kk: (kk, j))
    in_specs = [a_spec, b_spec]
    args = [a, b]
    if has_res:
        in_specs.append(pl.BlockSpec((tm, tn), lambda i, j, kk: (i, j)))
        args.append(res)
    return pl.pallas_call(
        kern, name=name, grid=(m // tm, n // tn, nk), in_specs=in_specs,
        out_specs=pl.BlockSpec((tm, tn), lambda i, j, kk: (i, j)),
        out_shape=jax.ShapeDtypeStruct((m, n), out_dtype),
        scratch_shapes=[pltpu.VMEM((tm, tn) if nk > 1 else (SUBLANE, LANE), f32)],
        compiler_params=pltpu.CompilerParams(dimension_semantics=("parallel", "parallel", "arbitrary")))(*args)


ATT_TQ, ATT_TK = 512, 512


def _att_tiles(s_len):
    tk = min(ATT_TK, s_len)
    return min(ATT_TQ, tk), tk


def _fold_scale(scale):
    return (scale, 1.0) if math.frexp(scale)[0] == 0.5 else (1.0, scale)


def _query_rows(x):
    s_len = x.shape[1]
    tq = _att_tiles(s_len)[0]
    return x.reshape(HEADS, s_len // tq, 1, tq)


def _scores_t(kb, q_t, s_mul, ck, diag_offset, tq, tk):
    s = jnp.dot(kb, q_t, preferred_element_type=f32)
    if s_mul != 1.0:
        s = s * s_mul
    if ck is not None:
        s = s - ck
    if diag_offset is None:
        return s
    key = lax.broadcasted_iota(jnp.int32, (tk, tq), 0)
    query = lax.broadcasted_iota(jnp.int32, (tk, tq), 1) + diag_offset
    return jnp.where(key <= query, s, -jnp.inf)


ATT_ROWS = 64


def _finish_scores(s, s_mul, ck, first_row):
    if s_mul != 1.0:
        s = s * s_mul
    if ck is not None:
        s = s - ck
    if first_row is None:
        return s
    row = lax.broadcasted_iota(jnp.int32, s.shape, 0) + first_row
    col = lax.broadcasted_iota(jnp.int32, s.shape, 1)
    return jnp.where(col <= row, s, -jnp.inf)


def _attn_fwd(name, q, k, v, scale, c_row=None):
    (qa, qo), (ka, ko), (va, vo) = q, k, v
    s_len = qa.shape[0]
    t = _att_tiles(s_len)[1]
    nt = s_len // t
    decay = c_row is not None
    q_mul, s_mul = _fold_scale(scale)

    def kern(*refs):
        q_ref, k_ref, v_ref = refs[:3]
        o_ref, lse_ref = refs[-2:]
        i = pl.program_id(1)
        qb = (q_ref[...] * q_mul).astype(bf16)

        def step(j, carry, diagonal):
            m, l, acc = carry
            rows = pl.ds(pl.multiple_of(j * t, t), t)
            kb = k_ref[rows, :].astype(bf16)
            vb = v_ref[rows, :].astype(bf16)
            s = lax.dot_general(qb, kb, (_DOT_DIMS["nt"], ((), ())), preferred_element_type=f32)
            s = _finish_scores(s, s_mul, refs[3][j] if decay else None, 0 if diagonal else None)
            m_new = jnp.maximum(m, jnp.max(s, axis=1, keepdims=True))
            alpha = jnp.exp(m - m_new)
            p = jnp.exp(s - m_new)
            l = alpha * l + jnp.sum(p, axis=1, keepdims=True)
            acc = alpha * acc + jnp.dot(p.astype(bf16), vb, preferred_element_type=f32)
            return m_new, l, acc

        init = (jnp.full((t, 1), -jnp.inf, f32), jnp.zeros((t, 1), f32), jnp.zeros((t, LANE), f32))
        m, l, acc = step(i, lax.fori_loop(0, i, lambda j, c: step(j, c, False), init), True)
        o_ref[...] = acc / l
        lse_ref[...] = m + jnp.log(l)

    in_specs = [pl.BlockSpec((t, LANE), lambda h, i: (i, qo + h)),
                pl.BlockSpec((s_len, LANE), lambda h, i: (0, ko + h)),
                pl.BlockSpec((s_len, LANE), lambda h, i: (0, vo + h))]
    args = [qa, ka, va]
    if decay:
        in_specs.append(pl.BlockSpec((None, nt, 1, t), lambda h, i: (h, 0, 0, 0)))
        args.append(c_row)
    return pl.pallas_call(
        kern, name=name, grid=(HEADS, nt), in_specs=in_specs,
        out_specs=[pl.BlockSpec((t, LANE), lambda h, i: (i, h)), pl.BlockSpec((None, t, 1), lambda h, i: (h, i, 0))],
        out_shape=[jax.ShapeDtypeStruct((s_len, HEADS * LANE), f32), jax.ShapeDtypeStruct((HEADS, s_len, 1), f32)],
        compiler_params=pltpu.CompilerParams(dimension_semantics=("parallel", "arbitrary")))(*args)


def _attn_dq(name, q, k, v, o, do, lse, scale, c_row=None):
    (qa, qo), (ka, ko), (va, vo) = q, k, v
    s_len = qa.shape[0]
    t = _att_tiles(s_len)[1]
    nt = s_len // t
    decay = c_row is not None
    q_mul, s_mul = _fold_scale(scale)

    rp = min(ATT_ROWS, t)

    def kern(*refs):
        q_ref, k_ref, v_ref, o_ref, do_ref, lse_ref = refs[:6]
        dq_ref, delta_ref, drow_ref, s_ref, dp_ref, ds_ref = refs[-6:]
        i = pl.program_id(1)
        qb = (q_ref[...] * q_mul).astype(bf16)
        dob = do_ref[...]
        delta_ref[...] = jnp.sum(dob * o_ref[...], axis=1, keepdims=True)
        dob = dob.astype(bf16)
        drow_ref[...] = jnp.zeros((t, 1), f32)
        dq_ref[...] = jnp.zeros((t, LANE), f32)

        def step(j, diagonal):
            rows = pl.ds(pl.multiple_of(j * t, t), t)
            kb = k_ref[rows, :].astype(bf16)
            s_ref[...] = lax.dot_general(qb, kb, (_DOT_DIMS["nt"], ((), ())), preferred_element_type=f32)
            dp_ref[...] = lax.dot_general(dob, v_ref[rows, :].astype(bf16), (_DOT_DIMS["nt"], ((), ())),
                                          preferred_element_type=f32)
            ck = refs[6][j] if decay else None

            def rows_of(c, carry):
                r = slice(c * rp, (c + 1) * rp)
                s = _finish_scores(s_ref[r, :], s_mul, ck, c * rp if diagonal else None)
                ds = jnp.exp(s - lse_ref[r, :]) * (dp_ref[r, :] - delta_ref[r, :])
                drow_ref[r, :] += jnp.sum(ds, axis=1, keepdims=True)
                ds_ref[r, :] = ds.astype(bf16)
                return carry

            for c in range(t // rp):
                rows_of(c, 0)
            dq_ref[...] += jnp.dot(ds_ref[...], kb, preferred_element_type=f32)

        def below(j, carry):
            step(j, False)
            return carry

        lax.fori_loop(0, i, below, 0)
        step(i, True)
        dq_ref[...] = dq_ref[...] * scale

    in_specs = [pl.BlockSpec((t, LANE), lambda h, i: (i, qo + h)),
                pl.BlockSpec((s_len, LANE), lambda h, i: (0, ko + h)),
                pl.BlockSpec((s_len, LANE), lambda h, i: (0, vo + h)),
                pl.BlockSpec((t, LANE), lambda h, i: (i, h)),
                pl.BlockSpec((t, LANE), lambda h, i: (i, h)),
                pl.BlockSpec((None, t, 1), lambda h, i: (h, i, 0))]
    args = [qa, ka, va, o, do, lse]
    if decay:
        in_specs.append(pl.BlockSpec((None, nt, 1, t), lambda h, i: (h, 0, 0, 0)))
        args.append(c_row)
    col = pl.BlockSpec((None, t, 1), lambda h, i: (h, i, 0))
    return pl.pallas_call(
        kern, name=name, grid=(HEADS, nt), in_specs=in_specs,
        out_specs=[pl.BlockSpec((t, LANE), lambda h, i: (i, h)), col, col],
        out_shape=[jax.ShapeDtypeStruct((s_len, HEADS * LANE), f32), jax.ShapeDtypeStruct((HEADS, s_len, 1), f32),
                   jax.ShapeDtypeStruct((HEADS, s_len, 1), f32)],
        scratch_shapes=[pltpu.VMEM((t, t), f32), pltpu.VMEM((t, t), f32), pltpu.VMEM((t, t), bf16)],
        compiler_params=pltpu.CompilerParams(dimension_semantics=("parallel", "arbitrary")))(*args)


def _attn_dkv(name, q, k, v, do, lse, delta, scale, c_col=None):
    (qa, qo), (ka, ko), (va, vo) = q, k, v
    s_len = qa.shape[0]
    tq, tk = _att_tiles(s_len)
    nq, per = s_len // tq, tk // tq
    decay = c_col is not None
    q_mul, s_mul = _fold_scale(scale)

    def kern(*refs):
        q_ref, k_ref, v_ref, do_ref, lse_ref, delta_ref = refs[:6]
        j = pl.program_id(1)
        kb = k_ref[...].astype(bf16)
        vb = v_ref[...].astype(bf16)
        ck = refs[6][...] if decay else None

        def step(i, carry, diagonal):
            dk, dv, dsum = carry
            for d in range(per):
                tile = i * per + d
                rows = pl.ds(pl.multiple_of(tile * tq, tq), tq)
                qb = (q_ref[rows, :] * q_mul).astype(bf16)
                dob = do_ref[rows, :].astype(bf16)
                s = _scores_t(kb, qb.T, s_mul, ck, d * tq if diagonal else None, tq, tk)
                p = jnp.exp(s - lse_ref[tile])
                dv = dv + jnp.dot(p.astype(bf16), dob, preferred_element_type=f32)
                dp = jnp.dot(vb, dob.T, preferred_element_type=f32)
                ds = p * (dp - delta_ref[tile])
                dk = dk + jnp.dot(ds.astype(bf16), qb, preferred_element_type=f32)
                if decay:
                    dsum = dsum + ds
            return dk, dv, dsum

        init = (jnp.zeros((tk, LANE), f32), jnp.zeros((tk, LANE), f32), jnp.zeros((tk, tq), f32))
        dk, dv, dsum = lax.fori_loop(j + 1, s_len // tk, lambda i, c: step(i, c, False), step(j, init, True))
        if decay:
            dk_ref, dv_ref, dc_ref = refs[-3:]
            dc_ref[...] = -jnp.sum(dsum, axis=1, keepdims=True)
        else:
            dk_ref, dv_ref = refs[-2:]
        dk_ref[...] = dk * s_mul
        dv_ref[...] = dv

    stat = pl.BlockSpec((None, nq, 1, tq), lambda h, j: (h, 0, 0, 0))
    in_specs = [pl.BlockSpec((s_len, LANE), lambda h, j: (0, qo + h)),
                pl.BlockSpec((tk, LANE), lambda h, j: (j, ko + h)),
                pl.BlockSpec((tk, LANE), lambda h, j: (j, vo + h)),
                pl.BlockSpec((s_len, LANE), lambda h, j: (0, h)), stat, stat]
    args = [qa, ka, va, do, lse, delta]
    out_specs = [pl.BlockSpec((tk, LANE), lambda h, j: (j, h)), pl.BlockSpec((tk, LANE), lambda h, j: (j, h))]
    out_shape = [jax.ShapeDtypeStruct((s_len, HEADS * LANE), f32), jax.ShapeDtypeStruct((s_len, HEADS * LANE), f32)]
    if decay:
        in_specs.append(pl.BlockSpec((None, tk, 1), lambda h, j: (h, j, 0)))
        args.append(c_col)
        out_specs.append(pl.BlockSpec((None, tk, 1), lambda h, j: (h, j, 0)))
        out_shape.append(jax.ShapeDtypeStruct((HEADS, s_len, 1), f32))
    return pl.pallas_call(
        kern, name=name, grid=(HEADS, s_len // tk), in_specs=in_specs, out_specs=out_specs, out_shape=out_shape,
        compiler_params=pltpu.CompilerParams(dimension_semantics=("parallel", "arbitrary")))(*args)


CONV_TS, CONV_CB = 1024, 256
FFN_ROWS = 64


def _conv_fwd(name, x, w, b, taps):
    xa, width, xidx = _view(x)
    s_len = xa.shape[0]
    ts, cb = min(CONV_TS, s_len), CONV_CB
    xo = xidx * width // cb

    def kern(x_ref, halo_ref, w_ref, b_ref, o_ref):
        i = pl.program_id(1)
        xb = x_ref[...]
        halo = jnp.where(i == 0, 0.0, halo_ref[...])
        xx = jnp.concatenate([halo, xb], axis=0)
        out = b_ref[...] + w_ref[taps - 1:taps, :] * xb
        for k in range(taps - 1):
            out = out + w_ref[k:k + 1, :] * pltpu.roll(xx, taps - 1 - k, 0)[SUBLANE:]
        o_ref[...] = out

    return pl.pallas_call(
        kern, name=name, grid=(width // cb, s_len // ts),
        in_specs=[pl.BlockSpec((ts, cb), lambda j, i: (i, xo + j)),
                  pl.BlockSpec((SUBLANE, cb), lambda j, i: (jnp.maximum(i * (ts // SUBLANE) - 1, 0), xo + j)),
                  pl.BlockSpec((taps, cb), lambda j, i: (0, j)),
                  pl.BlockSpec((1, cb), lambda j, i: (0, j))],
        out_specs=pl.BlockSpec((ts, cb), lambda j, i: (i, j)),
        out_shape=jax.ShapeDtypeStruct((s_len, width), f32),
        compiler_params=pltpu.CompilerParams(dimension_semantics=("parallel", "parallel")))(xa, xa, w, b)


def _conv_bwd(name, x, dout, w, taps, dout2=None, dx_dtype=f32):
    xa, width, xidx = _view(x)
    s_len = xa.shape[0]
    ts, cb = min(CONV_TS, s_len), CONV_CB
    xo = xidx * width // cb
    n_i = s_len // ts
    two = dout2 is not None

    def kern(*refs):
        x_ref, halo_ref, w_ref = refs[:3]
        dx_ref, dw_ref, db_ref = refs[-3:]
        i = pl.program_id(1)
        if two:
            d = refs[3][...] + refs[5][...]
            dn = refs[4][...] + refs[6][...]
        else:
            d, dn = refs[3][...], refs[4][...]
        dn = jnp.where(i == n_i - 1, 0.0, dn)
        xb = x_ref[...]
        halo = jnp.where(i == 0, 0.0, halo_ref[...])
        xx = jnp.concatenate([halo, xb], axis=0)
        dd = jnp.concatenate([d, dn], axis=0)

        @pl.when(i == 0)
        def _():
            dw_ref[...] = jnp.zeros_like(dw_ref)
            db_ref[...] = jnp.zeros_like(db_ref)

        dx = w_ref[taps - 1:taps, :] * d
        dw_ref[taps - 1:taps, :] += jnp.sum(d * xb, axis=0, keepdims=True)
        for k in range(taps - 1):
            sh = taps - 1 - k
            dx = dx + w_ref[k:k + 1, :] * pltpu.roll(dd, ts + SUBLANE - sh, 0)[:ts]
            dw_ref[k:k + 1, :] += jnp.sum(d * pltpu.roll(xx, sh, 0)[SUBLANE:], axis=0, keepdims=True)
        dx_ref[...] = dx.astype(dx_ref.dtype)
        db_ref[...] += jnp.sum(d, axis=0, keepdims=True)

    d_spec = pl.BlockSpec((ts, cb), lambda j, i: (i, j))
    dn_spec = pl.BlockSpec((SUBLANE, cb), lambda j, i: (jnp.minimum((i + 1) * (ts // SUBLANE), s_len // SUBLANE - 1), j))
    in_specs = [pl.BlockSpec((ts, cb), lambda j, i: (i, xo + j)),
                pl.BlockSpec((SUBLANE, cb), lambda j, i: (jnp.maximum(i * (ts // SUBLANE) - 1, 0), xo + j)),
                pl.BlockSpec((taps, cb), lambda j, i: (0, j)), d_spec, dn_spec]
    args = [xa, xa, w, dout, dout]
    if two:
        in_specs += [d_spec, dn_spec]
        args += [dout2, dout2]
    return pl.pallas_call(
        kern, name=name, grid=(width // cb, n_i), in_specs=in_specs,
        out_specs=[pl.BlockSpec((ts, cb), lambda j, i: (i, j)), pl.BlockSpec((taps, cb), lambda j, i: (0, j)),
                   pl.BlockSpec((1, cb), lambda j, i: (0, j))],
        out_shape=[jax.ShapeDtypeStruct((s_len, width), dx_dtype), jax.ShapeDtypeStruct((taps, width), f32),
                   jax.ShapeDtypeStruct((1, width), f32)],
        compiler_params=pltpu.CompilerParams(dimension_semantics=("parallel", "arbitrary")))(*args)


def _conv_rows(xx, w_ref, b_ref, taps):
    out = b_ref[...] + w_ref[taps - 1:taps, :] * xx[SUBLANE:]
    for k in range(taps - 1):
        out = out + w_ref[k:k + 1, :] * pltpu.roll(xx, taps - 1 - k, 0)[SUBLANE:]
    return out


def _ffn_act_fwd(name, up, w, b):
    s_len = up.shape[0]
    ts, cb = min(CONV_TS, s_len), CONV_CB
    nf = D_FF // cb

    def kern(g_ref, gp_ref, v_ref, vp_ref, wg_ref, wv_ref, bg_ref, bv_ref, o_ref):
        first = pl.program_id(1) == 0
        ug = _conv_rows(jnp.concatenate([jnp.where(first, 0.0, gp_ref[...]), g_ref[...]], axis=0), wg_ref, bg_ref, FFN_CONV)
        uv = _conv_rows(jnp.concatenate([jnp.where(first, 0.0, vp_ref[...]), v_ref[...]], axis=0), wv_ref, bv_ref, FFN_CONV)
        o_ref[...] = (jax.nn.silu(ug) * uv).astype(o_ref.dtype)

    def half(off):
        return [pl.BlockSpec((ts, cb), lambda j, i: (i, off + j)),
                pl.BlockSpec((SUBLANE, cb), lambda j, i: (jnp.maximum(i * (ts // SUBLANE) - 1, 0), off + j))]

    def par(rows, off):
        return pl.BlockSpec((rows, cb), lambda j, i: (0, off + j))

    return pl.pallas_call(
        kern, name=name, grid=(nf, s_len // ts),
        in_specs=half(0) + half(nf) + [par(FFN_CONV, 0), par(FFN_CONV, nf), par(1, 0), par(1, nf)],
        out_specs=pl.BlockSpec((ts, cb), lambda j, i: (i, j)),
        out_shape=jax.ShapeDtypeStruct((s_len, D_FF), bf16),
        compiler_params=pltpu.CompilerParams(dimension_semantics=("parallel", "parallel")))(up, up, up, up, w, w, b, b)


def _ffn_act_bwd(name, up, dact, w, b):
    s_len = up.shape[0]
    ts, cb = min(CONV_TS, s_len), CONV_CB
    nf = D_FF // cb
    n_i = s_len // ts
    taps = FFN_CONV

    ch = min(FFN_ROWS, ts)

    def kern(g_ref, gp_ref, gn_ref, v_ref, vp_ref, vn_ref, d_ref, dn_ref, wg_ref, wv_ref, bg_ref, bv_ref,
             dg_ref, dv_ref, dwg_ref, dwv_ref, dbg_ref, dbv_ref, gx_ref, vx_ref, dd_ref):
        i = pl.program_id(1)
        first, last = i == 0, i == n_i - 1
        for x_ref, p_ref, n_ref, ext in ((g_ref, gp_ref, gn_ref, gx_ref), (v_ref, vp_ref, vn_ref, vx_ref)):
            ext[:SUBLANE, :] = jnp.where(first, 0.0, p_ref[...])
            ext[SUBLANE:SUBLANE + ts, :] = x_ref[...]
            ext[SUBLANE + ts:, :] = jnp.where(last, 0.0, n_ref[...])
        dd_ref[:ts, :] = d_ref[...]
        dd_ref[ts:, :] = jnp.where(last, 0.0, dn_ref[...])

        @pl.when(first)
        def _():
            for ref in (dwg_ref, dwv_ref, dbg_ref, dbv_ref):
                ref[...] = jnp.zeros_like(ref)

        def rows_of(c, carry):
            r0 = pl.multiple_of(c * ch, ch)
            gx, vx = gx_ref[pl.ds(r0, ch + 2 * SUBLANE), :], vx_ref[pl.ds(r0, ch + 2 * SUBLANE), :]
            ug, uv = _conv_rows(gx, wg_ref, bg_ref, taps), _conv_rows(vx, wv_ref, bv_ref, taps)
            dd = dd_ref[pl.ds(r0, ch + SUBLANE), :]
            sg = jax.nn.sigmoid(ug)
            out = []
            for du, xx, w_ref, dx_ref, sums in ((dd * uv * (sg * (1.0 + ug * (1.0 - sg))), gx, wg_ref, dg_ref, carry[0]),
                                                (dd * (ug * sg), vx, wv_ref, dv_ref, carry[1])):
                d = du[:ch]
                dx = w_ref[taps - 1:taps, :] * d
                new = [None] * (taps + 1)
                new[taps - 1] = sums[taps - 1] + jnp.sum(d * xx[SUBLANE:SUBLANE + ch], axis=0, keepdims=True)
                for k in range(taps - 1):
                    sh = taps - 1 - k
                    dx = dx + w_ref[k:k + 1, :] * pltpu.roll(du, ch + SUBLANE - sh, 0)[:ch]
                    new[k] = sums[k] + jnp.sum(d * pltpu.roll(xx, sh, 0)[SUBLANE:SUBLANE + ch], axis=0, keepdims=True)
                new[taps] = sums[taps] + jnp.sum(d, axis=0, keepdims=True)
                dx_ref[pl.ds(r0, ch), :] = dx.astype(dx_ref.dtype)
                out.append(tuple(new))
            return tuple(out)

        zero = tuple(jnp.zeros((1, cb), f32) for _ in range(taps + 1))
        sums_g, sums_v = lax.fori_loop(0, ts // ch, rows_of, (zero, zero))
        for sums, dw_ref, db_ref in ((sums_g, dwg_ref, dbg_ref), (sums_v, dwv_ref, dbv_ref)):
            for k in range(taps):
                dw_ref[k:k + 1, :] += sums[k]
            db_ref[...] += sums[taps]

    blocks = s_len // SUBLANE

    def half(off):
        return [pl.BlockSpec((ts, cb), lambda j, i: (i, off + j)),
                pl.BlockSpec((SUBLANE, cb), lambda j, i: (jnp.maximum(i * (ts // SUBLANE) - 1, 0), off + j)),
                pl.BlockSpec((SUBLANE, cb), lambda j, i: (jnp.minimum((i + 1) * (ts // SUBLANE), blocks - 1), off + j))]

    def par(rows, off):
        return pl.BlockSpec((rows, cb), lambda j, i: (0, off + j))

    d_specs = [pl.BlockSpec((ts, cb), lambda j, i: (i, j)),
               pl.BlockSpec((SUBLANE, cb), lambda j, i: (jnp.minimum((i + 1) * (ts // SUBLANE), blocks - 1), j))]
    out_par = [pl.BlockSpec((r, cb), lambda j, i: (0, j)) for r in (taps, taps, 1, 1)]
    return pl.pallas_call(
        kern, name=name, grid=(nf, n_i),
        in_specs=half(0) + half(nf) + d_specs + [par(taps, 0), par(taps, nf), par(1, 0), par(1, nf)],
        out_specs=[pl.BlockSpec((ts, cb), lambda j, i: (i, j))] * 2 + out_par,
        out_shape=[jax.ShapeDtypeStruct((s_len, D_FF), bf16)] * 2 + [jax.ShapeDtypeStruct((taps, D_FF), f32)] * 2
        + [jax.ShapeDtypeStruct((1, D_FF), f32)] * 2,
        scratch_shapes=[pltpu.VMEM((ts + 2 * SUBLANE, cb), f32)] * 2 + [pltpu.VMEM((ts + SUBLANE, cb), f32)],
        compiler_params=pltpu.CompilerParams(dimension_semantics=("parallel", "arbitrary")))(
        up, up, up, up, up, up, dact, dact, w, w, b, b)


SCAN_ROWS = 128


def _block_scan(a, b, reverse):
    t = a.shape[0]
    row = lax.broadcasted_iota(jnp.int32, a.shape, 0)
    d = 1
    while d < t:
        keep = row < t - d if reverse else row >= d
        shift = t - d if reverse else d
        a_far = jnp.where(keep, pltpu.roll(a, shift, 0), 1.0)
        b_far = jnp.where(keep, pltpu.roll(b, shift, 0), 0.0)
        b = a * b_far + b
        a = a * a_far
        d *= 2
    return a, b


def _scan_fwd(name, a, b):
    s_len, width = a.shape
    t = min(SCAN_ROWS, s_len)

    def kern(a_ref, b_ref, h_ref):
        def block(k, carry):
            rows = pl.ds(pl.multiple_of(k * t, t), t)
            acc, h = _block_scan(a_ref[rows, :], b_ref[rows, :], False)
            h_ref[rows, :] = h + acc * carry
            return h_ref[pl.ds(k * t + t - 1, 1), :]

        lax.fori_loop(0, s_len // t, block, jnp.zeros((1, LANE), f32))

    spec = pl.BlockSpec((s_len, LANE), lambda j: (0, j))
    return pl.pallas_call(
        kern, name=name, grid=(width // LANE,), in_specs=[spec, spec], out_specs=spec,
        out_shape=jax.ShapeDtypeStruct((s_len, width), f32),
        compiler_params=pltpu.CompilerParams(dimension_semantics=("parallel",)))(a, b)


def _scan_bwd(name, a_next, h_prev, dh):
    s_len, width = dh.shape
    t = min(SCAN_ROWS, s_len)
    n_blocks = s_len // t

    def kern(an_ref, hp_ref, dh_ref, da_ref, db_ref):
        def block(kk, carry):
            k = n_blocks - 1 - kk
            rows = pl.ds(pl.multiple_of(k * t, t), t)
            acc, g = _block_scan(an_ref[rows, :], dh_ref[rows, :], True)
            g = g + acc * carry
            db_ref[rows, :] = g
            da_ref[rows, :] = g * hp_ref[rows, :]
            return db_ref[pl.ds(k * t, 1), :]

        lax.fori_loop(0, n_blocks, block, jnp.zeros((1, LANE), f32))

    spec = pl.BlockSpec((s_len, LANE), lambda j: (0, j))
    return pl.pallas_call(
        kern, name=name, grid=(width // LANE,), in_specs=[spec, spec, spec], out_specs=[spec, spec],
        out_shape=[jax.ShapeDtypeStruct((s_len, width), f32)] * 2,
        compiler_params=pltpu.CompilerParams(dimension_semantics=("parallel",)))(a_next, h_prev, dh)


def _lane_cumsum(x, reverse):
    n = x.shape[1]
    lane = lax.broadcasted_iota(jnp.int32, x.shape, 1)
    sh = 1
    while sh < n:
        if reverse:
            x = x + jnp.where(lane < n - sh, pltpu.roll(x, n - sh, 1), 0.0)
        else:
            x = x + jnp.where(lane >= sh, pltpu.roll(x, sh, 1), 0.0)
        sh *= 2
    return x


def _decay_fwd(name, fl_t, b8):
    def kern(f_ref, b_ref, c_ref):
        c_ref[...] = _lane_cumsum(jax.nn.log_sigmoid(f_ref[...] + b_ref[...]), False)

    return pl.pallas_call(kern, name=name, out_shape=jax.ShapeDtypeStruct(fl_t.shape, f32))(fl_t, b8)


def _decay_bwd(name, fl_t, b8, dc_key, dc_query):
    def kern(f_ref, b_ref, dck_ref, dcq_ref, df_ref, db_ref):
        dlogf = _lane_cumsum(dck_ref[...] + dcq_ref[...], True)
        df = dlogf * jax.nn.sigmoid(-(f_ref[...] + b_ref[...]))
        df_ref[...] = df
        db_ref[...] = jnp.sum(df, axis=1, keepdims=True)

    return pl.pallas_call(kern, name=name, out_shape=[jax.ShapeDtypeStruct(fl_t.shape, f32),
                                                      jax.ShapeDtypeStruct((SUBLANE, 1), f32)])(fl_t, b8, dc_key, dc_query)


def _rms(x, g, n):
    return x * lax.rsqrt(jnp.sum(x * x, axis=-1, keepdims=True) * (1.0 / n) + EPS) * g


def _loss_head(name, h, target, g, tb=256):
    n, d = h.shape
    tb = min(tb, n)

    def kern(h_ref, t_ref, g_ref, loss_ref, dh_ref, dg_ref):
        i = pl.program_id(0)
        tgt = t_ref[...]

        def f(hv, gv):
            err = _rms(hv, gv, d) - tgt
            return 0.5 * jnp.sum(jnp.sum(err * err, axis=-1, keepdims=True) * (1.0 / d), axis=0, keepdims=True)

        val, vjp = jax.vjp(f, h_ref[...], g_ref[...])
        dh, dg = vjp(jnp.ones((1, 1), f32))
        dh_ref[...] = dh

        @pl.when(i == 0)
        def _():
            loss_ref[...] = jnp.zeros_like(loss_ref)
            dg_ref[...] = jnp.zeros_like(dg_ref)

        loss_ref[...] += val
        dg_ref[...] += dg

    return pl.pallas_call(
        kern, name=name, grid=(n // tb,),
        in_specs=[pl.BlockSpec((tb, d), lambda i: (i, 0)), pl.BlockSpec((tb, d), lambda i: (i, 0)),
                  pl.BlockSpec((1, d), lambda i: (0, 0))],
        out_specs=[pl.BlockSpec((1, 1), lambda i: (0, 0)), pl.BlockSpec((tb, d), lambda i: (i, 0)),
                   pl.BlockSpec((1, d), lambda i: (0, 0))],
        out_shape=[jax.ShapeDtypeStruct((1, 1), f32), jax.ShapeDtypeStruct((n, d), f32), jax.ShapeDtypeStruct((1, d), f32)],
        compiler_params=pltpu.CompilerParams(dimension_semantics=("arbitrary",)))(h, target, g)


def _f_norm(x, g):
    return (_rms(x, g, D_MODEL),)


def _f_latent(qc, kvc, gq, gkv):
    return _rms(qc, gq, MLA_Q_RANK), _rms(kvc, gkv, MLA_KV_RANK)


def _f_rope_table(pos, freq, m1, m2):
    ang = pos * freq
    sin = jnp.sin(ang)
    return jnp.cos(ang), -sin * m1, sin * m2


def _rope(x, cos, s_up, s_down):
    w = x.shape[1]
    return x * cos + _roll(x, w - MLA_ROPE // 2, 1) * s_up + _roll(x, MLA_ROPE // 2, 1) * s_down


def _f_mla_prep(q, kpart, kr, cos, s_up, s_down):
    def heads(t):
        return jnp.concatenate([t] * HEADS, axis=1)

    kr = _rope(kr, cos, s_up, s_down)
    return _rope(q, heads(cos), heads(s_up), heads(s_down)), kpart + heads(kr)


def _f_lru_gate(gates, xc, b_r, b_i, lam):
    r = jax.nn.sigmoid(gates[:, :LRU_WIDTH] + b_r)
    i = jax.nn.sigmoid(gates[:, LRU_WIDTH:] + b_i)
    log_a = -LRU_C * r * jax.nn.softplus(-lam)
    mult = jnp.sqrt(-jnp.tanh(log_a) * (1.0 + jnp.exp(2.0 * log_a)))
    return jnp.exp(log_a), mult * (i * xc)


def _f_merge(o_mla, o_fox, hs, lg, g):
    o_lru = hs * jax.nn.gelu(lg)
    return (jnp.concatenate([_rms(o_mla, g[:, :512], HEADS * MLA_V), _rms(o_fox, g[:, 512:1024], HEADS * FOX_HEAD_DIM),
                             _rms(o_lru, g[:, 1024:], LRU_WIDTH)], axis=1),)


def _f_ffn_gate(u):
    return (jax.nn.silu(u[:, :D_FF]) * u[:, D_FF:],)


def _f_ple(h, gpre, pp):
    return (h + jax.nn.sigmoid(gpre) * pp,)


MIX_PART = ["w_in", "w_uq", "w_ukv", "lru_conv_w"]
FFN_PART = ["w_o", "w_up", "ffn_conv_w", "w_down", "w_ple_gate", "w_ple_proj"]


def _prep_mix_weights(w):
    eye = jnp.eye(LRU_BLOCKS, dtype=f32)

    def block_diag(m):
        return (eye[:, None, :, None] * m[:, :, None, :]).reshape(LRU_WIDTH, LRU_WIDTH)

    return dict(
        w_in=_take_pad(w["w_in"], Z_MAP, 1),
        w_uq=_take_pad(_take_pad(w["w_uq"], UQ_COL_MAP, 1), UQ_ROW_MAP, 0),
        w_ukv=_take_pad(w["w_ukv"], UKV_MAP, 1),
        w_ri=jnp.concatenate([block_diag(w["w_r"]), block_diag(w["w_i"])], axis=1).astype(bf16),
        g_mix=w["g_mix"].reshape(1, -1), g_ffn=w["g_ffn"].reshape(1, -1), g_ple=w["g_ple"].reshape(1, -1),
        g_qc=_take_pad(w["g_qc"], UQ_ROW_MAP, 0).reshape(1, -1), g_kvc=w["g_kvc"].reshape(1, -1),
        g_out=_take_pad(w["g_out"], OMIX_MAP, 0).reshape(1, -1),
        b_f8=_take_pad(w["b_f"], _pad_to(np.arange(FOX_HEADS), SUBLANE), 0).reshape(SUBLANE, 1),
        lru_conv_w=w["lru_conv_w"], lru_conv_b=w["lru_conv_b"].reshape(1, -1),
        b_r=w["b_r"].reshape(1, -1), b_i=w["b_i"].reshape(1, -1), lam=w["lru_lambda"].reshape(1, -1),
        ffn_conv_b=w["ffn_conv_b"].reshape(1, -1),
    )


def _prep_ffn_weights(w):
    return dict(w_o=_take_pad(w["w_o"], OMIX_MAP, 0),
                w_up=w["w_up"], w_up_g=w["w_up"][:, :D_FF], w_up_v=w["w_up"][:, D_FF:], ffn_conv_w=w["ffn_conv_w"],
                w_down=w["w_down"], w_ple_gate=w["w_ple_gate"], w_ple_proj=w["w_ple_proj"])


def _rope_rows(pos):
    consts = [jnp.asarray(t) for t in _rope_tables(LANE, ROPE_AT)]
    return _rowwise("rope_table", _f_rope_table, [pos], consts, [(LANE, f32)] * 3)


def _key_decay(c_t, s_len):
    t = _att_tiles(s_len)[1]
    return c_t[:HEADS].reshape(HEADS, s_len // t, 1, t), c_t[:HEADS].reshape(HEADS, s_len, 1)


def _layer_fwd(l, h0, p_l, rope, weights_of):
    s_len = h0.shape[0]
    n = f"l{l}_"
    w = _prep_mix_weights(weights_of("mix", h0))
    xn, = _rowwise(n + "norm_mix", _f_norm, [h0], [w["g_mix"]], [(D_MODEL, bf16)])
    z = _mm(n + "in_proj", xn, w["w_in"])
    zq = (z, QC_W, Z_QC // QC_W)
    zkv = (z, LANE, Z_KVC // LANE)
    zkr = (z, LANE, Z_KR // LANE)
    zlx = (z, LRU_WIDTH, Z_LX // LRU_WIDTH)
    zlg = (z, LRU_WIDTH, Z_LG // LRU_WIDTH)
    qcn, kvn = _rowwise(n + "latent_norm", _f_latent, [zq, zkv], [w["g_qc"], w["g_kvc"]], [(QC_W, bf16), (LANE, bf16)])
    q = _mm(n + "uq", qcn, w["w_uq"])
    kv = _mm(n + "ukv", kvn, w["w_ukv"])
    kpart = (kv, HEADS * LANE, 0)
    qr, kk = _rowwise(n + "mla_prep", _f_mla_prep, [q, kpart, zkr, *rope], [],
                      [(HEADS * LANE, bf16), (HEADS * LANE, bf16)])
    mla_scale = (MLA_NOPE + MLA_ROPE) ** -0.5
    o_mla, lse_m = _attn_fwd(n + "mla_fwd", (qr, 0), (kk, 0), (kv, HEADS), mla_scale)
    fl_t = z[:, Z_FL:Z_FL + SUBLANE].T
    c_t = _decay_fwd(n + "decay", fl_t, w["b_f8"])
    c_row, c_col = _key_decay(c_t, s_len)
    fox_scale = FOX_HEAD_DIM ** -0.5
    o_fox, lse_f = _attn_fwd(n + "fox_fwd", (z, Z_FQ // LANE), (z, Z_FK // LANE), (z, Z_FV // LANE), fox_scale, c_row)
    xc = _conv_fwd(n + "lru_conv", zlx, w["lru_conv_w"], w["lru_conv_b"], LRU_CONV)
    gates = _mm(n + "lru_gates", xc, w["w_ri"])
    a, bx = _rowwise(n + "lru_gate", _f_lru_gate, [gates, xc], [w["b_r"], w["b_i"], w["lam"]],
                     [(LRU_WIDTH, f32), (LRU_WIDTH, f32)])
    hs = _scan_fwd(n + "lru_scan", a, bx)
    ocat, = _rowwise(n + "merge", _f_merge, [o_mla, o_fox, hs, zlg], [w["g_out"]], [(OMIX_W, bf16)])
    w.update(_prep_ffn_weights(weights_of("ffn", ocat)))
    h1 = _mm(n + "out_proj", ocat, w["w_o"], res=h0)
    xn2, = _rowwise(n + "norm_ffn", _f_norm, [h1], [w["g_ffn"]], [(D_MODEL, bf16)])
    up = _mm(n + "up_proj", xn2, w["w_up"])
    act = _ffn_act_fwd(n + "ffn_act", up, w["ffn_conv_w"], w["ffn_conv_b"])
    h2 = _mm(n + "down_proj", act, w["w_down"], res=h1)
    hn, = _rowwise(n + "norm_ple", _f_norm, [h2], [w["g_ple"]], [(D_MODEL, bf16)])
    gpre = _mm(n + "ple_gate", hn, w["w_ple_gate"])
    pp = _mm(n + "ple_proj", p_l, w["w_ple_proj"])
    h3, = _rowwise(n + "ple_mix", _f_ple, [h2, gpre, pp], [], [(D_MODEL, f32)])
    res = dict(h0=h0, xn=xn, z=z, qcn=qcn, kvn=kvn, q=q, kv=kv, qr=qr, kk=kk, o_mla=o_mla, lse_m=lse_m, fl_t=fl_t,
               c_row=c_row, c_col=c_col, o_fox=o_fox, lse_f=lse_f, xc=xc, gates=gates, a=a, hs=hs, ocat=ocat, h1=h1,
               xn2=xn2, up=up, act=act, h2=h2, hn=hn, gpre=gpre, pp=pp, p_l=p_l)
    return h3, res, w


def _layer_bwd(l, dh3, r, rope, w, token, grads_to):
    s_len = dh3.shape[0]
    n = f"l{l}_"
    g = {}
    w = dict(w, g_ple=w["g_ple"] + token)
    z = r["z"]
    zq = (z, QC_W, Z_QC // QC_W)
    zkv = (z, LANE, Z_KVC // LANE)
    zkr = (z, LANE, Z_KR // LANE)
    zlx = (z, LRU_WIDTH, Z_LX // LRU_WIDTH)
    zlg = (z, LRU_WIDTH, Z_LG // LRU_WIDTH)
    (dh2a, dgpre, dpp), _ = _rowwise_bwd(n + "ple_mix_b", _f_ple, [r["h2"], r["gpre"], r["pp"]], [], [dh3], 3,
                                         dts=[f32, bf16, bf16])
    g["w_ple_proj"] = _mm(n + "ple_proj_dw", r["p_l"], dpp, "tn", bf16)
    dhn = _mm(n + "ple_gate_dx", dgpre, w["w_ple_gate"], "nt")
    g["w_ple_gate"] = _mm(n + "ple_gate_dw", r["hn"], dgpre, "tn", bf16)
    (dh2,), (g["g_ple"],) = _rowwise_bwd(n + "norm_ple_b", _f_norm, [r["h2"]], [w["g_ple"]], [dhn], 1, adds={0: dh2a})
    dact = _mm(n + "down_dx", dh2, w["w_down"], "nt")
    g["w_down"] = _mm(n + "down_dw", r["act"], dh2, "tn", bf16)
    dup_g, dup_v, dcw_g, dcw_v, dcb_g, dcb_v = _ffn_act_bwd(n + "ffn_act_b", r["up"], dact, w["ffn_conv_w"], w["ffn_conv_b"])
    g["ffn_conv_w"] = jnp.concatenate([dcw_g, dcw_v], axis=1)
    g["ffn_conv_b"] = jnp.concatenate([dcb_g, dcb_v], axis=1)
    dxn2 = _mm(n + "up_dx_v", dup_v, w["w_up_v"], "nt", res=_mm(n + "up_dx_g", dup_g, w["w_up_g"], "nt"))
    g["w_up"] = jnp.concatenate([_mm(n + "up_dw_g", r["xn2"], dup_g, "tn", bf16),
                                 _mm(n + "up_dw_v", r["xn2"], dup_v, "tn", bf16)], axis=1)
    (dh1,), (g["g_ffn"],) = _rowwise_bwd(n + "norm_ffn_b", _f_norm, [r["h1"]], [w["g_ffn"]], [dxn2], 1, adds={0: dh2})
    docat = _mm(n + "out_dx", dh1, w["w_o"], "nt")
    g["w_o"] = _mm(n + "out_dw", r["ocat"], dh1, "tn", bf16)
    token = grads_to("ffn", dict(w_o=_take_inv(g["w_o"], OMIX_MAP, 0), w_up=g["w_up"], ffn_conv_w=g["ffn_conv_w"],
                                 w_down=g["w_down"], w_ple_gate=g["w_ple_gate"], w_ple_proj=g["w_ple_proj"]))
    w = dict(w, g_out=w["g_out"] + token)
    (do_mla, do_fox, dhs, dlg), (g["g_out"],) = _rowwise_bwd(
        n + "merge_b", _f_merge, [r["o_mla"], r["o_fox"], r["hs"], zlg], [w["g_out"]], [docat], 4)
    a, hs = r["a"], r["hs"]
    a_next = jnp.concatenate([a[1:], jnp.zeros((1, LRU_WIDTH), f32)], axis=0)
    h_prev = jnp.concatenate([jnp.zeros((1, LRU_WIDTH), f32), hs[:-1]], axis=0)
    da, dbx = _scan_bwd(n + "lru_scan_b", a_next, h_prev, dhs)
    (dgates, dxc_a), (g["b_r"], g["b_i"], g["lam"]) = _rowwise_bwd(
        n + "lru_gate_b", _f_lru_gate, [r["gates"], r["xc"]], [w["b_r"], w["b_i"], w["lam"]], [da, dbx], 2,
        dts=[bf16, f32])
    dxc_b = _mm(n + "lru_gates_dx", dgates, w["w_ri"], "nt")
    g["w_ri"] = _mm(n + "lru_gates_dw", r["xc"], dgates, "tn")
    dlx, g["lru_conv_w"], g["lru_conv_b"] = _conv_bwd(n + "lru_conv_b", zlx, dxc_a, w["lru_conv_w"], LRU_CONV, dout2=dxc_b)
    fox_scale = FOX_HEAD_DIM ** -0.5
    fq, fk, fv = (z, Z_FQ // LANE), (z, Z_FK // LANE), (z, Z_FV // LANE)
    dfq, delta_f, dc_q = _attn_dq(n + "fox_dq", fq, fk, fv, r["o_fox"], do_fox, r["lse_f"], fox_scale, r["c_row"])
    dfk, dfv, dc_k = _attn_dkv(n + "fox_dkv", fq, fk, fv, do_fox, _query_rows(r["lse_f"]), _query_rows(delta_f), fox_scale,
                               r["c_col"])
    pad_rows = jnp.zeros((SUBLANE - HEADS, s_len), f32)
    dfl_t, g["b_f8"] = _decay_bwd(n + "decay_b", r["fl_t"], w["b_f8"],
                                  jnp.concatenate([dc_k.reshape(HEADS, s_len), pad_rows], axis=0),
                                  jnp.concatenate([dc_q.reshape(HEADS, s_len), pad_rows], axis=0))
    dfl = jnp.pad(dfl_t.T, ((0, 0), (0, LANE - SUBLANE)))
    mla_scale = (MLA_NOPE + MLA_ROPE) ** -0.5
    qr, kk, kv = (r["qr"], 0), (r["kk"], 0), (r["kv"], HEADS)
    dqr, delta_m, _ = _attn_dq(n + "mla_dq", qr, kk, kv, r["o_mla"], do_mla, r["lse_m"], mla_scale)
    dkk, dv_m = _attn_dkv(n + "mla_dkv", qr, kk, kv, do_mla, _query_rows(r["lse_m"]), _query_rows(delta_m), mla_scale)
    (dq, dkpart, dkr), _ = _rowwise_bwd(n + "mla_prep_b", _f_mla_prep, [r["q"], (r["kv"], HEADS * LANE, 0), zkr, *rope],
                                        [], [dqr, dkk], 3, dts=[bf16, bf16, f32])
    dkv = jnp.concatenate([dkpart, dv_m.astype(bf16)], axis=1)
    dkvn = _mm(n + "ukv_dx", dkv, w["w_ukv"], "nt")
    g["w_ukv"] = _mm(n + "ukv_dw", r["kvn"], dkv, "tn", bf16)
    dqcn = _mm(n + "uq_dx", dq, w["w_uq"], "nt")
    g["w_uq"] = _mm(n + "uq_dw", r["qcn"], dq, "tn", bf16)
    (dqc, dkvc), (g["g_qc"], g["g_kvc"]) = _rowwise_bwd(n + "latent_norm_b", _f_latent, [zq, zkv],
                                                        [w["g_qc"], w["g_kvc"]], [dqcn, dkvn], 2)
    dz = jnp.concatenate([t.astype(bf16) for t in (dfq, dfk, dfv, dlx, dlg, dqc, dkvc, dkr, dfl)], axis=1)
    dxn = _mm(n + "in_dx", dz, w["w_in"], "nt")
    g["w_in"] = _mm(n + "in_dw", r["xn"], dz, "tn", bf16)
    (dh0,), (g["g_mix"],) = _rowwise_bwd(n + "norm_mix_b", _f_norm, [r["h0"]], [w["g_mix"]], [dxn], 1, adds={0: dh1})
    return dh0, grads_to("mix", _unpad_mix_grads(g))


def _unpad_mix_grads(g):
    d_ri = g["w_ri"]
    idx = jnp.arange(LRU_BLOCKS)

    def diag_blocks(m):
        return m.reshape(LRU_BLOCKS, LRU_BLOCK, LRU_BLOCKS, LRU_BLOCK)[idx, :, idx, :]

    return dict(
        g_mix=g["g_mix"][0], w_in=_take_inv(g["w_in"], Z_MAP, 1), g_qc=g["g_qc"][0, :MLA_Q_RANK],
        w_uq=_take_inv(g["w_uq"][:MLA_Q_RANK], UQ_COL_MAP, 1), g_kvc=g["g_kvc"][0],
        w_ukv=_take_inv(g["w_ukv"], UKV_MAP, 1), b_f=g["b_f8"][:FOX_HEADS, 0],
        lru_conv_w=g["lru_conv_w"], lru_conv_b=g["lru_conv_b"][0],
        w_r=diag_blocks(d_ri[:, :LRU_WIDTH]), b_r=g["b_r"][0], w_i=diag_blocks(d_ri[:, LRU_WIDTH:]), b_i=g["b_i"][0],
        lru_lambda=g["lam"][0], g_out=_take_inv(g["g_out"][0], OMIX_MAP, 0),
        g_ffn=g["g_ffn"][0], ffn_conv_b=g["ffn_conv_b"][0], g_ple=g["g_ple"][0],
    )


LAYER_WEIGHTS = ["g_mix", "w_in", "g_qc", "w_uq", "g_kvc", "w_ukv", "b_f", "lru_conv_w", "lru_conv_b", "w_r", "b_r", "w_i",
                 "b_i", "lru_lambda", "g_out", "w_o", "g_ffn", "w_up", "ffn_conv_w", "ffn_conv_b", "w_down", "g_ple",
                 "w_ple_gate", "w_ple_proj"]
WEIGHTS = LAYER_WEIGHTS + ["g_final"]


def _local_step(x, p, pos, target, g_final, weights_of, grads_to):
    h = x
    rope = _rope_rows(pos)
    ws, saved = [], []
    for l in range(DEPTH):
        h, r, w = _layer_fwd(l, h, p[l], rope, functools.partial(weights_of, l))
        ws.append(w)
        saved.append(r)
    loss, dh, dg_final = _loss_head("loss_head", h, target, g_final.reshape(1, -1))
    token = jnp.zeros((), f32)
    for l in reversed(range(DEPTH)):
        dh, token = _layer_bwd(l, dh, saved[l], rope, ws[l], token, functools.partial(grads_to, l))
    return loss[0, 0], dh, dg_final[0]


MESH_AXES = ("x", "y", "c")


def _row_tile(rows, cap):
    if rows <= cap:
        return rows
    for t in range(cap, SUBLANE - 1, -SUBLANE):
        if rows % t == 0:
            return t
    return rows


ADAM_BLOCK_BYTES = 2 ** 20


def _adamw(name, w, g, m, v):
    rows, cols = w.shape
    tr = _row_tile(rows, max(SUBLANE, ADAM_BLOCK_BYTES // (4 * cols) // SUBLANE * SUBLANE))

    def kern(w_ref, g_ref, m_ref, v_ref, d_ref, nm_ref, nv_ref):
        gv = g_ref[...]
        nm = ADAM_B1 * m_ref[...] + (1.0 - ADAM_B1) * gv
        nv = ADAM_B2 * v_ref[...] + (1.0 - ADAM_B2) * (gv * gv)
        m_hat = nm / (1.0 - ADAM_B1 ** ADAM_STEP)
        v_hat = nv / (1.0 - ADAM_B2 ** ADAM_STEP)
        d_ref[...] = -ADAM_LR * (m_hat / (jnp.sqrt(v_hat) + ADAM_EPS) + ADAM_WD * w_ref[...])
        nm_ref[...] = nm
        nv_ref[...] = nv

    spec = pl.BlockSpec((tr, cols), lambda i: (i, 0))
    return pl.pallas_call(
        kern, name=name, grid=(rows // tr,), in_specs=[spec] * 4, out_specs=[spec] * 3,
        out_shape=[jax.ShapeDtypeStruct((rows, cols), f32)] * 3,
        compiler_params=pltpu.CompilerParams(dimension_semantics=("parallel",)))(w, g, m, v)


def _packed_rows(shape):
    return -(-int(np.prod(shape)) // (SUBLANE * LANE)) * SUBLANE


def _pack(arrays):
    rows = []
    for a in arrays:
        flat = a.reshape(-1)
        rows.append(jnp.pad(flat, (0, _packed_rows(a.shape) * LANE - flat.shape[0])).reshape(-1, LANE))
    return jnp.concatenate(rows, axis=0)


def _unpack(buf, shapes):
    out, at = [], 0
    for s in shapes:
        rows = _packed_rows(s)
        out.append(buf[at:at + rows].reshape(-1)[:int(np.prod(s))].reshape(s))
        at += rows
    return out


SHARD_AXIS = {"w_in": 2, "w_uq": 2, "w_ukv": 2, "lru_conv_w": 2, "w_o": 1, "w_up": 2, "ffn_conv_w": 2, "w_down": 1,
              "w_ple_gate": 1, "w_ple_proj": 2}
SHARDED = [k for k in WEIGHTS if k in SHARD_AXIS]
REPLICATED = [k for k in WEIGHTS if k not in SHARD_AXIS]
ELEMENTWISE_F32 = ("lru_conv_w", "ffn_conv_w")
N_SHARDS = 4
BF16_TILE_ROWS = 16


HBM_SPEC = pl.BlockSpec(memory_space=pl.ANY)
SEM_SPEC = pl.BlockSpec(memory_space=pltpu.SEMAPHORE)
SPLIT_EFFECT = pltpu.SideEffectType.DATAFLOW_SIDE_EFFECTING
CHIP_FLIPS = ((1, 0), (0, 1), (1, 1))
N_DEVICES = 8
SUM_BLOCK_BYTES = 4 * 2 ** 20


def _device_index():
    return 4 * lax.axis_index("x") + 2 * lax.axis_index("y") + lax.axis_index("c")


def _when(cond, fn):
    if cond is None:
        fn()
    else:
        pl.when(cond)(fn)


class _Exchange:
    def __init__(self, name, plan, srcs, land_shapes, n_send, n_recv):
        self.name, self.plan, self.srcs, self.n = name, plan, list(srcs), len(srcs)
        self.land_shapes, self.n_send, self.n_recv = land_shapes, n_send, n_recv

    def run(self):
        n = self.n

        def body(*refs):
            sends, arrivals = self.plan(refs[:n], refs[n:2 * n], refs[2 * n], refs[2 * n + 1])
            for cond, cp in sends:
                _when(cond, cp.start)
            for cond, cp in arrivals:
                _when(cond, cp.wait_recv)
            for cond, cp in sends:
                _when(cond, cp.wait_send)

        return pl.pallas_call(
            body, name=self.name, out_shape=self.land_shapes, in_specs=[HBM_SPEC] * n, out_specs=[HBM_SPEC] * n,
            scratch_shapes=[pltpu.SemaphoreType.DMA((self.n_send,)), pltpu.SemaphoreType.DMA((self.n_recv,))])(*self.srcs)

    def start(self, after=None):
        n = self.n
        lands = [lax.empty(s.shape, s.dtype) for s in self.land_shapes]
        extra = [] if after is None else [after]

        def body(*refs):
            ins, lands_in = refs[:n], refs[n:2 * n]
            send_sems, recv_sems, token = refs[2 * n + len(extra)], refs[2 * n + len(extra) + 1], refs[-1]
            sends, _ = self.plan(ins, lands_in, send_sems, recv_sems)
            for cond, cp in sends:
                _when(cond, cp.start)
            token[...] = jnp.zeros_like(token)

        hbm = [pltpu.with_memory_space_constraint(a, pltpu.HBM) for a in self.srcs + lands]
        res = pl.pallas_call(
            body, name=self.name + "_start",
            out_shape=(pltpu.SemaphoreType.DMA((self.n_send,)), pltpu.SemaphoreType.DMA((self.n_recv,)),
                       *[pltpu.HBM(a.shape, a.dtype) for a in hbm], jax.ShapeDtypeStruct((SUBLANE, LANE), f32)),
            in_specs=[HBM_SPEC] * (2 * n + len(extra)),
            out_specs=(SEM_SPEC, SEM_SPEC, *[HBM_SPEC] * (2 * n), pl.BlockSpec(memory_space=pltpu.VMEM)),
            input_output_aliases={i: 2 + i for i in range(2 * n)},
            compiler_params=pltpu.CompilerParams(has_side_effects=SPLIT_EFFECT))(*hbm, *extra)
        self.sems, self.thru, token = res[:2], res[2:2 + 2 * n], res[-1]
        return token[0, 0]

    def finish(self, after):
        n = self.n

        def body(*refs):
            ins, lands_in, send_sems, recv_sems = refs[:n], refs[n:2 * n], refs[2 * n], refs[2 * n + 1]
            sends, arrivals = self.plan(ins, lands_in, send_sems, recv_sems)
            for cond, cp in arrivals:
                _when(cond, cp.wait_recv)
            for cond, cp in sends:
                _when(cond, cp.wait_send)

        res = pl.pallas_call(
            body, name=self.name + "_finish", out_shape=tuple(pltpu.HBM(a.shape, a.dtype) for a in self.thru),
            in_specs=[HBM_SPEC] * (2 * n) + [SEM_SPEC, SEM_SPEC, HBM_SPEC], out_specs=tuple([HBM_SPEC] * (2 * n)),
            input_output_aliases={i: i for i in range(2 * n)},
            compiler_params=pltpu.CompilerParams(has_side_effects=SPLIT_EFFECT))(*self.thru, *self.sems, after)
        return list(res[n:])


def _gather_exchange(name, shards):
    def plan(ins, lands, send_sems, recv_sems):
        x, y, c = (lax.axis_index(a) for a in MESH_AXES)
        copies = []
        for i in range(len(ins)):
            for k, (fx, fy) in enumerate(CHIP_FLIPS):
                peer = (1 - x if fx else x, 1 - y if fy else y, c)
                copies.append((None, pltpu.make_async_remote_copy(
                    src_ref=ins[i], dst_ref=lands[i].at[2 * x + y], send_sem=send_sems.at[3 * i + k],
                    recv_sem=recv_sems.at[3 * i + k], device_id=peer, device_id_type=pl.DeviceIdType.MESH)))
        return copies, copies

    n = len(shards)
    return _Exchange(name, plan, shards, [jax.ShapeDtypeStruct((N_SHARDS,) + s.shape, s.dtype) for s in shards], 3 * n, 3 * n)


def _scatter_exchange(name, layer, chunks):
    def plan(ins, lands, send_sems, recv_sems):
        x, y, c = (lax.axis_index(a) for a in MESH_AXES)
        me = _device_index()
        sends, arrivals = [], []
        for i in range(len(ins)):
            for j in range(N_SHARDS):
                target = (j // 2, j % 2, layer)
                remote = jnp.logical_not((x == target[0]) & (y == target[1]) & (c == layer))
                sends.append((remote, pltpu.make_async_remote_copy(
                    src_ref=ins[i].at[j], dst_ref=lands[i].at[me], send_sem=send_sems.at[N_SHARDS * i + j],
                    recv_sem=recv_sems.at[N_DEVICES * i + me], device_id=target, device_id_type=pl.DeviceIdType.MESH)))
            for s in range(N_DEVICES):
                arrivals.append(((c == layer) & (me != s), pltpu.make_async_remote_copy(
                    src_ref=ins[i].at[0], dst_ref=lands[i].at[s], send_sem=send_sems.at[0],
                    recv_sem=recv_sems.at[N_DEVICES * i + s], device_id=(x, y, c), device_id_type=pl.DeviceIdType.MESH)))
        return sends, arrivals

    n = len(chunks)
    lands = [jax.ShapeDtypeStruct((N_DEVICES,) + ch.shape[1:], ch.dtype) for ch in chunks]
    return _Exchange(name, plan, chunks, lands, N_SHARDS * n, N_DEVICES * n)


def _sum_contributions(name, got, mine):
    _, a, b = got.shape
    ta = _row_tile(a, max(SUBLANE, SUM_BLOCK_BYTES // (N_DEVICES * b * got.dtype.itemsize) // SUBLANE * SUBLANE))

    def kern(got_ref, mine_ref, o_ref):
        me = _device_index()
        acc = jnp.zeros(o_ref.shape, f32)
        for s in range(N_DEVICES):
            acc = acc + jnp.where(me == s, mine_ref[...].astype(f32), got_ref[s].astype(f32))
        o_ref[...] = acc

    return pl.pallas_call(
        kern, name=name, grid=(a // ta,),
        in_specs=[pl.BlockSpec((N_DEVICES, ta, b), lambda i: (0, i, 0)), pl.BlockSpec((ta, b), lambda i: (i, 0))],
        out_specs=pl.BlockSpec((ta, b), lambda i: (i, 0)), out_shape=jax.ShapeDtypeStruct((a, b), f32),
        compiler_params=pltpu.CompilerParams(dimension_semantics=("parallel",)))(got, mine)


def _swap_layers(name, sums):
    n = len(sums[0])

    def body(*refs):
        srcs = (refs[:n], refs[n:2 * n])
        outs, (send_sems, recv_sems) = refs[2 * n:3 * n], refs[3 * n:]
        x, y, c = (lax.axis_index(a) for a in MESH_AXES)
        for i in range(n):
            for layer in range(DEPTH):
                cp = pltpu.make_async_remote_copy(
                    src_ref=srcs[layer][i], dst_ref=outs[i], send_sem=send_sems.at[i], recv_sem=recv_sems.at[i],
                    device_id=(x, y, 1 - c), device_id_type=pl.DeviceIdType.MESH)
                pl.when(c == layer)(cp.start)
        for i in range(n):
            pltpu.make_async_remote_copy(
                src_ref=srcs[0][i], dst_ref=outs[i], send_sem=send_sems.at[i], recv_sem=recv_sems.at[i],
                device_id=(x, y, 1 - c), device_id_type=pl.DeviceIdType.MESH).wait()

    return pl.pallas_call(
        body, name=name, out_shape=[jax.ShapeDtypeStruct(s.shape, s.dtype) for s in sums[0]],
        in_specs=[HBM_SPEC] * (2 * n), out_specs=[HBM_SPEC] * n,
        scratch_shapes=[pltpu.SemaphoreType.DMA((n,)), pltpu.SemaphoreType.DMA((n,))])(*sums[0], *sums[1])


def _stack_shards(g, axis):
    if axis == 1:
        return g.reshape(N_SHARDS, g.shape[0] // N_SHARDS, g.shape[1])
    return g.reshape(g.shape[0], N_SHARDS, g.shape[1] // N_SHARDS).transpose(1, 0, 2)


def _join_shards(s, axis):
    if axis == 1:
        return s.reshape(-1, s.shape[2])
    return s.transpose(1, 0, 2).reshape(s.shape[1], -1)


def _layer_shards(w, l, names):
    return [w[k][l] if k in ELEMENTWISE_F32 else w[k][l].astype(bf16) for k in names]


def _full_weights(names, sent, got):
    j = 2 * lax.axis_index("x") + lax.axis_index("y")
    return {k: _join_shards(lax.dynamic_update_slice(g, own[None], (j, 0, 0)), SHARD_AXIS[k])
            for k, own, g in zip(names, sent, got)}


def _grad_chunks(grads, names):
    return [_stack_shards(grads[k], SHARD_AXIS[k]).astype(bf16) for k in names]


def _sum_group(l, names, got, chunks):
    j = 2 * lax.axis_index("x") + lax.axis_index("y")
    return {k: _sum_contributions(f"sum_l{l}_{k}", g, lax.dynamic_index_in_dim(ch, j, 0, keepdims=False))
            for k, g, ch in zip(names, got, chunks)}


def _both_layers(name, names, sums):
    c = lax.axis_index("c")
    mine = [[sums[l][k] for k in names] for l in range(DEPTH)]
    other = _swap_layers(name, mine)
    return {k: jnp.stack([jnp.where(c == 0, mine[0][i], other[i]), jnp.where(c == 0, other[i], mine[1][i])])
            for i, k in enumerate(names)}


def _gather_all_exchange(name, src):
    def plan(ins, lands, send_sems, recv_sems):
        coords = [lax.axis_index(a) for a in MESH_AXES]
        me = _device_index()
        sends, arrivals = [], []
        for f in range(1, N_DEVICES):
            peer = tuple(1 - cd if (f >> (2 - b)) & 1 else cd for b, cd in enumerate(coords))
            sends.append((None, pltpu.make_async_remote_copy(
                src_ref=ins[0], dst_ref=lands[0].at[me], send_sem=send_sems.at[f - 1], recv_sem=recv_sems.at[me],
                device_id=peer, device_id_type=pl.DeviceIdType.MESH)))
        for s in range(N_DEVICES):
            arrivals.append((me != s, pltpu.make_async_remote_copy(
                src_ref=ins[0], dst_ref=lands[0].at[s], send_sem=send_sems.at[0], recv_sem=recv_sems.at[s],
                device_id=tuple(coords), device_id_type=pl.DeviceIdType.MESH)))
        return sends, arrivals

    return _Exchange(name, plan, [src], [jax.ShapeDtypeStruct((N_DEVICES,) + src.shape, src.dtype)], N_DEVICES - 1, N_DEVICES)


def kernel(x, p, positions, g_mix, w_in, g_qc, w_uq, g_kvc, w_ukv, b_f, lru_conv_w, lru_conv_b, w_r, b_r, w_i, b_i, lru_lambda, g_out, w_o, g_ffn, w_up, ffn_conv_w, ffn_conv_b, w_down, g_ple, w_ple_gate, w_ple_proj, g_final, loss_target, m_g_mix, m_w_in, m_g_qc, m_w_uq, m_g_kvc, m_w_ukv, m_b_f, m_lru_conv_w, m_lru_conv_b, m_w_r, m_b_r, m_w_i, m_b_i, m_lru_lambda, m_g_out, m_w_o, m_g_ffn, m_w_up, m_ffn_conv_w, m_ffn_conv_b, m_w_down, m_g_ple, m_w_ple_gate, m_w_ple_proj, m_g_final, v_g_mix, v_w_in, v_g_qc, v_w_uq, v_g_kvc, v_w_ukv, v_b_f, v_lru_conv_w, v_lru_conv_b, v_w_r, v_b_r, v_w_i, v_b_i, v_lru_lambda, v_g_out, v_w_o, v_g_ffn, v_w_up, v_ffn_conv_w, v_ffn_conv_b, v_w_down, v_g_ple, v_w_ple_gate, v_w_ple_proj, v_g_final):
    given = locals()
    w = {k: given[k] for k in WEIGHTS}
    m = {k: given["m_" + k] for k in WEIGHTS}
    v = {k: given["v_" + k] for k in WEIGHTS}

    parts = {"mix": MIX_PART, "ffn": FFN_PART}
    groups = [(l, part) for l in range(DEPTH) for part in ("mix", "ffn")]
    sent = {g: _layer_shards(w, g[0], parts[g[1]]) for g in groups}
    first = _gather_exchange("gather_l0_mix", sent[groups[0]]).run()
    ahead = {g: _gather_exchange(f"gather_l{g[0]}_{g[1]}", sent[g]) for g in groups[1:]}
    pos = positions[0].astype(f32).reshape(-1, 1)
    for ex in ahead.values():
        pos = pos + ex.start(after=first[0])
    behind, layer_grads, chunks = {}, [{} for _ in range(DEPTH)], {}

    def weights_of(l, part, after):
        g = (l, part)
        full = _full_weights(parts[part], sent[g], first if g == groups[0] else ahead[g].finish(after=after))
        if part == "mix":
            full.update({k: w[k][l] for k in LAYER_WEIGHTS if k in REPLICATED})
        return full

    def grads_to(l, part, grads):
        g = (l, part)
        layer_grads[l].update(grads)
        chunks[g] = _grad_chunks(grads, parts[part])
        if g == groups[0]:
            return jnp.zeros((), f32)
        behind[g] = _scatter_exchange(f"scatter_l{l}_{part}", l, chunks[g])
        return behind[g].start()

    loss, dx, dg_final = _local_step(x[0], p[:, 0], pos, loss_target[0], w["g_final"], weights_of, grads_to)

    grads = {k: jnp.stack([layer_grads[l][k] for l in range(DEPTH)]) for k in LAYER_WEIGHTS if k in REPLICATED}
    grads["g_final"] = dg_final
    rep_shapes = [w[k].shape for k in REPLICATED] + [(1,)]
    contrib = _pack([grads[k] for k in REPLICATED] + [loss.reshape(1)])
    last = _scatter_exchange("scatter_l0_mix", 0, chunks[groups[0]])
    everyone = _gather_all_exchange("gather_replicated", contrib)
    started = (last.start() + everyone.start() + dx[0, 0]).reshape(1, 1)

    def adamw_of(names, g_sharded):
        out = {}
        for k in names:
            shape = w[k].shape
            flat = [t.reshape(-1, shape[-1]) for t in (w[k], g_sharded[k], m[k], v[k])]
            out[k] = [t.reshape(shape) for t in (flat[1],) + tuple(_adamw("adamw_" + k, *flat))]
        return out

    sums = [{} for _ in range(DEPTH)]
    for g in groups[1:]:
        sums[g[0]].update(_sum_group(g[0], parts[g[1]], behind[g].finish(after=started), chunks[g]))
    big = adamw_of(FFN_PART, _both_layers("swap_ffn", FFN_PART, sums))
    sums[0].update(_sum_group(0, MIX_PART, last.finish(after=big[FFN_PART[0]][1]), chunks[groups[0]]))
    big.update(adamw_of(MIX_PART, _both_layers("swap_mix", MIX_PART, sums)))

    g_rep = _sum_contributions("sum_replicated", everyone.finish(after=big[MIX_PART[0]][1])[0], contrib)
    zero = jnp.zeros((1,), f32)
    w_rep, m_rep, v_rep = (_pack([t[k] for k in REPLICATED] + [zero]) for t in (w, m, v))
    rep = [_unpack(b, rep_shapes) for b in (g_rep,) + tuple(_adamw("adamw_replicated", w_rep, g_rep, m_rep, v_rep))]

    outs = []
    for kind in range(4):
        by_name = {k: big[k][kind] for k in SHARDED}
        by_name.update(zip(REPLICATED, rep[kind][:-1]))
        outs.append([by_name[k] for k in WEIGHTS])
    total_loss = rep[0][-1][0]
    return (total_loss, dx.reshape(x.shape), *outs[0], *outs[1], *outs[2], *outs[3])
```

```python
import functools
import math

import numpy as np
import jax
import jax.numpy as jnp
from jax import lax
from jax.experimental import pallas as pl
from jax.experimental.pallas import tpu as pltpu

f32, bf16 = jnp.float32, jnp.bfloat16

D_MODEL = 1024
PLE_DIM = 256
MLA_HEADS, MLA_NOPE, MLA_ROPE, MLA_V = 4, 64, 32, 64
MLA_Q_RANK, MLA_KV_RANK = 192, 128
FOX_HEADS, FOX_HEAD_DIM = 4, 64
LRU_WIDTH, LRU_BLOCKS, LRU_BLOCK, LRU_CONV, LRU_C = 512, 8, 64, 4, 8.0
D_FF, FFN_CONV = 2816, 3
ROPE_THETA = 10000.0
EPS = 1e-6
DEPTH = 2
ADAM_LR, ADAM_B1, ADAM_B2, ADAM_EPS, ADAM_WD, ADAM_STEP = 0.001, 0.9, 0.999, 1e-08, 0.01, 10

LANE = 128
SUBLANE = 8
HEADS = 4

Z_FQ, Z_FK, Z_FV, Z_LX, Z_LG, Z_QC, Z_KVC, Z_KR, Z_FL, Z_W = 0, 512, 1024, 1536, 2048, 2560, 2816, 2944, 3072, 3200
QC_W = 256
ROPE_AT = 64


def _head_pad_map(n_heads, width):
    m = -np.ones(n_heads * LANE, np.int64)
    for h in range(n_heads):
        m[h * LANE:h * LANE + width] = h * width + np.arange(width)
    return m


def _z_map():
    m = -np.ones(Z_W, np.int64)
    o_qc, o_kvc, o_kr = 0, MLA_Q_RANK, MLA_Q_RANK + MLA_KV_RANK
    o_fq = o_kr + MLA_ROPE
    o_fk, o_fv = o_fq + 256, o_fq + 512
    o_fl = o_fv + 256
    o_lx = o_fl + FOX_HEADS
    o_lg = o_lx + LRU_WIDTH
    m[Z_FQ:Z_FQ + 512] = np.where(_head_pad_map(4, 64) >= 0, _head_pad_map(4, 64) + o_fq, -1)
    m[Z_FK:Z_FK + 512] = np.where(_head_pad_map(4, 64) >= 0, _head_pad_map(4, 64) + o_fk, -1)
    m[Z_FV:Z_FV + 512] = np.where(_head_pad_map(4, 64) >= 0, _head_pad_map(4, 64) + o_fv, -1)
    m[Z_LX:Z_LX + 512] = o_lx + np.arange(512)
    m[Z_LG:Z_LG + 512] = o_lg + np.arange(512)
    m[Z_QC:Z_QC + MLA_Q_RANK] = o_qc + np.arange(MLA_Q_RANK)
    m[Z_KVC:Z_KVC + MLA_KV_RANK] = o_kvc + np.arange(MLA_KV_RANK)
    m[Z_KR + ROPE_AT:Z_KR + ROPE_AT + MLA_ROPE] = o_kr + np.arange(MLA_ROPE)
    m[Z_FL:Z_FL + FOX_HEADS] = o_fl + np.arange(FOX_HEADS)
    return m


def _ukv_map():
    m = -np.ones(2 * HEADS * LANE, np.int64)
    for h in range(HEADS):
        m[h * LANE:h * LANE + MLA_NOPE] = h * (MLA_NOPE + MLA_V) + np.arange(MLA_NOPE)
        m[HEADS * LANE + h * LANE:HEADS * LANE + h * LANE + MLA_V] = h * (MLA_NOPE + MLA_V) + MLA_NOPE + np.arange(MLA_V)
    return m


def _omix_map():
    return np.concatenate([_head_pad_map(4, 64), np.where(_head_pad_map(4, 64) >= 0, _head_pad_map(4, 64) + 256, -1),
                           512 + np.arange(512)])


def _pad_to(m, n):
    return np.concatenate([m, -np.ones(n - m.shape[0], np.int64)])


def _runs(m):
    out, at = [], 0
    while at < len(m):
        end = at + 1
        while end < len(m) and (m[end] == m[end - 1] + 1 if m[at] >= 0 else m[end] < 0):
            end += 1
        out.append((int(m[at]), end - at))
        at = end
    return out


def _take_runs(a, m, axis):
    parts = []
    for start, size in _runs(m):
        if start < 0:
            shape = list(a.shape)
            shape[axis] = size
            parts.append(jnp.zeros(shape, a.dtype))
        else:
            parts.append(lax.slice_in_dim(a, start, start + size, axis=axis))
    return parts[0] if len(parts) == 1 else jnp.concatenate(parts, axis=axis)


def _take_pad(a, m, axis):
    return _take_runs(a, m, axis)


def _take_inv(a, m, axis):
    n = int(m.max()) + 1
    inv = np.zeros(n, np.int64)
    inv[m[m >= 0]] = np.nonzero(m >= 0)[0]
    return _take_runs(a, inv, axis)


Z_MAP = _z_map()
UQ_COL_MAP = _head_pad_map(HEADS, MLA_NOPE + MLA_ROPE)
UQ_ROW_MAP = _pad_to(np.arange(MLA_Q_RANK), QC_W)
UKV_MAP = _ukv_map()
OMIX_MAP = _omix_map()
OMIX_W = 1536


def _rope_tables(width, at):
    half = MLA_ROPE // 2
    inv = ROPE_THETA ** (-np.arange(half, dtype=np.float32) / half)
    freq = np.zeros((1, width), np.float32)
    m1 = np.zeros((1, width), np.float32)
    m2 = np.zeros((1, width), np.float32)
    for h in range(width // LANE):
        b = h * LANE + at
        freq[0, b:b + half] = inv
        freq[0, b + half:b + 2 * half] = inv
        m1[0, b:b + half] = 1.0
        m2[0, b + half:b + 2 * half] = 1.0
    return freq, m1, m2


def _view(r):
    return r if isinstance(r, tuple) else (r, r.shape[1], 0)


def _blk(dim, cap):
    if dim <= cap:
        return dim
    for b in range(cap, LANE - 1, -LANE):
        if dim % b == 0:
            return b
    return dim


@functools.partial(jax.custom_vjp, nondiff_argnums=(1, 2))
def _roll(x, shift, axis):
    return pltpu.roll(x, shift, axis)


def _roll_fwd(x, shift, axis):
    return pltpu.roll(x, shift, axis), None


def _roll_bwd(shift, axis, _, g):
    return (pltpu.roll(g, g.shape[axis] - shift, axis),)


_roll.defvjp(_roll_fwd, _roll_bwd)


def _rowwise(name, fn, rows, pars, outs, tb=256):
    rows = [_view(r) for r in rows]
    n = rows[0][0].shape[0]
    tb = min(tb, n)
    nr, npar = len(rows), len(pars)

    def kern(*refs):
        r = [refs[k][...].astype(f32) for k in range(nr)]
        p = [refs[nr + k][...] for k in range(npar)]
        res = fn(*r, *p)
        for o_ref, o in zip(refs[nr + npar:], res):
            o_ref[...] = o.astype(o_ref.dtype)

    in_specs = [pl.BlockSpec((tb, w), lambda i, j=idx: (i, j)) for (_, w, idx) in rows]
    in_specs += [pl.BlockSpec(p.shape, lambda i: (0, 0)) for p in pars]
    out_specs = [pl.BlockSpec((tb, w), lambda i: (i, 0)) for (w, _) in outs]
    out_shape = [jax.ShapeDtypeStruct((n, w), dt) for (w, dt) in outs]
    return pl.pallas_call(kern, name=name, grid=(n // tb,), in_specs=in_specs, out_specs=out_specs, out_shape=out_shape,
                          compiler_params=pltpu.CompilerParams(dimension_semantics=("parallel",)))(*[r[0] for r in rows], *pars)


def _rowwise_bwd(name, fn, rows, pars, cts, ndiff, adds=None, tb=256, dts=None):
    rows = [_view(r) for r in rows]
    dts = dts or [f32] * ndiff
    adds = adds or {}
    add_keys = sorted(adds)
    n = rows[0][0].shape[0]
    tb = min(tb, n)
    nr, npar, nct, nadd = len(rows), len(pars), len(cts), len(add_keys)

    def kern(*refs):
        i = pl.program_id(0)
        r = [refs[k][...].astype(f32) for k in range(nr)]
        p = [refs[nr + k][...] for k in range(npar)]
        ct = [refs[nr + npar + k][...].astype(f32) for k in range(nct)]
        ad = {key: refs[nr + npar + nct + k][...] for k, key in enumerate(add_keys)}
        o_refs = refs[nr + npar + nct + nadd:]

        def g(*d):
            return tuple(fn(*d[:ndiff], *r[ndiff:], *d[ndiff:]))

        _, vjp = jax.vjp(g, *r[:ndiff], *p)
        grads = vjp(tuple(ct))
        for k in range(ndiff):
            gk = grads[k]
            if k in ad:
                gk = gk + ad[k]
            o_refs[k][...] = gk.astype(o_refs[k].dtype)

        @pl.when(i == 0)
        def _():
            for k in range(npar):
                o_refs[ndiff + k][...] = jnp.zeros_like(o_refs[ndiff + k])

        for k in range(npar):
            o_refs[ndiff + k][...] += grads[ndiff + k]

    in_specs = [pl.BlockSpec((tb, w), lambda i, j=idx: (i, j)) for (_, w, idx) in rows]
    in_specs += [pl.BlockSpec(p.shape, lambda i: (0, 0)) for p in pars]
    in_specs += [pl.BlockSpec((tb, c.shape[1]), lambda i: (i, 0)) for c in cts]
    in_specs += [pl.BlockSpec((tb, adds[k].shape[1]), lambda i: (i, 0)) for k in add_keys]
    out_specs = [pl.BlockSpec((tb, rows[k][1]), lambda i: (i, 0)) for k in range(ndiff)]
    out_specs += [pl.BlockSpec(p.shape, lambda i: (0, 0)) for p in pars]
    out_shape = [jax.ShapeDtypeStruct((n, rows[k][1]), dts[k]) for k in range(ndiff)]
    out_shape += [jax.ShapeDtypeStruct(p.shape, f32) for p in pars]
    res = pl.pallas_call(kern, name=name, grid=(n // tb,), in_specs=in_specs, out_specs=out_specs, out_shape=out_shape,
                         compiler_params=pltpu.CompilerParams(dimension_semantics=("arbitrary",)))(
        *[r[0] for r in rows], *pars, *cts, *[adds[k] for k in add_keys])
    return res[:ndiff], res[ndiff:]


_DOT_DIMS = {"nn": ((1,), (0,)), "nt": ((1,), (1,)), "tn": ((0,), (0,))}

MM_VMEM_BUDGET = 36 * 2 ** 20
MM_MAX_TM = 1408
MM_STEP, MM_RESULT, MM_XPOSE, MM_CAST = 700.0, 7.5e-4, 9e-4, 1e-3


def _tile_candidates(dim):
    c = [d for d in range(LANE, dim + 1, LANE) if dim % d == 0]
    return c or [dim]


@functools.lru_cache(maxsize=None)
def _mm_tiles(mode, m, n, k, a_bytes, b_bytes, o_bytes):
    best, best_cost = None, None
    for tm in _tile_candidates(m):
        if tm > MM_MAX_TM:
            continue
        for tn in _tile_candidates(n):
            for tk in _tile_candidates(k):
                vmem = 2 * (tm * tk * a_bytes + tk * tn * b_bytes + tm * tn * o_bytes) + 4 * tm * tn * (2 if tk < k else 1)
                vmem += (2 * tm * tk if a_bytes > 2 else 0) + (2 * tk * tn if b_bytes > 2 else 0)
                if vmem > MM_VMEM_BUDGET:
                    continue
                steps = (m // tm) * (n // tn) * (k // tk)
                cost = steps * MM_STEP + m * n * (k // tk) * MM_RESULT
                if mode == "tn":
                    cost += m * k * (n // tn) * MM_XPOSE
                cost += (m * k * (n // tn) * MM_CAST if a_bytes > 2 else 0) + (k * n * (m // tm) * MM_CAST if b_bytes > 2 else 0)
                if best is None or cost < best_cost:
                    best, best_cost = (tm, tn, tk), cost
    return best


def _mm(name, a, b, mode="nn", out_dtype=f32, res=None):
    if mode == "nn":
        (m, k), (_, n) = a.shape, b.shape
    elif mode == "nt":
        (m, k), (n, _) = a.shape, b.shape
    else:
        (k, m), (_, n) = a.shape, b.shape
    has_res = res is not None
    tm, tn, tk = _mm_tiles(mode, m, n, k, a.dtype.itemsize, b.dtype.itemsize,
                           jnp.dtype(out_dtype).itemsize + (res.dtype.itemsize if has_res else 0))
    nk = k // tk
    dims = (_DOT_DIMS[mode], ((), ()))

    def kern(*refs):
        a_ref, b_ref = refs[0], refs[1]
        o_ref, acc_ref = refs[-2], refs[-1]
        kk = pl.program_id(2)
        part = lax.dot_general(a_ref[...].astype(bf16), b_ref[...].astype(bf16), dims, preferred_element_type=f32)

        def finish(out):
            if has_res:
                out = out + refs[2][...]
            o_ref[...] = out.astype(o_ref.dtype)

        if nk == 1:
            finish(part)
            return

        @pl.when(kk == 0)
        def _():
            acc_ref[...] = part

        @pl.when(jnp.logical_and(kk > 0, kk < nk - 1))
        def _():
            acc_ref[...] += part

        @pl.when(kk == nk - 1)
        def _():
            finish(acc_ref[...] + part)

    if mode == "tn":
        a_spec = pl.BlockSpec((tk, tm), lambda i, j, kk: (kk, i))
    else:
        a_spec = pl.BlockSpec((tm, tk), lambda i, j, kk: (i, kk))
    if mode == "nt":
        b_spec = pl.BlockSpec((tn, tk), lambda i, j, kk: (j, kk))
    else:
        b_spec = pl.BlockSpec((tk, tn), lambda i, j, kk: (kk, j))
    in_specs = [a_spec, b_spec]
    args = [a, b]
    if has_res:
        in_specs.append(pl.BlockSpec((tm, tn), lambda i, j, kk: (i, j)))
        args.append(res)
    return pl.pallas_call(
        kern, name=name, grid=(m // tm, n // tn, nk), in_specs=in_specs,
        out_specs=pl.BlockSpec((tm, tn), lambda i, j, kk: (i, j)),
        out_shape=jax.ShapeDtypeStruct((m, n), out_dtype),
        scratch_shapes=[pltpu.VMEM((tm, tn) if nk > 1 else (SUBLANE, LANE), f32)],
        compiler_params=pltpu.CompilerParams(dimension_semantics=("parallel", "parallel", "arbitrary")))(*args)


ATT_TQ, ATT_TK = 512, 512


def _att_tiles(s_len):
    tk = min(ATT_TK, s_len)
    return min(ATT_TQ, tk), tk


def _fold_scale(scale):
    return (scale, 1.0) if math.frexp(scale)[0] == 0.5 else (1.0, scale)


def _as_row(col):
    return jnp.max(jnp.broadcast_to(col, (col.shape[0], LANE)).T[:SUBLANE], axis=0, keepdims=True)


def _scores_t(kb, q_t, s_mul, ck, diag_offset, tq, tk):
    s = jnp.dot(kb, q_t, preferred_element_type=f32)
    if s_mul != 1.0:
        s = s * s_mul
    if ck is not None:
        s = s - ck
    if diag_offset is None:
        return s
    key = lax.broadcasted_iota(jnp.int32, (tk, tq), 0)
    query = lax.broadcasted_iota(jnp.int32, (tk, tq), 1) + diag_offset
    return jnp.where(key <= query, s, -jnp.inf)


ATT_ROWS = 64


def _finish_scores(s, s_mul, ck, first_row):
    if s_mul != 1.0:
        s = s * s_mul
    if ck is not None:
        s = s - ck
    if first_row is None:
        return s
    row = lax.broadcasted_iota(jnp.int32, s.shape, 0) + first_row
    col = lax.broadcasted_iota(jnp.int32, s.shape, 1)
    return jnp.where(col <= row, s, -jnp.inf)


def _attn_fwd(name, q, k, v, scale, c_row=None):
    (qa, qo), (ka, ko), (va, vo) = q, k, v
    s_len = qa.shape[0]
    t = _att_tiles(s_len)[1]
    nt = s_len // t
    decay = c_row is not None
    q_mul, s_mul = _fold_scale(scale)

    def kern(*refs):
        q_ref, k_ref, v_ref = refs[:3]
        o_ref, lse_ref, lse_row_ref = refs[-3:]
        i = pl.program_id(1)
        qb = (q_ref[...] * q_mul).astype(bf16)

        def step(j, carry, diagonal):
            m, l, acc = carry
            rows = pl.ds(pl.multiple_of(j * t, t), t)
            kb = k_ref[rows, :].astype(bf16)
            vb = v_ref[rows, :].astype(bf16)
            s = lax.dot_general(qb, kb, (_DOT_DIMS["nt"], ((), ())), preferred_element_type=f32)
            s = _finish_scores(s, s_mul, refs[3][j] if decay else None, 0 if diagonal else None)
            m_new = jnp.maximum(m, jnp.max(s, axis=1, keepdims=True))
            alpha = jnp.exp(m - m_new)
            p = jnp.exp(s - m_new)
            l = alpha * l + jnp.sum(p, axis=1, keepdims=True)
            acc = alpha * acc + jnp.dot(p.astype(bf16), vb, preferred_element_type=f32)
            return m_new, l, acc

        init = (jnp.full((t, 1), -jnp.inf, f32), jnp.zeros((t, 1), f32), jnp.zeros((t, LANE), f32))
        m, l, acc = step(i, lax.fori_loop(0, i, lambda j, c: step(j, c, False), init), True)
        o_ref[...] = acc / l
        lse = m + jnp.log(l)
        lse_ref[...] = lse
        lse_row_ref[...] = _as_row(lse)

    in_specs = [pl.BlockSpec((t, LANE), lambda h, i: (i, qo + h)),
                pl.BlockSpec((s_len, LANE), lambda h, i: (0, ko + h)),
                pl.BlockSpec((s_len, LANE), lambda h, i: (0, vo + h))]
    args = [qa, ka, va]
    if decay:
        in_specs.append(pl.BlockSpec((None, nt, 1, t), lambda h, i: (h, 0, 0, 0)))
        args.append(c_row)
    return pl.pallas_call(
        kern, name=name, grid=(HEADS, nt), in_specs=in_specs,
        out_specs=[pl.BlockSpec((t, LANE), lambda h, i: (i, h)), pl.BlockSpec((None, t, 1), lambda h, i: (h, i, 0)),
                   pl.BlockSpec((None, None, 1, t), lambda h, i: (h, i, 0, 0))],
        out_shape=[jax.ShapeDtypeStruct((s_len, HEADS * LANE), f32), jax.ShapeDtypeStruct((HEADS, s_len, 1), f32),
                   jax.ShapeDtypeStruct((HEADS, nt, 1, t), f32)],
        compiler_params=pltpu.CompilerParams(dimension_semantics=("parallel", "arbitrary")))(*args)


def _attn_dq(name, q, k, v, o, do, lse, scale, c_row=None):
    (qa, qo), (ka, ko), (va, vo) = q, k, v
    s_len = qa.shape[0]
    t = _att_tiles(s_len)[1]
    nt = s_len // t
    decay = c_row is not None
    q_mul, s_mul = _fold_scale(scale)

    rp = min(ATT_ROWS, t)

    def kern(*refs):
        q_ref, k_ref, v_ref, o_ref, do_ref, lse_ref = refs[:6]
        dq_ref, delta_row_ref, drow_ref, delta_ref, s_ref, dp_ref, ds_ref = refs[-7:]
        i = pl.program_id(1)
        qb = (q_ref[...] * q_mul).astype(bf16)
        dob = do_ref[...]
        delta = jnp.sum(dob * o_ref[...], axis=1, keepdims=True)
        delta_ref[...] = delta
        delta_row_ref[...] = _as_row(delta)
        dob = dob.astype(bf16)
        drow_ref[...] = jnp.zeros((t, 1), f32)
        dq_ref[...] = jnp.zeros((t, LANE), f32)

        def step(j, diagonal):
            rows = pl.ds(pl.multiple_of(j * t, t), t)
            kb = k_ref[rows, :].astype(bf16)
            s_ref[...] = lax.dot_general(qb, kb, (_DOT_DIMS["nt"], ((), ())), preferred_element_type=f32)
            dp_ref[...] = lax.dot_general(dob, v_ref[rows, :].astype(bf16), (_DOT_DIMS["nt"], ((), ())),
                                          preferred_element_type=f32)
            ck = refs[6][j] if decay else None

            def rows_of(c, carry):
                r = slice(c * rp, (c + 1) * rp)
                s = _finish_scores(s_ref[r, :], s_mul, ck, c * rp if diagonal else None)
                ds = jnp.exp(s - lse_ref[r, :]) * (dp_ref[r, :] - delta_ref[r, :])
                drow_ref[r, :] += jnp.sum(ds, axis=1, keepdims=True)
                ds_ref[r, :] = ds.astype(bf16)
                return carry

            for c in range(t // rp):
                rows_of(c, 0)
            dq_ref[...] += jnp.dot(ds_ref[...], kb, preferred_element_type=f32)

        def below(j, carry):
            step(j, False)
            return carry

        lax.fori_loop(0, i, below, 0)
        step(i, True)
        dq_ref[...] = dq_ref[...] * scale

    in_specs = [pl.BlockSpec((t, LANE), lambda h, i: (i, qo + h)),
                pl.BlockSpec((s_len, LANE), lambda h, i: (0, ko + h)),
                pl.BlockSpec((s_len, LANE), lambda h, i: (0, vo + h)),
                pl.BlockSpec((t, LANE), lambda h, i: (i, h)),
                pl.BlockSpec((t, LANE), lambda h, i: (i, h)),
                pl.BlockSpec((None, t, 1), lambda h, i: (h, i, 0))]
    args = [qa, ka, va, o, do, lse]
    if decay:
        in_specs.append(pl.BlockSpec((None, nt, 1, t), lambda h, i: (h, 0, 0, 0)))
        args.append(c_row)
    col = pl.BlockSpec((None, t, 1), lambda h, i: (h, i, 0))
    return pl.pallas_call(
        kern, name=name, grid=(HEADS, nt), in_specs=in_specs,
        out_specs=[pl.BlockSpec((t, LANE), lambda h, i: (i, h)), pl.BlockSpec((None, None, 1, t), lambda h, i: (h, i, 0, 0)), col],
        out_shape=[jax.ShapeDtypeStruct((s_len, HEADS * LANE), f32), jax.ShapeDtypeStruct((HEADS, nt, 1, t), f32),
                   jax.ShapeDtypeStruct((HEADS, s_len, 1), f32)],
        scratch_shapes=[pltpu.VMEM((t, 1), f32), pltpu.VMEM((t, t), f32), pltpu.VMEM((t, t), f32), pltpu.VMEM((t, t), bf16)],
        compiler_params=pltpu.CompilerParams(dimension_semantics=("parallel", "arbitrary")))(*args)


def _attn_dkv(name, q, k, v, do, lse, delta, scale, c_col=None):
    (qa, qo), (ka, ko), (va, vo) = q, k, v
    s_len = qa.shape[0]
    tq, tk = _att_tiles(s_len)
    assert lse.shape == (HEADS, s_len // tq, 1, tq), (lse.shape, tq)
    nq, per = s_len // tq, tk // tq
    decay = c_col is not None
    q_mul, s_mul = _fold_scale(scale)

    def kern(*refs):
        q_ref, k_ref, v_ref, do_ref, lse_ref, delta_ref = refs[:6]
        j = pl.program_id(1)
        kb = k_ref[...].astype(bf16)
        vb = v_ref[...].astype(bf16)
        ck = refs[6][...] if decay else None

        def step(i, carry, diagonal):
            dk, dv, dsum = carry
            for d in range(per):
                tile = i * per + d
                rows = pl.ds(pl.multiple_of(tile * tq, tq), tq)
                qb = (q_ref[rows, :] * q_mul).astype(bf16)
                dob = do_ref[rows, :].astype(bf16)
                s = _scores_t(kb, qb.T, s_mul, ck, d * tq if diagonal else None, tq, tk)
                p = jnp.exp(s - lse_ref[tile])
                dv = dv + jnp.dot(p.astype(bf16), dob, preferred_element_type=f32)
                dp = jnp.dot(vb, dob.T, preferred_element_type=f32)
                ds = p * (dp - delta_ref[tile])
                dk = dk + jnp.dot(ds.astype(bf16), qb, preferred_element_type=f32)
                if decay:
                    dsum = dsum + ds
            return dk, dv, dsum

        init = (jnp.zeros((tk, LANE), f32), jnp.zeros((tk, LANE), f32), jnp.zeros((tk, tq), f32))
        dk, dv, dsum = lax.fori_loop(j + 1, s_len // tk, lambda i, c: step(i, c, False), step(j, init, True))
        if decay:
            dk_ref, dv_ref, dc_ref = refs[-3:]
            dc_ref[...] = -jnp.sum(dsum, axis=1, keepdims=True)
        else:
            dk_ref, dv_ref = refs[-2:]
        dk_ref[...] = dk * s_mul
        dv_ref[...] = dv

    stat = pl.BlockSpec((None, nq, 1, tq), lambda h, j: (h, 0, 0, 0))
    in_specs = [pl.BlockSpec((s_len, LANE), lambda h, j: (0, qo + h)),
                pl.BlockSpec((tk, LANE), lambda h, j: (j, ko + h)),
                pl.BlockSpec((tk, LANE), lambda h, j: (j, vo + h)),
                pl.BlockSpec((s_len, LANE), lambda h, j: (0, h)), stat, stat]
    args = [qa, ka, va, do, lse, delta]
    out_specs = [pl.BlockSpec((tk, LANE), lambda h, j: (j, h)), pl.BlockSpec((tk, LANE), lambda h, j: (j, h))]
    out_shape = [jax.ShapeDtypeStruct((s_len, HEADS * LANE), f32), jax.ShapeDtypeStruct((s_len, HEADS * LANE), f32)]
    if decay:
        in_specs.append(pl.BlockSpec((None, tk, 1), lambda h, j: (h, j, 0)))
        args.append(c_col)
        out_specs.append(pl.BlockSpec((None, tk, 1), lambda h, j: (h, j, 0)))
        out_shape.append(jax.ShapeDtypeStruct((HEADS, s_len, 1), f32))
    return pl.pallas_call(
        kern, name=name, grid=(HEADS, s_len // tk), in_specs=in_specs, out_specs=out_specs, out_shape=out_shape,
        compiler_params=pltpu.CompilerParams(dimension_semantics=("parallel", "arbitrary")))(*args)


CONV_TS, CONV_CB = 1024, 256
FFN_ROWS = 64


def _conv_fwd(name, x, w, b, taps):
    xa, width, xidx = _view(x)
    s_len = xa.shape[0]
    ts, cb = min(CONV_TS, s_len), CONV_CB
    xo = xidx * width // cb

    def kern(x_ref, halo_ref, w_ref, b_ref, o_ref):
        i = pl.program_id(1)
        xb = x_ref[...]
        halo = jnp.where(i == 0, 0.0, halo_ref[...])
        xx = jnp.concatenate([halo, xb], axis=0)
        out = b_ref[...] + w_ref[taps - 1:taps, :] * xb
        for k in range(taps - 1):
            out = out + w_ref[k:k + 1, :] * pltpu.roll(xx, taps - 1 - k, 0)[SUBLANE:]
        o_ref[...] = out

    return pl.pallas_call(
        kern, name=name, grid=(width // cb, s_len // ts),
        in_specs=[pl.BlockSpec((ts, cb), lambda j, i: (i, xo + j)),
                  pl.BlockSpec((SUBLANE, cb), lambda j, i: (jnp.maximum(i * (ts // SUBLANE) - 1, 0), xo + j)),
                  pl.BlockSpec((taps, cb), lambda j, i: (0, j)),
                  pl.BlockSpec((1, cb), lambda j, i: (0, j))],
        out_specs=pl.BlockSpec((ts, cb), lambda j, i: (i, j)),
        out_shape=jax.ShapeDtypeStruct((s_len, width), f32),
        compiler_params=pltpu.CompilerParams(dimension_semantics=("parallel", "parallel")))(xa, xa, w, b)


def _conv_bwd(name, x, dout, w, taps, dout2=None, dx_dtype=f32):
    xa, width, xidx = _view(x)
    s_len = xa.shape[0]
    ts, cb = min(CONV_TS, s_len), CONV_CB
    xo = xidx * width // cb
    n_i = s_len // ts
    two = dout2 is not None

    def kern(*refs):
        x_ref, halo_ref, w_ref = refs[:3]
        dx_ref, dw_ref, db_ref = refs[-3:]
        i = pl.program_id(1)
        if two:
            d = refs[3][...] + refs[5][...]
            dn = refs[4][...] + refs[6][...]
        else:
            d, dn = refs[3][...], refs[4][...]
        dn = jnp.where(i == n_i - 1, 0.0, dn)
        xb = x_ref[...]
        halo = jnp.where(i == 0, 0.0, halo_ref[...])
        xx = jnp.concatenate([halo, xb], axis=0)
        dd = jnp.concatenate([d, dn], axis=0)

        @pl.when(i == 0)
        def _():
            dw_ref[...] = jnp.zeros_like(dw_ref)
            db_ref[...] = jnp.zeros_like(db_ref)

        dx = w_ref[taps - 1:taps, :] * d
        dw_ref[taps - 1:taps, :] += jnp.sum(d * xb, axis=0, keepdims=True)
        for k in range(taps - 1):
            sh = taps - 1 - k
            dx = dx + w_ref[k:k + 1, :] * pltpu.roll(dd, ts + SUBLANE - sh, 0)[:ts]
            dw_ref[k:k + 1, :] += jnp.sum(d * pltpu.roll(xx, sh, 0)[SUBLANE:], axis=0, keepdims=True)
        dx_ref[...] = dx.astype(dx_ref.dtype)
        db_ref[...] += jnp.sum(d, axis=0, keepdims=True)

    d_spec = pl.BlockSpec((ts, cb), lambda j, i: (i, j))
    dn_spec = pl.BlockSpec((SUBLANE, cb), lambda j, i: (jnp.minimum((i + 1) * (ts // SUBLANE), s_len // SUBLANE - 1), j))
    in_specs = [pl.BlockSpec((ts, cb), lambda j, i: (i, xo + j)),
                pl.BlockSpec((SUBLANE, cb), lambda j, i: (jnp.maximum(i * (ts // SUBLANE) - 1, 0), xo + j)),
                pl.BlockSpec((taps, cb), lambda j, i: (0, j)), d_spec, dn_spec]
    args = [xa, xa, w, dout, dout]
    if two:
        in_specs += [d_spec, dn_spec]
        args += [dout2, dout2]
    return pl.pallas_call(
        kern, name=name, grid=(width // cb, n_i), in_specs=in_specs,
        out_specs=[pl.BlockSpec((ts, cb), lambda j, i: (i, j)), pl.BlockSpec((taps, cb), lambda j, i: (0, j)),
                   pl.BlockSpec((1, cb), lambda j, i: (0, j))],
        out_shape=[jax.ShapeDtypeStruct((s_len, width), dx_dtype), jax.ShapeDtypeStruct((taps, width), f32),
                   jax.ShapeDtypeStruct((1, width), f32)],
        compiler_params=pltpu.CompilerParams(dimension_semantics=("parallel", "arbitrary")))(*args)


def _conv_rows(xx, w_ref, b_ref, taps):
    out = b_ref[...] + w_ref[taps - 1:taps, :] * xx[SUBLANE:]
    for k in range(taps - 1):
        out = out + w_ref[k:k + 1, :] * pltpu.roll(xx, taps - 1 - k, 0)[SUBLANE:]
    return out


def _ffn_act_fwd(name, up, w, b):
    s_len = up.shape[0]
    ts, cb = min(CONV_TS, s_len), CONV_CB
    nf = D_FF // cb

    def kern(g_ref, gp_ref, v_ref, vp_ref, wg_ref, wv_ref, bg_ref, bv_ref, o_ref):
        first = pl.program_id(1) == 0
        ug = _conv_rows(jnp.concatenate([jnp.where(first, 0.0, gp_ref[...]), g_ref[...]], axis=0), wg_ref, bg_ref, FFN_CONV)
        uv = _conv_rows(jnp.concatenate([jnp.where(first, 0.0, vp_ref[...]), v_ref[...]], axis=0), wv_ref, bv_ref, FFN_CONV)
        o_ref[...] = (jax.nn.silu(ug) * uv).astype(o_ref.dtype)

    def half(off):
        return [pl.BlockSpec((ts, cb), lambda j, i: (i, off + j)),
                pl.BlockSpec((SUBLANE, cb), lambda j, i: (jnp.maximum(i * (ts // SUBLANE) - 1, 0), off + j))]

    def par(rows, off):
        return pl.BlockSpec((rows, cb), lambda j, i: (0, off + j))

    return pl.pallas_call(
        kern, name=name, grid=(nf, s_len // ts),
        in_specs=half(0) + half(nf) + [par(FFN_CONV, 0), par(FFN_CONV, nf), par(1, 0), par(1, nf)],
        out_specs=pl.BlockSpec((ts, cb), lambda j, i: (i, j)),
        out_shape=jax.ShapeDtypeStruct((s_len, D_FF), bf16),
        compiler_params=pltpu.CompilerParams(dimension_semantics=("parallel", "parallel")))(up, up, up, up, w, w, b, b)


def _ffn_act_bwd(name, up, dact, w, b):
    s_len = up.shape[0]
    ts, cb = min(CONV_TS, s_len), CONV_CB
    nf = D_FF // cb
    n_i = s_len // ts
    taps = FFN_CONV

    ch = min(FFN_ROWS, ts)

    def kern(g_ref, gp_ref, gn_ref, v_ref, vp_ref, vn_ref, d_ref, dn_ref, wg_ref, wv_ref, bg_ref, bv_ref,
             dg_ref, dv_ref, dwg_ref, dwv_ref, dbg_ref, dbv_ref, gx_ref, vx_ref, dd_ref):
        i = pl.program_id(1)
        first, last = i == 0, i == n_i - 1
        for x_ref, p_ref, n_ref, ext in ((g_ref, gp_ref, gn_ref, gx_ref), (v_ref, vp_ref, vn_ref, vx_ref)):
            ext[:SUBLANE, :] = jnp.where(first, 0.0, p_ref[...])
            ext[SUBLANE:SUBLANE + ts, :] = x_ref[...]
            ext[SUBLANE + ts:, :] = jnp.where(last, 0.0, n_ref[...])
        dd_ref[:ts, :] = d_ref[...]
        dd_ref[ts:, :] = jnp.where(last, 0.0, dn_ref[...])

        @pl.when(first)
        def _():
            for ref in (dwg_ref, dwv_ref, dbg_ref, dbv_ref):
                ref[...] = jnp.zeros_like(ref)

        def rows_of(c, carry):
            r0 = pl.multiple_of(c * ch, ch)
            gx, vx = gx_ref[pl.ds(r0, ch + 2 * SUBLANE), :], vx_ref[pl.ds(r0, ch + 2 * SUBLANE), :]
            ug, uv = _conv_rows(gx, wg_ref, bg_ref, taps), _conv_rows(vx, wv_ref, bv_ref, taps)
            dd = dd_ref[pl.ds(r0, ch + SUBLANE), :]
            sg = jax.nn.sigmoid(ug)
            out = []
            for du, xx, w_ref, dx_ref, sums in ((dd * uv * (sg * (1.0 + ug * (1.0 - sg))), gx, wg_ref, dg_ref, carry[0]),
                                                (dd * (ug * sg), vx, wv_ref, dv_ref, carry[1])):
                d = du[:ch]
                dx = w_ref[taps - 1:taps, :] * d
                new = [None] * (taps + 1)
                new[taps - 1] = sums[taps - 1] + jnp.sum(d * xx[SUBLANE:SUBLANE + ch], axis=0, keepdims=True)
                for k in range(taps - 1):
                    sh = taps - 1 - k
                    dx = dx + w_ref[k:k + 1, :] * pltpu.roll(du, ch + SUBLANE - sh, 0)[:ch]
                    new[k] = sums[k] + jnp.sum(d * pltpu.roll(xx, sh, 0)[SUBLANE:SUBLANE + ch], axis=0, keepdims=True)
                new[taps] = sums[taps] + jnp.sum(d, axis=0, keepdims=True)
                dx_ref[pl.ds(r0, ch), :] = dx.astype(dx_ref.dtype)
                out.append(tuple(new))
            return tuple(out)

        zero = tuple(jnp.zeros((1, cb), f32) for _ in range(taps + 1))
        sums_g, sums_v = lax.fori_loop(0, ts // ch, rows_of, (zero, zero))
        for sums, dw_ref, db_ref in ((sums_g, dwg_ref, dbg_ref), (sums_v, dwv_ref, dbv_ref)):
            for k in range(taps):
                dw_ref[k:k + 1, :] += sums[k]
            db_ref[...] += sums[taps]

    blocks = s_len // SUBLANE

    def half(off):
        return [pl.BlockSpec((ts, cb), lambda j, i: (i, off + j)),
                pl.BlockSpec((SUBLANE, cb), lambda j, i: (jnp.maximum(i * (ts // SUBLANE) - 1, 0), off + j)),
                pl.BlockSpec((SUBLANE, cb), lambda j, i: (jnp.minimum((i + 1) * (ts // SUBLANE), blocks - 1), off + j))]

    def par(rows, off):
        return pl.BlockSpec((rows, cb), lambda j, i: (0, off + j))

    d_specs = [pl.BlockSpec((ts, cb), lambda j, i: (i, j)),
               pl.BlockSpec((SUBLANE, cb), lambda j, i: (jnp.minimum((i + 1) * (ts // SUBLANE), blocks - 1), j))]
    out_par = [pl.BlockSpec((r, cb), lambda j, i: (0, j)) for r in (taps, taps, 1, 1)]
    return pl.pallas_call(
        kern, name=name, grid=(nf, n_i),
        in_specs=half(0) + half(nf) + d_specs + [par(taps, 0), par(taps, nf), par(1, 0), par(1, nf)],
        out_specs=[pl.BlockSpec((ts, cb), lambda j, i: (i, j))] * 2 + out_par,
        out_shape=[jax.ShapeDtypeStruct((s_len, D_FF), bf16)] * 2 + [jax.ShapeDtypeStruct((taps, D_FF), f32)] * 2
        + [jax.ShapeDtypeStruct((1, D_FF), f32)] * 2,
        scratch_shapes=[pltpu.VMEM((ts + 2 * SUBLANE, cb), f32)] * 2 + [pltpu.VMEM((ts + SUBLANE, cb), f32)],
        compiler_params=pltpu.CompilerParams(dimension_semantics=("parallel", "arbitrary")))(
        up, up, up, up, up, up, dact, dact, w, w, b, b)


SCAN_ROWS = 128


def _block_scan(a, b, reverse):
    t = a.shape[0]
    row = lax.broadcasted_iota(jnp.int32, a.shape, 0)
    d = 1
    while d < t:
        keep = row < t - d if reverse else row >= d
        shift = t - d if reverse else d
        a_far = jnp.where(keep, pltpu.roll(a, shift, 0), 1.0)
        b_far = jnp.where(keep, pltpu.roll(b, shift, 0), 0.0)
        b = a * b_far + b
        a = a * a_far
        d *= 2
    return a, b


def _scan_fwd(name, a, b):
    s_len, width = a.shape
    t = min(SCAN_ROWS, s_len)

    def kern(a_ref, b_ref, h_ref):
        def block(k, carry):
            rows = pl.ds(pl.multiple_of(k * t, t), t)
            acc, h = _block_scan(a_ref[rows, :], b_ref[rows, :], False)
            h_ref[rows, :] = h + acc * carry
            return h_ref[pl.ds(k * t + t - 1, 1), :]

        lax.fori_loop(0, s_len // t, block, jnp.zeros((1, LANE), f32))

    spec = pl.BlockSpec((s_len, LANE), lambda j: (0, j))
    return pl.pallas_call(
        kern, name=name, grid=(width // LANE,), in_specs=[spec, spec], out_specs=spec,
        out_shape=jax.ShapeDtypeStruct((s_len, width), f32),
        compiler_params=pltpu.CompilerParams(dimension_semantics=("parallel",)))(a, b)


def _scan_bwd(name, a_next, h_prev, dh):
    s_len, width = dh.shape
    t = min(SCAN_ROWS, s_len)
    n_blocks = s_len // t

    def kern(an_ref, hp_ref, dh_ref, da_ref, db_ref):
        def block(kk, carry):
            k = n_blocks - 1 - kk
            rows = pl.ds(pl.multiple_of(k * t, t), t)
            acc, g = _block_scan(an_ref[rows, :], dh_ref[rows, :], True)
            g = g + acc * carry
            db_ref[rows, :] = g
            da_ref[rows, :] = g * hp_ref[rows, :]
            return db_ref[pl.ds(k * t, 1), :]

        lax.fori_loop(0, n_blocks, block, jnp.zeros((1, LANE), f32))

    spec = pl.BlockSpec((s_len, LANE), lambda j: (0, j))
    return pl.pallas_call(
        kern, name=name, grid=(width // LANE,), in_specs=[spec, spec, spec], out_specs=[spec, spec],
        out_shape=[jax.ShapeDtypeStruct((s_len, width), f32)] * 2,
        compiler_params=pltpu.CompilerParams(dimension_semantics=("parallel",)))(a_next, h_prev, dh)


def _lane_cumsum(x, reverse):
    n = x.shape[1]
    lane = lax.broadcasted_iota(jnp.int32, x.shape, 1)
    sh = 1
    while sh < n:
        if reverse:
            x = x + jnp.where(lane < n - sh, pltpu.roll(x, n - sh, 1), 0.0)
        else:
            x = x + jnp.where(lane >= sh, pltpu.roll(x, sh, 1), 0.0)
        sh *= 2
    return x


def _decay_fwd(name, fl_t, b8):
    def kern(f_ref, b_ref, c_ref):
        c_ref[...] = _lane_cumsum(jax.nn.log_sigmoid(f_ref[...] + b_ref[...]), False)

    return pl.pallas_call(kern, name=name, out_shape=jax.ShapeDtypeStruct(fl_t.shape, f32))(fl_t, b8)


def _decay_bwd(name, fl_t, b8, dc_key, dc_query):
    def kern(f_ref, b_ref, dck_ref, dcq_ref, df_ref, db_ref):
        dlogf = _lane_cumsum(dck_ref[...] + dcq_ref[...], True)
        df = dlogf * jax.nn.sigmoid(-(f_ref[...] + b_ref[...]))
        df_ref[...] = df
        db_ref[...] = jnp.sum(df, axis=1, keepdims=True)

    return pl.pallas_call(kern, name=name, out_shape=[jax.ShapeDtypeStruct(fl_t.shape, f32),
                                                      jax.ShapeDtypeStruct((SUBLANE, 1), f32)])(fl_t, b8, dc_key, dc_query)


def _rms(x, g, n):
    return x * lax.rsqrt(jnp.sum(x * x, axis=-1, keepdims=True) * (1.0 / n) + EPS) * g


def _loss_head(name, h, target, g, tb=256):
    n, d = h.shape
    tb = min(tb, n)

    def kern(h_ref, t_ref, g_ref, loss_ref, dh_ref, dg_ref):
        i = pl.program_id(0)
        tgt = t_ref[...]

        def f(hv, gv):
            err = _rms(hv, gv, d) - tgt
            return 0.5 * jnp.sum(jnp.sum(err * err, axis=-1, keepdims=True) * (1.0 / d), axis=0, keepdims=True)

        val, vjp = jax.vjp(f, h_ref[...], g_ref[...])
        dh, dg = vjp(jnp.ones((1, 1), f32))
        dh_ref[...] = dh

        @pl.when(i == 0)
        def _():
            loss_ref[...] = jnp.zeros_like(loss_ref)
            dg_ref[...] = jnp.zeros_like(dg_ref)

        loss_ref[...] += val
        dg_ref[...] += dg

    return pl.pallas_call(
        kern, name=name, grid=(n // tb,),
        in_specs=[pl.BlockSpec((tb, d), lambda i: (i, 0)), pl.BlockSpec((tb, d), lambda i: (i, 0)),
                  pl.BlockSpec((1, d), lambda i: (0, 0))],
        out_specs=[pl.BlockSpec((1, 1), lambda i: (0, 0)), pl.BlockSpec((tb, d), lambda i: (i, 0)),
                   pl.BlockSpec((1, d), lambda i: (0, 0))],
        out_shape=[jax.ShapeDtypeStruct((1, 1), f32), jax.ShapeDtypeStruct((n, d), f32), jax.ShapeDtypeStruct((1, d), f32)],
        compiler_params=pltpu.CompilerParams(dimension_semantics=("arbitrary",)))(h, target, g)


def _f_norm(x, g):
    return (_rms(x, g, D_MODEL),)


def _f_latent(qc, kvc, gq, gkv):
    return _rms(qc, gq, MLA_Q_RANK), _rms(kvc, gkv, MLA_KV_RANK)


def _f_rope_table(pos, freq, m1, m2):
    ang = pos * freq
    sin = jnp.sin(ang)
    return jnp.cos(ang), -sin * m1, sin * m2


def _rope(x, cos, s_up, s_down):
    w = x.shape[1]
    return x * cos + _roll(x, w - MLA_ROPE // 2, 1) * s_up + _roll(x, MLA_ROPE // 2, 1) * s_down


def _f_mla_prep(q, kpart, kr, cos, s_up, s_down):
    def heads(t):
        return jnp.concatenate([t] * HEADS, axis=1)

    kr = _rope(kr, cos, s_up, s_down)
    return _rope(q, heads(cos), heads(s_up), heads(s_down)), kpart + heads(kr)


def _f_lru_gate(gates, xc, b_r, b_i, lam):
    r = jax.nn.sigmoid(gates[:, :LRU_WIDTH] + b_r)
    i = jax.nn.sigmoid(gates[:, LRU_WIDTH:] + b_i)
    log_a = -LRU_C * r * jax.nn.softplus(-lam)
    mult = jnp.sqrt(-jnp.tanh(log_a) * (1.0 + jnp.exp(2.0 * log_a)))
    return jnp.exp(log_a), mult * (i * xc)


def _f_merge(o_mla, o_fox, hs, lg, g):
    o_lru = hs * jax.nn.gelu(lg)
    return (jnp.concatenate([_rms(o_mla, g[:, :512], HEADS * MLA_V), _rms(o_fox, g[:, 512:1024], HEADS * FOX_HEAD_DIM),
                             _rms(o_lru, g[:, 1024:], LRU_WIDTH)], axis=1),)


def _f_ffn_gate(u):
    return (jax.nn.silu(u[:, :D_FF]) * u[:, D_FF:],)


def _f_ple(h, gpre, pp):
    return (h + jax.nn.sigmoid(gpre) * pp,)


MIX_PART = ["w_in", "w_uq", "w_ukv", "lru_conv_w"]
FFN_PART = ["w_o", "w_up", "ffn_conv_w", "w_down", "w_ple_gate", "w_ple_proj"]


def _prep_mix_weights(w):
    eye = jnp.eye(LRU_BLOCKS, dtype=f32)

    def block_diag(m):
        return (eye[:, None, :, None] * m[:, :, None, :]).reshape(LRU_WIDTH, LRU_WIDTH)

    return dict(
        w_in=_take_pad(w["w_in"], Z_MAP, 1),
        w_uq=_take_pad(_take_pad(w["w_uq"], UQ_COL_MAP, 1), UQ_ROW_MAP, 0),
        w_ukv=_take_pad(w["w_ukv"], UKV_MAP, 1),
        w_ri=jnp.concatenate([block_diag(w["w_r"]), block_diag(w["w_i"])], axis=1).astype(bf16),
        g_mix=w["g_mix"].reshape(1, -1), g_ffn=w["g_ffn"].reshape(1, -1), g_ple=w["g_ple"].reshape(1, -1),
        g_qc=_take_pad(w["g_qc"], UQ_ROW_MAP, 0).reshape(1, -1), g_kvc=w["g_kvc"].reshape(1, -1),
        g_out=_take_pad(w["g_out"], OMIX_MAP, 0).reshape(1, -1),
        b_f8=_take_pad(w["b_f"], _pad_to(np.arange(FOX_HEADS), SUBLANE), 0).reshape(SUBLANE, 1),
        lru_conv_w=w["lru_conv_w"], lru_conv_b=w["lru_conv_b"].reshape(1, -1),
        b_r=w["b_r"].reshape(1, -1), b_i=w["b_i"].reshape(1, -1), lam=w["lru_lambda"].reshape(1, -1),
        ffn_conv_b=w["ffn_conv_b"].reshape(1, -1),
    )


def _prep_ffn_weights(w):
    return dict(w_o=_take_pad(w["w_o"], OMIX_MAP, 0),
                w_up=w["w_up"], w_up_g=w["w_up"][:, :D_FF], w_up_v=w["w_up"][:, D_FF:], ffn_conv_w=w["ffn_conv_w"],
                w_down=w["w_down"], w_ple_gate=w["w_ple_gate"], w_ple_proj=w["w_ple_proj"])


def _rope_rows(pos):
    consts = [jnp.asarray(t) for t in _rope_tables(LANE, ROPE_AT)]
    return _rowwise("rope_table", _f_rope_table, [pos], consts, [(LANE, f32)] * 3)


def _key_decay(c_t, s_len):
    t = _att_tiles(s_len)[1]
    return c_t[:HEADS].reshape(HEADS, s_len // t, 1, t), c_t[:HEADS].reshape(HEADS, s_len, 1)


def _layer_fwd(l, h0, p_l, rope, weights_of):
    s_len = h0.shape[0]
    n = f"l{l}_"
    w = _prep_mix_weights(weights_of("mix", h0))
    xn, = _rowwise(n + "norm_mix", _f_norm, [h0], [w["g_mix"]], [(D_MODEL, bf16)])
    z = _mm(n + "in_proj", xn, w["w_in"])
    zq = (z, QC_W, Z_QC // QC_W)
    zkv = (z, LANE, Z_KVC // LANE)
    zkr = (z, LANE, Z_KR // LANE)
    zlx = (z, LRU_WIDTH, Z_LX // LRU_WIDTH)
    zlg = (z, LRU_WIDTH, Z_LG // LRU_WIDTH)
    qcn, kvn = _rowwise(n + "latent_norm", _f_latent, [zq, zkv], [w["g_qc"], w["g_kvc"]], [(QC_W, bf16), (LANE, bf16)])
    q = _mm(n + "uq", qcn, w["w_uq"])
    kv = _mm(n + "ukv", kvn, w["w_ukv"])
    kpart = (kv, HEADS * LANE, 0)
    qr, kk = _rowwise(n + "mla_prep", _f_mla_prep, [q, kpart, zkr, *rope], [],
                      [(HEADS * LANE, bf16), (HEADS * LANE, bf16)])
    mla_scale = (MLA_NOPE + MLA_ROPE) ** -0.5
    o_mla, lse_m, lse_m_row = _attn_fwd(n + "mla_fwd", (qr, 0), (kk, 0), (kv, HEADS), mla_scale)
    fl_t = z[:, Z_FL:Z_FL + SUBLANE].T
    c_t = _decay_fwd(n + "decay", fl_t, w["b_f8"])
    c_row, c_col = _key_decay(c_t, s_len)
    fox_scale = FOX_HEAD_DIM ** -0.5
    o_fox, lse_f, lse_f_row = _attn_fwd(n + "fox_fwd", (z, Z_FQ // LANE), (z, Z_FK // LANE), (z, Z_FV // LANE), fox_scale, c_row)
    xc = _conv_fwd(n + "lru_conv", zlx, w["lru_conv_w"], w["lru_conv_b"], LRU_CONV)
    gates = _mm(n + "lru_gates", xc, w["w_ri"])
    a, bx = _rowwise(n + "lru_gate", _f_lru_gate, [gates, xc], [w["b_r"], w["b_i"], w["lam"]],
                     [(LRU_WIDTH, f32), (LRU_WIDTH, f32)])
    hs = _scan_fwd(n + "lru_scan", a, bx)
    ocat, = _rowwise(n + "merge", _f_merge, [o_mla, o_fox, hs, zlg], [w["g_out"]], [(OMIX_W, bf16)])
    w.update(_prep_ffn_weights(weights_of("ffn", ocat)))
    h1 = _mm(n + "out_proj", ocat, w["w_o"], res=h0)
    xn2, = _rowwise(n + "norm_ffn", _f_norm, [h1], [w["g_ffn"]], [(D_MODEL, bf16)])
    up = _mm(n + "up_proj", xn2, w["w_up"])
    act = _ffn_act_fwd(n + "ffn_act", up, w["ffn_conv_w"], w["ffn_conv_b"])
    h2 = _mm(n + "down_proj", act, w["w_down"], res=h1)
    hn, = _rowwise(n + "norm_ple", _f_norm, [h2], [w["g_ple"]], [(D_MODEL, bf16)])
    gpre = _mm(n + "ple_gate", hn, w["w_ple_gate"])
    pp = _mm(n + "ple_proj", p_l, w["w_ple_proj"])
    h3, = _rowwise(n + "ple_mix", _f_ple, [h2, gpre, pp], [], [(D_MODEL, f32)])
    res = dict(h0=h0, xn=xn, z=z, qcn=qcn, kvn=kvn, q=q, kv=kv, qr=qr, kk=kk, o_mla=o_mla, lse_m=lse_m, fl_t=fl_t,
               lse_m_row=lse_m_row, lse_f_row=lse_f_row, c_row=c_row, c_col=c_col, o_fox=o_fox, lse_f=lse_f, xc=xc, gates=gates, a=a, hs=hs, ocat=ocat, h1=h1,
               xn2=xn2, up=up, act=act, h2=h2, hn=hn, gpre=gpre, pp=pp, p_l=p_l)
    return h3, res, w


def _layer_bwd(l, dh3, r, rope, w, token, grads_to):
    s_len = dh3.shape[0]
    n = f"l{l}_"
    g = {}
    w = dict(w, g_ple=w["g_ple"] + token)
    z = r["z"]
    zq = (z, QC_W, Z_QC // QC_W)
    zkv = (z, LANE, Z_KVC // LANE)
    zkr = (z, LANE, Z_KR // LANE)
    zlx = (z, LRU_WIDTH, Z_LX // LRU_WIDTH)
    zlg = (z, LRU_WIDTH, Z_LG // LRU_WIDTH)
    (dh2a, dgpre, dpp), _ = _rowwise_bwd(n + "ple_mix_b", _f_ple, [r["h2"], r["gpre"], r["pp"]], [], [dh3], 3,
                                         dts=[f32, bf16, bf16])
    g["w_ple_proj"] = _mm(n + "ple_proj_dw", r["p_l"], dpp, "tn", bf16)
    dhn = _mm(n + "ple_gate_dx", dgpre, w["w_ple_gate"], "nt")
    g["w_ple_gate"] = _mm(n + "ple_gate_dw", r["hn"], dgpre, "tn", bf16)
    (dh2,), (g["g_ple"],) = _rowwise_bwd(n + "norm_ple_b", _f_norm, [r["h2"]], [w["g_ple"]], [dhn], 1, adds={0: dh2a})
    dact = _mm(n + "down_dx", dh2, w["w_down"], "nt")
    g["w_down"] = _mm(n + "down_dw", r["act"], dh2, "tn", bf16)
    dup_g, dup_v, dcw_g, dcw_v, dcb_g, dcb_v = _ffn_act_bwd(n + "ffn_act_b", r["up"], dact, w["ffn_conv_w"], w["ffn_conv_b"])
    g["ffn_conv_w"] = jnp.concatenate([dcw_g, dcw_v], axis=1)
    g["ffn_conv_b"] = jnp.concatenate([dcb_g, dcb_v], axis=1)
    dxn2 = _mm(n + "up_dx_v", dup_v, w["w_up_v"], "nt", res=_mm(n + "up_dx_g", dup_g, w["w_up_g"], "nt"))
    g["w_up"] = jnp.concatenate([_mm(n + "up_dw_g", r["xn2"], dup_g, "tn", bf16),
                                 _mm(n + "up_dw_v", r["xn2"], dup_v, "tn", bf16)], axis=1)
    (dh1,), (g["g_ffn"],) = _rowwise_bwd(n + "norm_ffn_b", _f_norm, [r["h1"]], [w["g_ffn"]], [dxn2], 1, adds={0: dh2})
    docat = _mm(n + "out_dx", dh1, w["w_o"], "nt")
    g["w_o"] = _mm(n + "out_dw", r["ocat"], dh1, "tn", bf16)
    token = grads_to("ffn", dict(w_o=_take_inv(g["w_o"], OMIX_MAP, 0), w_up=g["w_up"], ffn_conv_w=g["ffn_conv_w"],
                                 w_down=g["w_down"], w_ple_gate=g["w_ple_gate"], w_ple_proj=g["w_ple_proj"]))
    w = dict(w, g_out=w["g_out"] + token)
    (do_mla, do_fox, dhs, dlg), (g["g_out"],) = _rowwise_bwd(
        n + "merge_b", _f_merge, [r["o_mla"], r["o_fox"], r["hs"], zlg], [w["g_out"]], [docat], 4)
    a, hs = r["a"], r["hs"]
    a_next = jnp.concatenate([a[1:], jnp.zeros((1, LRU_WIDTH), f32)], axis=0)
    h_prev = jnp.concatenate([jnp.zeros((1, LRU_WIDTH), f32), hs[:-1]], axis=0)
    da, dbx = _scan_bwd(n + "lru_scan_b", a_next, h_prev, dhs)
    (dgates, dxc_a), (g["b_r"], g["b_i"], g["lam"]) = _rowwise_bwd(
        n + "lru_gate_b", _f_lru_gate, [r["gates"], r["xc"]], [w["b_r"], w["b_i"], w["lam"]], [da, dbx], 2,
        dts=[bf16, f32])
    dxc_b = _mm(n + "lru_gates_dx", dgates, w["w_ri"], "nt")
    g["w_ri"] = _mm(n + "lru_gates_dw", r["xc"], dgates, "tn")
    dlx, g["lru_conv_w"], g["lru_conv_b"] = _conv_bwd(n + "lru_conv_b", zlx, dxc_a, w["lru_conv_w"], LRU_CONV, dout2=dxc_b)
    fox_scale = FOX_HEAD_DIM ** -0.5
    fq, fk, fv = (z, Z_FQ // LANE), (z, Z_FK // LANE), (z, Z_FV // LANE)
    dfq, delta_f, dc_q = _attn_dq(n + "fox_dq", fq, fk, fv, r["o_fox"], do_fox, r["lse_f"], fox_scale, r["c_row"])
    dfk, dfv, dc_k = _attn_dkv(n + "fox_dkv", fq, fk, fv, do_fox, r["lse_f_row"], delta_f, fox_scale,
                               r["c_col"])
    pad_rows = jnp.zeros((SUBLANE - HEADS, s_len), f32)
    dfl_t, g["b_f8"] = _decay_bwd(n + "decay_b", r["fl_t"], w["b_f8"],
                                  jnp.concatenate([dc_k.reshape(HEADS, s_len), pad_rows], axis=0),
                                  jnp.concatenate([dc_q.reshape(HEADS, s_len), pad_rows], axis=0))
    dfl = jnp.pad(dfl_t.T, ((0, 0), (0, LANE - SUBLANE)))
    mla_scale = (MLA_NOPE + MLA_ROPE) ** -0.5
    qr, kk, kv = (r["qr"], 0), (r["kk"], 0), (r["kv"], HEADS)
    dqr, delta_m, _ = _attn_dq(n + "mla_dq", qr, kk, kv, r["o_mla"], do_mla, r["lse_m"], mla_scale)
    dkk, dv_m = _attn_dkv(n + "mla_dkv", qr, kk, kv, do_mla, r["lse_m_row"], delta_m, mla_scale)
    (dq, dkpart, dkr), _ = _rowwise_bwd(n + "mla_prep_b", _f_mla_prep, [r["q"], (r["kv"], HEADS * LANE, 0), zkr, *rope],
                                        [], [dqr, dkk], 3, dts=[bf16, bf16, f32])
    dkv = jnp.concatenate([dkpart, dv_m.astype(bf16)], axis=1)
    dkvn = _mm(n + "ukv_dx", dkv, w["w_ukv"], "nt")
    g["w_ukv"] = _mm(n + "ukv_dw", r["kvn"], dkv, "tn", bf16)
    dqcn = _mm(n + "uq_dx", dq, w["w_uq"], "nt")
    g["w_uq"] = _mm(n + "uq_dw", r["qcn"], dq, "tn", bf16)
    (dqc, dkvc), (g["g_qc"], g["g_kvc"]) = _rowwise_bwd(n + "latent_norm_b", _f_latent, [zq, zkv],
                                                        [w["g_qc"], w["g_kvc"]], [dqcn, dkvn], 2)
    dz = jnp.concatenate([t.astype(bf16) for t in (dfq, dfk, dfv, dlx, dlg, dqc, dkvc, dkr, dfl)], axis=1)
    dxn = _mm(n + "in_dx", dz, w["w_in"], "nt")
    g["w_in"] = _mm(n + "in_dw", r["xn"], dz, "tn", bf16)
    (dh0,), (g["g_mix"],) = _rowwise_bwd(n + "norm_mix_b", _f_norm, [r["h0"]], [w["g_mix"]], [dxn], 1, adds={0: dh1})
    return dh0, grads_to("mix", _unpad_mix_grads(g))


def _unpad_mix_grads(g):
    d_ri = g["w_ri"]
    idx = jnp.arange(LRU_BLOCKS)

    def diag_blocks(m):
        return m.reshape(LRU_BLOCKS, LRU_BLOCK, LRU_BLOCKS, LRU_BLOCK)[idx, :, idx, :]

    return dict(
        g_mix=g["g_mix"][0], w_in=_take_inv(g["w_in"], Z_MAP, 1), g_qc=g["g_qc"][0, :MLA_Q_RANK],
        w_uq=_take_inv(g["w_uq"][:MLA_Q_RANK], UQ_COL_MAP, 1), g_kvc=g["g_kvc"][0],
        w_ukv=_take_inv(g["w_ukv"], UKV_MAP, 1), b_f=g["b_f8"][:FOX_HEADS, 0],
        lru_conv_w=g["lru_conv_w"], lru_conv_b=g["lru_conv_b"][0],
        w_r=diag_blocks(d_ri[:, :LRU_WIDTH]), b_r=g["b_r"][0], w_i=diag_blocks(d_ri[:, LRU_WIDTH:]), b_i=g["b_i"][0],
        lru_lambda=g["lam"][0], g_out=_take_inv(g["g_out"][0], OMIX_MAP, 0),
        g_ffn=g["g_ffn"][0], ffn_conv_b=g["ffn_conv_b"][0], g_ple=g["g_ple"][0],
    )


LAYER_WEIGHTS = ["g_mix", "w_in", "g_qc", "w_uq", "g_kvc", "w_ukv", "b_f", "lru_conv_w", "lru_conv_b", "w_r", "b_r", "w_i",
                 "b_i", "lru_lambda", "g_out", "w_o", "g_ffn", "w_up", "ffn_conv_w", "ffn_conv_b", "w_down", "g_ple",
                 "w_ple_gate", "w_ple_proj"]
WEIGHTS = LAYER_WEIGHTS + ["g_final"]


def _local_step(x, p, pos, target, g_final, weights_of, grads_to):
    h = x
    rope = _rope_rows(pos)
    ws, saved = [], []
    for l in range(DEPTH):
        h, r, w = _layer_fwd(l, h, p[l], rope, functools.partial(weights_of, l))
        ws.append(w)
        saved.append(r)
    loss, dh, dg_final = _loss_head("loss_head", h, target, g_final.reshape(1, -1))
    token = jnp.zeros((), f32)
    for l in reversed(range(DEPTH)):
        dh, token = _layer_bwd(l, dh, saved[l], rope, ws[l], token, functools.partial(grads_to, l))
    return loss[0, 0], dh, dg_final[0]


MESH_AXES = ("x", "y", "c")


def _row_tile(rows, cap):
    if rows <= cap:
        return rows
    for t in range(cap, SUBLANE - 1, -SUBLANE):
        if rows % t == 0:
            return t
    return rows


ADAM_BLOCK_BYTES = 2 ** 20


def _adamw(name, w, g, m, v):
    rows, cols = w.shape
    tr = _row_tile(rows, max(SUBLANE, ADAM_BLOCK_BYTES // (4 * cols) // SUBLANE * SUBLANE))

    def kern(w_ref, g_ref, m_ref, v_ref, d_ref, nm_ref, nv_ref):
        gv = g_ref[...]
        nm = ADAM_B1 * m_ref[...] + (1.0 - ADAM_B1) * gv
        nv = ADAM_B2 * v_ref[...] + (1.0 - ADAM_B2) * (gv * gv)
        m_hat = nm / (1.0 - ADAM_B1 ** ADAM_STEP)
        v_hat = nv / (1.0 - ADAM_B2 ** ADAM_STEP)
        d_ref[...] = -ADAM_LR * (m_hat / (jnp.sqrt(v_hat) + ADAM_EPS) + ADAM_WD * w_ref[...])
        nm_ref[...] = nm
        nv_ref[...] = nv

    spec = pl.BlockSpec((tr, cols), lambda i: (i, 0))
    return pl.pallas_call(
        kern, name=name, grid=(rows // tr,), in_specs=[spec] * 4, out_specs=[spec] * 3,
        out_shape=[jax.ShapeDtypeStruct((rows, cols), f32)] * 3,
        compiler_params=pltpu.CompilerParams(dimension_semantics=("parallel",)))(w, g, m, v)


def _packed_rows(shape):
    return -(-int(np.prod(shape)) // (SUBLANE * LANE)) * SUBLANE


def _pack(arrays):
    rows = []
    for a in arrays:
        flat = a.reshape(-1)
        rows.append(jnp.pad(flat, (0, _packed_rows(a.shape) * LANE - flat.shape[0])).reshape(-1, LANE))
    return jnp.concatenate(rows, axis=0)


def _unpack(buf, shapes):
    out, at = [], 0
    for s in shapes:
        rows = _packed_rows(s)
        out.append(buf[at:at + rows].reshape(-1)[:int(np.prod(s))].reshape(s))
        at += rows
    return out


SHARD_AXIS = {"w_in": 2, "w_uq": 2, "w_ukv": 2, "lru_conv_w": 2, "w_o": 1, "w_up": 2, "ffn_conv_w": 2, "w_down": 1,
              "w_ple_gate": 1, "w_ple_proj": 2}
SHARDED = [k for k in WEIGHTS if k in SHARD_AXIS]
REPLICATED = [k for k in WEIGHTS if k not in SHARD_AXIS]
ELEMENTWISE_F32 = ("lru_conv_w", "ffn_conv_w")
N_SHARDS = 4
BF16_TILE_ROWS = 16


HBM_SPEC = pl.BlockSpec(memory_space=pl.ANY)
SEM_SPEC = pl.BlockSpec(memory_space=pltpu.SEMAPHORE)
SPLIT_EFFECT = pltpu.SideEffectType.DATAFLOW_SIDE_EFFECTING
CHIP_FLIPS = ((1, 0), (0, 1), (1, 1))
N_DEVICES = 8
SUM_BLOCK_BYTES = 4 * 2 ** 20


def _device_index():
    return 4 * lax.axis_index("x") + 2 * lax.axis_index("y") + lax.axis_index("c")


def _when(cond, fn):
    if cond is None:
        fn()
    else:
        pl.when(cond)(fn)


class _Exchange:
    def __init__(self, name, plan, srcs, land_shapes, n_send, n_recv):
        self.name, self.plan, self.srcs, self.n = name, plan, list(srcs), len(srcs)
        self.land_shapes, self.n_send, self.n_recv = land_shapes, n_send, n_recv

    def run(self):
        n = self.n

        def body(*refs):
            sends, arrivals = self.plan(refs[:n], refs[n:2 * n], refs[2 * n], refs[2 * n + 1])
            for cond, cp in sends:
                _when(cond, cp.start)
            for cond, cp in arrivals:
                _when(cond, cp.wait_recv)
            for cond, cp in sends:
                _when(cond, cp.wait_send)

        return pl.pallas_call(
            body, name=self.name, out_shape=self.land_shapes, in_specs=[HBM_SPEC] * n, out_specs=[HBM_SPEC] * n,
            scratch_shapes=[pltpu.SemaphoreType.DMA((self.n_send,)), pltpu.SemaphoreType.DMA((self.n_recv,))])(*self.srcs)

    def start(self, after=None):
        n = self.n
        lands = [lax.empty(s.shape, s.dtype) for s in self.land_shapes]
        extra = [] if after is None else [after]

        def body(*refs):
            ins, lands_in = refs[:n], refs[n:2 * n]
            send_sems, recv_sems, token = refs[2 * n + len(extra)], refs[2 * n + len(extra) + 1], refs[-1]
            sends, _ = self.plan(ins, lands_in, send_sems, recv_sems)
            for cond, cp in sends:
                _when(cond, cp.start)
            token[...] = jnp.zeros_like(token)

        hbm = [pltpu.with_memory_space_constraint(a, pltpu.HBM) for a in self.srcs + lands]
        res = pl.pallas_call(
            body, name=self.name + "_start",
            out_shape=(pltpu.SemaphoreType.DMA((self.n_send,)), pltpu.SemaphoreType.DMA((self.n_recv,)),
                       *[pltpu.HBM(a.shape, a.dtype) for a in hbm], jax.ShapeDtypeStruct((SUBLANE, LANE), f32)),
            in_specs=[HBM_SPEC] * (2 * n + len(extra)),
            out_specs=(SEM_SPEC, SEM_SPEC, *[HBM_SPEC] * (2 * n), pl.BlockSpec(memory_space=pltpu.VMEM)),
            input_output_aliases={i: 2 + i for i in range(2 * n)},
            compiler_params=pltpu.CompilerParams(has_side_effects=SPLIT_EFFECT))(*hbm, *extra)
        self.sems, self.thru, token = res[:2], res[2:2 + 2 * n], res[-1]
        return token[0, 0]

    def finish(self, after):
        n = self.n

        def body(*refs):
            ins, lands_in, send_sems, recv_sems = refs[:n], refs[n:2 * n], refs[2 * n], refs[2 * n + 1]
            sends, arrivals = self.plan(ins, lands_in, send_sems, recv_sems)
            for cond, cp in arrivals:
                _when(cond, cp.wait_recv)
            for cond, cp in sends:
                _when(cond, cp.wait_send)

        res = pl.pallas_call(
            body, name=self.name + "_finish", out_shape=tuple(pltpu.HBM(a.shape, a.dtype) for a in self.thru),
            in_specs=[HBM_SPEC] * (2 * n) + [SEM_SPEC, SEM_SPEC, HBM_SPEC], out_specs=tuple([HBM_SPEC] * (2 * n)),
            input_output_aliases={i: i for i in range(2 * n)},
            compiler_params=pltpu.CompilerParams(has_side_effects=SPLIT_EFFECT))(*self.thru, *self.sems, after)
        return list(res[n:])


def _gather_exchange(name, shards):
    def plan(ins, lands, send_sems, recv_sems):
        x, y, c = (lax.axis_index(a) for a in MESH_AXES)
        copies = []
        for i in range(len(ins)):
            for k, (fx, fy) in enumerate(CHIP_FLIPS):
                peer = (1 - x if fx else x, 1 - y if fy else y, c)
                copies.append((None, pltpu.make_async_remote_copy(
                    src_ref=ins[i], dst_ref=lands[i].at[2 * x + y], send_sem=send_sems.at[3 * i + k],
                    recv_sem=recv_sems.at[3 * i + k], device_id=peer, device_id_type=pl.DeviceIdType.MESH)))
        return copies, copies

    n = len(shards)
    return _Exchange(name, plan, shards, [jax.ShapeDtypeStruct((N_SHARDS,) + s.shape, s.dtype) for s in shards], 3 * n, 3 * n)


def _scatter_exchange(name, layer, chunks):
    def plan(ins, lands, send_sems, recv_sems):
        x, y, c = (lax.axis_index(a) for a in MESH_AXES)
        me = _device_index()
        sends, arrivals = [], []
        for i in range(len(ins)):
            for j in range(N_SHARDS):
                target = (j // 2, j % 2, layer)
                remote = jnp.logical_not((x == target[0]) & (y == target[1]) & (c == layer))
                sends.append((remote, pltpu.make_async_remote_copy(
                    src_ref=ins[i].at[j], dst_ref=lands[i].at[me], send_sem=send_sems.at[N_SHARDS * i + j],
                    recv_sem=recv_sems.at[N_DEVICES * i + me], device_id=target, device_id_type=pl.DeviceIdType.MESH)))
            for s in range(N_DEVICES):
                arrivals.append(((c == layer) & (me != s), pltpu.make_async_remote_copy(
                    src_ref=ins[i].at[0], dst_ref=lands[i].at[s], send_sem=send_sems.at[0],
                    recv_sem=recv_sems.at[N_DEVICES * i + s], device_id=(x, y, c), device_id_type=pl.DeviceIdType.MESH)))
        return sends, arrivals

    n = len(chunks)
    lands = [jax.ShapeDtypeStruct((N_DEVICES,) + ch.shape[1:], ch.dtype) for ch in chunks]
    return _Exchange(name, plan, chunks, lands, N_SHARDS * n, N_DEVICES * n)


def _sum_contributions(name, got, mine):
    _, a, b = got.shape
    ta = _row_tile(a, max(SUBLANE, SUM_BLOCK_BYTES // (N_DEVICES * b * got.dtype.itemsize) // SUBLANE * SUBLANE))

    def kern(got_ref, mine_ref, o_ref):
        me = _device_index()
        acc = jnp.zeros(o_ref.shape, f32)
        for s in range(N_DEVICES):
            acc = acc + jnp.where(me == s, mine_ref[...].astype(f32), got_ref[s].astype(f32))
        o_ref[...] = acc

    return pl.pallas_call(
        kern, name=name, grid=(a // ta,),
        in_specs=[pl.BlockSpec((N_DEVICES, ta, b), lambda i: (0, i, 0)), pl.BlockSpec((ta, b), lambda i: (i, 0))],
        out_specs=pl.BlockSpec((ta, b), lambda i: (i, 0)), out_shape=jax.ShapeDtypeStruct((a, b), f32),
        compiler_params=pltpu.CompilerParams(dimension_semantics=("parallel",)))(got, mine)


def _swap_layers(name, sums):
    n = len(sums[0])

    def body(*refs):
        srcs = (refs[:n], refs[n:2 * n])
        outs, (send_sems, recv_sems) = refs[2 * n:3 * n], refs[3 * n:]
        x, y, c = (lax.axis_index(a) for a in MESH_AXES)
        for i in range(n):
            for layer in range(DEPTH):
                cp = pltpu.make_async_remote_copy(
                    src_ref=srcs[layer][i], dst_ref=outs[i], send_sem=send_sems.at[i], recv_sem=recv_sems.at[i],
                    device_id=(x, y, 1 - c), device_id_type=pl.DeviceIdType.MESH)
                pl.when(c == layer)(cp.start)
        for i in range(n):
            pltpu.make_async_remote_copy(
                src_ref=srcs[0][i], dst_ref=outs[i], send_sem=send_sems.at[i], recv_sem=recv_sems.at[i],
                device_id=(x, y, 1 - c), device_id_type=pl.DeviceIdType.MESH).wait()

    return pl.pallas_call(
        body, name=name, out_shape=[jax.ShapeDtypeStruct(s.shape, s.dtype) for s in sums[0]],
        in_specs=[HBM_SPEC] * (2 * n), out_specs=[HBM_SPEC] * n,
        scratch_shapes=[pltpu.SemaphoreType.DMA((n,)), pltpu.SemaphoreType.DMA((n,))])(*sums[0], *sums[1])


def _stack_shards(g, axis):
    if axis == 1:
        return g.reshape(N_SHARDS, g.shape[0] // N_SHARDS, g.shape[1])
    return g.reshape(g.shape[0], N_SHARDS, g.shape[1] // N_SHARDS).transpose(1, 0, 2)


def _join_shards(s, axis):
    if axis == 1:
        return s.reshape(-1, s.shape[2])
    return s.transpose(1, 0, 2).reshape(s.shape[1], -1)


def _layer_shards(w, l, names):
    return [w[k][l] if k in ELEMENTWISE_F32 else w[k][l].astype(bf16) for k in names]


def _full_weights(names, sent, got):
    j = 2 * lax.axis_index("x") + lax.axis_index("y")
    return {k: _join_shards(lax.dynamic_update_slice(g, own[None], (j, 0, 0)), SHARD_AXIS[k])
            for k, own, g in zip(names, sent, got)}


def _grad_chunks(grads, names):
    return [_stack_shards(grads[k], SHARD_AXIS[k]).astype(bf16) for k in names]


def _sum_group(l, names, got, chunks):
    j = 2 * lax.axis_index("x") + lax.axis_index("y")
    return {k: _sum_contributions(f"sum_l{l}_{k}", g, lax.dynamic_index_in_dim(ch, j, 0, keepdims=False))
            for k, g, ch in zip(names, got, chunks)}


def _both_layers(name, names, sums):
    c = lax.axis_index("c")
    mine = [[sums[l][k] for k in names] for l in range(DEPTH)]
    other = _swap_layers(name, mine)
    return {k: jnp.stack([jnp.where(c == 0, mine[0][i], other[i]), jnp.where(c == 0, other[i], mine[1][i])])
            for i, k in enumerate(names)}


def _gather_all_exchange(name, src):
    def plan(ins, lands, send_sems, recv_sems):
        coords = [lax.axis_index(a) for a in MESH_AXES]
        me = _device_index()
        sends, arrivals = [], []
        for f in range(1, N_DEVICES):
            peer = tuple(1 - cd if (f >> (2 - b)) & 1 else cd for b, cd in enumerate(coords))
            sends.append((None, pltpu.make_async_remote_copy(
                src_ref=ins[0], dst_ref=lands[0].at[me], send_sem=send_sems.at[f - 1], recv_sem=recv_sems.at[me],
                device_id=peer, device_id_type=pl.DeviceIdType.MESH)))
        for s in range(N_DEVICES):
            arrivals.append((me != s, pltpu.make_async_remote_copy(
                src_ref=ins[0], dst_ref=lands[0].at[s], send_sem=send_sems.at[0], recv_sem=recv_sems.at[s],
                device_id=tuple(coords), device_id_type=pl.DeviceIdType.MESH)))
        return sends, arrivals

    return _Exchange(name, plan, [src], [jax.ShapeDtypeStruct((N_DEVICES,) + src.shape, src.dtype)], N_DEVICES - 1, N_DEVICES)


def kernel(x, p, positions, g_mix, w_in, g_qc, w_uq, g_kvc, w_ukv, b_f, lru_conv_w, lru_conv_b, w_r, b_r, w_i, b_i, lru_lambda, g_out, w_o, g_ffn, w_up, ffn_conv_w, ffn_conv_b, w_down, g_ple, w_ple_gate, w_ple_proj, g_final, loss_target, m_g_mix, m_w_in, m_g_qc, m_w_uq, m_g_kvc, m_w_ukv, m_b_f, m_lru_conv_w, m_lru_conv_b, m_w_r, m_b_r, m_w_i, m_b_i, m_lru_lambda, m_g_out, m_w_o, m_g_ffn, m_w_up, m_ffn_conv_w, m_ffn_conv_b, m_w_down, m_g_ple, m_w_ple_gate, m_w_ple_proj, m_g_final, v_g_mix, v_w_in, v_g_qc, v_w_uq, v_g_kvc, v_w_ukv, v_b_f, v_lru_conv_w, v_lru_conv_b, v_w_r, v_b_r, v_w_i, v_b_i, v_lru_lambda, v_g_out, v_w_o, v_g_ffn, v_w_up, v_ffn_conv_w, v_ffn_conv_b, v_w_down, v_g_ple, v_w_ple_gate, v_w_ple_proj, v_g_final):
    given = locals()
    w = {k: given[k] for k in WEIGHTS}
    m = {k: given["m_" + k] for k in WEIGHTS}
    v = {k: given["v_" + k] for k in WEIGHTS}

    parts = {"mix": MIX_PART, "ffn": FFN_PART}
    groups = [(l, part) for l in range(DEPTH) for part in ("mix", "ffn")]
    sent = {g: _layer_shards(w, g[0], parts[g[1]]) for g in groups}
    first = _gather_exchange("gather_l0_mix", sent[groups[0]]).run()
    ahead = {g: _gather_exchange(f"gather_l{g[0]}_{g[1]}", sent[g]) for g in groups[1:]}
    pos = positions[0].astype(f32).reshape(-1, 1)
    for ex in ahead.values():
        pos = pos + ex.start(after=first[0])
    behind, layer_grads, chunks = {}, [{} for _ in range(DEPTH)], {}

    def weights_of(l, part, after):
        g = (l, part)
        full = _full_weights(parts[part], sent[g], first if g == groups[0] else ahead[g].finish(after=after))
        if part == "mix":
            full.update({k: w[k][l] for k in LAYER_WEIGHTS if k in REPLICATED})
        return full

    def grads_to(l, part, grads):
        g = (l, part)
        layer_grads[l].update(grads)
        chunks[g] = _grad_chunks(grads, parts[part])
        if g == groups[0]:
            return jnp.zeros((), f32)
        behind[g] = _scatter_exchange(f"scatter_l{l}_{part}", l, chunks[g])
        return behind[g].start()

    loss, dx, dg_final = _local_step(x[0], p[:, 0], pos, loss_target[0], w["g_final"], weights_of, grads_to)

    grads = {k: jnp.stack([layer_grads[l][k] for l in range(DEPTH)]) for k in LAYER_WEIGHTS if k in REPLICATED}
    grads["g_final"] = dg_final
    rep_shapes = [w[k].shape for k in REPLICATED] + [(1,)]
    contrib = _pack([grads[k] for k in REPLICATED] + [loss.reshape(1)])
    last = _scatter_exchange("scatter_l0_mix", 0, chunks[groups[0]])
    everyone = _gather_all_exchange("gather_replicated", contrib)
    started = (last.start() + everyone.start() + dx[0, 0]).reshape(1, 1)

    def adamw_of(names, g_sharded):
        out = {}
        for k in names:
            shape = w[k].shape
            flat = [t.reshape(-1, shape[-1]) for t in (w[k], g_sharded[k], m[k], v[k])]
            out[k] = [t.reshape(shape) for t in (flat[1],) + tuple(_adamw("adamw_" + k, *flat))]
        return out

    sums = [{} for _ in range(DEPTH)]
    for g in groups[1:]:
        sums[g[0]].update(_sum_group(g[0], parts[g[1]], behind[g].finish(after=started), chunks[g]))
    big = adamw_of(FFN_PART, _both_layers("swap_ffn", FFN_PART, sums))
    sums[0].update(_sum_group(0, MIX_PART, last.finish(after=big[FFN_PART[0]][1]), chunks[groups[0]]))
    big.update(adamw_of(MIX_PART, _both_layers("swap_mix", MIX_PART, sums)))

    g_rep = _sum_contributions("sum_replicated", everyone.finish(after=big[MIX_PART[0]][1])[0], contrib)
    zero = jnp.zeros((1,), f32)
    w_rep, m_rep, v_rep = (_pack([t[k] for k in REPLICATED] + [zero]) for t in (w, m, v))
    rep = [_unpack(b, rep_shapes) for b in (g_rep,) + tuple(_adamw("adamw_replicated", w_rep, g_rep, m_rep, v_rep))]

    outs = []
    for kind in range(4):
        by_name = {k: big[k][kind] for k in SHARDED}
        by_name.update(zip(REPLICATED, rep[kind][:-1]))
        outs.append([by_name[k] for k in WEIGHTS])
    total_loss = rep[0][-1][0]
    return (total_loss, dx.reshape(x.shape), *outs[0], *outs[1], *outs[2], *outs[3])
```

```python
import functools
import math

import numpy as np
import jax
import jax.numpy as jnp
from jax import lax
from jax.experimental import pallas as pl
from jax.experimental.pallas import tpu as pltpu

f32, bf16 = jnp.float32, jnp.bfloat16

D_MODEL = 1024
PLE_DIM = 256
MLA_HEADS, MLA_NOPE, MLA_ROPE, MLA_V = 4, 64, 32, 64
MLA_Q_RANK, MLA_KV_RANK = 192, 128
FOX_HEADS, FOX_HEAD_DIM = 4, 64
LRU_WIDTH, LRU_BLOCKS, LRU_BLOCK, LRU_CONV, LRU_C = 512, 8, 64, 4, 8.0
D_FF, FFN_CONV = 2816, 3
ROPE_THETA = 10000.0
EPS = 1e-6
DEPTH = 2
ADAM_LR, ADAM_B1, ADAM_B2, ADAM_EPS, ADAM_WD, ADAM_STEP = 0.001, 0.9, 0.999, 1e-08, 0.01, 10

LANE = 128
SUBLANE = 8
HEADS = 4

Z_FQ, Z_FK, Z_FV, Z_LX, Z_LG, Z_QC, Z_KVC, Z_KR, Z_FL, Z_W = 0, 512, 1024, 1536, 2048, 2560, 2816, 2944, 3072, 3200
QC_W = 256
ROPE_AT = 64


def _head_pad_map(n_heads, width):
    m = -np.ones(n_heads * LANE, np.int64)
    for h in range(n_heads):
        m[h * LANE:h * LANE + width] = h * width + np.arange(width)
    return m


def _z_map():
    m = -np.ones(Z_W, np.int64)
    o_qc, o_kvc, o_kr = 0, MLA_Q_RANK, MLA_Q_RANK + MLA_KV_RANK
    o_fq = o_kr + MLA_ROPE
    o_fk, o_fv = o_fq + 256, o_fq + 512
    o_fl = o_fv + 256
    o_lx = o_fl + FOX_HEADS
    o_lg = o_lx + LRU_WIDTH
    m[Z_FQ:Z_FQ + 512] = np.where(_head_pad_map(4, 64) >= 0, _head_pad_map(4, 64) + o_fq, -1)
    m[Z_FK:Z_FK + 512] = np.where(_head_pad_map(4, 64) >= 0, _head_pad_map(4, 64) + o_fk, -1)
    m[Z_FV:Z_FV + 512] = np.where(_head_pad_map(4, 64) >= 0, _head_pad_map(4, 64) + o_fv, -1)
    m[Z_LX:Z_LX + 512] = o_lx + np.arange(512)
    m[Z_LG:Z_LG + 512] = o_lg + np.arange(512)
    m[Z_QC:Z_QC + MLA_Q_RANK] = o_qc + np.arange(MLA_Q_RANK)
    m[Z_KVC:Z_KVC + MLA_KV_RANK] = o_kvc + np.arange(MLA_KV_RANK)
    m[Z_KR + ROPE_AT:Z_KR + ROPE_AT + MLA_ROPE] = o_kr + np.arange(MLA_ROPE)
    m[Z_FL:Z_FL + FOX_HEADS] = o_fl + np.arange(FOX_HEADS)
    return m


def _ukv_map():
    m = -np.ones(2 * HEADS * LANE, np.int64)
    for h in range(HEADS):
        m[h * LANE:h * LANE + MLA_NOPE] = h * (MLA_NOPE + MLA_V) + np.arange(MLA_NOPE)
        m[HEADS * LANE + h * LANE:HEADS * LANE + h * LANE + MLA_V] = h * (MLA_NOPE + MLA_V) + MLA_NOPE + np.arange(MLA_V)
    return m


def _omix_map():
    return np.concatenate([_head_pad_map(4, 64), np.where(_head_pad_map(4, 64) >= 0, _head_pad_map(4, 64) + 256, -1),
                           512 + np.arange(512)])


def _pad_to(m, n):
    return np.concatenate([m, -np.ones(n - m.shape[0], np.int64)])


def _runs(m):
    out, at = [], 0
    while at < len(m):
        end = at + 1
        while end < len(m) and (m[end] == m[end - 1] + 1 if m[at] >= 0 else m[end] < 0):
            end += 1
        out.append((int(m[at]), end - at))
        at = end
    return out


def _take_runs(a, m, axis):
    parts = []
    for start, size in _runs(m):
        if start < 0:
            shape = list(a.shape)
            shape[axis] = size
            parts.append(jnp.zeros(shape, a.dtype))
        else:
            parts.append(lax.slice_in_dim(a, start, start + size, axis=axis))
    return parts[0] if len(parts) == 1 else jnp.concatenate(parts, axis=axis)


def _take_pad(a, m, axis):
    return _take_runs(a, m, axis)


def _take_inv(a, m, axis):
    n = int(m.max()) + 1
    inv = np.zeros(n, np.int64)
    inv[m[m >= 0]] = np.nonzero(m >= 0)[0]
    return _take_runs(a, inv, axis)


Z_MAP = _z_map()
UQ_COL_MAP = _head_pad_map(HEADS, MLA_NOPE + MLA_ROPE)
UQ_ROW_MAP = _pad_to(np.arange(MLA_Q_RANK), QC_W)
UKV_MAP = _ukv_map()
OMIX_MAP = _omix_map()
OMIX_W = 1536


def _rope_tables(width, at):
    half = MLA_ROPE // 2
    inv = ROPE_THETA ** (-np.arange(half, dtype=np.float32) / half)
    freq = np.zeros((1, width), np.float32)
    m1 = np.zeros((1, width), np.float32)
    m2 = np.zeros((1, width), np.float32)
    for h in range(width // LANE):
        b = h * LANE + at
        freq[0, b:b + half] = inv
        freq[0, b + half:b + 2 * half] = inv
        m1[0, b:b + half] = 1.0
        m2[0, b + half:b + 2 * half] = 1.0
    return freq, m1, m2


def _view(r):
    return r if isinstance(r, tuple) else (r, r.shape[1], 0)


def _blk(dim, cap):
    if dim <= cap:
        return dim
    for b in range(cap, LANE - 1, -LANE):
        if dim % b == 0:
            return b
    return dim


@functools.partial(jax.custom_vjp, nondiff_argnums=(1, 2))
def _roll(x, shift, axis):
    return pltpu.roll(x, shift, axis)


def _roll_fwd(x, shift, axis):
    return pltpu.roll(x, shift, axis), None


def _roll_bwd(shift, axis, _, g):
    return (pltpu.roll(g, g.shape[axis] - shift, axis),)


_roll.defvjp(_roll_fwd, _roll_bwd)


def _rowwise(name, fn, rows, pars, outs, tb=256):
    rows = [_view(r) for r in rows]
    n = rows[0][0].shape[0]
    tb = min(tb, n)
    nr, npar = len(rows), len(pars)

    def kern(*refs):
        r = [refs[k][...].astype(f32) for k in range(nr)]
        p = [refs[nr + k][...] for k in range(npar)]
        res = fn(*r, *p)
        for o_ref, o in zip(refs[nr + npar:], res):
            o_ref[...] = o.astype(o_ref.dtype)

    in_specs = [pl.BlockSpec((tb, w), lambda i, j=idx: (i, j)) for (_, w, idx) in rows]
    in_specs += [pl.BlockSpec(p.shape, lambda i: (0, 0)) for p in pars]
    out_specs = [pl.BlockSpec((tb, w), lambda i: (i, 0)) for (w, _) in outs]
    out_shape = [jax.ShapeDtypeStruct((n, w), dt) for (w, dt) in outs]
    return pl.pallas_call(kern, name=name, grid=(n // tb,), in_specs=in_specs, out_specs=out_specs, out_shape=out_shape,
                          compiler_params=pltpu.CompilerParams(dimension_semantics=("parallel",)))(*[r[0] for r in rows], *pars)


def _rowwise_bwd(name, fn, rows, pars, cts, ndiff, adds=None, tb=256, dts=None):
    rows = [_view(r) for r in rows]
    dts = dts or [f32] * ndiff
    adds = adds or {}
    add_keys = sorted(adds)
    n = rows[0][0].shape[0]
    tb = min(tb, n)
    nr, npar, nct, nadd = len(rows), len(pars), len(cts), len(add_keys)

    def kern(*refs):
        i = pl.program_id(0)
        r = [refs[k][...].astype(f32) for k in range(nr)]
        p = [refs[nr + k][...] for k in range(npar)]
        ct = [refs[nr + npar + k][...].astype(f32) for k in range(nct)]
        ad = {key: refs[nr + npar + nct + k][...] for k, key in enumerate(add_keys)}
        o_refs = refs[nr + npar + nct + nadd:]

        def g(*d):
            return tuple(fn(*d[:ndiff], *r[ndiff:], *d[ndiff:]))

        _, vjp = jax.vjp(g, *r[:ndiff], *p)
        grads = vjp(tuple(ct))
        for k in range(ndiff):
            gk = grads[k]
            if k in ad:
                gk = gk + ad[k]
            o_refs[k][...] = gk.astype(o_refs[k].dtype)

        @pl.when(i == 0)
        def _():
            for k in range(npar):
                o_refs[ndiff + k][...] = jnp.zeros_like(o_refs[ndiff + k])

        for k in range(npar):
            o_refs[ndiff + k][...] += grads[ndiff + k]

    in_specs = [pl.BlockSpec((tb, w), lambda i, j=idx: (i, j)) for (_, w, idx) in rows]
    in_specs += [pl.BlockSpec(p.shape, lambda i: (0, 0)) for p in pars]
    in_specs += [pl.BlockSpec((tb, c.shape[1]), lambda i: (i, 0)) for c in cts]
    in_specs += [pl.BlockSpec((tb, adds[k].shape[1]), lambda i: (i, 0)) for k in add_keys]
    out_specs = [pl.BlockSpec((tb, rows[k][1]), lambda i: (i, 0)) for k in range(ndiff)]
    out_specs += [pl.BlockSpec(p.shape, lambda i: (0, 0)) for p in pars]
    out_shape = [jax.ShapeDtypeStruct((n, rows[k][1]), dts[k]) for k in range(ndiff)]
    out_shape += [jax.ShapeDtypeStruct(p.shape, f32) for p in pars]
    res = pl.pallas_call(kern, name=name, grid=(n // tb,), in_specs=in_specs, out_specs=out_specs, out_shape=out_shape,
                         compiler_params=pltpu.CompilerParams(dimension_semantics=("arbitrary",)))(
        *[r[0] for r in rows], *pars, *cts, *[adds[k] for k in add_keys])
    return res[:ndiff], res[ndiff:]


_DOT_DIMS = {"nn": ((1,), (0,)), "nt": ((1,), (1,)), "tn": ((0,), (0,))}

MM_VMEM_BUDGET = 36 * 2 ** 20
MM_MAX_TM = 1408
MM_STEP, MM_RESULT, MM_XPOSE, MM_CAST = 700.0, 7.5e-4, 9e-4, 1e-3


def _tile_candidates(dim):
    c = [d for d in range(LANE, dim + 1, LANE) if dim % d == 0]
    return c or [dim]


@functools.lru_cache(maxsize=None)
def _mm_tiles(mode, m, n, k, a_bytes, b_bytes, o_bytes):
    best, best_cost = None, None
    for tm in _tile_candidates(m):
        if tm > MM_MAX_TM:
            continue
        for tn in _tile_candidates(n):
            for tk in _tile_candidates(k):
                vmem = 2 * (tm * tk * a_bytes + tk * tn * b_bytes + tm * tn * o_bytes) + 4 * tm * tn * (2 if tk < k else 1)
                vmem += (2 * tm * tk if a_bytes > 2 else 0) + (2 * tk * tn if b_bytes > 2 else 0)
                if vmem > MM_VMEM_BUDGET:
                    continue
                steps = (m // tm) * (n // tn) * (k // tk)
                cost = steps * MM_STEP + m * n * (k // tk) * MM_RESULT
                if mode == "tn":
                    cost += m * k * (n // tn) * MM_XPOSE
                cost += (m * k * (n // tn) * MM_CAST if a_bytes > 2 else 0) + (k * n * (m // tm) * MM_CAST if b_bytes > 2 else 0)
                if best is None or cost < best_cost:
                    best, best_cost = (tm, tn, tk), cost
    return best


def _mm(name, a, b, mode="nn", out_dtype=f32, res=None):
    if mode == "nn":
        (m, k), (_, n) = a.shape, b.shape
    elif mode == "nt":
        (m, k), (n, _) = a.shape, b.shape
    else:
        (k, m), (_, n) = a.shape, b.shape
    has_res = res is not None
    tm, tn, tk = _mm_tiles(mode, m, n, k, a.dtype.itemsize, b.dtype.itemsize,
                           jnp.dtype(out_dtype).itemsize + (res.dtype.itemsize if has_res else 0))
    nk = k // tk
    dims = (_DOT_DIMS[mode], ((), ()))

    def kern(*refs):
        a_ref, b_ref = refs[0], refs[1]
        o_ref, acc_ref = refs[-2], refs[-1]
        kk = pl.program_id(2)
        part = lax.dot_general(a_ref[...].astype(bf16), b_ref[...].astype(bf16), dims, preferred_element_type=f32)

        def finish(out):
            if has_res:
                out = out + refs[2][...]
            o_ref[...] = out.astype(o_ref.dtype)

        if nk == 1:
            finish(part)
            return

        @pl.when(kk == 0)
        def _():
            acc_ref[...] = part

        @pl.when(jnp.logical_and(kk > 0, kk < nk - 1))
        def _():
            acc_ref[...] += part

        @pl.when(kk == nk - 1)
        def _():
            finish(acc_ref[...] + part)

    if mode == "tn":
        a_spec = pl.BlockSpec((tk, tm), lambda i, j, kk: (kk, i))
    else:
        a_spec = pl.BlockSpec((tm, tk), lambda i, j, kk: (i, kk))
    if mode == "nt":
        b_spec = pl.BlockSpec((tn, tk), lambda i, j, kk: (j, kk))
    else:
        b_spec = pl.BlockSpec((tk, tn), lambda i, j, kk: (kk, j))
    in_specs = [a_spec, b_spec]
    args = [a, b]
    if has_res:
        in_specs.append(pl.BlockSpec((tm, tn), lambda i, j, kk: (i, j)))
        args.append(res)
    return pl.pallas_call(
        kern, name=name, grid=(m // tm, n // tn, nk), in_specs=in_specs,
        out_specs=pl.BlockSpec((tm, tn), lambda i, j, kk: (i, j)),
        out_shape=jax.ShapeDtypeStruct((m, n), out_dtype),
        scratch_shapes=[pltpu.VMEM((tm, tn) if nk > 1 else (SUBLANE, LANE), f32)],
        compiler_params=pltpu.CompilerParams(dimension_semantics=("parallel", "parallel", "arbitrary")))(*args)


ATT_TQ, ATT_TK = 512, 512


def _att_tiles(s_len):
    tk = min(ATT_TK, s_len)
    return min(ATT_TQ, tk), tk


def _fold_scale(scale):
    return (scale, 1.0) if math.frexp(scale)[0] == 0.5 else (1.0, scale)


def _as_row(col):
    return jnp.max(jnp.broadcast_to(col, (col.shape[0], LANE)).T[:SUBLANE], axis=0, keepdims=True)


def _scores_t(kb, q_t, s_mul, ck, diag_offset, tq, tk):
    s = jnp.dot(kb, q_t, preferred_element_type=f32)
    if s_mul != 1.0:
        s = s * s_mul
    if ck is not None:
        s = s - ck
    if diag_offset is None:
        return s
    key = lax.broadcasted_iota(jnp.int32, (tk, tq), 0)
    query = lax.broadcasted_iota(jnp.int32, (tk, tq), 1) + diag_offset
    return jnp.where(key <= query, s, -jnp.inf)


ATT_ROWS = 64


def _finish_scores(s, s_mul, ck, first_row):
    if s_mul != 1.0:
        s = s * s_mul
    if ck is not None:
        s = s - ck
    if first_row is None:
        return s
    row = lax.broadcasted_iota(jnp.int32, s.shape, 0) + first_row
    col = lax.broadcasted_iota(jnp.int32, s.shape, 1)
    return jnp.where(col <= row, s, -jnp.inf)


def _attn_fwd(name, q, k, v, scale, c_row=None):
    (qa, qo), (ka, ko), (va, vo) = q, k, v
    s_len = qa.shape[0]
    t = _att_tiles(s_len)[1]
    nt = s_len // t
    decay = c_row is not None
    q_mul, s_mul = _fold_scale(scale)

    def kern(*refs):
        q_ref, k_ref, v_ref = refs[:3]
        o_ref, lse_ref, lse_row_ref = refs[-3:]
        i = pl.program_id(1)
        qb = (q_ref[...] * q_mul).astype(bf16)

        def step(j, carry, diagonal):
            m, l, acc = carry
            rows = pl.ds(pl.multiple_of(j * t, t), t)
            kb = k_ref[rows, :].astype(bf16)
            vb = v_ref[rows, :].astype(bf16)
            s = lax.dot_general(qb, kb, (_DOT_DIMS["nt"], ((), ())), preferred_element_type=f32)
            s = _finish_scores(s, s_mul, refs[3][j] if decay else None, 0 if diagonal else None)
            m_new = jnp.maximum(m, jnp.max(s, axis=1, keepdims=True))
            alpha = jnp.exp(m - m_new)
            p = jnp.exp(s - m_new)
            l = alpha * l + jnp.sum(p, axis=1, keepdims=True)
            acc = alpha * acc + jnp.dot(p.astype(bf16), vb, preferred_element_type=f32)
            return m_new, l, acc

        init = (jnp.full((t, 1), -jnp.inf, f32), jnp.zeros((t, 1), f32), jnp.zeros((t, LANE), f32))
        m, l, acc = step(i, lax.fori_loop(0, i, lambda j, c: step(j, c, False), init), True)
        o_ref[...] = acc / l
        lse = m + jnp.log(l)
        lse_ref[...] = lse
        lse_row_ref[...] = _as_row(lse)

    in_specs = [pl.BlockSpec((t, LANE), lambda h, i: (i, qo + h)),
                pl.BlockSpec((s_len, LANE), lambda h, i: (0, ko + h)),
                pl.BlockSpec((s_len, LANE), lambda h, i: (0, vo + h))]
    args = [qa, ka, va]
    if decay:
        in_specs.append(pl.BlockSpec((None, nt, 1, t), lambda h, i: (h, 0, 0, 0)))
        args.append(c_row)
    return pl.pallas_call(
        kern, name=name, grid=(HEADS, nt), in_specs=in_specs,
        out_specs=[pl.BlockSpec((t, LANE), lambda h, i: (i, h)), pl.BlockSpec((None, t, 1), lambda h, i: (h, i, 0)),
                   pl.BlockSpec((None, None, 1, t), lambda h, i: (h, i, 0, 0))],
        out_shape=[jax.ShapeDtypeStruct((s_len, HEADS * LANE), f32), jax.ShapeDtypeStruct((HEADS, s_len, 1), f32),
                   jax.ShapeDtypeStruct((HEADS, nt, 1, t), f32)],
        compiler_params=pltpu.CompilerParams(dimension_semantics=("parallel", "arbitrary")))(*args)


def _attn_dq(name, q, k, v, o, do, lse, scale, c_row=None):
    (qa, qo), (ka, ko), (va, vo) = q, k, v
    s_len = qa.shape[0]
    t = _att_tiles(s_len)[1]
    nt = s_len // t
    decay = c_row is not None
    q_mul, s_mul = _fold_scale(scale)

    rp = min(ATT_ROWS, t)

    def kern(*refs):
        q_ref, k_ref, v_ref, o_ref, do_ref, lse_ref = refs[:6]
        dq_ref, delta_row_ref, drow_ref, delta_ref, s_ref, dp_ref, ds_ref = refs[-7:]
        i = pl.program_id(1)
        qb = (q_ref[...] * q_mul).astype(bf16)
        dob = do_ref[...]
        delta = jnp.sum(dob * o_ref[...], axis=1, keepdims=True)
        delta_ref[...] = delta
        delta_row_ref[...] = _as_row(delta)
        dob = dob.astype(bf16)
        drow_ref[...] = jnp.zeros((t, 1), f32)
        dq_ref[...] = jnp.zeros((t, LANE), f32)

        def step(j, diagonal):
            rows = pl.ds(pl.multiple_of(j * t, t), t)
            kb = k_ref[rows, :].astype(bf16)
            s_ref[...] = lax.dot_general(qb, kb, (_DOT_DIMS["nt"], ((), ())), preferred_element_type=f32)
            dp_ref[...] = lax.dot_general(dob, v_ref[rows, :].astype(bf16), (_DOT_DIMS["nt"], ((), ())),
                                          preferred_element_type=f32)
            ck = refs[6][j] if decay else None

            def rows_of(c, carry):
                r = slice(c * rp, (c + 1) * rp)
                s = _finish_scores(s_ref[r, :], s_mul, ck, c * rp if diagonal else None)
                ds = jnp.exp(s - lse_ref[r, :]) * (dp_ref[r, :] - delta_ref[r, :])
                drow_ref[r, :] += jnp.sum(ds, axis=1, keepdims=True)
                ds_ref[r, :] = ds.astype(bf16)
                return carry

            for c in range(t // rp):
                rows_of(c, 0)
            dq_ref[...] += jnp.dot(ds_ref[...], kb, preferred_element_type=f32)

        def below(j, carry):
            step(j, False)
            return carry

        lax.fori_loop(0, i, below, 0)
        step(i, True)
        dq_ref[...] = dq_ref[...] * scale

    in_specs = [pl.BlockSpec((t, LANE), lambda h, i: (i, qo + h)),
                pl.BlockSpec((s_len, LANE), lambda h, i: (0, ko + h)),
                pl.BlockSpec((s_len, LANE), lambda h, i: (0, vo + h)),
                pl.BlockSpec((t, LANE), lambda h, i: (i, h)),
                pl.BlockSpec((t, LANE), lambda h, i: (i, h)),
                pl.BlockSpec((None, t, 1), lambda h, i: (h, i, 0))]
    args = [qa, ka, va, o, do, lse]
    if decay:
        in_specs.append(pl.BlockSpec((None, nt, 1, t), lambda h, i: (h, 0, 0, 0)))
        args.append(c_row)
    col = pl.BlockSpec((None, t, 1), lambda h, i: (h, i, 0))
    return pl.pallas_call(
        kern, name=name, grid=(HEADS, nt), in_specs=in_specs,
        out_specs=[pl.BlockSpec((t, LANE), lambda h, i: (i, h)), pl.BlockSpec((None, None, 1, t), lambda h, i: (h, i, 0, 0)), col],
        out_shape=[jax.ShapeDtypeStruct((s_len, HEADS * LANE), f32), jax.ShapeDtypeStruct((HEADS, nt, 1, t), f32),
                   jax.ShapeDtypeStruct((HEADS, s_len, 1), f32)],
        scratch_shapes=[pltpu.VMEM((t, 1), f32), pltpu.VMEM((t, t), f32), pltpu.VMEM((t, t), f32), pltpu.VMEM((t, t), bf16)],
        compiler_params=pltpu.CompilerParams(dimension_semantics=("parallel", "arbitrary")))(*args)


def _attn_dkv(name, q, k, v, do, lse, delta, scale, c_col=None):
    (qa, qo), (ka, ko), (va, vo) = q, k, v
    s_len = qa.shape[0]
    tq, tk = _att_tiles(s_len)
    assert lse.shape == (HEADS, s_len // tq, 1, tq), (lse.shape, tq)
    nq, per = s_len // tq, tk // tq
    decay = c_col is not None
    q_mul, s_mul = _fold_scale(scale)

    def kern(*refs):
        q_ref, k_ref, v_ref, do_ref, lse_ref, delta_ref = refs[:6]
        j = pl.program_id(1)
        kb = k_ref[...].astype(bf16)
        vb = v_ref[...].astype(bf16)
        ck = refs[6][...] if decay else None

        def step(i, carry, diagonal):
            dk, dv, dsum = carry
            for d in range(per):
                tile = i * per + d
                rows = pl.ds(pl.multiple_of(tile * tq, tq), tq)
                qb = (q_ref[rows, :] * q_mul).astype(bf16)
                dob = do_ref[rows, :].astype(bf16)
                s = _scores_t(kb, qb.T, s_mul, ck, d * tq if diagonal else None, tq, tk)
                p = jnp.exp(s - lse_ref[tile])
                dv = dv + jnp.dot(p.astype(bf16), dob, preferred_element_type=f32)
                dp = jnp.dot(vb, dob.T, preferred_element_type=f32)
                ds = p * (dp - delta_ref[tile])
                dk = dk + jnp.dot(ds.astype(bf16), qb, preferred_element_type=f32)
                if decay:
                    dsum = dsum + ds
            return dk, dv, dsum

        init = (jnp.zeros((tk, LANE), f32), jnp.zeros((tk, LANE), f32), jnp.zeros((tk, tq), f32))
        dk, dv, dsum = lax.fori_loop(j + 1, s_len // tk, lambda i, c: step(i, c, False), step(j, init, True))
        if decay:
            dk_ref, dv_ref, dc_ref = refs[-3:]
            dc_ref[...] = -jnp.sum(dsum, axis=1, keepdims=True)
        else:
            dk_ref, dv_ref = refs[-2:]
        dk_ref[...] = dk * s_mul
        dv_ref[...] = dv

    stat = pl.BlockSpec((None, nq, 1, tq), lambda h, j: (h, 0, 0, 0))
    in_specs = [pl.BlockSpec((s_len, LANE), lambda h, j: (0, qo + h)),
                pl.BlockSpec((tk, LANE), lambda h, j: (j, ko + h)),
                pl.BlockSpec((tk, LANE), lambda h, j: (j, vo + h)),
                pl.BlockSpec((s_len, LANE), lambda h, j: (0, h)), stat, stat]
    args = [qa, ka, va, do, lse, delta]
    out_specs = [pl.BlockSpec((tk, LANE), lambda h, j: (j, h)), pl.BlockSpec((tk, LANE), lambda h, j: (j, h))]
    out_shape = [jax.ShapeDtypeStruct((s_len, HEADS * LANE), f32), jax.ShapeDtypeStruct((s_len, HEADS * LANE), f32)]
    if decay:
        in_specs.append(pl.BlockSpec((None, tk, 1), lambda h, j: (h, j, 0)))
        args.append(c_col)
        out_specs.append(pl.BlockSpec((None, tk, 1), lambda h, j: (h, j, 0)))
        out_shape.append(jax.ShapeDtypeStruct((HEADS, s_len, 1), f32))
    return pl.pallas_call(
        kern, name=name, grid=(HEADS, s_len // tk), in_specs=in_specs, out_specs=out_specs, out_shape=out_shape,
        compiler_params=pltpu.CompilerParams(dimension_semantics=("parallel", "arbitrary")))(*args)


CONV_TS, CONV_CB = 1024, 256
FFN_ROWS = 64


def _conv_fwd(name, x, w, b, taps):
    xa, width, xidx = _view(x)
    s_len = xa.shape[0]
    ts, cb = min(CONV_TS, s_len), CONV_CB
    xo = xidx * width // cb

    def kern(x_ref, halo_ref, w_ref, b_ref, o_ref):
        i = pl.program_id(1)
        xb = x_ref[...]
        halo = jnp.where(i == 0, 0.0, halo_ref[...])
        xx = jnp.concatenate([halo, xb], axis=0)
        out = b_ref[...] + w_ref[taps - 1:taps, :] * xb
        for k in range(taps - 1):
            out = out + w_ref[k:k + 1, :] * pltpu.roll(xx, taps - 1 - k, 0)[SUBLANE:]
        o_ref[...] = out

    return pl.pallas_call(
        kern, name=name, grid=(width // cb, s_len // ts),
        in_specs=[pl.BlockSpec((ts, cb), lambda j, i: (i, xo + j)),
                  pl.BlockSpec((SUBLANE, cb), lambda j, i: (jnp.maximum(i * (ts // SUBLANE) - 1, 0), xo + j)),
                  pl.BlockSpec((taps, cb), lambda j, i: (0, j)),
                  pl.BlockSpec((1, cb), lambda j, i: (0, j))],
        out_specs=pl.BlockSpec((ts, cb), lambda j, i: (i, j)),
        out_shape=jax.ShapeDtypeStruct((s_len, width), f32),
        compiler_params=pltpu.CompilerParams(dimension_semantics=("parallel", "parallel")))(xa, xa, w, b)


def _conv_bwd(name, x, dout, w, taps, dout2=None, dx_dtype=f32):
    xa, width, xidx = _view(x)
    s_len = xa.shape[0]
    ts, cb = min(CONV_TS, s_len), CONV_CB
    xo = xidx * width // cb
    n_i = s_len // ts
    two = dout2 is not None

    def kern(*refs):
        x_ref, halo_ref, w_ref = refs[:3]
        dx_ref, dw_ref, db_ref = refs[-3:]
        i = pl.program_id(1)
        if two:
            d = refs[3][...] + refs[5][...]
            dn = refs[4][...] + refs[6][...]
        else:
            d, dn = refs[3][...], refs[4][...]
        dn = jnp.where(i == n_i - 1, 0.0, dn)
        xb = x_ref[...]
        halo = jnp.where(i == 0, 0.0, halo_ref[...])
        xx = jnp.concatenate([halo, xb], axis=0)
        dd = jnp.concatenate([d, dn], axis=0)

        @pl.when(i == 0)
        def _():
            dw_ref[...] = jnp.zeros_like(dw_ref)
            db_ref[...] = jnp.zeros_like(db_ref)

        dx = w_ref[taps - 1:taps, :] * d
        dw_ref[taps - 1:taps, :] += jnp.sum(d * xb, axis=0, keepdims=True)
        for k in range(taps - 1):
            sh = taps - 1 - k
            dx = dx + w_ref[k:k + 1, :] * pltpu.roll(dd, ts + SUBLANE - sh, 0)[:ts]
            dw_ref[k:k + 1, :] += jnp.sum(d * pltpu.roll(xx, sh, 0)[SUBLANE:], axis=0, keepdims=True)
        dx_ref[...] = dx.astype(dx_ref.dtype)
        db_ref[...] += jnp.sum(d, axis=0, keepdims=True)

    d_spec = pl.BlockSpec((ts, cb), lambda j, i: (i, j))
    dn_spec = pl.BlockSpec((SUBLANE, cb), lambda j, i: (jnp.minimum((i + 1) * (ts // SUBLANE), s_len // SUBLANE - 1), j))
    in_specs = [pl.BlockSpec((ts, cb), lambda j, i: (i, xo + j)),
                pl.BlockSpec((SUBLANE, cb), lambda j, i: (jnp.maximum(i * (ts // SUBLANE) - 1, 0), xo + j)),
                pl.BlockSpec((taps, cb), lambda j, i: (0, j)), d_spec, dn_spec]
    args = [xa, xa, w, dout, dout]
    if two:
        in_specs += [d_spec, dn_spec]
        args += [dout2, dout2]
    return pl.pallas_call(
        kern, name=name, grid=(width // cb, n_i), in_specs=in_specs,
        out_specs=[pl.BlockSpec((ts, cb), lambda j, i: (i, j)), pl.BlockSpec((taps, cb), lambda j, i: (0, j)),
                   pl.BlockSpec((1, cb), lambda j, i: (0, j))],
        out_shape=[jax.ShapeDtypeStruct((s_len, width), dx_dtype), jax.ShapeDtypeStruct((taps, width), f32),
                   jax.ShapeDtypeStruct((1, width), f32)],
        compiler_params=pltpu.CompilerParams(dimension_semantics=("parallel", "arbitrary")))(*args)


def _conv_rows(xx, w_ref, b_ref, taps):
    out = b_ref[...] + w_ref[taps - 1:taps, :] * xx[SUBLANE:]
    for k in range(taps - 1):
        out = out + w_ref[k:k + 1, :] * pltpu.roll(xx, taps - 1 - k, 0)[SUBLANE:]
    return out


def _ffn_act_fwd(name, up, w, b):
    s_len = up.shape[0]
    ts, cb = min(CONV_TS, s_len), CONV_CB
    nf = D_FF // cb

    def kern(g_ref, gp_ref, v_ref, vp_ref, wg_ref, wv_ref, bg_ref, bv_ref, o_ref):
        first = pl.program_id(1) == 0
        ug = _conv_rows(jnp.concatenate([jnp.where(first, 0.0, gp_ref[...]), g_ref[...]], axis=0), wg_ref, bg_ref, FFN_CONV)
        uv = _conv_rows(jnp.concatenate([jnp.where(first, 0.0, vp_ref[...]), v_ref[...]], axis=0), wv_ref, bv_ref, FFN_CONV)
        o_ref[...] = (jax.nn.silu(ug) * uv).astype(o_ref.dtype)

    def half(off):
        return [pl.BlockSpec((ts, cb), lambda j, i: (i, off + j)),
                pl.BlockSpec((SUBLANE, cb), lambda j, i: (jnp.maximum(i * (ts // SUBLANE) - 1, 0), off + j))]

    def par(rows, off):
        return pl.BlockSpec((rows, cb), lambda j, i: (0, off + j))

    return pl.pallas_call(
        kern, name=name, grid=(nf, s_len // ts),
        in_specs=half(0) + half(nf) + [par(FFN_CONV, 0), par(FFN_CONV, nf), par(1, 0), par(1, nf)],
        out_specs=pl.BlockSpec((ts, cb), lambda j, i: (i, j)),
        out_shape=jax.ShapeDtypeStruct((s_len, D_FF), bf16),
        compiler_params=pltpu.CompilerParams(dimension_semantics=("parallel", "parallel")))(up, up, up, up, w, w, b, b)


def _ffn_act_bwd(name, up, dact, w, b):
    s_len = up.shape[0]
    ts, cb = min(CONV_TS, s_len), CONV_CB
    nf = D_FF // cb
    n_i = s_len // ts
    taps = FFN_CONV

    ch = min(FFN_ROWS, ts)

    def kern(g_ref, gp_ref, gn_ref, v_ref, vp_ref, vn_ref, d_ref, dn_ref, wg_ref, wv_ref, bg_ref, bv_ref,
             dg_ref, dv_ref, dwg_ref, dwv_ref, dbg_ref, dbv_ref, gx_ref, vx_ref, dd_ref):
        i = pl.program_id(1)
        first, last = i == 0, i == n_i - 1
        for x_ref, p_ref, n_ref, ext in ((g_ref, gp_ref, gn_ref, gx_ref), (v_ref, vp_ref, vn_ref, vx_ref)):
            ext[:SUBLANE, :] = jnp.where(first, 0.0, p_ref[...])
            ext[SUBLANE:SUBLANE + ts, :] = x_ref[...]
            ext[SUBLANE + ts:, :] = jnp.where(last, 0.0, n_ref[...])
        dd_ref[:ts, :] = d_ref[...]
        dd_ref[ts:, :] = jnp.where(last, 0.0, dn_ref[...])

        @pl.when(first)
        def _():
            for ref in (dwg_ref, dwv_ref, dbg_ref, dbv_ref):
                ref[...] = jnp.zeros_like(ref)

        def rows_of(c, carry):
            r0 = pl.multiple_of(c * ch, ch)
            gx, vx = gx_ref[pl.ds(r0, ch + 2 * SUBLANE), :], vx_ref[pl.ds(r0, ch + 2 * SUBLANE), :]
            ug, uv = _conv_rows(gx, wg_ref, bg_ref, taps), _conv_rows(vx, wv_ref, bv_ref, taps)
            dd = dd_ref[pl.ds(r0, ch + SUBLANE), :]
            sg = jax.nn.sigmoid(ug)
            out = []
            for du, xx, w_ref, dx_ref, sums in ((dd * uv * (sg * (1.0 + ug * (1.0 - sg))), gx, wg_ref, dg_ref, carry[0]),
                                                (dd * (ug * sg), vx, wv_ref, dv_ref, carry[1])):
                d = du[:ch]
                dx = w_ref[taps - 1:taps, :] * d
                new = [None] * (taps + 1)
                new[taps - 1] = sums[taps - 1] + jnp.sum(d * xx[SUBLANE:SUBLANE + ch], axis=0, keepdims=True)
                for k in range(taps - 1):
                    sh = taps - 1 - k
                    dx = dx + w_ref[k:k + 1, :] * pltpu.roll(du, ch + SUBLANE - sh, 0)[:ch]
                    new[k] = sums[k] + jnp.sum(d * pltpu.roll(xx, sh, 0)[SUBLANE:SUBLANE + ch], axis=0, keepdims=True)
                new[taps] = sums[taps] + jnp.sum(d, axis=0, keepdims=True)
                dx_ref[pl.ds(r0, ch), :] = dx.astype(dx_ref.dtype)
                out.append(tuple(new))
            return tuple(out)

        zero = tuple(jnp.zeros((1, cb), f32) for _ in range(taps + 1))
        sums_g, sums_v = lax.fori_loop(0, ts // ch, rows_of, (zero, zero))
        for sums, dw_ref, db_ref in ((sums_g, dwg_ref, dbg_ref), (sums_v, dwv_ref, dbv_ref)):
            for k in range(taps):
                dw_ref[k:k + 1, :] += sums[k]
            db_ref[...] += sums[taps]

    blocks = s_len // SUBLANE

    def half(off):
        return [pl.BlockSpec((ts, cb), lambda j, i: (i, off + j)),
                pl.BlockSpec((SUBLANE, cb), lambda j, i: (jnp.maximum(i * (ts // SUBLANE) - 1, 0), off + j)),
                pl.BlockSpec((SUBLANE, cb), lambda j, i: (jnp.minimum((i + 1) * (ts // SUBLANE), blocks - 1), off + j))]

    def par(rows, off):
        return pl.BlockSpec((rows, cb), lambda j, i: (0, off + j))

    d_specs = [pl.BlockSpec((ts, cb), lambda j, i: (i, j)),
               pl.BlockSpec((SUBLANE, cb), lambda j, i: (jnp.minimum((i + 1) * (ts // SUBLANE), blocks - 1), j))]
    out_par = [pl.BlockSpec((r, cb), lambda j, i: (0, j)) for r in (taps, taps, 1, 1)]
    return pl.pallas_call(
        kern, name=name, grid=(nf, n_i),
        in_specs=half(0) + half(nf) + d_specs + [par(taps, 0), par(taps, nf), par(1, 0), par(1, nf)],
        out_specs=[pl.BlockSpec((ts, cb), lambda j, i: (i, j))] * 2 + out_par,
        out_shape=[jax.ShapeDtypeStruct((s_len, D_FF), bf16)] * 2 + [jax.ShapeDtypeStruct((taps, D_FF), f32)] * 2
        + [jax.ShapeDtypeStruct((1, D_FF), f32)] * 2,
        scratch_shapes=[pltpu.VMEM((ts + 2 * SUBLANE, cb), f32)] * 2 + [pltpu.VMEM((ts + SUBLANE, cb), f32)],
        compiler_params=pltpu.CompilerParams(dimension_semantics=("parallel", "arbitrary")))(
        up, up, up, up, up, up, dact, dact, w, w, b, b)


SCAN_ROWS = 128


def _block_scan(a, b, reverse):
    t = a.shape[0]
    row = lax.broadcasted_iota(jnp.int32, a.shape, 0)
    d = 1
    while d < t:
        keep = row < t - d if reverse else row >= d
        shift = t - d if reverse else d
        a_far = jnp.where(keep, pltpu.roll(a, shift, 0), 1.0)
        b_far = jnp.where(keep, pltpu.roll(b, shift, 0), 0.0)
        b = a * b_far + b
        a = a * a_far
        d *= 2
    return a, b


def _scan_fwd(name, a, b):
    s_len, width = a.shape
    t = min(SCAN_ROWS, s_len)

    def kern(a_ref, b_ref, h_ref):
        def block(k, carry):
            rows = pl.ds(pl.multiple_of(k * t, t), t)
            acc, h = _block_scan(a_ref[rows, :], b_ref[rows, :], False)
            h_ref[rows, :] = h + acc * carry
            return h_ref[pl.ds(k * t + t - 1, 1), :]

        lax.fori_loop(0, s_len // t, block, jnp.zeros((1, LANE), f32))

    spec = pl.BlockSpec((s_len, LANE), lambda j: (0, j))
    return pl.pallas_call(
        kern, name=name, grid=(width // LANE,), in_specs=[spec, spec], out_specs=spec,
        out_shape=jax.ShapeDtypeStruct((s_len, width), f32),
        compiler_params=pltpu.CompilerParams(dimension_semantics=("parallel",)))(a, b)


def _scan_bwd(name, a_next, h_prev, dh):
    s_len, width = dh.shape
    t = min(SCAN_ROWS, s_len)
    n_blocks = s_len // t

    def kern(an_ref, hp_ref, dh_ref, da_ref, db_ref):
        def block(kk, carry):
            k = n_blocks - 1 - kk
            rows = pl.ds(pl.multiple_of(k * t, t), t)
            acc, g = _block_scan(an_ref[rows, :], dh_ref[rows, :], True)
            g = g + acc * carry
            db_ref[rows, :] = g
            da_ref[rows, :] = g * hp_ref[rows, :]
            return db_ref[pl.ds(k * t, 1), :]

        lax.fori_loop(0, n_blocks, block, jnp.zeros((1, LANE), f32))

    spec = pl.BlockSpec((s_len, LANE), lambda j: (0, j))
    return pl.pallas_call(
        kern, name=name, grid=(width // LANE,), in_specs=[spec, spec, spec], out_specs=[spec, spec],
        out_shape=[jax.ShapeDtypeStruct((s_len, width), f32)] * 2,
        compiler_params=pltpu.CompilerParams(dimension_semantics=("parallel",)))(a_next, h_prev, dh)


def _lane_cumsum(x, reverse):
    n = x.shape[1]
    lane = lax.broadcasted_iota(jnp.int32, x.shape, 1)
    sh = 1
    while sh < n:
        if reverse:
            x = x + jnp.where(lane < n - sh, pltpu.roll(x, n - sh, 1), 0.0)
        else:
            x = x + jnp.where(lane >= sh, pltpu.roll(x, sh, 1), 0.0)
        sh *= 2
    return x


def _decay_fwd(name, fl_t, b8):
    def kern(f_ref, b_ref, c_ref):
        c_ref[...] = _lane_cumsum(jax.nn.log_sigmoid(f_ref[...] + b_ref[...]), False)

    return pl.pallas_call(kern, name=name, out_shape=jax.ShapeDtypeStruct(fl_t.shape, f32))(fl_t, b8)


def _decay_bwd(name, fl_t, b8, dc_key, dc_query):
    def kern(f_ref, b_ref, dck_ref, dcq_ref, df_ref, db_ref):
        dlogf = _lane_cumsum(dck_ref[...] + dcq_ref[...], True)
        df = dlogf * jax.nn.sigmoid(-(f_ref[...] + b_ref[...]))
        df_ref[...] = df
        db_ref[...] = jnp.sum(df, axis=1, keepdims=True)

    return pl.pallas_call(kern, name=name, out_shape=[jax.ShapeDtypeStruct(fl_t.shape, f32),
                                                      jax.ShapeDtypeStruct((SUBLANE, 1), f32)])(fl_t, b8, dc_key, dc_query)


def _rms(x, g, n):
    return x * lax.rsqrt(jnp.sum(x * x, axis=-1, keepdims=True) * (1.0 / n) + EPS) * g


def _loss_head(name, h, target, g, tb=256):
    n, d = h.shape
    tb = min(tb, n)

    def kern(h_ref, t_ref, g_ref, loss_ref, dh_ref, dg_ref):
        i = pl.program_id(0)
        tgt = t_ref[...]

        def f(hv, gv):
            err = _rms(hv, gv, d) - tgt
            return 0.5 * jnp.sum(jnp.sum(err * err, axis=-1, keepdims=True) * (1.0 / d), axis=0, keepdims=True)

        val, vjp = jax.vjp(f, h_ref[...], g_ref[...])
        dh, dg = vjp(jnp.ones((1, 1), f32))
        dh_ref[...] = dh

        @pl.when(i == 0)
        def _():
            loss_ref[...] = jnp.zeros_like(loss_ref)
            dg_ref[...] = jnp.zeros_like(dg_ref)

        loss_ref[...] += val
        dg_ref[...] += dg

    return pl.pallas_call(
        kern, name=name, grid=(n // tb,),
        in_specs=[pl.BlockSpec((tb, d), lambda i: (i, 0)), pl.BlockSpec((tb, d), lambda i: (i, 0)),
                  pl.BlockSpec((1, d), lambda i: (0, 0))],
        out_specs=[pl.BlockSpec((1, 1), lambda i: (0, 0)), pl.BlockSpec((tb, d), lambda i: (i, 0)),
                   pl.BlockSpec((1, d), lambda i: (0, 0))],
        out_shape=[jax.ShapeDtypeStruct((1, 1), f32), jax.ShapeDtypeStruct((n, d), f32), jax.ShapeDtypeStruct((1, d), f32)],
        compiler_params=pltpu.CompilerParams(dimension_semantics=("arbitrary",)))(h, target, g)


def _f_norm(x, g):
    return (_rms(x, g, D_MODEL),)


def _f_latent(qc, kvc, gq, gkv):
    return _rms(qc, gq, MLA_Q_RANK), _rms(kvc, gkv, MLA_KV_RANK)


def _f_rope_table(pos, freq, m1, m2):
    ang = pos * freq
    sin = jnp.sin(ang)
    return jnp.cos(ang), -sin * m1, sin * m2


def _rope(x, cos, s_up, s_down):
    w = x.shape[1]
    return x * cos + _roll(x, w - MLA_ROPE // 2, 1) * s_up + _roll(x, MLA_ROPE // 2, 1) * s_down


def _f_mla_prep(q, kpart, kr, cos, s_up, s_down):
    def heads(t):
        return jnp.concatenate([t] * HEADS, axis=1)

    kr = _rope(kr, cos, s_up, s_down)
    return _rope(q, heads(cos), heads(s_up), heads(s_down)), kpart + heads(kr)


def _f_lru_gate(gates, xc, b_r, b_i, lam):
    r = jax.nn.sigmoid(gates[:, :LRU_WIDTH] + b_r)
    i = jax.nn.sigmoid(gates[:, LRU_WIDTH:] + b_i)
    log_a = -LRU_C * r * jax.nn.softplus(-lam)
    mult = jnp.sqrt(-jnp.tanh(log_a) * (1.0 + jnp.exp(2.0 * log_a)))
    return jnp.exp(log_a), mult * (i * xc)


def _f_merge(o_mla, o_fox, hs, lg, g):
    o_lru = hs * jax.nn.gelu(lg)
    return (jnp.concatenate([_rms(o_mla, g[:, :512], HEADS * MLA_V), _rms(o_fox, g[:, 512:1024], HEADS * FOX_HEAD_DIM),
                             _rms(o_lru, g[:, 1024:], LRU_WIDTH)], axis=1),)


def _f_ffn_gate(u):
    return (jax.nn.silu(u[:, :D_FF]) * u[:, D_FF:],)


def _f_ple(h, gpre, pp):
    return (h + jax.nn.sigmoid(gpre) * pp,)


MIX_PART = ["w_in", "w_uq", "w_ukv", "lru_conv_w"]
FFN_PART = ["w_o", "w_up", "ffn_conv_w", "w_down", "w_ple_gate", "w_ple_proj"]


def _prep_mix_weights(w):
    eye = jnp.eye(LRU_BLOCKS, dtype=f32)

    def block_diag(m):
        return (eye[:, None, :, None] * m[:, :, None, :]).reshape(LRU_WIDTH, LRU_WIDTH)

    return dict(
        w_in=_take_pad(w["w_in"], Z_MAP, 1),
        w_uq=_take_pad(_take_pad(w["w_uq"], UQ_COL_MAP, 1), UQ_ROW_MAP, 0),
        w_ukv=_take_pad(w["w_ukv"], UKV_MAP, 1),
        w_ri=jnp.concatenate([block_diag(w["w_r"]), block_diag(w["w_i"])], axis=1).astype(bf16),
        g_mix=w["g_mix"].reshape(1, -1), g_ffn=w["g_ffn"].reshape(1, -1), g_ple=w["g_ple"].reshape(1, -1),
        g_qc=_take_pad(w["g_qc"], UQ_ROW_MAP, 0).reshape(1, -1), g_kvc=w["g_kvc"].reshape(1, -1),
        g_out=_take_pad(w["g_out"], OMIX_MAP, 0).reshape(1, -1),
        b_f8=_take_pad(w["b_f"], _pad_to(np.arange(FOX_HEADS), SUBLANE), 0).reshape(SUBLANE, 1),
        lru_conv_w=w["lru_conv_w"], lru_conv_b=w["lru_conv_b"].reshape(1, -1),
        b_r=w["b_r"].reshape(1, -1), b_i=w["b_i"].reshape(1, -1), lam=w["lru_lambda"].reshape(1, -1),
        ffn_conv_b=w["ffn_conv_b"].reshape(1, -1),
    )


def _prep_ffn_weights(w):
    return dict(w_o=_take_pad(w["w_o"], OMIX_MAP, 0),
                w_up=w["w_up"], w_up_g=w["w_up"][:, :D_FF], w_up_v=w["w_up"][:, D_FF:], ffn_conv_w=w["ffn_conv_w"],
                w_down=w["w_down"], w_ple_gate=w["w_ple_gate"], w_ple_proj=w["w_ple_proj"])


def _rope_rows(pos):
    consts = [jnp.asarray(t) for t in _rope_tables(LANE, ROPE_AT)]
    return _rowwise("rope_table", _f_rope_table, [pos], consts, [(LANE, f32)] * 3)


def _key_decay(c_t, s_len):
    t = _att_tiles(s_len)[1]
    return c_t[:HEADS].reshape(HEADS, s_len // t, 1, t), c_t[:HEADS].reshape(HEADS, s_len, 1)


def _layer_fwd(l, h0, p_l, rope, weights_of):
    s_len = h0.shape[0]
    n = f"l{l}_"
    w = _prep_mix_weights(weights_of("mix", h0))
    xn, = _rowwise(n + "norm_mix", _f_norm, [h0], [w["g_mix"]], [(D_MODEL, bf16)])
    z = _mm(n + "in_proj", xn, w["w_in"])
    zq = (z, QC_W, Z_QC // QC_W)
    zkv = (z, LANE, Z_KVC // LANE)
    zkr = (z, LANE, Z_KR // LANE)
    zlx = (z, LRU_WIDTH, Z_LX // LRU_WIDTH)
    zlg = (z, LRU_WIDTH, Z_LG // LRU_WIDTH)
    qcn, kvn = _rowwise(n + "latent_norm", _f_latent, [zq, zkv], [w["g_qc"], w["g_kvc"]], [(QC_W, bf16), (LANE, bf16)])
    q = _mm(n + "uq", qcn, w["w_uq"])
    kv = _mm(n + "ukv", kvn, w["w_ukv"])
    kpart = (kv, HEADS * LANE, 0)
    qr, kk = _rowwise(n + "mla_prep", _f_mla_prep, [q, kpart, zkr, *rope], [],
                      [(HEADS * LANE, bf16), (HEADS * LANE, bf16)])
    mla_scale = (MLA_NOPE + MLA_ROPE) ** -0.5
    o_mla, lse_m, lse_m_row = _attn_fwd(n + "mla_fwd", (qr, 0), (kk, 0), (kv, HEADS), mla_scale)
    fl_t = z[:, Z_FL:Z_FL + SUBLANE].T
    c_t = _decay_fwd(n + "decay", fl_t, w["b_f8"])
    c_row, c_col = _key_decay(c_t, s_len)
    fox_scale = FOX_HEAD_DIM ** -0.5
    o_fox, lse_f, lse_f_row = _attn_fwd(n + "fox_fwd", (z, Z_FQ // LANE), (z, Z_FK // LANE), (z, Z_FV // LANE), fox_scale, c_row)
    xc = _conv_fwd(n + "lru_conv", zlx, w["lru_conv_w"], w["lru_conv_b"], LRU_CONV)
    gates = _mm(n + "lru_gates", xc, w["w_ri"])
    a, bx = _rowwise(n + "lru_gate", _f_lru_gate, [gates, xc], [w["b_r"], w["b_i"], w["lam"]],
                     [(LRU_WIDTH, f32), (LRU_WIDTH, f32)])
    hs = _scan_fwd(n + "lru_scan", a, bx)
    ocat, = _rowwise(n + "merge", _f_merge, [o_mla, o_fox, hs, zlg], [w["g_out"]], [(OMIX_W, bf16)])
    w.update(_prep_ffn_weights(weights_of("ffn", ocat)))
    h1 = _mm(n + "out_proj", ocat, w["w_o"], res=h0)
    xn2, = _rowwise(n + "norm_ffn", _f_norm, [h1], [w["g_ffn"]], [(D_MODEL, bf16)])
    up = _mm(n + "up_proj", xn2, w["w_up"])
    act = _ffn_act_fwd(n + "ffn_act", up, w["ffn_conv_w"], w["ffn_conv_b"])
    h2 = _mm(n + "down_proj", act, w["w_down"], res=h1)
    hn, = _rowwise(n + "norm_ple", _f_norm, [h2], [w["g_ple"]], [(D_MODEL, bf16)])
    gpre = _mm(n + "ple_gate", hn, w["w_ple_gate"])
    pp = _mm(n + "ple_proj", p_l, w["w_ple_proj"])
    h3, = _rowwise(n + "ple_mix", _f_ple, [h2, gpre, pp], [], [(D_MODEL, f32)])
    res = dict(h0=h0, xn=xn, z=z, qcn=qcn, kvn=kvn, q=q, kv=kv, qr=qr, kk=kk, o_mla=o_mla, lse_m=lse_m, fl_t=fl_t,
               lse_m_row=lse_m_row, lse_f_row=lse_f_row, c_row=c_row, c_col=c_col, o_fox=o_fox, lse_f=lse_f, xc=xc, gates=gates, a=a, hs=hs, ocat=ocat, h1=h1,
               xn2=xn2, up=up, act=act, h2=h2, hn=hn, gpre=gpre, pp=pp, p_l=p_l)
    return h3, res, w


def _layer_bwd(l, dh3, r, rope, w, token, grads_to):
    s_len = dh3.shape[0]
    n = f"l{l}_"
    g = {}
    w = dict(w, g_ple=w["g_ple"] + token)
    z = r["z"]
    zq = (z, QC_W, Z_QC // QC_W)
    zkv = (z, LANE, Z_KVC // LANE)
    zkr = (z, LANE, Z_KR // LANE)
    zlx = (z, LRU_WIDTH, Z_LX // LRU_WIDTH)
    zlg = (z, LRU_WIDTH, Z_LG // LRU_WIDTH)
    (dh2a, dgpre, dpp), _ = _rowwise_bwd(n + "ple_mix_b", _f_ple, [r["h2"], r["gpre"], r["pp"]], [], [dh3], 3,
                                         dts=[f32, bf16, bf16])
    g["w_ple_proj"] = _mm(n + "ple_proj_dw", r["p_l"], dpp, "tn", bf16)
    dhn = _mm(n + "ple_gate_dx", dgpre, w["w_ple_gate"], "nt")
    g["w_ple_gate"] = _mm(n + "ple_gate_dw", r["hn"], dgpre, "tn", bf16)
    (dh2,), (g["g_ple"],) = _rowwise_bwd(n + "norm_ple_b", _f_norm, [r["h2"]], [w["g_ple"]], [dhn], 1, adds={0: dh2a})
    dact = _mm(n + "down_dx", dh2, w["w_down"], "nt")
    g["w_down"] = _mm(n + "down_dw", r["act"], dh2, "tn", bf16)
    dup_g, dup_v, dcw_g, dcw_v, dcb_g, dcb_v = _ffn_act_bwd(n + "ffn_act_b", r["up"], dact, w["ffn_conv_w"], w["ffn_conv_b"])
    g["ffn_conv_w"] = jnp.concatenate([dcw_g, dcw_v], axis=1)
    g["ffn_conv_b"] = jnp.concatenate([dcb_g, dcb_v], axis=1)
    dxn2 = _mm(n + "up_dx_v", dup_v, w["w_up_v"], "nt", res=_mm(n + "up_dx_g", dup_g, w["w_up_g"], "nt"))
    g["w_up"] = jnp.concatenate([_mm(n + "up_dw_g", r["xn2"], dup_g, "tn", bf16),
                                 _mm(n + "up_dw_v", r["xn2"], dup_v, "tn", bf16)], axis=1)
    (dh1,), (g["g_ffn"],) = _rowwise_bwd(n + "norm_ffn_b", _f_norm, [r["h1"]], [w["g_ffn"]], [dxn2], 1, adds={0: dh2})
    docat = _mm(n + "out_dx", dh1, w["w_o"], "nt")
    g["w_o"] = _mm(n + "out_dw", r["ocat"], dh1, "tn", bf16)
    token = grads_to("ffn", dict(w_o=_take_inv(g["w_o"], OMIX_MAP, 0), w_up=g["w_up"], ffn_conv_w=g["ffn_conv_w"],
                                 w_down=g["w_down"], w_ple_gate=g["w_ple_gate"], w_ple_proj=g["w_ple_proj"]))
    w = dict(w, g_out=w["g_out"] + token)
    (do_mla, do_fox, dhs, dlg), (g["g_out"],) = _rowwise_bwd(
        n + "merge_b", _f_merge, [r["o_mla"], r["o_fox"], r["hs"], zlg], [w["g_out"]], [docat], 4)
    a, hs = r["a"], r["hs"]
    a_next = jnp.concatenate([a[1:], jnp.zeros((1, LRU_WIDTH), f32)], axis=0)
    h_prev = jnp.concatenate([jnp.zeros((1, LRU_WIDTH), f32), hs[:-1]], axis=0)
    da, dbx = _scan_bwd(n + "lru_scan_b", a_next, h_prev, dhs)
    (dgates, dxc_a), (g["b_r"], g["b_i"], g["lam"]) = _rowwise_bwd(
        n + "lru_gate_b", _f_lru_gate, [r["gates"], r["xc"]], [w["b_r"], w["b_i"], w["lam"]], [da, dbx], 2,
        dts=[bf16, f32])
    dxc_b = _mm(n + "lru_gates_dx", dgates, w["w_ri"], "nt")
    g["w_ri"] = _mm(n + "lru_gates_dw", r["xc"], dgates, "tn")
    dlx, g["lru_conv_w"], g["lru_conv_b"] = _conv_bwd(n + "lru_conv_b", zlx, dxc_a, w["lru_conv_w"], LRU_CONV, dout2=dxc_b)
    fox_scale = FOX_HEAD_DIM ** -0.5
    fq, fk, fv = (z, Z_FQ // LANE), (z, Z_FK // LANE), (z, Z_FV // LANE)
    dfq, delta_f, dc_q = _attn_dq(n + "fox_dq", fq, fk, fv, r["o_fox"], do_fox, r["lse_f"], fox_scale, r["c_row"])
    dfk, dfv, dc_k = _attn_dkv(n + "fox_dkv", fq, fk, fv, do_fox, r["lse_f_row"], delta_f, fox_scale,
                               r["c_col"])
    pad_rows = jnp.zeros((SUBLANE - HEADS, s_len), f32)
    dfl_t, g["b_f8"] = _decay_bwd(n + "decay_b", r["fl_t"], w["b_f8"],
                                  jnp.concatenate([dc_k.reshape(HEADS, s_len), pad_rows], axis=0),
                                  jnp.concatenate([dc_q.reshape(HEADS, s_len), pad_rows], axis=0))
    dfl = jnp.pad(dfl_t.T, ((0, 0), (0, LANE - SUBLANE)))
    mla_scale = (MLA_NOPE + MLA_ROPE) ** -0.5
    qr, kk, kv = (r["qr"], 0), (r["kk"], 0), (r["kv"], HEADS)
    dqr, delta_m, _ = _attn_dq(n + "mla_dq", qr, kk, kv, r["o_mla"], do_mla, r["lse_m"], mla_scale)
    dkk, dv_m = _attn_dkv(n + "mla_dkv", qr, kk, kv, do_mla, r["lse_m_row"], delta_m, mla_scale)
    (dq, dkpart, dkr), _ = _rowwise_bwd(n + "mla_prep_b", _f_mla_prep, [r["q"], (r["kv"], HEADS * LANE, 0), zkr, *rope],
                                        [], [dqr, dkk], 3, dts=[bf16, bf16, f32])
    dkv = jnp.concatenate([dkpart, dv_m.astype(bf16)], axis=1)
    dkvn = _mm(n + "ukv_dx", dkv, w["w_ukv"], "nt")
    g["w_ukv"] = _mm(n + "ukv_dw", r["kvn"], dkv, "tn", bf16)
    dqcn = _mm(n + "uq_dx", dq, w["w_uq"], "nt")
    g["w_uq"] = _mm(n + "uq_dw", r["qcn"], dq, "tn", bf16)
    (dqc, dkvc), (g["g_qc"], g["g_kvc"]) = _rowwise_bwd(n + "latent_norm_b", _f_latent, [zq, zkv],
                                                        [w["g_qc"], w["g_kvc"]], [dqcn, dkvn], 2)
    dz = jnp.concatenate([t.astype(bf16) for t in (dfq, dfk, dfv, dlx, dlg, dqc, dkvc, dkr, dfl)], axis=1)
    dxn = _mm(n + "in_dx", dz, w["w_in"], "nt")
    g["w_in"] = _mm(n + "in_dw", r["xn"], dz, "tn", bf16)
    (dh0,), (g["g_mix"],) = _rowwise_bwd(n + "norm_mix_b", _f_norm, [r["h0"]], [w["g_mix"]], [dxn], 1, adds={0: dh1})
    return dh0, grads_to("mix", _unpad_mix_grads(g))


def _unpad_mix_grads(g):
    d_ri = g["w_ri"]
    idx = jnp.arange(LRU_BLOCKS)

    def diag_blocks(m):
        return m.reshape(LRU_BLOCKS, LRU_BLOCK, LRU_BLOCKS, LRU_BLOCK)[idx, :, idx, :]

    return dict(
        g_mix=g["g_mix"][0], w_in=_take_inv(g["w_in"], Z_MAP, 1), g_qc=g["g_qc"][0, :MLA_Q_RANK],
        w_uq=_take_inv(g["w_uq"][:MLA_Q_RANK], UQ_COL_MAP, 1), g_kvc=g["g_kvc"][0],
        w_ukv=_take_inv(g["w_ukv"], UKV_MAP, 1), b_f=g["b_f8"][:FOX_HEADS, 0],
        lru_conv_w=g["lru_conv_w"], lru_conv_b=g["lru_conv_b"][0],
        w_r=diag_blocks(d_ri[:, :LRU_WIDTH]), b_r=g["b_r"][0], w_i=diag_blocks(d_ri[:, LRU_WIDTH:]), b_i=g["b_i"][0],
        lru_lambda=g["lam"][0], g_out=_take_inv(g["g_out"][0], OMIX_MAP, 0),
        g_ffn=g["g_ffn"][0], ffn_conv_b=g["ffn_conv_b"][0], g_ple=g["g_ple"][0],
    )


LAYER_WEIGHTS = ["g_mix", "w_in", "g_qc", "w_uq", "g_kvc", "w_ukv", "b_f", "lru_conv_w", "lru_conv_b", "w_r", "b_r", "w_i",
                 "b_i", "lru_lambda", "g_out", "w_o", "g_ffn", "w_up", "ffn_conv_w", "ffn_conv_b", "w_down", "g_ple",
                 "w_ple_gate", "w_ple_proj"]
WEIGHTS = LAYER_WEIGHTS + ["g_final"]


def _local_step(x, p, pos, target, g_final, weights_of, grads_to):
    h = x
    rope = _rope_rows(pos)
    ws, saved = [], []
    for l in range(DEPTH):
        h, r, w = _layer_fwd(l, h, p[l], rope, functools.partial(weights_of, l))
        ws.append(w)
        saved.append(r)
    loss, dh, dg_final = _loss_head("loss_head", h, target, g_final.reshape(1, -1))
    token = jnp.zeros((), f32)
    for l in reversed(range(DEPTH)):
        dh, token = _layer_bwd(l, dh, saved[l], rope, ws[l], token, functools.partial(grads_to, l))
    return loss[0, 0], dh, dg_final[0]


MESH_AXES = ("x", "y", "c")


def _row_tile(rows, cap):
    if rows <= cap:
        return rows
    for t in range(cap, SUBLANE - 1, -SUBLANE):
        if rows % t == 0:
            return t
    return rows


ADAM_BLOCK_BYTES = 2 ** 20


def _adamw(name, w, g, m, v):
    rows, cols = w.shape
    tr = _row_tile(rows, max(SUBLANE, ADAM_BLOCK_BYTES // (4 * cols) // SUBLANE * SUBLANE))

    def kern(w_ref, g_ref, m_ref, v_ref, d_ref, nm_ref, nv_ref):
        gv = g_ref[...]
        nm = ADAM_B1 * m_ref[...] + (1.0 - ADAM_B1) * gv
        nv = ADAM_B2 * v_ref[...] + (1.0 - ADAM_B2) * (gv * gv)
        m_hat = nm / (1.0 - ADAM_B1 ** ADAM_STEP)
        v_hat = nv / (1.0 - ADAM_B2 ** ADAM_STEP)
        d_ref[...] = -ADAM_LR * (m_hat / (jnp.sqrt(v_hat) + ADAM_EPS) + ADAM_WD * w_ref[...])
        nm_ref[...] = nm
        nv_ref[...] = nv

    spec = pl.BlockSpec((tr, cols), lambda i: (i, 0))
    return pl.pallas_call(
        kern, name=name, grid=(rows // tr,), in_specs=[spec] * 4, out_specs=[spec] * 3,
        out_shape=[jax.ShapeDtypeStruct((rows, cols), f32)] * 3,
        compiler_params=pltpu.CompilerParams(dimension_semantics=("parallel",)))(w, g, m, v)


def _packed_rows(shape):
    return -(-int(np.prod(shape)) // (SUBLANE * LANE)) * SUBLANE


def _pack(arrays):
    rows = []
    for a in arrays:
        flat = a.reshape(-1)
        rows.append(jnp.pad(flat, (0, _packed_rows(a.shape) * LANE - flat.shape[0])).reshape(-1, LANE))
    return jnp.concatenate(rows, axis=0)


def _unpack(buf, shapes):
    out, at = [], 0
    for s in shapes:
        rows = _packed_rows(s)
        out.append(buf[at:at + rows].reshape(-1)[:int(np.prod(s))].reshape(s))
        at += rows
    return out


SHARD_AXIS = {"w_in": 2, "w_uq": 2, "w_ukv": 2, "lru_conv_w": 2, "w_o": 1, "w_up": 2, "ffn_conv_w": 2, "w_down": 1,
              "w_ple_gate": 1, "w_ple_proj": 2}
SHARDED = [k for k in WEIGHTS if k in SHARD_AXIS]
REPLICATED = [k for k in WEIGHTS if k not in SHARD_AXIS]
ELEMENTWISE_F32 = ("lru_conv_w", "ffn_conv_w")
N_SHARDS = 4
BF16_TILE_ROWS = 16


HBM_SPEC = pl.BlockSpec(memory_space=pl.ANY)
SEM_SPEC = pl.BlockSpec(memory_space=pltpu.SEMAPHORE)
SPLIT_EFFECT = pltpu.SideEffectType.DATAFLOW_SIDE_EFFECTING
CHIP_FLIPS = ((1, 0), (0, 1), (1, 1))
N_DEVICES = 8
SUM_BLOCK_BYTES = 4 * 2 ** 20


def _device_index():
    return 4 * lax.axis_index("x") + 2 * lax.axis_index("y") + lax.axis_index("c")


def _when(cond, fn):
    if cond is None:
        fn()
    else:
        pl.when(cond)(fn)


class _Exchange:
    def __init__(self, name, plan, srcs, land_shapes, n_send, n_recv):
        self.name, self.plan, self.srcs, self.n = name, plan, list(srcs), len(srcs)
        self.land_shapes, self.n_send, self.n_recv = land_shapes, n_send, n_recv

    def run(self):
        n = self.n

        def body(*refs):
            sends, arrivals = self.plan(refs[:n], refs[n:2 * n], refs[2 * n], refs[2 * n + 1])
            for cond, cp in sends:
                _when(cond, cp.start)
            for cond, cp in arrivals:
                _when(cond, cp.wait_recv)
            for cond, cp in sends:
                _when(cond, cp.wait_send)

        return pl.pallas_call(
            body, name=self.name, out_shape=self.land_shapes, in_specs=[HBM_SPEC] * n, out_specs=[HBM_SPEC] * n,
            scratch_shapes=[pltpu.SemaphoreType.DMA((self.n_send,)), pltpu.SemaphoreType.DMA((self.n_recv,))])(*self.srcs)

    def start(self, after=None):
        n = self.n
        lands = [lax.empty(s.shape, s.dtype) for s in self.land_shapes]
        extra = [] if after is None else [after]

        def body(*refs):
            ins, lands_in = refs[:n], refs[n:2 * n]
            send_sems, recv_sems, token = refs[2 * n + len(extra)], refs[2 * n + len(extra) + 1], refs[-1]
            sends, _ = self.plan(ins, lands_in, send_sems, recv_sems)
            for cond, cp in sends:
                _when(cond, cp.start)
            token[...] = jnp.zeros_like(token)

        hbm = [pltpu.with_memory_space_constraint(a, pltpu.HBM) for a in self.srcs + lands]
        res = pl.pallas_call(
            body, name=self.name + "_start",
            out_shape=(pltpu.SemaphoreType.DMA((self.n_send,)), pltpu.SemaphoreType.DMA((self.n_recv,)),
                       *[pltpu.HBM(a.shape, a.dtype) for a in hbm], jax.ShapeDtypeStruct((SUBLANE, LANE), f32)),
            in_specs=[HBM_SPEC] * (2 * n + len(extra)),
            out_specs=(SEM_SPEC, SEM_SPEC, *[HBM_SPEC] * (2 * n), pl.BlockSpec(memory_space=pltpu.VMEM)),
            input_output_aliases={i: 2 + i for i in range(2 * n)},
            compiler_params=pltpu.CompilerParams(has_side_effects=SPLIT_EFFECT))(*hbm, *extra)
        self.sems, self.thru, token = res[:2], res[2:2 + 2 * n], res[-1]
        return token[0, 0]

    def finish(self, after):
        n = self.n

        def body(*refs):
            ins, lands_in, send_sems, recv_sems = refs[:n], refs[n:2 * n], refs[2 * n], refs[2 * n + 1]
            sends, arrivals = self.plan(ins, lands_in, send_sems, recv_sems)
            for cond, cp in arrivals:
                _when(cond, cp.wait_recv)
            for cond, cp in sends:
                _when(cond, cp.wait_send)

        res = pl.pallas_call(
            body, name=self.name + "_finish", out_shape=tuple(pltpu.HBM(a.shape, a.dtype) for a in self.thru),
            in_specs=[HBM_SPEC] * (2 * n) + [SEM_SPEC, SEM_SPEC, HBM_SPEC], out_specs=tuple([HBM_SPEC] * (2 * n)),
            input_output_aliases={i: i for i in range(2 * n)},
            compiler_params=pltpu.CompilerParams(has_side_effects=SPLIT_EFFECT))(*self.thru, *self.sems, after)
        return list(res[n:])


def _gather_exchange(name, shards):
    def plan(ins, lands, send_sems, recv_sems):
        x, y, c = (lax.axis_index(a) for a in MESH_AXES)
        copies = []
        for i in range(len(ins)):
            for k, (fx, fy) in enumerate(CHIP_FLIPS):
                peer = (1 - x if fx else x, 1 - y if fy else y, c)
                copies.append((None, pltpu.make_async_remote_copy(
                    src_ref=ins[i], dst_ref=lands[i].at[2 * x + y], send_sem=send_sems.at[3 * i + k],
                    recv_sem=recv_sems.at[3 * i + k], device_id=peer, device_id_type=pl.DeviceIdType.MESH)))
        return copies, copies

    n = len(shards)
    return _Exchange(name, plan, shards, [jax.ShapeDtypeStruct((N_SHARDS,) + s.shape, s.dtype) for s in shards], 3 * n, 3 * n)


def _scatter_exchange(name, layer, chunks):
    def plan(ins, lands, send_sems, recv_sems):
        x, y, c = (lax.axis_index(a) for a in MESH_AXES)
        me = _device_index()
        sends, arrivals = [], []
        for i in range(len(ins)):
            for j in range(N_SHARDS):
                target = (j // 2, j % 2, layer)
                remote = jnp.logical_not((x == target[0]) & (y == target[1]) & (c == layer))
                sends.append((remote, pltpu.make_async_remote_copy(
                    src_ref=ins[i].at[j], dst_ref=lands[i].at[me], send_sem=send_sems.at[N_SHARDS * i + j],
                    recv_sem=recv_sems.at[N_DEVICES * i + me], device_id=target, device_id_type=pl.DeviceIdType.MESH)))
            for s in range(N_DEVICES):
                arrivals.append(((c == layer) & (me != s), pltpu.make_async_remote_copy(
                    src_ref=ins[i].at[0], dst_ref=lands[i].at[s], send_sem=send_sems.at[0],
                    recv_sem=recv_sems.at[N_DEVICES * i + s], device_id=(x, y, c), device_id_type=pl.DeviceIdType.MESH)))
        return sends, arrivals

    n = len(chunks)
    lands = [jax.ShapeDtypeStruct((N_DEVICES,) + ch.shape[1:], ch.dtype) for ch in chunks]
    return _Exchange(name, plan, chunks, lands, N_SHARDS * n, N_DEVICES * n)


def _sum_contributions(name, got, mine):
    _, a, b = got.shape
    ta = _row_tile(a, max(SUBLANE, SUM_BLOCK_BYTES // (N_DEVICES * b * got.dtype.itemsize) // SUBLANE * SUBLANE))

    def kern(got_ref, mine_ref, o_ref):
        me = _device_index()
        acc = jnp.zeros(o_ref.shape, f32)
        for s in range(N_DEVICES):
            acc = acc + jnp.where(me == s, mine_ref[...].astype(f32), got_ref[s].astype(f32))
        o_ref[...] = acc

    return pl.pallas_call(
        kern, name=name, grid=(a // ta,),
        in_specs=[pl.BlockSpec((N_DEVICES, ta, b), lambda i: (0, i, 0)), pl.BlockSpec((ta, b), lambda i: (i, 0))],
        out_specs=pl.BlockSpec((ta, b), lambda i: (i, 0)), out_shape=jax.ShapeDtypeStruct((a, b), f32),
        compiler_params=pltpu.CompilerParams(dimension_semantics=("parallel",)))(got, mine)


def _swap_layers(name, sums):
    n = len(sums[0])

    def body(*refs):
        srcs = (refs[:n], refs[n:2 * n])
        outs, (send_sems, recv_sems) = refs[2 * n:3 * n], refs[3 * n:]
        x, y, c = (lax.axis_index(a) for a in MESH_AXES)
        for i in range(n):
            for layer in range(DEPTH):
                cp = pltpu.make_async_remote_copy(
                    src_ref=srcs[layer][i], dst_ref=outs[i], send_sem=send_sems.at[i], recv_sem=recv_sems.at[i],
                    device_id=(x, y, 1 - c), device_id_type=pl.DeviceIdType.MESH)
                pl.when(c == layer)(cp.start)
        for i in range(n):
            pltpu.make_async_remote_copy(
                src_ref=srcs[0][i], dst_ref=outs[i], send_sem=send_sems.at[i], recv_sem=recv_sems.at[i],
                device_id=(x, y, 1 - c), device_id_type=pl.DeviceIdType.MESH).wait()

    return pl.pallas_call(
        body, name=name, out_shape=[jax.ShapeDtypeStruct(s.shape, s.dtype) for s in sums[0]],
        in_specs=[HBM_SPEC] * (2 * n), out_specs=[HBM_SPEC] * n,
        scratch_shapes=[pltpu.SemaphoreType.DMA((n,)), pltpu.SemaphoreType.DMA((n,))])(*sums[0], *sums[1])


def _stack_shards(g, axis):
    if axis == 1:
        return g.reshape(N_SHARDS, g.shape[0] // N_SHARDS, g.shape[1])
    return g.reshape(g.shape[0], N_SHARDS, g.shape[1] // N_SHARDS).transpose(1, 0, 2)


def _join_shards(s, axis):
    if axis == 1:
        return s.reshape(-1, s.shape[2])
    return s.transpose(1, 0, 2).reshape(s.shape[1], -1)


def _layer_shards(w, l, names):
    return [w[k][l] if k in ELEMENTWISE_F32 else w[k][l].astype(bf16) for k in names]


def _full_weights(names, sent, got):
    j = 2 * lax.axis_index("x") + lax.axis_index("y")
    return {k: _join_shards(lax.dynamic_update_slice(g, own[None], (j, 0, 0)), SHARD_AXIS[k])
            for k, own, g in zip(names, sent, got)}


def _grad_chunks(grads, names):
    return [_stack_shards(grads[k], SHARD_AXIS[k]).astype(bf16) for k in names]


def _sum_group(l, names, got, chunks):
    j = 2 * lax.axis_index("x") + lax.axis_index("y")
    return {k: _sum_contributions(f"sum_l{l}_{k}", g, lax.dynamic_index_in_dim(ch, j, 0, keepdims=False))
            for k, g, ch in zip(names, got, chunks)}


def _both_layers(name, names, sums):
    c = lax.axis_index("c")
    mine = [[sums[l][k] for k in names] for l in range(DEPTH)]
    other = _swap_layers(name, mine)
    return {k: jnp.stack([jnp.where(c == 0, mine[0][i], other[i]), jnp.where(c == 0, other[i], mine[1][i])])
            for i, k in enumerate(names)}


def _gather_all_exchange(name, src):
    def plan(ins, lands, send_sems, recv_sems):
        coords = [lax.axis_index(a) for a in MESH_AXES]
        me = _device_index()
        sends, arrivals = [], []
        for f in range(1, N_DEVICES):
            peer = tuple(1 - cd if (f >> (2 - b)) & 1 else cd for b, cd in enumerate(coords))
            sends.append((None, pltpu.make_async_remote_copy(
                src_ref=ins[0], dst_ref=lands[0].at[me], send_sem=send_sems.at[f - 1], recv_sem=recv_sems.at[me],
                device_id=peer, device_id_type=pl.DeviceIdType.MESH)))
        for s in range(N_DEVICES):
            arrivals.append((me != s, pltpu.make_async_remote_copy(
                src_ref=ins[0], dst_ref=lands[0].at[s], send_sem=send_sems.at[0], recv_sem=recv_sems.at[s],
                device_id=tuple(coords), device_id_type=pl.DeviceIdType.MESH)))
        return sends, arrivals

    return _Exchange(name, plan, [src], [jax.ShapeDtypeStruct((N_DEVICES,) + src.shape, src.dtype)], N_DEVICES - 1, N_DEVICES)


def kernel(x, p, positions, g_mix, w_in, g_qc, w_uq, g_kvc, w_ukv, b_f, lru_conv_w, lru_conv_b, w_r, b_r, w_i, b_i, lru_lambda, g_out, w_o, g_ffn, w_up, ffn_conv_w, ffn_conv_b, w_down, g_ple, w_ple_gate, w_ple_proj, g_final, loss_target, m_g_mix, m_w_in, m_g_qc, m_w_uq, m_g_kvc, m_w_ukv, m_b_f, m_lru_conv_w, m_lru_conv_b, m_w_r, m_b_r, m_w_i, m_b_i, m_lru_lambda, m_g_out, m_w_o, m_g_ffn, m_w_up, m_ffn_conv_w, m_ffn_conv_b, m_w_down, m_g_ple, m_w_ple_gate, m_w_ple_proj, m_g_final, v_g_mix, v_w_in, v_g_qc, v_w_uq, v_g_kvc, v_w_ukv, v_b_f, v_lru_conv_w, v_lru_conv_b, v_w_r, v_b_r, v_w_i, v_b_i, v_lru_lambda, v_g_out, v_w_o, v_g_ffn, v_w_up, v_ffn_conv_w, v_ffn_conv_b, v_w_down, v_g_ple, v_w_ple_gate, v_w_ple_proj, v_g_final):
    given = locals()
    w = {k: given[k] for k in WEIGHTS}
    m = {k: given["m_" + k] for k in WEIGHTS}
    v = {k: given["v_" + k] for k in WEIGHTS}

    parts = {"mix": MIX_PART, "ffn": FFN_PART}
    groups = [(l, part) for l in range(DEPTH) for part in ("mix", "ffn")]
    sent = {g: _layer_shards(w, g[0], parts[g[1]]) for g in groups}
    first = _gather_exchange("gather_l0_mix", sent[groups[0]]).run()
    ahead = {g: _gather_exchange(f"gather_l{g[0]}_{g[1]}", sent[g]) for g in groups[1:]}
    pos = positions[0].astype(f32).reshape(-1, 1) + ahead[groups[1]].start(after=first[0])
    behind, layer_grads, chunks = {}, [{} for _ in range(DEPTH)], {}

    def weights_of(l, part, after):
        g = (l, part)
        got = first if g == groups[0] else ahead[g].finish(after=after)
        full = _full_weights(parts[part], sent[g], got)
        if part == "mix":
            full.update({k: w[k][l] for k in LAYER_WEIGHTS if k in REPLICATED})
        if g == groups[1]:
            for later in groups[2:]:
                full["ffn_conv_w"] = full["ffn_conv_w"] + ahead[later].start(after=got[0])
        return full

    def grads_to(l, part, grads):
        g = (l, part)
        layer_grads[l].update(grads)
        chunks[g] = _grad_chunks(grads, parts[part])
        if g == groups[0]:
            return jnp.zeros((), f32)
        behind[g] = _scatter_exchange(f"scatter_l{l}_{part}", l, chunks[g])
        return behind[g].start()

    loss, dx, dg_final = _local_step(x[0], p[:, 0], pos, loss_target[0], w["g_final"], weights_of, grads_to)

    grads = {k: jnp.stack([layer_grads[l][k] for l in range(DEPTH)]) for k in LAYER_WEIGHTS if k in REPLICATED}
    grads["g_final"] = dg_final
    rep_shapes = [w[k].shape for k in REPLICATED] + [(1,)]
    contrib = _pack([grads[k] for k in REPLICATED] + [loss.reshape(1)])
    last = _scatter_exchange("scatter_l0_mix", 0, chunks[groups[0]])
    everyone = _gather_all_exchange("gather_replicated", contrib)
    started = (last.start() + everyone.start() + dx[0, 0]).reshape(1, 1)

    def adamw_of(names, g_sharded):
        out = {}
        for k in names:
            shape = w[k].shape
            flat = [t.reshape(-1, shape[-1]) for t in (w[k], g_sharded[k], m[k], v[k])]
            out[k] = [t.reshape(shape) for t in (flat[1],) + tuple(_adamw("adamw_" + k, *flat))]
        return out

    sums = [{} for _ in range(DEPTH)]
    for g in groups[1:]:
        sums[g[0]].update(_sum_group(g[0], parts[g[1]], behind[g].finish(after=started), chunks[g]))
    big = adamw_of(FFN_PART, _both_layers("swap_ffn", FFN_PART, sums))
    sums[0].update(_sum_group(0, MIX_PART, last.finish(after=big[FFN_PART[0]][1]), chunks[groups[0]]))
    big.update(adamw_of(MIX_PART, _both_layers("swap_mix", MIX_PART, sums)))

    g_rep = _sum_contributions("sum_replicated", everyone.finish(after=big[MIX_PART[0]][1])[0], contrib)
    zero = jnp.zeros((1,), f32)
    w_rep, m_rep, v_rep = (_pack([t[k] for k in REPLICATED] + [zero]) for t in (w, m, v))
    rep = [_unpack(b, rep_shapes) for b in (g_rep,) + tuple(_adamw("adamw_replicated", w_rep, g_rep, m_rep, v_rep))]

    outs = []
    for kind in range(4):
        by_name = {k: big[k][kind] for k in SHARDED}
        by_name.update(zip(REPLICATED, rep[kind][:-1]))
        outs.append([by_name[k] for k in WEIGHTS])
    total_loss = rep[0][-1][0]
    return (total_loss, dx.reshape(x.shape), *outs[0], *outs[1], *outs[2], *outs[3])
```

```python
import functools
import math

import numpy as np
import jax
import jax.numpy as jnp
from jax import lax
from jax.experimental import pallas as pl
from jax.experimental.pallas import tpu as pltpu

f32, bf16 = jnp.float32, jnp.bfloat16

D_MODEL = 1024
PLE_DIM = 256
MLA_HEADS, MLA_NOPE, MLA_ROPE, MLA_V = 4, 64, 32, 64
MLA_Q_RANK, MLA_KV_RANK = 192, 128
FOX_HEADS, FOX_HEAD_DIM = 4, 64
LRU_WIDTH, LRU_BLOCKS, LRU_BLOCK, LRU_CONV, LRU_C = 512, 8, 64, 4, 8.0
D_FF, FFN_CONV = 2816, 3
ROPE_THETA = 10000.0
EPS = 1e-6
DEPTH = 2
ADAM_LR, ADAM_B1, ADAM_B2, ADAM_EPS, ADAM_WD, ADAM_STEP = 0.001, 0.9, 0.999, 1e-08, 0.01, 10

LANE = 128
SUBLANE = 8
HEADS = 4

Z_FQ, Z_FK, Z_FV, Z_LX, Z_LG, Z_QC, Z_KVC, Z_KR, Z_FL, Z_W = 0, 512, 1024, 1536, 2048, 2560, 2816, 2944, 3072, 3200
QC_W = 256
ROPE_AT = 64


def _head_pad_map(n_heads, width):
    m = -np.ones(n_heads * LANE, np.int64)
    for h in range(n_heads):
        m[h * LANE:h * LANE + width] = h * width + np.arange(width)
    return m


def _z_map():
    m = -np.ones(Z_W, np.int64)
    o_qc, o_kvc, o_kr = 0, MLA_Q_RANK, MLA_Q_RANK + MLA_KV_RANK
    o_fq = o_kr + MLA_ROPE
    o_fk, o_fv = o_fq + 256, o_fq + 512
    o_fl = o_fv + 256
    o_lx = o_fl + FOX_HEADS
    o_lg = o_lx + LRU_WIDTH
    m[Z_FQ:Z_FQ + 512] = np.where(_head_pad_map(4, 64) >= 0, _head_pad_map(4, 64) + o_fq, -1)
    m[Z_FK:Z_FK + 512] = np.where(_head_pad_map(4, 64) >= 0, _head_pad_map(4, 64) + o_fk, -1)
    m[Z_FV:Z_FV + 512] = np.where(_head_pad_map(4, 64) >= 0, _head_pad_map(4, 64) + o_fv, -1)
    m[Z_LX:Z_LX + 512] = o_lx + np.arange(512)
    m[Z_LG:Z_LG + 512] = o_lg + np.arange(512)
    m[Z_QC:Z_QC + MLA_Q_RANK] = o_qc + np.arange(MLA_Q_RANK)
    m[Z_KVC:Z_KVC + MLA_KV_RANK] = o_kvc + np.arange(MLA_KV_RANK)
    m[Z_KR + ROPE_AT:Z_KR + ROPE_AT + MLA_ROPE] = o_kr + np.arange(MLA_ROPE)
    m[Z_FL:Z_FL + FOX_HEADS] = o_fl + np.arange(FOX_HEADS)
    return m


def _ukv_map():
    m = -np.ones(2 * HEADS * LANE, np.int64)
    for h in range(HEADS):
        m[h * LANE:h * LANE + MLA_NOPE] = h * (MLA_NOPE + MLA_V) + np.arange(MLA_NOPE)
        m[HEADS * LANE + h * LANE:HEADS * LANE + h * LANE + MLA_V] = h * (MLA_NOPE + MLA_V) + MLA_NOPE + np.arange(MLA_V)
    return m


def _omix_map():
    return np.concatenate([_head_pad_map(4, 64), np.where(_head_pad_map(4, 64) >= 0, _head_pad_map(4, 64) + 256, -1),
                           512 + np.arange(512)])


def _pad_to(m, n):
    return np.concatenate([m, -np.ones(n - m.shape[0], np.int64)])


def _runs(m):
    out, at = [], 0
    while at < len(m):
        end = at + 1
        while end < len(m) and (m[end] == m[end - 1] + 1 if m[at] >= 0 else m[end] < 0):
            end += 1
        out.append((int(m[at]), end - at))
        at = end
    return out


def _take_runs(a, m, axis):
    parts = []
    for start, size in _runs(m):
        if start < 0:
            shape = list(a.shape)
            shape[axis] = size
            parts.append(jnp.zeros(shape, a.dtype))
        else:
            parts.append(lax.slice_in_dim(a, start, start + size, axis=axis))
    return parts[0] if len(parts) == 1 else jnp.concatenate(parts, axis=axis)


def _take_pad(a, m, axis):
    return _take_runs(a, m, axis)


def _take_inv(a, m, axis):
    n = int(m.max()) + 1
    inv = np.zeros(n, np.int64)
    inv[m[m >= 0]] = np.nonzero(m >= 0)[0]
    return _take_runs(a, inv, axis)


Z_MAP = _z_map()
UQ_COL_MAP = _head_pad_map(HEADS, MLA_NOPE + MLA_ROPE)
UQ_ROW_MAP = _pad_to(np.arange(MLA_Q_RANK), QC_W)
UKV_MAP = _ukv_map()
OMIX_MAP = _omix_map()
OMIX_W = 1536


def _rope_tables(width, at):
    half = MLA_ROPE // 2
    inv = ROPE_THETA ** (-np.arange(half, dtype=np.float32) / half)
    freq = np.zeros((1, width), np.float32)
    m1 = np.zeros((1, width), np.float32)
    m2 = np.zeros((1, width), np.float32)
    for h in range(width // LANE):
        b = h * LANE + at
        freq[0, b:b + half] = inv
        freq[0, b + half:b + 2 * half] = inv
        m1[0, b:b + half] = 1.0
        m2[0, b + half:b + 2 * half] = 1.0
    return freq, m1, m2


def _view(r):
    return r if isinstance(r, tuple) else (r, r.shape[1], 0)


def _blk(dim, cap):
    if dim <= cap:
        return dim
    for b in range(cap, LANE - 1, -LANE):
        if dim % b == 0:
            return b
    return dim


@functools.partial(jax.custom_vjp, nondiff_argnums=(1, 2))
def _roll(x, shift, axis):
    return pltpu.roll(x, shift, axis)


def _roll_fwd(x, shift, axis):
    return pltpu.roll(x, shift, axis), None


def _roll_bwd(shift, axis, _, g):
    return (pltpu.roll(g, g.shape[axis] - shift, axis),)


_roll.defvjp(_roll_fwd, _roll_bwd)


def _rowwise(name, fn, rows, pars, outs, tb=512):
    rows = [_view(r) for r in rows]
    n = rows[0][0].shape[0]
    tb = min(tb, n)
    nr, npar = len(rows), len(pars)

    def kern(*refs):
        r = [refs[k][...].astype(f32) for k in range(nr)]
        p = [refs[nr + k][...] for k in range(npar)]
        res = fn(*r, *p)
        for o_ref, o in zip(refs[nr + npar:], res):
            o_ref[...] = o.astype(o_ref.dtype)

    in_specs = [pl.BlockSpec((tb, w), lambda i, j=idx: (i, j)) for (_, w, idx) in rows]
    in_specs += [pl.BlockSpec(p.shape, lambda i: (0, 0)) for p in pars]
    out_specs = [pl.BlockSpec((tb, w), lambda i: (i, 0)) for (w, _) in outs]
    out_shape = [jax.ShapeDtypeStruct((n, w), dt) for (w, dt) in outs]
    return pl.pallas_call(kern, name=name, grid=(n // tb,), in_specs=in_specs, out_specs=out_specs, out_shape=out_shape,
                          compiler_params=pltpu.CompilerParams(dimension_semantics=("parallel",)))(*[r[0] for r in rows], *pars)


def _rowwise_bwd(name, fn, rows, pars, cts, ndiff, adds=None, tb=512, dts=None):
    rows = [_view(r) for r in rows]
    dts = dts or [f32] * ndiff
    adds = adds or {}
    add_keys = sorted(adds)
    n = rows[0][0].shape[0]
    tb = min(tb, n)
    nr, npar, nct, nadd = len(rows), len(pars), len(cts), len(add_keys)

    def kern(*refs):
        i = pl.program_id(0)
        r = [refs[k][...].astype(f32) for k in range(nr)]
        p = [refs[nr + k][...] for k in range(npar)]
        ct = [refs[nr + npar + k][...].astype(f32) for k in range(nct)]
        ad = {key: refs[nr + npar + nct + k][...] for k, key in enumerate(add_keys)}
        o_refs = refs[nr + npar + nct + nadd:]

        def g(*d):
            return tuple(fn(*d[:ndiff], *r[ndiff:], *d[ndiff:]))

        _, vjp = jax.vjp(g, *r[:ndiff], *p)
        grads = vjp(tuple(ct))
        for k in range(ndiff):
            gk = grads[k]
            if k in ad:
                gk = gk + ad[k]
            o_refs[k][...] = gk.astype(o_refs[k].dtype)

        @pl.when(i == 0)
        def _():
            for k in range(npar):
                o_refs[ndiff + k][...] = jnp.zeros_like(o_refs[ndiff + k])

        for k in range(npar):
            o_refs[ndiff + k][...] += grads[ndiff + k]

    in_specs = [pl.BlockSpec((tb, w), lambda i, j=idx: (i, j)) for (_, w, idx) in rows]
    in_specs += [pl.BlockSpec(p.shape, lambda i: (0, 0)) for p in pars]
    in_specs += [pl.BlockSpec((tb, c.shape[1]), lambda i: (i, 0)) for c in cts]
    in_specs += [pl.BlockSpec((tb, adds[k].shape[1]), lambda i: (i, 0)) for k in add_keys]
    out_specs = [pl.BlockSpec((tb, rows[k][1]), lambda i: (i, 0)) for k in range(ndiff)]
    out_specs += [pl.BlockSpec(p.shape, lambda i: (0, 0)) for p in pars]
    out_shape = [jax.ShapeDtypeStruct((n, rows[k][1]), dts[k]) for k in range(ndiff)]
    out_shape += [jax.ShapeDtypeStruct(p.shape, f32) for p in pars]
    res = pl.pallas_call(kern, name=name, grid=(n // tb,), in_specs=in_specs, out_specs=out_specs, out_shape=out_shape,
                         compiler_params=pltpu.CompilerParams(dimension_semantics=("arbitrary",)))(
        *[r[0] for r in rows], *pars, *cts, *[adds[k] for k in add_keys])
    return res[:ndiff], res[ndiff:]


_DOT_DIMS = {"nn": ((1,), (0,)), "nt": ((1,), (1,)), "tn": ((0,), (0,))}

MM_VMEM_BUDGET = 36 * 2 ** 20
MM_MAX_TM = 1408
MM_STEP, MM_RESULT, MM_XPOSE, MM_CAST = 700.0, 7.5e-4, 9e-4, 1e-3


def _tile_candidates(dim):
    c = [d for d in range(LANE, dim + 1, LANE) if dim % d == 0]
    return c or [dim]


@functools.lru_cache(maxsize=None)
def _mm_tiles(mode, m, n, k, a_bytes, b_bytes, o_bytes):
    best, best_cost = None, None
    for tm in _tile_candidates(m):
        if tm > MM_MAX_TM:
            continue
        for tn in _tile_candidates(n):
            for tk in _tile_candidates(k):
                vmem = 2 * (tm * tk * a_bytes + tk * tn * b_bytes + tm * tn * o_bytes) + 4 * tm * tn * (2 if tk < k else 1)
                vmem += (2 * tm * tk if a_bytes > 2 else 0) + (2 * tk * tn if b_bytes > 2 else 0)
                if vmem > MM_VMEM_BUDGET:
                    continue
                steps = (m // tm) * (n // tn) * (k // tk)
                cost = steps * MM_STEP + m * n * (k // tk) * MM_RESULT
                if mode == "tn":
                    cost += m * k * (n // tn) * MM_XPOSE
                cost += (m * k * (n // tn) * MM_CAST if a_bytes > 2 else 0) + (k * n * (m // tm) * MM_CAST if b_bytes > 2 else 0)
                if best is None or cost < best_cost:
                    best, best_cost = (tm, tn, tk), cost
    return best


def _mm(name, a, b, mode="nn", out_dtype=f32, res=None):
    if mode == "nn":
        (m, k), (_, n) = a.shape, b.shape
    elif mode == "nt":
        (m, k), (n, _) = a.shape, b.shape
    else:
        (k, m), (_, n) = a.shape, b.shape
    has_res = res is not None
    tm, tn, tk = _mm_tiles(mode, m, n, k, a.dtype.itemsize, b.dtype.itemsize,
                           jnp.dtype(out_dtype).itemsize + (res.dtype.itemsize if has_res else 0))
    nk = k // tk
    dims = (_DOT_DIMS[mode], ((), ()))

    def kern(*refs):
        a_ref, b_ref = refs[0], refs[1]
        o_ref, acc_ref = refs[-2], refs[-1]
        kk = pl.program_id(2)
        part = lax.dot_general(a_ref[...].astype(bf16), b_ref[...].astype(bf16), dims, preferred_element_type=f32)

        def finish(out):
            if has_res:
                out = out + refs[2][...]
            o_ref[...] = out.astype(o_ref.dtype)

        if nk == 1:
            finish(part)
            return

        @pl.when(kk == 0)
        def _():
            acc_ref[...] = part

        @pl.when(jnp.logical_and(kk > 0, kk < nk - 1))
        def _():
            acc_ref[...] += part

        @pl.when(kk == nk - 1)
        def _():
            finish(acc_ref[...] + part)

    if mode == "tn":
        a_spec = pl.BlockSpec((tk, tm), lambda i, j, kk: (kk, i))
    else:
        a_spec = pl.BlockSpec((tm, tk), lambda i, j, kk: (i, kk))
    if mode == "nt":
        b_spec = pl.BlockSpec((tn, tk), lambda i, j, kk: (j, kk))
    else:
        b_spec = pl.BlockSpec((tk, tn), lambda i, j, kk: (kk, j))
    in_specs = [a_spec, b_spec]
    args = [a, b]
    if has_res:
        in_specs.append(pl.BlockSpec((tm, tn), lambda i, j, kk: (i, j)))
        args.append(res)
    return pl.pallas_call(
        kern, name=name, grid=(m // tm, n // tn, nk), in_specs=in_specs,
        out_specs=pl.BlockSpec((tm, tn), lambda i, j, kk: (i, j)),
        out_shape=jax.ShapeDtypeStruct((m, n), out_dtype),
        scratch_shapes=[pltpu.VMEM((tm, tn) if nk > 1 else (SUBLANE, LANE), f32)],
        compiler_params=pltpu.CompilerParams(dimension_semantics=("parallel", "parallel", "arbitrary")))(*args)


ATT_TQ, ATT_TK = 512, 512


def _att_tiles(s_len):
    tk = min(ATT_TK, s_len)
    return min(ATT_TQ, tk), tk


def _fold_scale(scale):
    return (scale, 1.0) if math.frexp(scale)[0] == 0.5 else (1.0, scale)


def _as_row(col):
    return jnp.max(jnp.broadcast_to(col, (col.shape[0], LANE)).T[:SUBLANE], axis=0, keepdims=True)


def _scores_t(kb, q_t, s_mul, ck, diag_offset, tq, tk):
    s = jnp.dot(kb, q_t, preferred_element_type=f32)
    if s_mul != 1.0:
        s = s * s_mul
    if ck is not None:
        s = s - ck
    if diag_offset is None:
        return s
    key = lax.broadcasted_iota(jnp.int32, (tk, tq), 0)
    query = lax.broadcasted_iota(jnp.int32, (tk, tq), 1) + diag_offset
    return jnp.where(key <= query, s, -jnp.inf)


ATT_ROWS = 64


def _finish_scores(s, s_mul, ck, first_row):
    if s_mul != 1.0:
        s = s * s_mul
    if ck is not None:
        s = s - ck
    if first_row is None:
        return s
    row = lax.broadcasted_iota(jnp.int32, s.shape, 0) + first_row
    col = lax.broadcasted_iota(jnp.int32, s.shape, 1)
    return jnp.where(col <= row, s, -jnp.inf)


def _attn_fwd(name, q, k, v, scale, c_row=None):
    (qa, qo), (ka, ko), (va, vo) = q, k, v
    s_len = qa.shape[0]
    t = _att_tiles(s_len)[1]
    nt = s_len // t
    decay = c_row is not None
    q_mul, s_mul = _fold_scale(scale)

    def kern(*refs):
        q_ref, k_ref, v_ref = refs[:3]
        o_ref, lse_ref, lse_row_ref = refs[-3:]
        i = pl.program_id(1)
        qb = (q_ref[...] * q_mul).astype(bf16)

        def step(j, carry, diagonal):
            m, l, acc = carry
            rows = pl.ds(pl.multiple_of(j * t, t), t)
            kb = k_ref[rows, :].astype(bf16)
            vb = v_ref[rows, :].astype(bf16)
            s = lax.dot_general(qb, kb, (_DOT_DIMS["nt"], ((), ())), preferred_element_type=f32)
            s = _finish_scores(s, s_mul, refs[3][j] if decay else None, 0 if diagonal else None)
            m_new = jnp.maximum(m, jnp.max(s, axis=1, keepdims=True))
            alpha = jnp.exp(m - m_new)
            p = jnp.exp(s - m_new)
            l = alpha * l + jnp.sum(p, axis=1, keepdims=True)
            acc = alpha * acc + jnp.dot(p.astype(bf16), vb, preferred_element_type=f32)
            return m_new, l, acc

        init = (jnp.full((t, 1), -jnp.inf, f32), jnp.zeros((t, 1), f32), jnp.zeros((t, LANE), f32))
        m, l, acc = step(i, lax.fori_loop(0, i, lambda j, c: step(j, c, False), init), True)
        o_ref[...] = acc / l
        lse = m + jnp.log(l)
        lse_ref[...] = lse
        lse_row_ref[...] = _as_row(lse)

    in_specs = [pl.BlockSpec((t, LANE), lambda h, i: (i, qo + h)),
                pl.BlockSpec((s_len, LANE), lambda h, i: (0, ko + h)),
                pl.BlockSpec((s_len, LANE), lambda h, i: (0, vo + h))]
    args = [qa, ka, va]
    if decay:
        in_specs.append(pl.BlockSpec((None, nt, 1, t), lambda h, i: (h, 0, 0, 0)))
        args.append(c_row)
    return pl.pallas_call(
        kern, name=name, grid=(HEADS, nt), in_specs=in_specs,
        out_specs=[pl.BlockSpec((t, LANE), lambda h, i: (i, h)), pl.BlockSpec((None, t, 1), lambda h, i: (h, i, 0)),
                   pl.BlockSpec((None, None, 1, t), lambda h, i: (h, i, 0, 0))],
        out_shape=[jax.ShapeDtypeStruct((s_len, HEADS * LANE), f32), jax.ShapeDtypeStruct((HEADS, s_len, 1), f32),
                   jax.ShapeDtypeStruct((HEADS, nt, 1, t), f32)],
        compiler_params=pltpu.CompilerParams(dimension_semantics=("parallel", "arbitrary")))(*args)


def _attn_dq(name, q, k, v, o, do, lse, scale, c_row=None):
    (qa, qo), (ka, ko), (va, vo) = q, k, v
    s_len = qa.shape[0]
    t = _att_tiles(s_len)[1]
    nt = s_len // t
    decay = c_row is not None
    q_mul, s_mul = _fold_scale(scale)

    rp = min(ATT_ROWS, t)

    def kern(*refs):
        q_ref, k_ref, v_ref, o_ref, do_ref, lse_ref = refs[:6]
        dq_ref, delta_row_ref, drow_ref, delta_ref, s_ref, dp_ref, ds_ref = refs[-7:]
        i = pl.program_id(1)
        qb = (q_ref[...] * q_mul).astype(bf16)
        dob = do_ref[...]
        delta = jnp.sum(dob * o_ref[...], axis=1, keepdims=True)
        delta_ref[...] = delta
        delta_row_ref[...] = _as_row(delta)
        dob = dob.astype(bf16)
        drow_ref[...] = jnp.zeros((t, 1), f32)
        dq_ref[...] = jnp.zeros((t, LANE), f32)

        def step(j, diagonal):
            rows = pl.ds(pl.multiple_of(j * t, t), t)
            kb = k_ref[rows, :].astype(bf16)
            s_ref[...] = lax.dot_general(qb, kb, (_DOT_DIMS["nt"], ((), ())), preferred_element_type=f32)
            dp_ref[...] = lax.dot_general(dob, v_ref[rows, :].astype(bf16), (_DOT_DIMS["nt"], ((), ())),
                                          preferred_element_type=f32)
            ck = refs[6][j] if decay else None

            def rows_of(c, carry):
                r = slice(c * rp, (c + 1) * rp)
                s = _finish_scores(s_ref[r, :], s_mul, ck, c * rp if diagonal else None)
                ds = jnp.exp(s - lse_ref[r, :]) * (dp_ref[r, :] - delta_ref[r, :])
                drow_ref[r, :] += jnp.sum(ds, axis=1, keepdims=True)
                ds_ref[r, :] = ds.astype(bf16)
                return carry

            for c in range(t // rp):
                rows_of(c, 0)
            dq_ref[...] += jnp.dot(ds_ref[...], kb, preferred_element_type=f32)

        def below(j, carry):
            step(j, False)
            return carry

        lax.fori_loop(0, i, below, 0)
        step(i, True)
        dq_ref[...] = dq_ref[...] * scale

    in_specs = [pl.BlockSpec((t, LANE), lambda h, i: (i, qo + h)),
                pl.BlockSpec((s_len, LANE), lambda h, i: (0, ko + h)),
                pl.BlockSpec((s_len, LANE), lambda h, i: (0, vo + h)),
                pl.BlockSpec((t, LANE), lambda h, i: (i, h)),
                pl.BlockSpec((t, LANE), lambda h, i: (i, h)),
                pl.BlockSpec((None, t, 1), lambda h, i: (h, i, 0))]
    args = [qa, ka, va, o, do, lse]
    if decay:
        in_specs.append(pl.BlockSpec((None, nt, 1, t), lambda h, i: (h, 0, 0, 0)))
        args.append(c_row)
    col = pl.BlockSpec((None, t, 1), lambda h, i: (h, i, 0))
    return pl.pallas_call(
        kern, name=name, grid=(HEADS, nt), in_specs=in_specs,
        out_specs=[pl.BlockSpec((t, LANE), lambda h, i: (i, h)), pl.BlockSpec((None, None, 1, t), lambda h, i: (h, i, 0, 0)), col],
        out_shape=[jax.ShapeDtypeStruct((s_len, HEADS * LANE), f32), jax.ShapeDtypeStruct((HEADS, nt, 1, t), f32),
                   jax.ShapeDtypeStruct((HEADS, s_len, 1), f32)],
        scratch_shapes=[pltpu.VMEM((t, 1), f32), pltpu.VMEM((t, t), f32), pltpu.VMEM((t, t), f32), pltpu.VMEM((t, t), bf16)],
        compiler_params=pltpu.CompilerParams(dimension_semantics=("parallel", "arbitrary")))(*args)


def _attn_dkv(name, q, k, v, do, lse, delta, scale, c_col=None):
    (qa, qo), (ka, ko), (va, vo) = q, k, v
    s_len = qa.shape[0]
    tq, tk = _att_tiles(s_len)
    assert lse.shape == (HEADS, s_len // tq, 1, tq), (lse.shape, tq)
    nq, per = s_len // tq, tk // tq
    decay = c_col is not None
    q_mul, s_mul = _fold_scale(scale)

    def kern(*refs):
        q_ref, k_ref, v_ref, do_ref, lse_ref, delta_ref = refs[:6]
        j = pl.program_id(1)
        kb = k_ref[...].astype(bf16)
        vb = v_ref[...].astype(bf16)
        ck = refs[6][...] if decay else None

        def step(i, carry, diagonal):
            dk, dv, dsum = carry
            for d in range(per):
                tile = i * per + d
                rows = pl.ds(pl.multiple_of(tile * tq, tq), tq)
                qb = (q_ref[rows, :] * q_mul).astype(bf16)
                dob = do_ref[rows, :].astype(bf16)
                s = _scores_t(kb, qb.T, s_mul, ck, d * tq if diagonal else None, tq, tk)
                p = jnp.exp(s - lse_ref[tile])
                dv = dv + jnp.dot(p.astype(bf16), dob, preferred_element_type=f32)
                dp = jnp.dot(vb, dob.T, preferred_element_type=f32)
                ds = p * (dp - delta_ref[tile])
                dk = dk + jnp.dot(ds.astype(bf16), qb, preferred_element_type=f32)
                if decay:
                    dsum = dsum + ds
            return dk, dv, dsum

        init = (jnp.zeros((tk, LANE), f32), jnp.zeros((tk, LANE), f32), jnp.zeros((tk, tq), f32))
        dk, dv, dsum = lax.fori_loop(j + 1, s_len // tk, lambda i, c: step(i, c, False), step(j, init, True))
        if decay:
            dk_ref, dv_ref, dc_ref = refs[-3:]
            dc_ref[...] = -jnp.sum(dsum, axis=1, keepdims=True)
        else:
            dk_ref, dv_ref = refs[-2:]
        dk_ref[...] = dk * s_mul
        dv_ref[...] = dv

    stat = pl.BlockSpec((None, nq, 1, tq), lambda h, j: (h, 0, 0, 0))
    in_specs = [pl.BlockSpec((s_len, LANE), lambda h, j: (0, qo + h)),
                pl.BlockSpec((tk, LANE), lambda h, j: (j, ko + h)),
                pl.BlockSpec((tk, LANE), lambda h, j: (j, vo + h)),
                pl.BlockSpec((s_len, LANE), lambda h, j: (0, h)), stat, stat]
    args = [qa, ka, va, do, lse, delta]
    out_specs = [pl.BlockSpec((tk, LANE), lambda h, j: (j, h)), pl.BlockSpec((tk, LANE), lambda h, j: (j, h))]
    out_shape = [jax.ShapeDtypeStruct((s_len, HEADS * LANE), f32), jax.ShapeDtypeStruct((s_len, HEADS * LANE), f32)]
    if decay:
        in_specs.append(pl.BlockSpec((None, tk, 1), lambda h, j: (h, j, 0)))
        args.append(c_col)
        out_specs.append(pl.BlockSpec((None, tk, 1), lambda h, j: (h, j, 0)))
        out_shape.append(jax.ShapeDtypeStruct((HEADS, s_len, 1), f32))
    return pl.pallas_call(
        kern, name=name, grid=(HEADS, s_len // tk), in_specs=in_specs, out_specs=out_specs, out_shape=out_shape,
        compiler_params=pltpu.CompilerParams(dimension_semantics=("parallel", "arbitrary")))(*args)


CONV_TS, CONV_CB = 1024, 256
FFN_ROWS = 64


def _conv_fwd(name, x, w, b, taps):
    xa, width, xidx = _view(x)
    s_len = xa.shape[0]
    ts, cb = min(CONV_TS, s_len), CONV_CB
    xo = xidx * width // cb

    def kern(x_ref, halo_ref, w_ref, b_ref, o_ref):
        i = pl.program_id(1)
        xb = x_ref[...]
        halo = jnp.where(i == 0, 0.0, halo_ref[...])
        xx = jnp.concatenate([halo, xb], axis=0)
        out = b_ref[...] + w_ref[taps - 1:taps, :] * xb
        for k in range(taps - 1):
            out = out + w_ref[k:k + 1, :] * pltpu.roll(xx, taps - 1 - k, 0)[SUBLANE:]
        o_ref[...] = out

    return pl.pallas_call(
        kern, name=name, grid=(width // cb, s_len // ts),
        in_specs=[pl.BlockSpec((ts, cb), lambda j, i: (i, xo + j)),
                  pl.BlockSpec((SUBLANE, cb), lambda j, i: (jnp.maximum(i * (ts // SUBLANE) - 1, 0), xo + j)),
                  pl.BlockSpec((taps, cb), lambda j, i: (0, j)),
                  pl.BlockSpec((1, cb), lambda j, i: (0, j))],
        out_specs=pl.BlockSpec((ts, cb), lambda j, i: (i, j)),
        out_shape=jax.ShapeDtypeStruct((s_len, width), f32),
        compiler_params=pltpu.CompilerParams(dimension_semantics=("parallel", "parallel")))(xa, xa, w, b)


def _conv_bwd(name, x, dout, w, taps, dout2=None, dx_dtype=f32):
    xa, width, xidx = _view(x)
    s_len = xa.shape[0]
    ts, cb = min(CONV_TS, s_len), CONV_CB
    xo = xidx * width // cb
    n_i = s_len // ts
    two = dout2 is not None

    def kern(*refs):
        x_ref, halo_ref, w_ref = refs[:3]
        dx_ref, dw_ref, db_ref = refs[-3:]
        i = pl.program_id(1)
        if two:
            d = refs[3][...] + refs[5][...]
            dn = refs[4][...] + refs[6][...]
        else:
            d, dn = refs[3][...], refs[4][...]
        dn = jnp.where(i == n_i - 1, 0.0, dn)
        xb = x_ref[...]
        halo = jnp.where(i == 0, 0.0, halo_ref[...])
        xx = jnp.concatenate([halo, xb], axis=0)
        dd = jnp.concatenate([d, dn], axis=0)

        @pl.when(i == 0)
        def _():
            dw_ref[...] = jnp.zeros_like(dw_ref)
            db_ref[...] = jnp.zeros_like(db_ref)

        dx = w_ref[taps - 1:taps, :] * d
        dw_ref[taps - 1:taps, :] += jnp.sum(d * xb, axis=0, keepdims=True)
        for k in range(taps - 1):
            sh = taps - 1 - k
            dx = dx + w_ref[k:k + 1, :] * pltpu.roll(dd, ts + SUBLANE - sh, 0)[:ts]
            dw_ref[k:k + 1, :] += jnp.sum(d * pltpu.roll(xx, sh, 0)[SUBLANE:], axis=0, keepdims=True)
        dx_ref[...] = dx.astype(dx_ref.dtype)
        db_ref[...] += jnp.sum(d, axis=0, keepdims=True)

    d_spec = pl.BlockSpec((ts, cb), lambda j, i: (i, j))
    dn_spec = pl.BlockSpec((SUBLANE, cb), lambda j, i: (jnp.minimum((i + 1) * (ts // SUBLANE), s_len // SUBLANE - 1), j))
    in_specs = [pl.BlockSpec((ts, cb), lambda j, i: (i, xo + j)),
                pl.BlockSpec((SUBLANE, cb), lambda j, i: (jnp.maximum(i * (ts // SUBLANE) - 1, 0), xo + j)),
                pl.BlockSpec((taps, cb), lambda j, i: (0, j)), d_spec, dn_spec]
    args = [xa, xa, w, dout, dout]
    if two:
        in_specs += [d_spec, dn_spec]
        args += [dout2, dout2]
    return pl.pallas_call(
        kern, name=name, grid=(width // cb, n_i), in_specs=in_specs,
        out_specs=[pl.BlockSpec((ts, cb), lambda j, i: (i, j)), pl.BlockSpec((taps, cb), lambda j, i: (0, j)),
                   pl.BlockSpec((1, cb), lambda j, i: (0, j))],
        out_shape=[jax.ShapeDtypeStruct((s_len, width), dx_dtype), jax.ShapeDtypeStruct((taps, width), f32),
                   jax.ShapeDtypeStruct((1, width), f32)],
        compiler_params=pltpu.CompilerParams(dimension_semantics=("parallel", "arbitrary")))(*args)


def _conv_rows(xx, w_ref, b_ref, taps):
    out = b_ref[...] + w_ref[taps - 1:taps, :] * xx[SUBLANE:]
    for k in range(taps - 1):
        out = out + w_ref[k:k + 1, :] * pltpu.roll(xx, taps - 1 - k, 0)[SUBLANE:]
    return out


def _ffn_act_fwd(name, up, w, b):
    s_len = up.shape[0]
    ts, cb = min(CONV_TS, s_len), CONV_CB
    nf = D_FF // cb

    def kern(g_ref, gp_ref, v_ref, vp_ref, wg_ref, wv_ref, bg_ref, bv_ref, o_ref):
        first = pl.program_id(1) == 0
        ug = _conv_rows(jnp.concatenate([jnp.where(first, 0.0, gp_ref[...]), g_ref[...]], axis=0), wg_ref, bg_ref, FFN_CONV)
        uv = _conv_rows(jnp.concatenate([jnp.where(first, 0.0, vp_ref[...]), v_ref[...]], axis=0), wv_ref, bv_ref, FFN_CONV)
        o_ref[...] = (jax.nn.silu(ug) * uv).astype(o_ref.dtype)

    def half(off):
        return [pl.BlockSpec((ts, cb), lambda j, i: (i, off + j)),
                pl.BlockSpec((SUBLANE, cb), lambda j, i: (jnp.maximum(i * (ts // SUBLANE) - 1, 0), off + j))]

    def par(rows, off):
        return pl.BlockSpec((rows, cb), lambda j, i: (0, off + j))

    return pl.pallas_call(
        kern, name=name, grid=(nf, s_len // ts),
        in_specs=half(0) + half(nf) + [par(FFN_CONV, 0), par(FFN_CONV, nf), par(1, 0), par(1, nf)],
        out_specs=pl.BlockSpec((ts, cb), lambda j, i: (i, j)),
        out_shape=jax.ShapeDtypeStruct((s_len, D_FF), bf16),
        compiler_params=pltpu.CompilerParams(dimension_semantics=("parallel", "parallel")))(up, up, up, up, w, w, b, b)


def _ffn_act_bwd(name, up, dact, w, b):
    s_len = up.shape[0]
    ts, cb = min(CONV_TS, s_len), CONV_CB
    nf = D_FF // cb
    n_i = s_len // ts
    taps = FFN_CONV

    ch = min(FFN_ROWS, ts)

    def kern(g_ref, gp_ref, gn_ref, v_ref, vp_ref, vn_ref, d_ref, dn_ref, wg_ref, wv_ref, bg_ref, bv_ref,
             dg_ref, dv_ref, dwg_ref, dwv_ref, dbg_ref, dbv_ref, gx_ref, vx_ref, dd_ref):
        i = pl.program_id(1)
        first, last = i == 0, i == n_i - 1
        for x_ref, p_ref, n_ref, ext in ((g_ref, gp_ref, gn_ref, gx_ref), (v_ref, vp_ref, vn_ref, vx_ref)):
            ext[:SUBLANE, :] = jnp.where(first, 0.0, p_ref[...])
            ext[SUBLANE:SUBLANE + ts, :] = x_ref[...]
            ext[SUBLANE + ts:, :] = jnp.where(last, 0.0, n_ref[...])
        dd_ref[:ts, :] = d_ref[...]
        dd_ref[ts:, :] = jnp.where(last, 0.0, dn_ref[...])

        @pl.when(first)
        def _():
            for ref in (dwg_ref, dwv_ref, dbg_ref, dbv_ref):
                ref[...] = jnp.zeros_like(ref)

        def rows_of(c, carry):
            r0 = pl.multiple_of(c * ch, ch)
            gx, vx = gx_ref[pl.ds(r0, ch + 2 * SUBLANE), :], vx_ref[pl.ds(r0, ch + 2 * SUBLANE), :]
            ug, uv = _conv_rows(gx, wg_ref, bg_ref, taps), _conv_rows(vx, wv_ref, bv_ref, taps)
            dd = dd_ref[pl.ds(r0, ch + SUBLANE), :]
            sg = jax.nn.sigmoid(ug)
            out = []
            for du, xx, w_ref, dx_ref, sums in ((dd * uv * (sg * (1.0 + ug * (1.0 - sg))), gx, wg_ref, dg_ref, carry[0]),
                                                (dd * (ug * sg), vx, wv_ref, dv_ref, carry[1])):
                d = du[:ch]
                dx = w_ref[taps - 1:taps, :] * d
                new = [None] * (taps + 1)
                new[taps - 1] = sums[taps - 1] + jnp.sum(d * xx[SUBLANE:SUBLANE + ch], axis=0, keepdims=True)
                for k in range(taps - 1):
                    sh = taps - 1 - k
                    dx = dx + w_ref[k:k + 1, :] * pltpu.roll(du, ch + SUBLANE - sh, 0)[:ch]
                    new[k] = sums[k] + jnp.sum(d * pltpu.roll(xx, sh, 0)[SUBLANE:SUBLANE + ch], axis=0, keepdims=True)
                new[taps] = sums[taps] + jnp.sum(d, axis=0, keepdims=True)
                dx_ref[pl.ds(r0, ch), :] = dx.astype(dx_ref.dtype)
                out.append(tuple(new))
            return tuple(out)

        zero = tuple(jnp.zeros((1, cb), f32) for _ in range(taps + 1))
        sums_g, sums_v = lax.fori_loop(0, ts // ch, rows_of, (zero, zero))
        for sums, dw_ref, db_ref in ((sums_g, dwg_ref, dbg_ref), (sums_v, dwv_ref, dbv_ref)):
            for k in range(taps):
                dw_ref[k:k + 1, :] += sums[k]
            db_ref[...] += sums[taps]

    blocks = s_len // SUBLANE

    def half(off):
        return [pl.BlockSpec((ts, cb), lambda j, i: (i, off + j)),
                pl.BlockSpec((SUBLANE, cb), lambda j, i: (jnp.maximum(i * (ts // SUBLANE) - 1, 0), off + j)),
                pl.BlockSpec((SUBLANE, cb), lambda j, i: (jnp.minimum((i + 1) * (ts // SUBLANE), blocks - 1), off + j))]

    def par(rows, off):
        return pl.BlockSpec((rows, cb), lambda j, i: (0, off + j))

    d_specs = [pl.BlockSpec((ts, cb), lambda j, i: (i, j)),
               pl.BlockSpec((SUBLANE, cb), lambda j, i: (jnp.minimum((i + 1) * (ts // SUBLANE), blocks - 1), j))]
    out_par = [pl.BlockSpec((r, cb), lambda j, i: (0, j)) for r in (taps, taps, 1, 1)]
    return pl.pallas_call(
        kern, name=name, grid=(nf, n_i),
        in_specs=half(0) + half(nf) + d_specs + [par(taps, 0), par(taps, nf), par(1, 0), par(1, nf)],
        out_specs=[pl.BlockSpec((ts, cb), lambda j, i: (i, j))] * 2 + out_par,
        out_shape=[jax.ShapeDtypeStruct((s_len, D_FF), bf16)] * 2 + [jax.ShapeDtypeStruct((taps, D_FF), f32)] * 2
        + [jax.ShapeDtypeStruct((1, D_FF), f32)] * 2,
        scratch_shapes=[pltpu.VMEM((ts + 2 * SUBLANE, cb), f32)] * 2 + [pltpu.VMEM((ts + SUBLANE, cb), f32)],
        compiler_params=pltpu.CompilerParams(dimension_semantics=("parallel", "arbitrary")))(
        up, up, up, up, up, up, dact, dact, w, w, b, b)


SCAN_ROWS = 128


def _block_scan(a, b, reverse):
    t = a.shape[0]
    row = lax.broadcasted_iota(jnp.int32, a.shape, 0)
    d = 1
    while d < t:
        keep = row < t - d if reverse else row >= d
        shift = t - d if reverse else d
        a_far = jnp.where(keep, pltpu.roll(a, shift, 0), 1.0)
        b_far = jnp.where(keep, pltpu.roll(b, shift, 0), 0.0)
        b = a * b_far + b
        a = a * a_far
        d *= 2
    return a, b


def _scan_fwd(name, a, b):
    s_len, width = a.shape
    t = min(SCAN_ROWS, s_len)

    def kern(a_ref, b_ref, h_ref):
        def block(k, carry):
            rows = pl.ds(pl.multiple_of(k * t, t), t)
            acc, h = _block_scan(a_ref[rows, :], b_ref[rows, :], False)
            h_ref[rows, :] = h + acc * carry
            return h_ref[pl.ds(k * t + t - 1, 1), :]

        lax.fori_loop(0, s_len // t, block, jnp.zeros((1, LANE), f32))

    spec = pl.BlockSpec((s_len, LANE), lambda j: (0, j))
    return pl.pallas_call(
        kern, name=name, grid=(width // LANE,), in_specs=[spec, spec], out_specs=spec,
        out_shape=jax.ShapeDtypeStruct((s_len, width), f32),
        compiler_params=pltpu.CompilerParams(dimension_semantics=("parallel",)))(a, b)


def _scan_bwd(name, a_next, h_prev, dh):
    s_len, width = dh.shape
    t = min(SCAN_ROWS, s_len)
    n_blocks = s_len // t

    def kern(an_ref, hp_ref, dh_ref, da_ref, db_ref):
        def block(kk, carry):
            k = n_blocks - 1 - kk
            rows = pl.ds(pl.multiple_of(k * t, t), t)
            acc, g = _block_scan(an_ref[rows, :], dh_ref[rows, :], True)
            g = g + acc * carry
            db_ref[rows, :] = g
            da_ref[rows, :] = g * hp_ref[rows, :]
            return db_ref[pl.ds(k * t, 1), :]

        lax.fori_loop(0, n_blocks, block, jnp.zeros((1, LANE), f32))

    spec = pl.BlockSpec((s_len, LANE), lambda j: (0, j))
    return pl.pallas_call(
        kern, name=name, grid=(width // LANE,), in_specs=[spec, spec, spec], out_specs=[spec, spec],
        out_shape=[jax.ShapeDtypeStruct((s_len, width), f32)] * 2,
        compiler_params=pltpu.CompilerParams(dimension_semantics=("parallel",)))(a_next, h_prev, dh)


def _lane_cumsum(x, reverse):
    n = x.shape[1]
    lane = lax.broadcasted_iota(jnp.int32, x.shape, 1)
    sh = 1
    while sh < n:
        if reverse:
            x = x + jnp.where(lane < n - sh, pltpu.roll(x, n - sh, 1), 0.0)
        else:
            x = x + jnp.where(lane >= sh, pltpu.roll(x, sh, 1), 0.0)
        sh *= 2
    return x


def _decay_fwd(name, fl_t, b8):
    def kern(f_ref, b_ref, c_ref):
        c_ref[...] = _lane_cumsum(jax.nn.log_sigmoid(f_ref[...] + b_ref[...]), False)

    return pl.pallas_call(kern, name=name, out_shape=jax.ShapeDtypeStruct(fl_t.shape, f32))(fl_t, b8)


def _decay_bwd(name, fl_t, b8, dc_key, dc_query):
    def kern(f_ref, b_ref, dck_ref, dcq_ref, df_ref, db_ref):
        dlogf = _lane_cumsum(dck_ref[...] + dcq_ref[...], True)
        df = dlogf * jax.nn.sigmoid(-(f_ref[...] + b_ref[...]))
        df_ref[...] = df
        db_ref[...] = jnp.sum(df, axis=1, keepdims=True)

    return pl.pallas_call(kern, name=name, out_shape=[jax.ShapeDtypeStruct(fl_t.shape, f32),
                                                      jax.ShapeDtypeStruct((SUBLANE, 1), f32)])(fl_t, b8, dc_key, dc_query)


def _rms(x, g, n):
    return x * lax.rsqrt(jnp.sum(x * x, axis=-1, keepdims=True) * (1.0 / n) + EPS) * g


def _loss_head(name, h, target, g, tb=512):
    n, d = h.shape
    tb = min(tb, n)

    def kern(h_ref, t_ref, g_ref, loss_ref, dh_ref, dg_ref):
        i = pl.program_id(0)
        tgt = t_ref[...]

        def f(hv, gv):
            err = _rms(hv, gv, d) - tgt
            return 0.5 * jnp.sum(jnp.sum(err * err, axis=-1, keepdims=True) * (1.0 / d), axis=0, keepdims=True)

        val, vjp = jax.vjp(f, h_ref[...], g_ref[...])
        dh, dg = vjp(jnp.ones((1, 1), f32))
        dh_ref[...] = dh

        @pl.when(i == 0)
        def _():
            loss_ref[...] = jnp.zeros_like(loss_ref)
            dg_ref[...] = jnp.zeros_like(dg_ref)

        loss_ref[...] += val
        dg_ref[...] += dg

    return pl.pallas_call(
        kern, name=name, grid=(n // tb,),
        in_specs=[pl.BlockSpec((tb, d), lambda i: (i, 0)), pl.BlockSpec((tb, d), lambda i: (i, 0)),
                  pl.BlockSpec((1, d), lambda i: (0, 0))],
        out_specs=[pl.BlockSpec((1, 1), lambda i: (0, 0)), pl.BlockSpec((tb, d), lambda i: (i, 0)),
                   pl.BlockSpec((1, d), lambda i: (0, 0))],
        out_shape=[jax.ShapeDtypeStruct((1, 1), f32), jax.ShapeDtypeStruct((n, d), f32), jax.ShapeDtypeStruct((1, d), f32)],
        compiler_params=pltpu.CompilerParams(dimension_semantics=("arbitrary",)))(h, target, g)


def _f_norm(x, g):
    return (_rms(x, g, D_MODEL),)


def _f_latent(qc, kvc, gq, gkv):
    return _rms(qc, gq, MLA_Q_RANK), _rms(kvc, gkv, MLA_KV_RANK)


def _f_rope_table(pos, freq, m1, m2):
    ang = pos * freq
    sin = jnp.sin(ang)
    return jnp.cos(ang), -sin * m1, sin * m2


def _rope(x, cos, s_up, s_down):
    w = x.shape[1]
    return x * cos + _roll(x, w - MLA_ROPE // 2, 1) * s_up + _roll(x, MLA_ROPE // 2, 1) * s_down


def _f_mla_prep(q, kpart, kr, cos, s_up, s_down):
    def heads(t):
        return jnp.concatenate([t] * HEADS, axis=1)

    kr = _rope(kr, cos, s_up, s_down)
    return _rope(q, heads(cos), heads(s_up), heads(s_down)), kpart + heads(kr)


def _f_lru_gate(gates, xc, b_r, b_i, lam):
    r = jax.nn.sigmoid(gates[:, :LRU_WIDTH] + b_r)
    i = jax.nn.sigmoid(gates[:, LRU_WIDTH:] + b_i)
    log_a = -LRU_C * r * jax.nn.softplus(-lam)
    mult = jnp.sqrt(-jnp.tanh(log_a) * (1.0 + jnp.exp(2.0 * log_a)))
    return jnp.exp(log_a), mult * (i * xc)


def _f_merge(o_mla, o_fox, hs, lg, g):
    o_lru = hs * jax.nn.gelu(lg)
    return (jnp.concatenate([_rms(o_mla, g[:, :512], HEADS * MLA_V), _rms(o_fox, g[:, 512:1024], HEADS * FOX_HEAD_DIM),
                             _rms(o_lru, g[:, 1024:], LRU_WIDTH)], axis=1),)


def _f_ffn_gate(u):
    return (jax.nn.silu(u[:, :D_FF]) * u[:, D_FF:],)


def _f_ple(h, gpre, pp):
    return (h + jax.nn.sigmoid(gpre) * pp,)


MIX_PART = ["w_in", "w_uq", "w_ukv", "lru_conv_w"]
FFN_PART = ["w_o", "w_up", "ffn_conv_w", "w_down", "w_ple_gate", "w_ple_proj"]


def _prep_mix_weights(w):
    eye = jnp.eye(LRU_BLOCKS, dtype=f32)

    def block_diag(m):
        return (eye[:, None, :, None] * m[:, :, None, :]).reshape(LRU_WIDTH, LRU_WIDTH)

    return dict(
        w_in=_take_pad(w["w_in"], Z_MAP, 1),
        w_uq=_take_pad(_take_pad(w["w_uq"], UQ_COL_MAP, 1), UQ_ROW_MAP, 0),
        w_ukv=_take_pad(w["w_ukv"], UKV_MAP, 1),
        w_ri=jnp.concatenate([block_diag(w["w_r"]), block_diag(w["w_i"])], axis=1).astype(bf16),
        g_mix=w["g_mix"].reshape(1, -1), g_ffn=w["g_ffn"].reshape(1, -1), g_ple=w["g_ple"].reshape(1, -1),
        g_qc=_take_pad(w["g_qc"], UQ_ROW_MAP, 0).reshape(1, -1), g_kvc=w["g_kvc"].reshape(1, -1),
        g_out=_take_pad(w["g_out"], OMIX_MAP, 0).reshape(1, -1),
        b_f8=_take_pad(w["b_f"], _pad_to(np.arange(FOX_HEADS), SUBLANE), 0).reshape(SUBLANE, 1),
        lru_conv_w=w["lru_conv_w"], lru_conv_b=w["lru_conv_b"].reshape(1, -1),
        b_r=w["b_r"].reshape(1, -1), b_i=w["b_i"].reshape(1, -1), lam=w["lru_lambda"].reshape(1, -1),
        ffn_conv_b=w["ffn_conv_b"].reshape(1, -1),
    )


def _prep_ffn_weights(w):
    return dict(w_o=_take_pad(w["w_o"], OMIX_MAP, 0),
                w_up=w["w_up"], w_up_g=w["w_up"][:, :D_FF], w_up_v=w["w_up"][:, D_FF:], ffn_conv_w=w["ffn_conv_w"],
                w_down=w["w_down"], w_ple_gate=w["w_ple_gate"], w_ple_proj=w["w_ple_proj"])


def _rope_rows(pos):
    consts = [jnp.asarray(t) for t in _rope_tables(LANE, ROPE_AT)]
    return _rowwise("rope_table", _f_rope_table, [pos], consts, [(LANE, f32)] * 3)


def _key_decay(c_t, s_len):
    t = _att_tiles(s_len)[1]
    return c_t[:HEADS].reshape(HEADS, s_len // t, 1, t), c_t[:HEADS].reshape(HEADS, s_len, 1)


def _layer_fwd(l, h0, p_l, rope, weights_of):
    s_len = h0.shape[0]
    n = f"l{l}_"
    w = _prep_mix_weights(weights_of("mix", h0))
    xn, = _rowwise(n + "norm_mix", _f_norm, [h0], [w["g_mix"]], [(D_MODEL, bf16)])
    z = _mm(n + "in_proj", xn, w["w_in"])
    zq = (z, QC_W, Z_QC // QC_W)
    zkv = (z, LANE, Z_KVC // LANE)
    zkr = (z, LANE, Z_KR // LANE)
    zlx = (z, LRU_WIDTH, Z_LX // LRU_WIDTH)
    zlg = (z, LRU_WIDTH, Z_LG // LRU_WIDTH)
    qcn, kvn = _rowwise(n + "latent_norm", _f_latent, [zq, zkv], [w["g_qc"], w["g_kvc"]], [(QC_W, bf16), (LANE, bf16)])
    q = _mm(n + "uq", qcn, w["w_uq"])
    kv = _mm(n + "ukv", kvn, w["w_ukv"])
    kpart = (kv, HEADS * LANE, 0)
    qr, kk = _rowwise(n + "mla_prep", _f_mla_prep, [q, kpart, zkr, *rope], [],
                      [(HEADS * LANE, bf16), (HEADS * LANE, bf16)])
    mla_scale = (MLA_NOPE + MLA_ROPE) ** -0.5
    o_mla, lse_m, lse_m_row = _attn_fwd(n + "mla_fwd", (qr, 0), (kk, 0), (kv, HEADS), mla_scale)
    fl_t = z[:, Z_FL:Z_FL + SUBLANE].T
    c_t = _decay_fwd(n + "decay", fl_t, w["b_f8"])
    c_row, c_col = _key_decay(c_t, s_len)
    fox_scale = FOX_HEAD_DIM ** -0.5
    o_fox, lse_f, lse_f_row = _attn_fwd(n + "fox_fwd", (z, Z_FQ // LANE), (z, Z_FK // LANE), (z, Z_FV // LANE), fox_scale, c_row)
    xc = _conv_fwd(n + "lru_conv", zlx, w["lru_conv_w"], w["lru_conv_b"], LRU_CONV)
    gates = _mm(n + "lru_gates", xc, w["w_ri"])
    a, bx = _rowwise(n + "lru_gate", _f_lru_gate, [gates, xc], [w["b_r"], w["b_i"], w["lam"]],
                     [(LRU_WIDTH, f32), (LRU_WIDTH, f32)])
    hs = _scan_fwd(n + "lru_scan", a, bx)
    ocat, = _rowwise(n + "merge", _f_merge, [o_mla, o_fox, hs, zlg], [w["g_out"]], [(OMIX_W, bf16)])
    w.update(_prep_ffn_weights(weights_of("ffn", ocat)))
    h1 = _mm(n + "out_proj", ocat, w["w_o"], res=h0)
    xn2, = _rowwise(n + "norm_ffn", _f_norm, [h1], [w["g_ffn"]], [(D_MODEL, bf16)])
    up = _mm(n + "up_proj", xn2, w["w_up"])
    act = _ffn_act_fwd(n + "ffn_act", up, w["ffn_conv_w"], w["ffn_conv_b"])
    h2 = _mm(n + "down_proj", act, w["w_down"], res=h1)
    hn, = _rowwise(n + "norm_ple", _f_norm, [h2], [w["g_ple"]], [(D_MODEL, bf16)])
    gpre = _mm(n + "ple_gate", hn, w["w_ple_gate"])
    pp = _mm(n + "ple_proj", p_l, w["w_ple_proj"])
    h3, = _rowwise(n + "ple_mix", _f_ple, [h2, gpre, pp], [], [(D_MODEL, f32)])
    res = dict(h0=h0, xn=xn, z=z, qcn=qcn, kvn=kvn, q=q, kv=kv, qr=qr, kk=kk, o_mla=o_mla, lse_m=lse_m, fl_t=fl_t,
               lse_m_row=lse_m_row, lse_f_row=lse_f_row, c_row=c_row, c_col=c_col, o_fox=o_fox, lse_f=lse_f, xc=xc, gates=gates, a=a, hs=hs, ocat=ocat, h1=h1,
               xn2=xn2, up=up, act=act, h2=h2, hn=hn, gpre=gpre, pp=pp, p_l=p_l)
    return h3, res, w


def _layer_bwd(l, dh3, r, rope, w, token, grads_to):
    s_len = dh3.shape[0]
    n = f"l{l}_"
    g = {}
    w = dict(w, g_ple=w["g_ple"] + token)
    z = r["z"]
    zq = (z, QC_W, Z_QC // QC_W)
    zkv = (z, LANE, Z_KVC // LANE)
    zkr = (z, LANE, Z_KR // LANE)
    zlx = (z, LRU_WIDTH, Z_LX // LRU_WIDTH)
    zlg = (z, LRU_WIDTH, Z_LG // LRU_WIDTH)
    (dh2a, dgpre, dpp), _ = _rowwise_bwd(n + "ple_mix_b", _f_ple, [r["h2"], r["gpre"], r["pp"]], [], [dh3], 3,
                                         dts=[f32, bf16, bf16])
    g["w_ple_proj"] = _mm(n + "ple_proj_dw", r["p_l"], dpp, "tn", bf16)
    dhn = _mm(n + "ple_gate_dx", dgpre, w["w_ple_gate"], "nt")
    g["w_ple_gate"] = _mm(n + "ple_gate_dw", r["hn"], dgpre, "tn", bf16)
    (dh2,), (g["g_ple"],) = _rowwise_bwd(n + "norm_ple_b", _f_norm, [r["h2"]], [w["g_ple"]], [dhn], 1, adds={0: dh2a})
    dact = _mm(n + "down_dx", dh2, w["w_down"], "nt")
    g["w_down"] = _mm(n + "down_dw", r["act"], dh2, "tn", bf16)
    dup_g, dup_v, dcw_g, dcw_v, dcb_g, dcb_v = _ffn_act_bwd(n + "ffn_act_b", r["up"], dact, w["ffn_conv_w"], w["ffn_conv_b"])
    g["ffn_conv_w"] = jnp.concatenate([dcw_g, dcw_v], axis=1)
    g["ffn_conv_b"] = jnp.concatenate([dcb_g, dcb_v], axis=1)
    dxn2 = _mm(n + "up_dx_v", dup_v, w["w_up_v"], "nt", res=_mm(n + "up_dx_g", dup_g, w["w_up_g"], "nt"))
    g["w_up"] = jnp.concatenate([_mm(n + "up_dw_g", r["xn2"], dup_g, "tn", bf16),
                                 _mm(n + "up_dw_v", r["xn2"], dup_v, "tn", bf16)], axis=1)
    (dh1,), (g["g_ffn"],) = _rowwise_bwd(n + "norm_ffn_b", _f_norm, [r["h1"]], [w["g_ffn"]], [dxn2], 1, adds={0: dh2})
    docat = _mm(n + "out_dx", dh1, w["w_o"], "nt")
    g["w_o"] = _mm(n + "out_dw", r["ocat"], dh1, "tn", bf16)
    token = grads_to("ffn", dict(w_o=_take_inv(g["w_o"], OMIX_MAP, 0), w_up=g["w_up"], ffn_conv_w=g["ffn_conv_w"],
                                 w_down=g["w_down"], w_ple_gate=g["w_ple_gate"], w_ple_proj=g["w_ple_proj"]))
    w = dict(w, g_out=w["g_out"] + token)
    (do_mla, do_fox, dhs, dlg), (g["g_out"],) = _rowwise_bwd(
        n + "merge_b", _f_merge, [r["o_mla"], r["o_fox"], r["hs"], zlg], [w["g_out"]], [docat], 4)
    a, hs = r["a"], r["hs"]
    a_next = jnp.concatenate([a[1:], jnp.zeros((1, LRU_WIDTH), f32)], axis=0)
    h_prev = jnp.concatenate([jnp.zeros((1, LRU_WIDTH), f32), hs[:-1]], axis=0)
    da, dbx = _scan_bwd(n + "lru_scan_b", a_next, h_prev, dhs)
    (dgates, dxc_a), (g["b_r"], g["b_i"], g["lam"]) = _rowwise_bwd(
        n + "lru_gate_b", _f_lru_gate, [r["gates"], r["xc"]], [w["b_r"], w["b_i"], w["lam"]], [da, dbx], 2,
        dts=[bf16, f32])
    dxc_b = _mm(n + "lru_gates_dx", dgates, w["w_ri"], "nt")
    g["w_ri"] = _mm(n + "lru_gates_dw", r["xc"], dgates, "tn")
    dlx, g["lru_conv_w"], g["lru_conv_b"] = _conv_bwd(n + "lru_conv_b", zlx, dxc_a, w["lru_conv_w"], LRU_CONV, dout2=dxc_b)
    fox_scale = FOX_HEAD_DIM ** -0.5
    fq, fk, fv = (z, Z_FQ // LANE), (z, Z_FK // LANE), (z, Z_FV // LANE)
    dfq, delta_f, dc_q = _attn_dq(n + "fox_dq", fq, fk, fv, r["o_fox"], do_fox, r["lse_f"], fox_scale, r["c_row"])
    dfk, dfv, dc_k = _attn_dkv(n + "fox_dkv", fq, fk, fv, do_fox, r["lse_f_row"], delta_f, fox_scale,
                               r["c_col"])
    pad_rows = jnp.zeros((SUBLANE - HEADS, s_len), f32)
    dfl_t, g["b_f8"] = _decay_bwd(n + "decay_b", r["fl_t"], w["b_f8"],
                                  jnp.concatenate([dc_k.reshape(HEADS, s_len), pad_rows], axis=0),
                                  jnp.concatenate([dc_q.reshape(HEADS, s_len), pad_rows], axis=0))
    dfl = jnp.pad(dfl_t.T, ((0, 0), (0, LANE - SUBLANE)))
    mla_scale = (MLA_NOPE + MLA_ROPE) ** -0.5
    qr, kk, kv = (r["qr"], 0), (r["kk"], 0), (r["kv"], HEADS)
    dqr, delta_m, _ = _attn_dq(n + "mla_dq", qr, kk, kv, r["o_mla"], do_mla, r["lse_m"], mla_scale)
    dkk, dv_m = _attn_dkv(n + "mla_dkv", qr, kk, kv, do_mla, r["lse_m_row"], delta_m, mla_scale)
    (dq, dkpart, dkr), _ = _rowwise_bwd(n + "mla_prep_b", _f_mla_prep, [r["q"], (r["kv"], HEADS * LANE, 0), zkr, *rope],
                                        [], [dqr, dkk], 3, dts=[bf16, bf16, f32])
    dkv = jnp.concatenate([dkpart, dv_m.astype(bf16)], axis=1)
    dkvn = _mm(n + "ukv_dx", dkv, w["w_ukv"], "nt")
    g["w_ukv"] = _mm(n + "ukv_dw", r["kvn"], dkv, "tn", bf16)
    dqcn = _mm(n + "uq_dx", dq, w["w_uq"], "nt")
    g["w_uq"] = _mm(n + "uq_dw", r["qcn"], dq, "tn", bf16)
    (dqc, dkvc), (g["g_qc"], g["g_kvc"]) = _rowwise_bwd(n + "latent_norm_b", _f_latent, [zq, zkv],
                                                        [w["g_qc"], w["g_kvc"]], [dqcn, dkvn], 2)
    dz = jnp.concatenate([t.astype(bf16) for t in (dfq, dfk, dfv, dlx, dlg, dqc, dkvc, dkr, dfl)], axis=1)
    dxn = _mm(n + "in_dx", dz, w["w_in"], "nt")
    g["w_in"] = _mm(n + "in_dw", r["xn"], dz, "tn", bf16)
    (dh0,), (g["g_mix"],) = _rowwise_bwd(n + "norm_mix_b", _f_norm, [r["h0"]], [w["g_mix"]], [dxn], 1, adds={0: dh1})
    return dh0, grads_to("mix", _unpad_mix_grads(g))


def _unpad_mix_grads(g):
    d_ri = g["w_ri"]
    idx = jnp.arange(LRU_BLOCKS)

    def diag_blocks(m):
        return m.reshape(LRU_BLOCKS, LRU_BLOCK, LRU_BLOCKS, LRU_BLOCK)[idx, :, idx, :]

    return dict(
        g_mix=g["g_mix"][0], w_in=_take_inv(g["w_in"], Z_MAP, 1), g_qc=g["g_qc"][0, :MLA_Q_RANK],
        w_uq=_take_inv(g["w_uq"][:MLA_Q_RANK], UQ_COL_MAP, 1), g_kvc=g["g_kvc"][0],
        w_ukv=_take_inv(g["w_ukv"], UKV_MAP, 1), b_f=g["b_f8"][:FOX_HEADS, 0],
        lru_conv_w=g["lru_conv_w"], lru_conv_b=g["lru_conv_b"][0],
        w_r=diag_blocks(d_ri[:, :LRU_WIDTH]), b_r=g["b_r"][0], w_i=diag_blocks(d_ri[:, LRU_WIDTH:]), b_i=g["b_i"][0],
        lru_lambda=g["lam"][0], g_out=_take_inv(g["g_out"][0], OMIX_MAP, 0),
        g_ffn=g["g_ffn"][0], ffn_conv_b=g["ffn_conv_b"][0], g_ple=g["g_ple"][0],
    )


LAYER_WEIGHTS = ["g_mix", "w_in", "g_qc", "w_uq", "g_kvc", "w_ukv", "b_f", "lru_conv_w", "lru_conv_b", "w_r", "b_r", "w_i",
                 "b_i", "lru_lambda", "g_out", "w_o", "g_ffn", "w_up", "ffn_conv_w", "ffn_conv_b", "w_down", "g_ple",
                 "w_ple_gate", "w_ple_proj"]
WEIGHTS = LAYER_WEIGHTS + ["g_final"]


def _local_step(x, p, pos, target, g_final, weights_of, grads_to):
    h = x
    rope = _rope_rows(pos)
    ws, saved = [], []
    for l in range(DEPTH):
        h, r, w = _layer_fwd(l, h, p[l], rope, functools.partial(weights_of, l))
        ws.append(w)
        saved.append(r)
    loss, dh, dg_final = _loss_head("loss_head", h, target, g_final.reshape(1, -1))
    token = jnp.zeros((), f32)
    for l in reversed(range(DEPTH)):
        dh, token = _layer_bwd(l, dh, saved[l], rope, ws[l], token, functools.partial(grads_to, l))
    return loss[0, 0], dh, dg_final[0]


MESH_AXES = ("x", "y", "c")


def _row_tile(rows, cap):
    if rows <= cap:
        return rows
    for t in range(cap, SUBLANE - 1, -SUBLANE):
        if rows % t == 0:
            return t
    return rows


ADAM_BLOCK_BYTES = 2 ** 20


def _adamw(name, w, g, m, v):
    rows, cols = w.shape
    tr = _row_tile(rows, max(SUBLANE, ADAM_BLOCK_BYTES // (4 * cols) // SUBLANE * SUBLANE))

    def kern(w_ref, g_ref, m_ref, v_ref, d_ref, nm_ref, nv_ref):
        gv = g_ref[...]
        nm = ADAM_B1 * m_ref[...] + (1.0 - ADAM_B1) * gv
        nv = ADAM_B2 * v_ref[...] + (1.0 - ADAM_B2) * (gv * gv)
        m_hat = nm / (1.0 - ADAM_B1 ** ADAM_STEP)
        v_hat = nv / (1.0 - ADAM_B2 ** ADAM_STEP)
        d_ref[...] = -ADAM_LR * (m_hat / (jnp.sqrt(v_hat) + ADAM_EPS) + ADAM_WD * w_ref[...])
        nm_ref[...] = nm
        nv_ref[...] = nv

    spec = pl.BlockSpec((tr, cols), lambda i: (i, 0))
    return pl.pallas_call(
        kern, name=name, grid=(rows // tr,), in_specs=[spec] * 4, out_specs=[spec] * 3,
        out_shape=[jax.ShapeDtypeStruct((rows, cols), f32)] * 3,
        compiler_params=pltpu.CompilerParams(dimension_semantics=("parallel",)))(w, g, m, v)


def _packed_rows(shape):
    return -(-int(np.prod(shape)) // (SUBLANE * LANE)) * SUBLANE


def _pack(arrays):
    rows = []
    for a in arrays:
        flat = a.reshape(-1)
        rows.append(jnp.pad(flat, (0, _packed_rows(a.shape) * LANE - flat.shape[0])).reshape(-1, LANE))
    return jnp.concatenate(rows, axis=0)


def _unpack(buf, shapes):
    out, at = [], 0
    for s in shapes:
        rows = _packed_rows(s)
        out.append(buf[at:at + rows].reshape(-1)[:int(np.prod(s))].reshape(s))
        at += rows
    return out


SHARD_AXIS = {"w_in": 2, "w_uq": 2, "w_ukv": 2, "lru_conv_w": 2, "w_o": 1, "w_up": 2, "ffn_conv_w": 2, "w_down": 1,
              "w_ple_gate": 1, "w_ple_proj": 2}
SHARDED = [k for k in WEIGHTS if k in SHARD_AXIS]
REPLICATED = [k for k in WEIGHTS if k not in SHARD_AXIS]
ELEMENTWISE_F32 = ("lru_conv_w", "ffn_conv_w")
N_SHARDS = 4
BF16_TILE_ROWS = 16


HBM_SPEC = pl.BlockSpec(memory_space=pl.ANY)
SEM_SPEC = pl.BlockSpec(memory_space=pltpu.SEMAPHORE)
SPLIT_EFFECT = pltpu.SideEffectType.DATAFLOW_SIDE_EFFECTING
CHIP_FLIPS = ((1, 0), (0, 1), (1, 1))
N_DEVICES = 8
SUM_BLOCK_BYTES = 4 * 2 ** 20


def _device_index():
    return 4 * lax.axis_index("x") + 2 * lax.axis_index("y") + lax.axis_index("c")


def _when(cond, fn):
    if cond is None:
        fn()
    else:
        pl.when(cond)(fn)


class _Exchange:
    def __init__(self, name, plan, srcs, land_shapes, n_send, n_recv):
        self.name, self.plan, self.srcs, self.n = name, plan, list(srcs), len(srcs)
        self.land_shapes, self.n_send, self.n_recv = land_shapes, n_send, n_recv

    def run(self):
        n = self.n

        def body(*refs):
            sends, arrivals = self.plan(refs[:n], refs[n:2 * n], refs[2 * n], refs[2 * n + 1])
            for cond, cp in sends:
                _when(cond, cp.start)
            for cond, cp in arrivals:
                _when(cond, cp.wait_recv)
            for cond, cp in sends:
                _when(cond, cp.wait_send)

        return pl.pallas_call(
            body, name=self.name, out_shape=self.land_shapes, in_specs=[HBM_SPEC] * n, out_specs=[HBM_SPEC] * n,
            scratch_shapes=[pltpu.SemaphoreType.DMA((self.n_send,)), pltpu.SemaphoreType.DMA((self.n_recv,))])(*self.srcs)

    def start(self, after=None):
        n = self.n
        lands = [lax.empty(s.shape, s.dtype) for s in self.land_shapes]
        extra = [] if after is None else [after]

        def body(*refs):
            ins, lands_in = refs[:n], refs[n:2 * n]
            send_sems, recv_sems, token = refs[2 * n + len(extra)], refs[2 * n + len(extra) + 1], refs[-1]
            sends, _ = self.plan(ins, lands_in, send_sems, recv_sems)
            for cond, cp in sends:
                _when(cond, cp.start)
            token[...] = jnp.zeros_like(token)

        hbm = [pltpu.with_memory_space_constraint(a, pltpu.HBM) for a in self.srcs + lands]
        res = pl.pallas_call(
            body, name=self.name + "_start",
            out_shape=(pltpu.SemaphoreType.DMA((self.n_send,)), pltpu.SemaphoreType.DMA((self.n_recv,)),
                       *[pltpu.HBM(a.shape, a.dtype) for a in hbm], jax.ShapeDtypeStruct((SUBLANE, LANE), f32)),
            in_specs=[HBM_SPEC] * (2 * n + len(extra)),
            out_specs=(SEM_SPEC, SEM_SPEC, *[HBM_SPEC] * (2 * n), pl.BlockSpec(memory_space=pltpu.VMEM)),
            input_output_aliases={i: 2 + i for i in range(2 * n)},
            compiler_params=pltpu.CompilerParams(has_side_effects=SPLIT_EFFECT))(*hbm, *extra)
        self.sems, self.thru, token = res[:2], res[2:2 + 2 * n], res[-1]
        return token[0, 0]

    def finish(self, after):
        n = self.n

        def body(*refs):
            ins, lands_in, send_sems, recv_sems = refs[:n], refs[n:2 * n], refs[2 * n], refs[2 * n + 1]
            sends, arrivals = self.plan(ins, lands_in, send_sems, recv_sems)
            for cond, cp in arrivals:
                _when(cond, cp.wait_recv)
            for cond, cp in sends:
                _when(cond, cp.wait_send)

        res = pl.pallas_call(
            body, name=self.name + "_finish", out_shape=tuple(pltpu.HBM(a.shape, a.dtype) for a in self.thru),
            in_specs=[HBM_SPEC] * (2 * n) + [SEM_SPEC, SEM_SPEC, HBM_SPEC], out_specs=tuple([HBM_SPEC] * (2 * n)),
            input_output_aliases={i: i for i in range(2 * n)},
            compiler_params=pltpu.CompilerParams(has_side_effects=SPLIT_EFFECT))(*self.thru, *self.sems, after)
        return list(res[n:])


def _gather_exchange(name, shards):
    def plan(ins, lands, send_sems, recv_sems):
        x, y, c = (lax.axis_index(a) for a in MESH_AXES)
        copies = []
        for i in range(len(ins)):
            for k, (fx, fy) in enumerate(CHIP_FLIPS):
                peer = (1 - x if fx else x, 1 - y if fy else y, c)
                copies.append((None, pltpu.make_async_remote_copy(
                    src_ref=ins[i], dst_ref=lands[i].at[2 * x + y], send_sem=send_sems.at[3 * i + k],
                    recv_sem=recv_sems.at[3 * i + k], device_id=peer, device_id_type=pl.DeviceIdType.MESH)))
        return copies, copies

    n = len(shards)
    return _Exchange(name, plan, shards, [jax.ShapeDtypeStruct((N_SHARDS,) + s.shape, s.dtype) for s in shards], 3 * n, 3 * n)


def _scatter_exchange(name, layer, chunks):
    def plan(ins, lands, send_sems, recv_sems):
        x, y, c = (lax.axis_index(a) for a in MESH_AXES)
        me = _device_index()
        sends, arrivals = [], []
        for i in range(len(ins)):
            for j in range(N_SHARDS):
                target = (j // 2, j % 2, layer)
                remote = jnp.logical_not((x == target[0]) & (y == target[1]) & (c == layer))
                sends.append((remote, pltpu.make_async_remote_copy(
                    src_ref=ins[i].at[j], dst_ref=lands[i].at[me], send_sem=send_sems.at[N_SHARDS * i + j],
                    recv_sem=recv_sems.at[N_DEVICES * i + me], device_id=target, device_id_type=pl.DeviceIdType.MESH)))
            for s in range(N_DEVICES):
                arrivals.append(((c == layer) & (me != s), pltpu.make_async_remote_copy(
                    src_ref=ins[i].at[0], dst_ref=lands[i].at[s], send_sem=send_sems.at[0],
                    recv_sem=recv_sems.at[N_DEVICES * i + s], device_id=(x, y, c), device_id_type=pl.DeviceIdType.MESH)))
        return sends, arrivals

    n = len(chunks)
    lands = [jax.ShapeDtypeStruct((N_DEVICES,) + ch.shape[1:], ch.dtype) for ch in chunks]
    return _Exchange(name, plan, chunks, lands, N_SHARDS * n, N_DEVICES * n)


def _sum_contributions(name, got, mine):
    _, a, b = got.shape
    ta = _row_tile(a, max(SUBLANE, SUM_BLOCK_BYTES // (N_DEVICES * b * got.dtype.itemsize) // SUBLANE * SUBLANE))

    def kern(got_ref, mine_ref, o_ref):
        me = _device_index()
        acc = jnp.zeros(o_ref.shape, f32)
        for s in range(N_DEVICES):
            acc = acc + jnp.where(me == s, mine_ref[...].astype(f32), got_ref[s].astype(f32))
        o_ref[...] = acc

    return pl.pallas_call(
        kern, name=name, grid=(a // ta,),
        in_specs=[pl.BlockSpec((N_DEVICES, ta, b), lambda i: (0, i, 0)), pl.BlockSpec((ta, b), lambda i: (i, 0))],
        out_specs=pl.BlockSpec((ta, b), lambda i: (i, 0)), out_shape=jax.ShapeDtypeStruct((a, b), f32),
        compiler_params=pltpu.CompilerParams(dimension_semantics=("parallel",)))(got, mine)


def _swap_layers(name, sums):
    n = len(sums[0])

    def body(*refs):
        srcs = (refs[:n], refs[n:2 * n])
        outs, (send_sems, recv_sems) = refs[2 * n:3 * n], refs[3 * n:]
        x, y, c = (lax.axis_index(a) for a in MESH_AXES)
        for i in range(n):
            for layer in range(DEPTH):
                cp = pltpu.make_async_remote_copy(
                    src_ref=srcs[layer][i], dst_ref=outs[i], send_sem=send_sems.at[i], recv_sem=recv_sems.at[i],
                    device_id=(x, y, 1 - c), device_id_type=pl.DeviceIdType.MESH)
                pl.when(c == layer)(cp.start)
        for i in range(n):
            pltpu.make_async_remote_copy(
                src_ref=srcs[0][i], dst_ref=outs[i], send_sem=send_sems.at[i], recv_sem=recv_sems.at[i],
                device_id=(x, y, 1 - c), device_id_type=pl.DeviceIdType.MESH).wait()

    return pl.pallas_call(
        body, name=name, out_shape=[jax.ShapeDtypeStruct(s.shape, s.dtype) for s in sums[0]],
        in_specs=[HBM_SPEC] * (2 * n), out_specs=[HBM_SPEC] * n,
        scratch_shapes=[pltpu.SemaphoreType.DMA((n,)), pltpu.SemaphoreType.DMA((n,))])(*sums[0], *sums[1])


def _stack_shards(g, axis):
    if axis == 1:
        return g.reshape(N_SHARDS, g.shape[0] // N_SHARDS, g.shape[1])
    return g.reshape(g.shape[0], N_SHARDS, g.shape[1] // N_SHARDS).transpose(1, 0, 2)


def _join_shards(s, axis):
    if axis == 1:
        return s.reshape(-1, s.shape[2])
    return s.transpose(1, 0, 2).reshape(s.shape[1], -1)


def _layer_shards(w, l, names):
    return [w[k][l] if k in ELEMENTWISE_F32 else w[k][l].astype(bf16) for k in names]


def _full_weights(names, sent, got):
    j = 2 * lax.axis_index("x") + lax.axis_index("y")
    return {k: _join_shards(lax.dynamic_update_slice(g, own[None], (j, 0, 0)), SHARD_AXIS[k])
            for k, own, g in zip(names, sent, got)}


def _grad_chunks(grads, names):
    return [_stack_shards(grads[k], SHARD_AXIS[k]).astype(bf16) for k in names]


def _sum_group(l, names, got, chunks):
    j = 2 * lax.axis_index("x") + lax.axis_index("y")
    return {k: _sum_contributions(f"sum_l{l}_{k}", g, lax.dynamic_index_in_dim(ch, j, 0, keepdims=False))
            for k, g, ch in zip(names, got, chunks)}


def _both_layers(name, names, sums):
    c = lax.axis_index("c")
    mine = [[sums[l][k] for k in names] for l in range(DEPTH)]
    other = _swap_layers(name, mine)
    return {k: jnp.stack([jnp.where(c == 0, mine[0][i], other[i]), jnp.where(c == 0, other[i], mine[1][i])])
            for i, k in enumerate(names)}


def _gather_all_exchange(name, src):
    def plan(ins, lands, send_sems, recv_sems):
        coords = [lax.axis_index(a) for a in MESH_AXES]
        me = _device_index()
        sends, arrivals = [], []
        for f in range(1, N_DEVICES):
            peer = tuple(1 - cd if (f >> (2 - b)) & 1 else cd for b, cd in enumerate(coords))
            sends.append((None, pltpu.make_async_remote_copy(
                src_ref=ins[0], dst_ref=lands[0].at[me], send_sem=send_sems.at[f - 1], recv_sem=recv_sems.at[me],
                device_id=peer, device_id_type=pl.DeviceIdType.MESH)))
        for s in range(N_DEVICES):
            arrivals.append((me != s, pltpu.make_async_remote_copy(
                src_ref=ins[0], dst_ref=lands[0].at[s], send_sem=send_sems.at[0], recv_sem=recv_sems.at[s],
                device_id=tuple(coords), device_id_type=pl.DeviceIdType.MESH)))
        return sends, arrivals

    return _Exchange(name, plan, [src], [jax.ShapeDtypeStruct((N_DEVICES,) + src.shape, src.dtype)], N_DEVICES - 1, N_DEVICES)


def kernel(x, p, positions, g_mix, w_in, g_qc, w_uq, g_kvc, w_ukv, b_f, lru_conv_w, lru_conv_b, w_r, b_r, w_i, b_i, lru_lambda, g_out, w_o, g_ffn, w_up, ffn_conv_w, ffn_conv_b, w_down, g_ple, w_ple_gate, w_ple_proj, g_final, loss_target, m_g_mix, m_w_in, m_g_qc, m_w_uq, m_g_kvc, m_w_ukv, m_b_f, m_lru_conv_w, m_lru_conv_b, m_w_r, m_b_r, m_w_i, m_b_i, m_lru_lambda, m_g_out, m_w_o, m_g_ffn, m_w_up, m_ffn_conv_w, m_ffn_conv_b, m_w_down, m_g_ple, m_w_ple_gate, m_w_ple_proj, m_g_final, v_g_mix, v_w_in, v_g_qc, v_w_uq, v_g_kvc, v_w_ukv, v_b_f, v_lru_conv_w, v_lru_conv_b, v_w_r, v_b_r, v_w_i, v_b_i, v_lru_lambda, v_g_out, v_w_o, v_g_ffn, v_w_up, v_ffn_conv_w, v_ffn_conv_b, v_w_down, v_g_ple, v_w_ple_gate, v_w_ple_proj, v_g_final):
    given = locals()
    w = {k: given[k] for k in WEIGHTS}
    m = {k: given["m_" + k] for k in WEIGHTS}
    v = {k: given["v_" + k] for k in WEIGHTS}

    parts = {"mix": MIX_PART, "ffn": FFN_PART}
    groups = [(l, part) for l in range(DEPTH) for part in ("mix", "ffn")]
    sent = {g: _layer_shards(w, g[0], parts[g[1]]) for g in groups}
    first = _gather_exchange("gather_l0_mix", sent[groups[0]]).run()
    ahead = {g: _gather_exchange(f"gather_l{g[0]}_{g[1]}", sent[g]) for g in groups[1:]}
    pos = positions[0].astype(f32).reshape(-1, 1) + ahead[groups[1]].start(after=first[0])
    behind, layer_grads, chunks = {}, [{} for _ in range(DEPTH)], {}

    def weights_of(l, part, after):
        g = (l, part)
        got = first if g == groups[0] else ahead[g].finish(after=after)
        full = _full_weights(parts[part], sent[g], got)
        if part == "mix":
            full.update({k: w[k][l] for k in LAYER_WEIGHTS if k in REPLICATED})
        if g == groups[1]:
            for later in groups[2:]:
                full["ffn_conv_w"] = full["ffn_conv_w"] + ahead[later].start(after=got[0])
        return full

    def grads_to(l, part, grads):
        g = (l, part)
        layer_grads[l].update(grads)
        chunks[g] = _grad_chunks(grads, parts[part])
        if g == groups[0]:
            return jnp.zeros((), f32)
        behind[g] = _scatter_exchange(f"scatter_l{l}_{part}", l, chunks[g])
        return behind[g].start()

    loss, dx, dg_final = _local_step(x[0], p[:, 0], pos, loss_target[0], w["g_final"], weights_of, grads_to)

    grads = {k: jnp.stack([layer_grads[l][k] for l in range(DEPTH)]) for k in LAYER_WEIGHTS if k in REPLICATED}
    grads["g_final"] = dg_final
    rep_shapes = [w[k].shape for k in REPLICATED] + [(1,)]
    contrib = _pack([grads[k] for k in REPLICATED] + [loss.reshape(1)])
    last = _scatter_exchange("scatter_l0_mix", 0, chunks[groups[0]])
    everyone = _gather_all_exchange("gather_replicated", contrib)
    started = (last.start() + everyone.start() + dx[0, 0]).reshape(1, 1)

    def adamw_of(names, g_sharded):
        out = {}
        for k in names:
            shape = w[k].shape
            flat = [t.reshape(-1, shape[-1]) for t in (w[k], g_sharded[k], m[k], v[k])]
            out[k] = [t.reshape(shape) for t in (flat[1],) + tuple(_adamw("adamw_" + k, *flat))]
        return out

    sums = [{} for _ in range(DEPTH)]
    for g in groups[1:]:
        sums[g[0]].update(_sum_group(g[0], parts[g[1]], behind[g].finish(after=started), chunks[g]))
    big = adamw_of(FFN_PART, _both_layers("swap_ffn", FFN_PART, sums))
    sums[0].update(_sum_group(0, MIX_PART, last.finish(after=big[FFN_PART[0]][1]), chunks[groups[0]]))
    big.update(adamw_of(MIX_PART, _both_layers("swap_mix", MIX_PART, sums)))

    g_rep = _sum_contributions("sum_replicated", everyone.finish(after=big[MIX_PART[0]][1])[0], contrib)
    zero = jnp.zeros((1,), f32)
    w_rep, m_rep, v_rep = (_pack([t[k] for k in REPLICATED] + [zero]) for t in (w, m, v))
    rep = [_unpack(b, rep_shapes) for b in (g_rep,) + tuple(_adamw("adamw_replicated", w_rep, g_rep, m_rep, v_rep))]

    outs = []
    for kind in range(4):
        by_name = {k: big[k][kind] for k in SHARDED}
        by_name.update(zip(REPLICATED, rep[kind][:-1]))
        outs.append([by_name[k] for k in WEIGHTS])
    total_loss = rep[0][-1][0]
    return (total_loss, dx.reshape(x.shape), *outs[0], *outs[1], *outs[2], *outs[3])
```

```python
import functools
import math

import numpy as np
import jax
import jax.numpy as jnp
from jax import lax
from jax.experimental import pallas as pl
from jax.experimental.pallas import tpu as pltpu

f32, bf16 = jnp.float32, jnp.bfloat16

D_MODEL = 1024
PLE_DIM = 256
MLA_HEADS, MLA_NOPE, MLA_ROPE, MLA_V = 4, 64, 32, 64
MLA_Q_RANK, MLA_KV_RANK = 192, 128
FOX_HEADS, FOX_HEAD_DIM = 4, 64
LRU_WIDTH, LRU_BLOCKS, LRU_BLOCK, LRU_CONV, LRU_C = 512, 8, 64, 4, 8.0
D_FF, FFN_CONV = 2816, 3
ROPE_THETA = 10000.0
EPS = 1e-6
DEPTH = 2
ADAM_LR, ADAM_B1, ADAM_B2, ADAM_EPS, ADAM_WD, ADAM_STEP = 0.001, 0.9, 0.999, 1e-08, 0.01, 10

LANE = 128
SUBLANE = 8
HEADS = 4

Z_FQ, Z_FK, Z_FV, Z_LX, Z_LG, Z_QC, Z_KVC, Z_KR, Z_FL, Z_W = 0, 512, 1024, 1536, 2048, 2560, 2816, 2944, 3072, 3200
QC_W = 256
ROPE_AT = 64


def _head_pad_map(n_heads, width):
    m = -np.ones(n_heads * LANE, np.int64)
    for h in range(n_heads):
        m[h * LANE:h * LANE + width] = h * width + np.arange(width)
    return m


def _z_map():
    m = -np.ones(Z_W, np.int64)
    o_qc, o_kvc, o_kr = 0, MLA_Q_RANK, MLA_Q_RANK + MLA_KV_RANK
    o_fq = o_kr + MLA_ROPE
    o_fk, o_fv = o_fq + 256, o_fq + 512
    o_fl = o_fv + 256
    o_lx = o_fl + FOX_HEADS
    o_lg = o_lx + LRU_WIDTH
    m[Z_FQ:Z_FQ + 512] = np.where(_head_pad_map(4, 64) >= 0, _head_pad_map(4, 64) + o_fq, -1)
    m[Z_FK:Z_FK + 512] = np.where(_head_pad_map(4, 64) >= 0, _head_pad_map(4, 64) + o_fk, -1)
    m[Z_FV:Z_FV + 512] = np.where(_head_pad_map(4, 64) >= 0, _head_pad_map(4, 64) + o_fv, -1)
    m[Z_LX:Z_LX + 512] = o_lx + np.arange(512)
    m[Z_LG:Z_LG + 512] = o_lg + np.arange(512)
    m[Z_QC:Z_QC + MLA_Q_RANK] = o_qc + np.arange(MLA_Q_RANK)
    m[Z_KVC:Z_KVC + MLA_KV_RANK] = o_kvc + np.arange(MLA_KV_RANK)
    m[Z_KR + ROPE_AT:Z_KR + ROPE_AT + MLA_ROPE] = o_kr + np.arange(MLA_ROPE)
    m[Z_FL:Z_FL + FOX_HEADS] = o_fl + np.arange(FOX_HEADS)
    return m


def _ukv_map():
    m = -np.ones(2 * HEADS * LANE, np.int64)
    for h in range(HEADS):
        m[h * LANE:h * LANE + MLA_NOPE] = h * (MLA_NOPE + MLA_V) + np.arange(MLA_NOPE)
        m[HEADS * LANE + h * LANE:HEADS * LANE + h * LANE + MLA_V] = h * (MLA_NOPE + MLA_V) + MLA_NOPE + np.arange(MLA_V)
    return m


def _omix_map():
    return np.concatenate([_head_pad_map(4, 64), np.where(_head_pad_map(4, 64) >= 0, _head_pad_map(4, 64) + 256, -1),
                           512 + np.arange(512)])


def _pad_to(m, n):
    return np.concatenate([m, -np.ones(n - m.shape[0], np.int64)])


def _runs(m):
    out, at = [], 0
    while at < len(m):
        end = at + 1
        while end < len(m) and (m[end] == m[end - 1] + 1 if m[at] >= 0 else m[end] < 0):
            end += 1
        out.append((int(m[at]), end - at))
        at = end
    return out


def _take_runs(a, m, axis):
    parts = []
    for start, size in _runs(m):
        if start < 0:
            shape = list(a.shape)
            shape[axis] = size
            parts.append(jnp.zeros(shape, a.dtype))
        else:
            parts.append(lax.slice_in_dim(a, start, start + size, axis=axis))
    return parts[0] if len(parts) == 1 else jnp.concatenate(parts, axis=axis)


def _take_pad(a, m, axis):
    return _take_runs(a, m, axis)


def _take_inv(a, m, axis):
    n = int(m.max()) + 1
    inv = np.zeros(n, np.int64)
    inv[m[m >= 0]] = np.nonzero(m >= 0)[0]
    return _take_runs(a, inv, axis)


Z_MAP = _z_map()
UQ_COL_MAP = _head_pad_map(HEADS, MLA_NOPE + MLA_ROPE)
UQ_ROW_MAP = _pad_to(np.arange(MLA_Q_RANK), QC_W)
UKV_MAP = _ukv_map()
OMIX_MAP = _omix_map()
OMIX_W = 1536


def _rope_tables(width, at):
    half = MLA_ROPE // 2
    inv = ROPE_THETA ** (-np.arange(half, dtype=np.float32) / half)
    freq = np.zeros((1, width), np.float32)
    m1 = np.zeros((1, width), np.float32)
    m2 = np.zeros((1, width), np.float32)
    for h in range(width // LANE):
        b = h * LANE + at
        freq[0, b:b + half] = inv
        freq[0, b + half:b + 2 * half] = inv
        m1[0, b:b + half] = 1.0
        m2[0, b + half:b + 2 * half] = 1.0
    return freq, m1, m2


def _view(r):
    return r if isinstance(r, tuple) else (r, r.shape[1], 0)


def _blk(dim, cap):
    if dim <= cap:
        return dim
    for b in range(cap, LANE - 1, -LANE):
        if dim % b == 0:
            return b
    return dim


@functools.partial(jax.custom_vjp, nondiff_argnums=(1, 2))
def _roll(x, shift, axis):
    return pltpu.roll(x, shift, axis)


def _roll_fwd(x, shift, axis):
    return pltpu.roll(x, shift, axis), None


def _roll_bwd(shift, axis, _, g):
    return (pltpu.roll(g, g.shape[axis] - shift, axis),)


_roll.defvjp(_roll_fwd, _roll_bwd)


ROW_VMEM_BUDGET = 20 * 2 ** 20
ROW_TILES = (1024, 512, 256)


def _row_block(n, bytes_per_row):
    for tb in ROW_TILES:
        if n % tb == 0 and 2 * tb * bytes_per_row <= ROW_VMEM_BUDGET:
            return tb
    return min(ROW_TILES[-1], n)


def _rowwise(name, fn, rows, pars, outs):
    rows = [_view(r) for r in rows]
    n = rows[0][0].shape[0]
    tb = _row_block(n, sum(w * a.dtype.itemsize for a, w, _ in rows) + sum(w * jnp.dtype(dt).itemsize for w, dt in outs))
    nr, npar = len(rows), len(pars)

    def kern(*refs):
        r = [refs[k][...].astype(f32) for k in range(nr)]
        p = [refs[nr + k][...] for k in range(npar)]
        res = fn(*r, *p)
        for o_ref, o in zip(refs[nr + npar:], res):
            o_ref[...] = o.astype(o_ref.dtype)

    in_specs = [pl.BlockSpec((tb, w), lambda i, j=idx: (i, j)) for (_, w, idx) in rows]
    in_specs += [pl.BlockSpec(p.shape, lambda i: (0, 0)) for p in pars]
    out_specs = [pl.BlockSpec((tb, w), lambda i: (i, 0)) for (w, _) in outs]
    out_shape = [jax.ShapeDtypeStruct((n, w), dt) for (w, dt) in outs]
    return pl.pallas_call(kern, name=name, grid=(n // tb,), in_specs=in_specs, out_specs=out_specs, out_shape=out_shape,
                          compiler_params=pltpu.CompilerParams(dimension_semantics=("parallel",)))(*[r[0] for r in rows], *pars)


def _rowwise_bwd(name, fn, rows, pars, cts, ndiff, adds=None, dts=None):
    rows = [_view(r) for r in rows]
    dts = dts or [f32] * ndiff
    adds = adds or {}
    add_keys = sorted(adds)
    n = rows[0][0].shape[0]
    tb = _row_block(n, sum(w * a.dtype.itemsize for a, w, _ in rows) + sum(c.shape[1] * c.dtype.itemsize for c in cts)
                    + sum(a.shape[1] * a.dtype.itemsize for a in adds.values())
                    + sum(rows[k][1] * jnp.dtype(dts[k]).itemsize for k in range(ndiff)))
    nr, npar, nct, nadd = len(rows), len(pars), len(cts), len(add_keys)

    def kern(*refs):
        i = pl.program_id(0)
        r = [refs[k][...].astype(f32) for k in range(nr)]
        p = [refs[nr + k][...] for k in range(npar)]
        ct = [refs[nr + npar + k][...].astype(f32) for k in range(nct)]
        ad = {key: refs[nr + npar + nct + k][...] for k, key in enumerate(add_keys)}
        o_refs = refs[nr + npar + nct + nadd:]

        def g(*d):
            return tuple(fn(*d[:ndiff], *r[ndiff:], *d[ndiff:]))

        _, vjp = jax.vjp(g, *r[:ndiff], *p)
        grads = vjp(tuple(ct))
        for k in range(ndiff):
            gk = grads[k]
            if k in ad:
                gk = gk + ad[k]
            o_refs[k][...] = gk.astype(o_refs[k].dtype)

        @pl.when(i == 0)
        def _():
            for k in range(npar):
                o_refs[ndiff + k][...] = jnp.zeros_like(o_refs[ndiff + k])

        for k in range(npar):
            o_refs[ndiff + k][...] += grads[ndiff + k]

    in_specs = [pl.BlockSpec((tb, w), lambda i, j=idx: (i, j)) for (_, w, idx) in rows]
    in_specs += [pl.BlockSpec(p.shape, lambda i: (0, 0)) for p in pars]
    in_specs += [pl.BlockSpec((tb, c.shape[1]), lambda i: (i, 0)) for c in cts]
    in_specs += [pl.BlockSpec((tb, adds[k].shape[1]), lambda i: (i, 0)) for k in add_keys]
    out_specs = [pl.BlockSpec((tb, rows[k][1]), lambda i: (i, 0)) for k in range(ndiff)]
    out_specs += [pl.BlockSpec(p.shape, lambda i: (0, 0)) for p in pars]
    out_shape = [jax.ShapeDtypeStruct((n, rows[k][1]), dts[k]) for k in range(ndiff)]
    out_shape += [jax.ShapeDtypeStruct(p.shape, f32) for p in pars]
    res = pl.pallas_call(kern, name=name, grid=(n // tb,), in_specs=in_specs, out_specs=out_specs, out_shape=out_shape,
                         compiler_params=pltpu.CompilerParams(dimension_semantics=("arbitrary",)))(
        *[r[0] for r in rows], *pars, *cts, *[adds[k] for k in add_keys])
    return res[:ndiff], res[ndiff:]


_DOT_DIMS = {"nn": ((1,), (0,)), "nt": ((1,), (1,)), "tn": ((0,), (0,))}

MM_VMEM_BUDGET = 36 * 2 ** 20
MM_MAX_TM = 1408
MM_STEP, MM_RESULT, MM_XPOSE, MM_CAST = 700.0, 7.5e-4, 9e-4, 1e-3


def _tile_candidates(dim):
    c = [d for d in range(LANE, dim + 1, LANE) if dim % d == 0]
    return c or [dim]


@functools.lru_cache(maxsize=None)
def _mm_tiles(mode, m, n, k, a_bytes, b_bytes, o_bytes):
    best, best_cost = None, None
    for tm in _tile_candidates(m):
        if tm > MM_MAX_TM:
            continue
        for tn in _tile_candidates(n):
            for tk in _tile_candidates(k):
                vmem = 2 * (tm * tk * a_bytes + tk * tn * b_bytes + tm * tn * o_bytes) + 4 * tm * tn * (2 if tk < k else 1)
                vmem += (2 * tm * tk if a_bytes > 2 else 0) + (2 * tk * tn if b_bytes > 2 else 0)
                if vmem > MM_VMEM_BUDGET:
                    continue
                steps = (m // tm) * (n // tn) * (k // tk)
                cost = steps * MM_STEP + m * n * (k // tk) * MM_RESULT
                if mode == "tn":
                    cost += m * k * (n // tn) * MM_XPOSE
                cost += (m * k * (n // tn) * MM_CAST if a_bytes > 2 else 0) + (k * n * (m // tm) * MM_CAST if b_bytes > 2 else 0)
                if best is None or cost < best_cost:
                    best, best_cost = (tm, tn, tk), cost
    return best


def _mm(name, a, b, mode="nn", out_dtype=f32, res=None):
    if mode == "nn":
        (m, k), (_, n) = a.shape, b.shape
    elif mode == "nt":
        (m, k), (n, _) = a.shape, b.shape
    else:
        (k, m), (_, n) = a.shape, b.shape
    has_res = res is not None
    tm, tn, tk = _mm_tiles(mode, m, n, k, a.dtype.itemsize, b.dtype.itemsize,
                           jnp.dtype(out_dtype).itemsize + (res.dtype.itemsize if has_res else 0))
    nk = k // tk
    dims = (_DOT_DIMS[mode], ((), ()))

    def kern(*refs):
        a_ref, b_ref = refs[0], refs[1]
        o_ref, acc_ref = refs[-2], refs[-1]
        kk = pl.program_id(2)
        part = lax.dot_general(a_ref[...].astype(bf16), b_ref[...].astype(bf16), dims, preferred_element_type=f32)

        def finish(out):
            if has_res:
                out = out + refs[2][...]
            o_ref[...] = out.astype(o_ref.dtype)

        if nk == 1:
            finish(part)
            return

        @pl.when(kk == 0)
        def _():
            acc_ref[...] = part

        @pl.when(jnp.logical_and(kk > 0, kk < nk - 1))
        def _():
            acc_ref[...] += part

        @pl.when(kk == nk - 1)
        def _():
            finish(acc_ref[...] + part)

    if mode == "tn":
        a_spec = pl.BlockSpec((tk, tm), lambda i, j, kk: (kk, i))
    else:
        a_spec = pl.BlockSpec((tm, tk), lambda i, j, kk: (i, kk))
    if mode == "nt":
        b_spec = pl.BlockSpec((tn, tk), lambda i, j, kk: (j, kk))
    else:
        b_spec = pl.BlockSpec((tk, tn), lambda i, j, kk: (kk, j))
    in_specs = [a_spec, b_spec]
    args = [a, b]
    if has_res:
        in_specs.append(pl.BlockSpec((tm, tn), lambda i, j, kk: (i, j)))
        args.append(res)
    return pl.pallas_call(
        kern, name=name, grid=(m // tm, n // tn, nk), in_specs=in_specs,
        out_specs=pl.BlockSpec((tm, tn), lambda i, j, kk: (i, j)),
        out_shape=jax.ShapeDtypeStruct((m, n), out_dtype),
        scratch_shapes=[pltpu.VMEM((tm, tn) if nk > 1 else (SUBLANE, LANE), f32)],
        compiler_params=pltpu.CompilerParams(dimension_semantics=("parallel", "parallel", "arbitrary")))(*args)


ATT_TQ, ATT_TK = 512, 512


def _att_tiles(s_len):
    tk = min(ATT_TK, s_len)
    return min(ATT_TQ, tk), tk


def _fold_scale(scale):
    return (scale, 1.0) if math.frexp(scale)[0] == 0.5 else (1.0, scale)


def _as_row(col):
    return jnp.max(jnp.broadcast_to(col, (col.shape[0], LANE)).T[:SUBLANE], axis=0, keepdims=True)


def _scores_t(kb, q_t, s_mul, ck, diag_offset, tq, tk):
    s = jnp.dot(kb, q_t, preferred_element_type=f32)
    if s_mul != 1.0:
        s = s * s_mul
    if ck is not None:
        s = s - ck
    if diag_offset is None:
        return s
    key = lax.broadcasted_iota(jnp.int32, (tk, tq), 0)
    query = lax.broadcasted_iota(jnp.int32, (tk, tq), 1) + diag_offset
    return jnp.where(key <= query, s, -jnp.inf)


ATT_ROWS = 64


def _finish_scores(s, s_mul, ck, first_row):
    if s_mul != 1.0:
        s = s * s_mul
    if ck is not None:
        s = s - ck
    if first_row is None:
        return s
    row = lax.broadcasted_iota(jnp.int32, s.shape, 0) + first_row
    col = lax.broadcasted_iota(jnp.int32, s.shape, 1)
    return jnp.where(col <= row, s, -jnp.inf)


def _attn_fwd(name, q, k, v, scale, c_row=None):
    (qa, qo), (ka, ko), (va, vo) = q, k, v
    s_len = qa.shape[0]
    t = _att_tiles(s_len)[1]
    nt = s_len // t
    decay = c_row is not None
    q_mul, s_mul = _fold_scale(scale)

    def kern(*refs):
        q_ref, k_ref, v_ref = refs[:3]
        o_ref, lse_ref, lse_row_ref = refs[-3:]
        i = pl.program_id(1)
        qb = (q_ref[...] * q_mul).astype(bf16)

        def step(j, carry, diagonal):
            m, l, acc = carry
            rows = pl.ds(pl.multiple_of(j * t, t), t)
            kb = k_ref[rows, :].astype(bf16)
            vb = v_ref[rows, :].astype(bf16)
            s = lax.dot_general(qb, kb, (_DOT_DIMS["nt"], ((), ())), preferred_element_type=f32)
            s = _finish_scores(s, s_mul, refs[3][j] if decay else None, 0 if diagonal else None)
            m_new = jnp.maximum(m, jnp.max(s, axis=1, keepdims=True))
            alpha = jnp.exp(m - m_new)
            p = jnp.exp(s - m_new)
            l = alpha * l + jnp.sum(p, axis=1, keepdims=True)
            acc = alpha * acc + jnp.dot(p.astype(bf16), vb, preferred_element_type=f32)
            return m_new, l, acc

        init = (jnp.full((t, 1), -jnp.inf, f32), jnp.zeros((t, 1), f32), jnp.zeros((t, LANE), f32))
        m, l, acc = step(i, lax.fori_loop(0, i, lambda j, c: step(j, c, False), init), True)
        o_ref[...] = acc / l
        lse = m + jnp.log(l)
        lse_ref[...] = lse
        lse_row_ref[...] = _as_row(lse)

    in_specs = [pl.BlockSpec((t, LANE), lambda h, i: (i, qo + h)),
                pl.BlockSpec((s_len, LANE), lambda h, i: (0, ko + h)),
                pl.BlockSpec((s_len, LANE), lambda h, i: (0, vo + h))]
    args = [qa, ka, va]
    if decay:
        in_specs.append(pl.BlockSpec((None, nt, 1, t), lambda h, i: (h, 0, 0, 0)))
        args.append(c_row)
    return pl.pallas_call(
        kern, name=name, grid=(HEADS, nt), in_specs=in_specs,
        out_specs=[pl.BlockSpec((t, LANE), lambda h, i: (i, h)), pl.BlockSpec((None, t, 1), lambda h, i: (h, i, 0)),
                   pl.BlockSpec((None, None, 1, t), lambda h, i: (h, i, 0, 0))],
        out_shape=[jax.ShapeDtypeStruct((s_len, HEADS * LANE), f32), jax.ShapeDtypeStruct((HEADS, s_len, 1), f32),
                   jax.ShapeDtypeStruct((HEADS, nt, 1, t), f32)],
        compiler_params=pltpu.CompilerParams(dimension_semantics=("parallel", "arbitrary")))(*args)


def _attn_dq(name, q, k, v, o, do, lse, scale, c_row=None):
    (qa, qo), (ka, ko), (va, vo) = q, k, v
    s_len = qa.shape[0]
    t = _att_tiles(s_len)[1]
    nt = s_len // t
    decay = c_row is not None
    q_mul, s_mul = _fold_scale(scale)

    rp = min(ATT_ROWS, t)

    def kern(*refs):
        q_ref, k_ref, v_ref, o_ref, do_ref, lse_ref = refs[:6]
        dq_ref, delta_row_ref, drow_ref, delta_ref, s_ref, dp_ref, ds_ref = refs[-7:]
        i = pl.program_id(1)
        qb = (q_ref[...] * q_mul).astype(bf16)
        dob = do_ref[...]
        delta = jnp.sum(dob * o_ref[...], axis=1, keepdims=True)
        delta_ref[...] = delta
        delta_row_ref[...] = _as_row(delta)
        dob = dob.astype(bf16)
        drow_ref[...] = jnp.zeros((t, 1), f32)
        dq_ref[...] = jnp.zeros((t, LANE), f32)

        def step(j, diagonal):
            rows = pl.ds(pl.multiple_of(j * t, t), t)
            kb = k_ref[rows, :].astype(bf16)
            s_ref[...] = lax.dot_general(qb, kb, (_DOT_DIMS["nt"], ((), ())), preferred_element_type=f32)
            dp_ref[...] = lax.dot_general(dob, v_ref[rows, :].astype(bf16), (_DOT_DIMS["nt"], ((), ())),
                                          preferred_element_type=f32)
            ck = refs[6][j] if decay else None

            def rows_of(c, carry):
                r = slice(c * rp, (c + 1) * rp)
                s = _finish_scores(s_ref[r, :], s_mul, ck, c * rp if diagonal else None)
                ds = jnp.exp(s - lse_ref[r, :]) * (dp_ref[r, :] - delta_ref[r, :])
                drow_ref[r, :] += jnp.sum(ds, axis=1, keepdims=True)
                ds_ref[r, :] = ds.astype(bf16)
                return carry

            for c in range(t // rp):
                rows_of(c, 0)
            dq_ref[...] += jnp.dot(ds_ref[...], kb, preferred_element_type=f32)

        def below(j, carry):
            step(j, False)
            return carry

        lax.fori_loop(0, i, below, 0)
        step(i, True)
        dq_ref[...] = dq_ref[...] * scale

    in_specs = [pl.BlockSpec((t, LANE), lambda h, i: (i, qo + h)),
                pl.BlockSpec((s_len, LANE), lambda h, i: (0, ko + h)),
                pl.BlockSpec((s_len, LANE), lambda h, i: (0, vo + h)),
                pl.BlockSpec((t, LANE), lambda h, i: (i, h)),
                pl.BlockSpec((t, LANE), lambda h, i: (i, h)),
                pl.BlockSpec((None, t, 1), lambda h, i: (h, i, 0))]
    args = [qa, ka, va, o, do, lse]
    if decay:
        in_specs.append(pl.BlockSpec((None, nt, 1, t), lambda h, i: (h, 0, 0, 0)))
        args.append(c_row)
    col = pl.BlockSpec((None, t, 1), lambda h, i: (h, i, 0))
    return pl.pallas_call(
        kern, name=name, grid=(HEADS, nt), in_specs=in_specs,
        out_specs=[pl.BlockSpec((t, LANE), lambda h, i: (i, h)), pl.BlockSpec((None, None, 1, t), lambda h, i: (h, i, 0, 0)), col],
        out_shape=[jax.ShapeDtypeStruct((s_len, HEADS * LANE), f32), jax.ShapeDtypeStruct((HEADS, nt, 1, t), f32),
                   jax.ShapeDtypeStruct((HEADS, s_len, 1), f32)],
        scratch_shapes=[pltpu.VMEM((t, 1), f32), pltpu.VMEM((t, t), f32), pltpu.VMEM((t, t), f32), pltpu.VMEM((t, t), bf16)],
        compiler_params=pltpu.CompilerParams(dimension_semantics=("parallel", "arbitrary")))(*args)


def _attn_dkv(name, q, k, v, do, lse, delta, scale, c_col=None):
    (qa, qo), (ka, ko), (va, vo) = q, k, v
    s_len = qa.shape[0]
    tq, tk = _att_tiles(s_len)
    assert lse.shape == (HEADS, s_len // tq, 1, tq), (lse.shape, tq)
    nq, per = s_len // tq, tk // tq
    decay = c_col is not None
    q_mul, s_mul = _fold_scale(scale)

    def kern(*refs):
        q_ref, k_ref, v_ref, do_ref, lse_ref, delta_ref = refs[:6]
        j = pl.program_id(1)
        kb = k_ref[...].astype(bf16)
        vb = v_ref[...].astype(bf16)
        ck = refs[6][...] if decay else None

        def step(i, carry, diagonal):
            dk, dv, dsum = carry
            for d in range(per):
                tile = i * per + d
                rows = pl.ds(pl.multiple_of(tile * tq, tq), tq)
                qb = (q_ref[rows, :] * q_mul).astype(bf16)
                dob = do_ref[rows, :].astype(bf16)
                s = _scores_t(kb, qb.T, s_mul, ck, d * tq if diagonal else None, tq, tk)
                p = jnp.exp(s - lse_ref[tile])
                dv = dv + jnp.dot(p.astype(bf16), dob, preferred_element_type=f32)
                dp = jnp.dot(vb, dob.T, preferred_element_type=f32)
                ds = p * (dp - delta_ref[tile])
                dk = dk + jnp.dot(ds.astype(bf16), qb, preferred_element_type=f32)
                if decay:
                    dsum = dsum + ds
            return dk, dv, dsum

        init = (jnp.zeros((tk, LANE), f32), jnp.zeros((tk, LANE), f32), jnp.zeros((tk, tq), f32))
        dk, dv, dsum = lax.fori_loop(j + 1, s_len // tk, lambda i, c: step(i, c, False), step(j, init, True))
        if decay:
            dk_ref, dv_ref, dc_ref = refs[-3:]
            dc_ref[...] = -jnp.sum(dsum, axis=1, keepdims=True)
        else:
            dk_ref, dv_ref = refs[-2:]
        dk_ref[...] = dk * s_mul
        dv_ref[...] = dv

    stat = pl.BlockSpec((None, nq, 1, tq), lambda h, j: (h, 0, 0, 0))
    in_specs = [pl.BlockSpec((s_len, LANE), lambda h, j: (0, qo + h)),
                pl.BlockSpec((tk, LANE), lambda h, j: (j, ko + h)),
                pl.BlockSpec((tk, LANE), lambda h, j: (j, vo + h)),
                pl.BlockSpec((s_len, LANE), lambda h, j: (0, h)), stat, stat]
    args = [qa, ka, va, do, lse, delta]
    out_specs = [pl.BlockSpec((tk, LANE), lambda h, j: (j, h)), pl.BlockSpec((tk, LANE), lambda h, j: (j, h))]
    out_shape = [jax.ShapeDtypeStruct((s_len, HEADS * LANE), f32), jax.ShapeDtypeStruct((s_len, HEADS * LANE), f32)]
    if decay:
        in_specs.append(pl.BlockSpec((None, tk, 1), lambda h, j: (h, j, 0)))
        args.append(c_col)
        out_specs.append(pl.BlockSpec((None, tk, 1), lambda h, j: (h, j, 0)))
        out_shape.append(jax.ShapeDtypeStruct((HEADS, s_len, 1), f32))
    return pl.pallas_call(
        kern, name=name, grid=(HEADS, s_len // tk), in_specs=in_specs, out_specs=out_specs, out_shape=out_shape,
        compiler_params=pltpu.CompilerParams(dimension_semantics=("parallel", "arbitrary")))(*args)


CONV_TS, CONV_CB = 1024, 256
FFN_ROWS = 64


def _conv_fwd(name, x, w, b, taps):
    xa, width, xidx = _view(x)
    s_len = xa.shape[0]
    ts, cb = min(CONV_TS, s_len), CONV_CB
    xo = xidx * width // cb

    def kern(x_ref, halo_ref, w_ref, b_ref, o_ref):
        i = pl.program_id(1)
        xb = x_ref[...]
        halo = jnp.where(i == 0, 0.0, halo_ref[...])
        xx = jnp.concatenate([halo, xb], axis=0)
        out = b_ref[...] + w_ref[taps - 1:taps, :] * xb
        for k in range(taps - 1):
            out = out + w_ref[k:k + 1, :] * pltpu.roll(xx, taps - 1 - k, 0)[SUBLANE:]
        o_ref[...] = out

    return pl.pallas_call(
        kern, name=name, grid=(width // cb, s_len // ts),
        in_specs=[pl.BlockSpec((ts, cb), lambda j, i: (i, xo + j)),
                  pl.BlockSpec((SUBLANE, cb), lambda j, i: (jnp.maximum(i * (ts // SUBLANE) - 1, 0), xo + j)),
                  pl.BlockSpec((taps, cb), lambda j, i: (0, j)),
                  pl.BlockSpec((1, cb), lambda j, i: (0, j))],
        out_specs=pl.BlockSpec((ts, cb), lambda j, i: (i, j)),
        out_shape=jax.ShapeDtypeStruct((s_len, width), f32),
        compiler_params=pltpu.CompilerParams(dimension_semantics=("parallel", "parallel")))(xa, xa, w, b)


def _conv_bwd(name, x, dout, w, taps, dout2=None, dx_dtype=f32):
    xa, width, xidx = _view(x)
    s_len = xa.shape[0]
    ts, cb = min(CONV_TS, s_len), CONV_CB
    xo = xidx * width // cb
    n_i = s_len // ts
    two = dout2 is not None

    def kern(*refs):
        x_ref, halo_ref, w_ref = refs[:3]
        dx_ref, dw_ref, db_ref = refs[-3:]
        i = pl.program_id(1)
        if two:
            d = refs[3][...] + refs[5][...]
            dn = refs[4][...] + refs[6][...]
        else:
            d, dn = refs[3][...], refs[4][...]
        dn = jnp.where(i == n_i - 1, 0.0, dn)
        xb = x_ref[...]
        halo = jnp.where(i == 0, 0.0, halo_ref[...])
        xx = jnp.concatenate([halo, xb], axis=0)
        dd = jnp.concatenate([d, dn], axis=0)

        @pl.when(i == 0)
        def _():
            dw_ref[...] = jnp.zeros_like(dw_ref)
            db_ref[...] = jnp.zeros_like(db_ref)

        dx = w_ref[taps - 1:taps, :] * d
        dw_ref[taps - 1:taps, :] += jnp.sum(d * xb, axis=0, keepdims=True)
        for k in range(taps - 1):
            sh = taps - 1 - k
            dx = dx + w_ref[k:k + 1, :] * pltpu.roll(dd, ts + SUBLANE - sh, 0)[:ts]
            dw_ref[k:k + 1, :] += jnp.sum(d * pltpu.roll(xx, sh, 0)[SUBLANE:], axis=0, keepdims=True)
        dx_ref[...] = dx.astype(dx_ref.dtype)
        db_ref[...] += jnp.sum(d, axis=0, keepdims=True)

    d_spec = pl.BlockSpec((ts, cb), lambda j, i: (i, j))
    dn_spec = pl.BlockSpec((SUBLANE, cb), lambda j, i: (jnp.minimum((i + 1) * (ts // SUBLANE), s_len // SUBLANE - 1), j))
    in_specs = [pl.BlockSpec((ts, cb), lambda j, i: (i, xo + j)),
                pl.BlockSpec((SUBLANE, cb), lambda j, i: (jnp.maximum(i * (ts // SUBLANE) - 1, 0), xo + j)),
                pl.BlockSpec((taps, cb), lambda j, i: (0, j)), d_spec, dn_spec]
    args = [xa, xa, w, dout, dout]
    if two:
        in_specs += [d_spec, dn_spec]
        args += [dout2, dout2]
    return pl.pallas_call(
        kern, name=name, grid=(width // cb, n_i), in_specs=in_specs,
        out_specs=[pl.BlockSpec((ts, cb), lambda j, i: (i, j)), pl.BlockSpec((taps, cb), lambda j, i: (0, j)),
                   pl.BlockSpec((1, cb), lambda j, i: (0, j))],
        out_shape=[jax.ShapeDtypeStruct((s_len, width), dx_dtype), jax.ShapeDtypeStruct((taps, width), f32),
                   jax.ShapeDtypeStruct((1, width), f32)],
        compiler_params=pltpu.CompilerParams(dimension_semantics=("parallel", "arbitrary")))(*args)


def _conv_rows(xx, w_ref, b_ref, taps):
    out = b_ref[...] + w_ref[taps - 1:taps, :] * xx[SUBLANE:]
    for k in range(taps - 1):
        out = out + w_ref[k:k + 1, :] * pltpu.roll(xx, taps - 1 - k, 0)[SUBLANE:]
    return out


def _ffn_act_fwd(name, up, w, b):
    s_len = up.shape[0]
    ts, cb = min(CONV_TS, s_len), CONV_CB
    nf = D_FF // cb

    def kern(g_ref, gp_ref, v_ref, vp_ref, wg_ref, wv_ref, bg_ref, bv_ref, o_ref):
        first = pl.program_id(1) == 0
        ug = _conv_rows(jnp.concatenate([jnp.where(first, 0.0, gp_ref[...]), g_ref[...]], axis=0), wg_ref, bg_ref, FFN_CONV)
        uv = _conv_rows(jnp.concatenate([jnp.where(first, 0.0, vp_ref[...]), v_ref[...]], axis=0), wv_ref, bv_ref, FFN_CONV)
        o_ref[...] = (jax.nn.silu(ug) * uv).astype(o_ref.dtype)

    def half(off):
        return [pl.BlockSpec((ts, cb), lambda j, i: (i, off + j)),
                pl.BlockSpec((SUBLANE, cb), lambda j, i: (jnp.maximum(i * (ts // SUBLANE) - 1, 0), off + j))]

    def par(rows, off):
        return pl.BlockSpec((rows, cb), lambda j, i: (0, off + j))

    return pl.pallas_call(
        kern, name=name, grid=(nf, s_len // ts),
        in_specs=half(0) + half(nf) + [par(FFN_CONV, 0), par(FFN_CONV, nf), par(1, 0), par(1, nf)],
        out_specs=pl.BlockSpec((ts, cb), lambda j, i: (i, j)),
        out_shape=jax.ShapeDtypeStruct((s_len, D_FF), bf16),
        compiler_params=pltpu.CompilerParams(dimension_semantics=("parallel", "parallel")))(up, up, up, up, w, w, b, b)


def _ffn_act_bwd(name, up, dact, w, b):
    s_len = up.shape[0]
    ts, cb = min(CONV_TS, s_len), CONV_CB
    nf = D_FF // cb
    n_i = s_len // ts
    taps = FFN_CONV

    ch = min(FFN_ROWS, ts)

    def kern(g_ref, gp_ref, gn_ref, v_ref, vp_ref, vn_ref, d_ref, dn_ref, wg_ref, wv_ref, bg_ref, bv_ref,
             dg_ref, dv_ref, dwg_ref, dwv_ref, dbg_ref, dbv_ref, gx_ref, vx_ref, dd_ref):
        i = pl.program_id(1)
        first, last = i == 0, i == n_i - 1
        for x_ref, p_ref, n_ref, ext in ((g_ref, gp_ref, gn_ref, gx_ref), (v_ref, vp_ref, vn_ref, vx_ref)):
            ext[:SUBLANE, :] = jnp.where(first, 0.0, p_ref[...])
            ext[SUBLANE:SUBLANE + ts, :] = x_ref[...]
            ext[SUBLANE + ts:, :] = jnp.where(last, 0.0, n_ref[...])
        dd_ref[:ts, :] = d_ref[...]
        dd_ref[ts:, :] = jnp.where(last, 0.0, dn_ref[...])

        @pl.when(first)
        def _():
            for ref in (dwg_ref, dwv_ref, dbg_ref, dbv_ref):
                ref[...] = jnp.zeros_like(ref)

        def rows_of(c, carry):
            r0 = pl.multiple_of(c * ch, ch)
            gx, vx = gx_ref[pl.ds(r0, ch + 2 * SUBLANE), :], vx_ref[pl.ds(r0, ch + 2 * SUBLANE), :]
            ug, uv = _conv_rows(gx, wg_ref, bg_ref, taps), _conv_rows(vx, wv_ref, bv_ref, taps)
            dd = dd_ref[pl.ds(r0, ch + SUBLANE), :]
            sg = jax.nn.sigmoid(ug)
            out = []
            for du, xx, w_ref, dx_ref, sums in ((dd * uv * (sg * (1.0 + ug * (1.0 - sg))), gx, wg_ref, dg_ref, carry[0]),
                                                (dd * (ug * sg), vx, wv_ref, dv_ref, carry[1])):
                d = du[:ch]
                dx = w_ref[taps - 1:taps, :] * d
                new = [None] * (taps + 1)
                new[taps - 1] = sums[taps - 1] + jnp.sum(d * xx[SUBLANE:SUBLANE + ch], axis=0, keepdims=True)
                for k in range(taps - 1):
                    sh = taps - 1 - k
                    dx = dx + w_ref[k:k + 1, :] * pltpu.roll(du, ch + SUBLANE - sh, 0)[:ch]
                    new[k] = sums[k] + jnp.sum(d * pltpu.roll(xx, sh, 0)[SUBLANE:SUBLANE + ch], axis=0, keepdims=True)
                new[taps] = sums[taps] + jnp.sum(d, axis=0, keepdims=True)
                dx_ref[pl.ds(r0, ch), :] = dx.astype(dx_ref.dtype)
                out.append(tuple(new))
            return tuple(out)

        zero = tuple(jnp.zeros((1, cb), f32) for _ in range(taps + 1))
        sums_g, sums_v = lax.fori_loop(0, ts // ch, rows_of, (zero, zero))
        for sums, dw_ref, db_ref in ((sums_g, dwg_ref, dbg_ref), (sums_v, dwv_ref, dbv_ref)):
            for k in range(taps):
                dw_ref[k:k + 1, :] += sums[k]
            db_ref[...] += sums[taps]

    blocks = s_len // SUBLANE

    def half(off):
        return [pl.BlockSpec((ts, cb), lambda j, i: (i, off + j)),
                pl.BlockSpec((SUBLANE, cb), lambda j, i: (jnp.maximum(i * (ts // SUBLANE) - 1, 0), off + j)),
                pl.BlockSpec((SUBLANE, cb), lambda j, i: (jnp.minimum((i + 1) * (ts // SUBLANE), blocks - 1), off + j))]

    def par(rows, off):
        return pl.BlockSpec((rows, cb), lambda j, i: (0, off + j))

    d_specs = [pl.BlockSpec((ts, cb), lambda j, i: (i, j)),
               pl.BlockSpec((SUBLANE, cb), lambda j, i: (jnp.minimum((i + 1) * (ts // SUBLANE), blocks - 1), j))]
    out_par = [pl.BlockSpec((r, cb), lambda j, i: (0, j)) for r in (taps, taps, 1, 1)]
    return pl.pallas_call(
        kern, name=name, grid=(nf, n_i),
        in_specs=half(0) + half(nf) + d_specs + [par(taps, 0), par(taps, nf), par(1, 0), par(1, nf)],
        out_specs=[pl.BlockSpec((ts, cb), lambda j, i: (i, j))] * 2 + out_par,
        out_shape=[jax.ShapeDtypeStruct((s_len, D_FF), bf16)] * 2 + [jax.ShapeDtypeStruct((taps, D_FF), f32)] * 2
        + [jax.ShapeDtypeStruct((1, D_FF), f32)] * 2,
        scratch_shapes=[pltpu.VMEM((ts + 2 * SUBLANE, cb), f32)] * 2 + [pltpu.VMEM((ts + SUBLANE, cb), f32)],
        compiler_params=pltpu.CompilerParams(dimension_semantics=("parallel", "arbitrary")))(
        up, up, up, up, up, up, dact, dact, w, w, b, b)


SCAN_ROWS = 128


def _block_scan(a, b, reverse):
    t = a.shape[0]
    row = lax.broadcasted_iota(jnp.int32, a.shape, 0)
    d = 1
    while d < t:
        keep = row < t - d if reverse else row >= d
        shift = t - d if reverse else d
        a_far = jnp.where(keep, pltpu.roll(a, shift, 0), 1.0)
        b_far = jnp.where(keep, pltpu.roll(b, shift, 0), 0.0)
        b = a * b_far + b
        a = a * a_far
        d *= 2
    return a, b


def _scan_fwd(name, a, b):
    s_len, width = a.shape
    t = min(SCAN_ROWS, s_len)

    def kern(a_ref, b_ref, h_ref):
        def block(k, carry):
            rows = pl.ds(pl.multiple_of(k * t, t), t)
            acc, h = _block_scan(a_ref[rows, :], b_ref[rows, :], False)
            h_ref[rows, :] = h + acc * carry
            return h_ref[pl.ds(k * t + t - 1, 1), :]

        lax.fori_loop(0, s_len // t, block, jnp.zeros((1, LANE), f32))

    spec = pl.BlockSpec((s_len, LANE), lambda j: (0, j))
    return pl.pallas_call(
        kern, name=name, grid=(width // LANE,), in_specs=[spec, spec], out_specs=spec,
        out_shape=jax.ShapeDtypeStruct((s_len, width), f32),
        compiler_params=pltpu.CompilerParams(dimension_semantics=("parallel",)))(a, b)


def _scan_bwd(name, a_next, h_prev, dh):
    s_len, width = dh.shape
    t = min(SCAN_ROWS, s_len)
    n_blocks = s_len // t

    def kern(an_ref, hp_ref, dh_ref, da_ref, db_ref):
        def block(kk, carry):
            k = n_blocks - 1 - kk
            rows = pl.ds(pl.multiple_of(k * t, t), t)
            acc, g = _block_scan(an_ref[rows, :], dh_ref[rows, :], True)
            g = g + acc * carry
            db_ref[rows, :] = g
            da_ref[rows, :] = g * hp_ref[rows, :]
            return db_ref[pl.ds(k * t, 1), :]

        lax.fori_loop(0, n_blocks, block, jnp.zeros((1, LANE), f32))

    spec = pl.BlockSpec((s_len, LANE), lambda j: (0, j))
    return pl.pallas_call(
        kern, name=name, grid=(width // LANE,), in_specs=[spec, spec, spec], out_specs=[spec, spec],
        out_shape=[jax.ShapeDtypeStruct((s_len, width), f32)] * 2,
        compiler_params=pltpu.CompilerParams(dimension_semantics=("parallel",)))(a_next, h_prev, dh)


def _lane_cumsum(x, reverse):
    n = x.shape[1]
    lane = lax.broadcasted_iota(jnp.int32, x.shape, 1)
    sh = 1
    while sh < n:
        if reverse:
            x = x + jnp.where(lane < n - sh, pltpu.roll(x, n - sh, 1), 0.0)
        else:
            x = x + jnp.where(lane >= sh, pltpu.roll(x, sh, 1), 0.0)
        sh *= 2
    return x


def _decay_fwd(name, fl_t, b8):
    def kern(f_ref, b_ref, c_ref):
        c_ref[...] = _lane_cumsum(jax.nn.log_sigmoid(f_ref[...] + b_ref[...]), False)

    return pl.pallas_call(kern, name=name, out_shape=jax.ShapeDtypeStruct(fl_t.shape, f32))(fl_t, b8)


def _decay_bwd(name, fl_t, b8, dc_key, dc_query):
    def kern(f_ref, b_ref, dck_ref, dcq_ref, df_ref, db_ref):
        dlogf = _lane_cumsum(dck_ref[...] + dcq_ref[...], True)
        df = dlogf * jax.nn.sigmoid(-(f_ref[...] + b_ref[...]))
        df_ref[...] = df
        db_ref[...] = jnp.sum(df, axis=1, keepdims=True)

    return pl.pallas_call(kern, name=name, out_shape=[jax.ShapeDtypeStruct(fl_t.shape, f32),
                                                      jax.ShapeDtypeStruct((SUBLANE, 1), f32)])(fl_t, b8, dc_key, dc_query)


def _rms(x, g, n):
    return x * lax.rsqrt(jnp.sum(x * x, axis=-1, keepdims=True) * (1.0 / n) + EPS) * g


def _loss_head(name, h, target, g, tb=512):
    n, d = h.shape
    tb = min(tb, n)

    def kern(h_ref, t_ref, g_ref, loss_ref, dh_ref, dg_ref):
        i = pl.program_id(0)
        tgt = t_ref[...]

        def f(hv, gv):
            err = _rms(hv, gv, d) - tgt
            return 0.5 * jnp.sum(jnp.sum(err * err, axis=-1, keepdims=True) * (1.0 / d), axis=0, keepdims=True)

        val, vjp = jax.vjp(f, h_ref[...], g_ref[...])
        dh, dg = vjp(jnp.ones((1, 1), f32))
        dh_ref[...] = dh

        @pl.when(i == 0)
        def _():
            loss_ref[...] = jnp.zeros_like(loss_ref)
            dg_ref[...] = jnp.zeros_like(dg_ref)

        loss_ref[...] += val
        dg_ref[...] += dg

    return pl.pallas_call(
        kern, name=name, grid=(n // tb,),
        in_specs=[pl.BlockSpec((tb, d), lambda i: (i, 0)), pl.BlockSpec((tb, d), lambda i: (i, 0)),
                  pl.BlockSpec((1, d), lambda i: (0, 0))],
        out_specs=[pl.BlockSpec((1, 1), lambda i: (0, 0)), pl.BlockSpec((tb, d), lambda i: (i, 0)),
                   pl.BlockSpec((1, d), lambda i: (0, 0))],
        out_shape=[jax.ShapeDtypeStruct((1, 1), f32), jax.ShapeDtypeStruct((n, d), f32), jax.ShapeDtypeStruct((1, d), f32)],
        compiler_params=pltpu.CompilerParams(dimension_semantics=("arbitrary",)))(h, target, g)


def _f_norm(x, g):
    return (_rms(x, g, D_MODEL),)


def _f_latent(qc, kvc, gq, gkv):
    return _rms(qc, gq, MLA_Q_RANK), _rms(kvc, gkv, MLA_KV_RANK)


def _f_rope_table(pos, freq, m1, m2):
    ang = pos * freq
    sin = jnp.sin(ang)
    return jnp.cos(ang), -sin * m1, sin * m2


def _rope(x, cos, s_up, s_down):
    w = x.shape[1]
    return x * cos + _roll(x, w - MLA_ROPE // 2, 1) * s_up + _roll(x, MLA_ROPE // 2, 1) * s_down


def _f_mla_prep(q, kpart, kr, cos, s_up, s_down):
    def heads(t):
        return jnp.concatenate([t] * HEADS, axis=1)

    kr = _rope(kr, cos, s_up, s_down)
    return _rope(q, heads(cos), heads(s_up), heads(s_down)), kpart + heads(kr)


def _f_lru_gate(gates, xc, b_r, b_i, lam):
    r = jax.nn.sigmoid(gates[:, :LRU_WIDTH] + b_r)
    i = jax.nn.sigmoid(gates[:, LRU_WIDTH:] + b_i)
    log_a = -LRU_C * r * jax.nn.softplus(-lam)
    mult = jnp.sqrt(-jnp.tanh(log_a) * (1.0 + jnp.exp(2.0 * log_a)))
    return jnp.exp(log_a), mult * (i * xc)


def _f_merge(o_mla, o_fox, hs, lg, g):
    o_lru = hs * jax.nn.gelu(lg)
    return (jnp.concatenate([_rms(o_mla, g[:, :512], HEADS * MLA_V), _rms(o_fox, g[:, 512:1024], HEADS * FOX_HEAD_DIM),
                             _rms(o_lru, g[:, 1024:], LRU_WIDTH)], axis=1),)


def _f_ffn_gate(u):
    return (jax.nn.silu(u[:, :D_FF]) * u[:, D_FF:],)


def _f_ple(h, gpre, pp):
    return (h + jax.nn.sigmoid(gpre) * pp,)


MIX_PART = ["w_in", "w_uq", "w_ukv", "lru_conv_w"]
FFN_PART = ["w_o", "w_up", "ffn_conv_w", "w_down", "w_ple_gate", "w_ple_proj"]


def _prep_mix_weights(w):
    eye = jnp.eye(LRU_BLOCKS, dtype=f32)

    def block_diag(m):
        return (eye[:, None, :, None] * m[:, :, None, :]).reshape(LRU_WIDTH, LRU_WIDTH)

    return dict(
        w_in=_take_pad(w["w_in"], Z_MAP, 1),
        w_uq=_take_pad(_take_pad(w["w_uq"], UQ_COL_MAP, 1), UQ_ROW_MAP, 0),
        w_ukv=_take_pad(w["w_ukv"], UKV_MAP, 1),
        w_ri=jnp.concatenate([block_diag(w["w_r"]), block_diag(w["w_i"])], axis=1).astype(bf16),
        g_mix=w["g_mix"].reshape(1, -1), g_ffn=w["g_ffn"].reshape(1, -1), g_ple=w["g_ple"].reshape(1, -1),
        g_qc=_take_pad(w["g_qc"], UQ_ROW_MAP, 0).reshape(1, -1), g_kvc=w["g_kvc"].reshape(1, -1),
        g_out=_take_pad(w["g_out"], OMIX_MAP, 0).reshape(1, -1),
        b_f8=_take_pad(w["b_f"], _pad_to(np.arange(FOX_HEADS), SUBLANE), 0).reshape(SUBLANE, 1),
        lru_conv_w=w["lru_conv_w"], lru_conv_b=w["lru_conv_b"].reshape(1, -1),
        b_r=w["b_r"].reshape(1, -1), b_i=w["b_i"].reshape(1, -1), lam=w["lru_lambda"].reshape(1, -1),
        ffn_conv_b=w["ffn_conv_b"].reshape(1, -1),
    )


def _prep_ffn_weights(w):
    return dict(w_o=_take_pad(w["w_o"], OMIX_MAP, 0),
                w_up=w["w_up"], w_up_g=w["w_up"][:, :D_FF], w_up_v=w["w_up"][:, D_FF:], ffn_conv_w=w["ffn_conv_w"],
                w_down=w["w_down"], w_ple_gate=w["w_ple_gate"], w_ple_proj=w["w_ple_proj"])


def _rope_rows(pos):
    consts = [jnp.asarray(t) for t in _rope_tables(LANE, ROPE_AT)]
    return _rowwise("rope_table", _f_rope_table, [pos], consts, [(LANE, f32)] * 3)


def _key_decay(c_t, s_len):
    t = _att_tiles(s_len)[1]
    return c_t[:HEADS].reshape(HEADS, s_len // t, 1, t), c_t[:HEADS].reshape(HEADS, s_len, 1)


def _layer_fwd(l, h0, p_l, rope, weights_of):
    s_len = h0.shape[0]
    n = f"l{l}_"
    w = _prep_mix_weights(weights_of("mix", h0))
    xn, = _rowwise(n + "norm_mix", _f_norm, [h0], [w["g_mix"]], [(D_MODEL, bf16)])
    z = _mm(n + "in_proj", xn, w["w_in"])
    zq = (z, QC_W, Z_QC // QC_W)
    zkv = (z, LANE, Z_KVC // LANE)
    zkr = (z, LANE, Z_KR // LANE)
    zlx = (z, LRU_WIDTH, Z_LX // LRU_WIDTH)
    zlg = (z, LRU_WIDTH, Z_LG // LRU_WIDTH)
    qcn, kvn = _rowwise(n + "latent_norm", _f_latent, [zq, zkv], [w["g_qc"], w["g_kvc"]], [(QC_W, bf16), (LANE, bf16)])
    q = _mm(n + "uq", qcn, w["w_uq"])
    kv = _mm(n + "ukv", kvn, w["w_ukv"])
    kpart = (kv, HEADS * LANE, 0)
    qr, kk = _rowwise(n + "mla_prep", _f_mla_prep, [q, kpart, zkr, *rope], [],
                      [(HEADS * LANE, bf16), (HEADS * LANE, bf16)])
    mla_scale = (MLA_NOPE + MLA_ROPE) ** -0.5
    o_mla, lse_m, lse_m_row = _attn_fwd(n + "mla_fwd", (qr, 0), (kk, 0), (kv, HEADS), mla_scale)
    fl_t = z[:, Z_FL:Z_FL + SUBLANE].T
    c_t = _decay_fwd(n + "decay", fl_t, w["b_f8"])
    c_row, c_col = _key_decay(c_t, s_len)
    fox_scale = FOX_HEAD_DIM ** -0.5
    o_fox, lse_f, lse_f_row = _attn_fwd(n + "fox_fwd", (z, Z_FQ // LANE), (z, Z_FK // LANE), (z, Z_FV // LANE), fox_scale, c_row)
    xc = _conv_fwd(n + "lru_conv", zlx, w["lru_conv_w"], w["lru_conv_b"], LRU_CONV)
    gates = _mm(n + "lru_gates", xc, w["w_ri"])
    a, bx = _rowwise(n + "lru_gate", _f_lru_gate, [gates, xc], [w["b_r"], w["b_i"], w["lam"]],
                     [(LRU_WIDTH, f32), (LRU_WIDTH, f32)])
    hs = _scan_fwd(n + "lru_scan", a, bx)
    ocat, = _rowwise(n + "merge", _f_merge, [o_mla, o_fox, hs, zlg], [w["g_out"]], [(OMIX_W, bf16)])
    w.update(_prep_ffn_weights(weights_of("ffn", ocat)))
    h1 = _mm(n + "out_proj", ocat, w["w_o"], res=h0)
    xn2, = _rowwise(n + "norm_ffn", _f_norm, [h1], [w["g_ffn"]], [(D_MODEL, bf16)])
    up = _mm(n + "up_proj", xn2, w["w_up"])
    act = _ffn_act_fwd(n + "ffn_act", up, w["ffn_conv_w"], w["ffn_conv_b"])
    h2 = _mm(n + "down_proj", act, w["w_down"], res=h1)
    hn, = _rowwise(n + "norm_ple", _f_norm, [h2], [w["g_ple"]], [(D_MODEL, bf16)])
    gpre = _mm(n + "ple_gate", hn, w["w_ple_gate"])
    pp = _mm(n + "ple_proj", p_l, w["w_ple_proj"])
    h3, = _rowwise(n + "ple_mix", _f_ple, [h2, gpre, pp], [], [(D_MODEL, f32)])
    res = dict(h0=h0, xn=xn, z=z, qcn=qcn, kvn=kvn, q=q, kv=kv, qr=qr, kk=kk, o_mla=o_mla, lse_m=lse_m, fl_t=fl_t,
               lse_m_row=lse_m_row, lse_f_row=lse_f_row, c_row=c_row, c_col=c_col, o_fox=o_fox, lse_f=lse_f, xc=xc, gates=gates, a=a, hs=hs, ocat=ocat, h1=h1,
               xn2=xn2, up=up, act=act, h2=h2, hn=hn, gpre=gpre, pp=pp, p_l=p_l)
    return h3, res, w


def _layer_bwd(l, dh3, r, rope, w, token, grads_to):
    s_len = dh3.shape[0]
    n = f"l{l}_"
    g = {}
    w = dict(w, g_ple=w["g_ple"] + token)
    z = r["z"]
    zq = (z, QC_W, Z_QC // QC_W)
    zkv = (z, LANE, Z_KVC // LANE)
    zkr = (z, LANE, Z_KR // LANE)
    zlx = (z, LRU_WIDTH, Z_LX // LRU_WIDTH)
    zlg = (z, LRU_WIDTH, Z_LG // LRU_WIDTH)
    (dh2a, dgpre, dpp), _ = _rowwise_bwd(n + "ple_mix_b", _f_ple, [r["h2"], r["gpre"], r["pp"]], [], [dh3], 3,
                                         dts=[f32, bf16, bf16])
    g["w_ple_proj"] = _mm(n + "ple_proj_dw", r["p_l"], dpp, "tn", bf16)
    dhn = _mm(n + "ple_gate_dx", dgpre, w["w_ple_gate"], "nt")
    g["w_ple_gate"] = _mm(n + "ple_gate_dw", r["hn"], dgpre, "tn", bf16)
    (dh2,), (g["g_ple"],) = _rowwise_bwd(n + "norm_ple_b", _f_norm, [r["h2"]], [w["g_ple"]], [dhn], 1, adds={0: dh2a})
    dact = _mm(n + "down_dx", dh2, w["w_down"], "nt")
    g["w_down"] = _mm(n + "down_dw", r["act"], dh2, "tn", bf16)
    dup_g, dup_v, dcw_g, dcw_v, dcb_g, dcb_v = _ffn_act_bwd(n + "ffn_act_b", r["up"], dact, w["ffn_conv_w"], w["ffn_conv_b"])
    g["ffn_conv_w"] = jnp.concatenate([dcw_g, dcw_v], axis=1)
    g["ffn_conv_b"] = jnp.concatenate([dcb_g, dcb_v], axis=1)
    dxn2 = _mm(n + "up_dx_v", dup_v, w["w_up_v"], "nt", res=_mm(n + "up_dx_g", dup_g, w["w_up_g"], "nt"))
    g["w_up"] = jnp.concatenate([_mm(n + "up_dw_g", r["xn2"], dup_g, "tn", bf16),
                                 _mm(n + "up_dw_v", r["xn2"], dup_v, "tn", bf16)], axis=1)
    (dh1,), (g["g_ffn"],) = _rowwise_bwd(n + "norm_ffn_b", _f_norm, [r["h1"]], [w["g_ffn"]], [dxn2], 1, adds={0: dh2})
    docat = _mm(n + "out_dx", dh1, w["w_o"], "nt")
    g["w_o"] = _mm(n + "out_dw", r["ocat"], dh1, "tn", bf16)
    token = grads_to("ffn", dict(w_o=_take_inv(g["w_o"], OMIX_MAP, 0), w_up=g["w_up"], ffn_conv_w=g["ffn_conv_w"],
                                 w_down=g["w_down"], w_ple_gate=g["w_ple_gate"], w_ple_proj=g["w_ple_proj"]))
    w = dict(w, g_out=w["g_out"] + token)
    (do_mla, do_fox, dhs, dlg), (g["g_out"],) = _rowwise_bwd(
        n + "merge_b", _f_merge, [r["o_mla"], r["o_fox"], r["hs"], zlg], [w["g_out"]], [docat], 4)
    a, hs = r["a"], r["hs"]
    a_next = jnp.concatenate([a[1:], jnp.zeros((1, LRU_WIDTH), f32)], axis=0)
    h_prev = jnp.concatenate([jnp.zeros((1, LRU_WIDTH), f32), hs[:-1]], axis=0)
    da, dbx = _scan_bwd(n + "lru_scan_b", a_next, h_prev, dhs)
    (dgates, dxc_a), (g["b_r"], g["b_i"], g["lam"]) = _rowwise_bwd(
        n + "lru_gate_b", _f_lru_gate, [r["gates"], r["xc"]], [w["b_r"], w["b_i"], w["lam"]], [da, dbx], 2,
        dts=[bf16, f32])
    dxc_b = _mm(n + "lru_gates_dx", dgates, w["w_ri"], "nt")
    g["w_ri"] = _mm(n + "lru_gates_dw", r["xc"], dgates, "tn")
    dlx, g["lru_conv_w"], g["lru_conv_b"] = _conv_bwd(n + "lru_conv_b", zlx, dxc_a, w["lru_conv_w"], LRU_CONV, dout2=dxc_b)
    fox_scale = FOX_HEAD_DIM ** -0.5
    fq, fk, fv = (z, Z_FQ // LANE), (z, Z_FK // LANE), (z, Z_FV // LANE)
    dfq, delta_f, dc_q = _attn_dq(n + "fox_dq", fq, fk, fv, r["o_fox"], do_fox, r["lse_f"], fox_scale, r["c_row"])
    dfk, dfv, dc_k = _attn_dkv(n + "fox_dkv", fq, fk, fv, do_fox, r["lse_f_row"], delta_f, fox_scale,
                               r["c_col"])
    pad_rows = jnp.zeros((SUBLANE - HEADS, s_len), f32)
    dfl_t, g["b_f8"] = _decay_bwd(n + "decay_b", r["fl_t"], w["b_f8"],
                                  jnp.concatenate([dc_k.reshape(HEADS, s_len), pad_rows], axis=0),
                                  jnp.concatenate([dc_q.reshape(HEADS, s_len), pad_rows], axis=0))
    dfl = jnp.pad(dfl_t.T, ((0, 0), (0, LANE - SUBLANE)))
    mla_scale = (MLA_NOPE + MLA_ROPE) ** -0.5
    qr, kk, kv = (r["qr"], 0), (r["kk"], 0), (r["kv"], HEADS)
    dqr, delta_m, _ = _attn_dq(n + "mla_dq", qr, kk, kv, r["o_mla"], do_mla, r["lse_m"], mla_scale)
    dkk, dv_m = _attn_dkv(n + "mla_dkv", qr, kk, kv, do_mla, r["lse_m_row"], delta_m, mla_scale)
    (dq, dkpart, dkr), _ = _rowwise_bwd(n + "mla_prep_b", _f_mla_prep, [r["q"], (r["kv"], HEADS * LANE, 0), zkr, *rope],
                                        [], [dqr, dkk], 3, dts=[bf16, bf16, f32])
    dkv = jnp.concatenate([dkpart, dv_m.astype(bf16)], axis=1)
    dkvn = _mm(n + "ukv_dx", dkv, w["w_ukv"], "nt")
    g["w_ukv"] = _mm(n + "ukv_dw", r["kvn"], dkv, "tn", bf16)
    dqcn = _mm(n + "uq_dx", dq, w["w_uq"], "nt")
    g["w_uq"] = _mm(n + "uq_dw", r["qcn"], dq, "tn", bf16)
    (dqc, dkvc), (g["g_qc"], g["g_kvc"]) = _rowwise_bwd(n + "latent_norm_b", _f_latent, [zq, zkv],
                                                        [w["g_qc"], w["g_kvc"]], [dqcn, dkvn], 2)
    dz = jnp.concatenate([t.astype(bf16) for t in (dfq, dfk, dfv, dlx, dlg, dqc, dkvc, dkr, dfl)], axis=1)
    dxn = _mm(n + "in_dx", dz, w["w_in"], "nt")
    g["w_in"] = _mm(n + "in_dw", r["xn"], dz, "tn", bf16)
    (dh0,), (g["g_mix"],) = _rowwise_bwd(n + "norm_mix_b", _f_norm, [r["h0"]], [w["g_mix"]], [dxn], 1, adds={0: dh1})
    return dh0, grads_to("mix", _unpad_mix_grads(g))


def _unpad_mix_grads(g):
    d_ri = g["w_ri"]
    idx = jnp.arange(LRU_BLOCKS)

    def diag_blocks(m):
        return m.reshape(LRU_BLOCKS, LRU_BLOCK, LRU_BLOCKS, LRU_BLOCK)[idx, :, idx, :]

    return dict(
        g_mix=g["g_mix"][0], w_in=_take_inv(g["w_in"], Z_MAP, 1), g_qc=g["g_qc"][0, :MLA_Q_RANK],
        w_uq=_take_inv(g["w_uq"][:MLA_Q_RANK], UQ_COL_MAP, 1), g_kvc=g["g_kvc"][0],
        w_ukv=_take_inv(g["w_ukv"], UKV_MAP, 1), b_f=g["b_f8"][:FOX_HEADS, 0],
        lru_conv_w=g["lru_conv_w"], lru_conv_b=g["lru_conv_b"][0],
        w_r=diag_blocks(d_ri[:, :LRU_WIDTH]), b_r=g["b_r"][0], w_i=diag_blocks(d_ri[:, LRU_WIDTH:]), b_i=g["b_i"][0],
        lru_lambda=g["lam"][0], g_out=_take_inv(g["g_out"][0], OMIX_MAP, 0),
        g_ffn=g["g_ffn"][0], ffn_conv_b=g["ffn_conv_b"][0], g_ple=g["g_ple"][0],
    )


LAYER_WEIGHTS = ["g_mix", "w_in", "g_qc", "w_uq", "g_kvc", "w_ukv", "b_f", "lru_conv_w", "lru_conv_b", "w_r", "b_r", "w_i",
                 "b_i", "lru_lambda", "g_out", "w_o", "g_ffn", "w_up", "ffn_conv_w", "ffn_conv_b", "w_down", "g_ple",
                 "w_ple_gate", "w_ple_proj"]
WEIGHTS = LAYER_WEIGHTS + ["g_final"]


def _local_step(x, p, pos, target, g_final, weights_of, grads_to):
    h = x
    rope = _rope_rows(pos)
    ws, saved = [], []
    for l in range(DEPTH):
        h, r, w = _layer_fwd(l, h, p[l], rope, functools.partial(weights_of, l))
        ws.append(w)
        saved.append(r)
    loss, dh, dg_final = _loss_head("loss_head", h, target, g_final.reshape(1, -1))
    token = jnp.zeros((), f32)
    for l in reversed(range(DEPTH)):
        dh, token = _layer_bwd(l, dh, saved[l], rope, ws[l], token, functools.partial(grads_to, l))
    return loss[0, 0], dh, dg_final[0]


MESH_AXES = ("x", "y", "c")


def _row_tile(rows, cap):
    if rows <= cap:
        return rows
    for t in range(cap, SUBLANE - 1, -SUBLANE):
        if rows % t == 0:
            return t
    return rows


ADAM_BLOCK_BYTES = 2 ** 21


def _adamw(name, w, g, m, v):
    rows, cols = w.shape
    tr = _row_tile(rows, max(SUBLANE, ADAM_BLOCK_BYTES // (4 * cols) // SUBLANE * SUBLANE))

    def kern(w_ref, g_ref, m_ref, v_ref, d_ref, nm_ref, nv_ref):
        gv = g_ref[...]
        nm = ADAM_B1 * m_ref[...] + (1.0 - ADAM_B1) * gv
        nv = ADAM_B2 * v_ref[...] + (1.0 - ADAM_B2) * (gv * gv)
        m_hat = nm / (1.0 - ADAM_B1 ** ADAM_STEP)
        v_hat = nv / (1.0 - ADAM_B2 ** ADAM_STEP)
        d_ref[...] = -ADAM_LR * (m_hat / (jnp.sqrt(v_hat) + ADAM_EPS) + ADAM_WD * w_ref[...])
        nm_ref[...] = nm
        nv_ref[...] = nv

    spec = pl.BlockSpec((tr, cols), lambda i: (i, 0))
    return pl.pallas_call(
        kern, name=name, grid=(rows // tr,), in_specs=[spec] * 4, out_specs=[spec] * 3,
        out_shape=[jax.ShapeDtypeStruct((rows, cols), f32)] * 3,
        compiler_params=pltpu.CompilerParams(dimension_semantics=("parallel",)))(w, g, m, v)


def _packed_rows(shape):
    return -(-int(np.prod(shape)) // (SUBLANE * LANE)) * SUBLANE


def _pack(arrays):
    rows = []
    for a in arrays:
        flat = a.reshape(-1)
        rows.append(jnp.pad(flat, (0, _packed_rows(a.shape) * LANE - flat.shape[0])).reshape(-1, LANE))
    return jnp.concatenate(rows, axis=0)


def _unpack(buf, shapes):
    out, at = [], 0
    for s in shapes:
        rows = _packed_rows(s)
        out.append(buf[at:at + rows].reshape(-1)[:int(np.prod(s))].reshape(s))
        at += rows
    return out


SHARD_AXIS = {"w_in": 2, "w_uq": 2, "w_ukv": 2, "lru_conv_w": 2, "w_o": 1, "w_up": 2, "ffn_conv_w": 2, "w_down": 1,
              "w_ple_gate": 1, "w_ple_proj": 2}
SHARDED = [k for k in WEIGHTS if k in SHARD_AXIS]
REPLICATED = [k for k in WEIGHTS if k not in SHARD_AXIS]
ELEMENTWISE_F32 = ("lru_conv_w", "ffn_conv_w")
N_SHARDS = 4
BF16_TILE_ROWS = 16


HBM_SPEC = pl.BlockSpec(memory_space=pl.ANY)
SEM_SPEC = pl.BlockSpec(memory_space=pltpu.SEMAPHORE)
SPLIT_EFFECT = pltpu.SideEffectType.DATAFLOW_SIDE_EFFECTING
CHIP_FLIPS = ((1, 0), (0, 1), (1, 1))
N_DEVICES = 8
SUM_BLOCK_BYTES = 4 * 2 ** 20


def _device_index():
    return 4 * lax.axis_index("x") + 2 * lax.axis_index("y") + lax.axis_index("c")


def _when(cond, fn):
    if cond is None:
        fn()
    else:
        pl.when(cond)(fn)


class _Exchange:
    def __init__(self, name, plan, srcs, land_shapes, n_send, n_recv):
        self.name, self.plan, self.srcs, self.n = name, plan, list(srcs), len(srcs)
        self.land_shapes, self.n_send, self.n_recv = land_shapes, n_send, n_recv

    def run(self):
        n = self.n

        def body(*refs):
            sends, arrivals = self.plan(refs[:n], refs[n:2 * n], refs[2 * n], refs[2 * n + 1])
            for cond, cp in sends:
                _when(cond, cp.start)
            for cond, cp in arrivals:
                _when(cond, cp.wait_recv)
            for cond, cp in sends:
                _when(cond, cp.wait_send)

        return pl.pallas_call(
            body, name=self.name, out_shape=self.land_shapes, in_specs=[HBM_SPEC] * n, out_specs=[HBM_SPEC] * n,
            scratch_shapes=[pltpu.SemaphoreType.DMA((self.n_send,)), pltpu.SemaphoreType.DMA((self.n_recv,))])(*self.srcs)

    def start(self, after=None):
        n = self.n
        lands = [lax.empty(s.shape, s.dtype) for s in self.land_shapes]
        extra = [] if after is None else [after]

        def body(*refs):
            ins, lands_in = refs[:n], refs[n:2 * n]
            send_sems, recv_sems, token = refs[2 * n + len(extra)], refs[2 * n + len(extra) + 1], refs[-1]
            sends, _ = self.plan(ins, lands_in, send_sems, recv_sems)
            for cond, cp in sends:
                _when(cond, cp.start)
            token[...] = jnp.zeros_like(token)

        hbm = [pltpu.with_memory_space_constraint(a, pltpu.HBM) for a in self.srcs + lands]
        res = pl.pallas_call(
            body, name=self.name + "_start",
            out_shape=(pltpu.SemaphoreType.DMA((self.n_send,)), pltpu.SemaphoreType.DMA((self.n_recv,)),
                       *[pltpu.HBM(a.shape, a.dtype) for a in hbm], jax.ShapeDtypeStruct((SUBLANE, LANE), f32)),
            in_specs=[HBM_SPEC] * (2 * n + len(extra)),
            out_specs=(SEM_SPEC, SEM_SPEC, *[HBM_SPEC] * (2 * n), pl.BlockSpec(memory_space=pltpu.VMEM)),
            input_output_aliases={i: 2 + i for i in range(2 * n)},
            compiler_params=pltpu.CompilerParams(has_side_effects=SPLIT_EFFECT))(*hbm, *extra)
        self.sems, self.thru, token = res[:2], res[2:2 + 2 * n], res[-1]
        return token[0, 0]

    def finish(self, after):
        n = self.n

        def body(*refs):
            ins, lands_in, send_sems, recv_sems = refs[:n], refs[n:2 * n], refs[2 * n], refs[2 * n + 1]
            sends, arrivals = self.plan(ins, lands_in, send_sems, recv_sems)
            for cond, cp in arrivals:
                _when(cond, cp.wait_recv)
            for cond, cp in sends:
                _when(cond, cp.wait_send)

        res = pl.pallas_call(
            body, name=self.name + "_finish", out_shape=tuple(pltpu.HBM(a.shape, a.dtype) for a in self.thru),
            in_specs=[HBM_SPEC] * (2 * n) + [SEM_SPEC, SEM_SPEC, HBM_SPEC], out_specs=tuple([HBM_SPEC] * (2 * n)),
            input_output_aliases={i: i for i in range(2 * n)},
            compiler_params=pltpu.CompilerParams(has_side_effects=SPLIT_EFFECT))(*self.thru, *self.sems, after)
        return list(res[n:])


def _gather_exchange(name, shards):
    def plan(ins, lands, send_sems, recv_sems):
        x, y, c = (lax.axis_index(a) for a in MESH_AXES)
        copies = []
        for i in range(len(ins)):
            for k, (fx, fy) in enumerate(CHIP_FLIPS):
                peer = (1 - x if fx else x, 1 - y if fy else y, c)
                copies.append((None, pltpu.make_async_remote_copy(
                    src_ref=ins[i], dst_ref=lands[i].at[2 * x + y], send_sem=send_sems.at[3 * i + k],
                    recv_sem=recv_sems.at[3 * i + k], device_id=peer, device_id_type=pl.DeviceIdType.MESH)))
        return copies, copies

    n = len(shards)
    return _Exchange(name, plan, shards, [jax.ShapeDtypeStruct((N_SHARDS,) + s.shape, s.dtype) for s in shards], 3 * n, 3 * n)


def _scatter_exchange(name, layer, chunks):
    def plan(ins, lands, send_sems, recv_sems):
        x, y, c = (lax.axis_index(a) for a in MESH_AXES)
        me = _device_index()
        sends, arrivals = [], []
        for i in range(len(ins)):
            for j in range(N_SHARDS):
                target = (j // 2, j % 2, layer)
                remote = jnp.logical_not((x == target[0]) & (y == target[1]) & (c == layer))
                sends.append((remote, pltpu.make_async_remote_copy(
                    src_ref=ins[i].at[j], dst_ref=lands[i].at[me], send_sem=send_sems.at[N_SHARDS * i + j],
                    recv_sem=recv_sems.at[N_DEVICES * i + me], device_id=target, device_id_type=pl.DeviceIdType.MESH)))
            for s in range(N_DEVICES):
                arrivals.append(((c == layer) & (me != s), pltpu.make_async_remote_copy(
                    src_ref=ins[i].at[0], dst_ref=lands[i].at[s], send_sem=send_sems.at[0],
                    recv_sem=recv_sems.at[N_DEVICES * i + s], device_id=(x, y, c), device_id_type=pl.DeviceIdType.MESH)))
        return sends, arrivals

    n = len(chunks)
    lands = [jax.ShapeDtypeStruct((N_DEVICES,) + ch.shape[1:], ch.dtype) for ch in chunks]
    return _Exchange(name, plan, chunks, lands, N_SHARDS * n, N_DEVICES * n)


def _sum_contributions(name, got, mine):
    _, a, b = got.shape
    ta = _row_tile(a, max(SUBLANE, SUM_BLOCK_BYTES // (N_DEVICES * b * got.dtype.itemsize) // SUBLANE * SUBLANE))

    def kern(got_ref, mine_ref, o_ref):
        me = _device_index()
        acc = jnp.zeros(o_ref.shape, f32)
        for s in range(N_DEVICES):
            acc = acc + jnp.where(me == s, mine_ref[...].astype(f32), got_ref[s].astype(f32))
        o_ref[...] = acc

    return pl.pallas_call(
        kern, name=name, grid=(a // ta,),
        in_specs=[pl.BlockSpec((N_DEVICES, ta, b), lambda i: (0, i, 0)), pl.BlockSpec((ta, b), lambda i: (i, 0))],
        out_specs=pl.BlockSpec((ta, b), lambda i: (i, 0)), out_shape=jax.ShapeDtypeStruct((a, b), f32),
        compiler_params=pltpu.CompilerParams(dimension_semantics=("parallel",)))(got, mine)


def _swap_layers(name, sums):
    n = len(sums[0])

    def body(*refs):
        srcs = (refs[:n], refs[n:2 * n])
        outs, (send_sems, recv_sems) = refs[2 * n:3 * n], refs[3 * n:]
        x, y, c = (lax.axis_index(a) for a in MESH_AXES)
        for i in range(n):
            for layer in range(DEPTH):
                cp = pltpu.make_async_remote_copy(
                    src_ref=srcs[layer][i], dst_ref=outs[i], send_sem=send_sems.at[i], recv_sem=recv_sems.at[i],
                    device_id=(x, y, 1 - c), device_id_type=pl.DeviceIdType.MESH)
                pl.when(c == layer)(cp.start)
        for i in range(n):
            pltpu.make_async_remote_copy(
                src_ref=srcs[0][i], dst_ref=outs[i], send_sem=send_sems.at[i], recv_sem=recv_sems.at[i],
                device_id=(x, y, 1 - c), device_id_type=pl.DeviceIdType.MESH).wait()

    return pl.pallas_call(
        body, name=name, out_shape=[jax.ShapeDtypeStruct(s.shape, s.dtype) for s in sums[0]],
        in_specs=[HBM_SPEC] * (2 * n), out_specs=[HBM_SPEC] * n,
        scratch_shapes=[pltpu.SemaphoreType.DMA((n,)), pltpu.SemaphoreType.DMA((n,))])(*sums[0], *sums[1])


def _stack_shards(g, axis):
    if axis == 1:
        return g.reshape(N_SHARDS, g.shape[0] // N_SHARDS, g.shape[1])
    return g.reshape(g.shape[0], N_SHARDS, g.shape[1] // N_SHARDS).transpose(1, 0, 2)


def _join_shards(s, axis):
    if axis == 1:
        return s.reshape(-1, s.shape[2])
    return s.transpose(1, 0, 2).reshape(s.shape[1], -1)


def _layer_shards(w, l, names):
    return [w[k][l] if k in ELEMENTWISE_F32 else w[k][l].astype(bf16) for k in names]


def _full_weights(names, sent, got):
    j = 2 * lax.axis_index("x") + lax.axis_index("y")
    return {k: _join_shards(lax.dynamic_update_slice(g, own[None], (j, 0, 0)), SHARD_AXIS[k])
            for k, own, g in zip(names, sent, got)}


def _grad_chunks(grads, names):
    return [_stack_shards(grads[k], SHARD_AXIS[k]).astype(bf16) for k in names]


def _sum_group(l, names, got, chunks):
    j = 2 * lax.axis_index("x") + lax.axis_index("y")
    return {k: _sum_contributions(f"sum_l{l}_{k}", g, lax.dynamic_index_in_dim(ch, j, 0, keepdims=False))
            for k, g, ch in zip(names, got, chunks)}


def _both_layers(name, names, sums):
    c = lax.axis_index("c")
    mine = [[sums[l][k] for k in names] for l in range(DEPTH)]
    other = _swap_layers(name, mine)
    return {k: jnp.stack([jnp.where(c == 0, mine[0][i], other[i]), jnp.where(c == 0, other[i], mine[1][i])])
            for i, k in enumerate(names)}


def _gather_all_exchange(name, src):
    def plan(ins, lands, send_sems, recv_sems):
        coords = [lax.axis_index(a) for a in MESH_AXES]
        me = _device_index()
        sends, arrivals = [], []
        for f in range(1, N_DEVICES):
            peer = tuple(1 - cd if (f >> (2 - b)) & 1 else cd for b, cd in enumerate(coords))
            sends.append((None, pltpu.make_async_remote_copy(
                src_ref=ins[0], dst_ref=lands[0].at[me], send_sem=send_sems.at[f - 1], recv_sem=recv_sems.at[me],
                device_id=peer, device_id_type=pl.DeviceIdType.MESH)))
        for s in range(N_DEVICES):
            arrivals.append((me != s, pltpu.make_async_remote_copy(
                src_ref=ins[0], dst_ref=lands[0].at[s], send_sem=send_sems.at[0], recv_sem=recv_sems.at[s],
                device_id=tuple(coords), device_id_type=pl.DeviceIdType.MESH)))
        return sends, arrivals

    return _Exchange(name, plan, [src], [jax.ShapeDtypeStruct((N_DEVICES,) + src.shape, src.dtype)], N_DEVICES - 1, N_DEVICES)


def kernel(x, p, positions, g_mix, w_in, g_qc, w_uq, g_kvc, w_ukv, b_f, lru_conv_w, lru_conv_b, w_r, b_r, w_i, b_i, lru_lambda, g_out, w_o, g_ffn, w_up, ffn_conv_w, ffn_conv_b, w_down, g_ple, w_ple_gate, w_ple_proj, g_final, loss_target, m_g_mix, m_w_in, m_g_qc, m_w_uq, m_g_kvc, m_w_ukv, m_b_f, m_lru_conv_w, m_lru_conv_b, m_w_r, m_b_r, m_w_i, m_b_i, m_lru_lambda, m_g_out, m_w_o, m_g_ffn, m_w_up, m_ffn_conv_w, m_ffn_conv_b, m_w_down, m_g_ple, m_w_ple_gate, m_w_ple_proj, m_g_final, v_g_mix, v_w_in, v_g_qc, v_w_uq, v_g_kvc, v_w_ukv, v_b_f, v_lru_conv_w, v_lru_conv_b, v_w_r, v_b_r, v_w_i, v_b_i, v_lru_lambda, v_g_out, v_w_o, v_g_ffn, v_w_up, v_ffn_conv_w, v_ffn_conv_b, v_w_down, v_g_ple, v_w_ple_gate, v_w_ple_proj, v_g_final):
    given = locals()
    w = {k: given[k] for k in WEIGHTS}
    m = {k: given["m_" + k] for k in WEIGHTS}
    v = {k: given["v_" + k] for k in WEIGHTS}

    parts = {"mix": MIX_PART, "ffn": FFN_PART}
    groups = [(l, part) for l in range(DEPTH) for part in ("mix", "ffn")]
    sent = {g: _layer_shards(w, g[0], parts[g[1]]) for g in groups}
    first = _gather_exchange("gather_l0_mix", sent[groups[0]]).run()
    ahead = {g: _gather_exchange(f"gather_l{g[0]}_{g[1]}", sent[g]) for g in groups[1:]}
    pos = positions[0].astype(f32).reshape(-1, 1) + ahead[groups[1]].start(after=first[0])
    behind, layer_grads, chunks = {}, [{} for _ in range(DEPTH)], {}

    def weights_of(l, part, after):
        g = (l, part)
        got = first if g == groups[0] else ahead[g].finish(after=after)
        full = _full_weights(parts[part], sent[g], got)
        if part == "mix":
            full.update({k: w[k][l] for k in LAYER_WEIGHTS if k in REPLICATED})
        if g == groups[1]:
            for later in groups[2:]:
                full["ffn_conv_w"] = full["ffn_conv_w"] + ahead[later].start(after=got[0])
        return full

    def grads_to(l, part, grads):
        g = (l, part)
        layer_grads[l].update(grads)
        chunks[g] = _grad_chunks(grads, parts[part])
        if g == groups[0]:
            return jnp.zeros((), f32)
        behind[g] = _scatter_exchange(f"scatter_l{l}_{part}", l, chunks[g])
        return behind[g].start()

    loss, dx, dg_final = _local_step(x[0], p[:, 0], pos, loss_target[0], w["g_final"], weights_of, grads_to)

    grads = {k: jnp.stack([layer_grads[l][k] for l in range(DEPTH)]) for k in LAYER_WEIGHTS if k in REPLICATED}
    grads["g_final"] = dg_final
    rep_shapes = [w[k].shape for k in REPLICATED] + [(1,)]
    contrib = _pack([grads[k] for k in REPLICATED] + [loss.reshape(1)])
    last = _scatter_exchange("scatter_l0_mix", 0, chunks[groups[0]])
    everyone = _gather_all_exchange("gather_replicated", contrib)
    started = (last.start() + everyone.start() + dx[0, 0]).reshape(1, 1)

    def adamw_of(names, g_sharded):
        out = {}
        for k in names:
            shape = w[k].shape
            flat = [t.reshape(-1, shape[-1]) for t in (w[k], g_sharded[k], m[k], v[k])]
            out[k] = [t.reshape(shape) for t in (flat[1],) + tuple(_adamw("adamw_" + k, *flat))]
        return out

    sums = [{} for _ in range(DEPTH)]
    for g in groups[1:]:
        sums[g[0]].update(_sum_group(g[0], parts[g[1]], behind[g].finish(after=started), chunks[g]))
    big = adamw_of(FFN_PART, _both_layers("swap_ffn", FFN_PART, sums))
    sums[0].update(_sum_group(0, MIX_PART, last.finish(after=big[FFN_PART[0]][1]), chunks[groups[0]]))
    big.update(adamw_of(MIX_PART, _both_layers("swap_mix", MIX_PART, sums)))

    g_rep = _sum_contributions("sum_replicated", everyone.finish(after=big[MIX_PART[0]][1])[0], contrib)
    zero = jnp.zeros((1,), f32)
    w_rep, m_rep, v_rep = (_pack([t[k] for k in REPLICATED] + [zero]) for t in (w, m, v))
    rep = [_unpack(b, rep_shapes) for b in (g_rep,) + tuple(_adamw("adamw_replicated", w_rep, g_rep, m_rep, v_rep))]

    outs = []
    for kind in range(4):
        by_name = {k: big[k][kind] for k in SHARDED}
        by_name.update(zip(REPLICATED, rep[kind][:-1]))
        outs.append([by_name[k] for k in WEIGHTS])
    total_loss = rep[0][-1][0]
    return (total_loss, dx.reshape(x.shape), *outs[0], *outs[1], *outs[2], *outs[3])
```

```python
import functools
import math

import numpy as np
import jax
import jax.numpy as jnp
from jax import lax
from jax.experimental import pallas as pl
from jax.experimental.pallas import tpu as pltpu

f32, bf16 = jnp.float32, jnp.bfloat16

D_MODEL = 1024
PLE_DIM = 256
MLA_HEADS, MLA_NOPE, MLA_ROPE, MLA_V = 4, 64, 32, 64
MLA_Q_RANK, MLA_KV_RANK = 192, 128
FOX_HEADS, FOX_HEAD_DIM = 4, 64
LRU_WIDTH, LRU_BLOCKS, LRU_BLOCK, LRU_CONV, LRU_C = 512, 8, 64, 4, 8.0
D_FF, FFN_CONV = 2816, 3
ROPE_THETA = 10000.0
EPS = 1e-6
DEPTH = 2
ADAM_LR, ADAM_B1, ADAM_B2, ADAM_EPS, ADAM_WD, ADAM_STEP = 0.001, 0.9, 0.999, 1e-08, 0.01, 10

LANE = 128
SUBLANE = 8
HEADS = 4

Z_FQ, Z_FK, Z_FV, Z_LX, Z_LG, Z_QC, Z_KVC, Z_KR, Z_FL, Z_W = 0, 512, 1024, 1536, 2048, 2560, 2816, 2944, 3072, 3200
QC_W = 256
ROPE_AT = 64


def _head_pad_map(n_heads, width):
    m = -np.ones(n_heads * LANE, np.int64)
    for h in range(n_heads):
        m[h * LANE:h * LANE + width] = h * width + np.arange(width)
    return m


def _z_map():
    m = -np.ones(Z_W, np.int64)
    o_qc, o_kvc, o_kr = 0, MLA_Q_RANK, MLA_Q_RANK + MLA_KV_RANK
    o_fq = o_kr + MLA_ROPE
    o_fk, o_fv = o_fq + 256, o_fq + 512
    o_fl = o_fv + 256
    o_lx = o_fl + FOX_HEADS
    o_lg = o_lx + LRU_WIDTH
    m[Z_FQ:Z_FQ + 512] = np.where(_head_pad_map(4, 64) >= 0, _head_pad_map(4, 64) + o_fq, -1)
    m[Z_FK:Z_FK + 512] = np.where(_head_pad_map(4, 64) >= 0, _head_pad_map(4, 64) + o_fk, -1)
    m[Z_FV:Z_FV + 512] = np.where(_head_pad_map(4, 64) >= 0, _head_pad_map(4, 64) + o_fv, -1)
    m[Z_LX:Z_LX + 512] = o_lx + np.arange(512)
    m[Z_LG:Z_LG + 512] = o_lg + np.arange(512)
    m[Z_QC:Z_QC + MLA_Q_RANK] = o_qc + np.arange(MLA_Q_RANK)
    m[Z_KVC:Z_KVC + MLA_KV_RANK] = o_kvc + np.arange(MLA_KV_RANK)
    m[Z_KR + ROPE_AT:Z_KR + ROPE_AT + MLA_ROPE] = o_kr + np.arange(MLA_ROPE)
    m[Z_FL:Z_FL + FOX_HEADS] = o_fl + np.arange(FOX_HEADS)
    return m


def _ukv_map():
    m = -np.ones(2 * HEADS * LANE, np.int64)
    for h in range(HEADS):
        m[h * LANE:h * LANE + MLA_NOPE] = h * (MLA_NOPE + MLA_V) + np.arange(MLA_NOPE)
        m[HEADS * LANE + h * LANE:HEADS * LANE + h * LANE + MLA_V] = h * (MLA_NOPE + MLA_V) + MLA_NOPE + np.arange(MLA_V)
    return m


def _omix_map():
    return np.concatenate([_head_pad_map(4, 64), np.where(_head_pad_map(4, 64) >= 0, _head_pad_map(4, 64) + 256, -1),
                           512 + np.arange(512)])


def _pad_to(m, n):
    return np.concatenate([m, -np.ones(n - m.shape[0], np.int64)])


def _runs(m):
    out, at = [], 0
    while at < len(m):
        end = at + 1
        while end < len(m) and (m[end] == m[end - 1] + 1 if m[at] >= 0 else m[end] < 0):
            end += 1
        out.append((int(m[at]), end - at))
        at = end
    return out


def _take_runs(a, m, axis):
    parts = []
    for start, size in _runs(m):
        if start < 0:
            shape = list(a.shape)
            shape[axis] = size
            parts.append(jnp.zeros(shape, a.dtype))
        else:
            parts.append(lax.slice_in_dim(a, start, start + size, axis=axis))
    return parts[0] if len(parts) == 1 else jnp.concatenate(parts, axis=axis)


def _take_pad(a, m, axis):
    return _take_runs(a, m, axis)


def _take_inv(a, m, axis):
    n = int(m.max()) + 1
    inv = np.zeros(n, np.int64)
    inv[m[m >= 0]] = np.nonzero(m >= 0)[0]
    return _take_runs(a, inv, axis)


Z_MAP = _z_map()
UQ_COL_MAP = _head_pad_map(HEADS, MLA_NOPE + MLA_ROPE)
UQ_ROW_MAP = _pad_to(np.arange(MLA_Q_RANK), QC_W)
UKV_MAP = _ukv_map()
OMIX_MAP = _omix_map()
OMIX_W = 1536


def _rope_tables(width, at):
    half = MLA_ROPE // 2
    inv = ROPE_THETA ** (-np.arange(half, dtype=np.float32) / half)
    freq = np.zeros((1, width), np.float32)
    m1 = np.zeros((1, width), np.float32)
    m2 = np.zeros((1, width), np.float32)
    for h in range(width // LANE):
        b = h * LANE + at
        freq[0, b:b + half] = inv
        freq[0, b + half:b + 2 * half] = inv
        m1[0, b:b + half] = 1.0
        m2[0, b + half:b + 2 * half] = 1.0
    return freq, m1, m2


def _view(r):
    return r if isinstance(r, tuple) else (r, r.shape[1], 0)


def _blk(dim, cap):
    if dim <= cap:
        return dim
    for b in range(cap, LANE - 1, -LANE):
        if dim % b == 0:
            return b
    return dim


@functools.partial(jax.custom_vjp, nondiff_argnums=(1, 2))
def _roll(x, shift, axis):
    return pltpu.roll(x, shift, axis)


def _roll_fwd(x, shift, axis):
    return pltpu.roll(x, shift, axis), None


def _roll_bwd(shift, axis, _, g):
    return (pltpu.roll(g, g.shape[axis] - shift, axis),)


_roll.defvjp(_roll_fwd, _roll_bwd)


ROW_VMEM_BUDGET = 20 * 2 ** 20
ROW_TILES = (1024, 512, 256)


def _row_block(n, bytes_per_row):
    for tb in ROW_TILES:
        if n % tb == 0 and 2 * tb * bytes_per_row <= ROW_VMEM_BUDGET:
            return tb
    return min(ROW_TILES[-1], n)


def _rowwise(name, fn, rows, pars, outs):
    rows = [_view(r) for r in rows]
    n = rows[0][0].shape[0]
    tb = _row_block(n, sum(w * a.dtype.itemsize for a, w, _ in rows) + sum(w * jnp.dtype(dt).itemsize for w, dt in outs))
    nr, npar = len(rows), len(pars)

    def kern(*refs):
        r = [refs[k][...].astype(f32) for k in range(nr)]
        p = [refs[nr + k][...] for k in range(npar)]
        res = fn(*r, *p)
        for o_ref, o in zip(refs[nr + npar:], res):
            o_ref[...] = o.astype(o_ref.dtype)

    in_specs = [pl.BlockSpec((tb, w), lambda i, j=idx: (i, j)) for (_, w, idx) in rows]
    in_specs += [pl.BlockSpec(p.shape, lambda i: (0, 0)) for p in pars]
    out_specs = [pl.BlockSpec((tb, w), lambda i: (i, 0)) for (w, _) in outs]
    out_shape = [jax.ShapeDtypeStruct((n, w), dt) for (w, dt) in outs]
    return pl.pallas_call(kern, name=name, grid=(n // tb,), in_specs=in_specs, out_specs=out_specs, out_shape=out_shape,
                          compiler_params=pltpu.CompilerParams(dimension_semantics=("parallel",)))(*[r[0] for r in rows], *pars)


def _rowwise_bwd(name, fn, rows, pars, cts, ndiff, adds=None, dts=None):
    rows = [_view(r) for r in rows]
    dts = dts or [f32] * ndiff
    adds = adds or {}
    add_keys = sorted(adds)
    n = rows[0][0].shape[0]
    tb = _row_block(n, sum(w * a.dtype.itemsize for a, w, _ in rows) + sum(c.shape[1] * c.dtype.itemsize for c in cts)
                    + sum(a.shape[1] * a.dtype.itemsize for a in adds.values())
                    + sum(rows[k][1] * jnp.dtype(dts[k]).itemsize for k in range(ndiff)))
    nr, npar, nct, nadd = len(rows), len(pars), len(cts), len(add_keys)

    def kern(*refs):
        i = pl.program_id(0)
        r = [refs[k][...].astype(f32) for k in range(nr)]
        p = [refs[nr + k][...] for k in range(npar)]
        ct = [refs[nr + npar + k][...].astype(f32) for k in range(nct)]
        ad = {key: refs[nr + npar + nct + k][...] for k, key in enumerate(add_keys)}
        o_refs = refs[nr + npar + nct + nadd:]

        def g(*d):
            return tuple(fn(*d[:ndiff], *r[ndiff:], *d[ndiff:]))

        _, vjp = jax.vjp(g, *r[:ndiff], *p)
        grads = vjp(tuple(ct))
        for k in range(ndiff):
            gk = grads[k]
            if k in ad:
                gk = gk + ad[k]
            o_refs[k][...] = gk.astype(o_refs[k].dtype)

        @pl.when(i == 0)
        def _():
            for k in range(npar):
                o_refs[ndiff + k][...] = jnp.zeros_like(o_refs[ndiff + k])

        for k in range(npar):
            o_refs[ndiff + k][...] += grads[ndiff + k]

    in_specs = [pl.BlockSpec((tb, w), lambda i, j=idx: (i, j)) for (_, w, idx) in rows]
    in_specs += [pl.BlockSpec(p.shape, lambda i: (0, 0)) for p in pars]
    in_specs += [pl.BlockSpec((tb, c.shape[1]), lambda i: (i, 0)) for c in cts]
    in_specs += [pl.BlockSpec((tb, adds[k].shape[1]), lambda i: (i, 0)) for k in add_keys]
    out_specs = [pl.BlockSpec((tb, rows[k][1]), lambda i: (i, 0)) for k in range(ndiff)]
    out_specs += [pl.BlockSpec(p.shape, lambda i: (0, 0)) for p in pars]
    out_shape = [jax.ShapeDtypeStruct((n, rows[k][1]), dts[k]) for k in range(ndiff)]
    out_shape += [jax.ShapeDtypeStruct(p.shape, f32) for p in pars]
    res = pl.pallas_call(kern, name=name, grid=(n // tb,), in_specs=in_specs, out_specs=out_specs, out_shape=out_shape,
                         compiler_params=pltpu.CompilerParams(dimension_semantics=("arbitrary",)))(
        *[r[0] for r in rows], *pars, *cts, *[adds[k] for k in add_keys])
    return res[:ndiff], res[ndiff:]


_DOT_DIMS = {"nn": ((1,), (0,)), "nt": ((1,), (1,)), "tn": ((0,), (0,))}

MM_VMEM_BUDGET = 36 * 2 ** 20
MM_MAX_TM = 1408
MM_STEP, MM_RESULT, MM_XPOSE, MM_CAST = 700.0, 7.5e-4, 9e-4, 1e-3


def _tile_candidates(dim):
    c = [d for d in range(LANE, dim + 1, LANE) if dim % d == 0]
    return c or [dim]


@functools.lru_cache(maxsize=None)
def _mm_tiles(mode, m, n, k, a_bytes, b_bytes, o_bytes):
    best, best_cost = None, None
    for tm in _tile_candidates(m):
        if tm > MM_MAX_TM:
            continue
        for tn in _tile_candidates(n):
            for tk in _tile_candidates(k):
                vmem = 2 * (tm * tk * a_bytes + tk * tn * b_bytes + tm * tn * o_bytes) + 4 * tm * tn * (2 if tk < k else 1)
                vmem += (2 * tm * tk if a_bytes > 2 else 0) + (2 * tk * tn if b_bytes > 2 else 0)
                if vmem > MM_VMEM_BUDGET:
                    continue
                steps = (m // tm) * (n // tn) * (k // tk)
                cost = steps * MM_STEP + m * n * (k // tk) * MM_RESULT
                if mode == "tn":
                    cost += m * k * (n // tn) * MM_XPOSE
                cost += (m * k * (n // tn) * MM_CAST if a_bytes > 2 else 0) + (k * n * (m // tm) * MM_CAST if b_bytes > 2 else 0)
                if best is None or cost < best_cost:
                    best, best_cost = (tm, tn, tk), cost
    return best


def _mm(name, a, b, mode="nn", out_dtype=f32, res=None):
    if mode == "nn":
        (m, k), (_, n) = a.shape, b.shape
    elif mode == "nt":
        (m, k), (n, _) = a.shape, b.shape
    else:
        (k, m), (_, n) = a.shape, b.shape
    has_res = res is not None
    tm, tn, tk = _mm_tiles(mode, m, n, k, a.dtype.itemsize, b.dtype.itemsize,
                           jnp.dtype(out_dtype).itemsize + (res.dtype.itemsize if has_res else 0))
    nk = k // tk
    dims = (_DOT_DIMS[mode], ((), ()))

    def kern(*refs):
        a_ref, b_ref = refs[0], refs[1]
        o_ref, acc_ref = refs[-2], refs[-1]
        kk = pl.program_id(2)
        part = lax.dot_general(a_ref[...].astype(bf16), b_ref[...].astype(bf16), dims, preferred_element_type=f32)

        def finish(out):
            if has_res:
                out = out + refs[2][...]
            o_ref[...] = out.astype(o_ref.dtype)

        if nk == 1:
            finish(part)
            return

        @pl.when(kk == 0)
        def _():
            acc_ref[...] = part

        @pl.when(jnp.logical_and(kk > 0, kk < nk - 1))
        def _():
            acc_ref[...] += part

        @pl.when(kk == nk - 1)
        def _():
            finish(acc_ref[...] + part)

    if mode == "tn":
        a_spec = pl.BlockSpec((tk, tm), lambda i, j, kk: (kk, i))
    else:
        a_spec = pl.BlockSpec((tm, tk), lambda i, j, kk: (i, kk))
    if mode == "nt":
        b_spec = pl.BlockSpec((tn, tk), lambda i, j, kk: (j, kk))
    else:
        b_spec = pl.BlockSpec((tk, tn), lambda i, j, kk: (kk, j))
    in_specs = [a_spec, b_spec]
    args = [a, b]
    if has_res:
        in_specs.append(pl.BlockSpec((tm, tn), lambda i, j, kk: (i, j)))
        args.append(res)
    return pl.pallas_call(
        kern, name=name, grid=(m // tm, n // tn, nk), in_specs=in_specs,
        out_specs=pl.BlockSpec((tm, tn), lambda i, j, kk: (i, j)),
        out_shape=jax.ShapeDtypeStruct((m, n), out_dtype),
        scratch_shapes=[pltpu.VMEM((tm, tn) if nk > 1 else (SUBLANE, LANE), f32)],
        compiler_params=pltpu.CompilerParams(dimension_semantics=("parallel", "parallel", "arbitrary")))(*args)


ATT_TQ, ATT_TK = 512, 512


def _att_tiles(s_len):
    tk = min(ATT_TK, s_len)
    return min(ATT_TQ, tk), tk


def _fold_scale(scale):
    return (scale, 1.0) if math.frexp(scale)[0] == 0.5 else (1.0, scale)


def _as_row(col):
    return jnp.max(jnp.broadcast_to(col, (col.shape[0], LANE)).T[:SUBLANE], axis=0, keepdims=True)


def _scores_t(kb, q_t, s_mul, ck, diag_offset, tq, tk):
    s = jnp.dot(kb, q_t, preferred_element_type=f32)
    if s_mul != 1.0:
        s = s * s_mul
    if ck is not None:
        s = s - ck
    if diag_offset is None:
        return s
    key = lax.broadcasted_iota(jnp.int32, (tk, tq), 0)
    query = lax.broadcasted_iota(jnp.int32, (tk, tq), 1) + diag_offset
    return jnp.where(key <= query, s, -jnp.inf)


ATT_ROWS = 64


def _finish_scores(s, s_mul, ck, first_row):
    if s_mul != 1.0:
        s = s * s_mul
    if ck is not None:
        s = s - ck
    if first_row is None:
        return s
    row = lax.broadcasted_iota(jnp.int32, s.shape, 0) + first_row
    col = lax.broadcasted_iota(jnp.int32, s.shape, 1)
    return jnp.where(col <= row, s, -jnp.inf)


def _attn_fwd(name, q, k, v, scale, c_row=None):
    (qa, qo), (ka, ko), (va, vo) = q, k, v
    s_len = qa.shape[0]
    t = _att_tiles(s_len)[1]
    nt = s_len // t
    decay = c_row is not None
    q_mul, s_mul = _fold_scale(scale)

    def kern(*refs):
        q_ref, k_ref, v_ref = refs[:3]
        o_ref, lse_ref, lse_row_ref = refs[-3:]
        i = pl.program_id(1)
        qb = (q_ref[...] * q_mul).astype(bf16)

        def step(j, carry, diagonal):
            m, l, acc = carry
            rows = pl.ds(pl.multiple_of(j * t, t), t)
            kb = k_ref[rows, :].astype(bf16)
            vb = v_ref[rows, :].astype(bf16)
            s = lax.dot_general(qb, kb, (_DOT_DIMS["nt"], ((), ())), preferred_element_type=f32)
            s = _finish_scores(s, s_mul, refs[3][j] if decay else None, 0 if diagonal else None)
            m_new = jnp.maximum(m, jnp.max(s, axis=1, keepdims=True))
            alpha = jnp.exp(m - m_new)
            p = jnp.exp(s - m_new)
            l = alpha * l + jnp.sum(p, axis=1, keepdims=True)
            acc = alpha * acc + jnp.dot(p.astype(bf16), vb, preferred_element_type=f32)
            return m_new, l, acc

        init = (jnp.full((t, 1), -jnp.inf, f32), jnp.zeros((t, 1), f32), jnp.zeros((t, LANE), f32))
        m, l, acc = step(i, lax.fori_loop(0, i, lambda j, c: step(j, c, False), init), True)
        o_ref[...] = acc / l
        lse = m + jnp.log(l)
        lse_ref[...] = lse
        lse_row_ref[...] = _as_row(lse)

    in_specs = [pl.BlockSpec((t, LANE), lambda h, i: (i, qo + h)),
                pl.BlockSpec((s_len, LANE), lambda h, i: (0, ko + h)),
                pl.BlockSpec((s_len, LANE), lambda h, i: (0, vo + h))]
    args = [qa, ka, va]
    if decay:
        in_specs.append(pl.BlockSpec((None, nt, 1, t), lambda h, i: (h, 0, 0, 0)))
        args.append(c_row)
    return pl.pallas_call(
        kern, name=name, grid=(HEADS, nt), in_specs=in_specs,
        out_specs=[pl.BlockSpec((t, LANE), lambda h, i: (i, h)), pl.BlockSpec((None, t, 1), lambda h, i: (h, i, 0)),
                   pl.BlockSpec((None, None, 1, t), lambda h, i: (h, i, 0, 0))],
        out_shape=[jax.ShapeDtypeStruct((s_len, HEADS * LANE), f32), jax.ShapeDtypeStruct((HEADS, s_len, 1), f32),
                   jax.ShapeDtypeStruct((HEADS, nt, 1, t), f32)],
        compiler_params=pltpu.CompilerParams(dimension_semantics=("parallel", "arbitrary")))(*args)


def _attn_dq(name, q, k, v, o, do, lse, scale, c_row=None):
    (qa, qo), (ka, ko), (va, vo) = q, k, v
    s_len = qa.shape[0]
    t = _att_tiles(s_len)[1]
    nt = s_len // t
    decay = c_row is not None
    q_mul, s_mul = _fold_scale(scale)

    rp = min(ATT_ROWS, t)

    def kern(*refs):
        q_ref, k_ref, v_ref, o_ref, do_ref, lse_ref = refs[:6]
        dq_ref, delta_row_ref, drow_ref, delta_ref, s_ref, dp_ref, ds_ref = refs[-7:]
        i = pl.program_id(1)
        qb = (q_ref[...] * q_mul).astype(bf16)
        dob = do_ref[...]
        delta = jnp.sum(dob * o_ref[...], axis=1, keepdims=True)
        delta_ref[...] = delta
        delta_row_ref[...] = _as_row(delta)
        dob = dob.astype(bf16)
        drow_ref[...] = jnp.zeros((t, 1), f32)
        dq_ref[...] = jnp.zeros((t, LANE), f32)

        def step(j, diagonal):
            rows = pl.ds(pl.multiple_of(j * t, t), t)
            kb = k_ref[rows, :].astype(bf16)
            s_ref[...] = lax.dot_general(qb, kb, (_DOT_DIMS["nt"], ((), ())), preferred_element_type=f32)
            dp_ref[...] = lax.dot_general(dob, v_ref[rows, :].astype(bf16), (_DOT_DIMS["nt"], ((), ())),
                                          preferred_element_type=f32)
            ck = refs[6][j] if decay else None

            def rows_of(c, carry):
                r = slice(c * rp, (c + 1) * rp)
                s = _finish_scores(s_ref[r, :], s_mul, ck, c * rp if diagonal else None)
                ds = jnp.exp(s - lse_ref[r, :]) * (dp_ref[r, :] - delta_ref[r, :])
                drow_ref[r, :] += jnp.sum(ds, axis=1, keepdims=True)
                ds_ref[r, :] = ds.astype(bf16)
                return carry

            for c in range(t // rp):
                rows_of(c, 0)
            dq_ref[...] += jnp.dot(ds_ref[...], kb, preferred_element_type=f32)

        def below(j, carry):
            step(j, False)
            return carry

        lax.fori_loop(0, i, below, 0)
        step(i, True)
        dq_ref[...] = dq_ref[...] * scale

    in_specs = [pl.BlockSpec((t, LANE), lambda h, i: (i, qo + h)),
                pl.BlockSpec((s_len, LANE), lambda h, i: (0, ko + h)),
                pl.BlockSpec((s_len, LANE), lambda h, i: (0, vo + h)),
                pl.BlockSpec((t, LANE), lambda h, i: (i, h)),
                pl.BlockSpec((t, LANE), lambda h, i: (i, h)),
                pl.BlockSpec((None, t, 1), lambda h, i: (h, i, 0))]
    args = [qa, ka, va, o, do, lse]
    if decay:
        in_specs.append(pl.BlockSpec((None, nt, 1, t), lambda h, i: (h, 0, 0, 0)))
        args.append(c_row)
    col = pl.BlockSpec((None, t, 1), lambda h, i: (h, i, 0))
    return pl.pallas_call(
        kern, name=name, grid=(HEADS, nt), in_specs=in_specs,
        out_specs=[pl.BlockSpec((t, LANE), lambda h, i: (i, h)), pl.BlockSpec((None, None, 1, t), lambda h, i: (h, i, 0, 0)), col],
        out_shape=[jax.ShapeDtypeStruct((s_len, HEADS * LANE), f32), jax.ShapeDtypeStruct((HEADS, nt, 1, t), f32),
                   jax.ShapeDtypeStruct((HEADS, s_len, 1), f32)],
        scratch_shapes=[pltpu.VMEM((t, 1), f32), pltpu.VMEM((t, t), f32), pltpu.VMEM((t, t), f32), pltpu.VMEM((t, t), bf16)],
        compiler_params=pltpu.CompilerParams(dimension_semantics=("parallel", "arbitrary")))(*args)


def _attn_dkv(name, q, k, v, do, lse, delta, scale, c_col=None):
    (qa, qo), (ka, ko), (va, vo) = q, k, v
    s_len = qa.shape[0]
    tq, tk = _att_tiles(s_len)
    assert lse.shape == (HEADS, s_len // tq, 1, tq), (lse.shape, tq)
    nq, per = s_len // tq, tk // tq
    decay = c_col is not None
    q_mul, s_mul = _fold_scale(scale)

    def kern(*refs):
        q_ref, k_ref, v_ref, do_ref, lse_ref, delta_ref = refs[:6]
        j = pl.program_id(1)
        kb = k_ref[...].astype(bf16)
        vb = v_ref[...].astype(bf16)
        ck = refs[6][...] if decay else None

        def step(i, carry, diagonal):
            dk, dv, dsum = carry
            for d in range(per):
                tile = i * per + d
                rows = pl.ds(pl.multiple_of(tile * tq, tq), tq)
                qb = (q_ref[rows, :] * q_mul).astype(bf16)
                dob = do_ref[rows, :].astype(bf16)
                s = _scores_t(kb, qb.T, s_mul, ck, d * tq if diagonal else None, tq, tk)
                p = jnp.exp(s - lse_ref[tile])
                dv = dv + jnp.dot(p.astype(bf16), dob, preferred_element_type=f32)
                dp = jnp.dot(vb, dob.T, preferred_element_type=f32)
                ds = p * (dp - delta_ref[tile])
                dk = dk + jnp.dot(ds.astype(bf16), qb, preferred_element_type=f32)
                if decay:
                    dsum = dsum + ds
            return dk, dv, dsum

        init = (jnp.zeros((tk, LANE), f32), jnp.zeros((tk, LANE), f32), jnp.zeros((tk, tq), f32))
        dk, dv, dsum = lax.fori_loop(j + 1, s_len // tk, lambda i, c: step(i, c, False), step(j, init, True))
        if decay:
            dk_ref, dv_ref, dc_ref = refs[-3:]
            dc_ref[...] = -jnp.sum(dsum, axis=1, keepdims=True)
        else:
            dk_ref, dv_ref = refs[-2:]
        dk_ref[...] = dk * s_mul
        dv_ref[...] = dv

    stat = pl.BlockSpec((None, nq, 1, tq), lambda h, j: (h, 0, 0, 0))
    in_specs = [pl.BlockSpec((s_len, LANE), lambda h, j: (0, qo + h)),
                pl.BlockSpec((tk, LANE), lambda h, j: (j, ko + h)),
                pl.BlockSpec((tk, LANE), lambda h, j: (j, vo + h)),
                pl.BlockSpec((s_len, LANE), lambda h, j: (0, h)), stat, stat]
    args = [qa, ka, va, do, lse, delta]
    out_specs = [pl.BlockSpec((tk, LANE), lambda h, j: (j, h)), pl.BlockSpec((tk, LANE), lambda h, j: (j, h))]
    out_shape = [jax.ShapeDtypeStruct((s_len, HEADS * LANE), f32), jax.ShapeDtypeStruct((s_len, HEADS * LANE), f32)]
    if decay:
        in_specs.append(pl.BlockSpec((None, tk, 1), lambda h, j: (h, j, 0)))
        args.append(c_col)
        out_specs.append(pl.BlockSpec((None, tk, 1), lambda h, j: (h, j, 0)))
        out_shape.append(jax.ShapeDtypeStruct((HEADS, s_len, 1), f32))
    return pl.pallas_call(
        kern, name=name, grid=(HEADS, s_len // tk), in_specs=in_specs, out_specs=out_specs, out_shape=out_shape,
        compiler_params=pltpu.CompilerParams(dimension_semantics=("parallel", "arbitrary")))(*args)


CONV_TS, CONV_CB = 1024, 256
FFN_ROWS = 64


def _conv_fwd(name, x, w, b, taps):
    xa, width, xidx = _view(x)
    s_len = xa.shape[0]
    ts, cb = min(CONV_TS, s_len), CONV_CB
    xo = xidx * width // cb

    def kern(x_ref, halo_ref, w_ref, b_ref, o_ref):
        i = pl.program_id(1)
        xb = x_ref[...]
        halo = jnp.where(i == 0, 0.0, halo_ref[...])
        xx = jnp.concatenate([halo, xb], axis=0)
        out = b_ref[...] + w_ref[taps - 1:taps, :] * xb
        for k in range(taps - 1):
            out = out + w_ref[k:k + 1, :] * pltpu.roll(xx, taps - 1 - k, 0)[SUBLANE:]
        o_ref[...] = out

    return pl.pallas_call(
        kern, name=name, grid=(width // cb, s_len // ts),
        in_specs=[pl.BlockSpec((ts, cb), lambda j, i: (i, xo + j)),
                  pl.BlockSpec((SUBLANE, cb), lambda j, i: (jnp.maximum(i * (ts // SUBLANE) - 1, 0), xo + j)),
                  pl.BlockSpec((taps, cb), lambda j, i: (0, j)),
                  pl.BlockSpec((1, cb), lambda j, i: (0, j))],
        out_specs=pl.BlockSpec((ts, cb), lambda j, i: (i, j)),
        out_shape=jax.ShapeDtypeStruct((s_len, width), f32),
        compiler_params=pltpu.CompilerParams(dimension_semantics=("parallel", "parallel")))(xa, xa, w, b)


def _conv_bwd(name, x, dout, w, taps, dout2=None, dx_dtype=f32):
    xa, width, xidx = _view(x)
    s_len = xa.shape[0]
    ts, cb = min(CONV_TS, s_len), CONV_CB
    xo = xidx * width // cb
    n_i = s_len // ts
    two = dout2 is not None

    def kern(*refs):
        x_ref, halo_ref, w_ref = refs[:3]
        dx_ref, dw_ref, db_ref = refs[-3:]
        i = pl.program_id(1)
        if two:
            d = refs[3][...] + refs[5][...]
            dn = refs[4][...] + refs[6][...]
        else:
            d, dn = refs[3][...], refs[4][...]
        dn = jnp.where(i == n_i - 1, 0.0, dn)
        xb = x_ref[...]
        halo = jnp.where(i == 0, 0.0, halo_ref[...])
        xx = jnp.concatenate([halo, xb], axis=0)
        dd = jnp.concatenate([d, dn], axis=0)

        @pl.when(i == 0)
        def _():
            dw_ref[...] = jnp.zeros_like(dw_ref)
            db_ref[...] = jnp.zeros_like(db_ref)

        dx = w_ref[taps - 1:taps, :] * d
        dw_ref[taps - 1:taps, :] += jnp.sum(d * xb, axis=0, keepdims=True)
        for k in range(taps - 1):
            sh = taps - 1 - k
            dx = dx + w_ref[k:k + 1, :] * pltpu.roll(dd, ts + SUBLANE - sh, 0)[:ts]
            dw_ref[k:k + 1, :] += jnp.sum(d * pltpu.roll(xx, sh, 0)[SUBLANE:], axis=0, keepdims=True)
        dx_ref[...] = dx.astype(dx_ref.dtype)
        db_ref[...] += jnp.sum(d, axis=0, keepdims=True)

    d_spec = pl.BlockSpec((ts, cb), lambda j, i: (i, j))
    dn_spec = pl.BlockSpec((SUBLANE, cb), lambda j, i: (jnp.minimum((i + 1) * (ts // SUBLANE), s_len // SUBLANE - 1), j))
    in_specs = [pl.BlockSpec((ts, cb), lambda j, i: (i, xo + j)),
                pl.BlockSpec((SUBLANE, cb), lambda j, i: (jnp.maximum(i * (ts // SUBLANE) - 1, 0), xo + j)),
                pl.BlockSpec((taps, cb), lambda j, i: (0, j)), d_spec, dn_spec]
    args = [xa, xa, w, dout, dout]
    if two:
        in_specs += [d_spec, dn_spec]
        args += [dout2, dout2]
    return pl.pallas_call(
        kern, name=name, grid=(width // cb, n_i), in_specs=in_specs,
        out_specs=[pl.BlockSpec((ts, cb), lambda j, i: (i, j)), pl.BlockSpec((taps, cb), lambda j, i: (0, j)),
                   pl.BlockSpec((1, cb), lambda j, i: (0, j))],
        out_shape=[jax.ShapeDtypeStruct((s_len, width), dx_dtype), jax.ShapeDtypeStruct((taps, width), f32),
                   jax.ShapeDtypeStruct((1, width), f32)],
        compiler_params=pltpu.CompilerParams(dimension_semantics=("parallel", "arbitrary")))(*args)


def _conv_rows(xx, w_ref, b_ref, taps):
    out = b_ref[...] + w_ref[taps - 1:taps, :] * xx[SUBLANE:]
    for k in range(taps - 1):
        out = out + w_ref[k:k + 1, :] * pltpu.roll(xx, taps - 1 - k, 0)[SUBLANE:]
    return out


def _ffn_act_fwd(name, up, w, b):
    s_len = up.shape[0]
    ts, cb = min(CONV_TS, s_len), CONV_CB
    nf = D_FF // cb

    def kern(g_ref, gp_ref, v_ref, vp_ref, wg_ref, wv_ref, bg_ref, bv_ref, o_ref):
        first = pl.program_id(1) == 0
        ug = _conv_rows(jnp.concatenate([jnp.where(first, 0.0, gp_ref[...]), g_ref[...]], axis=0), wg_ref, bg_ref, FFN_CONV)
        uv = _conv_rows(jnp.concatenate([jnp.where(first, 0.0, vp_ref[...]), v_ref[...]], axis=0), wv_ref, bv_ref, FFN_CONV)
        o_ref[...] = (jax.nn.silu(ug) * uv).astype(o_ref.dtype)

    def half(off):
        return [pl.BlockSpec((ts, cb), lambda j, i: (i, off + j)),
                pl.BlockSpec((SUBLANE, cb), lambda j, i: (jnp.maximum(i * (ts // SUBLANE) - 1, 0), off + j))]

    def par(rows, off):
        return pl.BlockSpec((rows, cb), lambda j, i: (0, off + j))

    return pl.pallas_call(
        kern, name=name, grid=(nf, s_len // ts),
        in_specs=half(0) + half(nf) + [par(FFN_CONV, 0), par(FFN_CONV, nf), par(1, 0), par(1, nf)],
        out_specs=pl.BlockSpec((ts, cb), lambda j, i: (i, j)),
        out_shape=jax.ShapeDtypeStruct((s_len, D_FF), bf16),
        compiler_params=pltpu.CompilerParams(dimension_semantics=("parallel", "parallel")))(up, up, up, up, w, w, b, b)


def _ffn_act_bwd(name, up, dact, w, b):
    s_len = up.shape[0]
    ts, cb = min(CONV_TS, s_len), CONV_CB
    nf = D_FF // cb
    n_i = s_len // ts
    taps = FFN_CONV

    ch = min(FFN_ROWS, ts)

    def kern(g_ref, gp_ref, gn_ref, v_ref, vp_ref, vn_ref, d_ref, dn_ref, wg_ref, wv_ref, bg_ref, bv_ref,
             dg_ref, dv_ref, dwg_ref, dwv_ref, dbg_ref, dbv_ref, gx_ref, vx_ref, dd_ref):
        i = pl.program_id(1)
        first, last = i == 0, i == n_i - 1
        for x_ref, p_ref, n_ref, ext in ((g_ref, gp_ref, gn_ref, gx_ref), (v_ref, vp_ref, vn_ref, vx_ref)):
            ext[:SUBLANE, :] = jnp.where(first, 0.0, p_ref[...])
            ext[SUBLANE:SUBLANE + ts, :] = x_ref[...]
            ext[SUBLANE + ts:, :] = jnp.where(last, 0.0, n_ref[...])
        dd_ref[:ts, :] = d_ref[...]
        dd_ref[ts:, :] = jnp.where(last, 0.0, dn_ref[...])

        @pl.when(first)
        def _():
            for ref in (dwg_ref, dwv_ref, dbg_ref, dbv_ref):
                ref[...] = jnp.zeros_like(ref)

        def rows_of(c, carry):
            r0 = pl.multiple_of(c * ch, ch)
            gx, vx = gx_ref[pl.ds(r0, ch + 2 * SUBLANE), :], vx_ref[pl.ds(r0, ch + 2 * SUBLANE), :]
            ug, uv = _conv_rows(gx, wg_ref, bg_ref, taps), _conv_rows(vx, wv_ref, bv_ref, taps)
            dd = dd_ref[pl.ds(r0, ch + SUBLANE), :]
            sg = jax.nn.sigmoid(ug)
            out = []
            for du, xx, w_ref, dx_ref, sums in ((dd * uv * (sg * (1.0 + ug * (1.0 - sg))), gx, wg_ref, dg_ref, carry[0]),
                                                (dd * (ug * sg), vx, wv_ref, dv_ref, carry[1])):
                d = du[:ch]
                dx = w_ref[taps - 1:taps, :] * d
                new = [None] * (taps + 1)
                new[taps - 1] = sums[taps - 1] + jnp.sum(d * xx[SUBLANE:SUBLANE + ch], axis=0, keepdims=True)
                for k in range(taps - 1):
                    sh = taps - 1 - k
                    dx = dx + w_ref[k:k + 1, :] * pltpu.roll(du, ch + SUBLANE - sh, 0)[:ch]
                    new[k] = sums[k] + jnp.sum(d * pltpu.roll(xx, sh, 0)[SUBLANE:SUBLANE + ch], axis=0, keepdims=True)
                new[taps] = sums[taps] + jnp.sum(d, axis=0, keepdims=True)
                dx_ref[pl.ds(r0, ch), :] = dx.astype(dx_ref.dtype)
                out.append(tuple(new))
            return tuple(out)

        zero = tuple(jnp.zeros((1, cb), f32) for _ in range(taps + 1))
        sums_g, sums_v = lax.fori_loop(0, ts // ch, rows_of, (zero, zero))
        for sums, dw_ref, db_ref in ((sums_g, dwg_ref, dbg_ref), (sums_v, dwv_ref, dbv_ref)):
            for k in range(taps):
                dw_ref[k:k + 1, :] += sums[k]
            db_ref[...] += sums[taps]

    blocks = s_len // SUBLANE

    def half(off):
        return [pl.BlockSpec((ts, cb), lambda j, i: (i, off + j)),
                pl.BlockSpec((SUBLANE, cb), lambda j, i: (jnp.maximum(i * (ts // SUBLANE) - 1, 0), off + j)),
                pl.BlockSpec((SUBLANE, cb), lambda j, i: (jnp.minimum((i + 1) * (ts // SUBLANE), blocks - 1), off + j))]

    def par(rows, off):
        return pl.BlockSpec((rows, cb), lambda j, i: (0, off + j))

    d_specs = [pl.BlockSpec((ts, cb), lambda j, i: (i, j)),
               pl.BlockSpec((SUBLANE, cb), lambda j, i: (jnp.minimum((i + 1) * (ts // SUBLANE), blocks - 1), j))]
    out_par = [pl.BlockSpec((r, cb), lambda j, i: (0, j)) for r in (taps, taps, 1, 1)]
    return pl.pallas_call(
        kern, name=name, grid=(nf, n_i),
        in_specs=half(0) + half(nf) + d_specs + [par(taps, 0), par(taps, nf), par(1, 0), par(1, nf)],
        out_specs=[pl.BlockSpec((ts, cb), lambda j, i: (i, j))] * 2 + out_par,
        out_shape=[jax.ShapeDtypeStruct((s_len, D_FF), bf16)] * 2 + [jax.ShapeDtypeStruct((taps, D_FF), f32)] * 2
        + [jax.ShapeDtypeStruct((1, D_FF), f32)] * 2,
        scratch_shapes=[pltpu.VMEM((ts + 2 * SUBLANE, cb), f32)] * 2 + [pltpu.VMEM((ts + SUBLANE, cb), f32)],
        compiler_params=pltpu.CompilerParams(dimension_semantics=("parallel", "arbitrary")))(
        up, up, up, up, up, up, dact, dact, w, w, b, b)


SCAN_ROWS = 128


def _block_scan(a, b, reverse):
    t = a.shape[0]
    row = lax.broadcasted_iota(jnp.int32, a.shape, 0)
    d = 1
    while d < t:
        keep = row < t - d if reverse else row >= d
        shift = t - d if reverse else d
        a_far = jnp.where(keep, pltpu.roll(a, shift, 0), 1.0)
        b_far = jnp.where(keep, pltpu.roll(b, shift, 0), 0.0)
        b = a * b_far + b
        a = a * a_far
        d *= 2
    return a, b


def _scan_fwd(name, a, b):
    s_len, width = a.shape
    t = min(SCAN_ROWS, s_len)

    def kern(a_ref, b_ref, h_ref):
        def block(k, carry):
            rows = pl.ds(pl.multiple_of(k * t, t), t)
            acc, h = _block_scan(a_ref[rows, :], b_ref[rows, :], False)
            h_ref[rows, :] = h + acc * carry
            return h_ref[pl.ds(k * t + t - 1, 1), :]

        lax.fori_loop(0, s_len // t, block, jnp.zeros((1, LANE), f32))

    spec = pl.BlockSpec((s_len, LANE), lambda j: (0, j))
    return pl.pallas_call(
        kern, name=name, grid=(width // LANE,), in_specs=[spec, spec], out_specs=spec,
        out_shape=jax.ShapeDtypeStruct((s_len, width), f32),
        compiler_params=pltpu.CompilerParams(dimension_semantics=("parallel",)))(a, b)


def _scan_bwd(name, a_next, h_prev, dh):
    s_len, width = dh.shape
    t = min(SCAN_ROWS, s_len)
    n_blocks = s_len // t

    def kern(an_ref, hp_ref, dh_ref, da_ref, db_ref):
        def block(kk, carry):
            k = n_blocks - 1 - kk
            rows = pl.ds(pl.multiple_of(k * t, t), t)
            acc, g = _block_scan(an_ref[rows, :], dh_ref[rows, :], True)
            g = g + acc * carry
            db_ref[rows, :] = g
            da_ref[rows, :] = g * hp_ref[rows, :]
            return db_ref[pl.ds(k * t, 1), :]

        lax.fori_loop(0, n_blocks, block, jnp.zeros((1, LANE), f32))

    spec = pl.BlockSpec((s_len, LANE), lambda j: (0, j))
    return pl.pallas_call(
        kern, name=name, grid=(width // LANE,), in_specs=[spec, spec, spec], out_specs=[spec, spec],
        out_shape=[jax.ShapeDtypeStruct((s_len, width), f32)] * 2,
        compiler_params=pltpu.CompilerParams(dimension_semantics=("parallel",)))(a_next, h_prev, dh)


def _lane_cumsum(x, reverse):
    n = x.shape[1]
    lane = lax.broadcasted_iota(jnp.int32, x.shape, 1)
    sh = 1
    while sh < n:
        if reverse:
            x = x + jnp.where(lane < n - sh, pltpu.roll(x, n - sh, 1), 0.0)
        else:
            x = x + jnp.where(lane >= sh, pltpu.roll(x, sh, 1), 0.0)
        sh *= 2
    return x


def _decay_fwd(name, fl_t, b8):
    def kern(f_ref, b_ref, c_ref):
        c_ref[...] = _lane_cumsum(jax.nn.log_sigmoid(f_ref[...] + b_ref[...]), False)

    return pl.pallas_call(kern, name=name, out_shape=jax.ShapeDtypeStruct(fl_t.shape, f32))(fl_t, b8)


def _decay_bwd(name, fl_t, b8, dc_key, dc_query):
    def kern(f_ref, b_ref, dck_ref, dcq_ref, df_ref, db_ref):
        dlogf = _lane_cumsum(dck_ref[...] + dcq_ref[...], True)
        df = dlogf * jax.nn.sigmoid(-(f_ref[...] + b_ref[...]))
        df_ref[...] = df
        db_ref[...] = jnp.sum(df, axis=1, keepdims=True)

    return pl.pallas_call(kern, name=name, out_shape=[jax.ShapeDtypeStruct(fl_t.shape, f32),
                                                      jax.ShapeDtypeStruct((SUBLANE, 1), f32)])(fl_t, b8, dc_key, dc_query)


def _rms(x, g, n):
    return x * lax.rsqrt(jnp.sum(x * x, axis=-1, keepdims=True) * (1.0 / n) + EPS) * g


def _loss_head(name, h, target, g, tb=512):
    n, d = h.shape
    tb = min(tb, n)

    def kern(h_ref, t_ref, g_ref, loss_ref, dh_ref, dg_ref):
        i = pl.program_id(0)
        tgt = t_ref[...]

        def f(hv, gv):
            err = _rms(hv, gv, d) - tgt
            return 0.5 * jnp.sum(jnp.sum(err * err, axis=-1, keepdims=True) * (1.0 / d), axis=0, keepdims=True)

        val, vjp = jax.vjp(f, h_ref[...], g_ref[...])
        dh, dg = vjp(jnp.ones((1, 1), f32))
        dh_ref[...] = dh

        @pl.when(i == 0)
        def _():
            loss_ref[...] = jnp.zeros_like(loss_ref)
            dg_ref[...] = jnp.zeros_like(dg_ref)

        loss_ref[...] += val
        dg_ref[...] += dg

    return pl.pallas_call(
        kern, name=name, grid=(n // tb,),
        in_specs=[pl.BlockSpec((tb, d), lambda i: (i, 0)), pl.BlockSpec((tb, d), lambda i: (i, 0)),
                  pl.BlockSpec((1, d), lambda i: (0, 0))],
        out_specs=[pl.BlockSpec((1, 1), lambda i: (0, 0)), pl.BlockSpec((tb, d), lambda i: (i, 0)),
                   pl.BlockSpec((1, d), lambda i: (0, 0))],
        out_shape=[jax.ShapeDtypeStruct((1, 1), f32), jax.ShapeDtypeStruct((n, d), f32), jax.ShapeDtypeStruct((1, d), f32)],
        compiler_params=pltpu.CompilerParams(dimension_semantics=("arbitrary",)))(h, target, g)


def _f_norm(x, g):
    return (_rms(x, g, D_MODEL),)


def _f_latent(qc, kvc, gq, gkv):
    return _rms(qc, gq, MLA_Q_RANK), _rms(kvc, gkv, MLA_KV_RANK)


def _f_rope_table(pos, freq, m1, m2):
    ang = pos * freq
    sin = jnp.sin(ang)
    return jnp.cos(ang), -sin * m1, sin * m2


def _rope(x, cos, s_up, s_down):
    w = x.shape[1]
    return x * cos + _roll(x, w - MLA_ROPE // 2, 1) * s_up + _roll(x, MLA_ROPE // 2, 1) * s_down


def _f_mla_prep(q, kpart, kr, cos, s_up, s_down):
    def heads(t):
        return jnp.concatenate([t] * HEADS, axis=1)

    kr = _rope(kr, cos, s_up, s_down)
    return _rope(q, heads(cos), heads(s_up), heads(s_down)), kpart + heads(kr)


def _f_lru_gate(gates, xc, b_r, b_i, lam):
    r = jax.nn.sigmoid(gates[:, :LRU_WIDTH] + b_r)
    i = jax.nn.sigmoid(gates[:, LRU_WIDTH:] + b_i)
    log_a = -LRU_C * r * jax.nn.softplus(-lam)
    mult = jnp.sqrt(-jnp.tanh(log_a) * (1.0 + jnp.exp(2.0 * log_a)))
    return jnp.exp(log_a), mult * (i * xc)


def _f_merge(o_mla, o_fox, hs, lg, g):
    o_lru = hs * jax.nn.gelu(lg)
    return (jnp.concatenate([_rms(o_mla, g[:, :512], HEADS * MLA_V), _rms(o_fox, g[:, 512:1024], HEADS * FOX_HEAD_DIM),
                             _rms(o_lru, g[:, 1024:], LRU_WIDTH)], axis=1),)


def _f_ffn_gate(u):
    return (jax.nn.silu(u[:, :D_FF]) * u[:, D_FF:],)


def _f_ple(h, gpre, pp):
    return (h + jax.nn.sigmoid(gpre) * pp,)


MIX_PART = ["w_in", "w_uq", "w_ukv", "lru_conv_w"]
FFN_PART = ["w_o", "w_up", "ffn_conv_w", "w_down", "w_ple_gate", "w_ple_proj"]


def _prep_mix_weights(w):
    return dict(w_in=_take_pad(w["w_in"], Z_MAP, 1), w_uq=_take_pad(_take_pad(w["w_uq"], UQ_COL_MAP, 1), UQ_ROW_MAP, 0),
                w_ukv=_take_pad(w["w_ukv"], UKV_MAP, 1), lru_conv_w=w["lru_conv_w"])


def _prep_rep_weights(w):
    eye = jnp.eye(LRU_BLOCKS, dtype=f32)

    def block_diag(m):
        return (eye[:, None, :, None] * m[:, :, None, :]).reshape(LRU_WIDTH, LRU_WIDTH)

    return dict(
        w_ri=jnp.concatenate([block_diag(w["w_r"]), block_diag(w["w_i"])], axis=1).astype(bf16),
        g_mix=w["g_mix"].reshape(1, -1), g_ffn=w["g_ffn"].reshape(1, -1), g_ple=w["g_ple"].reshape(1, -1),
        g_qc=_take_pad(w["g_qc"], UQ_ROW_MAP, 0).reshape(1, -1), g_kvc=w["g_kvc"].reshape(1, -1),
        g_out=_take_pad(w["g_out"], OMIX_MAP, 0).reshape(1, -1),
        b_f8=_take_pad(w["b_f"], _pad_to(np.arange(FOX_HEADS), SUBLANE), 0).reshape(SUBLANE, 1),
        lru_conv_b=w["lru_conv_b"].reshape(1, -1),
        b_r=w["b_r"].reshape(1, -1), b_i=w["b_i"].reshape(1, -1), lam=w["lru_lambda"].reshape(1, -1),
        ffn_conv_b=w["ffn_conv_b"].reshape(1, -1),
    )


def _prep_ffn_weights(w):
    return dict(w_o=_take_pad(w["w_o"], OMIX_MAP, 0),
                w_up=w["w_up"], w_up_g=w["w_up"][:, :D_FF], w_up_v=w["w_up"][:, D_FF:], ffn_conv_w=w["ffn_conv_w"],
                w_down=w["w_down"], w_ple_gate=w["w_ple_gate"], w_ple_proj=w["w_ple_proj"])


def _rope_rows(pos):
    consts = [jnp.asarray(t) for t in _rope_tables(LANE, ROPE_AT)]
    return _rowwise("rope_table", _f_rope_table, [pos], consts, [(LANE, f32)] * 3)


def _key_decay(c_t, s_len):
    t = _att_tiles(s_len)[1]
    return c_t[:HEADS].reshape(HEADS, s_len // t, 1, t), c_t[:HEADS].reshape(HEADS, s_len, 1)


def _layer_fwd(l, h0, p_l, rope, weights_of):
    s_len = h0.shape[0]
    n = f"l{l}_"
    w = _prep_rep_weights(weights_of("rep", h0))
    xn, = _rowwise(n + "norm_mix", _f_norm, [h0], [w["g_mix"]], [(D_MODEL, bf16)])
    w.update(_prep_mix_weights(weights_of("mix", xn)))
    z = _mm(n + "in_proj", xn, w["w_in"])
    zq = (z, QC_W, Z_QC // QC_W)
    zkv = (z, LANE, Z_KVC // LANE)
    zkr = (z, LANE, Z_KR // LANE)
    zlx = (z, LRU_WIDTH, Z_LX // LRU_WIDTH)
    zlg = (z, LRU_WIDTH, Z_LG // LRU_WIDTH)
    qcn, kvn = _rowwise(n + "latent_norm", _f_latent, [zq, zkv], [w["g_qc"], w["g_kvc"]], [(QC_W, bf16), (LANE, bf16)])
    q = _mm(n + "uq", qcn, w["w_uq"])
    kv = _mm(n + "ukv", kvn, w["w_ukv"])
    kpart = (kv, HEADS * LANE, 0)
    qr, kk = _rowwise(n + "mla_prep", _f_mla_prep, [q, kpart, zkr, *rope], [],
                      [(HEADS * LANE, bf16), (HEADS * LANE, bf16)])
    mla_scale = (MLA_NOPE + MLA_ROPE) ** -0.5
    o_mla, lse_m, lse_m_row = _attn_fwd(n + "mla_fwd", (qr, 0), (kk, 0), (kv, HEADS), mla_scale)
    fl_t = z[:, Z_FL:Z_FL + SUBLANE].T
    c_t = _decay_fwd(n + "decay", fl_t, w["b_f8"])
    c_row, c_col = _key_decay(c_t, s_len)
    fox_scale = FOX_HEAD_DIM ** -0.5
    o_fox, lse_f, lse_f_row = _attn_fwd(n + "fox_fwd", (z, Z_FQ // LANE), (z, Z_FK // LANE), (z, Z_FV // LANE), fox_scale, c_row)
    xc = _conv_fwd(n + "lru_conv", zlx, w["lru_conv_w"], w["lru_conv_b"], LRU_CONV)
    gates = _mm(n + "lru_gates", xc, w["w_ri"])
    a, bx = _rowwise(n + "lru_gate", _f_lru_gate, [gates, xc], [w["b_r"], w["b_i"], w["lam"]],
                     [(LRU_WIDTH, f32), (LRU_WIDTH, f32)])
    hs = _scan_fwd(n + "lru_scan", a, bx)
    ocat, = _rowwise(n + "merge", _f_merge, [o_mla, o_fox, hs, zlg], [w["g_out"]], [(OMIX_W, bf16)])
    w.update(_prep_ffn_weights(weights_of("ffn", ocat)))
    h1 = _mm(n + "out_proj", ocat, w["w_o"], res=h0)
    xn2, = _rowwise(n + "norm_ffn", _f_norm, [h1], [w["g_ffn"]], [(D_MODEL, bf16)])
    up = _mm(n + "up_proj", xn2, w["w_up"])
    act = _ffn_act_fwd(n + "ffn_act", up, w["ffn_conv_w"], w["ffn_conv_b"])
    h2 = _mm(n + "down_proj", act, w["w_down"], res=h1)
    hn, = _rowwise(n + "norm_ple", _f_norm, [h2], [w["g_ple"]], [(D_MODEL, bf16)])
    gpre = _mm(n + "ple_gate", hn, w["w_ple_gate"])
    pp = _mm(n + "ple_proj", p_l, w["w_ple_proj"])
    h3, = _rowwise(n + "ple_mix", _f_ple, [h2, gpre, pp], [], [(D_MODEL, f32)])
    res = dict(h0=h0, xn=xn, z=z, qcn=qcn, kvn=kvn, q=q, kv=kv, qr=qr, kk=kk, o_mla=o_mla, lse_m=lse_m, fl_t=fl_t,
               lse_m_row=lse_m_row, lse_f_row=lse_f_row, c_row=c_row, c_col=c_col, o_fox=o_fox, lse_f=lse_f, xc=xc, gates=gates, a=a, hs=hs, ocat=ocat, h1=h1,
               xn2=xn2, up=up, act=act, h2=h2, hn=hn, gpre=gpre, pp=pp, p_l=p_l)
    return h3, res, w


def _layer_bwd(l, dh3, r, rope, w, token, grads_to):
    s_len = dh3.shape[0]
    n = f"l{l}_"
    g = {}
    w = dict(w, g_ple=w["g_ple"] + token)
    z = r["z"]
    zq = (z, QC_W, Z_QC // QC_W)
    zkv = (z, LANE, Z_KVC // LANE)
    zkr = (z, LANE, Z_KR // LANE)
    zlx = (z, LRU_WIDTH, Z_LX // LRU_WIDTH)
    zlg = (z, LRU_WIDTH, Z_LG // LRU_WIDTH)
    (dh2a, dgpre, dpp), _ = _rowwise_bwd(n + "ple_mix_b", _f_ple, [r["h2"], r["gpre"], r["pp"]], [], [dh3], 3,
                                         dts=[f32, bf16, bf16])
    g["w_ple_proj"] = _mm(n + "ple_proj_dw", r["p_l"], dpp, "tn", bf16)
    dhn = _mm(n + "ple_gate_dx", dgpre, w["w_ple_gate"], "nt")
    g["w_ple_gate"] = _mm(n + "ple_gate_dw", r["hn"], dgpre, "tn", bf16)
    (dh2,), (g["g_ple"],) = _rowwise_bwd(n + "norm_ple_b", _f_norm, [r["h2"]], [w["g_ple"]], [dhn], 1, adds={0: dh2a})
    dact = _mm(n + "down_dx", dh2, w["w_down"], "nt")
    g["w_down"] = _mm(n + "down_dw", r["act"], dh2, "tn", bf16)
    dup_g, dup_v, dcw_g, dcw_v, dcb_g, dcb_v = _ffn_act_bwd(n + "ffn_act_b", r["up"], dact, w["ffn_conv_w"], w["ffn_conv_b"])
    g["ffn_conv_w"] = jnp.concatenate([dcw_g, dcw_v], axis=1)
    g["ffn_conv_b"] = jnp.concatenate([dcb_g, dcb_v], axis=1)
    dxn2 = _mm(n + "up_dx_v", dup_v, w["w_up_v"], "nt", res=_mm(n + "up_dx_g", dup_g, w["w_up_g"], "nt"))
    g["w_up"] = jnp.concatenate([_mm(n + "up_dw_g", r["xn2"], dup_g, "tn", bf16),
                                 _mm(n + "up_dw_v", r["xn2"], dup_v, "tn", bf16)], axis=1)
    (dh1,), (g["g_ffn"],) = _rowwise_bwd(n + "norm_ffn_b", _f_norm, [r["h1"]], [w["g_ffn"]], [dxn2], 1, adds={0: dh2})
    docat = _mm(n + "out_dx", dh1, w["w_o"], "nt")
    g["w_o"] = _mm(n + "out_dw", r["ocat"], dh1, "tn", bf16)
    token = grads_to("ffn", dict(w_o=_take_inv(g["w_o"], OMIX_MAP, 0), w_up=g["w_up"], ffn_conv_w=g["ffn_conv_w"],
                                 w_down=g["w_down"], w_ple_gate=g["w_ple_gate"], w_ple_proj=g["w_ple_proj"]))
    w = dict(w, g_out=w["g_out"] + token)
    (do_mla, do_fox, dhs, dlg), (g["g_out"],) = _rowwise_bwd(
        n + "merge_b", _f_merge, [r["o_mla"], r["o_fox"], r["hs"], zlg], [w["g_out"]], [docat], 4)
    a, hs = r["a"], r["hs"]
    a_next = jnp.concatenate([a[1:], jnp.zeros((1, LRU_WIDTH), f32)], axis=0)
    h_prev = jnp.concatenate([jnp.zeros((1, LRU_WIDTH), f32), hs[:-1]], axis=0)
    da, dbx = _scan_bwd(n + "lru_scan_b", a_next, h_prev, dhs)
    (dgates, dxc_a), (g["b_r"], g["b_i"], g["lam"]) = _rowwise_bwd(
        n + "lru_gate_b", _f_lru_gate, [r["gates"], r["xc"]], [w["b_r"], w["b_i"], w["lam"]], [da, dbx], 2,
        dts=[bf16, f32])
    dxc_b = _mm(n + "lru_gates_dx", dgates, w["w_ri"], "nt")
    g["w_ri"] = _mm(n + "lru_gates_dw", r["xc"], dgates, "tn")
    dlx, g["lru_conv_w"], g["lru_conv_b"] = _conv_bwd(n + "lru_conv_b", zlx, dxc_a, w["lru_conv_w"], LRU_CONV, dout2=dxc_b)
    fox_scale = FOX_HEAD_DIM ** -0.5
    fq, fk, fv = (z, Z_FQ // LANE), (z, Z_FK // LANE), (z, Z_FV // LANE)
    dfq, delta_f, dc_q = _attn_dq(n + "fox_dq", fq, fk, fv, r["o_fox"], do_fox, r["lse_f"], fox_scale, r["c_row"])
    dfk, dfv, dc_k = _attn_dkv(n + "fox_dkv", fq, fk, fv, do_fox, r["lse_f_row"], delta_f, fox_scale,
                               r["c_col"])
    pad_rows = jnp.zeros((SUBLANE - HEADS, s_len), f32)
    dfl_t, g["b_f8"] = _decay_bwd(n + "decay_b", r["fl_t"], w["b_f8"],
                                  jnp.concatenate([dc_k.reshape(HEADS, s_len), pad_rows], axis=0),
                                  jnp.concatenate([dc_q.reshape(HEADS, s_len), pad_rows], axis=0))
    dfl = jnp.pad(dfl_t.T, ((0, 0), (0, LANE - SUBLANE)))
    mla_scale = (MLA_NOPE + MLA_ROPE) ** -0.5
    qr, kk, kv = (r["qr"], 0), (r["kk"], 0), (r["kv"], HEADS)
    dqr, delta_m, _ = _attn_dq(n + "mla_dq", qr, kk, kv, r["o_mla"], do_mla, r["lse_m"], mla_scale)
    dkk, dv_m = _attn_dkv(n + "mla_dkv", qr, kk, kv, do_mla, r["lse_m_row"], delta_m, mla_scale)
    (dq, dkpart, dkr), _ = _rowwise_bwd(n + "mla_prep_b", _f_mla_prep, [r["q"], (r["kv"], HEADS * LANE, 0), zkr, *rope],
                                        [], [dqr, dkk], 3, dts=[bf16, bf16, f32])
    dkv = jnp.concatenate([dkpart, dv_m.astype(bf16)], axis=1)
    dkvn = _mm(n + "ukv_dx", dkv, w["w_ukv"], "nt")
    g["w_ukv"] = _mm(n + "ukv_dw", r["kvn"], dkv, "tn", bf16)
    dqcn = _mm(n + "uq_dx", dq, w["w_uq"], "nt")
    g["w_uq"] = _mm(n + "uq_dw", r["qcn"], dq, "tn", bf16)
    (dqc, dkvc), (g["g_qc"], g["g_kvc"]) = _rowwise_bwd(n + "latent_norm_b", _f_latent, [zq, zkv],
                                                        [w["g_qc"], w["g_kvc"]], [dqcn, dkvn], 2)
    dz = jnp.concatenate([t.astype(bf16) for t in (dfq, dfk, dfv, dlx, dlg, dqc, dkvc, dkr, dfl)], axis=1)
    dxn = _mm(n + "in_dx", dz, w["w_in"], "nt")
    g["w_in"] = _mm(n + "in_dw", r["xn"], dz, "tn", bf16)
    (dh0,), (g["g_mix"],) = _rowwise_bwd(n + "norm_mix_b", _f_norm, [r["h0"]], [w["g_mix"]], [dxn], 1, adds={0: dh1})
    return dh0, grads_to("mix", _unpad_mix_grads(g))


def _unpad_mix_grads(g):
    d_ri = g["w_ri"]
    idx = jnp.arange(LRU_BLOCKS)

    def diag_blocks(m):
        return m.reshape(LRU_BLOCKS, LRU_BLOCK, LRU_BLOCKS, LRU_BLOCK)[idx, :, idx, :]

    return dict(
        g_mix=g["g_mix"][0], w_in=_take_inv(g["w_in"], Z_MAP, 1), g_qc=g["g_qc"][0, :MLA_Q_RANK],
        w_uq=_take_inv(g["w_uq"][:MLA_Q_RANK], UQ_COL_MAP, 1), g_kvc=g["g_kvc"][0],
        w_ukv=_take_inv(g["w_ukv"], UKV_MAP, 1), b_f=g["b_f8"][:FOX_HEADS, 0],
        lru_conv_w=g["lru_conv_w"], lru_conv_b=g["lru_conv_b"][0],
        w_r=diag_blocks(d_ri[:, :LRU_WIDTH]), b_r=g["b_r"][0], w_i=diag_blocks(d_ri[:, LRU_WIDTH:]), b_i=g["b_i"][0],
        lru_lambda=g["lam"][0], g_out=_take_inv(g["g_out"][0], OMIX_MAP, 0),
        g_ffn=g["g_ffn"][0], ffn_conv_b=g["ffn_conv_b"][0], g_ple=g["g_ple"][0],
    )


LAYER_WEIGHTS = ["g_mix", "w_in", "g_qc", "w_uq", "g_kvc", "w_ukv", "b_f", "lru_conv_w", "lru_conv_b", "w_r", "b_r", "w_i",
                 "b_i", "lru_lambda", "g_out", "w_o", "g_ffn", "w_up", "ffn_conv_w", "ffn_conv_b", "w_down", "g_ple",
                 "w_ple_gate", "w_ple_proj"]
WEIGHTS = LAYER_WEIGHTS + ["g_final"]


def _local_step(x, p, pos, target, g_final, weights_of, grads_to):
    h = x
    rope = _rope_rows(pos)
    ws, saved = [], []
    for l in range(DEPTH):
        h, r, w = _layer_fwd(l, h, p[l], rope, functools.partial(weights_of, l))
        ws.append(w)
        saved.append(r)
    loss, dh, dg_final = _loss_head("loss_head", h, target, g_final.reshape(1, -1))
    token = jnp.zeros((), f32)
    for l in reversed(range(DEPTH)):
        dh, token = _layer_bwd(l, dh, saved[l], rope, ws[l], token, functools.partial(grads_to, l))
    return loss[0, 0], dh, dg_final[0]


MESH_AXES = ("x", "y", "c")


def _row_tile(rows, cap):
    if rows <= cap:
        return rows
    for t in range(cap, SUBLANE - 1, -SUBLANE):
        if rows % t == 0:
            return t
    return rows


ADAM_BLOCK_BYTES = 2 ** 21


def _adamw(name, w, g, m, v):
    rows, cols = w.shape
    tr = _row_tile(rows, max(SUBLANE, ADAM_BLOCK_BYTES // (4 * cols) // SUBLANE * SUBLANE))

    def kern(w_ref, g_ref, m_ref, v_ref, d_ref, nm_ref, nv_ref):
        gv = g_ref[...]
        nm = ADAM_B1 * m_ref[...] + (1.0 - ADAM_B1) * gv
        nv = ADAM_B2 * v_ref[...] + (1.0 - ADAM_B2) * (gv * gv)
        m_hat = nm / (1.0 - ADAM_B1 ** ADAM_STEP)
        v_hat = nv / (1.0 - ADAM_B2 ** ADAM_STEP)
        d_ref[...] = -ADAM_LR * (m_hat / (jnp.sqrt(v_hat) + ADAM_EPS) + ADAM_WD * w_ref[...])
        nm_ref[...] = nm
        nv_ref[...] = nv

    spec = pl.BlockSpec((tr, cols), lambda i: (i, 0))
    return pl.pallas_call(
        kern, name=name, grid=(rows // tr,), in_specs=[spec] * 4, out_specs=[spec] * 3,
        out_shape=[jax.ShapeDtypeStruct((rows, cols), f32)] * 3,
        compiler_params=pltpu.CompilerParams(dimension_semantics=("parallel",)))(w, g, m, v)


def _packed_rows(shape):
    return -(-int(np.prod(shape)) // (SUBLANE * LANE)) * SUBLANE


def _pack(arrays):
    rows = []
    for a in arrays:
        flat = a.reshape(-1)
        rows.append(jnp.pad(flat, (0, _packed_rows(a.shape) * LANE - flat.shape[0])).reshape(-1, LANE))
    return jnp.concatenate(rows, axis=0)


def _unpack(buf, shapes):
    out, at = [], 0
    for s in shapes:
        rows = _packed_rows(s)
        out.append(buf[at:at + rows].reshape(-1)[:int(np.prod(s))].reshape(s))
        at += rows
    return out


SHARD_AXIS = {"w_in": 2, "w_uq": 2, "w_ukv": 2, "lru_conv_w": 2, "w_o": 1, "w_up": 2, "ffn_conv_w": 2, "w_down": 1,
              "w_ple_gate": 1, "w_ple_proj": 2}
SHARDED = [k for k in WEIGHTS if k in SHARD_AXIS]
REPLICATED = [k for k in WEIGHTS if k not in SHARD_AXIS]
ELEMENTWISE_F32 = ("lru_conv_w", "ffn_conv_w")
N_SHARDS = 4
BF16_TILE_ROWS = 16


HBM_SPEC = pl.BlockSpec(memory_space=pl.ANY)
SEM_SPEC = pl.BlockSpec(memory_space=pltpu.SEMAPHORE)
SPLIT_EFFECT = pltpu.SideEffectType.DATAFLOW_SIDE_EFFECTING
CHIP_FLIPS = ((1, 0), (0, 1), (1, 1))
N_DEVICES = 8
SUM_BLOCK_BYTES = 4 * 2 ** 20


def _device_index():
    return 4 * lax.axis_index("x") + 2 * lax.axis_index("y") + lax.axis_index("c")


def _when(cond, fn):
    if cond is None:
        fn()
    else:
        pl.when(cond)(fn)


class _Exchange:
    def __init__(self, name, plan, srcs, land_shapes, n_send, n_recv):
        self.name, self.plan, self.srcs, self.n = name, plan, list(srcs), len(srcs)
        self.land_shapes, self.n_send, self.n_recv = land_shapes, n_send, n_recv

    def start(self, after=None):
        n = self.n
        lands = [lax.empty(s.shape, s.dtype) for s in self.land_shapes]
        extra = [] if after is None else [after]

        def body(*refs):
            ins, lands_in = refs[:n], refs[n:2 * n]
            send_sems, recv_sems, token = refs[2 * n + len(extra)], refs[2 * n + len(extra) + 1], refs[-1]
            sends, _ = self.plan(ins, lands_in, send_sems, recv_sems)
            for cond, cp in sends:
                _when(cond, cp.start)
            token[...] = jnp.zeros_like(token)

        hbm = [pltpu.with_memory_space_constraint(a, pltpu.HBM) for a in self.srcs + lands]
        res = pl.pallas_call(
            body, name=self.name + "_start",
            out_shape=(pltpu.SemaphoreType.DMA((self.n_send,)), pltpu.SemaphoreType.DMA((self.n_recv,)),
                       *[pltpu.HBM(a.shape, a.dtype) for a in hbm], jax.ShapeDtypeStruct((SUBLANE, LANE), f32)),
            in_specs=[HBM_SPEC] * (2 * n + len(extra)),
            out_specs=(SEM_SPEC, SEM_SPEC, *[HBM_SPEC] * (2 * n), pl.BlockSpec(memory_space=pltpu.VMEM)),
            input_output_aliases={i: 2 + i for i in range(2 * n)},
            compiler_params=pltpu.CompilerParams(has_side_effects=SPLIT_EFFECT))(*hbm, *extra)
        self.sems, self.thru, token = res[:2], res[2:2 + 2 * n], res[-1]
        return token[0, 0]

    def finish(self, after):
        n = self.n

        def body(*refs):
            ins, lands_in, send_sems, recv_sems = refs[:n], refs[n:2 * n], refs[2 * n], refs[2 * n + 1]
            sends, arrivals = self.plan(ins, lands_in, send_sems, recv_sems)
            for cond, cp in arrivals:
                _when(cond, cp.wait_recv)
            for cond, cp in sends:
                _when(cond, cp.wait_send)

        res = pl.pallas_call(
            body, name=self.name + "_finish", out_shape=tuple(pltpu.HBM(a.shape, a.dtype) for a in self.thru),
            in_specs=[HBM_SPEC] * (2 * n) + [SEM_SPEC, SEM_SPEC, HBM_SPEC], out_specs=tuple([HBM_SPEC] * (2 * n)),
            input_output_aliases={i: i for i in range(2 * n)},
            compiler_params=pltpu.CompilerParams(has_side_effects=SPLIT_EFFECT))(*self.thru, *self.sems, after)
        return list(res[n:])


def _gather_exchange(name, shards):
    def plan(ins, lands, send_sems, recv_sems):
        x, y, c = (lax.axis_index(a) for a in MESH_AXES)
        copies = []
        for i in range(len(ins)):
            for k, (fx, fy) in enumerate(CHIP_FLIPS):
                peer = (1 - x if fx else x, 1 - y if fy else y, c)
                copies.append((None, pltpu.make_async_remote_copy(
                    src_ref=ins[i], dst_ref=lands[i].at[2 * x + y], send_sem=send_sems.at[3 * i + k],
                    recv_sem=recv_sems.at[3 * i + k], device_id=peer, device_id_type=pl.DeviceIdType.MESH)))
        return copies, copies

    n = len(shards)
    return _Exchange(name, plan, shards, [jax.ShapeDtypeStruct((N_SHARDS,) + s.shape, s.dtype) for s in shards], 3 * n, 3 * n)


def _scatter_exchange(name, layer, chunks):
    def plan(ins, lands, send_sems, recv_sems):
        x, y, c = (lax.axis_index(a) for a in MESH_AXES)
        me = _device_index()
        sends, arrivals = [], []
        for i in range(len(ins)):
            for j in range(N_SHARDS):
                target = (j // 2, j % 2, layer)
                remote = jnp.logical_not((x == target[0]) & (y == target[1]) & (c == layer))
                sends.append((remote, pltpu.make_async_remote_copy(
                    src_ref=ins[i].at[j], dst_ref=lands[i].at[me], send_sem=send_sems.at[N_SHARDS * i + j],
                    recv_sem=recv_sems.at[N_DEVICES * i + me], device_id=target, device_id_type=pl.DeviceIdType.MESH)))
            for s in range(N_DEVICES):
                arrivals.append(((c == layer) & (me != s), pltpu.make_async_remote_copy(
                    src_ref=ins[i].at[0], dst_ref=lands[i].at[s], send_sem=send_sems.at[0],
                    recv_sem=recv_sems.at[N_DEVICES * i + s], device_id=(x, y, c), device_id_type=pl.DeviceIdType.MESH)))
        return sends, arrivals

    n = len(chunks)
    lands = [jax.ShapeDtypeStruct((N_DEVICES,) + ch.shape[1:], ch.dtype) for ch in chunks]
    return _Exchange(name, plan, chunks, lands, N_SHARDS * n, N_DEVICES * n)


def _sum_contributions(name, got, mine):
    _, a, b = got.shape
    ta = _row_tile(a, max(SUBLANE, SUM_BLOCK_BYTES // (N_DEVICES * b * got.dtype.itemsize) // SUBLANE * SUBLANE))

    def kern(got_ref, mine_ref, o_ref):
        me = _device_index()
        acc = jnp.zeros(o_ref.shape, f32)
        for s in range(N_DEVICES):
            acc = acc + jnp.where(me == s, mine_ref[...].astype(f32), got_ref[s].astype(f32))
        o_ref[...] = acc

    return pl.pallas_call(
        kern, name=name, grid=(a // ta,),
        in_specs=[pl.BlockSpec((N_DEVICES, ta, b), lambda i: (0, i, 0)), pl.BlockSpec((ta, b), lambda i: (i, 0))],
        out_specs=pl.BlockSpec((ta, b), lambda i: (i, 0)), out_shape=jax.ShapeDtypeStruct((a, b), f32),
        compiler_params=pltpu.CompilerParams(dimension_semantics=("parallel",)))(got, mine)


def _swap_layers(name, sums):
    n = len(sums[0])

    def body(*refs):
        srcs = (refs[:n], refs[n:2 * n])
        outs, (send_sems, recv_sems) = refs[2 * n:3 * n], refs[3 * n:]
        x, y, c = (lax.axis_index(a) for a in MESH_AXES)
        for i in range(n):
            for layer in range(DEPTH):
                cp = pltpu.make_async_remote_copy(
                    src_ref=srcs[layer][i], dst_ref=outs[i], send_sem=send_sems.at[i], recv_sem=recv_sems.at[i],
                    device_id=(x, y, 1 - c), device_id_type=pl.DeviceIdType.MESH)
                pl.when(c == layer)(cp.start)
        for i in range(n):
            pltpu.make_async_remote_copy(
                src_ref=srcs[0][i], dst_ref=outs[i], send_sem=send_sems.at[i], recv_sem=recv_sems.at[i],
                device_id=(x, y, 1 - c), device_id_type=pl.DeviceIdType.MESH).wait()

    return pl.pallas_call(
        body, name=name, out_shape=[jax.ShapeDtypeStruct(s.shape, s.dtype) for s in sums[0]],
        in_specs=[HBM_SPEC] * (2 * n), out_specs=[HBM_SPEC] * n,
        scratch_shapes=[pltpu.SemaphoreType.DMA((n,)), pltpu.SemaphoreType.DMA((n,))])(*sums[0], *sums[1])


def _stack_shards(g, axis):
    if axis == 1:
        return g.reshape(N_SHARDS, g.shape[0] // N_SHARDS, g.shape[1])
    return g.reshape(g.shape[0], N_SHARDS, g.shape[1] // N_SHARDS).transpose(1, 0, 2)


def _join_shards(s, axis):
    if axis == 1:
        return s.reshape(-1, s.shape[2])
    return s.transpose(1, 0, 2).reshape(s.shape[1], -1)


def _layer_shards(w, l, names):
    return [w[k][l] if k in ELEMENTWISE_F32 else w[k][l].astype(bf16) for k in names]


def _full_weights(names, sent, got):
    j = 2 * lax.axis_index("x") + lax.axis_index("y")
    return {k: _join_shards(lax.dynamic_update_slice(g, own[None], (j, 0, 0)), SHARD_AXIS[k])
            for k, own, g in zip(names, sent, got)}


def _grad_chunks(grads, names):
    return [_stack_shards(grads[k], SHARD_AXIS[k]).astype(bf16) for k in names]


def _sum_group(l, names, got, chunks):
    j = 2 * lax.axis_index("x") + lax.axis_index("y")
    return {k: _sum_contributions(f"sum_l{l}_{k}", g, lax.dynamic_index_in_dim(ch, j, 0, keepdims=False))
            for k, g, ch in zip(names, got, chunks)}


def _both_layers(name, names, sums):
    c = lax.axis_index("c")
    mine = [[sums[l][k] for k in names] for l in range(DEPTH)]
    other = _swap_layers(name, mine)
    return {k: jnp.stack([jnp.where(c == 0, mine[0][i], other[i]), jnp.where(c == 0, other[i], mine[1][i])])
            for i, k in enumerate(names)}


def _gather_all_exchange(name, src):
    def plan(ins, lands, send_sems, recv_sems):
        coords = [lax.axis_index(a) for a in MESH_AXES]
        me = _device_index()
        sends, arrivals = [], []
        for f in range(1, N_DEVICES):
            peer = tuple(1 - cd if (f >> (2 - b)) & 1 else cd for b, cd in enumerate(coords))
            sends.append((None, pltpu.make_async_remote_copy(
                src_ref=ins[0], dst_ref=lands[0].at[me], send_sem=send_sems.at[f - 1], recv_sem=recv_sems.at[me],
                device_id=peer, device_id_type=pl.DeviceIdType.MESH)))
        for s in range(N_DEVICES):
            arrivals.append((me != s, pltpu.make_async_remote_copy(
                src_ref=ins[0], dst_ref=lands[0].at[s], send_sem=send_sems.at[0], recv_sem=recv_sems.at[s],
                device_id=tuple(coords), device_id_type=pl.DeviceIdType.MESH)))
        return sends, arrivals

    return _Exchange(name, plan, [src], [jax.ShapeDtypeStruct((N_DEVICES,) + src.shape, src.dtype)], N_DEVICES - 1, N_DEVICES)


def kernel(x, p, positions, g_mix, w_in, g_qc, w_uq, g_kvc, w_ukv, b_f, lru_conv_w, lru_conv_b, w_r, b_r, w_i, b_i, lru_lambda, g_out, w_o, g_ffn, w_up, ffn_conv_w, ffn_conv_b, w_down, g_ple, w_ple_gate, w_ple_proj, g_final, loss_target, m_g_mix, m_w_in, m_g_qc, m_w_uq, m_g_kvc, m_w_ukv, m_b_f, m_lru_conv_w, m_lru_conv_b, m_w_r, m_b_r, m_w_i, m_b_i, m_lru_lambda, m_g_out, m_w_o, m_g_ffn, m_w_up, m_ffn_conv_w, m_ffn_conv_b, m_w_down, m_g_ple, m_w_ple_gate, m_w_ple_proj, m_g_final, v_g_mix, v_w_in, v_g_qc, v_w_uq, v_g_kvc, v_w_ukv, v_b_f, v_lru_conv_w, v_lru_conv_b, v_w_r, v_b_r, v_w_i, v_b_i, v_lru_lambda, v_g_out, v_w_o, v_g_ffn, v_w_up, v_ffn_conv_w, v_ffn_conv_b, v_w_down, v_g_ple, v_w_ple_gate, v_w_ple_proj, v_g_final):
    given = locals()
    w = {k: given[k] for k in WEIGHTS}
    m = {k: given["m_" + k] for k in WEIGHTS}
    v = {k: given["v_" + k] for k in WEIGHTS}

    parts = {"mix": MIX_PART, "ffn": FFN_PART}
    groups = [(l, part) for l in range(DEPTH) for part in ("mix", "ffn")]
    sent = {g: _layer_shards(w, g[0], parts[g[1]]) for g in groups}
    ahead = {g: _gather_exchange(f"gather_l{g[0]}_{g[1]}", sent[g]) for g in groups}
    pos = positions[0].astype(f32).reshape(-1, 1) + ahead[groups[0]].start()
    behind, layer_grads, chunks = {}, [{} for _ in range(DEPTH)], {}

    def weights_of(l, part, after):
        if part == "rep":
            return {k: w[k][l] for k in LAYER_WEIGHTS if k in REPLICATED}
        g = (l, part)
        got = ahead[g].finish(after=after)
        full = _full_weights(parts[part], sent[g], got)
        if g == groups[0]:
            full["lru_conv_w"] = full["lru_conv_w"] + ahead[groups[1]].start(after=got[0])
        if g == groups[1]:
            for later in groups[2:]:
                full["ffn_conv_w"] = full["ffn_conv_w"] + ahead[later].start(after=got[0])
        return full

    def grads_to(l, part, grads):
        g = (l, part)
        layer_grads[l].update(grads)
        chunks[g] = _grad_chunks(grads, parts[part])
        if g == groups[0]:
            return jnp.zeros((), f32)
        behind[g] = _scatter_exchange(f"scatter_l{l}_{part}", l, chunks[g])
        return behind[g].start()

    loss, dx, dg_final = _local_step(x[0], p[:, 0], pos, loss_target[0], w["g_final"], weights_of, grads_to)

    grads = {k: jnp.stack([layer_grads[l][k] for l in range(DEPTH)]) for k in LAYER_WEIGHTS if k in REPLICATED}
    grads["g_final"] = dg_final
    rep_shapes = [w[k].shape for k in REPLICATED] + [(1,)]
    contrib = _pack([grads[k] for k in REPLICATED] + [loss.reshape(1)])
    last = _scatter_exchange("scatter_l0_mix", 0, chunks[groups[0]])
    everyone = _gather_all_exchange("gather_replicated", contrib)
    started = (last.start() + everyone.start() + dx[0, 0]).reshape(1, 1)

    def adamw_of(names, g_sharded):
        out = {}
        for k in names:
            shape = w[k].shape
            flat = [t.reshape(-1, shape[-1]) for t in (w[k], g_sharded[k], m[k], v[k])]
            out[k] = [t.reshape(shape) for t in (flat[1],) + tuple(_adamw("adamw_" + k, *flat))]
        return out

    sums = [{} for _ in range(DEPTH)]
    for g in groups[1:]:
        sums[g[0]].update(_sum_group(g[0], parts[g[1]], behind[g].finish(after=started), chunks[g]))
    big = adamw_of(FFN_PART, _both_layers("swap_ffn", FFN_PART, sums))
    sums[0].update(_sum_group(0, MIX_PART, last.finish(after=big[FFN_PART[0]][1]), chunks[groups[0]]))
    big.update(adamw_of(MIX_PART, _both_layers("swap_mix", MIX_PART, sums)))

    g_rep = _sum_contributions("sum_replicated", everyone.finish(after=big[MIX_PART[0]][1])[0], contrib)
    zero = jnp.zeros((1,), f32)
    w_rep, m_rep, v_rep = (_pack([t[k] for k in REPLICATED] + [zero]) for t in (w, m, v))
    rep = [_unpack(b, rep_shapes) for b in (g_rep,) + tuple(_adamw("adamw_replicated", w_rep, g_rep, m_rep, v_rep))]

    outs = []
    for kind in range(4):
        by_name = {k: big[k][kind] for k in SHARDED}
        by_name.update(zip(REPLICATED, rep[kind][:-1]))
        outs.append([by_name[k] for k in WEIGHTS])
    total_loss = rep[0][-1][0]
    return (total_loss, dx.reshape(x.shape), *outs[0], *outs[1], *outs[2], *outs[3])
```

```python
import functools
import math

import numpy as np
import jax
import jax.numpy as jnp
from jax import lax
from jax.experimental import pallas as pl
from jax.experimental.pallas import tpu as pltpu

f32, bf16 = jnp.float32, jnp.bfloat16

D_MODEL = 1024
PLE_DIM = 256
MLA_HEADS, MLA_NOPE, MLA_ROPE, MLA_V = 4, 64, 32, 64
MLA_Q_RANK, MLA_KV_RANK = 192, 128
FOX_HEADS, FOX_HEAD_DIM = 4, 64
LRU_WIDTH, LRU_BLOCKS, LRU_BLOCK, LRU_CONV, LRU_C = 512, 8, 64, 4, 8.0
D_FF, FFN_CONV = 2816, 3
ROPE_THETA = 10000.0
EPS = 1e-6
DEPTH = 2
ADAM_LR, ADAM_B1, ADAM_B2, ADAM_EPS, ADAM_WD, ADAM_STEP = 0.001, 0.9, 0.999, 1e-08, 0.01, 10

LANE = 128
SUBLANE = 8
HEADS = 4

Z_FQ, Z_FK, Z_FV, Z_LX, Z_LG, Z_QC, Z_KVC, Z_KR, Z_FL, Z_W = 0, 512, 1024, 1536, 2048, 2560, 2816, 2944, 3072, 3200
QC_W = 256
ROPE_AT = 64


def _head_pad_map(n_heads, width):
    m = -np.ones(n_heads * LANE, np.int64)
    for h in range(n_heads):
        m[h * LANE:h * LANE + width] = h * width + np.arange(width)
    return m


def _z_map():
    m = -np.ones(Z_W, np.int64)
    o_qc, o_kvc, o_kr = 0, MLA_Q_RANK, MLA_Q_RANK + MLA_KV_RANK
    o_fq = o_kr + MLA_ROPE
    o_fk, o_fv = o_fq + 256, o_fq + 512
    o_fl = o_fv + 256
    o_lx = o_fl + FOX_HEADS
    o_lg = o_lx + LRU_WIDTH
    m[Z_FQ:Z_FQ + 512] = np.where(_head_pad_map(4, 64) >= 0, _head_pad_map(4, 64) + o_fq, -1)
    m[Z_FK:Z_FK + 512] = np.where(_head_pad_map(4, 64) >= 0, _head_pad_map(4, 64) + o_fk, -1)
    m[Z_FV:Z_FV + 512] = np.where(_head_pad_map(4, 64) >= 0, _head_pad_map(4, 64) + o_fv, -1)
    m[Z_LX:Z_LX + 512] = o_lx + np.arange(512)
    m[Z_LG:Z_LG + 512] = o_lg + np.arange(512)
    m[Z_QC:Z_QC + MLA_Q_RANK] = o_qc + np.arange(MLA_Q_RANK)
    m[Z_KVC:Z_KVC + MLA_KV_RANK] = o_kvc + np.arange(MLA_KV_RANK)
    m[Z_KR + ROPE_AT:Z_KR + ROPE_AT + MLA_ROPE] = o_kr + np.arange(MLA_ROPE)
    m[Z_FL:Z_FL + FOX_HEADS] = o_fl + np.arange(FOX_HEADS)
    return m


def _ukv_map():
    m = -np.ones(2 * HEADS * LANE, np.int64)
    for h in range(HEADS):
        m[h * LANE:h * LANE + MLA_NOPE] = h * (MLA_NOPE + MLA_V) + np.arange(MLA_NOPE)
        m[HEADS * LANE + h * LANE:HEADS * LANE + h * LANE + MLA_V] = h * (MLA_NOPE + MLA_V) + MLA_NOPE + np.arange(MLA_V)
    return m


def _omix_map():
    return np.concatenate([_head_pad_map(4, 64), np.where(_head_pad_map(4, 64) >= 0, _head_pad_map(4, 64) + 256, -1),
                           512 + np.arange(512)])


def _pad_to(m, n):
    return np.concatenate([m, -np.ones(n - m.shape[0], np.int64)])


def _runs(m):
    out, at = [], 0
    while at < len(m):
        end = at + 1
        while end < len(m) and (m[end] == m[end - 1] + 1 if m[at] >= 0 else m[end] < 0):
            end += 1
        out.append((int(m[at]), end - at))
        at = end
    return out


def _take_runs(a, m, axis):
    parts = []
    for start, size in _runs(m):
        if start < 0:
            shape = list(a.shape)
            shape[axis] = size
            parts.append(jnp.zeros(shape, a.dtype))
        else:
            parts.append(lax.slice_in_dim(a, start, start + size, axis=axis))
    return parts[0] if len(parts) == 1 else jnp.concatenate(parts, axis=axis)


def _take_pad(a, m, axis):
    return _take_runs(a, m, axis)


def _take_inv(a, m, axis):
    n = int(m.max()) + 1
    inv = np.zeros(n, np.int64)
    inv[m[m >= 0]] = np.nonzero(m >= 0)[0]
    return _take_runs(a, inv, axis)


Z_MAP = _z_map()
UQ_COL_MAP = _head_pad_map(HEADS, MLA_NOPE + MLA_ROPE)
UQ_ROW_MAP = _pad_to(np.arange(MLA_Q_RANK), QC_W)
UKV_MAP = _ukv_map()
OMIX_MAP = _omix_map()
OMIX_W = 1536


def _rope_tables(width, at):
    half = MLA_ROPE // 2
    inv = ROPE_THETA ** (-np.arange(half, dtype=np.float32) / half)
    freq = np.zeros((1, width), np.float32)
    m1 = np.zeros((1, width), np.float32)
    m2 = np.zeros((1, width), np.float32)
    for h in range(width // LANE):
        b = h * LANE + at
        freq[0, b:b + half] = inv
        freq[0, b + half:b + 2 * half] = inv
        m1[0, b:b + half] = 1.0
        m2[0, b + half:b + 2 * half] = 1.0
    return freq, m1, m2


def _view(r):
    return r if isinstance(r, tuple) else (r, r.shape[1], 0)


def _blk(dim, cap):
    if dim <= cap:
        return dim
    for b in range(cap, LANE - 1, -LANE):
        if dim % b == 0:
            return b
    return dim


@functools.partial(jax.custom_vjp, nondiff_argnums=(1, 2))
def _roll(x, shift, axis):
    return pltpu.roll(x, shift, axis)


def _roll_fwd(x, shift, axis):
    return pltpu.roll(x, shift, axis), None


def _roll_bwd(shift, axis, _, g):
    return (pltpu.roll(g, g.shape[axis] - shift, axis),)


_roll.defvjp(_roll_fwd, _roll_bwd)


ROW_VMEM_BUDGET = 20 * 2 ** 20
ROW_TILES = (1024, 512, 256)


def _row_block(n, bytes_per_row):
    for tb in ROW_TILES:
        if n % tb == 0 and 2 * tb * bytes_per_row <= ROW_VMEM_BUDGET:
            return tb
    return min(ROW_TILES[-1], n)


def _rowwise(name, fn, rows, pars, outs):
    rows = [_view(r) for r in rows]
    n = rows[0][0].shape[0]
    tb = _row_block(n, sum(w * a.dtype.itemsize for a, w, _ in rows) + sum(w * jnp.dtype(dt).itemsize for w, dt in outs))
    nr, npar = len(rows), len(pars)

    def kern(*refs):
        r = [refs[k][...].astype(f32) for k in range(nr)]
        p = [refs[nr + k][...] for k in range(npar)]
        res = fn(*r, *p)
        for o_ref, o in zip(refs[nr + npar:], res):
            o_ref[...] = o.astype(o_ref.dtype)

    in_specs = [pl.BlockSpec((tb, w), lambda i, j=idx: (i, j)) for (_, w, idx) in rows]
    in_specs += [pl.BlockSpec(p.shape, lambda i: (0, 0)) for p in pars]
    out_specs = [pl.BlockSpec((tb, w), lambda i: (i, 0)) for (w, _) in outs]
    out_shape = [jax.ShapeDtypeStruct((n, w), dt) for (w, dt) in outs]
    return pl.pallas_call(kern, name=name, grid=(n // tb,), in_specs=in_specs, out_specs=out_specs, out_shape=out_shape,
                          compiler_params=pltpu.CompilerParams(dimension_semantics=("parallel",)))(*[r[0] for r in rows], *pars)


def _rowwise_bwd(name, fn, rows, pars, cts, ndiff, adds=None, dts=None):
    rows = [_view(r) for r in rows]
    dts = dts or [f32] * ndiff
    adds = adds or {}
    add_keys = sorted(adds)
    n = rows[0][0].shape[0]
    tb = _row_block(n, sum(w * a.dtype.itemsize for a, w, _ in rows) + sum(c.shape[1] * c.dtype.itemsize for c in cts)
                    + sum(a.shape[1] * a.dtype.itemsize for a in adds.values())
                    + sum(rows[k][1] * jnp.dtype(dts[k]).itemsize for k in range(ndiff)))
    nr, npar, nct, nadd = len(rows), len(pars), len(cts), len(add_keys)

    def kern(*refs):
        i = pl.program_id(0)
        r = [refs[k][...].astype(f32) for k in range(nr)]
        p = [refs[nr + k][...] for k in range(npar)]
        ct = [refs[nr + npar + k][...].astype(f32) for k in range(nct)]
        ad = {key: refs[nr + npar + nct + k][...] for k, key in enumerate(add_keys)}
        o_refs = refs[nr + npar + nct + nadd:]

        def g(*d):
            return tuple(fn(*d[:ndiff], *r[ndiff:], *d[ndiff:]))

        _, vjp = jax.vjp(g, *r[:ndiff], *p)
        grads = vjp(tuple(ct))
        for k in range(ndiff):
            gk = grads[k]
            if k in ad:
                gk = gk + ad[k]
            o_refs[k][...] = gk.astype(o_refs[k].dtype)

        @pl.when(i == 0)
        def _():
            for k in range(npar):
                o_refs[ndiff + k][...] = jnp.zeros_like(o_refs[ndiff + k])

        for k in range(npar):
            o_refs[ndiff + k][...] += grads[ndiff + k]

    in_specs = [pl.BlockSpec((tb, w), lambda i, j=idx: (i, j)) for (_, w, idx) in rows]
    in_specs += [pl.BlockSpec(p.shape, lambda i: (0, 0)) for p in pars]
    in_specs += [pl.BlockSpec((tb, c.shape[1]), lambda i: (i, 0)) for c in cts]
    in_specs += [pl.BlockSpec((tb, adds[k].shape[1]), lambda i: (i, 0)) for k in add_keys]
    out_specs = [pl.BlockSpec((tb, rows[k][1]), lambda i: (i, 0)) for k in range(ndiff)]
    out_specs += [pl.BlockSpec(p.shape, lambda i: (0, 0)) for p in pars]
    out_shape = [jax.ShapeDtypeStruct((n, rows[k][1]), dts[k]) for k in range(ndiff)]
    out_shape += [jax.ShapeDtypeStruct(p.shape, f32) for p in pars]
    res = pl.pallas_call(kern, name=name, grid=(n // tb,), in_specs=in_specs, out_specs=out_specs, out_shape=out_shape,
                         compiler_params=pltpu.CompilerParams(dimension_semantics=("arbitrary",)))(
        *[r[0] for r in rows], *pars, *cts, *[adds[k] for k in add_keys])
    return res[:ndiff], res[ndiff:]


_DOT_DIMS = {"nn": ((1,), (0,)), "nt": ((1,), (1,)), "tn": ((0,), (0,))}

MM_VMEM_BUDGET = 36 * 2 ** 20
MM_MAX_TM = 1408
MM_STEP, MM_RESULT, MM_XPOSE, MM_CAST = 700.0, 7.5e-4, 9e-4, 1e-3


def _tile_candidates(dim):
    c = [d for d in range(LANE, dim + 1, LANE) if dim % d == 0]
    return c or [dim]


@functools.lru_cache(maxsize=None)
def _mm_tiles(mode, m, n, k, a_bytes, b_bytes, o_bytes):
    best, best_cost = None, None
    for tm in _tile_candidates(m):
        if tm > MM_MAX_TM:
            continue
        for tn in _tile_candidates(n):
            for tk in _tile_candidates(k):
                vmem = 2 * (tm * tk * a_bytes + tk * tn * b_bytes + tm * tn * o_bytes) + 4 * tm * tn * (2 if tk < k else 1)
                vmem += (2 * tm * tk if a_bytes > 2 else 0) + (2 * tk * tn if b_bytes > 2 else 0)
                if vmem > MM_VMEM_BUDGET:
                    continue
                steps = (m // tm) * (n // tn) * (k // tk)
                cost = steps * MM_STEP + m * n * (k // tk) * MM_RESULT
                if mode == "tn":
                    cost += m * k * (n // tn) * MM_XPOSE
                cost += (m * k * (n // tn) * MM_CAST if a_bytes > 2 else 0) + (k * n * (m // tm) * MM_CAST if b_bytes > 2 else 0)
                if best is None or cost < best_cost:
                    best, best_cost = (tm, tn, tk), cost
    return best


def _mm(name, a, b, mode="nn", out_dtype=f32, res=None):
    if mode == "nn":
        (m, k), (_, n) = a.shape, b.shape
    elif mode == "nt":
        (m, k), (n, _) = a.shape, b.shape
    else:
        (k, m), (_, n) = a.shape, b.shape
    has_res = res is not None
    tm, tn, tk = _mm_tiles(mode, m, n, k, a.dtype.itemsize, b.dtype.itemsize,
                           jnp.dtype(out_dtype).itemsize + (res.dtype.itemsize if has_res else 0))
    nk = k // tk
    dims = (_DOT_DIMS[mode], ((), ()))

    def kern(*refs):
        a_ref, b_ref = refs[0], refs[1]
        o_ref, acc_ref = refs[-2], refs[-1]
        kk = pl.program_id(2)
        part = lax.dot_general(a_ref[...].astype(bf16), b_ref[...].astype(bf16), dims, preferred_element_type=f32)

        def finish(out):
            if has_res:
                out = out + refs[2][...]
            o_ref[...] = out.astype(o_ref.dtype)

        if nk == 1:
            finish(part)
            return

        @pl.when(kk == 0)
        def _():
            acc_ref[...] = part

        @pl.when(jnp.logical_and(kk > 0, kk < nk - 1))
        def _():
            acc_ref[...] += part

        @pl.when(kk == nk - 1)
        def _():
            finish(acc_ref[...] + part)

    if mode == "tn":
        a_spec = pl.BlockSpec((tk, tm), lambda i, j, kk: (kk, i))
    else:
        a_spec = pl.BlockSpec((tm, tk), lambda i, j, kk: (i, kk))
    if mode == "nt":
        b_spec = pl.BlockSpec((tn, tk), lambda i, j, kk: (j, kk))
    else:
        b_spec = pl.BlockSpec((tk, tn), lambda i, j, kk: (kk, j))
    in_specs = [a_spec, b_spec]
    args = [a, b]
    if has_res:
        in_specs.append(pl.BlockSpec((tm, tn), lambda i, j, kk: (i, j)))
        args.append(res)
    return pl.pallas_call(
        kern, name=name, grid=(m // tm, n // tn, nk), in_specs=in_specs,
        out_specs=pl.BlockSpec((tm, tn), lambda i, j, kk: (i, j)),
        out_shape=jax.ShapeDtypeStruct((m, n), out_dtype),
        scratch_shapes=[pltpu.VMEM((tm, tn) if nk > 1 else (SUBLANE, LANE), f32)],
        compiler_params=pltpu.CompilerParams(dimension_semantics=("parallel", "parallel", "arbitrary")))(*args)


ATT_TQ, ATT_TK = 512, 512


def _att_tiles(s_len):
    tk = min(ATT_TK, s_len)
    return min(ATT_TQ, tk), tk


def _fold_scale(scale):
    return (scale, 1.0) if math.frexp(scale)[0] == 0.5 else (1.0, scale)


def _as_row(col):
    return jnp.max(jnp.broadcast_to(col, (col.shape[0], LANE)).T[:SUBLANE], axis=0, keepdims=True)


def _scores_t(kb, q_t, s_mul, ck, diag_offset, tq, tk):
    s = jnp.dot(kb, q_t, preferred_element_type=f32)
    if s_mul != 1.0:
        s = s * s_mul
    if ck is not None:
        s = s - ck
    if diag_offset is None:
        return s
    key = lax.broadcasted_iota(jnp.int32, (tk, tq), 0)
    query = lax.broadcasted_iota(jnp.int32, (tk, tq), 1) + diag_offset
    return jnp.where(key <= query, s, -jnp.inf)


ATT_ROWS = 64


def _finish_scores(s, s_mul, ck, first_row):
    if s_mul != 1.0:
        s = s * s_mul
    if ck is not None:
        s = s - ck
    if first_row is None:
        return s
    row = lax.broadcasted_iota(jnp.int32, s.shape, 0) + first_row
    col = lax.broadcasted_iota(jnp.int32, s.shape, 1)
    return jnp.where(col <= row, s, -jnp.inf)


def _attn_fwd(name, q, k, v, scale, c_row=None):
    (qa, qo), (ka, ko), (va, vo) = q, k, v
    s_len = qa.shape[0]
    t = _att_tiles(s_len)[1]
    nt = s_len // t
    decay = c_row is not None
    q_mul, s_mul = _fold_scale(scale)

    def kern(*refs):
        q_ref, k_ref, v_ref = refs[:3]
        o_ref, lse_ref, lse_row_ref = refs[-3:]
        i = pl.program_id(1)
        qb = (q_ref[...] * q_mul).astype(bf16)

        def step(j, carry, diagonal):
            m, l, acc = carry
            rows = pl.ds(pl.multiple_of(j * t, t), t)
            kb = k_ref[rows, :].astype(bf16)
            vb = v_ref[rows, :].astype(bf16)
            s = lax.dot_general(qb, kb, (_DOT_DIMS["nt"], ((), ())), preferred_element_type=f32)
            s = _finish_scores(s, s_mul, refs[3][j] if decay else None, 0 if diagonal else None)
            m_new = jnp.maximum(m, jnp.max(s, axis=1, keepdims=True))
            alpha = jnp.exp(m - m_new)
            p = jnp.exp(s - m_new)
            l = alpha * l + jnp.sum(p, axis=1, keepdims=True)
            acc = alpha * acc + jnp.dot(p.astype(bf16), vb, preferred_element_type=f32)
            return m_new, l, acc

        init = (jnp.full((t, 1), -jnp.inf, f32), jnp.zeros((t, 1), f32), jnp.zeros((t, LANE), f32))
        m, l, acc = step(i, lax.fori_loop(0, i, lambda j, c: step(j, c, False), init), True)
        o_ref[...] = acc / l
        lse = m + jnp.log(l)
        lse_ref[...] = lse
        lse_row_ref[...] = _as_row(lse)

    in_specs = [pl.BlockSpec((t, LANE), lambda h, i: (i, qo + h)),
                pl.BlockSpec((s_len, LANE), lambda h, i: (0, ko + h)),
                pl.BlockSpec((s_len, LANE), lambda h, i: (0, vo + h))]
    args = [qa, ka, va]
    if decay:
        in_specs.append(pl.BlockSpec((None, nt, 1, t), lambda h, i: (h, 0, 0, 0)))
        args.append(c_row)
    return pl.pallas_call(
        kern, name=name, grid=(HEADS, nt), in_specs=in_specs,
        out_specs=[pl.BlockSpec((t, LANE), lambda h, i: (i, h)), pl.BlockSpec((None, t, 1), lambda h, i: (h, i, 0)),
                   pl.BlockSpec((None, None, 1, t), lambda h, i: (h, i, 0, 0))],
        out_shape=[jax.ShapeDtypeStruct((s_len, HEADS * LANE), f32), jax.ShapeDtypeStruct((HEADS, s_len, 1), f32),
                   jax.ShapeDtypeStruct((HEADS, nt, 1, t), f32)],
        compiler_params=pltpu.CompilerParams(dimension_semantics=("parallel", "arbitrary")))(*args)


def _attn_dq(name, q, k, v, o, do, lse, scale, c_row=None):
    (qa, qo), (ka, ko), (va, vo) = q, k, v
    s_len = qa.shape[0]
    t = _att_tiles(s_len)[1]
    nt = s_len // t
    decay = c_row is not None
    q_mul, s_mul = _fold_scale(scale)

    rp = min(ATT_ROWS, t)

    def kern(*refs):
        q_ref, k_ref, v_ref, o_ref, do_ref, lse_ref = refs[:6]
        dq_ref, delta_row_ref, drow_ref, delta_ref, s_ref, dp_ref, ds_ref = refs[-7:]
        i = pl.program_id(1)
        qb = (q_ref[...] * q_mul).astype(bf16)
        dob = do_ref[...]
        delta = jnp.sum(dob * o_ref[...], axis=1, keepdims=True)
        delta_ref[...] = delta
        delta_row_ref[...] = _as_row(delta)
        dob = dob.astype(bf16)
        drow_ref[...] = jnp.zeros((t, 1), f32)
        dq_ref[...] = jnp.zeros((t, LANE), f32)

        def step(j, diagonal):
            rows = pl.ds(pl.multiple_of(j * t, t), t)
            kb = k_ref[rows, :].astype(bf16)
            s_ref[...] = lax.dot_general(qb, kb, (_DOT_DIMS["nt"], ((), ())), preferred_element_type=f32)
            dp_ref[...] = lax.dot_general(dob, v_ref[rows, :].astype(bf16), (_DOT_DIMS["nt"], ((), ())),
                                          preferred_element_type=f32)
            ck = refs[6][j] if decay else None

            def rows_of(c, carry):
                r = slice(c * rp, (c + 1) * rp)
                s = _finish_scores(s_ref[r, :], s_mul, ck, c * rp if diagonal else None)
                ds = jnp.exp(s - lse_ref[r, :]) * (dp_ref[r, :] - delta_ref[r, :])
                drow_ref[r, :] += jnp.sum(ds, axis=1, keepdims=True)
                ds_ref[r, :] = ds.astype(bf16)
                return carry

            for c in range(t // rp):
                rows_of(c, 0)
            dq_ref[...] += jnp.dot(ds_ref[...], kb, preferred_element_type=f32)

        def below(j, carry):
            step(j, False)
            return carry

        lax.fori_loop(0, i, below, 0)
        step(i, True)
        dq_ref[...] = dq_ref[...] * scale

    in_specs = [pl.BlockSpec((t, LANE), lambda h, i: (i, qo + h)),
                pl.BlockSpec((s_len, LANE), lambda h, i: (0, ko + h)),
                pl.BlockSpec((s_len, LANE), lambda h, i: (0, vo + h)),
                pl.BlockSpec((t, LANE), lambda h, i: (i, h)),
                pl.BlockSpec((t, LANE), lambda h, i: (i, h)),
                pl.BlockSpec((None, t, 1), lambda h, i: (h, i, 0))]
    args = [qa, ka, va, o, do, lse]
    if decay:
        in_specs.append(pl.BlockSpec((None, nt, 1, t), lambda h, i: (h, 0, 0, 0)))
        args.append(c_row)
    col = pl.BlockSpec((None, t, 1), lambda h, i: (h, i, 0))
    return pl.pallas_call(
        kern, name=name, grid=(HEADS, nt), in_specs=in_specs,
        out_specs=[pl.BlockSpec((t, LANE), lambda h, i: (i, h)), pl.BlockSpec((None, None, 1, t), lambda h, i: (h, i, 0, 0)), col],
        out_shape=[jax.ShapeDtypeStruct((s_len, HEADS * LANE), f32), jax.ShapeDtypeStruct((HEADS, nt, 1, t), f32),
                   jax.ShapeDtypeStruct((HEADS, s_len, 1), f32)],
        scratch_shapes=[pltpu.VMEM((t, 1), f32), pltpu.VMEM((t, t), f32), pltpu.VMEM((t, t), f32), pltpu.VMEM((t, t), bf16)],
        compiler_params=pltpu.CompilerParams(dimension_semantics=("parallel", "arbitrary")))(*args)


def _attn_dkv(name, q, k, v, do, lse, delta, scale, c_col=None):
    (qa, qo), (ka, ko), (va, vo) = q, k, v
    s_len = qa.shape[0]
    tq, tk = _att_tiles(s_len)
    assert lse.shape == (HEADS, s_len // tq, 1, tq), (lse.shape, tq)
    nq, per = s_len // tq, tk // tq
    decay = c_col is not None
    q_mul, s_mul = _fold_scale(scale)

    def kern(*refs):
        q_ref, k_ref, v_ref, do_ref, lse_ref, delta_ref = refs[:6]
        j = pl.program_id(1)
        kb = k_ref[...].astype(bf16)
        vb = v_ref[...].astype(bf16)
        ck = refs[6][...] if decay else None

        def step(i, carry, diagonal):
            dk, dv, dsum = carry
            for d in range(per):
                tile = i * per + d
                rows = pl.ds(pl.multiple_of(tile * tq, tq), tq)
                qb = (q_ref[rows, :] * q_mul).astype(bf16)
                dob = do_ref[rows, :].astype(bf16)
                s = _scores_t(kb, qb.T, s_mul, ck, d * tq if diagonal else None, tq, tk)
                p = jnp.exp(s - lse_ref[tile])
                dv = dv + jnp.dot(p.astype(bf16), dob, preferred_element_type=f32)
                dp = jnp.dot(vb, dob.T, preferred_element_type=f32)
                ds = p * (dp - delta_ref[tile])
                dk = dk + jnp.dot(ds.astype(bf16), qb, preferred_element_type=f32)
                if decay:
                    dsum = dsum + ds
            return dk, dv, dsum

        init = (jnp.zeros((tk, LANE), f32), jnp.zeros((tk, LANE), f32), jnp.zeros((tk, tq), f32))
        dk, dv, dsum = lax.fori_loop(j + 1, s_len // tk, lambda i, c: step(i, c, False), step(j, init, True))
        if decay:
            dk_ref, dv_ref, dc_ref = refs[-3:]
            dc_ref[...] = -jnp.sum(dsum, axis=1, keepdims=True)
        else:
            dk_ref, dv_ref = refs[-2:]
        dk_ref[...] = dk * s_mul
        dv_ref[...] = dv

    stat = pl.BlockSpec((None, nq, 1, tq), lambda h, j: (h, 0, 0, 0))
    in_specs = [pl.BlockSpec((s_len, LANE), lambda h, j: (0, qo + h)),
                pl.BlockSpec((tk, LANE), lambda h, j: (j, ko + h)),
                pl.BlockSpec((tk, LANE), lambda h, j: (j, vo + h)),
                pl.BlockSpec((s_len, LANE), lambda h, j: (0, h)), stat, stat]
    args = [qa, ka, va, do, lse, delta]
    out_specs = [pl.BlockSpec((tk, LANE), lambda h, j: (j, h)), pl.BlockSpec((tk, LANE), lambda h, j: (j, h))]
    out_shape = [jax.ShapeDtypeStruct((s_len, HEADS * LANE), f32), jax.ShapeDtypeStruct((s_len, HEADS * LANE), f32)]
    if decay:
        in_specs.append(pl.BlockSpec((None, tk, 1), lambda h, j: (h, j, 0)))
        args.append(c_col)
        out_specs.append(pl.BlockSpec((None, tk, 1), lambda h, j: (h, j, 0)))
        out_shape.append(jax.ShapeDtypeStruct((HEADS, s_len, 1), f32))
    return pl.pallas_call(
        kern, name=name, grid=(HEADS, s_len // tk), in_specs=in_specs, out_specs=out_specs, out_shape=out_shape,
        compiler_params=pltpu.CompilerParams(dimension_semantics=("parallel", "arbitrary")))(*args)


CONV_TS, CONV_CB = 2048, 256
FFN_ROWS = 64


def _conv_fwd(name, x, w, b, taps):
    xa, width, xidx = _view(x)
    s_len = xa.shape[0]
    ts, cb = min(CONV_TS, s_len), CONV_CB
    xo = xidx * width // cb

    def kern(x_ref, halo_ref, w_ref, b_ref, o_ref):
        i = pl.program_id(1)
        xb = x_ref[...]
        halo = jnp.where(i == 0, 0.0, halo_ref[...])
        xx = jnp.concatenate([halo, xb], axis=0)
        out = b_ref[...] + w_ref[taps - 1:taps, :] * xb
        for k in range(taps - 1):
            out = out + w_ref[k:k + 1, :] * pltpu.roll(xx, taps - 1 - k, 0)[SUBLANE:]
        o_ref[...] = out

    return pl.pallas_call(
        kern, name=name, grid=(width // cb, s_len // ts),
        in_specs=[pl.BlockSpec((ts, cb), lambda j, i: (i, xo + j)),
                  pl.BlockSpec((SUBLANE, cb), lambda j, i: (jnp.maximum(i * (ts // SUBLANE) - 1, 0), xo + j)),
                  pl.BlockSpec((taps, cb), lambda j, i: (0, j)),
                  pl.BlockSpec((1, cb), lambda j, i: (0, j))],
        out_specs=pl.BlockSpec((ts, cb), lambda j, i: (i, j)),
        out_shape=jax.ShapeDtypeStruct((s_len, width), f32),
        compiler_params=pltpu.CompilerParams(dimension_semantics=("parallel", "parallel")))(xa, xa, w, b)


def _conv_bwd(name, x, dout, w, taps, dout2=None, dx_dtype=f32):
    xa, width, xidx = _view(x)
    s_len = xa.shape[0]
    ts, cb = min(CONV_TS, s_len), CONV_CB
    xo = xidx * width // cb
    n_i = s_len // ts
    two = dout2 is not None

    def kern(*refs):
        x_ref, halo_ref, w_ref = refs[:3]
        dx_ref, dw_ref, db_ref = refs[-3:]
        i = pl.program_id(1)
        if two:
            d = refs[3][...] + refs[5][...]
            dn = refs[4][...] + refs[6][...]
        else:
            d, dn = refs[3][...], refs[4][...]
        dn = jnp.where(i == n_i - 1, 0.0, dn)
        xb = x_ref[...]
        halo = jnp.where(i == 0, 0.0, halo_ref[...])
        xx = jnp.concatenate([halo, xb], axis=0)
        dd = jnp.concatenate([d, dn], axis=0)

        @pl.when(i == 0)
        def _():
            dw_ref[...] = jnp.zeros_like(dw_ref)
            db_ref[...] = jnp.zeros_like(db_ref)

        dx = w_ref[taps - 1:taps, :] * d
        dw_ref[taps - 1:taps, :] += jnp.sum(d * xb, axis=0, keepdims=True)
        for k in range(taps - 1):
            sh = taps - 1 - k
            dx = dx + w_ref[k:k + 1, :] * pltpu.roll(dd, ts + SUBLANE - sh, 0)[:ts]
            dw_ref[k:k + 1, :] += jnp.sum(d * pltpu.roll(xx, sh, 0)[SUBLANE:], axis=0, keepdims=True)
        dx_ref[...] = dx.astype(dx_ref.dtype)
        db_ref[...] += jnp.sum(d, axis=0, keepdims=True)

    d_spec = pl.BlockSpec((ts, cb), lambda j, i: (i, j))
    dn_spec = pl.BlockSpec((SUBLANE, cb), lambda j, i: (jnp.minimum((i + 1) * (ts // SUBLANE), s_len // SUBLANE - 1), j))
    in_specs = [pl.BlockSpec((ts, cb), lambda j, i: (i, xo + j)),
                pl.BlockSpec((SUBLANE, cb), lambda j, i: (jnp.maximum(i * (ts // SUBLANE) - 1, 0), xo + j)),
                pl.BlockSpec((taps, cb), lambda j, i: (0, j)), d_spec, dn_spec]
    args = [xa, xa, w, dout, dout]
    if two:
        in_specs += [d_spec, dn_spec]
        args += [dout2, dout2]
    return pl.pallas_call(
        kern, name=name, grid=(width // cb, n_i), in_specs=in_specs,
        out_specs=[pl.BlockSpec((ts, cb), lambda j, i: (i, j)), pl.BlockSpec((taps, cb), lambda j, i: (0, j)),
                   pl.BlockSpec((1, cb), lambda j, i: (0, j))],
        out_shape=[jax.ShapeDtypeStruct((s_len, width), dx_dtype), jax.ShapeDtypeStruct((taps, width), f32),
                   jax.ShapeDtypeStruct((1, width), f32)],
        compiler_params=pltpu.CompilerParams(dimension_semantics=("parallel", "arbitrary")))(*args)


def _conv_rows(xx, w_ref, b_ref, taps):
    out = b_ref[...] + w_ref[taps - 1:taps, :] * xx[SUBLANE:]
    for k in range(taps - 1):
        out = out + w_ref[k:k + 1, :] * pltpu.roll(xx, taps - 1 - k, 0)[SUBLANE:]
    return out


def _ffn_act_fwd(name, up, w, b):
    s_len = up.shape[0]
    ts, cb = min(CONV_TS, s_len), CONV_CB
    nf = D_FF // cb

    def kern(g_ref, gp_ref, v_ref, vp_ref, wg_ref, wv_ref, bg_ref, bv_ref, o_ref):
        first = pl.program_id(1) == 0
        ug = _conv_rows(jnp.concatenate([jnp.where(first, 0.0, gp_ref[...]), g_ref[...]], axis=0), wg_ref, bg_ref, FFN_CONV)
        uv = _conv_rows(jnp.concatenate([jnp.where(first, 0.0, vp_ref[...]), v_ref[...]], axis=0), wv_ref, bv_ref, FFN_CONV)
        o_ref[...] = (jax.nn.silu(ug) * uv).astype(o_ref.dtype)

    def half(off):
        return [pl.BlockSpec((ts, cb), lambda j, i: (i, off + j)),
                pl.BlockSpec((SUBLANE, cb), lambda j, i: (jnp.maximum(i * (ts // SUBLANE) - 1, 0), off + j))]

    def par(rows, off):
        return pl.BlockSpec((rows, cb), lambda j, i: (0, off + j))

    return pl.pallas_call(
        kern, name=name, grid=(nf, s_len // ts),
        in_specs=half(0) + half(nf) + [par(FFN_CONV, 0), par(FFN_CONV, nf), par(1, 0), par(1, nf)],
        out_specs=pl.BlockSpec((ts, cb), lambda j, i: (i, j)),
        out_shape=jax.ShapeDtypeStruct((s_len, D_FF), bf16),
        compiler_params=pltpu.CompilerParams(dimension_semantics=("parallel", "parallel")))(up, up, up, up, w, w, b, b)


def _ffn_act_bwd(name, up, dact, w, b):
    s_len = up.shape[0]
    ts, cb = min(CONV_TS, s_len), CONV_CB
    nf = D_FF // cb
    n_i = s_len // ts
    taps = FFN_CONV

    ch = min(FFN_ROWS, ts)

    def kern(g_ref, gp_ref, gn_ref, v_ref, vp_ref, vn_ref, d_ref, dn_ref, wg_ref, wv_ref, bg_ref, bv_ref,
             dg_ref, dv_ref, dwg_ref, dwv_ref, dbg_ref, dbv_ref, gx_ref, vx_ref, dd_ref):
        i = pl.program_id(1)
        first, last = i == 0, i == n_i - 1
        for x_ref, p_ref, n_ref, ext in ((g_ref, gp_ref, gn_ref, gx_ref), (v_ref, vp_ref, vn_ref, vx_ref)):
            ext[:SUBLANE, :] = jnp.where(first, 0.0, p_ref[...])
            ext[SUBLANE:SUBLANE + ts, :] = x_ref[...]
            ext[SUBLANE + ts:, :] = jnp.where(last, 0.0, n_ref[...])
        dd_ref[:ts, :] = d_ref[...]
        dd_ref[ts:, :] = jnp.where(last, 0.0, dn_ref[...])

        @pl.when(first)
        def _():
            for ref in (dwg_ref, dwv_ref, dbg_ref, dbv_ref):
                ref[...] = jnp.zeros_like(ref)

        def rows_of(c, carry):
            r0 = pl.multiple_of(c * ch, ch)
            gx, vx = gx_ref[pl.ds(r0, ch + 2 * SUBLANE), :], vx_ref[pl.ds(r0, ch + 2 * SUBLANE), :]
            ug, uv = _conv_rows(gx, wg_ref, bg_ref, taps), _conv_rows(vx, wv_ref, bv_ref, taps)
            dd = dd_ref[pl.ds(r0, ch + SUBLANE), :]
            sg = jax.nn.sigmoid(ug)
            out = []
            for du, xx, w_ref, dx_ref, sums in ((dd * uv * (sg * (1.0 + ug * (1.0 - sg))), gx, wg_ref, dg_ref, carry[0]),
                                                (dd * (ug * sg), vx, wv_ref, dv_ref, carry[1])):
                d = du[:ch]
                dx = w_ref[taps - 1:taps, :] * d
                new = [None] * (taps + 1)
                new[taps - 1] = sums[taps - 1] + jnp.sum(d * xx[SUBLANE:SUBLANE + ch], axis=0, keepdims=True)
                for k in range(taps - 1):
                    sh = taps - 1 - k
                    dx = dx + w_ref[k:k + 1, :] * pltpu.roll(du, ch + SUBLANE - sh, 0)[:ch]
                    new[k] = sums[k] + jnp.sum(d * pltpu.roll(xx, sh, 0)[SUBLANE:SUBLANE + ch], axis=0, keepdims=True)
                new[taps] = sums[taps] + jnp.sum(d, axis=0, keepdims=True)
                dx_ref[pl.ds(r0, ch), :] = dx.astype(dx_ref.dtype)
                out.append(tuple(new))
            return tuple(out)

        zero = tuple(jnp.zeros((1, cb), f32) for _ in range(taps + 1))
        sums_g, sums_v = lax.fori_loop(0, ts // ch, rows_of, (zero, zero))
        for sums, dw_ref, db_ref in ((sums_g, dwg_ref, dbg_ref), (sums_v, dwv_ref, dbv_ref)):
            for k in range(taps):
                dw_ref[k:k + 1, :] += sums[k]
            db_ref[...] += sums[taps]

    blocks = s_len // SUBLANE

    def half(off):
        return [pl.BlockSpec((ts, cb), lambda j, i: (i, off + j)),
                pl.BlockSpec((SUBLANE, cb), lambda j, i: (jnp.maximum(i * (ts // SUBLANE) - 1, 0), off + j)),
                pl.BlockSpec((SUBLANE, cb), lambda j, i: (jnp.minimum((i + 1) * (ts // SUBLANE), blocks - 1), off + j))]

    def par(rows, off):
        return pl.BlockSpec((rows, cb), lambda j, i: (0, off + j))

    d_specs = [pl.BlockSpec((ts, cb), lambda j, i: (i, j)),
               pl.BlockSpec((SUBLANE, cb), lambda j, i: (jnp.minimum((i + 1) * (ts // SUBLANE), blocks - 1), j))]
    out_par = [pl.BlockSpec((r, cb), lambda j, i: (0, j)) for r in (taps, taps, 1, 1)]
    return pl.pallas_call(
        kern, name=name, grid=(nf, n_i),
        in_specs=half(0) + half(nf) + d_specs + [par(taps, 0), par(taps, nf), par(1, 0), par(1, nf)],
        out_specs=[pl.BlockSpec((ts, cb), lambda j, i: (i, j))] * 2 + out_par,
        out_shape=[jax.ShapeDtypeStruct((s_len, D_FF), bf16)] * 2 + [jax.ShapeDtypeStruct((taps, D_FF), f32)] * 2
        + [jax.ShapeDtypeStruct((1, D_FF), f32)] * 2,
        scratch_shapes=[pltpu.VMEM((ts + 2 * SUBLANE, cb), f32)] * 2 + [pltpu.VMEM((ts + SUBLANE, cb), f32)],
        compiler_params=pltpu.CompilerParams(dimension_semantics=("parallel", "arbitrary")))(
        up, up, up, up, up, up, dact, dact, w, w, b, b)


SCAN_ROWS = 128


def _block_scan(a, b, reverse):
    t = a.shape[0]
    row = lax.broadcasted_iota(jnp.int32, a.shape, 0)
    d = 1
    while d < t:
        keep = row < t - d if reverse else row >= d
        shift = t - d if reverse else d
        a_far = jnp.where(keep, pltpu.roll(a, shift, 0), 1.0)
        b_far = jnp.where(keep, pltpu.roll(b, shift, 0), 0.0)
        b = a * b_far + b
        a = a * a_far
        d *= 2
    return a, b


def _scan_fwd(name, a, b):
    s_len, width = a.shape
    t = min(SCAN_ROWS, s_len)

    def kern(a_ref, b_ref, h_ref):
        def block(k, carry):
            rows = pl.ds(pl.multiple_of(k * t, t), t)
            acc, h = _block_scan(a_ref[rows, :], b_ref[rows, :], False)
            h_ref[rows, :] = h + acc * carry
            return h_ref[pl.ds(k * t + t - 1, 1), :]

        lax.fori_loop(0, s_len // t, block, jnp.zeros((1, LANE), f32))

    spec = pl.BlockSpec((s_len, LANE), lambda j: (0, j))
    return pl.pallas_call(
        kern, name=name, grid=(width // LANE,), in_specs=[spec, spec], out_specs=spec,
        out_shape=jax.ShapeDtypeStruct((s_len, width), f32),
        compiler_params=pltpu.CompilerParams(dimension_semantics=("parallel",)))(a, b)


def _scan_bwd(name, a_next, h_prev, dh):
    s_len, width = dh.shape
    t = min(SCAN_ROWS, s_len)
    n_blocks = s_len // t

    def kern(an_ref, hp_ref, dh_ref, da_ref, db_ref):
        def block(kk, carry):
            k = n_blocks - 1 - kk
            rows = pl.ds(pl.multiple_of(k * t, t), t)
            acc, g = _block_scan(an_ref[rows, :], dh_ref[rows, :], True)
            g = g + acc * carry
            db_ref[rows, :] = g
            da_ref[rows, :] = g * hp_ref[rows, :]
            return db_ref[pl.ds(k * t, 1), :]

        lax.fori_loop(0, n_blocks, block, jnp.zeros((1, LANE), f32))

    spec = pl.BlockSpec((s_len, LANE), lambda j: (0, j))
    return pl.pallas_call(
        kern, name=name, grid=(width // LANE,), in_specs=[spec, spec, spec], out_specs=[spec, spec],
        out_shape=[jax.ShapeDtypeStruct((s_len, width), f32)] * 2,
        compiler_params=pltpu.CompilerParams(dimension_semantics=("parallel",)))(a_next, h_prev, dh)


def _lane_cumsum(x, reverse):
    n = x.shape[1]
    lane = lax.broadcasted_iota(jnp.int32, x.shape, 1)
    sh = 1
    while sh < n:
        if reverse:
            x = x + jnp.where(lane < n - sh, pltpu.roll(x, n - sh, 1), 0.0)
        else:
            x = x + jnp.where(lane >= sh, pltpu.roll(x, sh, 1), 0.0)
        sh *= 2
    return x


def _decay_fwd(name, fl_t, b8):
    def kern(f_ref, b_ref, c_ref):
        c_ref[...] = _lane_cumsum(jax.nn.log_sigmoid(f_ref[...] + b_ref[...]), False)

    return pl.pallas_call(kern, name=name, out_shape=jax.ShapeDtypeStruct(fl_t.shape, f32))(fl_t, b8)


def _decay_bwd(name, fl_t, b8, dc_key, dc_query):
    def kern(f_ref, b_ref, dck_ref, dcq_ref, df_ref, db_ref):
        dlogf = _lane_cumsum(dck_ref[...] + dcq_ref[...], True)
        df = dlogf * jax.nn.sigmoid(-(f_ref[...] + b_ref[...]))
        df_ref[...] = df
        db_ref[...] = jnp.sum(df, axis=1, keepdims=True)

    return pl.pallas_call(kern, name=name, out_shape=[jax.ShapeDtypeStruct(fl_t.shape, f32),
                                                      jax.ShapeDtypeStruct((SUBLANE, 1), f32)])(fl_t, b8, dc_key, dc_query)


def _rms(x, g, n):
    return x * lax.rsqrt(jnp.sum(x * x, axis=-1, keepdims=True) * (1.0 / n) + EPS) * g


def _loss_head(name, h, target, g, tb=512):
    n, d = h.shape
    tb = min(tb, n)

    def kern(h_ref, t_ref, g_ref, loss_ref, dh_ref, dg_ref):
        i = pl.program_id(0)
        tgt = t_ref[...]

        def f(hv, gv):
            err = _rms(hv, gv, d) - tgt
            return 0.5 * jnp.sum(jnp.sum(err * err, axis=-1, keepdims=True) * (1.0 / d), axis=0, keepdims=True)

        val, vjp = jax.vjp(f, h_ref[...], g_ref[...])
        dh, dg = vjp(jnp.ones((1, 1), f32))
        dh_ref[...] = dh

        @pl.when(i == 0)
        def _():
            loss_ref[...] = jnp.zeros_like(loss_ref)
            dg_ref[...] = jnp.zeros_like(dg_ref)

        loss_ref[...] += val
        dg_ref[...] += dg

    return pl.pallas_call(
        kern, name=name, grid=(n // tb,),
        in_specs=[pl.BlockSpec((tb, d), lambda i: (i, 0)), pl.BlockSpec((tb, d), lambda i: (i, 0)),
                  pl.BlockSpec((1, d), lambda i: (0, 0))],
        out_specs=[pl.BlockSpec((1, 1), lambda i: (0, 0)), pl.BlockSpec((tb, d), lambda i: (i, 0)),
                   pl.BlockSpec((1, d), lambda i: (0, 0))],
        out_shape=[jax.ShapeDtypeStruct((1, 1), f32), jax.ShapeDtypeStruct((n, d), f32), jax.ShapeDtypeStruct((1, d), f32)],
        compiler_params=pltpu.CompilerParams(dimension_semantics=("arbitrary",)))(h, target, g)


def _f_norm(x, g):
    return (_rms(x, g, D_MODEL),)


def _f_latent(qc, kvc, gq, gkv):
    return _rms(qc, gq, MLA_Q_RANK), _rms(kvc, gkv, MLA_KV_RANK)


def _f_rope_table(pos, freq, m1, m2):
    ang = pos * freq
    sin = jnp.sin(ang)
    return jnp.cos(ang), -sin * m1, sin * m2


def _rope(x, cos, s_up, s_down):
    w = x.shape[1]
    return x * cos + _roll(x, w - MLA_ROPE // 2, 1) * s_up + _roll(x, MLA_ROPE // 2, 1) * s_down


def _f_mla_prep(q, kpart, kr, cos, s_up, s_down):
    def heads(t):
        return jnp.concatenate([t] * HEADS, axis=1)

    kr = _rope(kr, cos, s_up, s_down)
    return _rope(q, heads(cos), heads(s_up), heads(s_down)), kpart + heads(kr)


def _f_lru_gate(gates, xc, b_r, b_i, lam):
    r = jax.nn.sigmoid(gates[:, :LRU_WIDTH] + b_r)
    i = jax.nn.sigmoid(gates[:, LRU_WIDTH:] + b_i)
    log_a = -LRU_C * r * jax.nn.softplus(-lam)
    mult = jnp.sqrt(-jnp.tanh(log_a) * (1.0 + jnp.exp(2.0 * log_a)))
    return jnp.exp(log_a), mult * (i * xc)


def _f_merge(o_mla, o_fox, hs, lg, g):
    o_lru = hs * jax.nn.gelu(lg)
    return (jnp.concatenate([_rms(o_mla, g[:, :512], HEADS * MLA_V), _rms(o_fox, g[:, 512:1024], HEADS * FOX_HEAD_DIM),
                             _rms(o_lru, g[:, 1024:], LRU_WIDTH)], axis=1),)


def _f_ffn_gate(u):
    return (jax.nn.silu(u[:, :D_FF]) * u[:, D_FF:],)


def _f_ple(h, gpre, pp):
    return (h + jax.nn.sigmoid(gpre) * pp,)


MIX_PART = ["w_in", "w_uq", "w_ukv", "lru_conv_w"]
FFN_PART = ["w_o", "w_up", "ffn_conv_w", "w_down", "w_ple_gate", "w_ple_proj"]


def _prep_mix_weights(w):
    eye = jnp.eye(LRU_BLOCKS, dtype=f32)

    def block_diag(m):
        return (eye[:, None, :, None] * m[:, :, None, :]).reshape(LRU_WIDTH, LRU_WIDTH)

    return dict(
        w_in=_take_pad(w["w_in"], Z_MAP, 1),
        w_uq=_take_pad(_take_pad(w["w_uq"], UQ_COL_MAP, 1), UQ_ROW_MAP, 0),
        w_ukv=_take_pad(w["w_ukv"], UKV_MAP, 1),
        w_ri=jnp.concatenate([block_diag(w["w_r"]), block_diag(w["w_i"])], axis=1).astype(bf16),
        g_mix=w["g_mix"].reshape(1, -1), g_ffn=w["g_ffn"].reshape(1, -1), g_ple=w["g_ple"].reshape(1, -1),
        g_qc=_take_pad(w["g_qc"], UQ_ROW_MAP, 0).reshape(1, -1), g_kvc=w["g_kvc"].reshape(1, -1),
        g_out=_take_pad(w["g_out"], OMIX_MAP, 0).reshape(1, -1),
        b_f8=_take_pad(w["b_f"], _pad_to(np.arange(FOX_HEADS), SUBLANE), 0).reshape(SUBLANE, 1),
        lru_conv_w=w["lru_conv_w"], lru_conv_b=w["lru_conv_b"].reshape(1, -1),
        b_r=w["b_r"].reshape(1, -1), b_i=w["b_i"].reshape(1, -1), lam=w["lru_lambda"].reshape(1, -1),
        ffn_conv_b=w["ffn_conv_b"].reshape(1, -1),
    )


def _prep_ffn_weights(w):
    return dict(w_o=_take_pad(w["w_o"], OMIX_MAP, 0),
                w_up=w["w_up"], w_up_g=w["w_up"][:, :D_FF], w_up_v=w["w_up"][:, D_FF:], ffn_conv_w=w["ffn_conv_w"],
                w_down=w["w_down"], w_ple_gate=w["w_ple_gate"], w_ple_proj=w["w_ple_proj"])


def _rope_rows(pos):
    consts = [jnp.asarray(t) for t in _rope_tables(LANE, ROPE_AT)]
    return _rowwise("rope_table", _f_rope_table, [pos], consts, [(LANE, f32)] * 3)


def _key_decay(c_t, s_len):
    t = _att_tiles(s_len)[1]
    return c_t[:HEADS].reshape(HEADS, s_len // t, 1, t), c_t[:HEADS].reshape(HEADS, s_len, 1)


def _layer_fwd(l, h0, p_l, rope, weights_of):
    s_len = h0.shape[0]
    n = f"l{l}_"
    w = _prep_mix_weights(weights_of("mix", h0))
    xn, = _rowwise(n + "norm_mix", _f_norm, [h0], [w["g_mix"]], [(D_MODEL, bf16)])
    z = _mm(n + "in_proj", xn, w["w_in"])
    zq = (z, QC_W, Z_QC // QC_W)
    zkv = (z, LANE, Z_KVC // LANE)
    zkr = (z, LANE, Z_KR // LANE)
    zlx = (z, LRU_WIDTH, Z_LX // LRU_WIDTH)
    zlg = (z, LRU_WIDTH, Z_LG // LRU_WIDTH)
    qcn, kvn = _rowwise(n + "latent_norm", _f_latent, [zq, zkv], [w["g_qc"], w["g_kvc"]], [(QC_W, bf16), (LANE, bf16)])
    q = _mm(n + "uq", qcn, w["w_uq"])
    kv = _mm(n + "ukv", kvn, w["w_ukv"])
    kpart = (kv, HEADS * LANE, 0)
    qr, kk = _rowwise(n + "mla_prep", _f_mla_prep, [q, kpart, zkr, *rope], [],
                      [(HEADS * LANE, bf16), (HEADS * LANE, bf16)])
    mla_scale = (MLA_NOPE + MLA_ROPE) ** -0.5
    o_mla, lse_m, lse_m_row = _attn_fwd(n + "mla_fwd", (qr, 0), (kk, 0), (kv, HEADS), mla_scale)
    fl_t = z[:, Z_FL:Z_FL + SUBLANE].T
    c_t = _decay_fwd(n + "decay", fl_t, w["b_f8"])
    c_row, c_col = _key_decay(c_t, s_len)
    fox_scale = FOX_HEAD_DIM ** -0.5
    o_fox, lse_f, lse_f_row = _attn_fwd(n + "fox_fwd", (z, Z_FQ // LANE), (z, Z_FK // LANE), (z, Z_FV // LANE), fox_scale, c_row)
    xc = _conv_fwd(n + "lru_conv", zlx, w["lru_conv_w"], w["lru_conv_b"], LRU_CONV)
    gates = _mm(n + "lru_gates", xc, w["w_ri"])
    a, bx = _rowwise(n + "lru_gate", _f_lru_gate, [gates, xc], [w["b_r"], w["b_i"], w["lam"]],
                     [(LRU_WIDTH, f32), (LRU_WIDTH, f32)])
    hs = _scan_fwd(n + "lru_scan", a, bx)
    ocat, = _rowwise(n + "merge", _f_merge, [o_mla, o_fox, hs, zlg], [w["g_out"]], [(OMIX_W, bf16)])
    w.update(_prep_ffn_weights(weights_of("ffn", ocat)))
    h1 = _mm(n + "out_proj", ocat, w["w_o"], res=h0)
    xn2, = _rowwise(n + "norm_ffn", _f_norm, [h1], [w["g_ffn"]], [(D_MODEL, bf16)])
    up = _mm(n + "up_proj", xn2, w["w_up"])
    act = _ffn_act_fwd(n + "ffn_act", up, w["ffn_conv_w"], w["ffn_conv_b"])
    h2 = _mm(n + "down_proj", act, w["w_down"], res=h1)
    hn, = _rowwise(n + "norm_ple", _f_norm, [h2], [w["g_ple"]], [(D_MODEL, bf16)])
    gpre = _mm(n + "ple_gate", hn, w["w_ple_gate"])
    pp = _mm(n + "ple_proj", p_l, w["w_ple_proj"])
    h3, = _rowwise(n + "ple_mix", _f_ple, [h2, gpre, pp], [], [(D_MODEL, f32)])
    res = dict(h0=h0, xn=xn, z=z, qcn=qcn, kvn=kvn, q=q, kv=kv, qr=qr, kk=kk, o_mla=o_mla, lse_m=lse_m, fl_t=fl_t,
               lse_m_row=lse_m_row, lse_f_row=lse_f_row, c_row=c_row, c_col=c_col, o_fox=o_fox, lse_f=lse_f, xc=xc, gates=gates, a=a, hs=hs, ocat=ocat, h1=h1,
               xn2=xn2, up=up, act=act, h2=h2, hn=hn, gpre=gpre, pp=pp, p_l=p_l)
    return h3, res, w


def _layer_bwd(l, dh3, r, rope, w, token, grads_to):
    s_len = dh3.shape[0]
    n = f"l{l}_"
    g = {}
    w = dict(w, g_ple=w["g_ple"] + token)
    z = r["z"]
    zq = (z, QC_W, Z_QC // QC_W)
    zkv = (z, LANE, Z_KVC // LANE)
    zkr = (z, LANE, Z_KR // LANE)
    zlx = (z, LRU_WIDTH, Z_LX // LRU_WIDTH)
    zlg = (z, LRU_WIDTH, Z_LG // LRU_WIDTH)
    (dh2a, dgpre, dpp), _ = _rowwise_bwd(n + "ple_mix_b", _f_ple, [r["h2"], r["gpre"], r["pp"]], [], [dh3], 3,
                                         dts=[f32, bf16, bf16])
    g["w_ple_proj"] = _mm(n + "ple_proj_dw", r["p_l"], dpp, "tn", bf16)
    dhn = _mm(n + "ple_gate_dx", dgpre, w["w_ple_gate"], "nt")
    g["w_ple_gate"] = _mm(n + "ple_gate_dw", r["hn"], dgpre, "tn", bf16)
    (dh2,), (g["g_ple"],) = _rowwise_bwd(n + "norm_ple_b", _f_norm, [r["h2"]], [w["g_ple"]], [dhn], 1, adds={0: dh2a})
    dact = _mm(n + "down_dx", dh2, w["w_down"], "nt")
    g["w_down"] = _mm(n + "down_dw", r["act"], dh2, "tn", bf16)
    dup_g, dup_v, dcw_g, dcw_v, dcb_g, dcb_v = _ffn_act_bwd(n + "ffn_act_b", r["up"], dact, w["ffn_conv_w"], w["ffn_conv_b"])
    g["ffn_conv_w"] = jnp.concatenate([dcw_g, dcw_v], axis=1)
    g["ffn_conv_b"] = jnp.concatenate([dcb_g, dcb_v], axis=1)
    dxn2 = _mm(n + "up_dx_v", dup_v, w["w_up_v"], "nt", res=_mm(n + "up_dx_g", dup_g, w["w_up_g"], "nt"))
    g["w_up"] = jnp.concatenate([_mm(n + "up_dw_g", r["xn2"], dup_g, "tn", bf16),
                                 _mm(n + "up_dw_v", r["xn2"], dup_v, "tn", bf16)], axis=1)
    (dh1,), (g["g_ffn"],) = _rowwise_bwd(n + "norm_ffn_b", _f_norm, [r["h1"]], [w["g_ffn"]], [dxn2], 1, adds={0: dh2})
    docat = _mm(n + "out_dx", dh1, w["w_o"], "nt")
    g["w_o"] = _mm(n + "out_dw", r["ocat"], dh1, "tn", bf16)
    token = grads_to("ffn", dict(w_o=_take_inv(g["w_o"], OMIX_MAP, 0), w_up=g["w_up"], ffn_conv_w=g["ffn_conv_w"],
                                 w_down=g["w_down"], w_ple_gate=g["w_ple_gate"], w_ple_proj=g["w_ple_proj"]))
    w = dict(w, g_out=w["g_out"] + token)
    (do_mla, do_fox, dhs, dlg), (g["g_out"],) = _rowwise_bwd(
        n + "merge_b", _f_merge, [r["o_mla"], r["o_fox"], r["hs"], zlg], [w["g_out"]], [docat], 4)
    a, hs = r["a"], r["hs"]
    a_next = jnp.concatenate([a[1:], jnp.zeros((1, LRU_WIDTH), f32)], axis=0)
    h_prev = jnp.concatenate([jnp.zeros((1, LRU_WIDTH), f32), hs[:-1]], axis=0)
    da, dbx = _scan_bwd(n + "lru_scan_b", a_next, h_prev, dhs)
    (dgates, dxc_a), (g["b_r"], g["b_i"], g["lam"]) = _rowwise_bwd(
        n + "lru_gate_b", _f_lru_gate, [r["gates"], r["xc"]], [w["b_r"], w["b_i"], w["lam"]], [da, dbx], 2,
        dts=[bf16, f32])
    dxc_b = _mm(n + "lru_gates_dx", dgates, w["w_ri"], "nt")
    g["w_ri"] = _mm(n + "lru_gates_dw", r["xc"], dgates, "tn")
    dlx, g["lru_conv_w"], g["lru_conv_b"] = _conv_bwd(n + "lru_conv_b", zlx, dxc_a, w["lru_conv_w"], LRU_CONV, dout2=dxc_b)
    fox_scale = FOX_HEAD_DIM ** -0.5
    fq, fk, fv = (z, Z_FQ // LANE), (z, Z_FK // LANE), (z, Z_FV // LANE)
    dfq, delta_f, dc_q = _attn_dq(n + "fox_dq", fq, fk, fv, r["o_fox"], do_fox, r["lse_f"], fox_scale, r["c_row"])
    dfk, dfv, dc_k = _attn_dkv(n + "fox_dkv", fq, fk, fv, do_fox, r["lse_f_row"], delta_f, fox_scale,
                               r["c_col"])
    pad_rows = jnp.zeros((SUBLANE - HEADS, s_len), f32)
    dfl_t, g["b_f8"] = _decay_bwd(n + "decay_b", r["fl_t"], w["b_f8"],
                                  jnp.concatenate([dc_k.reshape(HEADS, s_len), pad_rows], axis=0),
                                  jnp.concatenate([dc_q.reshape(HEADS, s_len), pad_rows], axis=0))
    dfl = jnp.pad(dfl_t.T, ((0, 0), (0, LANE - SUBLANE)))
    mla_scale = (MLA_NOPE + MLA_ROPE) ** -0.5
    qr, kk, kv = (r["qr"], 0), (r["kk"], 0), (r["kv"], HEADS)
    dqr, delta_m, _ = _attn_dq(n + "mla_dq", qr, kk, kv, r["o_mla"], do_mla, r["lse_m"], mla_scale)
    dkk, dv_m = _attn_dkv(n + "mla_dkv", qr, kk, kv, do_mla, r["lse_m_row"], delta_m, mla_scale)
    (dq, dkpart, dkr), _ = _rowwise_bwd(n + "mla_prep_b", _f_mla_prep, [r["q"], (r["kv"], HEADS * LANE, 0), zkr, *rope],
                                        [], [dqr, dkk], 3, dts=[bf16, bf16, f32])
    dkv = jnp.concatenate([dkpart, dv_m.astype(bf16)], axis=1)
    dkvn = _mm(n + "ukv_dx", dkv, w["w_ukv"], "nt")
    g["w_ukv"] = _mm(n + "ukv_dw", r["kvn"], dkv, "tn", bf16)
    dqcn = _mm(n + "uq_dx", dq, w["w_uq"], "nt")
    g["w_uq"] = _mm(n + "uq_dw", r["qcn"], dq, "tn", bf16)
    (dqc, dkvc), (g["g_qc"], g["g_kvc"]) = _rowwise_bwd(n + "latent_norm_b", _f_latent, [zq, zkv],
                                                        [w["g_qc"], w["g_kvc"]], [dqcn, dkvn], 2)
    dz = jnp.concatenate([t.astype(bf16) for t in (dfq, dfk, dfv, dlx, dlg, dqc, dkvc, dkr, dfl)], axis=1)
    dxn = _mm(n + "in_dx", dz, w["w_in"], "nt")
    g["w_in"] = _mm(n + "in_dw", r["xn"], dz, "tn", bf16)
    (dh0,), (g["g_mix"],) = _rowwise_bwd(n + "norm_mix_b", _f_norm, [r["h0"]], [w["g_mix"]], [dxn], 1, adds={0: dh1})
    return dh0, grads_to("mix", _unpad_mix_grads(g))


def _unpad_mix_grads(g):
    d_ri = g["w_ri"]
    idx = jnp.arange(LRU_BLOCKS)

    def diag_blocks(m):
        return m.reshape(LRU_BLOCKS, LRU_BLOCK, LRU_BLOCKS, LRU_BLOCK)[idx, :, idx, :]

    return dict(
        g_mix=g["g_mix"][0], w_in=_take_inv(g["w_in"], Z_MAP, 1), g_qc=g["g_qc"][0, :MLA_Q_RANK],
        w_uq=_take_inv(g["w_uq"][:MLA_Q_RANK], UQ_COL_MAP, 1), g_kvc=g["g_kvc"][0],
        w_ukv=_take_inv(g["w_ukv"], UKV_MAP, 1), b_f=g["b_f8"][:FOX_HEADS, 0],
        lru_conv_w=g["lru_conv_w"], lru_conv_b=g["lru_conv_b"][0],
        w_r=diag_blocks(d_ri[:, :LRU_WIDTH]), b_r=g["b_r"][0], w_i=diag_blocks(d_ri[:, LRU_WIDTH:]), b_i=g["b_i"][0],
        lru_lambda=g["lam"][0], g_out=_take_inv(g["g_out"][0], OMIX_MAP, 0),
        g_ffn=g["g_ffn"][0], ffn_conv_b=g["ffn_conv_b"][0], g_ple=g["g_ple"][0],
    )


LAYER_WEIGHTS = ["g_mix", "w_in", "g_qc", "w_uq", "g_kvc", "w_ukv", "b_f", "lru_conv_w", "lru_conv_b", "w_r", "b_r", "w_i",
                 "b_i", "lru_lambda", "g_out", "w_o", "g_ffn", "w_up", "ffn_conv_w", "ffn_conv_b", "w_down", "g_ple",
                 "w_ple_gate", "w_ple_proj"]
WEIGHTS = LAYER_WEIGHTS + ["g_final"]


def _local_step(x, p, pos, target, g_final, weights_of, grads_to):
    h = x
    rope = _rope_rows(pos)
    ws, saved = [], []
    for l in range(DEPTH):
        h, r, w = _layer_fwd(l, h, p[l], rope, functools.partial(weights_of, l))
        ws.append(w)
        saved.append(r)
    loss, dh, dg_final = _loss_head("loss_head", h, target, g_final.reshape(1, -1))
    token = jnp.zeros((), f32)
    for l in reversed(range(DEPTH)):
        dh, token = _layer_bwd(l, dh, saved[l], rope, ws[l], token, functools.partial(grads_to, l))
    return loss[0, 0], dh, dg_final[0]


MESH_AXES = ("x", "y", "c")


def _row_tile(rows, cap):
    if rows <= cap:
        return rows
    for t in range(cap, SUBLANE - 1, -SUBLANE):
        if rows % t == 0:
            return t
    return rows


ADAM_BLOCK_BYTES = 2 ** 21


def _adamw(name, w, g, m, v):
    rows, cols = w.shape
    tr = _row_tile(rows, max(SUBLANE, ADAM_BLOCK_BYTES // (4 * cols) // SUBLANE * SUBLANE))

    def kern(w_ref, g_ref, m_ref, v_ref, d_ref, nm_ref, nv_ref):
        gv = g_ref[...]
        nm = ADAM_B1 * m_ref[...] + (1.0 - ADAM_B1) * gv
        nv = ADAM_B2 * v_ref[...] + (1.0 - ADAM_B2) * (gv * gv)
        m_hat = nm / (1.0 - ADAM_B1 ** ADAM_STEP)
        v_hat = nv / (1.0 - ADAM_B2 ** ADAM_STEP)
        d_ref[...] = -ADAM_LR * (m_hat / (jnp.sqrt(v_hat) + ADAM_EPS) + ADAM_WD * w_ref[...])
        nm_ref[...] = nm
        nv_ref[...] = nv

    spec = pl.BlockSpec((tr, cols), lambda i: (i, 0))
    return pl.pallas_call(
        kern, name=name, grid=(rows // tr,), in_specs=[spec] * 4, out_specs=[spec] * 3,
        out_shape=[jax.ShapeDtypeStruct((rows, cols), f32)] * 3,
        compiler_params=pltpu.CompilerParams(dimension_semantics=("parallel",)))(w, g, m, v)


def _packed_rows(shape):
    return -(-int(np.prod(shape)) // (SUBLANE * LANE)) * SUBLANE


def _pack(arrays):
    rows = []
    for a in arrays:
        flat = a.reshape(-1)
        rows.append(jnp.pad(flat, (0, _packed_rows(a.shape) * LANE - flat.shape[0])).reshape(-1, LANE))
    return jnp.concatenate(rows, axis=0)


def _unpack(buf, shapes):
    out, at = [], 0
    for s in shapes:
        rows = _packed_rows(s)
        out.append(buf[at:at + rows].reshape(-1)[:int(np.prod(s))].reshape(s))
        at += rows
    return out


SHARD_AXIS = {"w_in": 2, "w_uq": 2, "w_ukv": 2, "lru_conv_w": 2, "w_o": 1, "w_up": 2, "ffn_conv_w": 2, "w_down": 1,
              "w_ple_gate": 1, "w_ple_proj": 2}
SHARDED = [k for k in WEIGHTS if k in SHARD_AXIS]
REPLICATED = [k for k in WEIGHTS if k not in SHARD_AXIS]
ELEMENTWISE_F32 = ("lru_conv_w", "ffn_conv_w")
N_SHARDS = 4
BF16_TILE_ROWS = 16


HBM_SPEC = pl.BlockSpec(memory_space=pl.ANY)
SEM_SPEC = pl.BlockSpec(memory_space=pltpu.SEMAPHORE)
SPLIT_EFFECT = pltpu.SideEffectType.DATAFLOW_SIDE_EFFECTING
CHIP_FLIPS = ((1, 0), (0, 1), (1, 1))
N_DEVICES = 8
SUM_BLOCK_BYTES = 4 * 2 ** 20


def _device_index():
    return 4 * lax.axis_index("x") + 2 * lax.axis_index("y") + lax.axis_index("c")


def _when(cond, fn):
    if cond is None:
        fn()
    else:
        pl.when(cond)(fn)


class _Exchange:
    def __init__(self, name, plan, srcs, land_shapes, n_send, n_recv):
        self.name, self.plan, self.srcs, self.n = name, plan, list(srcs), len(srcs)
        self.land_shapes, self.n_send, self.n_recv = land_shapes, n_send, n_recv

    def run(self):
        n = self.n

        def body(*refs):
            sends, arrivals = self.plan(refs[:n], refs[n:2 * n], refs[2 * n], refs[2 * n + 1])
            for cond, cp in sends:
                _when(cond, cp.start)
            for cond, cp in arrivals:
                _when(cond, cp.wait_recv)
            for cond, cp in sends:
                _when(cond, cp.wait_send)

        return pl.pallas_call(
            body, name=self.name, out_shape=self.land_shapes, in_specs=[HBM_SPEC] * n, out_specs=[HBM_SPEC] * n,
            scratch_shapes=[pltpu.SemaphoreType.DMA((self.n_send,)), pltpu.SemaphoreType.DMA((self.n_recv,))])(*self.srcs)

    def start(self, after=None):
        n = self.n
        lands = [lax.empty(s.shape, s.dtype) for s in self.land_shapes]
        extra = [] if after is None else [after]

        def body(*refs):
            ins, lands_in = refs[:n], refs[n:2 * n]
            send_sems, recv_sems, token = refs[2 * n + len(extra)], refs[2 * n + len(extra) + 1], refs[-1]
            sends, _ = self.plan(ins, lands_in, send_sems, recv_sems)
            for cond, cp in sends:
                _when(cond, cp.start)
            token[...] = jnp.zeros_like(token)

        hbm = [pltpu.with_memory_space_constraint(a, pltpu.HBM) for a in self.srcs + lands]
        res = pl.pallas_call(
            body, name=self.name + "_start",
            out_shape=(pltpu.SemaphoreType.DMA((self.n_send,)), pltpu.SemaphoreType.DMA((self.n_recv,)),
                       *[pltpu.HBM(a.shape, a.dtype) for a in hbm], jax.ShapeDtypeStruct((SUBLANE, LANE), f32)),
            in_specs=[HBM_SPEC] * (2 * n + len(extra)),
            out_specs=(SEM_SPEC, SEM_SPEC, *[HBM_SPEC] * (2 * n), pl.BlockSpec(memory_space=pltpu.VMEM)),
            input_output_aliases={i: 2 + i for i in range(2 * n)},
            compiler_params=pltpu.CompilerParams(has_side_effects=SPLIT_EFFECT))(*hbm, *extra)
        self.sems, self.thru, token = res[:2], res[2:2 + 2 * n], res[-1]
        return token[0, 0]

    def finish(self, after):
        n = self.n

        def body(*refs):
            ins, lands_in, send_sems, recv_sems = refs[:n], refs[n:2 * n], refs[2 * n], refs[2 * n + 1]
            sends, arrivals = self.plan(ins, lands_in, send_sems, recv_sems)
            for cond, cp in arrivals:
                _when(cond, cp.wait_recv)
            for cond, cp in sends:
                _when(cond, cp.wait_send)

        res = pl.pallas_call(
            body, name=self.name + "_finish", out_shape=tuple(pltpu.HBM(a.shape, a.dtype) for a in self.thru),
            in_specs=[HBM_SPEC] * (2 * n) + [SEM_SPEC, SEM_SPEC, HBM_SPEC], out_specs=tuple([HBM_SPEC] * (2 * n)),
            input_output_aliases={i: i for i in range(2 * n)},
            compiler_params=pltpu.CompilerParams(has_side_effects=SPLIT_EFFECT))(*self.thru, *self.sems, after)
        return list(res[n:])


def _gather_exchange(name, shards):
    def plan(ins, lands, send_sems, recv_sems):
        x, y, c = (lax.axis_index(a) for a in MESH_AXES)
        copies = []
        for i in range(len(ins)):
            for k, (fx, fy) in enumerate(CHIP_FLIPS):
                peer = (1 - x if fx else x, 1 - y if fy else y, c)
                copies.append((None, pltpu.make_async_remote_copy(
                    src_ref=ins[i], dst_ref=lands[i].at[2 * x + y], send_sem=send_sems.at[3 * i + k],
                    recv_sem=recv_sems.at[3 * i + k], device_id=peer, device_id_type=pl.DeviceIdType.MESH)))
        return copies, copies

    n = len(shards)
    return _Exchange(name, plan, shards, [jax.ShapeDtypeStruct((N_SHARDS,) + s.shape, s.dtype) for s in shards], 3 * n, 3 * n)


def _scatter_exchange(name, layer, chunks):
    def plan(ins, lands, send_sems, recv_sems):
        x, y, c = (lax.axis_index(a) for a in MESH_AXES)
        me = _device_index()
        sends, arrivals = [], []
        for i in range(len(ins)):
            for j in range(N_SHARDS):
                target = (j // 2, j % 2, layer)
                remote = jnp.logical_not((x == target[0]) & (y == target[1]) & (c == layer))
                sends.append((remote, pltpu.make_async_remote_copy(
                    src_ref=ins[i].at[j], dst_ref=lands[i].at[me], send_sem=send_sems.at[N_SHARDS * i + j],
                    recv_sem=recv_sems.at[N_DEVICES * i + me], device_id=target, device_id_type=pl.DeviceIdType.MESH)))
            for s in range(N_DEVICES):
                arrivals.append(((c == layer) & (me != s), pltpu.make_async_remote_copy(
                    src_ref=ins[i].at[0], dst_ref=lands[i].at[s], send_sem=send_sems.at[0],
                    recv_sem=recv_sems.at[N_DEVICES * i + s], device_id=(x, y, c), device_id_type=pl.DeviceIdType.MESH)))
        return sends, arrivals

    n = len(chunks)
    lands = [jax.ShapeDtypeStruct((N_DEVICES,) + ch.shape[1:], ch.dtype) for ch in chunks]
    return _Exchange(name, plan, chunks, lands, N_SHARDS * n, N_DEVICES * n)


def _sum_contributions(name, got, mine):
    _, a, b = got.shape
    ta = _row_tile(a, max(SUBLANE, SUM_BLOCK_BYTES // (N_DEVICES * b * got.dtype.itemsize) // SUBLANE * SUBLANE))

    def kern(got_ref, mine_ref, o_ref):
        me = _device_index()
        acc = jnp.zeros(o_ref.shape, f32)
        for s in range(N_DEVICES):
            acc = acc + jnp.where(me == s, mine_ref[...].astype(f32), got_ref[s].astype(f32))
        o_ref[...] = acc

    return pl.pallas_call(
        kern, name=name, grid=(a // ta,),
        in_specs=[pl.BlockSpec((N_DEVICES, ta, b), lambda i: (0, i, 0)), pl.BlockSpec((ta, b), lambda i: (i, 0))],
        out_specs=pl.BlockSpec((ta, b), lambda i: (i, 0)), out_shape=jax.ShapeDtypeStruct((a, b), f32),
        compiler_params=pltpu.CompilerParams(dimension_semantics=("parallel",)))(got, mine)


def _swap_layers(name, sums):
    n = len(sums[0])

    def body(*refs):
        srcs = (refs[:n], refs[n:2 * n])
        outs, (send_sems, recv_sems) = refs[2 * n:3 * n], refs[3 * n:]
        x, y, c = (lax.axis_index(a) for a in MESH_AXES)
        for i in range(n):
            for layer in range(DEPTH):
                cp = pltpu.make_async_remote_copy(
                    src_ref=srcs[layer][i], dst_ref=outs[i], send_sem=send_sems.at[i], recv_sem=recv_sems.at[i],
                    device_id=(x, y, 1 - c), device_id_type=pl.DeviceIdType.MESH)
                pl.when(c == layer)(cp.start)
        for i in range(n):
            pltpu.make_async_remote_copy(
                src_ref=srcs[0][i], dst_ref=outs[i], send_sem=send_sems.at[i], recv_sem=recv_sems.at[i],
                device_id=(x, y, 1 - c), device_id_type=pl.DeviceIdType.MESH).wait()

    return pl.pallas_call(
        body, name=name, out_shape=[jax.ShapeDtypeStruct(s.shape, s.dtype) for s in sums[0]],
        in_specs=[HBM_SPEC] * (2 * n), out_specs=[HBM_SPEC] * n,
        scratch_shapes=[pltpu.SemaphoreType.DMA((n,)), pltpu.SemaphoreType.DMA((n,))])(*sums[0], *sums[1])


def _stack_shards(g, axis):
    if axis == 1:
        return g.reshape(N_SHARDS, g.shape[0] // N_SHARDS, g.shape[1])
    return g.reshape(g.shape[0], N_SHARDS, g.shape[1] // N_SHARDS).transpose(1, 0, 2)


def _join_shards(s, axis):
    if axis == 1:
        return s.reshape(-1, s.shape[2])
    return s.transpose(1, 0, 2).reshape(s.shape[1], -1)


def _layer_shards(w, l, names):
    return [w[k][l] if k in ELEMENTWISE_F32 else w[k][l].astype(bf16) for k in names]


def _full_weights(names, sent, got):
    j = 2 * lax.axis_index("x") + lax.axis_index("y")
    return {k: _join_shards(lax.dynamic_update_slice(g, own[None], (j, 0, 0)), SHARD_AXIS[k])
            for k, own, g in zip(names, sent, got)}


def _grad_chunks(grads, names):
    return [_stack_shards(grads[k], SHARD_AXIS[k]).astype(bf16) for k in names]


def _sum_group(l, names, got, chunks):
    j = 2 * lax.axis_index("x") + lax.axis_index("y")
    return {k: _sum_contributions(f"sum_l{l}_{k}", g, lax.dynamic_index_in_dim(ch, j, 0, keepdims=False))
            for k, g, ch in zip(names, got, chunks)}


def _both_layers(name, names, sums):
    c = lax.axis_index("c")
    mine = [[sums[l][k] for k in names] for l in range(DEPTH)]
    other = _swap_layers(name, mine)
    return {k: jnp.stack([jnp.where(c == 0, mine[0][i], other[i]), jnp.where(c == 0, other[i], mine[1][i])])
            for i, k in enumerate(names)}


def _gather_all_exchange(name, src):
    def plan(ins, lands, send_sems, recv_sems):
        coords = [lax.axis_index(a) for a in MESH_AXES]
        me = _device_index()
        sends, arrivals = [], []
        for f in range(1, N_DEVICES):
            peer = tuple(1 - cd if (f >> (2 - b)) & 1 else cd for b, cd in enumerate(coords))
            sends.append((None, pltpu.make_async_remote_copy(
                src_ref=ins[0], dst_ref=lands[0].at[me], send_sem=send_sems.at[f - 1], recv_sem=recv_sems.at[me],
                device_id=peer, device_id_type=pl.DeviceIdType.MESH)))
        for s in range(N_DEVICES):
            arrivals.append((me != s, pltpu.make_async_remote_copy(
                src_ref=ins[0], dst_ref=lands[0].at[s], send_sem=send_sems.at[0], recv_sem=recv_sems.at[s],
                device_id=tuple(coords), device_id_type=pl.DeviceIdType.MESH)))
        return sends, arrivals

    return _Exchange(name, plan, [src], [jax.ShapeDtypeStruct((N_DEVICES,) + src.shape, src.dtype)], N_DEVICES - 1, N_DEVICES)


def kernel(x, p, positions, g_mix, w_in, g_qc, w_uq, g_kvc, w_ukv, b_f, lru_conv_w, lru_conv_b, w_r, b_r, w_i, b_i, lru_lambda, g_out, w_o, g_ffn, w_up, ffn_conv_w, ffn_conv_b, w_down, g_ple, w_ple_gate, w_ple_proj, g_final, loss_target, m_g_mix, m_w_in, m_g_qc, m_w_uq, m_g_kvc, m_w_ukv, m_b_f, m_lru_conv_w, m_lru_conv_b, m_w_r, m_b_r, m_w_i, m_b_i, m_lru_lambda, m_g_out, m_w_o, m_g_ffn, m_w_up, m_ffn_conv_w, m_ffn_conv_b, m_w_down, m_g_ple, m_w_ple_gate, m_w_ple_proj, m_g_final, v_g_mix, v_w_in, v_g_qc, v_w_uq, v_g_kvc, v_w_ukv, v_b_f, v_lru_conv_w, v_lru_conv_b, v_w_r, v_b_r, v_w_i, v_b_i, v_lru_lambda, v_g_out, v_w_o, v_g_ffn, v_w_up, v_ffn_conv_w, v_ffn_conv_b, v_w_down, v_g_ple, v_w_ple_gate, v_w_ple_proj, v_g_final):
    given = locals()
    w = {k: given[k] for k in WEIGHTS}
    m = {k: given["m_" + k] for k in WEIGHTS}
    v = {k: given["v_" + k] for k in WEIGHTS}

    parts = {"mix": MIX_PART, "ffn": FFN_PART}
    groups = [(l, part) for l in range(DEPTH) for part in ("mix", "ffn")]
    sent = {g: _layer_shards(w, g[0], parts[g[1]]) for g in groups}
    first = _gather_exchange("gather_l0_mix", sent[groups[0]]).run()
    ahead = {g: _gather_exchange(f"gather_l{g[0]}_{g[1]}", sent[g]) for g in groups[1:]}
    pos = positions[0].astype(f32).reshape(-1, 1) + ahead[groups[1]].start(after=first[0])
    behind, layer_grads, chunks = {}, [{} for _ in range(DEPTH)], {}

    def weights_of(l, part, after):
        g = (l, part)
        got = first if g == groups[0] else ahead[g].finish(after=after)
        full = _full_weights(parts[part], sent[g], got)
        if part == "mix":
            full.update({k: w[k][l] for k in LAYER_WEIGHTS if k in REPLICATED})
        if g == groups[1]:
            for later in groups[2:]:
                full["ffn_conv_w"] = full["ffn_conv_w"] + ahead[later].start(after=got[0])
        return full

    def grads_to(l, part, grads):
        g = (l, part)
        layer_grads[l].update(grads)
        chunks[g] = _grad_chunks(grads, parts[part])
        if g == groups[0]:
            return jnp.zeros((), f32)
        behind[g] = _scatter_exchange(f"scatter_l{l}_{part}", l, chunks[g])
        return behind[g].start()

    loss, dx, dg_final = _local_step(x[0], p[:, 0], pos, loss_target[0], w["g_final"], weights_of, grads_to)

    grads = {k: jnp.stack([layer_grads[l][k] for l in range(DEPTH)]) for k in LAYER_WEIGHTS if k in REPLICATED}
    grads["g_final"] = dg_final
    rep_shapes = [w[k].shape for k in REPLICATED] + [(1,)]
    contrib = _pack([grads[k] for k in REPLICATED] + [loss.reshape(1)])
    last = _scatter_exchange("scatter_l0_mix", 0, chunks[groups[0]])
    everyone = _gather_all_exchange("gather_replicated", contrib)
    started = (last.start() + everyone.start() + dx[0, 0]).reshape(1, 1)

    def adamw_of(names, g_sharded):
        out = {}
        for k in names:
            shape = w[k].shape
            flat = [t.reshape(-1, shape[-1]) for t in (w[k], g_sharded[k], m[k], v[k])]
            out[k] = [t.reshape(shape) for t in (flat[1],) + tuple(_adamw("adamw_" + k, *flat))]
        return out

    sums = [{} for _ in range(DEPTH)]
    for g in groups[1:]:
        sums[g[0]].update(_sum_group(g[0], parts[g[1]], behind[g].finish(after=started), chunks[g]))
    big = adamw_of(FFN_PART, _both_layers("swap_ffn", FFN_PART, sums))
    sums[0].update(_sum_group(0, MIX_PART, last.finish(after=big[FFN_PART[0]][1]), chunks[groups[0]]))
    big.update(adamw_of(MIX_PART, _both_layers("swap_mix", MIX_PART, sums)))

    g_rep = _sum_contributions("sum_replicated", everyone.finish(after=big[MIX_PART[0]][1])[0], contrib)
    zero = jnp.zeros((1,), f32)
    w_rep, m_rep, v_rep = (_pack([t[k] for k in REPLICATED] + [zero]) for t in (w, m, v))
    rep = [_unpack(b, rep_shapes) for b in (g_rep,) + tuple(_adamw("adamw_replicated", w_rep, g_rep, m_rep, v_rep))]

    outs = []
    for kind in range(4):
        by_name = {k: big[k][kind] for k in SHARDED}
        by_name.update(zip(REPLICATED, rep[kind][:-1]))
        outs.append([by_name[k] for k in WEIGHTS])
    total_loss = rep[0][-1][0]
    return (total_loss, dx.reshape(x.shape), *outs[0], *outs[1], *outs[2], *outs[3])
```

```python
import functools
import math

import numpy as np
import jax
import jax.numpy as jnp
from jax import lax
from jax.experimental import pallas as pl
from jax.experimental.pallas import tpu as pltpu

f32, bf16 = jnp.float32, jnp.bfloat16

D_MODEL = 1024
PLE_DIM = 256
MLA_HEADS, MLA_NOPE, MLA_ROPE, MLA_V = 4, 64, 32, 64
MLA_Q_RANK, MLA_KV_RANK = 192, 128
FOX_HEADS, FOX_HEAD_DIM = 4, 64
LRU_WIDTH, LRU_BLOCKS, LRU_BLOCK, LRU_CONV, LRU_C = 512, 8, 64, 4, 8.0
D_FF, FFN_CONV = 2816, 3
ROPE_THETA = 10000.0
EPS = 1e-6
DEPTH = 2
ADAM_LR, ADAM_B1, ADAM_B2, ADAM_EPS, ADAM_WD, ADAM_STEP = 0.001, 0.9, 0.999, 1e-08, 0.01, 10

LANE = 128
SUBLANE = 8
HEADS = 4

Z_FQ, Z_FK, Z_FV, Z_LX, Z_LG, Z_QC, Z_KVC, Z_KR, Z_FL, Z_W = 0, 512, 1024, 1536, 2048, 2560, 2816, 2944, 3072, 3200
QC_W = 256
ROPE_AT = 64


def _head_pad_map(n_heads, width):
    m = -np.ones(n_heads * LANE, np.int64)
    for h in range(n_heads):
        m[h * LANE:h * LANE + width] = h * width + np.arange(width)
    return m


def _z_map():
    m = -np.ones(Z_W, np.int64)
    o_qc, o_kvc, o_kr = 0, MLA_Q_RANK, MLA_Q_RANK + MLA_KV_RANK
    o_fq = o_kr + MLA_ROPE
    o_fk, o_fv = o_fq + 256, o_fq + 512
    o_fl = o_fv + 256
    o_lx = o_fl + FOX_HEADS
    o_lg = o_lx + LRU_WIDTH
    m[Z_FQ:Z_FQ + 512] = np.where(_head_pad_map(4, 64) >= 0, _head_pad_map(4, 64) + o_fq, -1)
    m[Z_FK:Z_FK + 512] = np.where(_head_pad_map(4, 64) >= 0, _head_pad_map(4, 64) + o_fk, -1)
    m[Z_FV:Z_FV + 512] = np.where(_head_pad_map(4, 64) >= 0, _head_pad_map(4, 64) + o_fv, -1)
    m[Z_LX:Z_LX + 512] = o_lx + np.arange(512)
    m[Z_LG:Z_LG + 512] = o_lg + np.arange(512)
    m[Z_QC:Z_QC + MLA_Q_RANK] = o_qc + np.arange(MLA_Q_RANK)
    m[Z_KVC:Z_KVC + MLA_KV_RANK] = o_kvc + np.arange(MLA_KV_RANK)
    m[Z_KR + ROPE_AT:Z_KR + ROPE_AT + MLA_ROPE] = o_kr + np.arange(MLA_ROPE)
    m[Z_FL:Z_FL + FOX_HEADS] = o_fl + np.arange(FOX_HEADS)
    return m


def _ukv_map():
    m = -np.ones(2 * HEADS * LANE, np.int64)
    for h in range(HEADS):
        m[h * LANE:h * LANE + MLA_NOPE] = h * (MLA_NOPE + MLA_V) + np.arange(MLA_NOPE)
        m[HEADS * LANE + h * LANE:HEADS * LANE + h * LANE + MLA_V] = h * (MLA_NOPE + MLA_V) + MLA_NOPE + np.arange(MLA_V)
    return m


def _omix_map():
    return np.concatenate([_head_pad_map(4, 64), np.where(_head_pad_map(4, 64) >= 0, _head_pad_map(4, 64) + 256, -1),
                           512 + np.arange(512)])


def _pad_to(m, n):
    return np.concatenate([m, -np.ones(n - m.shape[0], np.int64)])


def _runs(m):
    out, at = [], 0
    while at < len(m):
        end = at + 1
        while end < len(m) and (m[end] == m[end - 1] + 1 if m[at] >= 0 else m[end] < 0):
            end += 1
        out.append((int(m[at]), end - at))
        at = end
    return out


def _take_runs(a, m, axis):
    parts = []
    for start, size in _runs(m):
        if start < 0:
            shape = list(a.shape)
            shape[axis] = size
            parts.append(jnp.zeros(shape, a.dtype))
        else:
            parts.append(lax.slice_in_dim(a, start, start + size, axis=axis))
    return parts[0] if len(parts) == 1 else jnp.concatenate(parts, axis=axis)


def _take_pad(a, m, axis):
    return _take_runs(a, m, axis)


def _take_inv(a, m, axis):
    n = int(m.max()) + 1
    inv = np.zeros(n, np.int64)
    inv[m[m >= 0]] = np.nonzero(m >= 0)[0]
    return _take_runs(a, inv, axis)


Z_MAP = _z_map()
UQ_COL_MAP = _head_pad_map(HEADS, MLA_NOPE + MLA_ROPE)
UQ_ROW_MAP = _pad_to(np.arange(MLA_Q_RANK), QC_W)
UKV_MAP = _ukv_map()
OMIX_MAP = _omix_map()
OMIX_W = 1536


def _rope_tables(width, at):
    half = MLA_ROPE // 2
    inv = ROPE_THETA ** (-np.arange(half, dtype=np.float32) / half)
    freq = np.zeros((1, width), np.float32)
    m1 = np.zeros((1, width), np.float32)
    m2 = np.zeros((1, width), np.float32)
    for h in range(width // LANE):
        b = h * LANE + at
        freq[0, b:b + half] = inv
        freq[0, b + half:b + 2 * half] = inv
        m1[0, b:b + half] = 1.0
        m2[0, b + half:b + 2 * half] = 1.0
    return freq, m1, m2


def _view(r):
    return r if isinstance(r, tuple) else (r, r.shape[1], 0)


def _blk(dim, cap):
    if dim <= cap:
        return dim
    for b in range(cap, LANE - 1, -LANE):
        if dim % b == 0:
            return b
    return dim


@functools.partial(jax.custom_vjp, nondiff_argnums=(1, 2))
def _roll(x, shift, axis):
    return pltpu.roll(x, shift, axis)


def _roll_fwd(x, shift, axis):
    return pltpu.roll(x, shift, axis), None


def _roll_bwd(shift, axis, _, g):
    return (pltpu.roll(g, g.shape[axis] - shift, axis),)


_roll.defvjp(_roll_fwd, _roll_bwd)


ROW_VMEM_BUDGET = 20 * 2 ** 20
ROW_TILES = (1024, 512, 256)


def _row_block(n, bytes_per_row):
    for tb in ROW_TILES:
        if n % tb == 0 and 2 * tb * bytes_per_row <= ROW_VMEM_BUDGET:
            return tb
    return min(ROW_TILES[-1], n)


def _rowwise(name, fn, rows, pars, outs):
    rows = [_view(r) for r in rows]
    n = rows[0][0].shape[0]
    tb = _row_block(n, sum(w * a.dtype.itemsize for a, w, _ in rows) + sum(w * jnp.dtype(dt).itemsize for w, dt in outs))
    nr, npar = len(rows), len(pars)

    def kern(*refs):
        r = [refs[k][...].astype(f32) for k in range(nr)]
        p = [refs[nr + k][...] for k in range(npar)]
        res = fn(*r, *p)
        for o_ref, o in zip(refs[nr + npar:], res):
            o_ref[...] = o.astype(o_ref.dtype)

    in_specs = [pl.BlockSpec((tb, w), lambda i, j=idx: (i, j)) for (_, w, idx) in rows]
    in_specs += [pl.BlockSpec(p.shape, lambda i: (0, 0)) for p in pars]
    out_specs = [pl.BlockSpec((tb, w), lambda i: (i, 0)) for (w, _) in outs]
    out_shape = [jax.ShapeDtypeStruct((n, w), dt) for (w, dt) in outs]
    return pl.pallas_call(kern, name=name, grid=(n // tb,), in_specs=in_specs, out_specs=out_specs, out_shape=out_shape,
                          compiler_params=pltpu.CompilerParams(dimension_semantics=("parallel",)))(*[r[0] for r in rows], *pars)


def _rowwise_bwd(name, fn, rows, pars, cts, ndiff, adds=None, dts=None):
    rows = [_view(r) for r in rows]
    dts = dts or [f32] * ndiff
    adds = adds or {}
    add_keys = sorted(adds)
    n = rows[0][0].shape[0]
    tb = _row_block(n, sum(w * a.dtype.itemsize for a, w, _ in rows) + sum(c.shape[1] * c.dtype.itemsize for c in cts)
                    + sum(a.shape[1] * a.dtype.itemsize for a in adds.values())
                    + sum(rows[k][1] * jnp.dtype(dts[k]).itemsize for k in range(ndiff)))
    nr, npar, nct, nadd = len(rows), len(pars), len(cts), len(add_keys)

    def kern(*refs):
        i = pl.program_id(0)
        r = [refs[k][...].astype(f32) for k in range(nr)]
        p = [refs[nr + k][...] for k in range(npar)]
        ct = [refs[nr + npar + k][...].astype(f32) for k in range(nct)]
        ad = {key: refs[nr + npar + nct + k][...] for k, key in enumerate(add_keys)}
        o_refs = refs[nr + npar + nct + nadd:]

        def g(*d):
            return tuple(fn(*d[:ndiff], *r[ndiff:], *d[ndiff:]))

        _, vjp = jax.vjp(g, *r[:ndiff], *p)
        grads = vjp(tuple(ct))
        for k in range(ndiff):
            gk = grads[k]
            if k in ad:
                gk = gk + ad[k]
            o_refs[k][...] = gk.astype(o_refs[k].dtype)

        @pl.when(i == 0)
        def _():
            for k in range(npar):
                o_refs[ndiff + k][...] = jnp.zeros_like(o_refs[ndiff + k])

        for k in range(npar):
            o_refs[ndiff + k][...] += grads[ndiff + k]

    in_specs = [pl.BlockSpec((tb, w), lambda i, j=idx: (i, j)) for (_, w, idx) in rows]
    in_specs += [pl.BlockSpec(p.shape, lambda i: (0, 0)) for p in pars]
    in_specs += [pl.BlockSpec((tb, c.shape[1]), lambda i: (i, 0)) for c in cts]
    in_specs += [pl.BlockSpec((tb, adds[k].shape[1]), lambda i: (i, 0)) for k in add_keys]
    out_specs = [pl.BlockSpec((tb, rows[k][1]), lambda i: (i, 0)) for k in range(ndiff)]
    out_specs += [pl.BlockSpec(p.shape, lambda i: (0, 0)) for p in pars]
    out_shape = [jax.ShapeDtypeStruct((n, rows[k][1]), dts[k]) for k in range(ndiff)]
    out_shape += [jax.ShapeDtypeStruct(p.shape, f32) for p in pars]
    res = pl.pallas_call(kern, name=name, grid=(n // tb,), in_specs=in_specs, out_specs=out_specs, out_shape=out_shape,
                         compiler_params=pltpu.CompilerParams(dimension_semantics=("arbitrary",)))(
        *[r[0] for r in rows], *pars, *cts, *[adds[k] for k in add_keys])
    return res[:ndiff], res[ndiff:]


_DOT_DIMS = {"nn": ((1,), (0,)), "nt": ((1,), (1,)), "tn": ((0,), (0,))}

MM_VMEM_BUDGET = 36 * 2 ** 20
MM_MAX_TM = 1408
MM_STEP, MM_RESULT, MM_XPOSE, MM_CAST = 700.0, 7.5e-4, 9e-4, 1e-3


def _tile_candidates(dim):
    c = [d for d in range(LANE, dim + 1, LANE) if dim % d == 0]
    return c or [dim]


@functools.lru_cache(maxsize=None)
def _mm_tiles(mode, m, n, k, a_bytes, b_bytes, o_bytes):
    best, best_cost = None, None
    for tm in _tile_candidates(m):
        if tm > MM_MAX_TM:
            continue
        for tn in _tile_candidates(n):
            for tk in _tile_candidates(k):
                vmem = 2 * (tm * tk * a_bytes + tk * tn * b_bytes + tm * tn * o_bytes) + 4 * tm * tn * (2 if tk < k else 1)
                vmem += (2 * tm * tk if a_bytes > 2 else 0) + (2 * tk * tn if b_bytes > 2 else 0)
                if vmem > MM_VMEM_BUDGET:
                    continue
                steps = (m // tm) * (n // tn) * (k // tk)
                cost = steps * MM_STEP + m * n * (k // tk) * MM_RESULT
                if mode == "tn":
                    cost += m * k * (n // tn) * MM_XPOSE
                cost += (m * k * (n // tn) * MM_CAST if a_bytes > 2 else 0) + (k * n * (m // tm) * MM_CAST if b_bytes > 2 else 0)
                if best is None or cost < best_cost:
                    best, best_cost = (tm, tn, tk), cost
    return best


def _mm(name, a, b, mode="nn", out_dtype=f32, res=None):
    if mode == "nn":
        (m, k), (_, n) = a.shape, b.shape
    elif mode == "nt":
        (m, k), (n, _) = a.shape, b.shape
    else:
        (k, m), (_, n) = a.shape, b.shape
    has_res = res is not None
    tm, tn, tk = _mm_tiles(mode, m, n, k, a.dtype.itemsize, b.dtype.itemsize,
                           jnp.dtype(out_dtype).itemsize + (res.dtype.itemsize if has_res else 0))
    nk = k // tk
    dims = (_DOT_DIMS[mode], ((), ()))

    def kern(*refs):
        a_ref, b_ref = refs[0], refs[1]
        o_ref, acc_ref = refs[-2], refs[-1]
        kk = pl.program_id(2)
        part = lax.dot_general(a_ref[...].astype(bf16), b_ref[...].astype(bf16), dims, preferred_element_type=f32)

        def finish(out):
            if has_res:
                out = out + refs[2][...]
            o_ref[...] = out.astype(o_ref.dtype)

        if nk == 1:
            finish(part)
            return

        @pl.when(kk == 0)
        def _():
            acc_ref[...] = part

        @pl.when(jnp.logical_and(kk > 0, kk < nk - 1))
        def _():
            acc_ref[...] += part

        @pl.when(kk == nk - 1)
        def _():
            finish(acc_ref[...] + part)

    if mode == "tn":
        a_spec = pl.BlockSpec((tk, tm), lambda i, j, kk: (kk, i))
    else:
        a_spec = pl.BlockSpec((tm, tk), lambda i, j, kk: (i, kk))
    if mode == "nt":
        b_spec = pl.BlockSpec((tn, tk), lambda i, j, kk: (j, kk))
    else:
        b_spec = pl.BlockSpec((tk, tn), lambda i, j, kk: (kk, j))
    in_specs = [a_spec, b_spec]
    args = [a, b]
    if has_res:
        in_specs.append(pl.BlockSpec((tm, tn), lambda i, j, kk: (i, j)))
        args.append(res)
    return pl.pallas_call(
        kern, name=name, grid=(m // tm, n // tn, nk), in_specs=in_specs,
        out_specs=pl.BlockSpec((tm, tn), lambda i, j, kk: (i, j)),
        out_shape=jax.ShapeDtypeStruct((m, n), out_dtype),
        scratch_shapes=[pltpu.VMEM((tm, tn) if nk > 1 else (SUBLANE, LANE), f32)],
        compiler_params=pltpu.CompilerParams(dimension_semantics=("parallel", "parallel", "arbitrary")))(*args)


ATT_TQ, ATT_TK = 512, 512


def _att_tiles(s_len):
    tk = min(ATT_TK, s_len)
    return min(ATT_TQ, tk), tk


def _fold_scale(scale):
    return (scale, 1.0) if math.frexp(scale)[0] == 0.5 else (1.0, scale)


def _as_row(col):
    return jnp.max(jnp.broadcast_to(col, (col.shape[0], LANE)).T[:SUBLANE], axis=0, keepdims=True)


def _scores_t(kb, q_t, s_mul, ck, diag_offset, tq, tk):
    s = jnp.dot(kb, q_t, preferred_element_type=f32)
    if s_mul != 1.0:
        s = s * s_mul
    if ck is not None:
        s = s - ck
    if diag_offset is None:
        return s
    key = lax.broadcasted_iota(jnp.int32, (tk, tq), 0)
    query = lax.broadcasted_iota(jnp.int32, (tk, tq), 1) + diag_offset
    return jnp.where(key <= query, s, -jnp.inf)


ATT_ROWS = 64


def _finish_scores(s, s_mul, ck, first_row):
    if s_mul != 1.0:
        s = s * s_mul
    if ck is not None:
        s = s - ck
    if first_row is None:
        return s
    row = lax.broadcasted_iota(jnp.int32, s.shape, 0) + first_row
    col = lax.broadcasted_iota(jnp.int32, s.shape, 1)
    return jnp.where(col <= row, s, -jnp.inf)


def _attn_fwd(name, q, k, v, scale, c_row=None):
    (qa, qo), (ka, ko), (va, vo) = q, k, v
    s_len = qa.shape[0]
    t = _att_tiles(s_len)[1]
    nt = s_len // t
    decay = c_row is not None
    q_mul, s_mul = _fold_scale(scale)

    def kern(*refs):
        q_ref, k_ref, v_ref = refs[:3]
        o_ref, lse_ref, lse_row_ref = refs[-3:]
        i = pl.program_id(1)
        qb = (q_ref[...] * q_mul).astype(bf16)

        def step(j, carry, diagonal):
            m, l, acc = carry
            rows = pl.ds(pl.multiple_of(j * t, t), t)
            kb = k_ref[rows, :].astype(bf16)
            vb = v_ref[rows, :].astype(bf16)
            s = lax.dot_general(qb, kb, (_DOT_DIMS["nt"], ((), ())), preferred_element_type=f32)
            s = _finish_scores(s, s_mul, refs[3][j] if decay else None, 0 if diagonal else None)
            m_new = jnp.maximum(m, jnp.max(s, axis=1, keepdims=True))
            alpha = jnp.exp(m - m_new)
            p = jnp.exp(s - m_new)
            l = alpha * l + jnp.sum(p, axis=1, keepdims=True)
            acc = alpha * acc + jnp.dot(p.astype(bf16), vb, preferred_element_type=f32)
            return m_new, l, acc

        init = (jnp.full((t, 1), -jnp.inf, f32), jnp.zeros((t, 1), f32), jnp.zeros((t, LANE), f32))
        m, l, acc = step(i, lax.fori_loop(0, i, lambda j, c: step(j, c, False), init), True)
        o_ref[...] = acc / l
        lse = m + jnp.log(l)
        lse_ref[...] = lse
        lse_row_ref[...] = _as_row(lse)

    in_specs = [pl.BlockSpec((t, LANE), lambda h, i: (i, qo + h)),
                pl.BlockSpec((s_len, LANE), lambda h, i: (0, ko + h)),
                pl.BlockSpec((s_len, LANE), lambda h, i: (0, vo + h))]
    args = [qa, ka, va]
    if decay:
        in_specs.append(pl.BlockSpec((None, nt, 1, t), lambda h, i: (h, 0, 0, 0)))
        args.append(c_row)
    return pl.pallas_call(
        kern, name=name, grid=(HEADS, nt), in_specs=in_specs,
        out_specs=[pl.BlockSpec((t, LANE), lambda h, i: (i, h)), pl.BlockSpec((None, t, 1), lambda h, i: (h, i, 0)),
                   pl.BlockSpec((None, None, 1, t), lambda h, i: (h, i, 0, 0))],
        out_shape=[jax.ShapeDtypeStruct((s_len, HEADS * LANE), f32), jax.ShapeDtypeStruct((HEADS, s_len, 1), f32),
                   jax.ShapeDtypeStruct((HEADS, nt, 1, t), f32)],
        compiler_params=pltpu.CompilerParams(dimension_semantics=("parallel", "arbitrary")))(*args)


def _attn_dq(name, q, k, v, o, do, lse, scale, c_row=None):
    (qa, qo), (ka, ko), (va, vo) = q, k, v
    s_len = qa.shape[0]
    t = _att_tiles(s_len)[1]
    nt = s_len // t
    decay = c_row is not None
    q_mul, s_mul = _fold_scale(scale)

    rp = min(ATT_ROWS, t)

    def kern(*refs):
        q_ref, k_ref, v_ref, o_ref, do_ref, lse_ref = refs[:6]
        dq_ref, delta_row_ref, drow_ref, delta_ref, s_ref, dp_ref, ds_ref = refs[-7:]
        i = pl.program_id(1)
        qb = (q_ref[...] * q_mul).astype(bf16)
        dob = do_ref[...]
        delta = jnp.sum(dob * o_ref[...], axis=1, keepdims=True)
        delta_ref[...] = delta
        delta_row_ref[...] = _as_row(delta)
        dob = dob.astype(bf16)
        drow_ref[...] = jnp.zeros((t, 1), f32)
        dq_ref[...] = jnp.zeros((t, LANE), f32)

        def step(j, diagonal):
            rows = pl.ds(pl.multiple_of(j * t, t), t)
            kb = k_ref[rows, :].astype(bf16)
            s_ref[...] = lax.dot_general(qb, kb, (_DOT_DIMS["nt"], ((), ())), preferred_element_type=f32)
            dp_ref[...] = lax.dot_general(dob, v_ref[rows, :].astype(bf16), (_DOT_DIMS["nt"], ((), ())),
                                          preferred_element_type=f32)
            ck = refs[6][j] if decay else None

            def rows_of(c, carry):
                r = slice(c * rp, (c + 1) * rp)
                s = _finish_scores(s_ref[r, :], s_mul, ck, c * rp if diagonal else None)
                ds = jnp.exp(s - lse_ref[r, :]) * (dp_ref[r, :] - delta_ref[r, :])
                drow_ref[r, :] += jnp.sum(ds, axis=1, keepdims=True)
                ds_ref[r, :] = ds.astype(bf16)
                return carry

            for c in range(t // rp):
                rows_of(c, 0)
            dq_ref[...] += jnp.dot(ds_ref[...], kb, preferred_element_type=f32)

        def below(j, carry):
            step(j, False)
            return carry

        lax.fori_loop(0, i, below, 0)
        step(i, True)
        dq_ref[...] = dq_ref[...] * scale

    in_specs = [pl.BlockSpec((t, LANE), lambda h, i: (i, qo + h)),
                pl.BlockSpec((s_len, LANE), lambda h, i: (0, ko + h)),
                pl.BlockSpec((s_len, LANE), lambda h, i: (0, vo + h)),
                pl.BlockSpec((t, LANE), lambda h, i: (i, h)),
                pl.BlockSpec((t, LANE), lambda h, i: (i, h)),
                pl.BlockSpec((None, t, 1), lambda h, i: (h, i, 0))]
    args = [qa, ka, va, o, do, lse]
    if decay:
        in_specs.append(pl.BlockSpec((None, nt, 1, t), lambda h, i: (h, 0, 0, 0)))
        args.append(c_row)
    col = pl.BlockSpec((None, t, 1), lambda h, i: (h, i, 0))
    return pl.pallas_call(
        kern, name=name, grid=(HEADS, nt), in_specs=in_specs,
        out_specs=[pl.BlockSpec((t, LANE), lambda h, i: (i, h)), pl.BlockSpec((None, None, 1, t), lambda h, i: (h, i, 0, 0)), col],
        out_shape=[jax.ShapeDtypeStruct((s_len, HEADS * LANE), f32), jax.ShapeDtypeStruct((HEADS, nt, 1, t), f32),
                   jax.ShapeDtypeStruct((HEADS, s_len, 1), f32)],
        scratch_shapes=[pltpu.VMEM((t, 1), f32), pltpu.VMEM((t, t), f32), pltpu.VMEM((t, t), f32), pltpu.VMEM((t, t), bf16)],
        compiler_params=pltpu.CompilerParams(dimension_semantics=("parallel", "arbitrary")))(*args)


def _attn_dkv(name, q, k, v, do, lse, delta, scale, c_col=None):
    (qa, qo), (ka, ko), (va, vo) = q, k, v
    s_len = qa.shape[0]
    tq, tk = _att_tiles(s_len)
    assert lse.shape == (HEADS, s_len // tq, 1, tq), (lse.shape, tq)
    nq, per = s_len // tq, tk // tq
    decay = c_col is not None
    q_mul, s_mul = _fold_scale(scale)

    def kern(*refs):
        q_ref, k_ref, v_ref, do_ref, lse_ref, delta_ref = refs[:6]
        j = pl.program_id(1)
        kb = k_ref[...].astype(bf16)
        vb = v_ref[...].astype(bf16)
        ck = refs[6][...] if decay else None

        def step(i, carry, diagonal):
            dk, dv, dsum = carry
            for d in range(per):
                tile = i * per + d
                rows = pl.ds(pl.multiple_of(tile * tq, tq), tq)
                qb = (q_ref[rows, :] * q_mul).astype(bf16)
                dob = do_ref[rows, :].astype(bf16)
                s = _scores_t(kb, qb.T, s_mul, ck, d * tq if diagonal else None, tq, tk)
                p = jnp.exp(s - lse_ref[tile])
                dv = dv + jnp.dot(p.astype(bf16), dob, preferred_element_type=f32)
                dp = jnp.dot(vb, dob.T, preferred_element_type=f32)
                ds = p * (dp - delta_ref[tile])
                dk = dk + jnp.dot(ds.astype(bf16), qb, preferred_element_type=f32)
                if decay:
                    dsum = dsum + ds
            return dk, dv, dsum

        init = (jnp.zeros((tk, LANE), f32), jnp.zeros((tk, LANE), f32), jnp.zeros((tk, tq), f32))
        dk, dv, dsum = lax.fori_loop(j + 1, s_len // tk, lambda i, c: step(i, c, False), step(j, init, True))
        if decay:
            dk_ref, dv_ref, dc_ref = refs[-3:]
            dc_ref[...] = -jnp.sum(dsum, axis=1, keepdims=True)
        else:
            dk_ref, dv_ref = refs[-2:]
        dk_ref[...] = dk * s_mul
        dv_ref[...] = dv

    stat = pl.BlockSpec((None, nq, 1, tq), lambda h, j: (h, 0, 0, 0))
    in_specs = [pl.BlockSpec((s_len, LANE), lambda h, j: (0, qo + h)),
                pl.BlockSpec((tk, LANE), lambda h, j: (j, ko + h)),
                pl.BlockSpec((tk, LANE), lambda h, j: (j, vo + h)),
                pl.BlockSpec((s_len, LANE), lambda h, j: (0, h)), stat, stat]
    args = [qa, ka, va, do, lse, delta]
    out_specs = [pl.BlockSpec((tk, LANE), lambda h, j: (j, h)), pl.BlockSpec((tk, LANE), lambda h, j: (j, h))]
    out_shape = [jax.ShapeDtypeStruct((s_len, HEADS * LANE), f32), jax.ShapeDtypeStruct((s_len, HEADS * LANE), f32)]
    if decay:
        in_specs.append(pl.BlockSpec((None, tk, 1), lambda h, j: (h, j, 0)))
        args.append(c_col)
        out_specs.append(pl.BlockSpec((None, tk, 1), lambda h, j: (h, j, 0)))
        out_shape.append(jax.ShapeDtypeStruct((HEADS, s_len, 1), f32))
    return pl.pallas_call(
        kern, name=name, grid=(HEADS, s_len // tk), in_specs=in_specs, out_specs=out_specs, out_shape=out_shape,
        compiler_params=pltpu.CompilerParams(dimension_semantics=("parallel", "arbitrary")))(*args)


CONV_TS, CONV_CB = 4096, 256
FFN_ROWS = 64


def _conv_fwd(name, x, w, b, taps):
    xa, width, xidx = _view(x)
    s_len = xa.shape[0]
    ts, cb = min(CONV_TS, s_len), CONV_CB
    xo = xidx * width // cb

    def kern(x_ref, halo_ref, w_ref, b_ref, o_ref):
        i = pl.program_id(1)
        xb = x_ref[...]
        halo = jnp.where(i == 0, 0.0, halo_ref[...])
        xx = jnp.concatenate([halo, xb], axis=0)
        out = b_ref[...] + w_ref[taps - 1:taps, :] * xb
        for k in range(taps - 1):
            out = out + w_ref[k:k + 1, :] * pltpu.roll(xx, taps - 1 - k, 0)[SUBLANE:]
        o_ref[...] = out

    return pl.pallas_call(
        kern, name=name, grid=(width // cb, s_len // ts),
        in_specs=[pl.BlockSpec((ts, cb), lambda j, i: (i, xo + j)),
                  pl.BlockSpec((SUBLANE, cb), lambda j, i: (jnp.maximum(i * (ts // SUBLANE) - 1, 0), xo + j)),
                  pl.BlockSpec((taps, cb), lambda j, i: (0, j)),
                  pl.BlockSpec((1, cb), lambda j, i: (0, j))],
        out_specs=pl.BlockSpec((ts, cb), lambda j, i: (i, j)),
        out_shape=jax.ShapeDtypeStruct((s_len, width), f32),
        compiler_params=pltpu.CompilerParams(dimension_semantics=("parallel", "parallel")))(xa, xa, w, b)


def _conv_bwd(name, x, dout, w, taps, dout2=None, dx_dtype=f32):
    xa, width, xidx = _view(x)
    s_len = xa.shape[0]
    ts, cb = min(CONV_TS, s_len), CONV_CB
    xo = xidx * width // cb
    n_i = s_len // ts
    two = dout2 is not None

    def kern(*refs):
        x_ref, halo_ref, w_ref = refs[:3]
        dx_ref, dw_ref, db_ref = refs[-3:]
        i = pl.program_id(1)
        if two:
            d = refs[3][...] + refs[5][...]
            dn = refs[4][...] + refs[6][...]
        else:
            d, dn = refs[3][...], refs[4][...]
        dn = jnp.where(i == n_i - 1, 0.0, dn)
        xb = x_ref[...]
        halo = jnp.where(i == 0, 0.0, halo_ref[...])
        xx = jnp.concatenate([halo, xb], axis=0)
        dd = jnp.concatenate([d, dn], axis=0)

        @pl.when(i == 0)
        def _():
            dw_ref[...] = jnp.zeros_like(dw_ref)
            db_ref[...] = jnp.zeros_like(db_ref)

        dx = w_ref[taps - 1:taps, :] * d
        dw_ref[taps - 1:taps, :] += jnp.sum(d * xb, axis=0, keepdims=True)
        for k in range(taps - 1):
            sh = taps - 1 - k
            dx = dx + w_ref[k:k + 1, :] * pltpu.roll(dd, ts + SUBLANE - sh, 0)[:ts]
            dw_ref[k:k + 1, :] += jnp.sum(d * pltpu.roll(xx, sh, 0)[SUBLANE:], axis=0, keepdims=True)
        dx_ref[...] = dx.astype(dx_ref.dtype)
        db_ref[...] += jnp.sum(d, axis=0, keepdims=True)

    d_spec = pl.BlockSpec((ts, cb), lambda j, i: (i, j))
    dn_spec = pl.BlockSpec((SUBLANE, cb), lambda j, i: (jnp.minimum((i + 1) * (ts // SUBLANE), s_len // SUBLANE - 1), j))
    in_specs = [pl.BlockSpec((ts, cb), lambda j, i: (i, xo + j)),
                pl.BlockSpec((SUBLANE, cb), lambda j, i: (jnp.maximum(i * (ts // SUBLANE) - 1, 0), xo + j)),
                pl.BlockSpec((taps, cb), lambda j, i: (0, j)), d_spec, dn_spec]
    args = [xa, xa, w, dout, dout]
    if two:
        in_specs += [d_spec, dn_spec]
        args += [dout2, dout2]
    return pl.pallas_call(
        kern, name=name, grid=(width // cb, n_i), in_specs=in_specs,
        out_specs=[pl.BlockSpec((ts, cb), lambda j, i: (i, j)), pl.BlockSpec((taps, cb), lambda j, i: (0, j)),
                   pl.BlockSpec((1, cb), lambda j, i: (0, j))],
        out_shape=[jax.ShapeDtypeStruct((s_len, width), dx_dtype), jax.ShapeDtypeStruct((taps, width), f32),
                   jax.ShapeDtypeStruct((1, width), f32)],
        compiler_params=pltpu.CompilerParams(dimension_semantics=("parallel", "arbitrary")))(*args)


def _conv_rows(xx, w_ref, b_ref, taps):
    out = b_ref[...] + w_ref[taps - 1:taps, :] * xx[SUBLANE:]
    for k in range(taps - 1):
        out = out + w_ref[k:k + 1, :] * pltpu.roll(xx, taps - 1 - k, 0)[SUBLANE:]
    return out


def _ffn_act_fwd(name, up, w, b):
    s_len = up.shape[0]
    ts, cb = min(CONV_TS, s_len), CONV_CB
    nf = D_FF // cb

    def kern(g_ref, gp_ref, v_ref, vp_ref, wg_ref, wv_ref, bg_ref, bv_ref, o_ref):
        first = pl.program_id(1) == 0
        ug = _conv_rows(jnp.concatenate([jnp.where(first, 0.0, gp_ref[...]), g_ref[...]], axis=0), wg_ref, bg_ref, FFN_CONV)
        uv = _conv_rows(jnp.concatenate([jnp.where(first, 0.0, vp_ref[...]), v_ref[...]], axis=0), wv_ref, bv_ref, FFN_CONV)
        o_ref[...] = (jax.nn.silu(ug) * uv).astype(o_ref.dtype)

    def half(off):
        return [pl.BlockSpec((ts, cb), lambda j, i: (i, off + j)),
                pl.BlockSpec((SUBLANE, cb), lambda j, i: (jnp.maximum(i * (ts // SUBLANE) - 1, 0), off + j))]

    def par(rows, off):
        return pl.BlockSpec((rows, cb), lambda j, i: (0, off + j))

    return pl.pallas_call(
        kern, name=name, grid=(nf, s_len // ts),
        in_specs=half(0) + half(nf) + [par(FFN_CONV, 0), par(FFN_CONV, nf), par(1, 0), par(1, nf)],
        out_specs=pl.BlockSpec((ts, cb), lambda j, i: (i, j)),
        out_shape=jax.ShapeDtypeStruct((s_len, D_FF), bf16),
        compiler_params=pltpu.CompilerParams(dimension_semantics=("parallel", "parallel")))(up, up, up, up, w, w, b, b)


def _ffn_act_bwd(name, up, dact, w, b):
    s_len = up.shape[0]
    ts, cb = min(CONV_TS, s_len), CONV_CB
    nf = D_FF // cb
    n_i = s_len // ts
    taps = FFN_CONV

    ch = min(FFN_ROWS, ts)

    def kern(g_ref, gp_ref, gn_ref, v_ref, vp_ref, vn_ref, d_ref, dn_ref, wg_ref, wv_ref, bg_ref, bv_ref,
             dg_ref, dv_ref, dwg_ref, dwv_ref, dbg_ref, dbv_ref, gx_ref, vx_ref, dd_ref):
        i = pl.program_id(1)
        first, last = i == 0, i == n_i - 1
        for x_ref, p_ref, n_ref, ext in ((g_ref, gp_ref, gn_ref, gx_ref), (v_ref, vp_ref, vn_ref, vx_ref)):
            ext[:SUBLANE, :] = jnp.where(first, 0.0, p_ref[...])
            ext[SUBLANE:SUBLANE + ts, :] = x_ref[...]
            ext[SUBLANE + ts:, :] = jnp.where(last, 0.0, n_ref[...])
        dd_ref[:ts, :] = d_ref[...]
        dd_ref[ts:, :] = jnp.where(last, 0.0, dn_ref[...])

        @pl.when(first)
        def _():
            for ref in (dwg_ref, dwv_ref, dbg_ref, dbv_ref):
                ref[...] = jnp.zeros_like(ref)

        def rows_of(c, carry):
            r0 = pl.multiple_of(c * ch, ch)
            gx, vx = gx_ref[pl.ds(r0, ch + 2 * SUBLANE), :], vx_ref[pl.ds(r0, ch + 2 * SUBLANE), :]
            ug, uv = _conv_rows(gx, wg_ref, bg_ref, taps), _conv_rows(vx, wv_ref, bv_ref, taps)
            dd = dd_ref[pl.ds(r0, ch + SUBLANE), :]
            sg = jax.nn.sigmoid(ug)
            out = []
            for du, xx, w_ref, dx_ref, sums in ((dd * uv * (sg * (1.0 + ug * (1.0 - sg))), gx, wg_ref, dg_ref, carry[0]),
                                                (dd * (ug * sg), vx, wv_ref, dv_ref, carry[1])):
                d = du[:ch]
                dx = w_ref[taps - 1:taps, :] * d
                new = [None] * (taps + 1)
                new[taps - 1] = sums[taps - 1] + jnp.sum(d * xx[SUBLANE:SUBLANE + ch], axis=0, keepdims=True)
                for k in range(taps - 1):
                    sh = taps - 1 - k
                    dx = dx + w_ref[k:k + 1, :] * pltpu.roll(du, ch + SUBLANE - sh, 0)[:ch]
                    new[k] = sums[k] + jnp.sum(d * pltpu.roll(xx, sh, 0)[SUBLANE:SUBLANE + ch], axis=0, keepdims=True)
                new[taps] = sums[taps] + jnp.sum(d, axis=0, keepdims=True)
                dx_ref[pl.ds(r0, ch), :] = dx.astype(dx_ref.dtype)
                out.append(tuple(new))
            return tuple(out)

        zero = tuple(jnp.zeros((1, cb), f32) for _ in range(taps + 1))
        sums_g, sums_v = lax.fori_loop(0, ts // ch, rows_of, (zero, zero))
        for sums, dw_ref, db_ref in ((sums_g, dwg_ref, dbg_ref), (sums_v, dwv_ref, dbv_ref)):
            for k in range(taps):
                dw_ref[k:k + 1, :] += sums[k]
            db_ref[...] += sums[taps]

    blocks = s_len // SUBLANE

    def half(off):
        return [pl.BlockSpec((ts, cb), lambda j, i: (i, off + j)),
                pl.BlockSpec((SUBLANE, cb), lambda j, i: (jnp.maximum(i * (ts // SUBLANE) - 1, 0), off + j)),
                pl.BlockSpec((SUBLANE, cb), lambda j, i: (jnp.minimum((i + 1) * (ts // SUBLANE), blocks - 1), off + j))]

    def par(rows, off):
        return pl.BlockSpec((rows, cb), lambda j, i: (0, off + j))

    d_specs = [pl.BlockSpec((ts, cb), lambda j, i: (i, j)),
               pl.BlockSpec((SUBLANE, cb), lambda j, i: (jnp.minimum((i + 1) * (ts // SUBLANE), blocks - 1), j))]
    out_par = [pl.BlockSpec((r, cb), lambda j, i: (0, j)) for r in (taps, taps, 1, 1)]
    return pl.pallas_call(
        kern, name=name, grid=(nf, n_i),
        in_specs=half(0) + half(nf) + d_specs + [par(taps, 0), par(taps, nf), par(1, 0), par(1, nf)],
        out_specs=[pl.BlockSpec((ts, cb), lambda j, i: (i, j))] * 2 + out_par,
        out_shape=[jax.ShapeDtypeStruct((s_len, D_FF), bf16)] * 2 + [jax.ShapeDtypeStruct((taps, D_FF), f32)] * 2
        + [jax.ShapeDtypeStruct((1, D_FF), f32)] * 2,
        scratch_shapes=[pltpu.VMEM((ts + 2 * SUBLANE, cb), f32)] * 2 + [pltpu.VMEM((ts + SUBLANE, cb), f32)],
        compiler_params=pltpu.CompilerParams(dimension_semantics=("parallel", "arbitrary")))(
        up, up, up, up, up, up, dact, dact, w, w, b, b)


SCAN_ROWS = 128


def _block_scan(a, b, reverse):
    t = a.shape[0]
    row = lax.broadcasted_iota(jnp.int32, a.shape, 0)
    d = 1
    while d < t:
        keep = row < t - d if reverse else row >= d
        shift = t - d if reverse else d
        a_far = jnp.where(keep, pltpu.roll(a, shift, 0), 1.0)
        b_far = jnp.where(keep, pltpu.roll(b, shift, 0), 0.0)
        b = a * b_far + b
        a = a * a_far
        d *= 2
    return a, b


def _scan_fwd(name, a, b):
    s_len, width = a.shape
    t = min(SCAN_ROWS, s_len)

    def kern(a_ref, b_ref, h_ref):
        def block(k, carry):
            rows = pl.ds(pl.multiple_of(k * t, t), t)
            acc, h = _block_scan(a_ref[rows, :], b_ref[rows, :], False)
            h_ref[rows, :] = h + acc * carry
            return h_ref[pl.ds(k * t + t - 1, 1), :]

        lax.fori_loop(0, s_len // t, block, jnp.zeros((1, LANE), f32))

    spec = pl.BlockSpec((s_len, LANE), lambda j: (0, j))
    return pl.pallas_call(
        kern, name=name, grid=(width // LANE,), in_specs=[spec, spec], out_specs=spec,
        out_shape=jax.ShapeDtypeStruct((s_len, width), f32),
        compiler_params=pltpu.CompilerParams(dimension_semantics=("parallel",)))(a, b)


def _scan_bwd(name, a_next, h_prev, dh):
    s_len, width = dh.shape
    t = min(SCAN_ROWS, s_len)
    n_blocks = s_len // t

    def kern(an_ref, hp_ref, dh_ref, da_ref, db_ref):
        def block(kk, carry):
            k = n_blocks - 1 - kk
            rows = pl.ds(pl.multiple_of(k * t, t), t)
            acc, g = _block_scan(an_ref[rows, :], dh_ref[rows, :], True)
            g = g + acc * carry
            db_ref[rows, :] = g
            da_ref[rows, :] = g * hp_ref[rows, :]
            return db_ref[pl.ds(k * t, 1), :]

        lax.fori_loop(0, n_blocks, block, jnp.zeros((1, LANE), f32))

    spec = pl.BlockSpec((s_len, LANE), lambda j: (0, j))
    return pl.pallas_call(
        kern, name=name, grid=(width // LANE,), in_specs=[spec, spec, spec], out_specs=[spec, spec],
        out_shape=[jax.ShapeDtypeStruct((s_len, width), f32)] * 2,
        compiler_params=pltpu.CompilerParams(dimension_semantics=("parallel",)))(a_next, h_prev, dh)


def _lane_cumsum(x, reverse):
    n = x.shape[1]
    lane = lax.broadcasted_iota(jnp.int32, x.shape, 1)
    sh = 1
    while sh < n:
        if reverse:
            x = x + jnp.where(lane < n - sh, pltpu.roll(x, n - sh, 1), 0.0)
        else:
            x = x + jnp.where(lane >= sh, pltpu.roll(x, sh, 1), 0.0)
        sh *= 2
    return x


def _decay_fwd(name, fl_t, b8):
    def kern(f_ref, b_ref, c_ref):
        c_ref[...] = _lane_cumsum(jax.nn.log_sigmoid(f_ref[...] + b_ref[...]), False)

    return pl.pallas_call(kern, name=name, out_shape=jax.ShapeDtypeStruct(fl_t.shape, f32))(fl_t, b8)


def _decay_bwd(name, fl_t, b8, dc_key, dc_query):
    def kern(f_ref, b_ref, dck_ref, dcq_ref, df_ref, db_ref):
        dlogf = _lane_cumsum(dck_ref[...] + dcq_ref[...], True)
        df = dlogf * jax.nn.sigmoid(-(f_ref[...] + b_ref[...]))
        df_ref[...] = df
        db_ref[...] = jnp.sum(df, axis=1, keepdims=True)

    return pl.pallas_call(kern, name=name, out_shape=[jax.ShapeDtypeStruct(fl_t.shape, f32),
                                                      jax.ShapeDtypeStruct((SUBLANE, 1), f32)])(fl_t, b8, dc_key, dc_query)


def _rms(x, g, n):
    return x * lax.rsqrt(jnp.sum(x * x, axis=-1, keepdims=True) * (1.0 / n) + EPS) * g


def _loss_head(name, h, target, g, tb=512):
    n, d = h.shape
    tb = min(tb, n)

    def kern(h_ref, t_ref, g_ref, loss_ref, dh_ref, dg_ref):
        i = pl.program_id(0)
        tgt = t_ref[...]

        def f(hv, gv):
            err = _rms(hv, gv, d) - tgt
            return 0.5 * jnp.sum(jnp.sum(err * err, axis=-1, keepdims=True) * (1.0 / d), axis=0, keepdims=True)

        val, vjp = jax.vjp(f, h_ref[...], g_ref[...])
        dh, dg = vjp(jnp.ones((1, 1), f32))
        dh_ref[...] = dh

        @pl.when(i == 0)
        def _():
            loss_ref[...] = jnp.zeros_like(loss_ref)
            dg_ref[...] = jnp.zeros_like(dg_ref)

        loss_ref[...] += val
        dg_ref[...] += dg

    return pl.pallas_call(
        kern, name=name, grid=(n // tb,),
        in_specs=[pl.BlockSpec((tb, d), lambda i: (i, 0)), pl.BlockSpec((tb, d), lambda i: (i, 0)),
                  pl.BlockSpec((1, d), lambda i: (0, 0))],
        out_specs=[pl.BlockSpec((1, 1), lambda i: (0, 0)), pl.BlockSpec((tb, d), lambda i: (i, 0)),
                   pl.BlockSpec((1, d), lambda i: (0, 0))],
        out_shape=[jax.ShapeDtypeStruct((1, 1), f32), jax.ShapeDtypeStruct((n, d), f32), jax.ShapeDtypeStruct((1, d), f32)],
        compiler_params=pltpu.CompilerParams(dimension_semantics=("arbitrary",)))(h, target, g)


def _f_norm(x, g):
    return (_rms(x, g, D_MODEL),)


def _f_latent(qc, kvc, gq, gkv):
    return _rms(qc, gq, MLA_Q_RANK), _rms(kvc, gkv, MLA_KV_RANK)


def _f_rope_table(pos, freq, m1, m2):
    ang = pos * freq
    sin = jnp.sin(ang)
    return jnp.cos(ang), -sin * m1, sin * m2


def _rope(x, cos, s_up, s_down):
    w = x.shape[1]
    return x * cos + _roll(x, w - MLA_ROPE // 2, 1) * s_up + _roll(x, MLA_ROPE // 2, 1) * s_down


def _f_mla_prep(q, kpart, kr, cos, s_up, s_down):
    def heads(t):
        return jnp.concatenate([t] * HEADS, axis=1)

    kr = _rope(kr, cos, s_up, s_down)
    return _rope(q, heads(cos), heads(s_up), heads(s_down)), kpart + heads(kr)


def _f_lru_gate(gates, xc, b_r, b_i, lam):
    r = jax.nn.sigmoid(gates[:, :LRU_WIDTH] + b_r)
    i = jax.nn.sigmoid(gates[:, LRU_WIDTH:] + b_i)
    log_a = -LRU_C * r * jax.nn.softplus(-lam)
    mult = jnp.sqrt(-jnp.tanh(log_a) * (1.0 + jnp.exp(2.0 * log_a)))
    return jnp.exp(log_a), mult * (i * xc)


def _f_merge(o_mla, o_fox, hs, lg, g):
    o_lru = hs * jax.nn.gelu(lg)
    return (jnp.concatenate([_rms(o_mla, g[:, :512], HEADS * MLA_V), _rms(o_fox, g[:, 512:1024], HEADS * FOX_HEAD_DIM),
                             _rms(o_lru, g[:, 1024:], LRU_WIDTH)], axis=1),)


def _f_ffn_gate(u):
    return (jax.nn.silu(u[:, :D_FF]) * u[:, D_FF:],)


def _f_ple(h, gpre, pp):
    return (h + jax.nn.sigmoid(gpre) * pp,)


MIX_PART = ["w_in", "w_uq", "w_ukv", "lru_conv_w"]
FFN_PART = ["w_o", "w_up", "ffn_conv_w", "w_down", "w_ple_gate", "w_ple_proj"]


def _prep_mix_weights(w):
    eye = jnp.eye(LRU_BLOCKS, dtype=f32)

    def block_diag(m):
        return (eye[:, None, :, None] * m[:, :, None, :]).reshape(LRU_WIDTH, LRU_WIDTH)

    return dict(
        w_in=_take_pad(w["w_in"], Z_MAP, 1),
        w_uq=_take_pad(_take_pad(w["w_uq"], UQ_COL_MAP, 1), UQ_ROW_MAP, 0),
        w_ukv=_take_pad(w["w_ukv"], UKV_MAP, 1),
        w_ri=jnp.concatenate([block_diag(w["w_r"]), block_diag(w["w_i"])], axis=1).astype(bf16),
        g_mix=w["g_mix"].reshape(1, -1), g_ffn=w["g_ffn"].reshape(1, -1), g_ple=w["g_ple"].reshape(1, -1),
        g_qc=_take_pad(w["g_qc"], UQ_ROW_MAP, 0).reshape(1, -1), g_kvc=w["g_kvc"].reshape(1, -1),
        g_out=_take_pad(w["g_out"], OMIX_MAP, 0).reshape(1, -1),
        b_f8=_take_pad(w["b_f"], _pad_to(np.arange(FOX_HEADS), SUBLANE), 0).reshape(SUBLANE, 1),
        lru_conv_w=w["lru_conv_w"], lru_conv_b=w["lru_conv_b"].reshape(1, -1),
        b_r=w["b_r"].reshape(1, -1), b_i=w["b_i"].reshape(1, -1), lam=w["lru_lambda"].reshape(1, -1),
        ffn_conv_b=w["ffn_conv_b"].reshape(1, -1),
    )


def _prep_ffn_weights(w):
    return dict(w_o=_take_pad(w["w_o"], OMIX_MAP, 0),
                w_up=w["w_up"], w_up_g=w["w_up"][:, :D_FF], w_up_v=w["w_up"][:, D_FF:], ffn_conv_w=w["ffn_conv_w"],
                w_down=w["w_down"], w_ple_gate=w["w_ple_gate"], w_ple_proj=w["w_ple_proj"])


def _rope_rows(pos):
    consts = [jnp.asarray(t) for t in _rope_tables(LANE, ROPE_AT)]
    return _rowwise("rope_table", _f_rope_table, [pos], consts, [(LANE, f32)] * 3)


def _key_decay(c_t, s_len):
    t = _att_tiles(s_len)[1]
    return c_t[:HEADS].reshape(HEADS, s_len // t, 1, t), c_t[:HEADS].reshape(HEADS, s_len, 1)


def _layer_fwd(l, h0, p_l, rope, weights_of):
    s_len = h0.shape[0]
    n = f"l{l}_"
    w = _prep_mix_weights(weights_of("mix", h0))
    xn, = _rowwise(n + "norm_mix", _f_norm, [h0], [w["g_mix"]], [(D_MODEL, bf16)])
    z = _mm(n + "in_proj", xn, w["w_in"])
    zq = (z, QC_W, Z_QC // QC_W)
    zkv = (z, LANE, Z_KVC // LANE)
    zkr = (z, LANE, Z_KR // LANE)
    zlx = (z, LRU_WIDTH, Z_LX // LRU_WIDTH)
    zlg = (z, LRU_WIDTH, Z_LG // LRU_WIDTH)
    qcn, kvn = _rowwise(n + "latent_norm", _f_latent, [zq, zkv], [w["g_qc"], w["g_kvc"]], [(QC_W, bf16), (LANE, bf16)])
    q = _mm(n + "uq", qcn, w["w_uq"])
    kv = _mm(n + "ukv", kvn, w["w_ukv"])
    kpart = (kv, HEADS * LANE, 0)
    qr, kk = _rowwise(n + "mla_prep", _f_mla_prep, [q, kpart, zkr, *rope], [],
                      [(HEADS * LANE, bf16), (HEADS * LANE, bf16)])
    mla_scale = (MLA_NOPE + MLA_ROPE) ** -0.5
    o_mla, lse_m, lse_m_row = _attn_fwd(n + "mla_fwd", (qr, 0), (kk, 0), (kv, HEADS), mla_scale)
    fl_t = z[:, Z_FL:Z_FL + SUBLANE].T
    c_t = _decay_fwd(n + "decay", fl_t, w["b_f8"])
    c_row, c_col = _key_decay(c_t, s_len)
    fox_scale = FOX_HEAD_DIM ** -0.5
    o_fox, lse_f, lse_f_row = _attn_fwd(n + "fox_fwd", (z, Z_FQ // LANE), (z, Z_FK // LANE), (z, Z_FV // LANE), fox_scale, c_row)
    xc = _conv_fwd(n + "lru_conv", zlx, w["lru_conv_w"], w["lru_conv_b"], LRU_CONV)
    gates = _mm(n + "lru_gates", xc, w["w_ri"])
    a, bx = _rowwise(n + "lru_gate", _f_lru_gate, [gates, xc], [w["b_r"], w["b_i"], w["lam"]],
                     [(LRU_WIDTH, f32), (LRU_WIDTH, f32)])
    hs = _scan_fwd(n + "lru_scan", a, bx)
    ocat, = _rowwise(n + "merge", _f_merge, [o_mla, o_fox, hs, zlg], [w["g_out"]], [(OMIX_W, bf16)])
    w.update(_prep_ffn_weights(weights_of("ffn", ocat)))
    h1 = _mm(n + "out_proj", ocat, w["w_o"], res=h0)
    xn2, = _rowwise(n + "norm_ffn", _f_norm, [h1], [w["g_ffn"]], [(D_MODEL, bf16)])
    up = _mm(n + "up_proj", xn2, w["w_up"])
    act = _ffn_act_fwd(n + "ffn_act", up, w["ffn_conv_w"], w["ffn_conv_b"])
    h2 = _mm(n + "down_proj", act, w["w_down"], res=h1)
    hn, = _rowwise(n + "norm_ple", _f_norm, [h2], [w["g_ple"]], [(D_MODEL, bf16)])
    gpre = _mm(n + "ple_gate", hn, w["w_ple_gate"])
    pp = _mm(n + "ple_proj", p_l, w["w_ple_proj"])
    h3, = _rowwise(n + "ple_mix", _f_ple, [h2, gpre, pp], [], [(D_MODEL, f32)])
    res = dict(h0=h0, xn=xn, z=z, qcn=qcn, kvn=kvn, q=q, kv=kv, qr=qr, kk=kk, o_mla=o_mla, lse_m=lse_m, fl_t=fl_t,
               lse_m_row=lse_m_row, lse_f_row=lse_f_row, c_row=c_row, c_col=c_col, o_fox=o_fox, lse_f=lse_f, xc=xc, gates=gates, a=a, hs=hs, ocat=ocat, h1=h1,
               xn2=xn2, up=up, act=act, h2=h2, hn=hn, gpre=gpre, pp=pp, p_l=p_l)
    return h3, res, w


def _layer_bwd(l, dh3, r, rope, w, token, grads_to):
    s_len = dh3.shape[0]
    n = f"l{l}_"
    g = {}
    w = dict(w, g_ple=w["g_ple"] + token)
    z = r["z"]
    zq = (z, QC_W, Z_QC // QC_W)
    zkv = (z, LANE, Z_KVC // LANE)
    zkr = (z, LANE, Z_KR // LANE)
    zlx = (z, LRU_WIDTH, Z_LX // LRU_WIDTH)
    zlg = (z, LRU_WIDTH, Z_LG // LRU_WIDTH)
    (dh2a, dgpre, dpp), _ = _rowwise_bwd(n + "ple_mix_b", _f_ple, [r["h2"], r["gpre"], r["pp"]], [], [dh3], 3,
                                         dts=[f32, bf16, bf16])
    g["w_ple_proj"] = _mm(n + "ple_proj_dw", r["p_l"], dpp, "tn", bf16)
    dhn = _mm(n + "ple_gate_dx", dgpre, w["w_ple_gate"], "nt")
    g["w_ple_gate"] = _mm(n + "ple_gate_dw", r["hn"], dgpre, "tn", bf16)
    (dh2,), (g["g_ple"],) = _rowwise_bwd(n + "norm_ple_b", _f_norm, [r["h2"]], [w["g_ple"]], [dhn], 1, adds={0: dh2a})
    dact = _mm(n + "down_dx", dh2, w["w_down"], "nt")
    g["w_down"] = _mm(n + "down_dw", r["act"], dh2, "tn", bf16)
    dup_g, dup_v, dcw_g, dcw_v, dcb_g, dcb_v = _ffn_act_bwd(n + "ffn_act_b", r["up"], dact, w["ffn_conv_w"], w["ffn_conv_b"])
    g["ffn_conv_w"] = jnp.concatenate([dcw_g, dcw_v], axis=1)
    g["ffn_conv_b"] = jnp.concatenate([dcb_g, dcb_v], axis=1)
    dxn2 = _mm(n + "up_dx_v", dup_v, w["w_up_v"], "nt", res=_mm(n + "up_dx_g", dup_g, w["w_up_g"], "nt"))
    g["w_up"] = jnp.concatenate([_mm(n + "up_dw_g", r["xn2"], dup_g, "tn", bf16),
                                 _mm(n + "up_dw_v", r["xn2"], dup_v, "tn", bf16)], axis=1)
    (dh1,), (g["g_ffn"],) = _rowwise_bwd(n + "norm_ffn_b", _f_norm, [r["h1"]], [w["g_ffn"]], [dxn2], 1, adds={0: dh2})
    docat = _mm(n + "out_dx", dh1, w["w_o"], "nt")
    g["w_o"] = _mm(n + "out_dw", r["ocat"], dh1, "tn", bf16)
    token = grads_to("ffn", dict(w_o=_take_inv(g["w_o"], OMIX_MAP, 0), w_up=g["w_up"], ffn_conv_w=g["ffn_conv_w"],
                                 w_down=g["w_down"], w_ple_gate=g["w_ple_gate"], w_ple_proj=g["w_ple_proj"]))
    w = dict(w, g_out=w["g_out"] + token)
    (do_mla, do_fox, dhs, dlg), (g["g_out"],) = _rowwise_bwd(
        n + "merge_b", _f_merge, [r["o_mla"], r["o_fox"], r["hs"], zlg], [w["g_out"]], [docat], 4)
    a, hs = r["a"], r["hs"]
    a_next = jnp.concatenate([a[1:], jnp.zeros((1, LRU_WIDTH), f32)], axis=0)
    h_prev = jnp.concatenate([jnp.zeros((1, LRU_WIDTH), f32), hs[:-1]], axis=0)
    da, dbx = _scan_bwd(n + "lru_scan_b", a_next, h_prev, dhs)
    (dgates, dxc_a), (g["b_r"], g["b_i"], g["lam"]) = _rowwise_bwd(
        n + "lru_gate_b", _f_lru_gate, [r["gates"], r["xc"]], [w["b_r"], w["b_i"], w["lam"]], [da, dbx], 2,
        dts=[bf16, f32])
    dxc_b = _mm(n + "lru_gates_dx", dgates, w["w_ri"], "nt")
    g["w_ri"] = _mm(n + "lru_gates_dw", r["xc"], dgates, "tn")
    dlx, g["lru_conv_w"], g["lru_conv_b"] = _conv_bwd(n + "lru_conv_b", zlx, dxc_a, w["lru_conv_w"], LRU_CONV, dout2=dxc_b)
    fox_scale = FOX_HEAD_DIM ** -0.5
    fq, fk, fv = (z, Z_FQ // LANE), (z, Z_FK // LANE), (z, Z_FV // LANE)
    dfq, delta_f, dc_q = _attn_dq(n + "fox_dq", fq, fk, fv, r["o_fox"], do_fox, r["lse_f"], fox_scale, r["c_row"])
    dfk, dfv, dc_k = _attn_dkv(n + "fox_dkv", fq, fk, fv, do_fox, r["lse_f_row"], delta_f, fox_scale,
                               r["c_col"])
    pad_rows = jnp.zeros((SUBLANE - HEADS, s_len), f32)
    dfl_t, g["b_f8"] = _decay_bwd(n + "decay_b", r["fl_t"], w["b_f8"],
                                  jnp.concatenate([dc_k.reshape(HEADS, s_len), pad_rows], axis=0),
                                  jnp.concatenate([dc_q.reshape(HEADS, s_len), pad_rows], axis=0))
    dfl = jnp.pad(dfl_t.T, ((0, 0), (0, LANE - SUBLANE)))
    mla_scale = (MLA_NOPE + MLA_ROPE) ** -0.5
    qr, kk, kv = (r["qr"], 0), (r["kk"], 0), (r["kv"], HEADS)
    dqr, delta_m, _ = _attn_dq(n + "mla_dq", qr, kk, kv, r["o_mla"], do_mla, r["lse_m"], mla_scale)
    dkk, dv_m = _attn_dkv(n + "mla_dkv", qr, kk, kv, do_mla, r["lse_m_row"], delta_m, mla_scale)
    (dq, dkpart, dkr), _ = _rowwise_bwd(n + "mla_prep_b", _f_mla_prep, [r["q"], (r["kv"], HEADS * LANE, 0), zkr, *rope],
                                        [], [dqr, dkk], 3, dts=[bf16, bf16, f32])
    dkv = jnp.concatenate([dkpart, dv_m.astype(bf16)], axis=1)
    dkvn = _mm(n + "ukv_dx", dkv, w["w_ukv"], "nt")
    g["w_ukv"] = _mm(n + "ukv_dw", r["kvn"], dkv, "tn", bf16)
    dqcn = _mm(n + "uq_dx", dq, w["w_uq"], "nt")
    g["w_uq"] = _mm(n + "uq_dw", r["qcn"], dq, "tn", bf16)
    (dqc, dkvc), (g["g_qc"], g["g_kvc"]) = _rowwise_bwd(n + "latent_norm_b", _f_latent, [zq, zkv],
                                                        [w["g_qc"], w["g_kvc"]], [dqcn, dkvn], 2)
    dz = jnp.concatenate([t.astype(bf16) for t in (dfq, dfk, dfv, dlx, dlg, dqc, dkvc, dkr, dfl)], axis=1)
    dxn = _mm(n + "in_dx", dz, w["w_in"], "nt")
    g["w_in"] = _mm(n + "in_dw", r["xn"], dz, "tn", bf16)
    (dh0,), (g["g_mix"],) = _rowwise_bwd(n + "norm_mix_b", _f_norm, [r["h0"]], [w["g_mix"]], [dxn], 1, adds={0: dh1})
    return dh0, grads_to("mix", _unpad_mix_grads(g))


def _unpad_mix_grads(g):
    d_ri = g["w_ri"]
    idx = jnp.arange(LRU_BLOCKS)

    def diag_blocks(m):
        return m.reshape(LRU_BLOCKS, LRU_BLOCK, LRU_BLOCKS, LRU_BLOCK)[idx, :, idx, :]

    return dict(
        g_mix=g["g_mix"][0], w_in=_take_inv(g["w_in"], Z_MAP, 1), g_qc=g["g_qc"][0, :MLA_Q_RANK],
        w_uq=_take_inv(g["w_uq"][:MLA_Q_RANK], UQ_COL_MAP, 1), g_kvc=g["g_kvc"][0],
        w_ukv=_take_inv(g["w_ukv"], UKV_MAP, 1), b_f=g["b_f8"][:FOX_HEADS, 0],
        lru_conv_w=g["lru_conv_w"], lru_conv_b=g["lru_conv_b"][0],
        w_r=diag_blocks(d_ri[:, :LRU_WIDTH]), b_r=g["b_r"][0], w_i=diag_blocks(d_ri[:, LRU_WIDTH:]), b_i=g["b_i"][0],
        lru_lambda=g["lam"][0], g_out=_take_inv(g["g_out"][0], OMIX_MAP, 0),
        g_ffn=g["g_ffn"][0], ffn_conv_b=g["ffn_conv_b"][0], g_ple=g["g_ple"][0],
    )


LAYER_WEIGHTS = ["g_mix", "w_in", "g_qc", "w_uq", "g_kvc", "w_ukv", "b_f", "lru_conv_w", "lru_conv_b", "w_r", "b_r", "w_i",
                 "b_i", "lru_lambda", "g_out", "w_o", "g_ffn", "w_up", "ffn_conv_w", "ffn_conv_b", "w_down", "g_ple",
                 "w_ple_gate", "w_ple_proj"]
WEIGHTS = LAYER_WEIGHTS + ["g_final"]


def _local_step(x, p, pos, target, g_final, weights_of, grads_to):
    h = x
    rope = _rope_rows(pos)
    ws, saved = [], []
    for l in range(DEPTH):
        h, r, w = _layer_fwd(l, h, p[l], rope, functools.partial(weights_of, l))
        ws.append(w)
        saved.append(r)
    loss, dh, dg_final = _loss_head("loss_head", h, target, g_final.reshape(1, -1))
    token = jnp.zeros((), f32)
    for l in reversed(range(DEPTH)):
        dh, token = _layer_bwd(l, dh, saved[l], rope, ws[l], token, functools.partial(grads_to, l))
    return loss[0, 0], dh, dg_final[0]


MESH_AXES = ("x", "y", "c")


def _row_tile(rows, cap):
    if rows <= cap:
        return rows
    for t in range(cap, SUBLANE - 1, -SUBLANE):
        if rows % t == 0:
            return t
    return rows


ADAM_BLOCK_BYTES = 2 ** 21


def _adamw(name, w, g, m, v):
    rows, cols = w.shape
    tr = _row_tile(rows, max(SUBLANE, ADAM_BLOCK_BYTES // (4 * cols) // SUBLANE * SUBLANE))

    def kern(w_ref, g_ref, m_ref, v_ref, d_ref, nm_ref, nv_ref):
        gv = g_ref[...]
        nm = ADAM_B1 * m_ref[...] + (1.0 - ADAM_B1) * gv
        nv = ADAM_B2 * v_ref[...] + (1.0 - ADAM_B2) * (gv * gv)
        m_hat = nm / (1.0 - ADAM_B1 ** ADAM_STEP)
        v_hat = nv / (1.0 - ADAM_B2 ** ADAM_STEP)
        d_ref[...] = -ADAM_LR * (m_hat / (jnp.sqrt(v_hat) + ADAM_EPS) + ADAM_WD * w_ref[...])
        nm_ref[...] = nm
        nv_ref[...] = nv

    spec = pl.BlockSpec((tr, cols), lambda i: (i, 0))
    return pl.pallas_call(
        kern, name=name, grid=(rows // tr,), in_specs=[spec] * 4, out_specs=[spec] * 3,
        out_shape=[jax.ShapeDtypeStruct((rows, cols), f32)] * 3,
        compiler_params=pltpu.CompilerParams(dimension_semantics=("parallel",)))(w, g, m, v)


def _packed_rows(shape):
    return -(-int(np.prod(shape)) // (SUBLANE * LANE)) * SUBLANE


def _pack(arrays):
    rows = []
    for a in arrays:
        flat = a.reshape(-1)
        rows.append(jnp.pad(flat, (0, _packed_rows(a.shape) * LANE - flat.shape[0])).reshape(-1, LANE))
    return jnp.concatenate(rows, axis=0)


def _unpack(buf, shapes):
    out, at = [], 0
    for s in shapes:
        rows = _packed_rows(s)
        out.append(buf[at:at + rows].reshape(-1)[:int(np.prod(s))].reshape(s))
        at += rows
    return out


SHARD_AXIS = {"w_in": 2, "w_uq": 2, "w_ukv": 2, "lru_conv_w": 2, "w_o": 1, "w_up": 2, "ffn_conv_w": 2, "w_down": 1,
              "w_ple_gate": 1, "w_ple_proj": 2}
SHARDED = [k for k in WEIGHTS if k in SHARD_AXIS]
REPLICATED = [k for k in WEIGHTS if k not in SHARD_AXIS]
ELEMENTWISE_F32 = ("lru_conv_w", "ffn_conv_w")
N_SHARDS = 4
BF16_TILE_ROWS = 16


HBM_SPEC = pl.BlockSpec(memory_space=pl.ANY)
SEM_SPEC = pl.BlockSpec(memory_space=pltpu.SEMAPHORE)
SPLIT_EFFECT = pltpu.SideEffectType.DATAFLOW_SIDE_EFFECTING
CHIP_FLIPS = ((1, 0), (0, 1), (1, 1))
N_DEVICES = 8
SUM_BLOCK_BYTES = 4 * 2 ** 20


def _device_index():
    return 4 * lax.axis_index("x") + 2 * lax.axis_index("y") + lax.axis_index("c")


def _when(cond, fn):
    if cond is None:
        fn()
    else:
        pl.when(cond)(fn)


class _Exchange:
    def __init__(self, name, plan, srcs, land_shapes, n_send, n_recv):
        self.name, self.plan, self.srcs, self.n = name, plan, list(srcs), len(srcs)
        self.land_shapes, self.n_send, self.n_recv = land_shapes, n_send, n_recv

    def run(self):
        n = self.n

        def body(*refs):
            sends, arrivals = self.plan(refs[:n], refs[n:2 * n], refs[2 * n], refs[2 * n + 1])
            for cond, cp in sends:
                _when(cond, cp.start)
            for cond, cp in arrivals:
                _when(cond, cp.wait_recv)
            for cond, cp in sends:
                _when(cond, cp.wait_send)

        return pl.pallas_call(
            body, name=self.name, out_shape=self.land_shapes, in_specs=[HBM_SPEC] * n, out_specs=[HBM_SPEC] * n,
            scratch_shapes=[pltpu.SemaphoreType.DMA((self.n_send,)), pltpu.SemaphoreType.DMA((self.n_recv,))])(*self.srcs)

    def start(self, after=None):
        n = self.n
        lands = [lax.empty(s.shape, s.dtype) for s in self.land_shapes]
        extra = [] if after is None else [after]

        def body(*refs):
            ins, lands_in = refs[:n], refs[n:2 * n]
            send_sems, recv_sems, token = refs[2 * n + len(extra)], refs[2 * n + len(extra) + 1], refs[-1]
            sends, _ = self.plan(ins, lands_in, send_sems, recv_sems)
            for cond, cp in sends:
                _when(cond, cp.start)
            token[...] = jnp.zeros_like(token)

        hbm = [pltpu.with_memory_space_constraint(a, pltpu.HBM) for a in self.srcs + lands]
        res = pl.pallas_call(
            body, name=self.name + "_start",
            out_shape=(pltpu.SemaphoreType.DMA((self.n_send,)), pltpu.SemaphoreType.DMA((self.n_recv,)),
                       *[pltpu.HBM(a.shape, a.dtype) for a in hbm], jax.ShapeDtypeStruct((SUBLANE, LANE), f32)),
            in_specs=[HBM_SPEC] * (2 * n + len(extra)),
            out_specs=(SEM_SPEC, SEM_SPEC, *[HBM_SPEC] * (2 * n), pl.BlockSpec(memory_space=pltpu.VMEM)),
            input_output_aliases={i: 2 + i for i in range(2 * n)},
            compiler_params=pltpu.CompilerParams(has_side_effects=SPLIT_EFFECT))(*hbm, *extra)
        self.sems, self.thru, token = res[:2], res[2:2 + 2 * n], res[-1]
        return token[0, 0]

    def finish(self, after):
        n = self.n

        def body(*refs):
            ins, lands_in, send_sems, recv_sems = refs[:n], refs[n:2 * n], refs[2 * n], refs[2 * n + 1]
            sends, arrivals = self.plan(ins, lands_in, send_sems, recv_sems)
            for cond, cp in arrivals:
                _when(cond, cp.wait_recv)
            for cond, cp in sends:
                _when(cond, cp.wait_send)

        res = pl.pallas_call(
            body, name=self.name + "_finish", out_shape=tuple(pltpu.HBM(a.shape, a.dtype) for a in self.thru),
            in_specs=[HBM_SPEC] * (2 * n) + [SEM_SPEC, SEM_SPEC, HBM_SPEC], out_specs=tuple([HBM_SPEC] * (2 * n)),
            input_output_aliases={i: i for i in range(2 * n)},
            compiler_params=pltpu.CompilerParams(has_side_effects=SPLIT_EFFECT))(*self.thru, *self.sems, after)
        return list(res[n:])


def _gather_exchange(name, shards):
    def plan(ins, lands, send_sems, recv_sems):
        x, y, c = (lax.axis_index(a) for a in MESH_AXES)
        copies = []
        for i in range(len(ins)):
            for k, (fx, fy) in enumerate(CHIP_FLIPS):
                peer = (1 - x if fx else x, 1 - y if fy else y, c)
                copies.append((None, pltpu.make_async_remote_copy(
                    src_ref=ins[i], dst_ref=lands[i].at[2 * x + y], send_sem=send_sems.at[3 * i + k],
                    recv_sem=recv_sems.at[3 * i + k], device_id=peer, device_id_type=pl.DeviceIdType.MESH)))
        return copies, copies

    n = len(shards)
    return _Exchange(name, plan, shards, [jax.ShapeDtypeStruct((N_SHARDS,) + s.shape, s.dtype) for s in shards], 3 * n, 3 * n)


def _scatter_exchange(name, layer, chunks):
    def plan(ins, lands, send_sems, recv_sems):
        x, y, c = (lax.axis_index(a) for a in MESH_AXES)
        me = _device_index()
        sends, arrivals = [], []
        for i in range(len(ins)):
            for j in range(N_SHARDS):
                target = (j // 2, j % 2, layer)
                remote = jnp.logical_not((x == target[0]) & (y == target[1]) & (c == layer))
                sends.append((remote, pltpu.make_async_remote_copy(
                    src_ref=ins[i].at[j], dst_ref=lands[i].at[me], send_sem=send_sems.at[N_SHARDS * i + j],
                    recv_sem=recv_sems.at[N_DEVICES * i + me], device_id=target, device_id_type=pl.DeviceIdType.MESH)))
            for s in range(N_DEVICES):
                arrivals.append(((c == layer) & (me != s), pltpu.make_async_remote_copy(
                    src_ref=ins[i].at[0], dst_ref=lands[i].at[s], send_sem=send_sems.at[0],
                    recv_sem=recv_sems.at[N_DEVICES * i + s], device_id=(x, y, c), device_id_type=pl.DeviceIdType.MESH)))
        return sends, arrivals

    n = len(chunks)
    lands = [jax.ShapeDtypeStruct((N_DEVICES,) + ch.shape[1:], ch.dtype) for ch in chunks]
    return _Exchange(name, plan, chunks, lands, N_SHARDS * n, N_DEVICES * n)


def _sum_contributions(name, got, mine):
    _, a, b = got.shape
    ta = _row_tile(a, max(SUBLANE, SUM_BLOCK_BYTES // (N_DEVICES * b * got.dtype.itemsize) // SUBLANE * SUBLANE))

    def kern(got_ref, mine_ref, o_ref):
        me = _device_index()
        acc = jnp.zeros(o_ref.shape, f32)
        for s in range(N_DEVICES):
            acc = acc + jnp.where(me == s, mine_ref[...].astype(f32), got_ref[s].astype(f32))
        o_ref[...] = acc

    return pl.pallas_call(
        kern, name=name, grid=(a // ta,),
        in_specs=[pl.BlockSpec((N_DEVICES, ta, b), lambda i: (0, i, 0)), pl.BlockSpec((ta, b), lambda i: (i, 0))],
        out_specs=pl.BlockSpec((ta, b), lambda i: (i, 0)), out_shape=jax.ShapeDtypeStruct((a, b), f32),
        compiler_params=pltpu.CompilerParams(dimension_semantics=("parallel",)))(got, mine)


def _swap_layers(name, sums):
    n = len(sums[0])

    def body(*refs):
        srcs = (refs[:n], refs[n:2 * n])
        outs, (send_sems, recv_sems) = refs[2 * n:3 * n], refs[3 * n:]
        x, y, c = (lax.axis_index(a) for a in MESH_AXES)
        for i in range(n):
            for layer in range(DEPTH):
                cp = pltpu.make_async_remote_copy(
                    src_ref=srcs[layer][i], dst_ref=outs[i], send_sem=send_sems.at[i], recv_sem=recv_sems.at[i],
                    device_id=(x, y, 1 - c), device_id_type=pl.DeviceIdType.MESH)
                pl.when(c == layer)(cp.start)
        for i in range(n):
            pltpu.make_async_remote_copy(
                src_ref=srcs[0][i], dst_ref=outs[i], send_sem=send_sems.at[i], recv_sem=recv_sems.at[i],
                device_id=(x, y, 1 - c), device_id_type=pl.DeviceIdType.MESH).wait()

    return pl.pallas_call(
        body, name=name, out_shape=[jax.ShapeDtypeStruct(s.shape, s.dtype) for s in sums[0]],
        in_specs=[HBM_SPEC] * (2 * n), out_specs=[HBM_SPEC] * n,
        scratch_shapes=[pltpu.SemaphoreType.DMA((n,)), pltpu.SemaphoreType.DMA((n,))])(*sums[0], *sums[1])


def _stack_shards(g, axis):
    if axis == 1:
        return g.reshape(N_SHARDS, g.shape[0] // N_SHARDS, g.shape[1])
    return g.reshape(g.shape[0], N_SHARDS, g.shape[1] // N_SHARDS).transpose(1, 0, 2)


def _join_shards(s, axis):
    if axis == 1:
        return s.reshape(-1, s.shape[2])
    return s.transpose(1, 0, 2).reshape(s.shape[1], -1)


def _layer_shards(w, l, names):
    return [w[k][l] if k in ELEMENTWISE_F32 else w[k][l].astype(bf16) for k in names]


def _full_weights(names, sent, got):
    j = 2 * lax.axis_index("x") + lax.axis_index("y")
    return {k: _join_shards(lax.dynamic_update_slice(g, own[None], (j, 0, 0)), SHARD_AXIS[k])
            for k, own, g in zip(names, sent, got)}


def _grad_chunks(grads, names):
    return [_stack_shards(grads[k], SHARD_AXIS[k]).astype(bf16) for k in names]


def _sum_group(l, names, got, chunks):
    j = 2 * lax.axis_index("x") + lax.axis_index("y")
    return {k: _sum_contributions(f"sum_l{l}_{k}", g, lax.dynamic_index_in_dim(ch, j, 0, keepdims=False))
            for k, g, ch in zip(names, got, chunks)}


def _both_layers(name, names, sums):
    c = lax.axis_index("c")
    mine = [[sums[l][k] for k in names] for l in range(DEPTH)]
    other = _swap_layers(name, mine)
    return {k: jnp.stack([jnp.where(c == 0, mine[0][i], other[i]), jnp.where(c == 0, other[i], mine[1][i])])
            for i, k in enumerate(names)}


def _gather_all_exchange(name, src):
    def plan(ins, lands, send_sems, recv_sems):
        coords = [lax.axis_index(a) for a in MESH_AXES]
        me = _device_index()
        sends, arrivals = [], []
        for f in range(1, N_DEVICES):
            peer = tuple(1 - cd if (f >> (2 - b)) & 1 else cd for b, cd in enumerate(coords))
            sends.append((None, pltpu.make_async_remote_copy(
                src_ref=ins[0], dst_ref=lands[0].at[me], send_sem=send_sems.at[f - 1], recv_sem=recv_sems.at[me],
                device_id=peer, device_id_type=pl.DeviceIdType.MESH)))
        for s in range(N_DEVICES):
            arrivals.append((me != s, pltpu.make_async_remote_copy(
                src_ref=ins[0], dst_ref=lands[0].at[s], send_sem=send_sems.at[0], recv_sem=recv_sems.at[s],
                device_id=tuple(coords), device_id_type=pl.DeviceIdType.MESH)))
        return sends, arrivals

    return _Exchange(name, plan, [src], [jax.ShapeDtypeStruct((N_DEVICES,) + src.shape, src.dtype)], N_DEVICES - 1, N_DEVICES)


def kernel(x, p, positions, g_mix, w_in, g_qc, w_uq, g_kvc, w_ukv, b_f, lru_conv_w, lru_conv_b, w_r, b_r, w_i, b_i, lru_lambda, g_out, w_o, g_ffn, w_up, ffn_conv_w, ffn_conv_b, w_down, g_ple, w_ple_gate, w_ple_proj, g_final, loss_target, m_g_mix, m_w_in, m_g_qc, m_w_uq, m_g_kvc, m_w_ukv, m_b_f, m_lru_conv_w, m_lru_conv_b, m_w_r, m_b_r, m_w_i, m_b_i, m_lru_lambda, m_g_out, m_w_o, m_g_ffn, m_w_up, m_ffn_conv_w, m_ffn_conv_b, m_w_down, m_g_ple, m_w_ple_gate, m_w_ple_proj, m_g_final, v_g_mix, v_w_in, v_g_qc, v_w_uq, v_g_kvc, v_w_ukv, v_b_f, v_lru_conv_w, v_lru_conv_b, v_w_r, v_b_r, v_w_i, v_b_i, v_lru_lambda, v_g_out, v_w_o, v_g_ffn, v_w_up, v_ffn_conv_w, v_ffn_conv_b, v_w_down, v_g_ple, v_w_ple_gate, v_w_ple_proj, v_g_final):
    given = locals()
    w = {k: given[k] for k in WEIGHTS}
    m = {k: given["m_" + k] for k in WEIGHTS}
    v = {k: given["v_" + k] for k in WEIGHTS}

    parts = {"mix": MIX_PART, "ffn": FFN_PART}
    groups = [(l, part) for l in range(DEPTH) for part in ("mix", "ffn")]
    sent = {g: _layer_shards(w, g[0], parts[g[1]]) for g in groups}
    first = _gather_exchange("gather_l0_mix", sent[groups[0]]).run()
    ahead = {g: _gather_exchange(f"gather_l{g[0]}_{g[1]}", sent[g]) for g in groups[1:]}
    pos = positions[0].astype(f32).reshape(-1, 1) + ahead[groups[1]].start(after=first[0])
    behind, layer_grads, chunks = {}, [{} for _ in range(DEPTH)], {}

    def weights_of(l, part, after):
        g = (l, part)
        got = first if g == groups[0] else ahead[g].finish(after=after)
        full = _full_weights(parts[part], sent[g], got)
        if part == "mix":
            full.update({k: w[k][l] for k in LAYER_WEIGHTS if k in REPLICATED})
        if g == groups[1]:
            for later in groups[2:]:
                full["ffn_conv_w"] = full["ffn_conv_w"] + ahead[later].start(after=got[0])
        return full

    def grads_to(l, part, grads):
        g = (l, part)
        layer_grads[l].update(grads)
        chunks[g] = _grad_chunks(grads, parts[part])
        if g == groups[0]:
            return jnp.zeros((), f32)
        behind[g] = _scatter_exchange(f"scatter_l{l}_{part}", l, chunks[g])
        return behind[g].start()

    loss, dx, dg_final = _local_step(x[0], p[:, 0], pos, loss_target[0], w["g_final"], weights_of, grads_to)

    grads = {k: jnp.stack([layer_grads[l][k] for l in range(DEPTH)]) for k in LAYER_WEIGHTS if k in REPLICATED}
    grads["g_final"] = dg_final
    rep_shapes = [w[k].shape for k in REPLICATED] + [(1,)]
    contrib = _pack([grads[k] for k in REPLICATED] + [loss.reshape(1)])
    last = _scatter_exchange("scatter_l0_mix", 0, chunks[groups[0]])
    everyone = _gather_all_exchange("gather_replicated", contrib)
    started = (last.start() + everyone.start() + dx[0, 0]).reshape(1, 1)

    def adamw_of(names, g_sharded):
        out = {}
        for k in names:
            shape = w[k].shape
            flat = [t.reshape(-1, shape[-1]) for t in (w[k], g_sharded[k], m[k], v[k])]
            out[k] = [t.reshape(shape) for t in (flat[1],) + tuple(_adamw("adamw_" + k, *flat))]
        return out

    sums = [{} for _ in range(DEPTH)]
    for g in groups[1:]:
        sums[g[0]].update(_sum_group(g[0], parts[g[1]], behind[g].finish(after=started), chunks[g]))
    big = adamw_of(FFN_PART, _both_layers("swap_ffn", FFN_PART, sums))
    sums[0].update(_sum_group(0, MIX_PART, last.finish(after=big[FFN_PART[0]][1]), chunks[groups[0]]))
    big.update(adamw_of(MIX_PART, _both_layers("swap_mix", MIX_PART, sums)))

    g_rep = _sum_contributions("sum_replicated", everyone.finish(after=big[MIX_PART[0]][1])[0], contrib)
    zero = jnp.zeros((1,), f32)
    w_rep, m_rep, v_rep = (_pack([t[k] for k in REPLICATED] + [zero]) for t in (w, m, v))
    rep = [_unpack(b, rep_shapes) for b in (g_rep,) + tuple(_adamw("adamw_replicated", w_rep, g_rep, m_rep, v_rep))]

    outs = []
    for kind in range(4):
        by_name = {k: big[k][kind] for k in SHARDED}
        by_name.update(zip(REPLICATED, rep[kind][:-1]))
        outs.append([by_name[k] for k in WEIGHTS])
    total_loss = rep[0][-1][0]
    return (total_loss, dx.reshape(x.shape), *outs[0], *outs[1], *outs[2], *outs[3])
```
